```python
import jax, jax.numpy as jnp
from jax import lax
import numpy as np

D_MODEL = 1024
BATCH = 8
SEQ = 4096
DEPTH = 1

MIX_WIDTH = D_MODEL
POOL_WIDTH = MIX_WIDTH // 2
POOL_WINDOWS = (2, 4, 8, 16)
N_POOL_GROUPS = len(POOL_WINDOWS)
POOL_GROUP_DIM = POOL_WIDTH // N_POOL_GROUPS
RET_WIDTH = MIX_WIDTH - POOL_WIDTH
RET_HEADS = 4
RET_HEAD_DIM = RET_WIDTH // RET_HEADS
RET_CHUNK = 128
ROPE_BASE = 10000.0
IN_WIDTH = POOL_WIDTH + 4 * RET_WIDTH
D_FF = 2816
EPS = 1e-6

kernel_name = "macaron_pool_retention_hybrid"


def rmsnorm(x, gain):
    xf = x.astype(jnp.float32)
    y = xf * lax.rsqrt(jnp.mean(xf * xf, axis=-1, keepdims=True) + EPS)
    return (y * gain.astype(jnp.float32)).astype(x.dtype)


def swiglu(x, w1, w3, w2):
    return (jax.nn.silu(x @ w1) * (x @ w3)) @ w2


def pool_mixer(u, pool_w, pool_scale):
    B, S, _ = u.shape
    ug = u.reshape(B, S, N_POOL_GROUPS, POOL_GROUP_DIM)
    t = jnp.arange(1, S + 1, dtype=jnp.float32)
    outs = []
    for gi, w in enumerate(POOL_WINDOWS):
        xg = ug[:, :, gi, :]
        cs = lax.cumsum(xg.astype(jnp.float32), axis=1)
        csp = jnp.pad(cs, ((0, 0), (w, 0), (0, 0)))
        window_sum = csp[:, w:, :] - csp[:, :S, :]
        count = jnp.minimum(t, float(w))[None, :, None]
        pooled = (window_sum / count).astype(u.dtype) - xg
        outs.append(pooled @ pool_w[gi])
    return jnp.concatenate(outs, axis=-1) * pool_scale


def rope_tables(S, D):
    inv_freq = 1.0 / (ROPE_BASE ** (jnp.arange(0, D, 2, dtype=jnp.float32) / D))
    ang = jnp.arange(S, dtype=jnp.float32)[:, None] * inv_freq[None, :]
    return jnp.cos(ang), jnp.sin(ang)


def apply_rope(t, cos, sin):
    t1, t2 = jnp.split(t, 2, axis=-1)
    c = cos[None, :, None, :]
    s = sin[None, :, None, :]
    return jnp.concatenate([t1 * c - t2 * s, t1 * s + t2 * c], axis=-1)


def retention(q, k, v, g, gain):
    B, S, _ = q.shape
    H, D, C = RET_HEADS, RET_HEAD_DIM, RET_CHUNK
    NC = S // C
    dt = q.dtype

    def heads(t):
        return t.astype(jnp.float32).reshape(B, S, H, D)

    cos, sin = rope_tables(S, D)
    qh = apply_rope(heads(q), cos, sin)
    kh = apply_rope(heads(k), cos, sin) * (D ** -0.5)
    vh = heads(v)

    def chunks(t):
        return t.reshape(B, NC, C, H, D).transpose(0, 3, 1, 2, 4)

    qc, kc, vc = chunks(qh), chunks(kh), chunks(vh)

    log_gamma = jnp.log1p(-jnp.exp2(-5.0 - jnp.arange(H, dtype=jnp.float32)))
    pos = jnp.arange(C, dtype=jnp.float32)
    rel = pos[:, None] - pos[None, :]
    intra_decay = jnp.where(rel[None] >= 0,
                            jnp.exp(log_gamma[:, None, None] * jnp.maximum(rel, 0.0)[None]),
                            0.0)

    scores = jnp.einsum('bhncd,bhnsd->bhncs', qc, kc) * intra_decay[None, :, None]
    o_intra = jnp.einsum('bhncs,bhnse->bhnce', scores, vc)

    k_tail = jnp.exp(log_gamma[:, None] * (C - 1 - pos)[None, :])
    kv = jnp.einsum('bhnsd,bhnse->nbhde', kc * k_tail[None, :, None, :, None], vc)
    chunk_decay = jnp.exp(log_gamma * C)[None, :, None, None]

    def step(R, kv_n):
        return chunk_decay * R + kv_n, R

    _, R_prev = lax.scan(step, jnp.zeros((B, H, D, D), jnp.float32), kv)

    q_head = jnp.exp(log_gamma[:, None] * (pos + 1.0)[None, :])
    o_cross = jnp.einsum('bhncd,nbhde->bhnce', qc * q_head[None, :, None, :, None], R_prev)

    o = (o_intra + o_cross).transpose(0, 2, 3, 1, 4).reshape(B, S, H, D)
    o = o * lax.rsqrt(jnp.mean(o * o, axis=-1, keepdims=True) + EPS)
    o = o.reshape(B, S, H * D) * gain.astype(jnp.float32)
    return (jax.nn.silu(g.astype(jnp.float32)) * o).astype(dt)


def _fwd_setup_inputs(seed: int = 0) -> dict:
    key = jax.random.key(seed)
    ks = jax.random.split(key, 20)
    f32 = jnp.float32

    def nrm(k, shape, fan_in):
        return jax.random.normal(k, shape, f32) * (fan_in ** -0.5)

    def gain(k, shape):
        return 1.0 + 0.05 * jax.random.normal(k, shape, f32)

    L = DEPTH
    return {
        "x": jax.random.normal(ks[0], (BATCH, SEQ, D_MODEL), f32),
        "ffn1_norm": gain(ks[1], (L, D_MODEL)),
        "ffn1_w1": nrm(ks[2], (L, D_MODEL, D_FF), D_MODEL),
        "ffn1_w3": nrm(ks[3], (L, D_MODEL, D_FF), D_MODEL),
        "ffn1_w2": nrm(ks[4], (L, D_FF, D_MODEL), D_FF),
        "mix_norm": gain(ks[5], (L, D_MODEL)),
        "w_in": nrm(ks[6], (L, D_MODEL, IN_WIDTH), D_MODEL),
        "pool_w": nrm(ks[7], (L, N_POOL_GROUPS, POOL_GROUP_DIM, POOL_GROUP_DIM), POOL_GROUP_DIM),
        "pool_scale": gain(ks[8], (L, POOL_WIDTH)),
        "ret_norm": gain(ks[9], (L, RET_WIDTH)),
        "w_out": nrm(ks[10], (L, MIX_WIDTH, D_MODEL), MIX_WIDTH),
        "ffn2_norm": gain(ks[11], (L, D_MODEL)),
        "ffn2_w1": nrm(ks[12], (L, D_MODEL, D_FF), D_MODEL),
        "ffn2_w3": nrm(ks[13], (L, D_MODEL, D_FF), D_MODEL),
        "ffn2_w2": nrm(ks[14], (L, D_FF, D_MODEL), D_FF),
        "final_norm": gain(ks[15], (D_MODEL,)),
    }


def _fwd_reference(x, ffn1_norm, ffn1_w1, ffn1_w3, ffn1_w2, mix_norm, w_in, pool_w, pool_scale,
              ret_norm, w_out, ffn2_norm, ffn2_w1, ffn2_w3, ffn2_w2, final_norm):
    h = x
    for l in range(DEPTH):
        h = h + 0.5 * swiglu(rmsnorm(h, ffn1_norm[l]), ffn1_w1[l], ffn1_w3[l], ffn1_w2[l])

        u = rmsnorm(h, mix_norm[l])
        proj = u @ w_in[l]
        u_pool, q, k, v, g = jnp.split(
            proj, [POOL_WIDTH, POOL_WIDTH + RET_WIDTH, POOL_WIDTH + 2 * RET_WIDTH,
                   POOL_WIDTH + 3 * RET_WIDTH], axis=-1)
        a = pool_mixer(u_pool, pool_w[l], pool_scale[l])
        b = retention(q, k, v, g, ret_norm[l])
        h = h + jnp.concatenate([a, b], axis=-1) @ w_out[l]

        h = h + 0.5 * swiglu(rmsnorm(h, ffn2_norm[l]), ffn2_w1[l], ffn2_w3[l], ffn2_w2[l])
    return rmsnorm(h, final_norm)


import jax as _jax
import jax.numpy as _jnp

TWIN_FORMAT = 'train_step'
FWD_PARAMS = ['x', 'ffn1_norm', 'ffn1_w1', 'ffn1_w3', 'ffn1_w2', 'mix_norm', 'w_in', 'pool_w', 'pool_scale', 'ret_norm', 'w_out', 'ffn2_norm', 'ffn2_w1', 'ffn2_w3', 'ffn2_w2', 'final_norm']
TWIN_WEIGHTS = ['ffn1_norm', 'ffn1_w1', 'ffn1_w3', 'ffn1_w2', 'mix_norm', 'w_in', 'pool_w', 'pool_scale', 'ret_norm', 'w_out', 'ffn2_norm', 'ffn2_w1', 'ffn2_w3', 'ffn2_w2', 'final_norm']
TWIN_DIFF_INPUT = 'x'
TWIN_INPUTS = ['x', 'ffn1_norm', 'ffn1_w1', 'ffn1_w3', 'ffn1_w2', 'mix_norm', 'w_in', 'pool_w', 'pool_scale', 'ret_norm', 'w_out', 'ffn2_norm', 'ffn2_w1', 'ffn2_w3', 'ffn2_w2', 'final_norm', 'loss_target', 'm_ffn1_norm', 'm_ffn1_w1', 'm_ffn1_w3', 'm_ffn1_w2', 'm_mix_norm', 'm_w_in', 'm_pool_w', 'm_pool_scale', 'm_ret_norm', 'm_w_out', 'm_ffn2_norm', 'm_ffn2_w1', 'm_ffn2_w3', 'm_ffn2_w2', 'm_final_norm', 'v_ffn1_norm', 'v_ffn1_w1', 'v_ffn1_w3', 'v_ffn1_w2', 'v_mix_norm', 'v_w_in', 'v_pool_w', 'v_pool_scale', 'v_ret_norm', 'v_w_out', 'v_ffn2_norm', 'v_ffn2_w1', 'v_ffn2_w3', 'v_ffn2_w2', 'v_final_norm']
TWIN_OUTPUTS = ['loss', 'grad_x', 'grad_ffn1_norm', 'grad_ffn1_w1', 'grad_ffn1_w3', 'grad_ffn1_w2', 'grad_mix_norm', 'grad_w_in', 'grad_pool_w', 'grad_pool_scale', 'grad_ret_norm', 'grad_w_out', 'grad_ffn2_norm', 'grad_ffn2_w1', 'grad_ffn2_w3', 'grad_ffn2_w2', 'grad_final_norm', 'delta_ffn1_norm', 'delta_ffn1_w1', 'delta_ffn1_w3', 'delta_ffn1_w2', 'delta_mix_norm', 'delta_w_in', 'delta_pool_w', 'delta_pool_scale', 'delta_ret_norm', 'delta_w_out', 'delta_ffn2_norm', 'delta_ffn2_w1', 'delta_ffn2_w3', 'delta_ffn2_w2', 'delta_final_norm', 'new_m_ffn1_norm', 'new_m_ffn1_w1', 'new_m_ffn1_w3', 'new_m_ffn1_w2', 'new_m_mix_norm', 'new_m_w_in', 'new_m_pool_w', 'new_m_pool_scale', 'new_m_ret_norm', 'new_m_w_out', 'new_m_ffn2_norm', 'new_m_ffn2_w1', 'new_m_ffn2_w3', 'new_m_ffn2_w2', 'new_m_final_norm', 'new_v_ffn1_norm', 'new_v_ffn1_w1', 'new_v_ffn1_w3', 'new_v_ffn1_w2', 'new_v_mix_norm', 'new_v_w_in', 'new_v_pool_w', 'new_v_pool_scale', 'new_v_ret_norm', 'new_v_w_out', 'new_v_ffn2_norm', 'new_v_ffn2_w1', 'new_v_ffn2_w3', 'new_v_ffn2_w2', 'new_v_final_norm']
TWIN_LEAF_KINDS = {'loss': 'loss', 'grad_x': 'grad_x', 'grad_ffn1_norm': 'grad_w', 'grad_ffn1_w1': 'grad_w', 'grad_ffn1_w3': 'grad_w', 'grad_ffn1_w2': 'grad_w', 'grad_mix_norm': 'grad_w', 'grad_w_in': 'grad_w', 'grad_pool_w': 'grad_w', 'grad_pool_scale': 'grad_w', 'grad_ret_norm': 'grad_w', 'grad_w_out': 'grad_w', 'grad_ffn2_norm': 'grad_w', 'grad_ffn2_w1': 'grad_w', 'grad_ffn2_w3': 'grad_w', 'grad_ffn2_w2': 'grad_w', 'grad_final_norm': 'grad_w', 'delta_ffn1_norm': 'delta_w', 'delta_ffn1_w1': 'delta_w', 'delta_ffn1_w3': 'delta_w', 'delta_ffn1_w2': 'delta_w', 'delta_mix_norm': 'delta_w', 'delta_w_in': 'delta_w', 'delta_pool_w': 'delta_w', 'delta_pool_scale': 'delta_w', 'delta_ret_norm': 'delta_w', 'delta_w_out': 'delta_w', 'delta_ffn2_norm': 'delta_w', 'delta_ffn2_w1': 'delta_w', 'delta_ffn2_w3': 'delta_w', 'delta_ffn2_w2': 'delta_w', 'delta_final_norm': 'delta_w', 'new_m_ffn1_norm': 'new_m', 'new_m_ffn1_w1': 'new_m', 'new_m_ffn1_w3': 'new_m', 'new_m_ffn1_w2': 'new_m', 'new_m_mix_norm': 'new_m', 'new_m_w_in': 'new_m', 'new_m_pool_w': 'new_m', 'new_m_pool_scale': 'new_m', 'new_m_ret_norm': 'new_m', 'new_m_w_out': 'new_m', 'new_m_ffn2_norm': 'new_m', 'new_m_ffn2_w1': 'new_m', 'new_m_ffn2_w3': 'new_m', 'new_m_ffn2_w2': 'new_m', 'new_m_final_norm': 'new_m', 'new_v_ffn1_norm': 'new_v', 'new_v_ffn1_w1': 'new_v', 'new_v_ffn1_w3': 'new_v', 'new_v_ffn1_w2': 'new_v', 'new_v_mix_norm': 'new_v', 'new_v_w_in': 'new_v', 'new_v_pool_w': 'new_v', 'new_v_pool_scale': 'new_v', 'new_v_ret_norm': 'new_v', 'new_v_w_out': 'new_v', 'new_v_ffn2_norm': 'new_v', 'new_v_ffn2_w1': 'new_v', 'new_v_ffn2_w3': 'new_v', 'new_v_ffn2_w2': 'new_v', 'new_v_final_norm': 'new_v'}


def _forward(args):
    return _fwd_reference(*[args[k] for k in FWD_PARAMS])


def _output_shape():
    out = _jax.eval_shape(lambda: _forward(_fwd_setup_inputs(0)))
    return out.shape, out.dtype

N_MICROBATCH = 1
ADAM_LR = 0.001
ADAM_B1 = 0.9
ADAM_B2 = 0.999
ADAM_EPS = 1e-08
ADAM_WD = 0.01
ADAM_STEP = 10
PER_EXAMPLE_BATCH_AXIS = {'x': 0, 'loss_target': 0}
SHARED_INPUTS = []
_WEIGHT_DTYPES = {'ffn1_norm': _jnp.float32, 'ffn1_w1': _jnp.float32, 'ffn1_w3': _jnp.float32, 'ffn1_w2': _jnp.float32, 'mix_norm': _jnp.float32, 'w_in': _jnp.float32, 'pool_w': _jnp.float32, 'pool_scale': _jnp.float32, 'ret_norm': _jnp.float32, 'w_out': _jnp.float32, 'ffn2_norm': _jnp.float32, 'ffn2_w1': _jnp.float32, 'ffn2_w3': _jnp.float32, 'ffn2_w2': _jnp.float32, 'final_norm': _jnp.float32}
MOMENT_SCALE = {'ffn1_norm': 9.208341e-02, 'ffn1_w1': 3.772909e-02, 'ffn1_w3': 3.643938e-02, 'ffn1_w2': 6.055299e-02, 'mix_norm': 1.509607e-01, 'w_in': 9.583619e-02, 'pool_w': 1.313277e-01, 'pool_scale': 1.293913e-01, 'ret_norm': 9.378186e-02, 'w_out': 1.118390e-01, 'ffn2_norm': 5.961791e-02, 'ffn2_w1': 2.518456e-02, 'ffn2_w3': 2.451336e-02, 'ffn2_w2': 4.066266e-02, 'final_norm': 3.200890e+01}


def _to_microbatches(a, axis):
    t = _jnp.moveaxis(a, axis, 0)
    t = t.reshape((N_MICROBATCH, t.shape[0] // N_MICROBATCH) + t.shape[1:])
    return _jnp.moveaxis(t, 1, axis + 1)


def setup_inputs(seed: int = 0) -> dict:
    inp = _fwd_setup_inputs(seed)
    key = _jax.random.fold_in(_jax.random.key(seed), 7919)
    shape, _ = _output_shape()
    out = dict(inp)
    out["loss_target"] = _jax.random.normal(_jax.random.fold_in(key, 0), shape, _jnp.float32)
    for i, name in enumerate(TWIN_WEIGHTS):
        w = inp[name].astype(_jnp.float32)
        if MOMENT_SCALE is None:
            s = _jnp.sqrt(_jnp.mean(_jnp.square(w)) + 1e-30)
        else:
            s = MOMENT_SCALE[name]
        km, kv = _jax.random.split(_jax.random.fold_in(key, i + 1))
        out[name] = w
        out["m_" + name] = s * _jax.random.normal(km, w.shape, _jnp.float32)
        out["v_" + name] = (s * s) * _jax.random.uniform(kv, w.shape, _jnp.float32, 0.5, 1.5)
    if N_MICROBATCH > 1:
        for name, axis in PER_EXAMPLE_BATCH_AXIS.items():
            out[name] = _to_microbatches(out[name], axis)
    return {'x': out['x'], 'ffn1_norm': out['ffn1_norm'], 'ffn1_w1': out['ffn1_w1'], 'ffn1_w3': out['ffn1_w3'], 'ffn1_w2': out['ffn1_w2'], 'mix_norm': out['mix_norm'], 'w_in': out['w_in'], 'pool_w': out['pool_w'], 'pool_scale': out['pool_scale'], 'ret_norm': out['ret_norm'], 'w_out': out['w_out'], 'ffn2_norm': out['ffn2_norm'], 'ffn2_w1': out['ffn2_w1'], 'ffn2_w3': out['ffn2_w3'], 'ffn2_w2': out['ffn2_w2'], 'final_norm': out['final_norm'], 'loss_target': out['loss_target'], 'm_ffn1_norm': out['m_ffn1_norm'], 'm_ffn1_w1': out['m_ffn1_w1'], 'm_ffn1_w3': out['m_ffn1_w3'], 'm_ffn1_w2': out['m_ffn1_w2'], 'm_mix_norm': out['m_mix_norm'], 'm_w_in': out['m_w_in'], 'm_pool_w': out['m_pool_w'], 'm_pool_scale': out['m_pool_scale'], 'm_ret_norm': out['m_ret_norm'], 'm_w_out': out['m_w_out'], 'm_ffn2_norm': out['m_ffn2_norm'], 'm_ffn2_w1': out['m_ffn2_w1'], 'm_ffn2_w3': out['m_ffn2_w3'], 'm_ffn2_w2': out['m_ffn2_w2'], 'm_final_norm': out['m_final_norm'], 'v_ffn1_norm': out['v_ffn1_norm'], 'v_ffn1_w1': out['v_ffn1_w1'], 'v_ffn1_w3': out['v_ffn1_w3'], 'v_ffn1_w2': out['v_ffn1_w2'], 'v_mix_norm': out['v_mix_norm'], 'v_w_in': out['v_w_in'], 'v_pool_w': out['v_pool_w'], 'v_pool_scale': out['v_pool_scale'], 'v_ret_norm': out['v_ret_norm'], 'v_w_out': out['v_w_out'], 'v_ffn2_norm': out['v_ffn2_norm'], 'v_ffn2_w1': out['v_ffn2_w1'], 'v_ffn2_w3': out['v_ffn2_w3'], 'v_ffn2_w2': out['v_ffn2_w2'], 'v_final_norm': out['v_final_norm']}


def _loss(weights, diff, rest, loss_target):
    with _jax.named_scope("forward"):
        args = {**rest, TWIN_DIFF_INPUT: diff, **{k: w.astype(_WEIGHT_DTYPES[k]) for k, w in weights.items()}}
        y = _forward(args)
    with _jax.named_scope("loss_head"):
        err = _jnp.square(y.astype(_jnp.float32) - loss_target)
        return 0.5 * _jnp.sum(_jnp.mean(err, axis=-1)) if err.ndim else 0.5 * err


def _adamw(w, g, m, v):
    m = ADAM_B1 * m + (1.0 - ADAM_B1) * g
    v = ADAM_B2 * v + (1.0 - ADAM_B2) * _jnp.square(g)
    m_hat = m / (1.0 - ADAM_B1 ** ADAM_STEP)
    v_hat = v / (1.0 - ADAM_B2 ** ADAM_STEP)
    delta = -ADAM_LR * (m_hat / (_jnp.sqrt(v_hat) + ADAM_EPS) + ADAM_WD * w)
    return delta, m, v


def reference(x, ffn1_norm, ffn1_w1, ffn1_w3, ffn1_w2, mix_norm, w_in, pool_w, pool_scale, ret_norm, w_out, ffn2_norm, ffn2_w1, ffn2_w3, ffn2_w2, final_norm, loss_target, m_ffn1_norm, m_ffn1_w1, m_ffn1_w3, m_ffn1_w2, m_mix_norm, m_w_in, m_pool_w, m_pool_scale, m_ret_norm, m_w_out, m_ffn2_norm, m_ffn2_w1, m_ffn2_w3, m_ffn2_w2, m_final_norm, v_ffn1_norm, v_ffn1_w1, v_ffn1_w3, v_ffn1_w2, v_mix_norm, v_w_in, v_pool_w, v_pool_scale, v_ret_norm, v_w_out, v_ffn2_norm, v_ffn2_w1, v_ffn2_w3, v_ffn2_w2, v_final_norm):
    given = dict(x=x, ffn1_norm=ffn1_norm, ffn1_w1=ffn1_w1, ffn1_w3=ffn1_w3, ffn1_w2=ffn1_w2, mix_norm=mix_norm, w_in=w_in, pool_w=pool_w, pool_scale=pool_scale, ret_norm=ret_norm, w_out=w_out, ffn2_norm=ffn2_norm, ffn2_w1=ffn2_w1, ffn2_w3=ffn2_w3, ffn2_w2=ffn2_w2, final_norm=final_norm, loss_target=loss_target, m_ffn1_norm=m_ffn1_norm, m_ffn1_w1=m_ffn1_w1, m_ffn1_w3=m_ffn1_w3, m_ffn1_w2=m_ffn1_w2, m_mix_norm=m_mix_norm, m_w_in=m_w_in, m_pool_w=m_pool_w, m_pool_scale=m_pool_scale, m_ret_norm=m_ret_norm, m_w_out=m_w_out, m_ffn2_norm=m_ffn2_norm, m_ffn2_w1=m_ffn2_w1, m_ffn2_w3=m_ffn2_w3, m_ffn2_w2=m_ffn2_w2, m_final_norm=m_final_norm, v_ffn1_norm=v_ffn1_norm, v_ffn1_w1=v_ffn1_w1, v_ffn1_w3=v_ffn1_w3, v_ffn1_w2=v_ffn1_w2, v_mix_norm=v_mix_norm, v_w_in=v_w_in, v_pool_w=v_pool_w, v_pool_scale=v_pool_scale, v_ret_norm=v_ret_norm, v_w_out=v_w_out, v_ffn2_norm=v_ffn2_norm, v_ffn2_w1=v_ffn2_w1, v_ffn2_w3=v_ffn2_w3, v_ffn2_w2=v_ffn2_w2, v_final_norm=v_final_norm)
    weights = {n: given[n] for n in TWIN_WEIGHTS}
    shared = {n: given[n] for n in SHARED_INPUTS}
    per_example = {n: given[n] for n in ['x']}
    grad_fn = _jax.value_and_grad(_loss, argnums=(0, 1))

    def one_microbatch(ex, loss_target):
        ex = dict(ex)
        diff = ex.pop(TWIN_DIFF_INPUT)
        return grad_fn(weights, diff, {**shared, **ex}, loss_target)

    if N_MICROBATCH == 1:
        loss, (grad_w, grad_x) = one_microbatch(per_example, given["loss_target"])
    else:
        def body(carry, xs):
            loss_sum, grad_sum = carry
            l_k, (gw_k, gx_k) = one_microbatch(xs[0], xs[1])
            with _jax.named_scope("update"):
                return (loss_sum + l_k, _jax.tree.map(_jnp.add, grad_sum, gw_k)), gx_k

        init = (_jnp.zeros((), _jnp.float32), _jax.tree.map(_jnp.zeros_like, weights))
        (loss, grad_w), grad_x = _jax.lax.scan(body, init, (per_example, given["loss_target"]))
    with _jax.named_scope("update"):
        delta_w, new_m, new_v = {}, {}, {}
        for n in TWIN_WEIGHTS:
            delta_w[n], new_m[n], new_v[n] = _adamw(weights[n], grad_w[n], given["m_" + n], given["v_" + n])
    return (loss, grad_x, *[grad_w[n] for n in TWIN_WEIGHTS], *[delta_w[n] for n in TWIN_WEIGHTS],
            *[new_m[n] for n in TWIN_WEIGHTS], *[new_v[n] for n in TWIN_WEIGHTS])
```

```python
import functools

import numpy as np
import jax
import jax.numpy as jnp
from jax import lax
from jax.experimental import pallas as pl
from jax.experimental.pallas import tpu as pltpu

F32 = jnp.float32
BF16 = jnp.bfloat16

NDEV = 8
EPS = 1e-6
N_POOL_GROUPS = 4
POOL_WINDOWS = (2, 4, 8, 16)
MAX_WINDOW = 16
GROUP = 128
RET_HEADS = 4
ROPE_BASE = 10000.0
ADAM_LR = 0.001
ADAM_B1 = 0.9
ADAM_B2 = 0.999
ADAM_EPS = 1e-08
ADAM_WD = 0.01
ADAM_STEP = 10

VMEM_LIMIT = 56 * 1024 * 1024
FFN_CHUNK = 256

NT = (((1,), (1,)), ((), ()))
NN = (((1,), (0,)), ((), ()))
TN = (((0,), (0,)), ((), ()))


def _dot(a, b, dims):
    return lax.dot_general(a, b, dims, preferred_element_type=F32)


def _call(body, **kw):
    return pl.pallas_call(body, **kw)


def _params(**kw):
    return pltpu.CompilerParams(vmem_limit_bytes=VMEM_LIMIT, **kw)


def _seq(n):
    return _params(dimension_semantics=("arbitrary",) * n)


def _peer(k):
    x, y, c = lax.axis_index("x"), lax.axis_index("y"), lax.axis_index("c")
    return (1 - x if k & 4 else x, 1 - y if k & 2 else y, 1 - c if k & 1 else c)


def _flat(pos):
    return 4 * pos[0] + 2 * pos[1] + pos[2]


def _row_tile(rows, cap):
    return max(t for t in range(16, min(rows, cap) + 1, 16) if rows % t == 0)


def _load_weights(wbuf_ref, dsts, sems):
    copies = []
    for off, rows, dst in dsts:
        for p in range(NDEV):
            cp = pltpu.make_async_copy(
                wbuf_ref.at[p, pl.ds(off, rows), :], dst.at[pl.ds(p * rows, rows), :], sems.at[len(copies)]
            )
            cp.start()
            copies.append(cp)
    return copies


def _sigmoid(a):
    return 1.0 / (1.0 + jnp.exp(-a))


def _all_gather(wsend):
    rows, d = wsend.shape

    def body(w_ref, out_ref, send_sems, recv_sems, local_sem):
        me, sibling = _peer(0), _peer(1)
        chips = (4, 2, 6)

        def copy(k, block, to, src=None):
            dst = out_ref.at[_flat(block)]
            return pltpu.make_async_remote_copy(
                src_ref=dst if src is None else src,
                dst_ref=dst,
                send_sem=send_sems.at[k],
                recv_sem=recv_sems.at[k],
                device_id=to,
                device_id_type=pl.DeviceIdType.MESH,
            )

        mine = pltpu.make_async_copy(w_ref, out_ref.at[_flat(me)], local_sem)
        mine.start()
        first = [copy(0, me, sibling, src=w_ref)]
        first += [copy(1 + j, me, _peer(k), src=w_ref) for j, k in enumerate(chips)]
        for cp in first:
            cp.start()
        passed = [copy(4 + j, _peer(k), sibling) for j, k in enumerate(chips)]
        for j, k in enumerate(chips):
            copy(1 + j, _peer(k), me).wait_recv()
            passed[j].start()
        copy(0, sibling, me).wait_recv()
        for j, k in enumerate(chips):
            copy(4 + j, _peer(k ^ 1), me).wait_recv()
        for cp in first + passed:
            cp.wait_send()
        mine.wait()

    return _call(
        body,
        name="weights_all_gather",
        out_shape=jax.ShapeDtypeStruct((NDEV, rows, d), wsend.dtype),
        in_specs=[pl.BlockSpec(memory_space=pl.ANY)],
        out_specs=pl.BlockSpec(memory_space=pl.ANY),
        scratch_shapes=[pltpu.SemaphoreType.DMA((7,)), pltpu.SemaphoreType.DMA((7,)), pltpu.SemaphoreType.DMA],
        compiler_params=pltpu.CompilerParams(has_side_effects=True),
    )(wsend)


def _grad_exchange(gbuf, stats, pw):
    def body(g_ref, s_ref, p_ref, rg_ref, rs_ref, rp_ref, send_sems, recv_sems, local_sems):
        me = _peer(0)
        mine = _flat(me)
        local = [
            pltpu.make_async_copy(g_ref.at[mine], rg_ref.at[mine], local_sems.at[0]),
            pltpu.make_async_copy(s_ref, rs_ref.at[mine], local_sems.at[1]),
            pltpu.make_async_copy(p_ref, rp_ref.at[mine], local_sems.at[2]),
        ]
        for cp in local:
            cp.start()
        sent = []
        for k in range(1, NDEV):
            to = _peer(k)
            for j, (src, dst) in enumerate(
                ((g_ref.at[_flat(to)], rg_ref.at[mine]), (s_ref, rs_ref.at[mine]), (p_ref, rp_ref.at[mine]))
            ):
                cp = pltpu.make_async_remote_copy(
                    src_ref=src,
                    dst_ref=dst,
                    send_sem=send_sems.at[3 * (k - 1) + j],
                    recv_sem=recv_sems.at[3 * (k - 1) + j],
                    device_id=to,
                    device_id_type=pl.DeviceIdType.MESH,
                )
                cp.start()
                sent.append(cp)
        for k in range(1, NDEV):
            frm = _flat(_peer(k))
            for j, (src, dst) in enumerate(
                ((g_ref.at[mine], rg_ref.at[frm]), (s_ref, rs_ref.at[frm]), (p_ref, rp_ref.at[frm]))
            ):
                pltpu.make_async_remote_copy(
                    src_ref=src,
                    dst_ref=dst,
                    send_sem=send_sems.at[3 * (k - 1) + j],
                    recv_sem=recv_sems.at[3 * (k - 1) + j],
                    device_id=_peer(k),
                    device_id_type=pl.DeviceIdType.MESH,
                ).wait_recv()
        for cp in sent:
            cp.wait_send()
        for cp in local:
            cp.wait()

    any_spec = pl.BlockSpec(memory_space=pl.ANY)
    return _call(
        body,
        name="grad_exchange",
        out_shape=(
            jax.ShapeDtypeStruct(gbuf.shape, gbuf.dtype),
            jax.ShapeDtypeStruct((NDEV,) + stats.shape, stats.dtype),
            jax.ShapeDtypeStruct((NDEV,) + pw.shape, pw.dtype),
        ),
        in_specs=[any_spec] * 3,
        out_specs=(any_spec,) * 3,
        scratch_shapes=[
            pltpu.SemaphoreType.DMA((3 * (NDEV - 1),)),
            pltpu.SemaphoreType.DMA((3 * (NDEV - 1),)),
            pltpu.SemaphoreType.DMA((3,)),
        ],
        compiler_params=pltpu.CompilerParams(has_side_effects=True),
    )(gbuf, stats, pw)


def _ffn_fwd(x, gain, wbuf, off, ffn, head=None):
    s, d = x.shape
    fs = ffn // NDEV
    tm = min(512, s)
    nchunk = ffn // FFN_CHUNK

    def body(*refs):
        if head is None:
            x_ref, g_ref, wbuf_ref, h_ref, a_ref, b_ref, w1s, w3s, w2s, sems = refs
        else:
            x_ref, g_ref, wbuf_ref, gf_ref, t_ref, h_ref, a_ref, b_ref, dgf_ref, loss_ref, w1s, w3s, w2s, sems = refs
        i = pl.program_id(0)

        @pl.when(i == 0)
        def _():
            for cp in _load_weights(wbuf_ref, ((off, fs, w1s), (off + fs, fs, w3s), (off + 2 * fs, fs, w2s)), sems):
                cp.wait()
            if head is not None:
                dgf_ref[...] = jnp.zeros_like(dgf_ref)
                loss_ref[...] = jnp.zeros_like(loss_ref)

        xv = x_ref[...]
        r = lax.rsqrt(jnp.mean(xv * xv, axis=-1, keepdims=True) + EPS)
        n = (xv * r * g_ref[...]).astype(BF16)
        acc = jnp.zeros((tm, d), F32)
        for c in range(nchunk):
            cols = slice(c * FFN_CHUNK, (c + 1) * FFN_CHUNK)
            a = _dot(n, w1s[cols, :], NT)
            b = _dot(n, w3s[cols, :], NT)
            a_ref[:, cols] = a.astype(BF16)
            b_ref[:, cols] = b.astype(BF16)
            hm = (a * _sigmoid(a) * b).astype(BF16)
            acc = acc + _dot(hm, w2s[cols, :], NN)
        h = xv + 0.5 * acc
        if head is None:
            h_ref[...] = h
        else:
            rf = lax.rsqrt(jnp.mean(h * h, axis=-1, keepdims=True) + EPS)
            nh = h * rf
            gf = gf_ref[...]
            err = nh * gf - t_ref[...]
            loss_ref[...] += jnp.sum(err * err, axis=0, keepdims=True) * (0.5 / d)
            dy = err * (1.0 / d)
            dgf_ref[...] += jnp.sum(dy * nh, axis=0, keepdims=True)
            dn = dy * gf
            h_ref[...] = rf * (dn - nh * jnp.mean(dn * nh, axis=-1, keepdims=True))

    tile = pl.BlockSpec((tm, d), lambda i: (i, 0))
    row = pl.BlockSpec((1, d), lambda i: (0, 0))
    wide = pl.BlockSpec((tm, ffn), lambda i: (i, 0))
    in_specs = [tile, row, pl.BlockSpec(memory_space=pl.ANY)]
    out_shape = [
        jax.ShapeDtypeStruct((s, d), F32),
        jax.ShapeDtypeStruct((s, ffn), BF16),
        jax.ShapeDtypeStruct((s, ffn), BF16),
    ]
    out_specs = [tile, wide, wide]
    args = [x, gain, wbuf]
    if head is not None:
        in_specs += [row, tile]
        args += list(head)
        out_shape += [jax.ShapeDtypeStruct((1, d), F32)] * 2
        out_specs += [row, row]
    return _call(
        body,
        name="ffn_fwd_loss" if head is not None else "ffn_fwd",
        grid=(s // tm,),
        in_specs=in_specs,
        out_specs=out_specs,
        out_shape=out_shape,
        scratch_shapes=[pltpu.VMEM((ffn, d), BF16)] * 3 + [pltpu.SemaphoreType.DMA((3 * NDEV,))],
        compiler_params=_seq(1),
    )(*args)


def _ffn_bwd(dh, x, a, b, gain, wbuf, off, ffn, name):
    s, d = x.shape
    fs = ffn // NDEV
    tm = min(256, s)
    nchunk = ffn // FFN_CHUNK

    def body(dh_ref, x_ref, a_ref, b_ref, g_ref, wbuf_ref, dx_ref, da_ref, db_ref, hm_ref, n_ref, dg_ref, w1s, w3s, w2s, sems):
        i = pl.program_id(0)

        @pl.when(i == 0)
        def _():
            for cp in _load_weights(wbuf_ref, ((off, fs, w1s), (off + fs, fs, w3s), (off + 2 * fs, fs, w2s)), sems):
                cp.wait()
            dg_ref[...] = jnp.zeros_like(dg_ref)

        xv = x_ref[...]
        dhv = dh_ref[...]
        g = g_ref[...]
        r = lax.rsqrt(jnp.mean(xv * xv, axis=-1, keepdims=True) + EPS)
        nh = xv * r
        n_ref[...] = (nh * g).astype(BF16)
        dout = (0.5 * dhv).astype(BF16)
        dn = jnp.zeros((tm, d), F32)
        for c in range(nchunk):
            cols = slice(c * FFN_CHUNK, (c + 1) * FFN_CHUNK)
            av = a_ref[:, cols].astype(F32)
            bv = b_ref[:, cols].astype(F32)
            sg = _sigmoid(av)
            sl = av * sg
            hm_ref[:, cols] = (sl * bv).astype(BF16)
            dhm = _dot(dout, w2s[cols, :], NT)
            da = (dhm * bv * (sg * (1.0 + av * (1.0 - sg)))).astype(BF16)
            db = (dhm * sl).astype(BF16)
            da_ref[:, cols] = da
            db_ref[:, cols] = db
            dn = dn + _dot(da, w1s[cols, :], NN) + _dot(db, w3s[cols, :], NN)
        dg_ref[...] += jnp.sum(dn * nh, axis=0, keepdims=True)
        dnh = dn * g
        dx_ref[...] = dhv + r * (dnh - nh * jnp.mean(dnh * nh, axis=-1, keepdims=True))

    tile = pl.BlockSpec((tm, d), lambda i: (i, 0))
    row = pl.BlockSpec((1, d), lambda i: (0, 0))
    wide = pl.BlockSpec((tm, ffn), lambda i: (i, 0))
    return _call(
        body,
        name=name,
        grid=(s // tm,),
        in_specs=[tile, tile, wide, wide, row, pl.BlockSpec(memory_space=pl.ANY)],
        out_specs=[tile, wide, wide, wide, tile, row],
        out_shape=[
            jax.ShapeDtypeStruct((s, d), F32),
            jax.ShapeDtypeStruct((s, ffn), BF16),
            jax.ShapeDtypeStruct((s, ffn), BF16),
            jax.ShapeDtypeStruct((s, ffn), BF16),
            jax.ShapeDtypeStruct((s, d), BF16),
            jax.ShapeDtypeStruct((1, d), F32),
        ],
        scratch_shapes=[pltpu.VMEM((ffn, d), BF16)] * 3 + [pltpu.SemaphoreType.DMA((3 * NDEV,))],
        compiler_params=_seq(1),
    )(dh, x, a, b, gain, wbuf)


def _wgrad(lhs, rhs, scale, name):
    s, m = lhs.shape
    n = rhs.shape[1]
    tk = min(512, s)
    steps = s // tk

    def body(l_ref, r_ref, o_ref, acc):
        k = pl.program_id(0)

        @pl.when(k == 0)
        def _():
            acc[...] = jnp.zeros_like(acc)

        acc[...] += _dot(l_ref[...], r_ref[...].astype(BF16), TN)

        @pl.when(k == steps - 1)
        def _():
            o_ref[...] = (acc[...] * scale).astype(BF16)

    return _call(
        body,
        name=name,
        grid=(steps,),
        in_specs=[pl.BlockSpec((tk, m), lambda k: (k, 0)), pl.BlockSpec((tk, n), lambda k: (k, 0))],
        out_specs=pl.BlockSpec((m, n), lambda k: (0, 0)),
        out_shape=jax.ShapeDtypeStruct((m, n), BF16),
        scratch_shapes=[pltpu.VMEM((m, n), F32)],
        compiler_params=_seq(1),
    )(lhs, rhs)


def _mix_constants(s):
    c = GROUP
    lg = np.log1p(-np.exp2(-5.0 - np.arange(RET_HEADS, dtype=np.float32))).astype(np.float32)
    pos = np.arange(c, dtype=np.float32)
    rel = pos[:, None] - pos[None, :]
    decay = np.where(rel[None] >= 0, np.exp(lg[:, None, None] * np.maximum(rel, 0.0)[None]), 0.0).astype(np.float32)
    ktail = np.exp(lg[:, None] * (c - 1 - pos)[None, :]).astype(np.float32)
    qhead = np.exp(lg[:, None] * (pos + 1.0)[None, :]).astype(np.float32)
    chunk_decay = [float(v) for v in np.exp(lg * np.float32(c)).astype(np.float32)]
    ones = np.ones((1, 1, c), np.float32)
    inv_freq = (1.0 / (np.float32(ROPE_BASE) ** (np.arange(0, c, 2, dtype=np.float32) / np.float32(c)))).astype(np.float32)
    ang = (np.arange(s, dtype=np.float32)[:, None] * inv_freq[None, :]).astype(np.float32)
    cos, sin = np.cos(ang).astype(np.float32), np.sin(ang).astype(np.float32)
    return dict(
        decay=jnp.asarray(decay),
        ktail=jnp.asarray(ktail[:, :, None] * ones),
        qhead=jnp.asarray(qhead[:, :, None] * ones),
        chunk_decay=chunk_decay,
        cos=jnp.asarray(np.concatenate([cos, cos], axis=-1)),
        sin=jnp.asarray(np.concatenate([-sin, sin], axis=-1)),
    )


def _rope(t, cos, sin):
    return t * cos + pltpu.roll(t, GROUP // 2, axis=1) * sin


def _rope_bwd(dt, cos, sin):
    return dt * cos + pltpu.roll(dt * sin, GROUP // 2, axis=1)


def _window_sums(ext, w, forward):
    rows = ext.shape[0]
    acc, k = ext, 1
    while k < w:
        acc = acc + pltpu.roll(acc, k if forward else rows - k, axis=0)
        k *= 2
    return acc


def _pool_counts(tile, tm, w):
    t = lax.broadcasted_iota(jnp.int32, (tm, 1), 0) + tile * tm
    return jnp.minimum(t + 1, w).astype(F32)


def _mix_fwd(h1, gain, wbuf, off_in, off_out, pool_w, pool_scale, ret_gain, consts):
    s, d = h1.shape
    pwid = N_POOL_GROUPS * GROUP
    rwid = RET_HEADS * GROUP
    inw = pwid + 4 * rwid
    tm = min(256, s)
    nck = tm // GROUP
    cd = consts["chunk_decay"]

    def body(h_ref, g_ref, wbuf_ref, pw_ref, ps_ref, rg_ref, cos_ref, sin_ref, dec_ref, kt_ref, qh_ref,
             h2_ref, proj_ref, o_ref, rs_ref, wins, wouts, state, carry, mbuf, sems):
        i = pl.program_id(0)

        @pl.when(i == 0)
        def _():
            for cp in _load_weights(wbuf_ref, ((off_in, inw // NDEV, wins), (off_out, d // NDEV, wouts)), sems):
                cp.wait()
            state[...] = jnp.zeros_like(state)
            carry[...] = jnp.zeros_like(carry)

        hv = h_ref[...]
        r = lax.rsqrt(jnp.mean(hv * hv, axis=-1, keepdims=True) + EPS)
        u = (hv * r * g_ref[...]).astype(BF16)
        proj_ref[...] = _dot(u, wins[...], NT)

        ext = jnp.concatenate([carry[...], proj_ref[:, 0:pwid]], axis=0)
        carry[...] = proj_ref[tm - MAX_WINDOW:tm, 0:pwid]
        for gi, w in enumerate(POOL_WINDOWS):
            cols = slice(gi * GROUP, (gi + 1) * GROUP)
            xg = ext[:, cols]
            ws = _window_sums(xg, w, True)[MAX_WINDOW:, :]
            pooled = ws / _pool_counts(i, tm, w) - xg[MAX_WINDOW:, :]
            z = _dot(pooled.astype(BF16), pw_ref[gi].astype(BF16), NN)
            mbuf[:, cols] = (z * ps_ref[:, cols]).astype(BF16)

        cos, sin = cos_ref[...], sin_ref[...]
        for h in range(RET_HEADS):
            cq = slice(pwid + h * GROUP, pwid + (h + 1) * GROUP)
            ck = slice(pwid + rwid + h * GROUP, pwid + rwid + (h + 1) * GROUP)
            cv = slice(pwid + 2 * rwid + h * GROUP, pwid + 2 * rwid + (h + 1) * GROUP)
            cg = slice(pwid + 3 * rwid + h * GROUP, pwid + 3 * rwid + (h + 1) * GROUP)
            ch = slice(h * GROUP, (h + 1) * GROUP)
            qr = _rope(proj_ref[:, cq], cos, sin)
            kr = _rope(proj_ref[:, ck], cos, sin) * (GROUP ** -0.5)
            vb = proj_ref[:, cv].astype(BF16)
            for n in range(nck):
                rows = slice(n * GROUP, (n + 1) * GROUP)
                qc, kc, vc = qr[rows], kr[rows], vb[rows]
                rb = state[h]
                rs_ref[n, h] = rb
                p = (_dot(qc.astype(BF16), kc.astype(BF16), NT) * dec_ref[h]).astype(BF16)
                o = _dot(p, vc, NN) + _dot((qc * qh_ref[h]).astype(BF16), rb.astype(BF16), NN)
                state[h] = cd[h] * rb + _dot((kc * kt_ref[h]).astype(BF16), vc, TN)
                o_ref[rows, ch] = o
                on = o * lax.rsqrt(jnp.mean(o * o, axis=-1, keepdims=True) + EPS)
                gv = proj_ref[rows, cg]
                mbuf[rows, pwid + h * GROUP:pwid + (h + 1) * GROUP] = (
                    gv * _sigmoid(gv) * (on * rg_ref[:, ch])
                ).astype(BF16)
        h2_ref[...] = hv + _dot(mbuf[...], wouts[...], NN)

    tile = pl.BlockSpec((tm, d), lambda i: (i, 0))
    full = lambda shape: pl.BlockSpec(shape, lambda i: (0,) * len(shape))
    return _call(
        body,
        name="mix_fwd",
        grid=(s // tm,),
        in_specs=[
            tile, full((1, d)), pl.BlockSpec(memory_space=pl.ANY),
            full((N_POOL_GROUPS, GROUP, GROUP)), full((1, pwid)), full((1, rwid)),
            pl.BlockSpec((tm, GROUP), lambda i: (i, 0)), pl.BlockSpec((tm, GROUP), lambda i: (i, 0)),
            full((RET_HEADS, GROUP, GROUP)), full((RET_HEADS, GROUP, GROUP)), full((RET_HEADS, GROUP, GROUP)),
        ],
        out_specs=[
            tile,
            pl.BlockSpec((tm, inw), lambda i: (i, 0)),
            pl.BlockSpec((tm, rwid), lambda i: (i, 0)),
            pl.BlockSpec((nck, RET_HEADS, GROUP, GROUP), lambda i: (i, 0, 0, 0)),
        ],
        out_shape=[
            jax.ShapeDtypeStruct((s, d), F32),
            jax.ShapeDtypeStruct((s, inw), F32),
            jax.ShapeDtypeStruct((s, rwid), F32),
            jax.ShapeDtypeStruct((s // GROUP, RET_HEADS, GROUP, GROUP), F32),
        ],
        scratch_shapes=[
            pltpu.VMEM((inw, d), BF16), pltpu.VMEM((d, d), BF16),
            pltpu.VMEM((RET_HEADS, GROUP, GROUP), F32), pltpu.VMEM((MAX_WINDOW, pwid), F32),
            pltpu.VMEM((tm, d), BF16), pltpu.SemaphoreType.DMA((2 * NDEV,)),
        ],
        compiler_params=_seq(1),
    )(h1, gain, wbuf, pool_w, pool_scale, ret_gain, consts["cos"], consts["sin"], consts["decay"], consts["ktail"], consts["qhead"])


def _mix_bwd(dh2, h1, proj, o_saved, rsave, gain, wbuf, off_in, off_out, pool_w, pool_scale, ret_gain, consts):
    s, d = h1.shape
    pwid = N_POOL_GROUPS * GROUP
    rwid = RET_HEADS * GROUP
    inw = pwid + 4 * rwid
    tm = min(256, s)
    nck = tm // GROUP
    nt = s // tm
    cd = consts["chunk_decay"]
    halo_per_tile = tm // MAX_WINDOW

    def body(dh2_ref, h_ref, proj_ref, halo_ref, o_ref, rs_ref, g_ref, wbuf_ref, pw_ref, ps_ref, rg_ref,
             cos_ref, sin_ref, dec_ref, kt_ref, qh_ref,
             dh1_ref, dproj_ref, u_ref, m_ref, dpw_ref, dps_ref, drg_ref, dg_ref,
             wins, wouts, dstate, carry, dm, dpj, sems):
        i = pl.program_id(0)
        tile = nt - 1 - i

        @pl.when(i == 0)
        def _():
            for cp in _load_weights(wbuf_ref, ((off_in, inw // NDEV, wins), (off_out, d // NDEV, wouts)), sems):
                cp.wait()
            dstate[...] = jnp.zeros_like(dstate)
            carry[...] = jnp.zeros_like(carry)
            for ref in (dpw_ref, dps_ref, drg_ref, dg_ref):
                ref[...] = jnp.zeros_like(ref)

        dh2v = dh2_ref[...]
        dm[...] = _dot(dh2v.astype(BF16), wouts[...], NT)
        hv = h_ref[...]
        g = g_ref[...]
        r = lax.rsqrt(jnp.mean(hv * hv, axis=-1, keepdims=True) + EPS)
        uh = hv * r
        u_ref[...] = (uh * g).astype(BF16)

        halo = jnp.where(tile == 0, 0.0, halo_ref[...])
        ext = jnp.concatenate([halo, proj_ref[:, 0:pwid]], axis=0)
        next_dpn = carry[...]
        for gi, w in enumerate(POOL_WINDOWS):
            cols = slice(gi * GROUP, (gi + 1) * GROUP)
            xg = ext[:, cols]
            cnt = _pool_counts(tile, tm, w)
            pooled = (_window_sums(xg, w, True)[MAX_WINDOW:, :] / cnt - xg[MAX_WINDOW:, :]).astype(BF16)
            pwb = pw_ref[gi].astype(BF16)
            z = _dot(pooled, pwb, NN)
            scale = ps_ref[:, cols]
            m_ref[:, cols] = (z * scale).astype(BF16)
            da = dm[:, cols]
            dps_ref[:, cols] += jnp.sum(da * z, axis=0, keepdims=True)
            dz = (da * scale).astype(BF16)
            dpw_ref[gi] += _dot(pooled, dz, TN)
            dpl = _dot(dz, pwb, NT)
            dpn = dpl / cnt
            ext2 = jnp.concatenate([dpn, next_dpn[:, cols]], axis=0)
            dpj[:, cols] = (_window_sums(ext2, w, False)[0:tm, :] - dpl).astype(BF16)
            carry[:, cols] = dpn[0:MAX_WINDOW, :]

        cos, sin = cos_ref[...], sin_ref[...]
        for h in range(RET_HEADS):
            cq = slice(pwid + h * GROUP, pwid + (h + 1) * GROUP)
            ck = slice(pwid + rwid + h * GROUP, pwid + rwid + (h + 1) * GROUP)
            cv = slice(pwid + 2 * rwid + h * GROUP, pwid + 2 * rwid + (h + 1) * GROUP)
            cg = slice(pwid + 3 * rwid + h * GROUP, pwid + 3 * rwid + (h + 1) * GROUP)
            ch = slice(h * GROUP, (h + 1) * GROUP)
            qr = _rope(proj_ref[:, cq], cos, sin)
            kr = _rope(proj_ref[:, ck], cos, sin) * (GROUP ** -0.5)
            vb = proj_ref[:, cv].astype(BF16)
            gv = proj_ref[:, cg]
            ov = o_ref[:, ch]
            ro = lax.rsqrt(jnp.mean(ov * ov, axis=-1, keepdims=True) + EPS)
            on = ov * ro
            rg = rg_ref[:, ch]
            db = dm[:, pwid + h * GROUP:pwid + (h + 1) * GROUP]
            sg = _sigmoid(gv)
            sl = gv * sg
            m_ref[:, pwid + h * GROUP:pwid + (h + 1) * GROUP] = (sl * (on * rg)).astype(BF16)
            dpj[:, cg] = (db * (on * rg) * (sg * (1.0 + gv * (1.0 - sg)))).astype(BF16)
            drg_ref[:, ch] += jnp.sum(db * sl * on, axis=0, keepdims=True)
            don = db * sl * rg
            do = (ro * (don - on * jnp.mean(don * on, axis=-1, keepdims=True))).astype(BF16)
            for n in reversed(range(nck)):
                rows = slice(n * GROUP, (n + 1) * GROUP)
                qc, kc, vc, dob = qr[rows], kr[rows], vb[rows], do[rows]
                qcb, kcb = qc.astype(BF16), kc.astype(BF16)
                qh = (qc * qh_ref[h]).astype(BF16)
                kt = (kc * kt_ref[h]).astype(BF16)
                rn = rs_ref[n, h].astype(BF16)
                dnext = dstate[h]
                dnb = dnext.astype(BF16)
                dec = dec_ref[h]
                p = (_dot(qcb, kcb, NT) * dec).astype(BF16)
                ds = (_dot(dob, vc, NT) * dec).astype(BF16)
                dv = _dot(p, dob, TN) + _dot(kt, dnb, NN)
                dq = _dot(ds, kcb, NN) + _dot(dob, rn, NT) * qh_ref[h]
                dk = _dot(ds, qcb, TN) + _dot(vc, dnb, NT) * kt_ref[h]
                dstate[h] = cd[h] * dnext + _dot(qh, dob, TN)
                dpj[rows, cq] = _rope_bwd(dq, cos[rows], sin[rows]).astype(BF16)
                dpj[rows, ck] = _rope_bwd(dk * (GROUP ** -0.5), cos[rows], sin[rows]).astype(BF16)
                dpj[rows, cv] = dv.astype(BF16)

        dproj_ref[...] = dpj[...]
        du = _dot(dpj[...], wins[...], NN)
        dg_ref[...] += jnp.sum(du * uh, axis=0, keepdims=True)
        dn = du * g
        dh1_ref[...] = dh2v + r * (dn - uh * jnp.mean(dn * uh, axis=-1, keepdims=True))

    rev = lambda i: (nt - 1 - i, 0)
    tile = pl.BlockSpec((tm, d), rev)
    full = lambda shape: pl.BlockSpec(shape, lambda i: (0,) * len(shape))
    return _call(
        body,
        name="mix_bwd",
        grid=(nt,),
        in_specs=[
            tile, tile,
            pl.BlockSpec((tm, inw), rev),
            pl.BlockSpec((MAX_WINDOW, pwid), lambda i: (jnp.maximum((nt - 1 - i) * halo_per_tile - 1, 0), 0)),
            pl.BlockSpec((tm, rwid), rev),
            pl.BlockSpec((nck, RET_HEADS, GROUP, GROUP), lambda i: (nt - 1 - i, 0, 0, 0)),
            full((1, d)), pl.BlockSpec(memory_space=pl.ANY),
            full((N_POOL_GROUPS, GROUP, GROUP)), full((1, pwid)), full((1, rwid)),
            pl.BlockSpec((tm, GROUP), rev), pl.BlockSpec((tm, GROUP), rev),
            full((RET_HEADS, GROUP, GROUP)), full((RET_HEADS, GROUP, GROUP)), full((RET_HEADS, GROUP, GROUP)),
        ],
        out_specs=[
            tile, pl.BlockSpec((tm, inw), rev), tile, tile,
            full((N_POOL_GROUPS, GROUP, GROUP)), full((1, pwid)), full((1, rwid)), full((1, d)),
        ],
        out_shape=[
            jax.ShapeDtypeStruct((s, d), F32),
            jax.ShapeDtypeStruct((s, inw), BF16),
            jax.ShapeDtypeStruct((s, d), BF16),
            jax.ShapeDtypeStruct((s, d), BF16),
            jax.ShapeDtypeStruct((N_POOL_GROUPS, GROUP, GROUP), F32),
            jax.ShapeDtypeStruct((1, pwid), F32),
            jax.ShapeDtypeStruct((1, rwid), F32),
            jax.ShapeDtypeStruct((1, d), F32),
        ],
        scratch_shapes=[
            pltpu.VMEM((inw, d), BF16), pltpu.VMEM((d, d), BF16),
            pltpu.VMEM((RET_HEADS, GROUP, GROUP), F32), pltpu.VMEM((MAX_WINDOW, pwid), F32),
            pltpu.VMEM((tm, d), F32), pltpu.VMEM((tm, inw), BF16), pltpu.SemaphoreType.DMA((2 * NDEV,)),
        ],
        compiler_params=_seq(1),
    )(dh2, h1, proj, proj, o_saved, rsave, gain, wbuf, pool_w, pool_scale, ret_gain,
      consts["cos"], consts["sin"], consts["decay"], consts["ktail"], consts["qhead"])


def _adam(w, g, m, v):
    m = ADAM_B1 * m + (1.0 - ADAM_B1) * g
    v = ADAM_B2 * v + (1.0 - ADAM_B2) * jnp.square(g)
    m_hat = m / (1.0 - ADAM_B1 ** ADAM_STEP)
    v_hat = v / (1.0 - ADAM_B2 ** ADAM_STEP)
    delta = -ADAM_LR * (m_hat / (jnp.sqrt(v_hat) + ADAM_EPS) + ADAM_WD * w)
    return delta, m, v


def _sum_partials(parts):
    _, rows, d = parts.shape
    tr = _row_tile(rows, 256)

    def body(p_ref, o_ref):
        acc = p_ref[0].astype(F32)
        for q in range(1, NDEV):
            acc = acc + p_ref[q].astype(F32)
        o_ref[...] = acc

    return _call(
        body,
        name="grad_sum",
        grid=(rows // tr,),
        in_specs=[pl.BlockSpec((NDEV, tr, d), lambda i: (0, i, 0))],
        out_specs=pl.BlockSpec((tr, d), lambda i: (i, 0)),
        out_shape=jax.ShapeDtypeStruct((rows, d), F32),
        compiler_params=_seq(1),
    )(parts)


def _adamw_big(w, g, m, v, name):
    rows, cols = w.shape
    tr = _row_tile(rows, 512)

    def body(w_ref, g_ref, m_ref, v_ref, d_ref, nm_ref, nv_ref):
        d_ref[...], nm_ref[...], nv_ref[...] = _adam(w_ref[...], g_ref[...], m_ref[...], v_ref[...])

    spec = pl.BlockSpec((tr, cols), lambda i: (i, 0))
    return _call(
        body,
        name=name,
        grid=(rows // tr,),
        in_specs=[spec] * 4,
        out_specs=[spec] * 3,
        out_shape=[jax.ShapeDtypeStruct((rows, cols), F32)] * 3,
        compiler_params=_seq(1),
    )(w, g, m, v)


def _adamw_small(stats_all, pw_all, ws, ms, vs, d, pwid):
    nsmall = len(ws)

    def body(*refs):
        st_ref, pwa_ref = refs[0], refs[1]
        w_refs = refs[2:2 + nsmall]
        m_refs = refs[2 + nsmall:2 + 2 * nsmall]
        v_refs = refs[2 + 2 * nsmall:2 + 3 * nsmall]
        outs = refs[2 + 3 * nsmall:]
        st = st_ref[0]
        pwg = pwa_ref[0]
        for q in range(1, NDEV):
            st = st + st_ref[q]
            pwg = pwg + pwa_ref[q]
        grads = [st[0:1, :], st[1:2, :], st[2:3, :], st[3:4, :], st[4:5, 0:pwid], st[4:5, pwid:2 * pwid], pwg]
        outs[0][...] = jnp.zeros((1, GROUP), F32) + jnp.sum(st[5:6, :])
        for j in range(nsmall):
            delta, nm, nv = _adam(w_refs[j][...], grads[j], m_refs[j][...], v_refs[j][...])
            outs[1 + 4 * j][...] = grads[j]
            outs[2 + 4 * j][...] = delta
            outs[3 + 4 * j][...] = nm
            outs[4 + 4 * j][...] = nv

    out_shape = [jax.ShapeDtypeStruct((1, GROUP), F32)]
    for w in ws:
        out_shape += [jax.ShapeDtypeStruct(w.shape, F32)] * 4
    return _call(body, name="adamw_small", out_shape=out_shape, compiler_params=_params())(
        stats_all, pw_all, *ws, *ms, *vs
    )


def kernel(x, ffn1_norm, ffn1_w1, ffn1_w3, ffn1_w2, mix_norm, w_in, pool_w, pool_scale, ret_norm, w_out, ffn2_norm, ffn2_w1, ffn2_w3, ffn2_w2, final_norm, loss_target, m_ffn1_norm, m_ffn1_w1, m_ffn1_w3, m_ffn1_w2, m_mix_norm, m_w_in, m_pool_w, m_pool_scale, m_ret_norm, m_w_out, m_ffn2_norm, m_ffn2_w1, m_ffn2_w3, m_ffn2_w2, m_final_norm, v_ffn1_norm, v_ffn1_w1, v_ffn1_w3, v_ffn1_w2, v_mix_norm, v_w_in, v_pool_w, v_pool_scale, v_ret_norm, v_w_out, v_ffn2_norm, v_ffn2_w1, v_ffn2_w3, v_ffn2_w2, v_final_norm):
    s, d = x.shape[1], x.shape[2]
    fs = ffn1_w1.shape[2]
    ffn = fs * NDEV
    ins = w_in.shape[2]
    outs_rows = w_out.shape[1]
    pwid = pool_scale.shape[1]
    xs, tgt = x[0], loss_target[0]

    transposed = (True, True, False, True, False, True, True, False)
    big = (ffn1_w1, ffn1_w3, ffn1_w2, w_in, w_out, ffn2_w1, ffn2_w3, ffn2_w2)
    big_m = (m_ffn1_w1, m_ffn1_w3, m_ffn1_w2, m_w_in, m_w_out, m_ffn2_w1, m_ffn2_w3, m_ffn2_w2)
    big_v = (v_ffn1_w1, v_ffn1_w3, v_ffn1_w2, v_w_in, v_w_out, v_ffn2_w1, v_ffn2_w3, v_ffn2_w2)
    shard_rows = (fs, fs, fs, ins, outs_rows, fs, fs, fs)
    offs = [0]
    for rws in shard_rows:
        offs.append(offs[-1] + rws)
    off_f1, off_in, off_out, off_f2 = offs[0], offs[3], offs[4], offs[5]

    wsend = jnp.concatenate([w[0].T if t else w[0] for w, t in zip(big, transposed)], axis=0).astype(BF16)
    wbuf = _all_gather(wsend)

    consts = _mix_constants(s)
    pw3 = pool_w[0]
    fnorm = final_norm.reshape(1, d)

    h1, a1, b1 = _ffn_fwd(xs, ffn1_norm, wbuf, off_f1, ffn)
    h2, proj, o_saved, rsave = _mix_fwd(h1, mix_norm, wbuf, off_in, off_out, pw3, pool_scale, ret_norm, consts)
    dh3, a2, b2, dgf, loss_cols = _ffn_fwd(h2, ffn2_norm, wbuf, off_f2, ffn, head=(fnorm, tgt))

    dh2, da2, db2, hm2, n2, dg2 = _ffn_bwd(dh3, h2, a2, b2, ffn2_norm, wbuf, off_f2, ffn, "ffn2_bwd")
    gw1_2 = _wgrad(da2, n2, 1.0, "ffn2_w1_grad")
    gw3_2 = _wgrad(db2, n2, 1.0, "ffn2_w3_grad")
    gw2_2 = _wgrad(hm2, dh3, 0.5, "ffn2_w2_grad")

    dh1, dproj, u, mm, dpw, dps, drg, dgm = _mix_bwd(
        dh2, h1, proj, o_saved, rsave, mix_norm, wbuf, off_in, off_out, pw3, pool_scale, ret_norm, consts
    )
    gwin = _wgrad(dproj, u, 1.0, "w_in_grad")
    gwout = _wgrad(mm, dh2, 1.0, "w_out_grad")

    dx, da1, db1, hm1, n1, dg1 = _ffn_bwd(dh1, xs, a1, b1, ffn1_norm, wbuf, off_f1, ffn, "ffn1_bwd")
    gw1_1 = _wgrad(da1, n1, 1.0, "ffn1_w1_grad")
    gw3_1 = _wgrad(db1, n1, 1.0, "ffn1_w3_grad")
    gw2_1 = _wgrad(hm1, dh1, 0.5, "ffn1_w2_grad")

    gparts = (gw1_1, gw3_1, gw2_1, gwin, gwout, gw1_2, gw3_2, gw2_2)
    gbuf = jnp.concatenate([g.reshape(NDEV, rws, d) for g, rws in zip(gparts, shard_rows)], axis=1)
    stats = jnp.concatenate(
        [dg1, dgm, dg2, dgf, jnp.concatenate([dps, drg], axis=1), loss_cols, jnp.zeros((2, d), F32)], axis=0
    )
    parts, stats_all, pw_all = _grad_exchange(gbuf, stats, dpw.reshape(N_POOL_GROUPS * GROUP, GROUP))

    gsum = _sum_partials(parts)
    big_out = []
    for j, (w, m, v, t) in enumerate(zip(big, big_m, big_v, transposed)):
        g = gsum[offs[j]:offs[j + 1]]
        g = g.T if t else g
        delta, nm, nv = _adamw_big(w[0], g, m[0], v[0], "adamw_%d" % j)
        big_out.append([a[None] for a in (g, delta, nm, nv)])

    small_w = (ffn1_norm, mix_norm, ffn2_norm, fnorm, pool_scale, ret_norm, pw3.reshape(-1, GROUP))
    small_m = (m_ffn1_norm, m_mix_norm, m_ffn2_norm, m_final_norm.reshape(1, d), m_pool_scale, m_ret_norm, m_pool_w.reshape(-1, GROUP))
    small_v = (v_ffn1_norm, v_mix_norm, v_ffn2_norm, v_final_norm.reshape(1, d), v_pool_scale, v_ret_norm, v_pool_w.reshape(-1, GROUP))
    res = _adamw_small(stats_all, pw_all, small_w, small_m, small_v, d, pwid)
    loss = res[0][0, 0]
    small_out = [list(res[1 + 4 * j:5 + 4 * j]) for j in range(len(small_w))]
    small_out[3] = [a.reshape(d) for a in small_out[3]]
    small_out[6] = [a.reshape(pool_w.shape) for a in small_out[6]]

    order = [small_out[0], big_out[0], big_out[1], big_out[2], small_out[1], big_out[3], small_out[6], small_out[4],
             small_out[5], big_out[4], small_out[2], big_out[5], big_out[6], big_out[7], small_out[3]]
    result = [loss, dx[None]]
    for kind in range(4):
        result += [t[kind] for t in order]
    return tuple(result)
```

```python
import numpy as np
import jax
import jax.numpy as jnp
from jax import lax
from jax.experimental import pallas as pl
from jax.experimental.pallas import tpu as pltpu

F32 = jnp.float32
BF16 = jnp.bfloat16

NDEV = 8
NCHIP = 4
EPS = 1e-6
N_POOL_GROUPS = 4
POOL_WINDOWS = (2, 4, 8, 16)
MAX_WINDOW = 16
GROUP = 128
RET_HEADS = 4
ROPE_BASE = 10000.0
ADAM_LR = 0.001
ADAM_B1 = 0.9
ADAM_B2 = 0.999
ADAM_EPS = 1e-08
ADAM_WD = 0.01
ADAM_STEP = 10

VMEM_LIMIT = 56 * 1024 * 1024
FFN_CHUNK = 256

NT = (((1,), (1,)), ((), ()))
NN = (((1,), (0,)), ((), ()))
TN = (((0,), (0,)), ((), ()))

ANY = pl.BlockSpec(memory_space=pl.ANY)


def _dot(a, b, dims):
    return lax.dot_general(a, b, dims, preferred_element_type=F32)


def _call(body, **kw):
    return pl.pallas_call(body, **kw)


def _params(**kw):
    return pltpu.CompilerParams(vmem_limit_bytes=VMEM_LIMIT, **kw)


def _seq(n):
    return _params(dimension_semantics=("arbitrary",) * n)


def _peer(k):
    x, y, c = lax.axis_index("x"), lax.axis_index("y"), lax.axis_index("c")
    return (1 - x if k & 4 else x, 1 - y if k & 2 else y, 1 - c if k & 1 else c)


def _flat(pos):
    return 4 * pos[0] + 2 * pos[1] + pos[2]


def _chip(pos):
    return 2 * pos[0] + pos[1]


def _row_tile(rows, cap):
    return max(t for t in range(16, min(rows, cap) + 1, 16) if rows % t == 0)


def _load_weights(parts, sems):
    copies = []
    for buf, off, rows, dst in parts:
        for p in range(NDEV):
            cp = pltpu.make_async_copy(
                buf.at[p, pl.ds(off, rows), :], dst.at[pl.ds(p * rows, rows), :], sems.at[len(copies)]
            )
            cp.start()
            copies.append(cp)
    return copies


def _sigmoid(a):
    return 1.0 / (1.0 + jnp.exp(-a))


def _remote(src, dst, send_sem, recv_sem, to):
    return pltpu.make_async_remote_copy(
        src_ref=src, dst_ref=dst, send_sem=send_sem, recv_sem=recv_sem, device_id=to, device_id_type=pl.DeviceIdType.MESH
    )


class _Gather:
    chips = (4, 2, 6)

    def __init__(self, shards):
        n = len(shards)
        self.operands = list(shards)
        self.out_shape = [jax.ShapeDtypeStruct((NDEV,) + a.shape, a.dtype) for a in shards]
        self.sems = [pltpu.SemaphoreType.DMA((7 * n,)), pltpu.SemaphoreType.DMA((7 * n,)), pltpu.SemaphoreType.DMA((n,))]

    def _copy(self, t, k, block, to, ins, outs, sems, own=False):
        dst = outs[t].at[_flat(block)]
        return _remote(ins[t] if own else dst, dst, sems[0].at[7 * t + k], sems[1].at[7 * t + k], to)

    def begin(self, ins, outs, sems):
        me, sibling = _peer(0), _peer(1)
        for t in range(len(ins)):
            pltpu.make_async_copy(ins[t], outs[t].at[_flat(me)], sems[2].at[t]).start()
            self._copy(t, 0, me, sibling, ins, outs, sems, own=True).start()
            for j, k in enumerate(self.chips):
                self._copy(t, 1 + j, me, _peer(k), ins, outs, sems, own=True).start()

    def mid(self, ins, outs, sems):
        me, sibling = _peer(0), _peer(1)
        for t in range(len(ins)):
            for j, k in enumerate(self.chips):
                self._copy(t, 1 + j, _peer(k), me, ins, outs, sems).wait_recv()
                self._copy(t, 4 + j, _peer(k), sibling, ins, outs, sems).start()

    def end(self, ins, outs, sems):
        me, sibling = _peer(0), _peer(1)
        for t in range(len(ins)):
            self._copy(t, 0, sibling, me, ins, outs, sems).wait_recv()
            for j, k in enumerate(self.chips):
                self._copy(t, 4 + j, _peer(k ^ 1), me, ins, outs, sems).wait_recv()
            for k in range(7):
                self._copy(t, k, me, me, ins, outs, sems).wait_send()
            pltpu.make_async_copy(ins[t], outs[t].at[_flat(me)], sems[2].at[t]).wait()


class _GatherDirect:
    def __init__(self, arrays):
        n = len(arrays)
        self.operands = list(arrays)
        self.out_shape = [jax.ShapeDtypeStruct((NDEV,) + a.shape, a.dtype) for a in arrays]
        self.sems = [pltpu.SemaphoreType.DMA((7 * n,)), pltpu.SemaphoreType.DMA((7 * n,)), pltpu.SemaphoreType.DMA((n,))]

    def begin(self, ins, outs, sems):
        mine = _flat(_peer(0))
        for t in range(len(ins)):
            pltpu.make_async_copy(ins[t], outs[t].at[mine], sems[2].at[t]).start()
            for k in range(1, NDEV):
                _remote(ins[t], outs[t].at[mine], sems[0].at[7 * t + k - 1], sems[1].at[7 * t + k - 1], _peer(k)).start()

    def mid(self, ins, outs, sems):
        pass

    def end(self, ins, outs, sems):
        mine = _flat(_peer(0))
        for t in range(len(ins)):
            for k in range(1, NDEV):
                cp = _remote(ins[t], outs[t].at[_flat(_peer(k))], sems[0].at[7 * t + k - 1], sems[1].at[7 * t + k - 1], _peer(k))
                cp.wait_recv()
                cp.wait_send()
            pltpu.make_async_copy(ins[t], outs[t].at[mine], sems[2].at[t]).wait()


class _SiblingSwap:
    def __init__(self, grads):
        n = len(grads)
        self.operands = list(grads)
        self.out_shape = []
        for g in grads:
            self.out_shape += [jax.ShapeDtypeStruct((NCHIP,) + g.shape[2:], g.dtype)] * 2
        self.sems = [pltpu.SemaphoreType.DMA((n,)), pltpu.SemaphoreType.DMA((n,)), pltpu.SemaphoreType.DMA((n,))]

    def _copies(self, t, ins, outs, sems):
        c = lax.axis_index("c")
        keep = pltpu.make_async_copy(ins[t].at[:, c], outs[2 * t], sems[2].at[t])
        give = _remote(ins[t].at[:, 1 - c], outs[2 * t + 1], sems[0].at[t], sems[1].at[t], _peer(1))
        return keep, give

    def begin(self, ins, outs, sems):
        for t in range(len(ins)):
            for cp in self._copies(t, ins, outs, sems):
                cp.start()

    def mid(self, ins, outs, sems):
        pass

    def end(self, ins, outs, sems):
        for t in range(len(ins)):
            keep, give = self._copies(t, ins, outs, sems)
            give.wait_recv()
            give.wait_send()
            keep.wait()


class _ChipScatter:
    def __init__(self, sums):
        n = len(sums)
        self.operands = list(sums)
        self.out_shape = [jax.ShapeDtypeStruct(a.shape, a.dtype) for a in sums]
        self.sems = [pltpu.SemaphoreType.DMA((3 * n,)), pltpu.SemaphoreType.DMA((3 * n,)), pltpu.SemaphoreType.DMA((n,))]

    def _keep(self, t, ins, outs, sems):
        mine = _chip(_peer(0))
        return pltpu.make_async_copy(ins[t].at[mine], outs[t].at[mine], sems[2].at[t])

    def begin(self, ins, outs, sems):
        mine = _chip(_peer(0))
        for t in range(len(ins)):
            self._keep(t, ins, outs, sems).start()
            for j, k in enumerate((4, 2, 6)):
                to = _peer(k)
                _remote(ins[t].at[_chip(to)], outs[t].at[mine], sems[0].at[3 * t + j], sems[1].at[3 * t + j], to).start()

    def mid(self, ins, outs, sems):
        pass

    def end(self, ins, outs, sems):
        for t in range(len(ins)):
            for j, k in enumerate((4, 2, 6)):
                frm = _chip(_peer(k))
                cp = _remote(ins[t].at[frm], outs[t].at[frm], sems[0].at[3 * t + j], sems[1].at[3 * t + j], _peer(k))
                cp.wait_recv()
                cp.wait_send()
            self._keep(t, ins, outs, sems).wait()


def _split_refs(refs, counts):
    out, at = [], 0
    for n in counts:
        out.append(refs[at:at + n])
        at += n
    return out


def _comm_call(carries, name):
    nin = [len(c.operands) for c in carries]
    nout = [len(c.out_shape) for c in carries]
    nsem = [len(c.sems) for c in carries]

    def body(*refs):
        ins, outs, sems = _split_refs(refs, (sum(nin), sum(nout), sum(nsem)))
        parts = list(zip(carries, _split_refs(ins, nin), _split_refs(outs, nout), _split_refs(sems, nsem)))
        for c, i, o, s in parts:
            c.begin(i, o, s)
        for c, i, o, s in parts:
            c.mid(i, o, s)
        for c, i, o, s in parts:
            c.end(i, o, s)

    res = _call(
        body,
        name=name,
        out_shape=[sh for c in carries for sh in c.out_shape],
        in_specs=[ANY] * sum(nin),
        out_specs=[ANY] * sum(nout),
        scratch_shapes=[sm for c in carries for sm in c.sems],
        compiler_params=pltpu.CompilerParams(has_side_effects=True),
    )(*[a for c in carries for a in c.operands])
    return _split_refs(list(res), nout)


def _grid_call(body, carries, *, name, steps, in_specs, out_specs, out_shape, scratch_shapes, args):
    ni, no, ns = len(in_specs), len(out_specs), len(scratch_shapes)
    nin = [len(c.operands) for c in carries]
    nout = [len(c.out_shape) for c in carries]
    nsem = [len(c.sems) for c in carries]
    mid_step = max(steps - 2, 0)

    def wrapped(*refs):
        ins, cins, outs, couts, scr, csems = _split_refs(refs, (ni, sum(nin), no, sum(nout), ns, sum(nsem)))
        if not carries:
            return body(*ins, *outs, *scr)
        parts = list(zip(carries, _split_refs(cins, nin), _split_refs(couts, nout), _split_refs(csems, nsem)))
        step = pl.program_id(0)

        @pl.when(step == 0)
        def _():
            for c, i, o, s in parts:
                c.begin(i, o, s)

        body(*ins, *outs, *scr)

        @pl.when(step == mid_step)
        def _():
            for c, i, o, s in parts:
                c.mid(i, o, s)

        @pl.when(step == steps - 1)
        def _():
            for c, i, o, s in parts:
                c.end(i, o, s)

    res = _call(
        wrapped,
        name=name,
        grid=(steps,),
        in_specs=list(in_specs) + [ANY] * sum(nin),
        out_specs=list(out_specs) + [ANY] * sum(nout),
        out_shape=list(out_shape) + [sh for c in carries for sh in c.out_shape],
        scratch_shapes=list(scratch_shapes) + [sm for c in carries for sm in c.sems],
        compiler_params=_seq(1),
    )(*args, *[a for c in carries for a in c.operands])
    res = list(res)
    return res[:no], _split_refs(res[no:], nout)


def _ffn_fwd(x, gain, weights, ffn, head=None, carries=()):
    s, d = x.shape
    fs = ffn // NDEV
    tm = min(512, s)
    nchunk = ffn // FFN_CHUNK
    offs = [off for _, off in weights]

    def body(*refs):
        if head is None:
            x_ref, g_ref, b1, b3, b2, h_ref, a_ref, b_ref, w1s, w3s, w2s, sems = refs
        else:
            x_ref, g_ref, b1, b3, b2, gf_ref, t_ref, h_ref, a_ref, b_ref, dgf_ref, loss_ref, w1s, w3s, w2s, sems = refs
        i = pl.program_id(0)

        @pl.when(i == 0)
        def _():
            for cp in _load_weights(((b1, offs[0], fs, w1s), (b3, offs[1], fs, w3s), (b2, offs[2], fs, w2s)), sems):
                cp.wait()
            if head is not None:
                dgf_ref[...] = jnp.zeros_like(dgf_ref)
                loss_ref[...] = jnp.zeros_like(loss_ref)

        xv = x_ref[...]
        r = lax.rsqrt(jnp.mean(xv * xv, axis=-1, keepdims=True) + EPS)
        n = (xv * r * g_ref[...]).astype(BF16)
        acc = jnp.zeros((tm, d), F32)
        for c in range(nchunk):
            cols = slice(c * FFN_CHUNK, (c + 1) * FFN_CHUNK)
            a = _dot(n, w1s[cols, :], NT)
            b = _dot(n, w3s[cols, :], NT)
            a_ref[:, cols] = a.astype(BF16)
            b_ref[:, cols] = b.astype(BF16)
            hm = (a * _sigmoid(a) * b).astype(BF16)
            acc = acc + _dot(hm, w2s[cols, :], NN)
        h = xv + 0.5 * acc
        if head is None:
            h_ref[...] = h
        else:
            rf = lax.rsqrt(jnp.mean(h * h, axis=-1, keepdims=True) + EPS)
            nh = h * rf
            gf = gf_ref[...]
            err = nh * gf - t_ref[...]
            loss_ref[...] += jnp.sum(err * err, axis=0, keepdims=True) * (0.5 / d)
            dy = err * (1.0 / d)
            dgf_ref[...] += jnp.sum(dy * nh, axis=0, keepdims=True)
            dn = dy * gf
            h_ref[...] = rf * (dn - nh * jnp.mean(dn * nh, axis=-1, keepdims=True))

    tile = pl.BlockSpec((tm, d), lambda i: (i, 0))
    row = pl.BlockSpec((1, d), lambda i: (0, 0))
    wide = pl.BlockSpec((tm, ffn), lambda i: (i, 0))
    in_specs = [tile, row, ANY, ANY, ANY]
    out_shape = [
        jax.ShapeDtypeStruct((s, d), F32),
        jax.ShapeDtypeStruct((s, ffn), BF16),
        jax.ShapeDtypeStruct((s, ffn), BF16),
    ]
    out_specs = [tile, wide, wide]
    args = [x, gain] + [buf for buf, _ in weights]
    if head is not None:
        in_specs += [row, tile]
        args += list(head)
        out_shape += [jax.ShapeDtypeStruct((1, d), F32)] * 2
        out_specs += [row, row]
    return _grid_call(
        body,
        carries,
        name="ffn_fwd_loss" if head is not None else "ffn_fwd",
        steps=s // tm,
        in_specs=in_specs,
        out_specs=out_specs,
        out_shape=out_shape,
        scratch_shapes=[pltpu.VMEM((ffn, d), BF16)] * 3 + [pltpu.SemaphoreType.DMA((3 * NDEV,))],
        args=args,
    )


def _ffn_bwd(dh, x, a, b, gain, weights, ffn, name, carries=()):
    s, d = x.shape
    fs = ffn // NDEV
    tm = min(256, s)
    nchunk = ffn // FFN_CHUNK
    offs = [off for _, off in weights]

    def body(dh_ref, x_ref, a_ref, b_ref, g_ref, b1, b3, b2, dx_ref, da_ref, db_ref, hm_ref, n_ref, dg_ref, w1s, w3s, w2s, sems):
        i = pl.program_id(0)

        @pl.when(i == 0)
        def _():
            for cp in _load_weights(((b1, offs[0], fs, w1s), (b3, offs[1], fs, w3s), (b2, offs[2], fs, w2s)), sems):
                cp.wait()
            dg_ref[...] = jnp.zeros_like(dg_ref)

        xv = x_ref[...]
        dhv = dh_ref[...]
        g = g_ref[...]
        r = lax.rsqrt(jnp.mean(xv * xv, axis=-1, keepdims=True) + EPS)
        nh = xv * r
        n_ref[...] = (nh * g).astype(BF16)
        dout = (0.5 * dhv).astype(BF16)
        dn = jnp.zeros((tm, d), F32)
        for c in range(nchunk):
            cols = slice(c * FFN_CHUNK, (c + 1) * FFN_CHUNK)
            av = a_ref[:, cols].astype(F32)
            bv = b_ref[:, cols].astype(F32)
            sg = _sigmoid(av)
            sl = av * sg
            hm_ref[:, cols] = (sl * bv).astype(BF16)
            dhm = _dot(dout, w2s[cols, :], NT)
            da = (dhm * bv * (sg * (1.0 + av * (1.0 - sg)))).astype(BF16)
            db = (dhm * sl).astype(BF16)
            da_ref[:, cols] = da
            db_ref[:, cols] = db
            dn = dn + _dot(da, w1s[cols, :], NN) + _dot(db, w3s[cols, :], NN)
        dg_ref[...] += jnp.sum(dn * nh, axis=0, keepdims=True)
        dnh = dn * g
        dx_ref[...] = dhv + r * (dnh - nh * jnp.mean(dnh * nh, axis=-1, keepdims=True))

    tile = pl.BlockSpec((tm, d), lambda i: (i, 0))
    row = pl.BlockSpec((1, d), lambda i: (0, 0))
    wide = pl.BlockSpec((tm, ffn), lambda i: (i, 0))
    return _grid_call(
        body,
        carries,
        name=name,
        steps=s // tm,
        in_specs=[tile, tile, wide, wide, row, ANY, ANY, ANY],
        out_specs=[tile, wide, wide, wide, tile, row],
        out_shape=[
            jax.ShapeDtypeStruct((s, d), F32),
            jax.ShapeDtypeStruct((s, ffn), BF16),
            jax.ShapeDtypeStruct((s, ffn), BF16),
            jax.ShapeDtypeStruct((s, ffn), BF16),
            jax.ShapeDtypeStruct((s, d), BF16),
            jax.ShapeDtypeStruct((1, d), F32),
        ],
        scratch_shapes=[pltpu.VMEM((ffn, d), BF16)] * 3 + [pltpu.SemaphoreType.DMA((3 * NDEV,))],
        args=[dh, x, a, b, gain] + [buf for buf, _ in weights],
    )


def _wgrad(lhs, rhs, scale, name, into=None, slot=0, nslots=1):
    s, m = lhs.shape
    n = rhs.shape[1]
    rs = m // NDEV
    tk = min(512, s)
    steps = s // tk

    def body(*refs):
        l_ref, r_ref = refs[0], refs[1]
        o_ref, acc = refs[-2], refs[-1]
        k = pl.program_id(0)

        @pl.when(k == 0)
        def _():
            acc[...] = jnp.zeros_like(acc)

        acc[...] += _dot(l_ref[...], r_ref[...].astype(BF16), TN)

        @pl.when(k == steps - 1)
        def _():
            for p in range(NDEV):
                o_ref[p] = (acc[p * rs:(p + 1) * rs, :] * scale).astype(BF16)

    in_specs = [pl.BlockSpec((tk, m), lambda k: (k, 0)), pl.BlockSpec((tk, n), lambda k: (k, 0))]
    args = [lhs, rhs]
    aliases = {}
    if into is not None:
        in_specs.append(ANY)
        args.append(into)
        aliases = {2: 0}
    return _call(
        body,
        name=name,
        grid=(steps,),
        in_specs=in_specs,
        out_specs=pl.BlockSpec((NDEV, rs, n), lambda k: (0, slot, 0)),
        out_shape=jax.ShapeDtypeStruct((NDEV, nslots * rs, n), BF16),
        scratch_shapes=[pltpu.VMEM((m, n), F32)],
        input_output_aliases=aliases,
        compiler_params=_seq(1),
    )(*args)


def _mix_constants(s):
    c = GROUP
    lg = np.log1p(-np.exp2(-5.0 - np.arange(RET_HEADS, dtype=np.float32))).astype(np.float32)
    pos = np.arange(c, dtype=np.float32)
    rel = pos[:, None] - pos[None, :]
    decay = np.where(rel[None] >= 0, np.exp(lg[:, None, None] * np.maximum(rel, 0.0)[None]), 0.0).astype(np.float32)
    ktail = np.exp(lg[:, None] * (c - 1 - pos)[None, :]).astype(np.float32)
    qhead = np.exp(lg[:, None] * (pos + 1.0)[None, :]).astype(np.float32)
    chunk_decay = [float(v) for v in np.exp(lg * np.float32(c)).astype(np.float32)]
    ones = np.ones((1, 1, c), np.float32)
    inv_freq = (1.0 / (np.float32(ROPE_BASE) ** (np.arange(0, c, 2, dtype=np.float32) / np.float32(c)))).astype(np.float32)
    ang = (np.arange(s, dtype=np.float32)[:, None] * inv_freq[None, :]).astype(np.float32)
    cos, sin = np.cos(ang).astype(np.float32), np.sin(ang).astype(np.float32)
    return dict(
        decay=jnp.asarray(decay),
        ktail=jnp.asarray(ktail[:, :, None] * ones),
        qhead=jnp.asarray(qhead[:, :, None] * ones),
        chunk_decay=chunk_decay,
        cos=jnp.asarray(np.concatenate([cos, cos], axis=-1)),
        sin=jnp.asarray(np.concatenate([-sin, sin], axis=-1)),
    )


def _rope(t, cos, sin):
    return t * cos + pltpu.roll(t, GROUP // 2, axis=1) * sin


def _rope_bwd(dt, cos, sin):
    return dt * cos + pltpu.roll(dt * sin, GROUP // 2, axis=1)


def _window_sums(ext, w, forward):
    rows = ext.shape[0]
    acc, k = ext, 1
    while k < w:
        acc = acc + pltpu.roll(acc, k if forward else rows - k, axis=0)
        k *= 2
    return acc


def _pool_counts(tile, tm, w):
    t = lax.broadcasted_iota(jnp.int32, (tm, 1), 0) + tile * tm
    return jnp.minimum(t + 1, w).astype(F32)


def _mix_fwd(h1, gain, weights, pool_w, pool_scale, ret_gain, consts, carries=()):
    s, d = h1.shape
    pwid = N_POOL_GROUPS * GROUP
    rwid = RET_HEADS * GROUP
    inw = pwid + 4 * rwid
    tm = min(256, s)
    nck = tm // GROUP
    cd = consts["chunk_decay"]
    offs = [off for _, off in weights]

    def body(h_ref, g_ref, bin_, bout, pw_ref, ps_ref, rg_ref, cos_ref, sin_ref, dec_ref, kt_ref, qh_ref,
             h2_ref, proj_ref, o_ref, rs_ref, wins, wouts, state, carry, mbuf, sems):
        i = pl.program_id(0)

        @pl.when(i == 0)
        def _():
            for cp in _load_weights(((bin_, offs[0], inw // NDEV, wins), (bout, offs[1], d // NDEV, wouts)), sems):
                cp.wait()
            state[...] = jnp.zeros_like(state)
            carry[...] = jnp.zeros_like(carry)

        hv = h_ref[...]
        r = lax.rsqrt(jnp.mean(hv * hv, axis=-1, keepdims=True) + EPS)
        u = (hv * r * g_ref[...]).astype(BF16)
        proj_ref[...] = _dot(u, wins[...], NT)

        ext = jnp.concatenate([carry[...], proj_ref[:, 0:pwid]], axis=0)
        carry[...] = proj_ref[tm - MAX_WINDOW:tm, 0:pwid]
        for gi, w in enumerate(POOL_WINDOWS):
            cols = slice(gi * GROUP, (gi + 1) * GROUP)
            xg = ext[:, cols]
            ws = _window_sums(xg, w, True)[MAX_WINDOW:, :]
            pooled = ws / _pool_counts(i, tm, w) - xg[MAX_WINDOW:, :]
            z = _dot(pooled.astype(BF16), pw_ref[gi].astype(BF16), NN)
            mbuf[:, cols] = (z * ps_ref[:, cols]).astype(BF16)

        cos, sin = cos_ref[...], sin_ref[...]
        for h in range(RET_HEADS):
            cq = slice(pwid + h * GROUP, pwid + (h + 1) * GROUP)
            ck = slice(pwid + rwid + h * GROUP, pwid + rwid + (h + 1) * GROUP)
            cv = slice(pwid + 2 * rwid + h * GROUP, pwid + 2 * rwid + (h + 1) * GROUP)
            cg = slice(pwid + 3 * rwid + h * GROUP, pwid + 3 * rwid + (h + 1) * GROUP)
            ch = slice(h * GROUP, (h + 1) * GROUP)
            qr = _rope(proj_ref[:, cq], cos, sin)
            kr = _rope(proj_ref[:, ck], cos, sin) * (GROUP ** -0.5)
            vb = proj_ref[:, cv].astype(BF16)
            for n in range(nck):
                rows = slice(n * GROUP, (n + 1) * GROUP)
                qc, kc, vc = qr[rows], kr[rows], vb[rows]
                rb = state[h]
                rs_ref[n, h] = rb
                p = (_dot(qc.astype(BF16), kc.astype(BF16), NT) * dec_ref[h]).astype(BF16)
                o = _dot(p, vc, NN) + _dot((qc * qh_ref[h]).astype(BF16), rb.astype(BF16), NN)
                state[h] = cd[h] * rb + _dot((kc * kt_ref[h]).astype(BF16), vc, TN)
                o_ref[rows, ch] = o
                on = o * lax.rsqrt(jnp.mean(o * o, axis=-1, keepdims=True) + EPS)
                gv = proj_ref[rows, cg]
                mbuf[rows, pwid + h * GROUP:pwid + (h + 1) * GROUP] = (
                    gv * _sigmoid(gv) * (on * rg_ref[:, ch])
                ).astype(BF16)
        h2_ref[...] = hv + _dot(mbuf[...], wouts[...], NN)

    tile = pl.BlockSpec((tm, d), lambda i: (i, 0))
    full = lambda shape: pl.BlockSpec(shape, lambda i: (0,) * len(shape))
    return _grid_call(
        body,
        carries,
        name="mix_fwd",
        steps=s // tm,
        in_specs=[
            tile, full((1, d)), ANY, ANY,
            full((N_POOL_GROUPS, GROUP, GROUP)), full((1, pwid)), full((1, rwid)),
            pl.BlockSpec((tm, GROUP), lambda i: (i, 0)), pl.BlockSpec((tm, GROUP), lambda i: (i, 0)),
            full((RET_HEADS, GROUP, GROUP)), full((RET_HEADS, GROUP, GROUP)), full((RET_HEADS, GROUP, GROUP)),
        ],
        out_specs=[
            tile,
            pl.BlockSpec((tm, inw), lambda i: (i, 0)),
            pl.BlockSpec((tm, rwid), lambda i: (i, 0)),
            pl.BlockSpec((nck, RET_HEADS, GROUP, GROUP), lambda i: (i, 0, 0, 0)),
        ],
        out_shape=[
            jax.ShapeDtypeStruct((s, d), F32),
            jax.ShapeDtypeStruct((s, inw), F32),
            jax.ShapeDtypeStruct((s, rwid), F32),
            jax.ShapeDtypeStruct((s // GROUP, RET_HEADS, GROUP, GROUP), F32),
        ],
        scratch_shapes=[
            pltpu.VMEM((inw, d), BF16), pltpu.VMEM((d, d), BF16),
            pltpu.VMEM((RET_HEADS, GROUP, GROUP), F32), pltpu.VMEM((MAX_WINDOW, pwid), F32),
            pltpu.VMEM((tm, d), BF16), pltpu.SemaphoreType.DMA((2 * NDEV,)),
        ],
        args=[h1, gain, weights[0][0], weights[1][0], pool_w, pool_scale, ret_gain,
              consts["cos"], consts["sin"], consts["decay"], consts["ktail"], consts["qhead"]],
    )


def _mix_bwd(dh2, h1, proj, o_saved, rsave, gain, weights, pool_w, pool_scale, ret_gain, consts, carries=()):
    s, d = h1.shape
    pwid = N_POOL_GROUPS * GROUP
    rwid = RET_HEADS * GROUP
    inw = pwid + 4 * rwid
    tm = min(256, s)
    nck = tm // GROUP
    nt = s // tm
    cd = consts["chunk_decay"]
    halo_per_tile = tm // MAX_WINDOW
    offs = [off for _, off in weights]

    def body(dh2_ref, h_ref, proj_ref, halo_ref, o_ref, rs_ref, g_ref, bin_, bout, pw_ref, ps_ref, rg_ref,
             cos_ref, sin_ref, dec_ref, kt_ref, qh_ref,
             dh1_ref, dproj_ref, u_ref, m_ref, dpw_ref, dps_ref, drg_ref, dg_ref,
             wins, wouts, dstate, carry, dm, dpj, sems):
        i = pl.program_id(0)
        tile = nt - 1 - i

        @pl.when(i == 0)
        def _():
            for cp in _load_weights(((bin_, offs[0], inw // NDEV, wins), (bout, offs[1], d // NDEV, wouts)), sems):
                cp.wait()
            dstate[...] = jnp.zeros_like(dstate)
            carry[...] = jnp.zeros_like(carry)
            for ref in (dpw_ref, dps_ref, drg_ref, dg_ref):
                ref[...] = jnp.zeros_like(ref)

        dh2v = dh2_ref[...]
        dm[...] = _dot(dh2v.astype(BF16), wouts[...], NT)
        hv = h_ref[...]
        g = g_ref[...]
        r = lax.rsqrt(jnp.mean(hv * hv, axis=-1, keepdims=True) + EPS)
        uh = hv * r
        u_ref[...] = (uh * g).astype(BF16)

        halo = jnp.where(tile == 0, 0.0, halo_ref[...])
        ext = jnp.concatenate([halo, proj_ref[:, 0:pwid]], axis=0)
        next_dpn = carry[...]
        for gi, w in enumerate(POOL_WINDOWS):
            cols = slice(gi * GROUP, (gi + 1) * GROUP)
            xg = ext[:, cols]
            cnt = _pool_counts(tile, tm, w)
            pooled = (_window_sums(xg, w, True)[MAX_WINDOW:, :] / cnt - xg[MAX_WINDOW:, :]).astype(BF16)
            pwb = pw_ref[gi].astype(BF16)
            z = _dot(pooled, pwb, NN)
            scale = ps_ref[:, cols]
            m_ref[:, cols] = (z * scale).astype(BF16)
            da = dm[:, cols]
            dps_ref[:, cols] += jnp.sum(da * z, axis=0, keepdims=True)
            dz = (da * scale).astype(BF16)
            dpw_ref[gi] += _dot(pooled, dz, TN)
            dpl = _dot(dz, pwb, NT)
            dpn = dpl / cnt
            ext2 = jnp.concatenate([dpn, next_dpn[:, cols]], axis=0)
            dpj[:, cols] = (_window_sums(ext2, w, False)[0:tm, :] - dpl).astype(BF16)
            carry[:, cols] = dpn[0:MAX_WINDOW, :]

        cos, sin = cos_ref[...], sin_ref[...]
        for h in range(RET_HEADS):
            cq = slice(pwid + h * GROUP, pwid + (h + 1) * GROUP)
            ck = slice(pwid + rwid + h * GROUP, pwid + rwid + (h + 1) * GROUP)
            cv = slice(pwid + 2 * rwid + h * GROUP, pwid + 2 * rwid + (h + 1) * GROUP)
            cg = slice(pwid + 3 * rwid + h * GROUP, pwid + 3 * rwid + (h + 1) * GROUP)
            ch = slice(h * GROUP, (h + 1) * GROUP)
            qr = _rope(proj_ref[:, cq], cos, sin)
            kr = _rope(proj_ref[:, ck], cos, sin) * (GROUP ** -0.5)
            vb = proj_ref[:, cv].astype(BF16)
            gv = proj_ref[:, cg]
            ov = o_ref[:, ch]
            ro = lax.rsqrt(jnp.mean(ov * ov, axis=-1, keepdims=True) + EPS)
            on = ov * ro
            rg = rg_ref[:, ch]
            db = dm[:, pwid + h * GROUP:pwid + (h + 1) * GROUP]
            sg = _sigmoid(gv)
            sl = gv * sg
            m_ref[:, pwid + h * GROUP:pwid + (h + 1) * GROUP] = (sl * (on * rg)).astype(BF16)
            dpj[:, cg] = (db * (on * rg) * (sg * (1.0 + gv * (1.0 - sg)))).astype(BF16)
            drg_ref[:, ch] += jnp.sum(db * sl * on, axis=0, keepdims=True)
            don = db * sl * rg
            do = (ro * (don - on * jnp.mean(don * on, axis=-1, keepdims=True))).astype(BF16)
            for n in reversed(range(nck)):
                rows = slice(n * GROUP, (n + 1) * GROUP)
                qc, kc, vc, dob = qr[rows], kr[rows], vb[rows], do[rows]
                qcb, kcb = qc.astype(BF16), kc.astype(BF16)
                qh = (qc * qh_ref[h]).astype(BF16)
                kt = (kc * kt_ref[h]).astype(BF16)
                rn = rs_ref[n, h].astype(BF16)
                dnext = dstate[h]
                dnb = dnext.astype(BF16)
                dec = dec_ref[h]
                p = (_dot(qcb, kcb, NT) * dec).astype(BF16)
                ds = (_dot(dob, vc, NT) * dec).astype(BF16)
                dv = _dot(p, dob, TN) + _dot(kt, dnb, NN)
                dq = _dot(ds, kcb, NN) + _dot(dob, rn, NT) * qh_ref[h]
                dk = _dot(ds, qcb, TN) + _dot(vc, dnb, NT) * kt_ref[h]
                dstate[h] = cd[h] * dnext + _dot(qh, dob, TN)
                dpj[rows, cq] = _rope_bwd(dq, cos[rows], sin[rows]).astype(BF16)
                dpj[rows, ck] = _rope_bwd(dk * (GROUP ** -0.5), cos[rows], sin[rows]).astype(BF16)
                dpj[rows, cv] = dv.astype(BF16)

        dproj_ref[...] = dpj[...]
        du = _dot(dpj[...], wins[...], NN)
        dg_ref[...] += jnp.sum(du * uh, axis=0, keepdims=True)
        dn = du * g
        dh1_ref[...] = dh2v + r * (dn - uh * jnp.mean(dn * uh, axis=-1, keepdims=True))

    rev = lambda i: (nt - 1 - i, 0)
    tile = pl.BlockSpec((tm, d), rev)
    full = lambda shape: pl.BlockSpec(shape, lambda i: (0,) * len(shape))
    return _grid_call(
        body,
        carries,
        name="mix_bwd",
        steps=nt,
        in_specs=[
            tile, tile,
            pl.BlockSpec((tm, inw), rev),
            pl.BlockSpec((MAX_WINDOW, pwid), lambda i: (jnp.maximum((nt - 1 - i) * halo_per_tile - 1, 0), 0)),
            pl.BlockSpec((tm, rwid), rev),
            pl.BlockSpec((nck, RET_HEADS, GROUP, GROUP), lambda i: (nt - 1 - i, 0, 0, 0)),
            full((1, d)), ANY, ANY,
            full((N_POOL_GROUPS, GROUP, GROUP)), full((1, pwid)), full((1, rwid)),
            pl.BlockSpec((tm, GROUP), rev), pl.BlockSpec((tm, GROUP), rev),
            full((RET_HEADS, GROUP, GROUP)), full((RET_HEADS, GROUP, GROUP)), full((RET_HEADS, GROUP, GROUP)),
        ],
        out_specs=[
            tile, pl.BlockSpec((tm, inw), rev), tile, tile,
            full((N_POOL_GROUPS, GROUP, GROUP)), full((1, pwid)), full((1, rwid)), full((1, d)),
        ],
        out_shape=[
            jax.ShapeDtypeStruct((s, d), F32),
            jax.ShapeDtypeStruct((s, inw), BF16),
            jax.ShapeDtypeStruct((s, d), BF16),
            jax.ShapeDtypeStruct((s, d), BF16),
            jax.ShapeDtypeStruct((N_POOL_GROUPS, GROUP, GROUP), F32),
            jax.ShapeDtypeStruct((1, pwid), F32),
            jax.ShapeDtypeStruct((1, rwid), F32),
            jax.ShapeDtypeStruct((1, d), F32),
        ],
        scratch_shapes=[
            pltpu.VMEM((inw, d), BF16), pltpu.VMEM((d, d), BF16),
            pltpu.VMEM((RET_HEADS, GROUP, GROUP), F32), pltpu.VMEM((MAX_WINDOW, pwid), F32),
            pltpu.VMEM((tm, d), F32), pltpu.VMEM((tm, inw), BF16), pltpu.SemaphoreType.DMA((2 * NDEV,)),
        ],
        args=[dh2, h1, proj, proj, o_saved, rsave, gain, weights[0][0], weights[1][0], pool_w, pool_scale, ret_gain,
              consts["cos"], consts["sin"], consts["decay"], consts["ktail"], consts["qhead"]],
    )


def _adam(w, g, m, v):
    m = ADAM_B1 * m + (1.0 - ADAM_B1) * g
    v = ADAM_B2 * v + (1.0 - ADAM_B2) * jnp.square(g)
    m_hat = m / (1.0 - ADAM_B1 ** ADAM_STEP)
    v_hat = v / (1.0 - ADAM_B2 ** ADAM_STEP)
    delta = -ADAM_LR * (m_hat / (jnp.sqrt(v_hat) + ADAM_EPS) + ADAM_WD * w)
    return delta, m, v


def _pair_sum(mine, theirs, name):
    n, rows, d = mine.shape
    tr = _row_tile(rows, 512)

    def body(a_ref, b_ref, o_ref):
        o_ref[...] = (a_ref[...].astype(F32) + b_ref[...].astype(F32)).astype(BF16)

    spec = pl.BlockSpec((1, tr, d), lambda j, i: (j, i, 0))
    return _call(
        body,
        name=name,
        grid=(n, rows // tr),
        in_specs=[spec, spec],
        out_specs=spec,
        out_shape=jax.ShapeDtypeStruct(mine.shape, BF16),
        compiler_params=_seq(2),
    )(mine, theirs)


def _chip_sum(parts, name):
    n, rows, d = parts.shape
    tr = _row_tile(rows, 512)

    def body(p_ref, o_ref):
        acc = p_ref[0].astype(F32)
        for q in range(1, n):
            acc = acc + p_ref[q].astype(F32)
        o_ref[...] = acc

    return _call(
        body,
        name=name,
        grid=(rows // tr,),
        in_specs=[pl.BlockSpec((n, tr, d), lambda i: (0, i, 0))],
        out_specs=pl.BlockSpec((tr, d), lambda i: (i, 0)),
        out_shape=jax.ShapeDtypeStruct((rows, d), F32),
        compiler_params=_seq(1),
    )(parts)


def _adamw_big(w, g, m, v, name):
    rows, cols = w.shape
    tr = _row_tile(rows, 512)

    def body(w_ref, g_ref, m_ref, v_ref, d_ref, nm_ref, nv_ref):
        d_ref[...], nm_ref[...], nv_ref[...] = _adam(w_ref[...], g_ref[...], m_ref[...], v_ref[...])

    spec = pl.BlockSpec((tr, cols), lambda i: (i, 0))
    return _call(
        body,
        name=name,
        grid=(rows // tr,),
        in_specs=[spec] * 4,
        out_specs=[spec] * 3,
        out_shape=[jax.ShapeDtypeStruct((rows, cols), F32)] * 3,
        compiler_params=_seq(1),
    )(w, g, m, v)


def _adamw_small(stats_all, pw_all, ws, ms, vs, d, pwid):
    nsmall = len(ws)

    def body(*refs):
        st_ref, pwa_ref = refs[0], refs[1]
        w_refs = refs[2:2 + nsmall]
        m_refs = refs[2 + nsmall:2 + 2 * nsmall]
        v_refs = refs[2 + 2 * nsmall:2 + 3 * nsmall]
        outs = refs[2 + 3 * nsmall:]
        st = st_ref[0]
        pwg = pwa_ref[0]
        for q in range(1, NDEV):
            st = st + st_ref[q]
            pwg = pwg + pwa_ref[q]
        grads = [st[0:1, :], st[1:2, :], st[2:3, :], st[3:4, :], st[4:5, 0:pwid], st[4:5, pwid:2 * pwid], pwg]
        outs[0][...] = jnp.zeros((1, GROUP), F32) + jnp.sum(st[5:6, :])
        for j in range(nsmall):
            delta, nm, nv = _adam(w_refs[j][...], grads[j], m_refs[j][...], v_refs[j][...])
            outs[1 + 4 * j][...] = grads[j]
            outs[2 + 4 * j][...] = delta
            outs[3 + 4 * j][...] = nm
            outs[4 + 4 * j][...] = nv

    out_shape = [jax.ShapeDtypeStruct((1, GROUP), F32)]
    for w in ws:
        out_shape += [jax.ShapeDtypeStruct(w.shape, F32)] * 4
    return _call(body, name="adamw_small", out_shape=out_shape, compiler_params=_params())(
        stats_all, pw_all, *ws, *ms, *vs
    )


def _by_core(g):
    return g.reshape(NCHIP, 2, g.shape[1], g.shape[2])


def kernel(x, ffn1_norm, ffn1_w1, ffn1_w3, ffn1_w2, mix_norm, w_in, pool_w, pool_scale, ret_norm, w_out, ffn2_norm, ffn2_w1, ffn2_w3, ffn2_w2, final_norm, loss_target, m_ffn1_norm, m_ffn1_w1, m_ffn1_w3, m_ffn1_w2, m_mix_norm, m_w_in, m_pool_w, m_pool_scale, m_ret_norm, m_w_out, m_ffn2_norm, m_ffn2_w1, m_ffn2_w3, m_ffn2_w2, m_final_norm, v_ffn1_norm, v_ffn1_w1, v_ffn1_w3, v_ffn1_w2, v_mix_norm, v_w_in, v_pool_w, v_pool_scale, v_ret_norm, v_w_out, v_ffn2_norm, v_ffn2_w1, v_ffn2_w3, v_ffn2_w2, v_final_norm):
    s, d = x.shape[1], x.shape[2]
    fs = ffn1_w1.shape[2]
    ffn = fs * NDEV
    ins = w_in.shape[2]
    pwid = pool_scale.shape[1]
    xs, tgt = x[0], loss_target[0]
    consts = _mix_constants(s)
    pw3 = pool_w[0]
    fnorm = final_norm.reshape(1, d)

    rows_of = lambda *ws: jnp.concatenate(ws, axis=0).astype(BF16)
    send_f1 = rows_of(ffn1_w1[0].T, ffn1_w3[0].T, ffn1_w2[0])
    send_mix = rows_of(w_in[0].T, w_out[0])
    send_f2a = rows_of(ffn2_w1[0].T)
    send_f2b = rows_of(ffn2_w3[0].T, ffn2_w2[0])

    ((buf_f1,),) = _comm_call([_Gather([send_f1])], "gather_ffn1")
    w_f1 = ((buf_f1, 0), (buf_f1, fs), (buf_f1, 2 * fs))
    (h1, a1, b1), ((buf_mix, buf_f2a),) = _ffn_fwd(xs, ffn1_norm, w_f1, ffn, carries=[_Gather([send_mix, send_f2a])])
    w_mix = ((buf_mix, 0), (buf_mix, ins))
    (h2, proj, o_saved, rsave), ((buf_f2b,),) = _mix_fwd(
        h1, mix_norm, w_mix, pw3, pool_scale, ret_norm, consts, carries=[_Gather([send_f2b])]
    )
    w_f2 = ((buf_f2a, 0), (buf_f2b, 0), (buf_f2b, fs))
    (dh3, a2, b2, dgf, loss_cols), _ = _ffn_fwd(h2, ffn2_norm, w_f2, ffn, head=(fnorm, tgt))

    (dh2, da2, db2, hm2, n2, dg2), _ = _ffn_bwd(dh3, h2, a2, b2, ffn2_norm, w_f2, ffn, "ffn2_bwd")
    g_f2 = _wgrad(da2, n2, 1.0, "ffn2_w1_grad", slot=0, nslots=3)
    g_f2 = _wgrad(db2, n2, 1.0, "ffn2_w3_grad", into=g_f2, slot=1, nslots=3)
    g_f2 = _wgrad(hm2, dh3, 0.5, "ffn2_w2_grad", into=g_f2, slot=2, nslots=3)
    ((mine, theirs),) = _comm_call([_SiblingSwap([_by_core(g_f2)])], "swap_ffn2")
    sum_f2 = _pair_sum(mine, theirs, "pair_sum_ffn2")

    (dh1, dproj, u, mm, dpw, dps, drg, dgm), ((parts_f2,),) = _mix_bwd(
        dh2, h1, proj, o_saved, rsave, mix_norm, w_mix, pw3, pool_scale, ret_norm, consts,
        carries=[_ChipScatter([sum_f2])],
    )
    g_in = _wgrad(dproj, u, 1.0, "w_in_grad")
    g_out = _wgrad(mm, dh2, 1.0, "w_out_grad")
    ((mine_in, theirs_in, mine_out, theirs_out),) = _comm_call([_SiblingSwap([_by_core(g_in), _by_core(g_out)])], "swap_mix")
    sum_in = _pair_sum(mine_in, theirs_in, "pair_sum_w_in")
    sum_out = _pair_sum(mine_out, theirs_out, "pair_sum_w_out")

    (dx, da1, db1, hm1, n1, dg1), ((parts_in, parts_out),) = _ffn_bwd(
        dh1, xs, a1, b1, ffn1_norm, w_f1, ffn, "ffn1_bwd", carries=[_ChipScatter([sum_in, sum_out])]
    )
    g_f1 = _wgrad(da1, n1, 1.0, "ffn1_w1_grad", slot=0, nslots=3)
    g_f1 = _wgrad(db1, n1, 1.0, "ffn1_w3_grad", into=g_f1, slot=1, nslots=3)
    g_f1 = _wgrad(hm1, dh1, 0.5, "ffn1_w2_grad", into=g_f1, slot=2, nslots=3)
    ((mine, theirs),) = _comm_call([_SiblingSwap([_by_core(g_f1)])], "swap_ffn1")
    sum_f1 = _pair_sum(mine, theirs, "pair_sum_ffn1")

    stats = jnp.concatenate(
        [dg1, dgm, dg2, dgf, jnp.concatenate([dps, drg], axis=1), loss_cols, jnp.zeros((2, d), F32)], axis=0
    )
    (parts_f1,), (stats_all, pw_all) = _comm_call(
        [_ChipScatter([sum_f1]), _GatherDirect([stats, dpw.reshape(N_POOL_GROUPS * GROUP, GROUP)])], "scatter_ffn1"
    )

    gs_f1 = _chip_sum(parts_f1, "chip_sum_ffn1")
    gs_in = _chip_sum(parts_in, "chip_sum_w_in")
    gs_out = _chip_sum(parts_out, "chip_sum_w_out")
    gs_f2 = _chip_sum(parts_f2, "chip_sum_ffn2")
    big = (
        (ffn1_w1, m_ffn1_w1, v_ffn1_w1, gs_f1[0:fs], True),
        (ffn1_w3, m_ffn1_w3, v_ffn1_w3, gs_f1[fs:2 * fs], True),
        (ffn1_w2, m_ffn1_w2, v_ffn1_w2, gs_f1[2 * fs:3 * fs], False),
        (w_in, m_w_in, v_w_in, gs_in, True),
        (w_out, m_w_out, v_w_out, gs_out, False),
        (ffn2_w1, m_ffn2_w1, v_ffn2_w1, gs_f2[0:fs], True),
        (ffn2_w3, m_ffn2_w3, v_ffn2_w3, gs_f2[fs:2 * fs], True),
        (ffn2_w2, m_ffn2_w2, v_ffn2_w2, gs_f2[2 * fs:3 * fs], False),
    )
    big_out = []
    for j, (w, m, v, g, t) in enumerate(big):
        g = g.T if t else g
        delta, nm, nv = _adamw_big(w[0], g, m[0], v[0], "adamw_%d" % j)
        big_out.append([a[None] for a in (g, delta, nm, nv)])

    small_w = (ffn1_norm, mix_norm, ffn2_norm, fnorm, pool_scale, ret_norm, pw3.reshape(-1, GROUP))
    small_m = (m_ffn1_norm, m_mix_norm, m_ffn2_norm, m_final_norm.reshape(1, d), m_pool_scale, m_ret_norm, m_pool_w.reshape(-1, GROUP))
    small_v = (v_ffn1_norm, v_mix_norm, v_ffn2_norm, v_final_norm.reshape(1, d), v_pool_scale, v_ret_norm, v_pool_w.reshape(-1, GROUP))
    res = _adamw_small(stats_all, pw_all, small_w, small_m, small_v, d, pwid)
    loss = res[0][0, 0]
    small_out = [list(res[1 + 4 * j:5 + 4 * j]) for j in range(len(small_w))]
    small_out[3] = [a.reshape(d) for a in small_out[3]]
    small_out[6] = [a.reshape(pool_w.shape) for a in small_out[6]]

    order = [small_out[0], big_out[0], big_out[1], big_out[2], small_out[1], big_out[3], small_out[6], small_out[4],
             small_out[5], big_out[4], small_out[2], big_out[5], big_out[6], big_out[7], small_out[3]]
    result = [loss, dx[None]]
    for kind in range(4):
        result += [t[kind] for t in order]
    return tuple(result)
```

```python
import numpy as np
import jax
import jax.numpy as jnp
from jax import lax
from jax.experimental import pallas as pl
from jax.experimental.pallas import tpu as pltpu

F32 = jnp.float32
BF16 = jnp.bfloat16

NDEV = 8
NCHIP = 4
EPS = 1e-6
N_POOL_GROUPS = 4
POOL_WINDOWS = (2, 4, 8, 16)
MAX_WINDOW = 16
GROUP = 128
RET_HEADS = 4
ROPE_BASE = 10000.0
ADAM_LR = 0.001
ADAM_B1 = 0.9
ADAM_B2 = 0.999
ADAM_EPS = 1e-08
ADAM_WD = 0.01
ADAM_STEP = 10

VMEM_LIMIT = 56 * 1024 * 1024
FFN_CHUNK = 256

NT = (((1,), (1,)), ((), ()))
NN = (((1,), (0,)), ((), ()))
TN = (((0,), (0,)), ((), ()))

ANY = pl.BlockSpec(memory_space=pl.ANY)


def _dot(a, b, dims):
    return lax.dot_general(a, b, dims, preferred_element_type=F32)


def _call(body, **kw):
    return pl.pallas_call(body, **kw)


def _params(**kw):
    return pltpu.CompilerParams(vmem_limit_bytes=VMEM_LIMIT, **kw)


def _seq(n):
    return _params(dimension_semantics=("arbitrary",) * n)


def _peer(k):
    x, y, c = lax.axis_index("x"), lax.axis_index("y"), lax.axis_index("c")
    return (1 - x if k & 4 else x, 1 - y if k & 2 else y, 1 - c if k & 1 else c)


def _flat(pos):
    return 4 * pos[0] + 2 * pos[1] + pos[2]


def _chip(pos):
    return 2 * pos[0] + pos[1]


def _row_tile(rows, cap):
    return max(t for t in range(16, min(rows, cap) + 1, 16) if rows % t == 0)


def _load_weights(parts, sems):
    copies = []
    for buf, dst in parts:
        rows = buf.shape[1]
        for p in range(NDEV):
            cp = pltpu.make_async_copy(buf.at[p], dst.at[pl.ds(p * rows, rows), :], sems.at[len(copies)])
            cp.start()
            copies.append(cp)
    return copies


def _sigmoid(a):
    return 1.0 / (1.0 + jnp.exp(-a))


def _remote(src, dst, send_sem, recv_sem, to):
    return pltpu.make_async_remote_copy(
        src_ref=src, dst_ref=dst, send_sem=send_sem, recv_sem=recv_sem, device_id=to, device_id_type=pl.DeviceIdType.MESH
    )


class _Gather:
    chips = (4, 2, 6)

    def __init__(self, shards):
        n = len(shards)
        self.operands = list(shards)
        self.out_shape = [jax.ShapeDtypeStruct((NDEV,) + a.shape, a.dtype) for a in shards]
        self.sems = [pltpu.SemaphoreType.DMA((7 * n,)), pltpu.SemaphoreType.DMA((7 * n,)), pltpu.SemaphoreType.DMA((n,))]

    def _copy(self, t, k, block, to, ins, outs, sems, own=False):
        dst = outs[t].at[_flat(block)]
        return _remote(ins[t] if own else dst, dst, sems[0].at[7 * t + k], sems[1].at[7 * t + k], to)

    def begin(self, ins, outs, sems):
        me, sibling = _peer(0), _peer(1)
        for t in range(len(ins)):
            pltpu.make_async_copy(ins[t], outs[t].at[_flat(me)], sems[2].at[t]).start()
            self._copy(t, 0, me, sibling, ins, outs, sems, own=True).start()
            for j, k in enumerate(self.chips):
                self._copy(t, 1 + j, me, _peer(k), ins, outs, sems, own=True).start()

    def mid(self, ins, outs, sems):
        me, sibling = _peer(0), _peer(1)
        for t in range(len(ins)):
            for j, k in enumerate(self.chips):
                self._copy(t, 1 + j, _peer(k), me, ins, outs, sems).wait_recv()
                self._copy(t, 4 + j, _peer(k), sibling, ins, outs, sems).start()

    def end(self, ins, outs, sems):
        me, sibling = _peer(0), _peer(1)
        for t in range(len(ins)):
            self._copy(t, 0, sibling, me, ins, outs, sems).wait_recv()
            for j, k in enumerate(self.chips):
                self._copy(t, 4 + j, _peer(k ^ 1), me, ins, outs, sems).wait_recv()
            for k in range(7):
                self._copy(t, k, me, me, ins, outs, sems).wait_send()
            pltpu.make_async_copy(ins[t], outs[t].at[_flat(me)], sems[2].at[t]).wait()


class _GatherDirect:
    def __init__(self, arrays):
        n = len(arrays)
        self.operands = list(arrays)
        self.out_shape = [jax.ShapeDtypeStruct((NDEV,) + a.shape, a.dtype) for a in arrays]
        self.sems = [pltpu.SemaphoreType.DMA((7 * n,)), pltpu.SemaphoreType.DMA((7 * n,)), pltpu.SemaphoreType.DMA((n,))]

    def begin(self, ins, outs, sems):
        mine = _flat(_peer(0))
        for t in range(len(ins)):
            pltpu.make_async_copy(ins[t], outs[t].at[mine], sems[2].at[t]).start()
            for k in range(1, NDEV):
                _remote(ins[t], outs[t].at[mine], sems[0].at[7 * t + k - 1], sems[1].at[7 * t + k - 1], _peer(k)).start()

    def mid(self, ins, outs, sems):
        pass

    def end(self, ins, outs, sems):
        mine = _flat(_peer(0))
        for t in range(len(ins)):
            for k in range(1, NDEV):
                cp = _remote(ins[t], outs[t].at[_flat(_peer(k))], sems[0].at[7 * t + k - 1], sems[1].at[7 * t + k - 1], _peer(k))
                cp.wait_recv()
                cp.wait_send()
            pltpu.make_async_copy(ins[t], outs[t].at[mine], sems[2].at[t]).wait()


class _SiblingSwap:
    def __init__(self, grads):
        n = len(grads)
        self.operands = list(grads)
        self.out_shape = []
        for g in grads:
            self.out_shape += [jax.ShapeDtypeStruct(g.shape[1:], g.dtype)] * 2
        self.sems = [pltpu.SemaphoreType.DMA((n,)), pltpu.SemaphoreType.DMA((n,)), pltpu.SemaphoreType.DMA((n,))]

    def _copies(self, t, ins, outs, sems):
        c = lax.axis_index("c")
        keep = pltpu.make_async_copy(ins[t].at[c], outs[2 * t], sems[2].at[t])
        give = _remote(ins[t].at[1 - c], outs[2 * t + 1], sems[0].at[t], sems[1].at[t], _peer(1))
        return keep, give

    def begin(self, ins, outs, sems):
        for t in range(len(ins)):
            for cp in self._copies(t, ins, outs, sems):
                cp.start()

    def mid(self, ins, outs, sems):
        pass

    def end(self, ins, outs, sems):
        for t in range(len(ins)):
            keep, give = self._copies(t, ins, outs, sems)
            give.wait_recv()
            give.wait_send()
            keep.wait()


class _ChipScatter:
    def __init__(self, sums):
        n = len(sums)
        self.operands = list(sums)
        self.out_shape = [jax.ShapeDtypeStruct(a.shape, a.dtype) for a in sums]
        self.sems = [pltpu.SemaphoreType.DMA((3 * n,)), pltpu.SemaphoreType.DMA((3 * n,)), pltpu.SemaphoreType.DMA((n,))]

    def _keep(self, t, ins, outs, sems):
        mine = _chip(_peer(0))
        return pltpu.make_async_copy(ins[t].at[mine], outs[t].at[mine], sems[2].at[t])

    def begin(self, ins, outs, sems):
        mine = _chip(_peer(0))
        for t in range(len(ins)):
            self._keep(t, ins, outs, sems).start()
            for j, k in enumerate((4, 2, 6)):
                to = _peer(k)
                _remote(ins[t].at[_chip(to)], outs[t].at[mine], sems[0].at[3 * t + j], sems[1].at[3 * t + j], to).start()

    def mid(self, ins, outs, sems):
        pass

    def end(self, ins, outs, sems):
        for t in range(len(ins)):
            for j, k in enumerate((4, 2, 6)):
                frm = _chip(_peer(k))
                cp = _remote(ins[t].at[frm], outs[t].at[frm], sems[0].at[3 * t + j], sems[1].at[3 * t + j], _peer(k))
                cp.wait_recv()
                cp.wait_send()
            self._keep(t, ins, outs, sems).wait()


def _split_refs(refs, counts):
    out, at = [], 0
    for n in counts:
        out.append(refs[at:at + n])
        at += n
    return out


def _comm_call(carries, name):
    nin = [len(c.operands) for c in carries]
    nout = [len(c.out_shape) for c in carries]
    nsem = [len(c.sems) for c in carries]

    def body(*refs):
        ins, outs, sems = _split_refs(refs, (sum(nin), sum(nout), sum(nsem)))
        parts = list(zip(carries, _split_refs(ins, nin), _split_refs(outs, nout), _split_refs(sems, nsem)))
        for c, i, o, s in parts:
            c.begin(i, o, s)
        for c, i, o, s in parts:
            c.mid(i, o, s)
        for c, i, o, s in parts:
            c.end(i, o, s)

    res = _call(
        body,
        name=name,
        out_shape=[sh for c in carries for sh in c.out_shape],
        in_specs=[ANY] * sum(nin),
        out_specs=[ANY] * sum(nout),
        scratch_shapes=[sm for c in carries for sm in c.sems],
        compiler_params=pltpu.CompilerParams(has_side_effects=True),
    )(*[a for c in carries for a in c.operands])
    return _split_refs(list(res), nout)


def _grid_call(body, carries, *, name, steps, in_specs, out_specs, out_shape, scratch_shapes, args):
    ni, no, ns = len(in_specs), len(out_specs), len(scratch_shapes)
    nin = [len(c.operands) for c in carries]
    nout = [len(c.out_shape) for c in carries]
    nsem = [len(c.sems) for c in carries]
    mid_step = max(steps - 2, 0)

    def wrapped(*refs):
        ins, cins, outs, couts, scr, csems = _split_refs(refs, (ni, sum(nin), no, sum(nout), ns, sum(nsem)))
        if not carries:
            return body(*ins, *outs, *scr)
        parts = list(zip(carries, _split_refs(cins, nin), _split_refs(couts, nout), _split_refs(csems, nsem)))
        step = pl.program_id(0)

        @pl.when(step == 0)
        def _():
            for c, i, o, s in parts:
                c.begin(i, o, s)

        body(*ins, *outs, *scr)

        @pl.when(step == mid_step)
        def _():
            for c, i, o, s in parts:
                c.mid(i, o, s)

        @pl.when(step == steps - 1)
        def _():
            for c, i, o, s in parts:
                c.end(i, o, s)

    res = _call(
        wrapped,
        name=name,
        grid=(steps,),
        in_specs=list(in_specs) + [ANY] * sum(nin),
        out_specs=list(out_specs) + [ANY] * sum(nout),
        out_shape=list(out_shape) + [sh for c in carries for sh in c.out_shape],
        scratch_shapes=list(scratch_shapes) + [sm for c in carries for sm in c.sems],
        compiler_params=_seq(1),
    )(*args, *[a for c in carries for a in c.operands])
    res = list(res)
    return res[:no], _split_refs(res[no:], nout)


def _ffn_fwd(x, gain, weights, ffn, head=None, carries=()):
    s, d = x.shape
    tm = min(512, s)
    nchunk = ffn // FFN_CHUNK

    def body(*refs):
        if head is None:
            x_ref, g_ref, b1, b3, b2, h_ref, a_ref, b_ref, w1s, w3s, w2s, sems = refs
        else:
            x_ref, g_ref, b1, b3, b2, gf_ref, t_ref, h_ref, a_ref, b_ref, dgf_ref, loss_ref, w1s, w3s, w2s, sems = refs
        i = pl.program_id(0)

        @pl.when(i == 0)
        def _():
            for cp in _load_weights(((b1, w1s), (b3, w3s), (b2, w2s)), sems):
                cp.wait()
            if head is not None:
                dgf_ref[...] = jnp.zeros_like(dgf_ref)
                loss_ref[...] = jnp.zeros_like(loss_ref)

        xv = x_ref[...]
        r = lax.rsqrt(jnp.mean(xv * xv, axis=-1, keepdims=True) + EPS)
        n = (xv * r * g_ref[...]).astype(BF16)
        acc = jnp.zeros((tm, d), F32)
        for c in range(nchunk):
            cols = slice(c * FFN_CHUNK, (c + 1) * FFN_CHUNK)
            a = _dot(n, w1s[cols, :], NT)
            b = _dot(n, w3s[cols, :], NT)
            a_ref[:, cols] = a.astype(BF16)
            b_ref[:, cols] = b.astype(BF16)
            hm = (a * _sigmoid(a) * b).astype(BF16)
            acc = acc + _dot(hm, w2s[cols, :], NN)
        h = xv + 0.5 * acc
        if head is None:
            h_ref[...] = h
        else:
            rf = lax.rsqrt(jnp.mean(h * h, axis=-1, keepdims=True) + EPS)
            nh = h * rf
            gf = gf_ref[...]
            err = nh * gf - t_ref[...]
            loss_ref[...] += jnp.sum(err * err, axis=0, keepdims=True) * (0.5 / d)
            dy = err * (1.0 / d)
            dgf_ref[...] += jnp.sum(dy * nh, axis=0, keepdims=True)
            dn = dy * gf
            h_ref[...] = rf * (dn - nh * jnp.mean(dn * nh, axis=-1, keepdims=True))

    tile = pl.BlockSpec((tm, d), lambda i: (i, 0))
    row = pl.BlockSpec((1, d), lambda i: (0, 0))
    wide = pl.BlockSpec((tm, ffn), lambda i: (i, 0))
    in_specs = [tile, row, ANY, ANY, ANY]
    out_shape = [
        jax.ShapeDtypeStruct((s, d), F32),
        jax.ShapeDtypeStruct((s, ffn), BF16),
        jax.ShapeDtypeStruct((s, ffn), BF16),
    ]
    out_specs = [tile, wide, wide]
    args = [x, gain] + list(weights)
    if head is not None:
        in_specs += [row, tile]
        args += list(head)
        out_shape += [jax.ShapeDtypeStruct((1, d), F32)] * 2
        out_specs += [row, row]
    return _grid_call(
        body,
        carries,
        name="ffn_fwd_loss" if head is not None else "ffn_fwd",
        steps=s // tm,
        in_specs=in_specs,
        out_specs=out_specs,
        out_shape=out_shape,
        scratch_shapes=[pltpu.VMEM((ffn, d), BF16)] * 3 + [pltpu.SemaphoreType.DMA((3 * NDEV,))],
        args=args,
    )


def _ffn_bwd(dh, x, a, b, gain, weights, ffn, name, carries=()):
    s, d = x.shape
    tm = min(256, s)
    nchunk = ffn // FFN_CHUNK

    def body(dh_ref, x_ref, a_ref, b_ref, g_ref, b1, b3, b2, dx_ref, da_ref, db_ref, hm_ref, n_ref, dg_ref, w1s, w3s, w2s, sems):
        i = pl.program_id(0)

        @pl.when(i == 0)
        def _():
            for cp in _load_weights(((b1, w1s), (b3, w3s), (b2, w2s)), sems):
                cp.wait()
            dg_ref[...] = jnp.zeros_like(dg_ref)

        xv = x_ref[...]
        dhv = dh_ref[...]
        g = g_ref[...]
        r = lax.rsqrt(jnp.mean(xv * xv, axis=-1, keepdims=True) + EPS)
        nh = xv * r
        n_ref[...] = (nh * g).astype(BF16)
        dout = (0.5 * dhv).astype(BF16)
        dn = jnp.zeros((tm, d), F32)
        for c in range(nchunk):
            cols = slice(c * FFN_CHUNK, (c + 1) * FFN_CHUNK)
            av = a_ref[:, cols].astype(F32)
            bv = b_ref[:, cols].astype(F32)
            sg = _sigmoid(av)
            sl = av * sg
            hm_ref[:, cols] = (sl * bv).astype(BF16)
            dhm = _dot(dout, w2s[cols, :], NT)
            da = (dhm * bv * (sg * (1.0 + av * (1.0 - sg)))).astype(BF16)
            db = (dhm * sl).astype(BF16)
            da_ref[:, cols] = da
            db_ref[:, cols] = db
            dn = dn + _dot(da, w1s[cols, :], NN) + _dot(db, w3s[cols, :], NN)
        dg_ref[...] += jnp.sum(dn * nh, axis=0, keepdims=True)
        dnh = dn * g
        dx_ref[...] = dhv + r * (dnh - nh * jnp.mean(dnh * nh, axis=-1, keepdims=True))

    tile = pl.BlockSpec((tm, d), lambda i: (i, 0))
    row = pl.BlockSpec((1, d), lambda i: (0, 0))
    wide = pl.BlockSpec((tm, ffn), lambda i: (i, 0))
    return _grid_call(
        body,
        carries,
        name=name,
        steps=s // tm,
        in_specs=[tile, tile, wide, wide, row, ANY, ANY, ANY],
        out_specs=[tile, wide, wide, wide, tile, row],
        out_shape=[
            jax.ShapeDtypeStruct((s, d), F32),
            jax.ShapeDtypeStruct((s, ffn), BF16),
            jax.ShapeDtypeStruct((s, ffn), BF16),
            jax.ShapeDtypeStruct((s, ffn), BF16),
            jax.ShapeDtypeStruct((s, d), BF16),
            jax.ShapeDtypeStruct((1, d), F32),
        ],
        scratch_shapes=[pltpu.VMEM((ffn, d), BF16)] * 3 + [pltpu.SemaphoreType.DMA((3 * NDEV,))],
        args=[dh, x, a, b, gain] + list(weights),
    )


def _wgrad(lhs, rhs, scale, name, carries=()):
    s, m = lhs.shape
    n = rhs.shape[1]
    rs = m // NDEV
    tk = min(512, s)
    steps = s // tk

    def body(l_ref, r_ref, o_ref, acc):
        k = pl.program_id(0)

        @pl.when(k == 0)
        def _():
            acc[...] = jnp.zeros_like(acc)

        acc[...] += _dot(l_ref[...], r_ref[...].astype(BF16), TN)

        @pl.when(k == steps - 1)
        def _():
            for p in range(NDEV):
                o_ref[p % 2, p // 2] = (acc[p * rs:(p + 1) * rs, :] * scale).astype(BF16)

    (out,), carried = _grid_call(
        body,
        carries,
        name=name,
        steps=steps,
        in_specs=[pl.BlockSpec((tk, m), lambda k: (k, 0)), pl.BlockSpec((tk, n), lambda k: (k, 0))],
        out_specs=[pl.BlockSpec((2, NCHIP, rs, n), lambda k: (0, 0, 0, 0))],
        out_shape=[jax.ShapeDtypeStruct((2, NCHIP, rs, n), BF16)],
        scratch_shapes=[pltpu.VMEM((m, n), F32)],
        args=[lhs, rhs],
    )
    return out, carried


def _mix_constants(s):
    c = GROUP
    lg = np.log1p(-np.exp2(-5.0 - np.arange(RET_HEADS, dtype=np.float32))).astype(np.float32)
    pos = np.arange(c, dtype=np.float32)
    rel = pos[:, None] - pos[None, :]
    decay = np.where(rel[None] >= 0, np.exp(lg[:, None, None] * np.maximum(rel, 0.0)[None]), 0.0).astype(np.float32)
    ktail = np.exp(lg[:, None] * (c - 1 - pos)[None, :]).astype(np.float32)
    qhead = np.exp(lg[:, None] * (pos + 1.0)[None, :]).astype(np.float32)
    chunk_decay = [float(v) for v in np.exp(lg * np.float32(c)).astype(np.float32)]
    ones = np.ones((1, 1, c), np.float32)
    inv_freq = (1.0 / (np.float32(ROPE_BASE) ** (np.arange(0, c, 2, dtype=np.float32) / np.float32(c)))).astype(np.float32)
    ang = (np.arange(s, dtype=np.float32)[:, None] * inv_freq[None, :]).astype(np.float32)
    cos, sin = np.cos(ang).astype(np.float32), np.sin(ang).astype(np.float32)
    return dict(
        decay=jnp.asarray(decay),
        ktail=jnp.asarray(ktail[:, :, None] * ones),
        qhead=jnp.asarray(qhead[:, :, None] * ones),
        chunk_decay=chunk_decay,
        cos=jnp.asarray(np.concatenate([cos, cos], axis=-1)),
        sin=jnp.asarray(np.concatenate([-sin, sin], axis=-1)),
    )


def _rope(t, cos, sin):
    return t * cos + pltpu.roll(t, GROUP // 2, axis=1) * sin


def _rope_bwd(dt, cos, sin):
    return dt * cos + pltpu.roll(dt * sin, GROUP // 2, axis=1)


def _window_sums(ext, w, forward):
    rows = ext.shape[0]
    acc, k = ext, 1
    while k < w:
        acc = acc + pltpu.roll(acc, k if forward else rows - k, axis=0)
        k *= 2
    return acc


def _pool_counts(tile, tm, w):
    t = lax.broadcasted_iota(jnp.int32, (tm, 1), 0) + tile * tm
    return jnp.minimum(t + 1, w).astype(F32)


def _mix_fwd(h1, gain, weights, pool_w, pool_scale, ret_gain, consts, carries=()):
    s, d = h1.shape
    pwid = N_POOL_GROUPS * GROUP
    rwid = RET_HEADS * GROUP
    inw = pwid + 4 * rwid
    tm = min(256, s)
    nck = tm // GROUP
    cd = consts["chunk_decay"]

    def body(h_ref, g_ref, bin_, bout, pw_ref, ps_ref, rg_ref, cos_ref, sin_ref, dec_ref, kt_ref, qh_ref,
             h2_ref, proj_ref, o_ref, rs_ref, wins, wouts, state, carry, mbuf, sems):
        i = pl.program_id(0)

        @pl.when(i == 0)
        def _():
            for cp in _load_weights(((bin_, wins), (bout, wouts)), sems):
                cp.wait()
            state[...] = jnp.zeros_like(state)
            carry[...] = jnp.zeros_like(carry)

        hv = h_ref[...]
        r = lax.rsqrt(jnp.mean(hv * hv, axis=-1, keepdims=True) + EPS)
        u = (hv * r * g_ref[...]).astype(BF16)
        proj_ref[...] = _dot(u, wins[...], NT)

        ext = jnp.concatenate([carry[...], proj_ref[:, 0:pwid]], axis=0)
        carry[...] = proj_ref[tm - MAX_WINDOW:tm, 0:pwid]
        for gi, w in enumerate(POOL_WINDOWS):
            cols = slice(gi * GROUP, (gi + 1) * GROUP)
            xg = ext[:, cols]
            ws = _window_sums(xg, w, True)[MAX_WINDOW:, :]
            pooled = ws / _pool_counts(i, tm, w) - xg[MAX_WINDOW:, :]
            z = _dot(pooled.astype(BF16), pw_ref[gi].astype(BF16), NN)
            mbuf[:, cols] = (z * ps_ref[:, cols]).astype(BF16)

        cos, sin = cos_ref[...], sin_ref[...]
        for h in range(RET_HEADS):
            cq = slice(pwid + h * GROUP, pwid + (h + 1) * GROUP)
            ck = slice(pwid + rwid + h * GROUP, pwid + rwid + (h + 1) * GROUP)
            cv = slice(pwid + 2 * rwid + h * GROUP, pwid + 2 * rwid + (h + 1) * GROUP)
            cg = slice(pwid + 3 * rwid + h * GROUP, pwid + 3 * rwid + (h + 1) * GROUP)
            ch = slice(h * GROUP, (h + 1) * GROUP)
            qr = _rope(proj_ref[:, cq], cos, sin)
            kr = _rope(proj_ref[:, ck], cos, sin) * (GROUP ** -0.5)
            vb = proj_ref[:, cv].astype(BF16)
            for n in range(nck):
                rows = slice(n * GROUP, (n + 1) * GROUP)
                qc, kc, vc = qr[rows], kr[rows], vb[rows]
                rb = state[h]
                rs_ref[n, h] = rb
                p = (_dot(qc.astype(BF16), kc.astype(BF16), NT) * dec_ref[h]).astype(BF16)
                o = _dot(p, vc, NN) + _dot((qc * qh_ref[h]).astype(BF16), rb.astype(BF16), NN)
                state[h] = cd[h] * rb + _dot((kc * kt_ref[h]).astype(BF16), vc, TN)
                o_ref[rows, ch] = o
                on = o * lax.rsqrt(jnp.mean(o * o, axis=-1, keepdims=True) + EPS)
                gv = proj_ref[rows, cg]
                mbuf[rows, pwid + h * GROUP:pwid + (h + 1) * GROUP] = (
                    gv * _sigmoid(gv) * (on * rg_ref[:, ch])
                ).astype(BF16)
        h2_ref[...] = hv + _dot(mbuf[...], wouts[...], NN)

    tile = pl.BlockSpec((tm, d), lambda i: (i, 0))
    full = lambda shape: pl.BlockSpec(shape, lambda i: (0,) * len(shape))
    return _grid_call(
        body,
        carries,
        name="mix_fwd",
        steps=s // tm,
        in_specs=[
            tile, full((1, d)), ANY, ANY,
            full((N_POOL_GROUPS, GROUP, GROUP)), full((1, pwid)), full((1, rwid)),
            pl.BlockSpec((tm, GROUP), lambda i: (i, 0)), pl.BlockSpec((tm, GROUP), lambda i: (i, 0)),
            full((RET_HEADS, GROUP, GROUP)), full((RET_HEADS, GROUP, GROUP)), full((RET_HEADS, GROUP, GROUP)),
        ],
        out_specs=[
            tile,
            pl.BlockSpec((tm, inw), lambda i: (i, 0)),
            pl.BlockSpec((tm, rwid), lambda i: (i, 0)),
            pl.BlockSpec((nck, RET_HEADS, GROUP, GROUP), lambda i: (i, 0, 0, 0)),
        ],
        out_shape=[
            jax.ShapeDtypeStruct((s, d), F32),
            jax.ShapeDtypeStruct((s, inw), F32),
            jax.ShapeDtypeStruct((s, rwid), F32),
            jax.ShapeDtypeStruct((s // GROUP, RET_HEADS, GROUP, GROUP), F32),
        ],
        scratch_shapes=[
            pltpu.VMEM((inw, d), BF16), pltpu.VMEM((d, d), BF16),
            pltpu.VMEM((RET_HEADS, GROUP, GROUP), F32), pltpu.VMEM((MAX_WINDOW, pwid), F32),
            pltpu.VMEM((tm, d), BF16), pltpu.SemaphoreType.DMA((2 * NDEV,)),
        ],
        args=[h1, gain, weights[0], weights[1], pool_w, pool_scale, ret_gain,
              consts["cos"], consts["sin"], consts["decay"], consts["ktail"], consts["qhead"]],
    )


def _mix_bwd(dh2, h1, proj, o_saved, rsave, gain, weights, pool_w, pool_scale, ret_gain, consts, carries=()):
    s, d = h1.shape
    pwid = N_POOL_GROUPS * GROUP
    rwid = RET_HEADS * GROUP
    inw = pwid + 4 * rwid
    tm = min(256, s)
    nck = tm // GROUP
    nt = s // tm
    cd = consts["chunk_decay"]
    halo_per_tile = tm // MAX_WINDOW

    def body(dh2_ref, h_ref, proj_ref, halo_ref, o_ref, rs_ref, g_ref, bin_, bout, pw_ref, ps_ref, rg_ref,
             cos_ref, sin_ref, dec_ref, kt_ref, qh_ref,
             dh1_ref, dproj_ref, u_ref, m_ref, dpw_ref, dps_ref, drg_ref, dg_ref,
             wins, wouts, dstate, carry, dm, dpj, sems):
        i = pl.program_id(0)
        tile = nt - 1 - i

        @pl.when(i == 0)
        def _():
            for cp in _load_weights(((bin_, wins), (bout, wouts)), sems):
                cp.wait()
            dstate[...] = jnp.zeros_like(dstate)
            carry[...] = jnp.zeros_like(carry)
            for ref in (dpw_ref, dps_ref, drg_ref, dg_ref):
                ref[...] = jnp.zeros_like(ref)

        dh2v = dh2_ref[...]
        dm[...] = _dot(dh2v.astype(BF16), wouts[...], NT)
        hv = h_ref[...]
        g = g_ref[...]
        r = lax.rsqrt(jnp.mean(hv * hv, axis=-1, keepdims=True) + EPS)
        uh = hv * r
        u_ref[...] = (uh * g).astype(BF16)

        halo = jnp.where(tile == 0, 0.0, halo_ref[...])
        ext = jnp.concatenate([halo, proj_ref[:, 0:pwid]], axis=0)
        next_dpn = carry[...]
        for gi, w in enumerate(POOL_WINDOWS):
            cols = slice(gi * GROUP, (gi + 1) * GROUP)
            xg = ext[:, cols]
            cnt = _pool_counts(tile, tm, w)
            pooled = (_window_sums(xg, w, True)[MAX_WINDOW:, :] / cnt - xg[MAX_WINDOW:, :]).astype(BF16)
            pwb = pw_ref[gi].astype(BF16)
            z = _dot(pooled, pwb, NN)
            scale = ps_ref[:, cols]
            m_ref[:, cols] = (z * scale).astype(BF16)
            da = dm[:, cols]
            dps_ref[:, cols] += jnp.sum(da * z, axis=0, keepdims=True)
            dz = (da * scale).astype(BF16)
            dpw_ref[gi] += _dot(pooled, dz, TN)
            dpl = _dot(dz, pwb, NT)
            dpn = dpl / cnt
            ext2 = jnp.concatenate([dpn, next_dpn[:, cols]], axis=0)
            dpj[:, cols] = (_window_sums(ext2, w, False)[0:tm, :] - dpl).astype(BF16)
            carry[:, cols] = dpn[0:MAX_WINDOW, :]

        cos, sin = cos_ref[...], sin_ref[...]
        for h in range(RET_HEADS):
            cq = slice(pwid + h * GROUP, pwid + (h + 1) * GROUP)
            ck = slice(pwid + rwid + h * GROUP, pwid + rwid + (h + 1) * GROUP)
            cv = slice(pwid + 2 * rwid + h * GROUP, pwid + 2 * rwid + (h + 1) * GROUP)
            cg = slice(pwid + 3 * rwid + h * GROUP, pwid + 3 * rwid + (h + 1) * GROUP)
            ch = slice(h * GROUP, (h + 1) * GROUP)
            qr = _rope(proj_ref[:, cq], cos, sin)
            kr = _rope(proj_ref[:, ck], cos, sin) * (GROUP ** -0.5)
            vb = proj_ref[:, cv].astype(BF16)
            gv = proj_ref[:, cg]
            ov = o_ref[:, ch]
            ro = lax.rsqrt(jnp.mean(ov * ov, axis=-1, keepdims=True) + EPS)
            on = ov * ro
            rg = rg_ref[:, ch]
            db = dm[:, pwid + h * GROUP:pwid + (h + 1) * GROUP]
            sg = _sigmoid(gv)
            sl = gv * sg
            m_ref[:, pwid + h * GROUP:pwid + (h + 1) * GROUP] = (sl * (on * rg)).astype(BF16)
            dpj[:, cg] = (db * (on * rg) * (sg * (1.0 + gv * (1.0 - sg)))).astype(BF16)
            drg_ref[:, ch] += jnp.sum(db * sl * on, axis=0, keepdims=True)
            don = db * sl * rg
            do = (ro * (don - on * jnp.mean(don * on, axis=-1, keepdims=True))).astype(BF16)
            for n in reversed(range(nck)):
                rows = slice(n * GROUP, (n + 1) * GROUP)
                qc, kc, vc, dob = qr[rows], kr[rows], vb[rows], do[rows]
                qcb, kcb = qc.astype(BF16), kc.astype(BF16)
                qh = (qc * qh_ref[h]).astype(BF16)
                kt = (kc * kt_ref[h]).astype(BF16)
                rn = rs_ref[n, h].astype(BF16)
                dnext = dstate[h]
                dnb = dnext.astype(BF16)
                dec = dec_ref[h]
                p = (_dot(qcb, kcb, NT) * dec).astype(BF16)
                ds = (_dot(dob, vc, NT) * dec).astype(BF16)
                dv = _dot(p, dob, TN) + _dot(kt, dnb, NN)
                dq = _dot(ds, kcb, NN) + _dot(dob, rn, NT) * qh_ref[h]
                dk = _dot(ds, qcb, TN) + _dot(vc, dnb, NT) * kt_ref[h]
                dstate[h] = cd[h] * dnext + _dot(qh, dob, TN)
                dpj[rows, cq] = _rope_bwd(dq, cos[rows], sin[rows]).astype(BF16)
                dpj[rows, ck] = _rope_bwd(dk * (GROUP ** -0.5), cos[rows], sin[rows]).astype(BF16)
                dpj[rows, cv] = dv.astype(BF16)

        dproj_ref[...] = dpj[...]
        du = _dot(dpj[...], wins[...], NN)
        dg_ref[...] += jnp.sum(du * uh, axis=0, keepdims=True)
        dn = du * g
        dh1_ref[...] = dh2v + r * (dn - uh * jnp.mean(dn * uh, axis=-1, keepdims=True))

    rev = lambda i: (nt - 1 - i, 0)
    tile = pl.BlockSpec((tm, d), rev)
    full = lambda shape: pl.BlockSpec(shape, lambda i: (0,) * len(shape))
    return _grid_call(
        body,
        carries,
        name="mix_bwd",
        steps=nt,
        in_specs=[
            tile, tile,
            pl.BlockSpec((tm, inw), rev),
            pl.BlockSpec((MAX_WINDOW, pwid), lambda i: (jnp.maximum((nt - 1 - i) * halo_per_tile - 1, 0), 0)),
            pl.BlockSpec((tm, rwid), rev),
            pl.BlockSpec((nck, RET_HEADS, GROUP, GROUP), lambda i: (nt - 1 - i, 0, 0, 0)),
            full((1, d)), ANY, ANY,
            full((N_POOL_GROUPS, GROUP, GROUP)), full((1, pwid)), full((1, rwid)),
            pl.BlockSpec((tm, GROUP), rev), pl.BlockSpec((tm, GROUP), rev),
            full((RET_HEADS, GROUP, GROUP)), full((RET_HEADS, GROUP, GROUP)), full((RET_HEADS, GROUP, GROUP)),
        ],
        out_specs=[
            tile, pl.BlockSpec((tm, inw), rev), tile, tile,
            full((N_POOL_GROUPS, GROUP, GROUP)), full((1, pwid)), full((1, rwid)), full((1, d)),
        ],
        out_shape=[
            jax.ShapeDtypeStruct((s, d), F32),
            jax.ShapeDtypeStruct((s, inw), BF16),
            jax.ShapeDtypeStruct((s, d), BF16),
            jax.ShapeDtypeStruct((s, d), BF16),
            jax.ShapeDtypeStruct((N_POOL_GROUPS, GROUP, GROUP), F32),
            jax.ShapeDtypeStruct((1, pwid), F32),
            jax.ShapeDtypeStruct((1, rwid), F32),
            jax.ShapeDtypeStruct((1, d), F32),
        ],
        scratch_shapes=[
            pltpu.VMEM((inw, d), BF16), pltpu.VMEM((d, d), BF16),
            pltpu.VMEM((RET_HEADS, GROUP, GROUP), F32), pltpu.VMEM((MAX_WINDOW, pwid), F32),
            pltpu.VMEM((tm, d), F32), pltpu.VMEM((tm, inw), BF16), pltpu.SemaphoreType.DMA((2 * NDEV,)),
        ],
        args=[dh2, h1, proj, proj, o_saved, rsave, gain, weights[0], weights[1], pool_w, pool_scale, ret_gain,
              consts["cos"], consts["sin"], consts["decay"], consts["ktail"], consts["qhead"]],
    )


def _adam(w, g, m, v):
    m = ADAM_B1 * m + (1.0 - ADAM_B1) * g
    v = ADAM_B2 * v + (1.0 - ADAM_B2) * jnp.square(g)
    m_hat = m / (1.0 - ADAM_B1 ** ADAM_STEP)
    v_hat = v / (1.0 - ADAM_B2 ** ADAM_STEP)
    delta = -ADAM_LR * (m_hat / (jnp.sqrt(v_hat) + ADAM_EPS) + ADAM_WD * w)
    return delta, m, v


def _pair_sum(mine, theirs, name):
    n, rows, d = mine.shape
    tr = _row_tile(rows, 512)

    def body(a_ref, b_ref, o_ref):
        o_ref[...] = (a_ref[...].astype(F32) + b_ref[...].astype(F32)).astype(BF16)

    spec = pl.BlockSpec((1, tr, d), lambda j, i: (j, i, 0))
    return _call(
        body,
        name=name,
        grid=(n, rows // tr),
        in_specs=[spec, spec],
        out_specs=spec,
        out_shape=jax.ShapeDtypeStruct(mine.shape, BF16),
        compiler_params=_seq(2),
    )(mine, theirs)


def _adamw_big(w, parts, m, v, name):
    rows, d = w.shape
    tr = _row_tile(rows, 512)

    def body(w_ref, p_ref, m_ref, v_ref, g_ref, d_ref, nm_ref, nv_ref):
        g = p_ref[0].astype(F32)
        for q in range(1, NCHIP):
            g = g + p_ref[q].astype(F32)
        g_ref[...] = g
        d_ref[...], nm_ref[...], nv_ref[...] = _adam(w_ref[...], g, m_ref[...], v_ref[...])

    spec = pl.BlockSpec((tr, d), lambda i: (i, 0))
    return _call(
        body,
        name=name,
        grid=(rows // tr,),
        in_specs=[spec, pl.BlockSpec((NCHIP, tr, d), lambda i: (0, i, 0)), spec, spec],
        out_specs=[spec] * 4,
        out_shape=[jax.ShapeDtypeStruct((rows, d), F32)] * 4,
        compiler_params=_seq(1),
    )(w, parts, m, v)


def _adamw_small(stats_all, pw_all, ws, ms, vs, d, pwid):
    nsmall = len(ws)

    def body(*refs):
        st_ref, pwa_ref = refs[0], refs[1]
        w_refs = refs[2:2 + nsmall]
        m_refs = refs[2 + nsmall:2 + 2 * nsmall]
        v_refs = refs[2 + 2 * nsmall:2 + 3 * nsmall]
        outs = refs[2 + 3 * nsmall:]
        st = st_ref[0]
        pwg = pwa_ref[0]
        for q in range(1, NDEV):
            st = st + st_ref[q]
            pwg = pwg + pwa_ref[q]
        grads = [st[0:1, :], st[1:2, :], st[2:3, :], st[3:4, :], st[4:5, 0:pwid], st[4:5, pwid:2 * pwid], pwg]
        outs[0][...] = jnp.zeros((1, GROUP), F32) + jnp.sum(st[5:6, :])
        for j in range(nsmall):
            delta, nm, nv = _adam(w_refs[j][...], grads[j], m_refs[j][...], v_refs[j][...])
            outs[1 + 4 * j][...] = grads[j]
            outs[2 + 4 * j][...] = delta
            outs[3 + 4 * j][...] = nm
            outs[4 + 4 * j][...] = nv

    out_shape = [jax.ShapeDtypeStruct((1, GROUP), F32)]
    for w in ws:
        out_shape += [jax.ShapeDtypeStruct(w.shape, F32)] * 4
    return _call(body, name="adamw_small", out_shape=out_shape, compiler_params=_params())(
        stats_all, pw_all, *ws, *ms, *vs
    )


def kernel(x, ffn1_norm, ffn1_w1, ffn1_w3, ffn1_w2, mix_norm, w_in, pool_w, pool_scale, ret_norm, w_out, ffn2_norm, ffn2_w1, ffn2_w3, ffn2_w2, final_norm, loss_target, m_ffn1_norm, m_ffn1_w1, m_ffn1_w3, m_ffn1_w2, m_mix_norm, m_w_in, m_pool_w, m_pool_scale, m_ret_norm, m_w_out, m_ffn2_norm, m_ffn2_w1, m_ffn2_w3, m_ffn2_w2, m_final_norm, v_ffn1_norm, v_ffn1_w1, v_ffn1_w3, v_ffn1_w2, v_mix_norm, v_w_in, v_pool_w, v_pool_scale, v_ret_norm, v_w_out, v_ffn2_norm, v_ffn2_w1, v_ffn2_w3, v_ffn2_w2, v_final_norm):
    s, d = x.shape[1], x.shape[2]
    ffn = ffn1_w1.shape[2] * NDEV
    pwid = pool_scale.shape[1]
    xs, tgt = x[0], loss_target[0]
    consts = _mix_constants(s)
    pw3 = pool_w[0]
    fnorm = final_norm.reshape(1, d)

    rows_of = lambda w, transposed: (w[0].T if transposed else w[0]).astype(BF16)
    send_f1 = [rows_of(ffn1_w1, True), rows_of(ffn1_w3, True), rows_of(ffn1_w2, False)]
    send_mix = [rows_of(w_in, True), rows_of(w_out, False)]
    send_f2 = [rows_of(ffn2_w1, True), rows_of(ffn2_w3, True), rows_of(ffn2_w2, False)]

    (w_f1,) = _comm_call([_Gather(send_f1)], "gather_ffn1")
    (h1, a1, b1), (more,) = _ffn_fwd(xs, ffn1_norm, w_f1, ffn, carries=[_Gather(send_mix + send_f2[:1])])
    w_mix = more[:2]
    (h2, proj, o_saved, rsave), (rest,) = _mix_fwd(
        h1, mix_norm, w_mix, pw3, pool_scale, ret_norm, consts, carries=[_Gather(send_f2[1:])]
    )
    w_f2 = more[2:] + rest
    (dh3, a2, b2, dgf, loss_cols), _ = _ffn_fwd(h2, ffn2_norm, w_f2, ffn, head=(fnorm, tgt))

    def pair_reduced(g, tag):
        ((mine, theirs),) = _comm_call([_SiblingSwap([g])], "swap_" + tag)
        return _pair_sum(mine, theirs, "pair_sum_" + tag)

    (dh2, da2, db2, hm2, n2, dg2), _ = _ffn_bwd(dh3, h2, a2, b2, ffn2_norm, w_f2, ffn, "ffn2_bwd")
    g, _ = _wgrad(da2, n2, 1.0, "ffn2_w1_grad")
    sum_f2w1 = pair_reduced(g, "ffn2_w1")
    g, ((parts_f2w1,),) = _wgrad(db2, n2, 1.0, "ffn2_w3_grad", carries=[_ChipScatter([sum_f2w1])])
    sum_f2w3 = pair_reduced(g, "ffn2_w3")
    g, ((parts_f2w3,),) = _wgrad(hm2, dh3, 0.5, "ffn2_w2_grad", carries=[_ChipScatter([sum_f2w3])])
    sum_f2w2 = pair_reduced(g, "ffn2_w2")

    (dh1, dproj, u, mm, dpw, dps, drg, dgm), ((parts_f2w2,),) = _mix_bwd(
        dh2, h1, proj, o_saved, rsave, mix_norm, w_mix, pw3, pool_scale, ret_norm, consts,
        carries=[_ChipScatter([sum_f2w2])],
    )
    g, _ = _wgrad(dproj, u, 1.0, "w_in_grad")
    sum_in = pair_reduced(g, "w_in")
    g, _ = _wgrad(mm, dh2, 1.0, "w_out_grad")
    sum_out = pair_reduced(g, "w_out")

    (dx, da1, db1, hm1, n1, dg1), ((parts_in, parts_out),) = _ffn_bwd(
        dh1, xs, a1, b1, ffn1_norm, w_f1, ffn, "ffn1_bwd", carries=[_ChipScatter([sum_in, sum_out])]
    )
    g, _ = _wgrad(hm1, dh1, 0.5, "ffn1_w2_grad")
    sum_f1w2 = pair_reduced(g, "ffn1_w2")
    g, ((parts_f1w2,),) = _wgrad(da1, n1, 1.0, "ffn1_w1_grad", carries=[_ChipScatter([sum_f1w2])])
    sum_f1w1 = pair_reduced(g, "ffn1_w1")
    g, ((parts_f1w1,),) = _wgrad(db1, n1, 1.0, "ffn1_w3_grad", carries=[_ChipScatter([sum_f1w1])])
    sum_f1w3 = pair_reduced(g, "ffn1_w3")

    stats = jnp.concatenate(
        [dg1, dgm, dg2, dgf, jnp.concatenate([dps, drg], axis=1), loss_cols, jnp.zeros((2, d), F32)], axis=0
    )
    (parts_f1w3,), (stats_all, pw_all) = _comm_call(
        [_ChipScatter([sum_f1w3]), _GatherDirect([stats, dpw.reshape(N_POOL_GROUPS * GROUP, GROUP)])], "scatter_last"
    )

    big = (
        (ffn1_w1, m_ffn1_w1, v_ffn1_w1, parts_f1w1, True),
        (ffn1_w3, m_ffn1_w3, v_ffn1_w3, parts_f1w3, True),
        (ffn1_w2, m_ffn1_w2, v_ffn1_w2, parts_f1w2, False),
        (w_in, m_w_in, v_w_in, parts_in, True),
        (w_out, m_w_out, v_w_out, parts_out, False),
        (ffn2_w1, m_ffn2_w1, v_ffn2_w1, parts_f2w1, True),
        (ffn2_w3, m_ffn2_w3, v_ffn2_w3, parts_f2w3, True),
        (ffn2_w2, m_ffn2_w2, v_ffn2_w2, parts_f2w2, False),
    )
    big_out = []
    for j, (w, m, v, parts, t) in enumerate(big):
        view = (lambda a: a[0].T) if t else (lambda a: a[0])
        back = (lambda a: a.T[None]) if t else (lambda a: a[None])
        big_out.append([back(a) for a in _adamw_big(view(w), parts, view(m), view(v), "adamw_%d" % j)])

    small_w = (ffn1_norm, mix_norm, ffn2_norm, fnorm, pool_scale, ret_norm, pw3.reshape(-1, GROUP))
    small_m = (m_ffn1_norm, m_mix_norm, m_ffn2_norm, m_final_norm.reshape(1, d), m_pool_scale, m_ret_norm, m_pool_w.reshape(-1, GROUP))
    small_v = (v_ffn1_norm, v_mix_norm, v_ffn2_norm, v_final_norm.reshape(1, d), v_pool_scale, v_ret_norm, v_pool_w.reshape(-1, GROUP))
    res = _adamw_small(stats_all, pw_all, small_w, small_m, small_v, d, pwid)
    loss = res[0][0, 0]
    small_out = [list(res[1 + 4 * j:5 + 4 * j]) for j in range(len(small_w))]
    small_out[3] = [a.reshape(d) for a in small_out[3]]
    small_out[6] = [a.reshape(pool_w.shape) for a in small_out[6]]

    order = [small_out[0], big_out[0], big_out[1], big_out[2], small_out[1], big_out[3], small_out[6], small_out[4],
             small_out[5], big_out[4], small_out[2], big_out[5], big_out[6], big_out[7], small_out[3]]
    result = [loss, dx[None]]
    for kind in range(4):
        result += [t[kind] for t in order]
    return tuple(result)
```

```python
import numpy as np
import jax
import jax.numpy as jnp
from jax import lax
from jax.experimental import pallas as pl
from jax.experimental.pallas import tpu as pltpu

F32 = jnp.float32
BF16 = jnp.bfloat16

NDEV = 8
NCHIP = 4
EPS = 1e-6
N_POOL_GROUPS = 4
POOL_WINDOWS = (2, 4, 8, 16)
MAX_WINDOW = 16
GROUP = 128
RET_HEADS = 4
ROPE_BASE = 10000.0
ADAM_LR = 0.001
ADAM_B1 = 0.9
ADAM_B2 = 0.999
ADAM_EPS = 1e-08
ADAM_WD = 0.01
ADAM_STEP = 10

VMEM_LIMIT = 56 * 1024 * 1024
FFN_CHUNK = 256

NT = (((1,), (1,)), ((), ()))
NN = (((1,), (0,)), ((), ()))
TN = (((0,), (0,)), ((), ()))

ANY = pl.BlockSpec(memory_space=pl.ANY)


def _dot(a, b, dims):
    return lax.dot_general(a, b, dims, preferred_element_type=F32)


def _call(body, **kw):
    return pl.pallas_call(body, **kw)


def _params(**kw):
    return pltpu.CompilerParams(vmem_limit_bytes=VMEM_LIMIT, **kw)


def _seq(n):
    return _params(dimension_semantics=("arbitrary",) * n)


def _peer(k):
    x, y, c = lax.axis_index("x"), lax.axis_index("y"), lax.axis_index("c")
    return (1 - x if k & 4 else x, 1 - y if k & 2 else y, 1 - c if k & 1 else c)


def _flat(pos):
    return 4 * pos[0] + 2 * pos[1] + pos[2]


def _chip(pos):
    return 2 * pos[0] + pos[1]


def _row_tile(rows, cap):
    return max(t for t in range(16, min(rows, cap) + 1, 16) if rows % t == 0)


def _load_weights(parts, sems):
    copies = []
    for buf, dst in parts:
        rows = buf.shape[1]
        for p in range(NDEV):
            cp = pltpu.make_async_copy(buf.at[p], dst.at[pl.ds(p * rows, rows), :], sems.at[len(copies)])
            cp.start()
            copies.append(cp)
    return copies


def _sigmoid(a):
    return 1.0 / (1.0 + jnp.exp(-a))


def _remote(src, dst, send_sem, recv_sem, to):
    return pltpu.make_async_remote_copy(
        src_ref=src, dst_ref=dst, send_sem=send_sem, recv_sem=recv_sem, device_id=to, device_id_type=pl.DeviceIdType.MESH
    )


class _Gather:
    chips = (4, 2, 6)

    def __init__(self, shards):
        n = len(shards)
        self.operands = list(shards)
        self.out_shape = [jax.ShapeDtypeStruct((NDEV,) + a.shape, a.dtype) for a in shards]
        self.sems = [pltpu.SemaphoreType.DMA((7 * n,)), pltpu.SemaphoreType.DMA((7 * n,)), pltpu.SemaphoreType.DMA((n,))]

    def _copy(self, t, k, block, to, ins, outs, sems, own=False):
        dst = outs[t].at[_flat(block)]
        return _remote(ins[t] if own else dst, dst, sems[0].at[7 * t + k], sems[1].at[7 * t + k], to)

    def begin(self, ins, outs, sems):
        me, sibling = _peer(0), _peer(1)
        for t in range(len(ins)):
            pltpu.make_async_copy(ins[t], outs[t].at[_flat(me)], sems[2].at[t]).start()
            self._copy(t, 0, me, sibling, ins, outs, sems, own=True).start()
            for j, k in enumerate(self.chips):
                self._copy(t, 1 + j, me, _peer(k), ins, outs, sems, own=True).start()

    def mid(self, ins, outs, sems):
        me, sibling = _peer(0), _peer(1)
        for t in range(len(ins)):
            for j, k in enumerate(self.chips):
                self._copy(t, 1 + j, _peer(k), me, ins, outs, sems).wait_recv()
                self._copy(t, 4 + j, _peer(k), sibling, ins, outs, sems).start()

    def end(self, ins, outs, sems):
        me, sibling = _peer(0), _peer(1)
        for t in range(len(ins)):
            self._copy(t, 0, sibling, me, ins, outs, sems).wait_recv()
            for j, k in enumerate(self.chips):
                self._copy(t, 4 + j, _peer(k ^ 1), me, ins, outs, sems).wait_recv()
            for k in range(7):
                self._copy(t, k, me, me, ins, outs, sems).wait_send()
            pltpu.make_async_copy(ins[t], outs[t].at[_flat(me)], sems[2].at[t]).wait()


class _GatherDirect:
    def __init__(self, arrays):
        n = len(arrays)
        self.operands = list(arrays)
        self.out_shape = [jax.ShapeDtypeStruct((NDEV,) + a.shape, a.dtype) for a in arrays]
        self.sems = [pltpu.SemaphoreType.DMA((7 * n,)), pltpu.SemaphoreType.DMA((7 * n,)), pltpu.SemaphoreType.DMA((n,))]

    def begin(self, ins, outs, sems):
        mine = _flat(_peer(0))
        for t in range(len(ins)):
            pltpu.make_async_copy(ins[t], outs[t].at[mine], sems[2].at[t]).start()
            for k in range(1, NDEV):
                _remote(ins[t], outs[t].at[mine], sems[0].at[7 * t + k - 1], sems[1].at[7 * t + k - 1], _peer(k)).start()

    def mid(self, ins, outs, sems):
        pass

    def end(self, ins, outs, sems):
        mine = _flat(_peer(0))
        for t in range(len(ins)):
            for k in range(1, NDEV):
                cp = _remote(ins[t], outs[t].at[_flat(_peer(k))], sems[0].at[7 * t + k - 1], sems[1].at[7 * t + k - 1], _peer(k))
                cp.wait_recv()
                cp.wait_send()
            pltpu.make_async_copy(ins[t], outs[t].at[mine], sems[2].at[t]).wait()


class _SiblingSwap:
    pieces = 2

    def __init__(self, grads):
        n = len(grads) * NCHIP * self.pieces
        self.operands = list(grads)
        self.out_shape = [jax.ShapeDtypeStruct(g.shape[1:], g.dtype) for g in grads]
        self.sems = [pltpu.SemaphoreType.DMA((n,)), pltpu.SemaphoreType.DMA((n,))]

    def _copies(self, ins, outs, sems):
        c = lax.axis_index("c")
        copies = []
        for t in range(len(ins)):
            rows = ins[t].shape[2] // self.pieces
            for j in range(NCHIP):
                for q in range(self.pieces):
                    part = pl.ds(q * rows, rows)
                    k = len(copies)
                    copies.append(
                        _remote(ins[t].at[1 - c, j, part, :], outs[t].at[j, part, :], sems[0].at[k], sems[1].at[k], _peer(1))
                    )
        return copies

    def begin(self, ins, outs, sems):
        for cp in self._copies(ins, outs, sems):
            cp.start()

    def mid(self, ins, outs, sems):
        pass

    def end(self, ins, outs, sems):
        for cp in self._copies(ins, outs, sems):
            cp.wait_recv()
            cp.wait_send()


class _ChipScatter:
    def __init__(self, sums):
        n = len(sums)
        self.operands = list(sums)
        self.out_shape = [jax.ShapeDtypeStruct(a.shape, a.dtype) for a in sums]
        self.sems = [pltpu.SemaphoreType.DMA((3 * n,)), pltpu.SemaphoreType.DMA((3 * n,)), pltpu.SemaphoreType.DMA((n,))]

    def _keep(self, t, ins, outs, sems):
        mine = _chip(_peer(0))
        return pltpu.make_async_copy(ins[t].at[mine], outs[t].at[mine], sems[2].at[t])

    def begin(self, ins, outs, sems):
        mine = _chip(_peer(0))
        for t in range(len(ins)):
            self._keep(t, ins, outs, sems).start()
            for j, k in enumerate((4, 2, 6)):
                to = _peer(k)
                _remote(ins[t].at[_chip(to)], outs[t].at[mine], sems[0].at[3 * t + j], sems[1].at[3 * t + j], to).start()

    def mid(self, ins, outs, sems):
        pass

    def end(self, ins, outs, sems):
        for t in range(len(ins)):
            for j, k in enumerate((4, 2, 6)):
                frm = _chip(_peer(k))
                cp = _remote(ins[t].at[frm], outs[t].at[frm], sems[0].at[3 * t + j], sems[1].at[3 * t + j], _peer(k))
                cp.wait_recv()
                cp.wait_send()
            self._keep(t, ins, outs, sems).wait()


def _split_refs(refs, counts):
    out, at = [], 0
    for n in counts:
        out.append(refs[at:at + n])
        at += n
    return out


def _comm_call(carries, name):
    nin = [len(c.operands) for c in carries]
    nout = [len(c.out_shape) for c in carries]
    nsem = [len(c.sems) for c in carries]

    def body(*refs):
        ins, outs, sems = _split_refs(refs, (sum(nin), sum(nout), sum(nsem)))
        parts = list(zip(carries, _split_refs(ins, nin), _split_refs(outs, nout), _split_refs(sems, nsem)))
        for c, i, o, s in parts:
            c.begin(i, o, s)
        for c, i, o, s in parts:
            c.mid(i, o, s)
        for c, i, o, s in parts:
            c.end(i, o, s)

    res = _call(
        body,
        name=name,
        out_shape=[sh for c in carries for sh in c.out_shape],
        in_specs=[ANY] * sum(nin),
        out_specs=[ANY] * sum(nout),
        scratch_shapes=[sm for c in carries for sm in c.sems],
        compiler_params=pltpu.CompilerParams(has_side_effects=True),
    )(*[a for c in carries for a in c.operands])
    return _split_refs(list(res), nout)


def _grid_call(body, carries, *, name, steps, in_specs, out_specs, out_shape, scratch_shapes, args):
    ni, no, ns = len(in_specs), len(out_specs), len(scratch_shapes)
    nin = [len(c.operands) for c in carries]
    nout = [len(c.out_shape) for c in carries]
    nsem = [len(c.sems) for c in carries]
    mid_step = max(steps - 2, 0)

    def wrapped(*refs):
        ins, cins, outs, couts, scr, csems = _split_refs(refs, (ni, sum(nin), no, sum(nout), ns, sum(nsem)))
        if not carries:
            return body(*ins, *outs, *scr)
        parts = list(zip(carries, _split_refs(cins, nin), _split_refs(couts, nout), _split_refs(csems, nsem)))
        step = pl.program_id(0)

        @pl.when(step == 0)
        def _():
            for c, i, o, s in parts:
                c.begin(i, o, s)

        body(*ins, *outs, *scr)

        @pl.when(step == mid_step)
        def _():
            for c, i, o, s in parts:
                c.mid(i, o, s)

        @pl.when(step == steps - 1)
        def _():
            for c, i, o, s in parts:
                c.end(i, o, s)

    res = _call(
        wrapped,
        name=name,
        grid=(steps,),
        in_specs=list(in_specs) + [ANY] * sum(nin),
        out_specs=list(out_specs) + [ANY] * sum(nout),
        out_shape=list(out_shape) + [sh for c in carries for sh in c.out_shape],
        scratch_shapes=list(scratch_shapes) + [sm for c in carries for sm in c.sems],
        compiler_params=_seq(1),
    )(*args, *[a for c in carries for a in c.operands])
    res = list(res)
    return res[:no], _split_refs(res[no:], nout)


def _ffn_fwd(x, gain, weights, ffn, head=None, carries=()):
    s, d = x.shape
    tm = min(512, s)
    nchunk = ffn // FFN_CHUNK

    def body(*refs):
        if head is None:
            x_ref, g_ref, b1, b3, b2, h_ref, a_ref, b_ref, w1s, w3s, w2s, sems = refs
        else:
            x_ref, g_ref, b1, b3, b2, gf_ref, t_ref, h_ref, a_ref, b_ref, dgf_ref, loss_ref, w1s, w3s, w2s, sems = refs
        i = pl.program_id(0)

        @pl.when(i == 0)
        def _():
            for cp in _load_weights(((b1, w1s), (b3, w3s), (b2, w2s)), sems):
                cp.wait()
            if head is not None:
                dgf_ref[...] = jnp.zeros_like(dgf_ref)
                loss_ref[...] = jnp.zeros_like(loss_ref)

        xv = x_ref[...]
        r = lax.rsqrt(jnp.mean(xv * xv, axis=-1, keepdims=True) + EPS)
        n = (xv * r * g_ref[...]).astype(BF16)
        acc = jnp.zeros((tm, d), F32)
        for c in range(nchunk):
            cols = slice(c * FFN_CHUNK, (c + 1) * FFN_CHUNK)
            a = _dot(n, w1s[cols, :], NT)
            b = _dot(n, w3s[cols, :], NT)
            a_ref[:, cols] = a.astype(BF16)
            b_ref[:, cols] = b.astype(BF16)
            hm = (a * _sigmoid(a) * b).astype(BF16)
            acc = acc + _dot(hm, w2s[cols, :], NN)
        h = xv + 0.5 * acc
        if head is None:
            h_ref[...] = h
        else:
            rf = lax.rsqrt(jnp.mean(h * h, axis=-1, keepdims=True) + EPS)
            nh = h * rf
            gf = gf_ref[...]
            err = nh * gf - t_ref[...]
            loss_ref[...] += jnp.sum(err * err, axis=0, keepdims=True) * (0.5 / d)
            dy = err * (1.0 / d)
            dgf_ref[...] += jnp.sum(dy * nh, axis=0, keepdims=True)
            dn = dy * gf
            h_ref[...] = rf * (dn - nh * jnp.mean(dn * nh, axis=-1, keepdims=True))

    tile = pl.BlockSpec((tm, d), lambda i: (i, 0))
    row = pl.BlockSpec((1, d), lambda i: (0, 0))
    wide = pl.BlockSpec((tm, ffn), lambda i: (i, 0))
    in_specs = [tile, row, ANY, ANY, ANY]
    out_shape = [
        jax.ShapeDtypeStruct((s, d), F32),
        jax.ShapeDtypeStruct((s, ffn), BF16),
        jax.ShapeDtypeStruct((s, ffn), BF16),
    ]
    out_specs = [tile, wide, wide]
    args = [x, gain] + list(weights)
    if head is not None:
        in_specs += [row, tile]
        args += list(head)
        out_shape += [jax.ShapeDtypeStruct((1, d), F32)] * 2
        out_specs += [row, row]
    return _grid_call(
        body,
        carries,
        name="ffn_fwd_loss" if head is not None else "ffn_fwd",
        steps=s // tm,
        in_specs=in_specs,
        out_specs=out_specs,
        out_shape=out_shape,
        scratch_shapes=[pltpu.VMEM((ffn, d), BF16)] * 3 + [pltpu.SemaphoreType.DMA((3 * NDEV,))],
        args=args,
    )


def _ffn_bwd(dh, x, a, b, gain, weights, ffn, name, carries=()):
    s, d = x.shape
    tm = min(256, s)
    nchunk = ffn // FFN_CHUNK

    def body(dh_ref, x_ref, a_ref, b_ref, g_ref, b1, b3, b2, dx_ref, da_ref, db_ref, hm_ref, n_ref, dg_ref, w1s, w3s, w2s, sems):
        i = pl.program_id(0)

        @pl.when(i == 0)
        def _():
            for cp in _load_weights(((b1, w1s), (b3, w3s), (b2, w2s)), sems):
                cp.wait()
            dg_ref[...] = jnp.zeros_like(dg_ref)

        xv = x_ref[...]
        dhv = dh_ref[...]
        g = g_ref[...]
        r = lax.rsqrt(jnp.mean(xv * xv, axis=-1, keepdims=True) + EPS)
        nh = xv * r
        n_ref[...] = (nh * g).astype(BF16)
        dout = (0.5 * dhv).astype(BF16)
        dn = jnp.zeros((tm, d), F32)
        for c in range(nchunk):
            cols = slice(c * FFN_CHUNK, (c + 1) * FFN_CHUNK)
            av = a_ref[:, cols].astype(F32)
            bv = b_ref[:, cols].astype(F32)
            sg = _sigmoid(av)
            sl = av * sg
            hm_ref[:, cols] = (sl * bv).astype(BF16)
            dhm = _dot(dout, w2s[cols, :], NT)
            da = (dhm * bv * (sg * (1.0 + av * (1.0 - sg)))).astype(BF16)
            db = (dhm * sl).astype(BF16)
            da_ref[:, cols] = da
            db_ref[:, cols] = db
            dn = dn + _dot(da, w1s[cols, :], NN) + _dot(db, w3s[cols, :], NN)
        dg_ref[...] += jnp.sum(dn * nh, axis=0, keepdims=True)
        dnh = dn * g
        dx_ref[...] = dhv + r * (dnh - nh * jnp.mean(dnh * nh, axis=-1, keepdims=True))

    tile = pl.BlockSpec((tm, d), lambda i: (i, 0))
    row = pl.BlockSpec((1, d), lambda i: (0, 0))
    wide = pl.BlockSpec((tm, ffn), lambda i: (i, 0))
    return _grid_call(
        body,
        carries,
        name=name,
        steps=s // tm,
        in_specs=[tile, tile, wide, wide, row, ANY, ANY, ANY],
        out_specs=[tile, wide, wide, wide, tile, row],
        out_shape=[
            jax.ShapeDtypeStruct((s, d), F32),
            jax.ShapeDtypeStruct((s, ffn), BF16),
            jax.ShapeDtypeStruct((s, ffn), BF16),
            jax.ShapeDtypeStruct((s, ffn), BF16),
            jax.ShapeDtypeStruct((s, d), BF16),
            jax.ShapeDtypeStruct((1, d), F32),
        ],
        scratch_shapes=[pltpu.VMEM((ffn, d), BF16)] * 3 + [pltpu.SemaphoreType.DMA((3 * NDEV,))],
        args=[dh, x, a, b, gain] + list(weights),
    )


def _wgrad(lhs, rhs, scale, name, carries=()):
    s, m = lhs.shape
    n = rhs.shape[1]
    rs = m // NDEV
    tk = min(512, s)
    steps = s // tk

    def body(l_ref, r_ref, o_ref, acc):
        k = pl.program_id(0)

        @pl.when(k == 0)
        def _():
            acc[...] = jnp.zeros_like(acc)

        acc[...] += _dot(l_ref[...], r_ref[...].astype(BF16), TN)

        @pl.when(k == steps - 1)
        def _():
            for p in range(NDEV):
                o_ref[p % 2, p // 2] = (acc[p * rs:(p + 1) * rs, :] * scale).astype(BF16)

    (out,), carried = _grid_call(
        body,
        carries,
        name=name,
        steps=steps,
        in_specs=[pl.BlockSpec((tk, m), lambda k: (k, 0)), pl.BlockSpec((tk, n), lambda k: (k, 0))],
        out_specs=[pl.BlockSpec((2, NCHIP, rs, n), lambda k: (0, 0, 0, 0))],
        out_shape=[jax.ShapeDtypeStruct((2, NCHIP, rs, n), BF16)],
        scratch_shapes=[pltpu.VMEM((m, n), F32)],
        args=[lhs, rhs],
    )
    return out, carried


def _mix_constants(s):
    c = GROUP
    lg = np.log1p(-np.exp2(-5.0 - np.arange(RET_HEADS, dtype=np.float32))).astype(np.float32)
    pos = np.arange(c, dtype=np.float32)
    rel = pos[:, None] - pos[None, :]
    decay = np.where(rel[None] >= 0, np.exp(lg[:, None, None] * np.maximum(rel, 0.0)[None]), 0.0).astype(np.float32)
    ktail = np.exp(lg[:, None] * (c - 1 - pos)[None, :]).astype(np.float32)
    qhead = np.exp(lg[:, None] * (pos + 1.0)[None, :]).astype(np.float32)
    chunk_decay = [float(v) for v in np.exp(lg * np.float32(c)).astype(np.float32)]
    ones = np.ones((1, 1, c), np.float32)
    inv_freq = (1.0 / (np.float32(ROPE_BASE) ** (np.arange(0, c, 2, dtype=np.float32) / np.float32(c)))).astype(np.float32)
    ang = (np.arange(s, dtype=np.float32)[:, None] * inv_freq[None, :]).astype(np.float32)
    cos, sin = np.cos(ang).astype(np.float32), np.sin(ang).astype(np.float32)
    return dict(
        decay=jnp.asarray(decay),
        ktail=jnp.asarray(ktail[:, :, None] * ones),
        qhead=jnp.asarray(qhead[:, :, None] * ones),
        chunk_decay=chunk_decay,
        cos=jnp.asarray(np.concatenate([cos, cos], axis=-1)),
        sin=jnp.asarray(np.concatenate([-sin, sin], axis=-1)),
    )


def _rope(t, cos, sin):
    return t * cos + pltpu.roll(t, GROUP // 2, axis=1) * sin


def _rope_bwd(dt, cos, sin):
    return dt * cos + pltpu.roll(dt * sin, GROUP // 2, axis=1)


def _window_sums(ext, w, forward):
    rows = ext.shape[0]
    acc, k = ext, 1
    while k < w:
        acc = acc + pltpu.roll(acc, k if forward else rows - k, axis=0)
        k *= 2
    return acc


def _pool_counts(tile, tm, w):
    t = lax.broadcasted_iota(jnp.int32, (tm, 1), 0) + tile * tm
    return jnp.minimum(t + 1, w).astype(F32)


def _mix_fwd(h1, gain, weights, pool_w, pool_scale, ret_gain, consts, carries=()):
    s, d = h1.shape
    pwid = N_POOL_GROUPS * GROUP
    rwid = RET_HEADS * GROUP
    inw = pwid + 4 * rwid
    tm = min(256, s)
    nck = tm // GROUP
    cd = consts["chunk_decay"]

    def body(h_ref, g_ref, bin_, bout, pw_ref, ps_ref, rg_ref, cos_ref, sin_ref, dec_ref, kt_ref, qh_ref,
             h2_ref, proj_ref, o_ref, rs_ref, wins, wouts, state, carry, mbuf, sems):
        i = pl.program_id(0)

        @pl.when(i == 0)
        def _():
            for cp in _load_weights(((bin_, wins), (bout, wouts)), sems):
                cp.wait()
            state[...] = jnp.zeros_like(state)
            carry[...] = jnp.zeros_like(carry)

        hv = h_ref[...]
        r = lax.rsqrt(jnp.mean(hv * hv, axis=-1, keepdims=True) + EPS)
        u = (hv * r * g_ref[...]).astype(BF16)
        proj_ref[...] = _dot(u, wins[...], NT)

        ext = jnp.concatenate([carry[...], proj_ref[:, 0:pwid]], axis=0)
        carry[...] = proj_ref[tm - MAX_WINDOW:tm, 0:pwid]
        for gi, w in enumerate(POOL_WINDOWS):
            cols = slice(gi * GROUP, (gi + 1) * GROUP)
            xg = ext[:, cols]
            ws = _window_sums(xg, w, True)[MAX_WINDOW:, :]
            pooled = ws / _pool_counts(i, tm, w) - xg[MAX_WINDOW:, :]
            z = _dot(pooled.astype(BF16), pw_ref[gi].astype(BF16), NN)
            mbuf[:, cols] = (z * ps_ref[:, cols]).astype(BF16)

        cos, sin = cos_ref[...], sin_ref[...]
        for h in range(RET_HEADS):
            cq = slice(pwid + h * GROUP, pwid + (h + 1) * GROUP)
            ck = slice(pwid + rwid + h * GROUP, pwid + rwid + (h + 1) * GROUP)
            cv = slice(pwid + 2 * rwid + h * GROUP, pwid + 2 * rwid + (h + 1) * GROUP)
            cg = slice(pwid + 3 * rwid + h * GROUP, pwid + 3 * rwid + (h + 1) * GROUP)
            ch = slice(h * GROUP, (h + 1) * GROUP)
            qr = _rope(proj_ref[:, cq], cos, sin)
            kr = _rope(proj_ref[:, ck], cos, sin) * (GROUP ** -0.5)
            vb = proj_ref[:, cv].astype(BF16)
            for n in range(nck):
                rows = slice(n * GROUP, (n + 1) * GROUP)
                qc, kc, vc = qr[rows], kr[rows], vb[rows]
                rb = state[h]
                rs_ref[n, h] = rb
                p = (_dot(qc.astype(BF16), kc.astype(BF16), NT) * dec_ref[h]).astype(BF16)
                o = _dot(p, vc, NN) + _dot((qc * qh_ref[h]).astype(BF16), rb.astype(BF16), NN)
                state[h] = cd[h] * rb + _dot((kc * kt_ref[h]).astype(BF16), vc, TN)
                o_ref[rows, ch] = o
                on = o * lax.rsqrt(jnp.mean(o * o, axis=-1, keepdims=True) + EPS)
                gv = proj_ref[rows, cg]
                mbuf[rows, pwid + h * GROUP:pwid + (h + 1) * GROUP] = (
                    gv * _sigmoid(gv) * (on * rg_ref[:, ch])
                ).astype(BF16)
        h2_ref[...] = hv + _dot(mbuf[...], wouts[...], NN)

    tile = pl.BlockSpec((tm, d), lambda i: (i, 0))
    full = lambda shape: pl.BlockSpec(shape, lambda i: (0,) * len(shape))
    return _grid_call(
        body,
        carries,
        name="mix_fwd",
        steps=s // tm,
        in_specs=[
            tile, full((1, d)), ANY, ANY,
            full((N_POOL_GROUPS, GROUP, GROUP)), full((1, pwid)), full((1, rwid)),
            pl.BlockSpec((tm, GROUP), lambda i: (i, 0)), pl.BlockSpec((tm, GROUP), lambda i: (i, 0)),
            full((RET_HEADS, GROUP, GROUP)), full((RET_HEADS, GROUP, GROUP)), full((RET_HEADS, GROUP, GROUP)),
        ],
        out_specs=[
            tile,
            pl.BlockSpec((tm, inw), lambda i: (i, 0)),
            pl.BlockSpec((tm, rwid), lambda i: (i, 0)),
            pl.BlockSpec((nck, RET_HEADS, GROUP, GROUP), lambda i: (i, 0, 0, 0)),
        ],
        out_shape=[
            jax.ShapeDtypeStruct((s, d), F32),
            jax.ShapeDtypeStruct((s, inw), F32),
            jax.ShapeDtypeStruct((s, rwid), F32),
            jax.ShapeDtypeStruct((s // GROUP, RET_HEADS, GROUP, GROUP), F32),
        ],
        scratch_shapes=[
            pltpu.VMEM((inw, d), BF16), pltpu.VMEM((d, d), BF16),
            pltpu.VMEM((RET_HEADS, GROUP, GROUP), F32), pltpu.VMEM((MAX_WINDOW, pwid), F32),
            pltpu.VMEM((tm, d), BF16), pltpu.SemaphoreType.DMA((2 * NDEV,)),
        ],
        args=[h1, gain, weights[0], weights[1], pool_w, pool_scale, ret_gain,
              consts["cos"], consts["sin"], consts["decay"], consts["ktail"], consts["qhead"]],
    )


def _mix_bwd(dh2, h1, proj, o_saved, rsave, gain, weights, pool_w, pool_scale, ret_gain, consts, carries=()):
    s, d = h1.shape
    pwid = N_POOL_GROUPS * GROUP
    rwid = RET_HEADS * GROUP
    inw = pwid + 4 * rwid
    tm = min(256, s)
    nck = tm // GROUP
    nt = s // tm
    cd = consts["chunk_decay"]
    halo_per_tile = tm // MAX_WINDOW

    def body(dh2_ref, h_ref, proj_ref, halo_ref, o_ref, rs_ref, g_ref, bin_, bout, pw_ref, ps_ref, rg_ref,
             cos_ref, sin_ref, dec_ref, kt_ref, qh_ref,
             dh1_ref, dproj_ref, u_ref, m_ref, dpw_ref, dps_ref, drg_ref, dg_ref,
             wins, wouts, dstate, carry, dm, dpj, sems):
        i = pl.program_id(0)
        tile = nt - 1 - i

        @pl.when(i == 0)
        def _():
            for cp in _load_weights(((bin_, wins), (bout, wouts)), sems):
                cp.wait()
            dstate[...] = jnp.zeros_like(dstate)
            carry[...] = jnp.zeros_like(carry)
            for ref in (dpw_ref, dps_ref, drg_ref, dg_ref):
                ref[...] = jnp.zeros_like(ref)

        dh2v = dh2_ref[...]
        dm[...] = _dot(dh2v.astype(BF16), wouts[...], NT)
        hv = h_ref[...]
        g = g_ref[...]
        r = lax.rsqrt(jnp.mean(hv * hv, axis=-1, keepdims=True) + EPS)
        uh = hv * r
        u_ref[...] = (uh * g).astype(BF16)

        halo = jnp.where(tile == 0, 0.0, halo_ref[...])
        ext = jnp.concatenate([halo, proj_ref[:, 0:pwid]], axis=0)
        next_dpn = carry[...]
        for gi, w in enumerate(POOL_WINDOWS):
            cols = slice(gi * GROUP, (gi + 1) * GROUP)
            xg = ext[:, cols]
            cnt = _pool_counts(tile, tm, w)
            pooled = (_window_sums(xg, w, True)[MAX_WINDOW:, :] / cnt - xg[MAX_WINDOW:, :]).astype(BF16)
            pwb = pw_ref[gi].astype(BF16)
            z = _dot(pooled, pwb, NN)
            scale = ps_ref[:, cols]
            m_ref[:, cols] = (z * scale).astype(BF16)
            da = dm[:, cols]
            dps_ref[:, cols] += jnp.sum(da * z, axis=0, keepdims=True)
            dz = (da * scale).astype(BF16)
            dpw_ref[gi] += _dot(pooled, dz, TN)
            dpl = _dot(dz, pwb, NT)
            dpn = dpl / cnt
            ext2 = jnp.concatenate([dpn, next_dpn[:, cols]], axis=0)
            dpj[:, cols] = (_window_sums(ext2, w, False)[0:tm, :] - dpl).astype(BF16)
            carry[:, cols] = dpn[0:MAX_WINDOW, :]

        cos, sin = cos_ref[...], sin_ref[...]
        for h in range(RET_HEADS):
            cq = slice(pwid + h * GROUP, pwid + (h + 1) * GROUP)
            ck = slice(pwid + rwid + h * GROUP, pwid + rwid + (h + 1) * GROUP)
            cv = slice(pwid + 2 * rwid + h * GROUP, pwid + 2 * rwid + (h + 1) * GROUP)
            cg = slice(pwid + 3 * rwid + h * GROUP, pwid + 3 * rwid + (h + 1) * GROUP)
            ch = slice(h * GROUP, (h + 1) * GROUP)
            qr = _rope(proj_ref[:, cq], cos, sin)
            kr = _rope(proj_ref[:, ck], cos, sin) * (GROUP ** -0.5)
            vb = proj_ref[:, cv].astype(BF16)
            gv = proj_ref[:, cg]
            ov = o_ref[:, ch]
            ro = lax.rsqrt(jnp.mean(ov * ov, axis=-1, keepdims=True) + EPS)
            on = ov * ro
            rg = rg_ref[:, ch]
            db = dm[:, pwid + h * GROUP:pwid + (h + 1) * GROUP]
            sg = _sigmoid(gv)
            sl = gv * sg
            m_ref[:, pwid + h * GROUP:pwid + (h + 1) * GROUP] = (sl * (on * rg)).astype(BF16)
            dpj[:, cg] = (db * (on * rg) * (sg * (1.0 + gv * (1.0 - sg)))).astype(BF16)
            drg_ref[:, ch] += jnp.sum(db * sl * on, axis=0, keepdims=True)
            don = db * sl * rg
            do = (ro * (don - on * jnp.mean(don * on, axis=-1, keepdims=True))).astype(BF16)
            for n in reversed(range(nck)):
                rows = slice(n * GROUP, (n + 1) * GROUP)
                qc, kc, vc, dob = qr[rows], kr[rows], vb[rows], do[rows]
                qcb, kcb = qc.astype(BF16), kc.astype(BF16)
                qh = (qc * qh_ref[h]).astype(BF16)
                kt = (kc * kt_ref[h]).astype(BF16)
                rn = rs_ref[n, h].astype(BF16)
                dnext = dstate[h]
                dnb = dnext.astype(BF16)
                dec = dec_ref[h]
                p = (_dot(qcb, kcb, NT) * dec).astype(BF16)
                ds = (_dot(dob, vc, NT) * dec).astype(BF16)
                dv = _dot(p, dob, TN) + _dot(kt, dnb, NN)
                dq = _dot(ds, kcb, NN) + _dot(dob, rn, NT) * qh_ref[h]
                dk = _dot(ds, qcb, TN) + _dot(vc, dnb, NT) * kt_ref[h]
                dstate[h] = cd[h] * dnext + _dot(qh, dob, TN)
                dpj[rows, cq] = _rope_bwd(dq, cos[rows], sin[rows]).astype(BF16)
                dpj[rows, ck] = _rope_bwd(dk * (GROUP ** -0.5), cos[rows], sin[rows]).astype(BF16)
                dpj[rows, cv] = dv.astype(BF16)

        dproj_ref[...] = dpj[...]
        du = _dot(dpj[...], wins[...], NN)
        dg_ref[...] += jnp.sum(du * uh, axis=0, keepdims=True)
        dn = du * g
        dh1_ref[...] = dh2v + r * (dn - uh * jnp.mean(dn * uh, axis=-1, keepdims=True))

    rev = lambda i: (nt - 1 - i, 0)
    tile = pl.BlockSpec((tm, d), rev)
    full = lambda shape: pl.BlockSpec(shape, lambda i: (0,) * len(shape))
    return _grid_call(
        body,
        carries,
        name="mix_bwd",
        steps=nt,
        in_specs=[
            tile, tile,
            pl.BlockSpec((tm, inw), rev),
            pl.BlockSpec((MAX_WINDOW, pwid), lambda i: (jnp.maximum((nt - 1 - i) * halo_per_tile - 1, 0), 0)),
            pl.BlockSpec((tm, rwid), rev),
            pl.BlockSpec((nck, RET_HEADS, GROUP, GROUP), lambda i: (nt - 1 - i, 0, 0, 0)),
            full((1, d)), ANY, ANY,
            full((N_POOL_GROUPS, GROUP, GROUP)), full((1, pwid)), full((1, rwid)),
            pl.BlockSpec((tm, GROUP), rev), pl.BlockSpec((tm, GROUP), rev),
            full((RET_HEADS, GROUP, GROUP)), full((RET_HEADS, GROUP, GROUP)), full((RET_HEADS, GROUP, GROUP)),
        ],
        out_specs=[
            tile, pl.BlockSpec((tm, inw), rev), tile, tile,
            full((N_POOL_GROUPS, GROUP, GROUP)), full((1, pwid)), full((1, rwid)), full((1, d)),
        ],
        out_shape=[
            jax.ShapeDtypeStruct((s, d), F32),
            jax.ShapeDtypeStruct((s, inw), BF16),
            jax.ShapeDtypeStruct((s, d), BF16),
            jax.ShapeDtypeStruct((s, d), BF16),
            jax.ShapeDtypeStruct((N_POOL_GROUPS, GROUP, GROUP), F32),
            jax.ShapeDtypeStruct((1, pwid), F32),
            jax.ShapeDtypeStruct((1, rwid), F32),
            jax.ShapeDtypeStruct((1, d), F32),
        ],
        scratch_shapes=[
            pltpu.VMEM((inw, d), BF16), pltpu.VMEM((d, d), BF16),
            pltpu.VMEM((RET_HEADS, GROUP, GROUP), F32), pltpu.VMEM((MAX_WINDOW, pwid), F32),
            pltpu.VMEM((tm, d), F32), pltpu.VMEM((tm, inw), BF16), pltpu.SemaphoreType.DMA((2 * NDEV,)),
        ],
        args=[dh2, h1, proj, proj, o_saved, rsave, gain, weights[0], weights[1], pool_w, pool_scale, ret_gain,
              consts["cos"], consts["sin"], consts["decay"], consts["ktail"], consts["qhead"]],
    )


def _adam(w, g, m, v):
    m = ADAM_B1 * m + (1.0 - ADAM_B1) * g
    v = ADAM_B2 * v + (1.0 - ADAM_B2) * jnp.square(g)
    m_hat = m / (1.0 - ADAM_B1 ** ADAM_STEP)
    v_hat = v / (1.0 - ADAM_B2 ** ADAM_STEP)
    delta = -ADAM_LR * (m_hat / (jnp.sqrt(v_hat) + ADAM_EPS) + ADAM_WD * w)
    return delta, m, v


def _pair_sum(grads, theirs, core, name):
    n, rows, d = theirs.shape
    tr = _row_tile(rows, 512)

    def body(c_ref, a_ref, b_ref, o_ref):
        o_ref[...] = (a_ref[0].astype(F32) + b_ref[...].astype(F32)).astype(BF16)

    spec = pl.BlockSpec((1, tr, d), lambda j, i, c: (j, i, 0))
    return _call(
        body,
        name=name,
        grid_spec=pltpu.PrefetchScalarGridSpec(
            num_scalar_prefetch=1,
            grid=(n, rows // tr),
            in_specs=[pl.BlockSpec((1, 1, tr, d), lambda j, i, c: (c[0], j, i, 0)), spec],
            out_specs=spec,
        ),
        out_shape=jax.ShapeDtypeStruct(theirs.shape, BF16),
        compiler_params=_seq(2),
    )(core, grads, theirs)


def _adamw_big(w, parts, m, v, name):
    rows, d = w.shape
    tr = _row_tile(rows, 512)

    def body(w_ref, p_ref, m_ref, v_ref, g_ref, d_ref, nm_ref, nv_ref):
        g = p_ref[0].astype(F32)
        for q in range(1, NCHIP):
            g = g + p_ref[q].astype(F32)
        g_ref[...] = g
        d_ref[...], nm_ref[...], nv_ref[...] = _adam(w_ref[...], g, m_ref[...], v_ref[...])

    spec = pl.BlockSpec((tr, d), lambda i: (i, 0))
    return _call(
        body,
        name=name,
        grid=(rows // tr,),
        in_specs=[spec, pl.BlockSpec((NCHIP, tr, d), lambda i: (0, i, 0)), spec, spec],
        out_specs=[spec] * 4,
        out_shape=[jax.ShapeDtypeStruct((rows, d), F32)] * 4,
        compiler_params=_seq(1),
    )(w, parts, m, v)


def _adamw_small(stats_all, pw_all, ws, ms, vs, d, pwid):
    nsmall = len(ws)

    def body(*refs):
        st_ref, pwa_ref = refs[0], refs[1]
        w_refs = refs[2:2 + nsmall]
        m_refs = refs[2 + nsmall:2 + 2 * nsmall]
        v_refs = refs[2 + 2 * nsmall:2 + 3 * nsmall]
        outs = refs[2 + 3 * nsmall:]
        st = st_ref[0]
        pwg = pwa_ref[0]
        for q in range(1, NDEV):
            st = st + st_ref[q]
            pwg = pwg + pwa_ref[q]
        grads = [st[0:1, :], st[1:2, :], st[2:3, :], st[3:4, :], st[4:5, 0:pwid], st[4:5, pwid:2 * pwid], pwg]
        outs[0][...] = jnp.zeros((1, GROUP), F32) + jnp.sum(st[5:6, :])
        for j in range(nsmall):
            delta, nm, nv = _adam(w_refs[j][...], grads[j], m_refs[j][...], v_refs[j][...])
            outs[1 + 4 * j][...] = grads[j]
            outs[2 + 4 * j][...] = delta
            outs[3 + 4 * j][...] = nm
            outs[4 + 4 * j][...] = nv

    out_shape = [jax.ShapeDtypeStruct((1, GROUP), F32)]
    for w in ws:
        out_shape += [jax.ShapeDtypeStruct(w.shape, F32)] * 4
    return _call(body, name="adamw_small", out_shape=out_shape, compiler_params=_params())(
        stats_all, pw_all, *ws, *ms, *vs
    )


def kernel(x, ffn1_norm, ffn1_w1, ffn1_w3, ffn1_w2, mix_norm, w_in, pool_w, pool_scale, ret_norm, w_out, ffn2_norm, ffn2_w1, ffn2_w3, ffn2_w2, final_norm, loss_target, m_ffn1_norm, m_ffn1_w1, m_ffn1_w3, m_ffn1_w2, m_mix_norm, m_w_in, m_pool_w, m_pool_scale, m_ret_norm, m_w_out, m_ffn2_norm, m_ffn2_w1, m_ffn2_w3, m_ffn2_w2, m_final_norm, v_ffn1_norm, v_ffn1_w1, v_ffn1_w3, v_ffn1_w2, v_mix_norm, v_w_in, v_pool_w, v_pool_scale, v_ret_norm, v_w_out, v_ffn2_norm, v_ffn2_w1, v_ffn2_w3, v_ffn2_w2, v_final_norm):
    s, d = x.shape[1], x.shape[2]
    ffn = ffn1_w1.shape[2] * NDEV
    pwid = pool_scale.shape[1]
    xs, tgt = x[0], loss_target[0]
    consts = _mix_constants(s)
    pw3 = pool_w[0]
    fnorm = final_norm.reshape(1, d)

    rows_of = lambda w, transposed: (w[0].T if transposed else w[0]).astype(BF16)
    send_f1 = [rows_of(ffn1_w1, True), rows_of(ffn1_w3, True), rows_of(ffn1_w2, False)]
    send_mix = [rows_of(w_in, True), rows_of(w_out, False)]
    send_f2 = [rows_of(ffn2_w1, True), rows_of(ffn2_w3, True), rows_of(ffn2_w2, False)]

    (w_f1,) = _comm_call([_Gather(send_f1)], "gather_ffn1")
    (h1, a1, b1), (more,) = _ffn_fwd(xs, ffn1_norm, w_f1, ffn, carries=[_Gather(send_mix + send_f2[:1])])
    w_mix = more[:2]
    (h2, proj, o_saved, rsave), (rest,) = _mix_fwd(
        h1, mix_norm, w_mix, pw3, pool_scale, ret_norm, consts, carries=[_Gather(send_f2[1:])]
    )
    w_f2 = more[2:] + rest
    (dh3, a2, b2, dgf, loss_cols), _ = _ffn_fwd(h2, ffn2_norm, w_f2, ffn, head=(fnorm, tgt))

    core = lax.axis_index("c").astype(jnp.int32).reshape(1)

    def pair_reduced(g, tag):
        ((theirs,),) = _comm_call([_SiblingSwap([g])], "swap_" + tag)
        return _pair_sum(g, theirs, core, "pair_sum_" + tag)

    (dh2, da2, db2, hm2, n2, dg2), _ = _ffn_bwd(dh3, h2, a2, b2, ffn2_norm, w_f2, ffn, "ffn2_bwd")
    g, _ = _wgrad(da2, n2, 1.0, "ffn2_w1_grad")
    sum_f2w1 = pair_reduced(g, "ffn2_w1")
    g, ((parts_f2w1,),) = _wgrad(db2, n2, 1.0, "ffn2_w3_grad", carries=[_ChipScatter([sum_f2w1])])
    sum_f2w3 = pair_reduced(g, "ffn2_w3")
    g, ((parts_f2w3,),) = _wgrad(hm2, dh3, 0.5, "ffn2_w2_grad", carries=[_ChipScatter([sum_f2w3])])
    sum_f2w2 = pair_reduced(g, "ffn2_w2")

    (dh1, dproj, u, mm, dpw, dps, drg, dgm), ((parts_f2w2,),) = _mix_bwd(
        dh2, h1, proj, o_saved, rsave, mix_norm, w_mix, pw3, pool_scale, ret_norm, consts,
        carries=[_ChipScatter([sum_f2w2])],
    )
    g, _ = _wgrad(dproj, u, 1.0, "w_in_grad")
    sum_in = pair_reduced(g, "w_in")
    g, _ = _wgrad(mm, dh2, 1.0, "w_out_grad")
    sum_out = pair_reduced(g, "w_out")

    (dx, da1, db1, hm1, n1, dg1), ((parts_in, parts_out),) = _ffn_bwd(
        dh1, xs, a1, b1, ffn1_norm, w_f1, ffn, "ffn1_bwd", carries=[_ChipScatter([sum_in, sum_out])]
    )
    g, _ = _wgrad(hm1, dh1, 0.5, "ffn1_w2_grad")
    sum_f1w2 = pair_reduced(g, "ffn1_w2")
    g, ((parts_f1w2,),) = _wgrad(da1, n1, 1.0, "ffn1_w1_grad", carries=[_ChipScatter([sum_f1w2])])
    sum_f1w1 = pair_reduced(g, "ffn1_w1")
    g, ((parts_f1w1,),) = _wgrad(db1, n1, 1.0, "ffn1_w3_grad", carries=[_ChipScatter([sum_f1w1])])
    sum_f1w3 = pair_reduced(g, "ffn1_w3")

    stats = jnp.concatenate(
        [dg1, dgm, dg2, dgf, jnp.concatenate([dps, drg], axis=1), loss_cols, jnp.zeros((2, d), F32)], axis=0
    )
    (parts_f1w3,), (stats_all, pw_all) = _comm_call(
        [_ChipScatter([sum_f1w3]), _GatherDirect([stats, dpw.reshape(N_POOL_GROUPS * GROUP, GROUP)])], "scatter_last"
    )

    big = (
        (ffn1_w1, m_ffn1_w1, v_ffn1_w1, parts_f1w1, True),
        (ffn1_w3, m_ffn1_w3, v_ffn1_w3, parts_f1w3, True),
        (ffn1_w2, m_ffn1_w2, v_ffn1_w2, parts_f1w2, False),
        (w_in, m_w_in, v_w_in, parts_in, True),
        (w_out, m_w_out, v_w_out, parts_out, False),
        (ffn2_w1, m_ffn2_w1, v_ffn2_w1, parts_f2w1, True),
        (ffn2_w3, m_ffn2_w3, v_ffn2_w3, parts_f2w3, True),
        (ffn2_w2, m_ffn2_w2, v_ffn2_w2, parts_f2w2, False),
    )
    big_out = []
    for j, (w, m, v, parts, t) in enumerate(big):
        view = (lambda a: a[0].T) if t else (lambda a: a[0])
        back = (lambda a: a.T[None]) if t else (lambda a: a[None])
        big_out.append([back(a) for a in _adamw_big(view(w), parts, view(m), view(v), "adamw_%d" % j)])

    small_w = (ffn1_norm, mix_norm, ffn2_norm, fnorm, pool_scale, ret_norm, pw3.reshape(-1, GROUP))
    small_m = (m_ffn1_norm, m_mix_norm, m_ffn2_norm, m_final_norm.reshape(1, d), m_pool_scale, m_ret_norm, m_pool_w.reshape(-1, GROUP))
    small_v = (v_ffn1_norm, v_mix_norm, v_ffn2_norm, v_final_norm.reshape(1, d), v_pool_scale, v_ret_norm, v_pool_w.reshape(-1, GROUP))
    res = _adamw_small(stats_all, pw_all, small_w, small_m, small_v, d, pwid)
    loss = res[0][0, 0]
    small_out = [list(res[1 + 4 * j:5 + 4 * j]) for j in range(len(small_w))]
    small_out[3] = [a.reshape(d) for a in small_out[3]]
    small_out[6] = [a.reshape(pool_w.shape) for a in small_out[6]]

    order = [small_out[0], big_out[0], big_out[1], big_out[2], small_out[1], big_out[3], small_out[6], small_out[4],
             small_out[5], big_out[4], small_out[2], big_out[5], big_out[6], big_out[7], small_out[3]]
    result = [loss, dx[None]]
    for kind in range(4):
        result += [t[kind] for t in order]
    return tuple(result)
```

```python
import numpy as np
import jax
import jax.numpy as jnp
from jax import lax
from jax.experimental import pallas as pl
from jax.experimental.pallas import tpu as pltpu

F32 = jnp.float32
BF16 = jnp.bfloat16

NDEV = 8
NCHIP = 4
EPS = 1e-6
N_POOL_GROUPS = 4
POOL_WINDOWS = (2, 4, 8, 16)
MAX_WINDOW = 16
GROUP = 128
RET_HEADS = 4
ROPE_BASE = 10000.0
ADAM_LR = 0.001
ADAM_B1 = 0.9
ADAM_B2 = 0.999
ADAM_EPS = 1e-08
ADAM_WD = 0.01
ADAM_STEP = 10

VMEM_LIMIT = 56 * 1024 * 1024
FFN_CHUNK = 256
LOAD_PIECES = 4
ROW_BAND = 32

NT = (((1,), (1,)), ((), ()))
NN = (((1,), (0,)), ((), ()))
TN = (((0,), (0,)), ((), ()))

ANY = pl.BlockSpec(memory_space=pl.ANY)


def _dot(a, b, dims):
    return lax.dot_general(a, b, dims, preferred_element_type=F32)


def _call(body, **kw):
    return pl.pallas_call(body, **kw)


def _params(**kw):
    return pltpu.CompilerParams(vmem_limit_bytes=VMEM_LIMIT, **kw)


def _seq(n):
    return _params(dimension_semantics=("arbitrary",) * n)


def _peer(k):
    x, y, c = lax.axis_index("x"), lax.axis_index("y"), lax.axis_index("c")
    return (1 - x if k & 4 else x, 1 - y if k & 2 else y, 1 - c if k & 1 else c)


def _flat(pos):
    return 4 * pos[0] + 2 * pos[1] + pos[2]


def _chip(pos):
    return 2 * pos[0] + pos[1]


def _row_tile(rows, cap):
    return max(t for t in range(16, min(rows, cap) + 1, 16) if rows % t == 0)


def _pieces(rows, n):
    tiles = rows // 16
    cuts = [16 * (tiles * q // n) for q in range(n + 1)]
    return [(a, b - a) for a, b in zip(cuts[:-1], cuts[1:])]


def _load_weights(parts, sems):
    copies = []
    for buf, dst in parts:
        rows = buf.shape[1]
        for p in range(NDEV):
            for at, size in _pieces(rows, LOAD_PIECES):
                cp = pltpu.make_async_copy(
                    buf.at[p, pl.ds(at, size), :], dst.at[pl.ds(p * rows + at, size), :], sems.at[len(copies)]
                )
                cp.start()
                copies.append(cp)
    return copies


def _sigmoid(a):
    return 1.0 / (1.0 + jnp.exp(-a))


def _remote(src, dst, send_sem, recv_sem, to):
    return pltpu.make_async_remote_copy(
        src_ref=src, dst_ref=dst, send_sem=send_sem, recv_sem=recv_sem, device_id=to, device_id_type=pl.DeviceIdType.MESH
    )


class _Gather:
    chips = (4, 2, 6)

    def __init__(self, shards):
        n = len(shards)
        self.operands = list(shards)
        self.out_shape = [jax.ShapeDtypeStruct((NDEV,) + a.shape, a.dtype) for a in shards]
        self.sems = [pltpu.SemaphoreType.DMA((7 * n,)), pltpu.SemaphoreType.DMA((7 * n,)), pltpu.SemaphoreType.DMA((n,))]

    def _copy(self, t, k, block, to, ins, outs, sems, own=False):
        dst = outs[t].at[_flat(block)]
        return _remote(ins[t] if own else dst, dst, sems[0].at[7 * t + k], sems[1].at[7 * t + k], to)

    def begin(self, ins, outs, sems):
        me, sibling = _peer(0), _peer(1)
        for t in range(len(ins)):
            pltpu.make_async_copy(ins[t], outs[t].at[_flat(me)], sems[2].at[t]).start()
            self._copy(t, 0, me, sibling, ins, outs, sems, own=True).start()
            for j, k in enumerate(self.chips):
                self._copy(t, 1 + j, me, _peer(k), ins, outs, sems, own=True).start()

    def mid(self, ins, outs, sems):
        me, sibling = _peer(0), _peer(1)
        for t in range(len(ins)):
            for j, k in enumerate(self.chips):
                self._copy(t, 1 + j, _peer(k), me, ins, outs, sems).wait_recv()
                self._copy(t, 4 + j, _peer(k), sibling, ins, outs, sems).start()

    def end(self, ins, outs, sems):
        me, sibling = _peer(0), _peer(1)
        for t in range(len(ins)):
            self._copy(t, 0, sibling, me, ins, outs, sems).wait_recv()
            for j, k in enumerate(self.chips):
                self._copy(t, 4 + j, _peer(k ^ 1), me, ins, outs, sems).wait_recv()
            for k in range(7):
                self._copy(t, k, me, me, ins, outs, sems).wait_send()
            pltpu.make_async_copy(ins[t], outs[t].at[_flat(me)], sems[2].at[t]).wait()


class _GatherDirect:
    def __init__(self, arrays):
        n = len(arrays)
        self.operands = list(arrays)
        self.out_shape = [jax.ShapeDtypeStruct((NDEV,) + a.shape, a.dtype) for a in arrays]
        self.sems = [pltpu.SemaphoreType.DMA((7 * n,)), pltpu.SemaphoreType.DMA((7 * n,)), pltpu.SemaphoreType.DMA((n,))]

    def begin(self, ins, outs, sems):
        mine = _flat(_peer(0))
        for t in range(len(ins)):
            pltpu.make_async_copy(ins[t], outs[t].at[mine], sems[2].at[t]).start()
            for k in range(1, NDEV):
                _remote(ins[t], outs[t].at[mine], sems[0].at[7 * t + k - 1], sems[1].at[7 * t + k - 1], _peer(k)).start()

    def mid(self, ins, outs, sems):
        pass

    def end(self, ins, outs, sems):
        mine = _flat(_peer(0))
        for t in range(len(ins)):
            for k in range(1, NDEV):
                cp = _remote(ins[t], outs[t].at[_flat(_peer(k))], sems[0].at[7 * t + k - 1], sems[1].at[7 * t + k - 1], _peer(k))
                cp.wait_recv()
                cp.wait_send()
            pltpu.make_async_copy(ins[t], outs[t].at[mine], sems[2].at[t]).wait()


class _SiblingSwap:
    pieces = 2

    def __init__(self, grads):
        n = len(grads) * NCHIP * self.pieces
        self.operands = list(grads)
        self.out_shape = [jax.ShapeDtypeStruct(g.shape[1:], g.dtype) for g in grads]
        self.sems = [pltpu.SemaphoreType.DMA((n,)), pltpu.SemaphoreType.DMA((n,))]

    def _copies(self, ins, outs, sems):
        c = lax.axis_index("c")
        copies = []
        for t in range(len(ins)):
            rows = ins[t].shape[2] // self.pieces
            for j in range(NCHIP):
                for q in range(self.pieces):
                    part = pl.ds(q * rows, rows)
                    k = len(copies)
                    copies.append(
                        _remote(ins[t].at[1 - c, j, part, :], outs[t].at[j, part, :], sems[0].at[k], sems[1].at[k], _peer(1))
                    )
        return copies

    def begin(self, ins, outs, sems):
        for cp in self._copies(ins, outs, sems):
            cp.start()

    def mid(self, ins, outs, sems):
        pass

    def end(self, ins, outs, sems):
        for cp in self._copies(ins, outs, sems):
            cp.wait_recv()
            cp.wait_send()


class _ChipScatter:
    pieces = 2

    def __init__(self, sums):
        n = len(sums) * NCHIP * self.pieces
        self.operands = list(sums)
        self.out_shape = [jax.ShapeDtypeStruct(a.shape, a.dtype) for a in sums]
        self.sems = [pltpu.SemaphoreType.DMA((n,)), pltpu.SemaphoreType.DMA((n,))]

    def _copies(self, ins, outs, sems, arriving):
        mine = _chip(_peer(0))
        copies = []
        for t in range(len(ins)):
            rows = ins[t].shape[1] // self.pieces
            for k in (0, 4, 2, 6):
                other = _chip(_peer(k))
                for q in range(self.pieces):
                    part = pl.ds(q * rows, rows)
                    at = len(copies)
                    if k == 0:
                        cp = pltpu.make_async_copy(ins[t].at[mine, part, :], outs[t].at[mine, part, :], sems[0].at[at])
                    else:
                        landing = outs[t].at[other if arriving else mine, part, :]
                        cp = _remote(ins[t].at[other, part, :], landing, sems[0].at[at], sems[1].at[at], _peer(k))
                    copies.append(cp)
        return copies

    def begin(self, ins, outs, sems):
        for cp in self._copies(ins, outs, sems, False):
            cp.start()

    def mid(self, ins, outs, sems):
        pass

    def end(self, ins, outs, sems):
        for at, cp in enumerate(self._copies(ins, outs, sems, True)):
            if at % (NCHIP * self.pieces) < self.pieces:
                cp.wait()
            else:
                cp.wait_recv()
                cp.wait_send()


def _split_refs(refs, counts):
    out, at = [], 0
    for n in counts:
        out.append(refs[at:at + n])
        at += n
    return out


def _comm_call(carries, name):
    nin = [len(c.operands) for c in carries]
    nout = [len(c.out_shape) for c in carries]
    nsem = [len(c.sems) for c in carries]

    def body(*refs):
        ins, outs, sems = _split_refs(refs, (sum(nin), sum(nout), sum(nsem)))
        parts = list(zip(carries, _split_refs(ins, nin), _split_refs(outs, nout), _split_refs(sems, nsem)))
        for c, i, o, s in parts:
            c.begin(i, o, s)
        for c, i, o, s in parts:
            c.mid(i, o, s)
        for c, i, o, s in parts:
            c.end(i, o, s)

    res = _call(
        body,
        name=name,
        out_shape=[sh for c in carries for sh in c.out_shape],
        in_specs=[ANY] * sum(nin),
        out_specs=[ANY] * sum(nout),
        scratch_shapes=[sm for c in carries for sm in c.sems],
        compiler_params=pltpu.CompilerParams(has_side_effects=True),
    )(*[a for c in carries for a in c.operands])
    return _split_refs(list(res), nout)


def _grid_call(body, carries, *, name, grid, in_specs, out_specs, out_shape, scratch_shapes, args):
    ni, no, ns = len(in_specs), len(out_specs), len(scratch_shapes)
    nin = [len(c.operands) for c in carries]
    nout = [len(c.out_shape) for c in carries]
    nsem = [len(c.sems) for c in carries]
    steps = int(np.prod(grid))
    mid_step = max(steps - 2, 0)

    def wrapped(*refs):
        ins, cins, outs, couts, scr, csems = _split_refs(refs, (ni, sum(nin), no, sum(nout), ns, sum(nsem)))
        if not carries:
            return body(*ins, *outs, *scr)
        parts = list(zip(carries, _split_refs(cins, nin), _split_refs(couts, nout), _split_refs(csems, nsem)))
        step = pl.program_id(0)
        for axis in range(1, len(grid)):
            step = step * grid[axis] + pl.program_id(axis)

        @pl.when(step == 0)
        def _():
            for c, i, o, s in parts:
                c.begin(i, o, s)

        body(*ins, *outs, *scr)

        @pl.when(step == mid_step)
        def _():
            for c, i, o, s in parts:
                c.mid(i, o, s)

        @pl.when(step == steps - 1)
        def _():
            for c, i, o, s in parts:
                c.end(i, o, s)

    res = _call(
        wrapped,
        name=name,
        grid=tuple(grid),
        in_specs=list(in_specs) + [ANY] * sum(nin),
        out_specs=list(out_specs) + [ANY] * sum(nout),
        out_shape=list(out_shape) + [sh for c in carries for sh in c.out_shape],
        scratch_shapes=list(scratch_shapes) + [sm for c in carries for sm in c.sems],
        compiler_params=_seq(len(grid)),
    )(*args, *[a for c in carries for a in c.operands])
    res = list(res)
    return res[:no], _split_refs(res[no:], nout)


def _chunks(width):
    return [(at, min(FFN_CHUNK, width - at)) for at in range(0, width, FFN_CHUNK)]


def _ffn_fwd(x, gain, weights, ffn, head=None, carries=()):
    s, d = x.shape
    tm = min(512, s)

    def body(*refs):
        if head is None:
            x_ref, g_ref, b1, b3, b2, h_ref, a_ref, b_ref, w1s, w3s, w2s, sems = refs
        else:
            x_ref, g_ref, b1, b3, b2, gf_ref, t_ref, h_ref, a_ref, b_ref, dgf_ref, loss_ref, w1s, w3s, w2s, sems = refs
        i = pl.program_id(0)

        @pl.when(i == 0)
        def _():
            for cp in _load_weights(((b1, w1s), (b3, w3s), (b2, w2s)), sems):
                cp.wait()
            if head is not None:
                dgf_ref[...] = jnp.zeros_like(dgf_ref)
                loss_ref[...] = jnp.zeros_like(loss_ref)

        xv = x_ref[...]
        r = lax.rsqrt(jnp.mean(xv * xv, axis=-1, keepdims=True) + EPS)
        n = (xv * r * g_ref[...]).astype(BF16)
        acc = jnp.zeros((tm, d), F32)
        for at, width in _chunks(ffn):
            cols = slice(at, at + width)
            a = _dot(n, w1s[cols, :], NT)
            b = _dot(n, w3s[cols, :], NT)
            a_ref[:, cols] = a.astype(BF16)
            b_ref[:, cols] = b.astype(BF16)
            hm = (a * _sigmoid(a) * b).astype(BF16)
            acc = acc + _dot(hm, w2s[cols, :], NN)
        h = xv + 0.5 * acc
        if head is None:
            h_ref[...] = h
        else:
            rf = lax.rsqrt(jnp.mean(h * h, axis=-1, keepdims=True) + EPS)
            nh = h * rf
            gf = gf_ref[...]
            err = nh * gf - t_ref[...]
            loss_ref[...] += jnp.sum(err * err, axis=0, keepdims=True) * (0.5 / d)
            dy = err * (1.0 / d)
            dgf_ref[...] += jnp.sum(dy * nh, axis=0, keepdims=True)
            dn = dy * gf
            h_ref[...] = rf * (dn - nh * jnp.mean(dn * nh, axis=-1, keepdims=True))

    tile = pl.BlockSpec((tm, d), lambda i: (i, 0))
    row = pl.BlockSpec((1, d), lambda i: (0, 0))
    wide = pl.BlockSpec((tm, ffn), lambda i: (i, 0))
    in_specs = [tile, row, ANY, ANY, ANY]
    out_shape = [
        jax.ShapeDtypeStruct((s, d), F32),
        jax.ShapeDtypeStruct((s, ffn), BF16),
        jax.ShapeDtypeStruct((s, ffn), BF16),
    ]
    out_specs = [tile, wide, wide]
    args = [x, gain] + list(weights)
    if head is not None:
        in_specs += [row, tile]
        args += list(head)
        out_shape += [jax.ShapeDtypeStruct((1, d), F32)] * 2
        out_specs += [row, row]
    return _grid_call(
        body,
        carries,
        name="ffn_fwd_loss" if head is not None else "ffn_fwd",
        grid=(s // tm,),
        in_specs=in_specs,
        out_specs=out_specs,
        out_shape=out_shape,
        scratch_shapes=[pltpu.VMEM((ffn, d), BF16)] * 3 + [pltpu.SemaphoreType.DMA((3 * NDEV * LOAD_PIECES,))],
        args=args,
    )


def _ffn_bwd(dh, x, a, b, gain, weights, ffn, name, carries=()):
    s, d = x.shape
    tm = min(512, s)
    halves = 2
    fh = ffn // halves

    def body(dh_ref, x_ref, a_ref, b_ref, g_ref, b1, b3, b2, dx_ref, da_ref, db_ref, n_ref, dg_ref, w1s, w3s, w2s, sems):
        i, j = pl.program_id(0), pl.program_id(1)

        @pl.when((i == 0) & (j == 0))
        def _():
            for cp in _load_weights(((b1, w1s), (b3, w3s), (b2, w2s)), sems):
                cp.wait()
            dg_ref[...] = jnp.zeros_like(dg_ref)

        @pl.when(j == 0)
        def _():
            xv = x_ref[...]
            r = lax.rsqrt(jnp.mean(xv * xv, axis=-1, keepdims=True) + EPS)
            n_ref[...] = (xv * r * g_ref[...]).astype(BF16)
            dx_ref[...] = jnp.zeros_like(dx_ref)

        dob = (0.5 * dh_ref[...]).astype(BF16)
        chunks = _chunks(fh)

        def dhm_of(k):
            at, width = chunks[k]
            return _dot(dob, w2s[pl.ds(pl.multiple_of(j * fh + at, GROUP), width), :], NT)

        ahead = dhm_of(0)
        for k, (at, width) in enumerate(chunks):
            cols = slice(at, at + width)
            dhm = ahead
            if k + 1 < len(chunks):
                ahead = dhm_of(k + 1)
            for top in range(0, tm, ROW_BAND):
                band = slice(top, top + ROW_BAND)
                av = a_ref[band, cols].astype(F32)
                bv = b_ref[band, cols].astype(F32)
                sg = _sigmoid(av)
                dv = dhm[band]
                da_ref[band, cols] = (dv * bv * (sg * (1.0 + av * (1.0 - sg)))).astype(BF16)
                db_ref[band, cols] = (dv * (av * sg)).astype(BF16)
        half = pl.ds(pl.multiple_of(j * fh, GROUP), fh)
        dx_ref[...] += _dot(da_ref[...], w1s[half, :], NN) + _dot(db_ref[...], w3s[half, :], NN)

        @pl.when(j == halves - 1)
        def _():
            xv = x_ref[...]
            g = g_ref[...]
            r = lax.rsqrt(jnp.mean(xv * xv, axis=-1, keepdims=True) + EPS)
            nh = xv * r
            total = dx_ref[...]
            dg_ref[...] += jnp.sum(total * nh, axis=0, keepdims=True)
            dnh = total * g
            dx_ref[...] = dh_ref[...] + r * (dnh - nh * jnp.mean(dnh * nh, axis=-1, keepdims=True))

    tile = pl.BlockSpec((tm, d), lambda i, j: (i, 0))
    row = pl.BlockSpec((1, d), lambda i, j: (0, 0))
    wide = pl.BlockSpec((tm, fh), lambda i, j: (i, j))
    return _grid_call(
        body,
        carries,
        name=name,
        grid=(s // tm, halves),
        in_specs=[tile, tile, wide, wide, row, ANY, ANY, ANY],
        out_specs=[tile, wide, wide, tile, row],
        out_shape=[
            jax.ShapeDtypeStruct((s, d), F32),
            jax.ShapeDtypeStruct((s, ffn), BF16),
            jax.ShapeDtypeStruct((s, ffn), BF16),
            jax.ShapeDtypeStruct((s, d), BF16),
            jax.ShapeDtypeStruct((1, d), F32),
        ],
        scratch_shapes=[pltpu.VMEM((ffn, d), BF16)] * 3 + [pltpu.SemaphoreType.DMA((3 * NDEV * LOAD_PIECES,))],
        args=[dh, x, a, b, gain] + list(weights),
    )


def _wgrad(lhs, rhs, scale, name, gate=None, carries=()):
    s, m = lhs.shape
    n = rhs.shape[1]
    rs = m // NDEV
    tk = min(512, s)
    steps = s // tk
    lefts = [lhs] if gate is None else [lhs, gate]

    def body(*refs):
        l_ref, r_ref, o_ref, acc = refs[0], refs[-3], refs[-2], refs[-1]
        k = pl.program_id(0)

        @pl.when(k == 0)
        def _():
            acc[...] = jnp.zeros_like(acc)

        left = l_ref[...]
        if gate is not None:
            av = left.astype(F32)
            left = (av * _sigmoid(av) * refs[1][...].astype(F32)).astype(BF16)
        acc[...] += _dot(left, r_ref[...].astype(BF16), TN)

        @pl.when(k == steps - 1)
        def _():
            for p in range(NDEV):
                o_ref[p % 2, p // 2] = (acc[p * rs:(p + 1) * rs, :] * scale).astype(BF16)

    (out,), carried = _grid_call(
        body,
        carries,
        name=name,
        grid=(steps,),
        in_specs=[pl.BlockSpec((tk, m), lambda k: (k, 0))] * len(lefts) + [pl.BlockSpec((tk, n), lambda k: (k, 0))],
        out_specs=[pl.BlockSpec((2, NCHIP, rs, n), lambda k: (0, 0, 0, 0))],
        out_shape=[jax.ShapeDtypeStruct((2, NCHIP, rs, n), BF16)],
        scratch_shapes=[pltpu.VMEM((m, n), F32)],
        args=lefts + [rhs],
    )
    return out, carried


def _mix_constants(s):
    c = GROUP
    lg = np.log1p(-np.exp2(-5.0 - np.arange(RET_HEADS, dtype=np.float32))).astype(np.float32)
    pos = np.arange(c, dtype=np.float32)
    rel = pos[:, None] - pos[None, :]
    decay = np.where(rel[None] >= 0, np.exp(lg[:, None, None] * np.maximum(rel, 0.0)[None]), 0.0).astype(np.float32)
    ktail = np.exp(lg[:, None] * (c - 1 - pos)[None, :]).astype(np.float32)
    qhead = np.exp(lg[:, None] * (pos + 1.0)[None, :]).astype(np.float32)
    chunk_decay = [float(v) for v in np.exp(lg * np.float32(c)).astype(np.float32)]
    ones = np.ones((1, 1, c), np.float32)
    inv_freq = (1.0 / (np.float32(ROPE_BASE) ** (np.arange(0, c, 2, dtype=np.float32) / np.float32(c)))).astype(np.float32)
    ang = (np.arange(s, dtype=np.float32)[:, None] * inv_freq[None, :]).astype(np.float32)
    cos, sin = np.cos(ang).astype(np.float32), np.sin(ang).astype(np.float32)
    return dict(
        decay=jnp.asarray(decay),
        ktail=jnp.asarray(ktail[:, :, None] * ones),
        qhead=jnp.asarray(qhead[:, :, None] * ones),
        chunk_decay=chunk_decay,
        cos=jnp.asarray(np.concatenate([cos, cos], axis=-1)),
        sin=jnp.asarray(np.concatenate([-sin, sin], axis=-1)),
    )


def _rope(t, cos, sin):
    return t * cos + pltpu.roll(t, GROUP // 2, axis=1) * sin


def _rope_bwd(dt, cos, sin):
    return dt * cos + pltpu.roll(dt * sin, GROUP // 2, axis=1)


def _window_sums(ext, w, forward):
    rows = ext.shape[0]
    acc, k = ext, 1
    while k < w:
        acc = acc + pltpu.roll(acc, k if forward else rows - k, axis=0)
        k *= 2
    return acc


def _pool_counts(tile, tm, w):
    t = lax.broadcasted_iota(jnp.int32, (tm, 1), 0) + tile * tm
    return jnp.minimum(t + 1, w).astype(F32)


def _mix_fwd(h1, gain, weights, pool_w, pool_scale, ret_gain, consts, carries=()):
    s, d = h1.shape
    pwid = N_POOL_GROUPS * GROUP
    rwid = RET_HEADS * GROUP
    inw = pwid + 4 * rwid
    tm = min(256, s)
    nck = tm // GROUP
    cd = consts["chunk_decay"]

    def body(h_ref, g_ref, bin_, bout, pw_ref, ps_ref, rg_ref, cos_ref, sin_ref, dec_ref, kt_ref, qh_ref,
             h2_ref, proj_ref, o_ref, rs_ref, wins, wouts, state, carry, mbuf, sems):
        i = pl.program_id(0)

        @pl.when(i == 0)
        def _():
            for cp in _load_weights(((bin_, wins), (bout, wouts)), sems):
                cp.wait()
            state[...] = jnp.zeros_like(state)
            carry[...] = jnp.zeros_like(carry)

        hv = h_ref[...]
        r = lax.rsqrt(jnp.mean(hv * hv, axis=-1, keepdims=True) + EPS)
        u = (hv * r * g_ref[...]).astype(BF16)
        proj_ref[...] = _dot(u, wins[...], NT)

        ext = jnp.concatenate([carry[...], proj_ref[:, 0:pwid]], axis=0)
        carry[...] = proj_ref[tm - MAX_WINDOW:tm, 0:pwid]
        for gi, w in enumerate(POOL_WINDOWS):
            cols = slice(gi * GROUP, (gi + 1) * GROUP)
            xg = ext[:, cols]
            ws = _window_sums(xg, w, True)[MAX_WINDOW:, :]
            pooled = ws / _pool_counts(i, tm, w) - xg[MAX_WINDOW:, :]
            z = _dot(pooled.astype(BF16), pw_ref[gi].astype(BF16), NN)
            mbuf[:, cols] = (z * ps_ref[:, cols]).astype(BF16)

        cos, sin = cos_ref[...], sin_ref[...]
        for h in range(RET_HEADS):
            cq = slice(pwid + h * GROUP, pwid + (h + 1) * GROUP)
            ck = slice(pwid + rwid + h * GROUP, pwid + rwid + (h + 1) * GROUP)
            cv = slice(pwid + 2 * rwid + h * GROUP, pwid + 2 * rwid + (h + 1) * GROUP)
            cg = slice(pwid + 3 * rwid + h * GROUP, pwid + 3 * rwid + (h + 1) * GROUP)
            ch = slice(h * GROUP, (h + 1) * GROUP)
            qr = _rope(proj_ref[:, cq], cos, sin)
            kr = _rope(proj_ref[:, ck], cos, sin) * (GROUP ** -0.5)
            vb = proj_ref[:, cv].astype(BF16)
            for n in range(nck):
                rows = slice(n * GROUP, (n + 1) * GROUP)
                qc, kc, vc = qr[rows], kr[rows], vb[rows]
                rb = state[h]
                rs_ref[n, h] = rb
                p = (_dot(qc.astype(BF16), kc.astype(BF16), NT) * dec_ref[h]).astype(BF16)
                o = _dot(p, vc, NN) + _dot((qc * qh_ref[h]).astype(BF16), rb.astype(BF16), NN)
                state[h] = cd[h] * rb + _dot((kc * kt_ref[h]).astype(BF16), vc, TN)
                o_ref[rows, ch] = o
                on = o * lax.rsqrt(jnp.mean(o * o, axis=-1, keepdims=True) + EPS)
                gv = proj_ref[rows, cg]
                mbuf[rows, pwid + h * GROUP:pwid + (h + 1) * GROUP] = (
                    gv * _sigmoid(gv) * (on * rg_ref[:, ch])
                ).astype(BF16)
        h2_ref[...] = hv + _dot(mbuf[...], wouts[...], NN)

    tile = pl.BlockSpec((tm, d), lambda i: (i, 0))
    full = lambda shape: pl.BlockSpec(shape, lambda i: (0,) * len(shape))
    return _grid_call(
        body,
        carries,
        name="mix_fwd",
        grid=(s // tm,),
        in_specs=[
            tile, full((1, d)), ANY, ANY,
            full((N_POOL_GROUPS, GROUP, GROUP)), full((1, pwid)), full((1, rwid)),
            pl.BlockSpec((tm, GROUP), lambda i: (i, 0)), pl.BlockSpec((tm, GROUP), lambda i: (i, 0)),
            full((RET_HEADS, GROUP, GROUP)), full((RET_HEADS, GROUP, GROUP)), full((RET_HEADS, GROUP, GROUP)),
        ],
        out_specs=[
            tile,
            pl.BlockSpec((tm, inw), lambda i: (i, 0)),
            pl.BlockSpec((tm, rwid), lambda i: (i, 0)),
            pl.BlockSpec((nck, RET_HEADS, GROUP, GROUP), lambda i: (i, 0, 0, 0)),
        ],
        out_shape=[
            jax.ShapeDtypeStruct((s, d), F32),
            jax.ShapeDtypeStruct((s, inw), F32),
            jax.ShapeDtypeStruct((s, rwid), F32),
            jax.ShapeDtypeStruct((s // GROUP, RET_HEADS, GROUP, GROUP), F32),
        ],
        scratch_shapes=[
            pltpu.VMEM((inw, d), BF16), pltpu.VMEM((d, d), BF16),
            pltpu.VMEM((RET_HEADS, GROUP, GROUP), F32), pltpu.VMEM((MAX_WINDOW, pwid), F32),
            pltpu.VMEM((tm, d), BF16), pltpu.SemaphoreType.DMA((2 * NDEV * LOAD_PIECES,)),
        ],
        args=[h1, gain, weights[0], weights[1], pool_w, pool_scale, ret_gain,
              consts["cos"], consts["sin"], consts["decay"], consts["ktail"], consts["qhead"]],
    )


def _mix_bwd(dh2, h1, proj, o_saved, rsave, gain, weights, pool_w, pool_scale, ret_gain, consts, carries=()):
    s, d = h1.shape
    pwid = N_POOL_GROUPS * GROUP
    rwid = RET_HEADS * GROUP
    inw = pwid + 4 * rwid
    tm = min(256, s)
    nck = tm // GROUP
    nt = s // tm
    cd = consts["chunk_decay"]
    halo_per_tile = tm // MAX_WINDOW

    def body(dh2_ref, h_ref, proj_ref, halo_ref, o_ref, rs_ref, g_ref, bin_, bout, pw_ref, ps_ref, rg_ref,
             cos_ref, sin_ref, dec_ref, kt_ref, qh_ref,
             dh1_ref, dproj_ref, u_ref, m_ref, dpw_ref, dps_ref, drg_ref, dg_ref,
             wins, wouts, dstate, carry, dm, dpj, sems):
        i = pl.program_id(0)
        tile = nt - 1 - i

        @pl.when(i == 0)
        def _():
            for cp in _load_weights(((bin_, wins), (bout, wouts)), sems):
                cp.wait()
            dstate[...] = jnp.zeros_like(dstate)
            carry[...] = jnp.zeros_like(carry)
            for ref in (dpw_ref, dps_ref, drg_ref, dg_ref):
                ref[...] = jnp.zeros_like(ref)

        dh2v = dh2_ref[...]
        dm[...] = _dot(dh2v.astype(BF16), wouts[...], NT)
        hv = h_ref[...]
        g = g_ref[...]
        r = lax.rsqrt(jnp.mean(hv * hv, axis=-1, keepdims=True) + EPS)
        uh = hv * r
        u_ref[...] = (uh * g).astype(BF16)

        halo = jnp.where(tile == 0, 0.0, halo_ref[...])
        ext = jnp.concatenate([halo, proj_ref[:, 0:pwid]], axis=0)
        next_dpn = carry[...]
        for gi, w in enumerate(POOL_WINDOWS):
            cols = slice(gi * GROUP, (gi + 1) * GROUP)
            xg = ext[:, cols]
            cnt = _pool_counts(tile, tm, w)
            pooled = (_window_sums(xg, w, True)[MAX_WINDOW:, :] / cnt - xg[MAX_WINDOW:, :]).astype(BF16)
            pwb = pw_ref[gi].astype(BF16)
            z = _dot(pooled, pwb, NN)
            scale = ps_ref[:, cols]
            m_ref[:, cols] = (z * scale).astype(BF16)
            da = dm[:, cols]
            dps_ref[:, cols] += jnp.sum(da * z, axis=0, keepdims=True)
            dz = (da * scale).astype(BF16)
            dpw_ref[gi] += _dot(pooled, dz, TN)
            dpl = _dot(dz, pwb, NT)
            dpn = dpl / cnt
            ext2 = jnp.concatenate([dpn, next_dpn[:, cols]], axis=0)
            dpj[:, cols] = (_window_sums(ext2, w, False)[0:tm, :] - dpl).astype(BF16)
            carry[:, cols] = dpn[0:MAX_WINDOW, :]

        cos, sin = cos_ref[...], sin_ref[...]
        for h in range(RET_HEADS):
            cq = slice(pwid + h * GROUP, pwid + (h + 1) * GROUP)
            ck = slice(pwid + rwid + h * GROUP, pwid + rwid + (h + 1) * GROUP)
            cv = slice(pwid + 2 * rwid + h * GROUP, pwid + 2 * rwid + (h + 1) * GROUP)
            cg = slice(pwid + 3 * rwid + h * GROUP, pwid + 3 * rwid + (h + 1) * GROUP)
            ch = slice(h * GROUP, (h + 1) * GROUP)
            qr = _rope(proj_ref[:, cq], cos, sin)
            kr = _rope(proj_ref[:, ck], cos, sin) * (GROUP ** -0.5)
            vb = proj_ref[:, cv].astype(BF16)
            gv = proj_ref[:, cg]
            ov = o_ref[:, ch]
            ro = lax.rsqrt(jnp.mean(ov * ov, axis=-1, keepdims=True) + EPS)
            on = ov * ro
            rg = rg_ref[:, ch]
            db = dm[:, pwid + h * GROUP:pwid + (h + 1) * GROUP]
            sg = _sigmoid(gv)
            sl = gv * sg
            m_ref[:, pwid + h * GROUP:pwid + (h + 1) * GROUP] = (sl * (on * rg)).astype(BF16)
            dpj[:, cg] = (db * (on * rg) * (sg * (1.0 + gv * (1.0 - sg)))).astype(BF16)
            drg_ref[:, ch] += jnp.sum(db * sl * on, axis=0, keepdims=True)
            don = db * sl * rg
            do = (ro * (don - on * jnp.mean(don * on, axis=-1, keepdims=True))).astype(BF16)
            for n in reversed(range(nck)):
                rows = slice(n * GROUP, (n + 1) * GROUP)
                qc, kc, vc, dob = qr[rows], kr[rows], vb[rows], do[rows]
                qcb, kcb = qc.astype(BF16), kc.astype(BF16)
                qh = (qc * qh_ref[h]).astype(BF16)
                kt = (kc * kt_ref[h]).astype(BF16)
                rn = rs_ref[n, h].astype(BF16)
                dnext = dstate[h]
                dnb = dnext.astype(BF16)
                dec = dec_ref[h]
                p = (_dot(qcb, kcb, NT) * dec).astype(BF16)
                ds = (_dot(dob, vc, NT) * dec).astype(BF16)
                dv = _dot(p, dob, TN) + _dot(kt, dnb, NN)
                dq = _dot(ds, kcb, NN) + _dot(dob, rn, NT) * qh_ref[h]
                dk = _dot(ds, qcb, TN) + _dot(vc, dnb, NT) * kt_ref[h]
                dstate[h] = cd[h] * dnext + _dot(qh, dob, TN)
                dpj[rows, cq] = _rope_bwd(dq, cos[rows], sin[rows]).astype(BF16)
                dpj[rows, ck] = _rope_bwd(dk * (GROUP ** -0.5), cos[rows], sin[rows]).astype(BF16)
                dpj[rows, cv] = dv.astype(BF16)

        dproj_ref[...] = dpj[...]
        du = _dot(dpj[...], wins[...], NN)
        dg_ref[...] += jnp.sum(du * uh, axis=0, keepdims=True)
        dn = du * g
        dh1_ref[...] = dh2v + r * (dn - uh * jnp.mean(dn * uh, axis=-1, keepdims=True))

    rev = lambda i: (nt - 1 - i, 0)
    tile = pl.BlockSpec((tm, d), rev)
    full = lambda shape: pl.BlockSpec(shape, lambda i: (0,) * len(shape))
    return _grid_call(
        body,
        carries,
        name="mix_bwd",
        grid=(nt,),
        in_specs=[
            tile, tile,
            pl.BlockSpec((tm, inw), rev),
            pl.BlockSpec((MAX_WINDOW, pwid), lambda i: (jnp.maximum((nt - 1 - i) * halo_per_tile - 1, 0), 0)),
            pl.BlockSpec((tm, rwid), rev),
            pl.BlockSpec((nck, RET_HEADS, GROUP, GROUP), lambda i: (nt - 1 - i, 0, 0, 0)),
            full((1, d)), ANY, ANY,
            full((N_POOL_GROUPS, GROUP, GROUP)), full((1, pwid)), full((1, rwid)),
            pl.BlockSpec((tm, GROUP), rev), pl.BlockSpec((tm, GROUP), rev),
            full((RET_HEADS, GROUP, GROUP)), full((RET_HEADS, GROUP, GROUP)), full((RET_HEADS, GROUP, GROUP)),
        ],
        out_specs=[
            tile, pl.BlockSpec((tm, inw), rev), tile, tile,
            full((N_POOL_GROUPS, GROUP, GROUP)), full((1, pwid)), full((1, rwid)), full((1, d)),
        ],
        out_shape=[
            jax.ShapeDtypeStruct((s, d), F32),
            jax.ShapeDtypeStruct((s, inw), BF16),
            jax.ShapeDtypeStruct((s, d), BF16),
            jax.ShapeDtypeStruct((s, d), BF16),
            jax.ShapeDtypeStruct((N_POOL_GROUPS, GROUP, GROUP), F32),
            jax.ShapeDtypeStruct((1, pwid), F32),
            jax.ShapeDtypeStruct((1, rwid), F32),
            jax.ShapeDtypeStruct((1, d), F32),
        ],
        scratch_shapes=[
            pltpu.VMEM((inw, d), BF16), pltpu.VMEM((d, d), BF16),
            pltpu.VMEM((RET_HEADS, GROUP, GROUP), F32), pltpu.VMEM((MAX_WINDOW, pwid), F32),
            pltpu.VMEM((tm, d), F32), pltpu.VMEM((tm, inw), BF16), pltpu.SemaphoreType.DMA((2 * NDEV * LOAD_PIECES,)),
        ],
        args=[dh2, h1, proj, proj, o_saved, rsave, gain, weights[0], weights[1], pool_w, pool_scale, ret_gain,
              consts["cos"], consts["sin"], consts["decay"], consts["ktail"], consts["qhead"]],
    )


def _adam(w, g, m, v):
    m = ADAM_B1 * m + (1.0 - ADAM_B1) * g
    v = ADAM_B2 * v + (1.0 - ADAM_B2) * jnp.square(g)
    m_hat = m / (1.0 - ADAM_B1 ** ADAM_STEP)
    v_hat = v / (1.0 - ADAM_B2 ** ADAM_STEP)
    delta = -ADAM_LR * (m_hat / (jnp.sqrt(v_hat) + ADAM_EPS) + ADAM_WD * w)
    return delta, m, v


def _pair_sum(grads, theirs, core, name):
    n, rows, d = theirs.shape
    tr = _row_tile(rows, 512)

    def body(c_ref, a_ref, b_ref, o_ref):
        o_ref[...] = (a_ref[0].astype(F32) + b_ref[...].astype(F32)).astype(BF16)

    spec = pl.BlockSpec((1, tr, d), lambda j, i, c: (j, i, 0))
    return _call(
        body,
        name=name,
        grid_spec=pltpu.PrefetchScalarGridSpec(
            num_scalar_prefetch=1,
            grid=(n, rows // tr),
            in_specs=[pl.BlockSpec((1, 1, tr, d), lambda j, i, c: (c[0], j, i, 0)), spec],
            out_specs=spec,
        ),
        out_shape=jax.ShapeDtypeStruct(theirs.shape, BF16),
        compiler_params=_seq(2),
    )(core, grads, theirs)


def _adamw_big(w, parts, m, v, name):
    rows, d = w.shape
    tr = _row_tile(rows, 512)

    def body(w_ref, p_ref, m_ref, v_ref, g_ref, d_ref, nm_ref, nv_ref):
        g = p_ref[0].astype(F32)
        for q in range(1, NCHIP):
            g = g + p_ref[q].astype(F32)
        g_ref[...] = g
        d_ref[...], nm_ref[...], nv_ref[...] = _adam(w_ref[...], g, m_ref[...], v_ref[...])

    spec = pl.BlockSpec((tr, d), lambda i: (i, 0))
    return _call(
        body,
        name=name,
        grid=(rows // tr,),
        in_specs=[spec, pl.BlockSpec((NCHIP, tr, d), lambda i: (0, i, 0)), spec, spec],
        out_specs=[spec] * 4,
        out_shape=[jax.ShapeDtypeStruct((rows, d), F32)] * 4,
        compiler_params=_seq(1),
    )(w, parts, m, v)


def _adamw_small(stats_all, pw_all, ws, ms, vs, d, pwid):
    nsmall = len(ws)

    def body(*refs):
        st_ref, pwa_ref = refs[0], refs[1]
        w_refs = refs[2:2 + nsmall]
        m_refs = refs[2 + nsmall:2 + 2 * nsmall]
        v_refs = refs[2 + 2 * nsmall:2 + 3 * nsmall]
        outs = refs[2 + 3 * nsmall:]
        st = st_ref[0]
        pwg = pwa_ref[0]
        for q in range(1, NDEV):
            st = st + st_ref[q]
            pwg = pwg + pwa_ref[q]
        grads = [st[0:1, :], st[1:2, :], st[2:3, :], st[3:4, :], st[4:5, 0:pwid], st[4:5, pwid:2 * pwid], pwg]
        outs[0][...] = jnp.zeros((1, GROUP), F32) + jnp.sum(st[5:6, :])
        for j in range(nsmall):
            delta, nm, nv = _adam(w_refs[j][...], grads[j], m_refs[j][...], v_refs[j][...])
            outs[1 + 4 * j][...] = grads[j]
            outs[2 + 4 * j][...] = delta
            outs[3 + 4 * j][...] = nm
            outs[4 + 4 * j][...] = nv

    out_shape = [jax.ShapeDtypeStruct((1, GROUP), F32)]
    for w in ws:
        out_shape += [jax.ShapeDtypeStruct(w.shape, F32)] * 4
    return _call(body, name="adamw_small", out_shape=out_shape, compiler_params=_params())(
        stats_all, pw_all, *ws, *ms, *vs
    )


def kernel(x, ffn1_norm, ffn1_w1, ffn1_w3, ffn1_w2, mix_norm, w_in, pool_w, pool_scale, ret_norm, w_out, ffn2_norm, ffn2_w1, ffn2_w3, ffn2_w2, final_norm, loss_target, m_ffn1_norm, m_ffn1_w1, m_ffn1_w3, m_ffn1_w2, m_mix_norm, m_w_in, m_pool_w, m_pool_scale, m_ret_norm, m_w_out, m_ffn2_norm, m_ffn2_w1, m_ffn2_w3, m_ffn2_w2, m_final_norm, v_ffn1_norm, v_ffn1_w1, v_ffn1_w3, v_ffn1_w2, v_mix_norm, v_w_in, v_pool_w, v_pool_scale, v_ret_norm, v_w_out, v_ffn2_norm, v_ffn2_w1, v_ffn2_w3, v_ffn2_w2, v_final_norm):
    s, d = x.shape[1], x.shape[2]
    ffn = ffn1_w1.shape[2] * NDEV
    pwid = pool_scale.shape[1]
    xs, tgt = x[0], loss_target[0]
    consts = _mix_constants(s)
    pw3 = pool_w[0]
    fnorm = final_norm.reshape(1, d)

    rows_of = lambda w, transposed: (w[0].T if transposed else w[0]).astype(BF16)
    send_f1 = [rows_of(ffn1_w1, True), rows_of(ffn1_w3, True), rows_of(ffn1_w2, False)]
    send_mix = [rows_of(w_in, True), rows_of(w_out, False)]
    send_f2 = [rows_of(ffn2_w1, True), rows_of(ffn2_w3, True), rows_of(ffn2_w2, False)]

    (w_f1,) = _comm_call([_Gather(send_f1)], "gather_ffn1")
    (h1, a1, b1), (more,) = _ffn_fwd(xs, ffn1_norm, w_f1, ffn, carries=[_Gather(send_mix + send_f2[:1])])
    w_mix = more[:2]
    (h2, proj, o_saved, rsave), (rest,) = _mix_fwd(
        h1, mix_norm, w_mix, pw3, pool_scale, ret_norm, consts, carries=[_Gather(send_f2[1:])]
    )
    w_f2 = more[2:] + rest
    (dh3, a2, b2, dgf, loss_cols), _ = _ffn_fwd(h2, ffn2_norm, w_f2, ffn, head=(fnorm, tgt))

    core = lax.axis_index("c").astype(jnp.int32).reshape(1)

    def pair_reduced(g, tag):
        ((theirs,),) = _comm_call([_SiblingSwap([g])], "swap_" + tag)
        return _pair_sum(g, theirs, core, "pair_sum_" + tag)

    (dh2, da2, db2, n2, dg2), _ = _ffn_bwd(dh3, h2, a2, b2, ffn2_norm, w_f2, ffn, "ffn2_bwd")
    g, _ = _wgrad(da2, n2, 1.0, "ffn2_w1_grad")
    sum_f2w1 = pair_reduced(g, "ffn2_w1")
    g, ((parts_f2w1,),) = _wgrad(db2, n2, 1.0, "ffn2_w3_grad", carries=[_ChipScatter([sum_f2w1])])
    sum_f2w3 = pair_reduced(g, "ffn2_w3")
    g, ((parts_f2w3,),) = _wgrad(a2, dh3, 0.5, "ffn2_w2_grad", gate=b2, carries=[_ChipScatter([sum_f2w3])])
    sum_f2w2 = pair_reduced(g, "ffn2_w2")

    (dh1, dproj, u, mm, dpw, dps, drg, dgm), ((parts_f2w2,),) = _mix_bwd(
        dh2, h1, proj, o_saved, rsave, mix_norm, w_mix, pw3, pool_scale, ret_norm, consts,
        carries=[_ChipScatter([sum_f2w2])],
    )
    g, _ = _wgrad(dproj, u, 1.0, "w_in_grad")
    sum_in = pair_reduced(g, "w_in")
    g, _ = _wgrad(mm, dh2, 1.0, "w_out_grad")
    sum_out = pair_reduced(g, "w_out")

    (dx, da1, db1, n1, dg1), ((parts_in, parts_out),) = _ffn_bwd(
        dh1, xs, a1, b1, ffn1_norm, w_f1, ffn, "ffn1_bwd", carries=[_ChipScatter([sum_in, sum_out])]
    )
    g, _ = _wgrad(a1, dh1, 0.5, "ffn1_w2_grad", gate=b1)
    sum_f1w2 = pair_reduced(g, "ffn1_w2")
    g, ((parts_f1w2,),) = _wgrad(da1, n1, 1.0, "ffn1_w1_grad", carries=[_ChipScatter([sum_f1w2])])
    sum_f1w1 = pair_reduced(g, "ffn1_w1")
    g, ((parts_f1w1,),) = _wgrad(db1, n1, 1.0, "ffn1_w3_grad", carries=[_ChipScatter([sum_f1w1])])
    sum_f1w3 = pair_reduced(g, "ffn1_w3")

    stats = jnp.concatenate(
        [dg1, dgm, dg2, dgf, jnp.concatenate([dps, drg], axis=1), loss_cols, jnp.zeros((2, d), F32)], axis=0
    )
    (parts_f1w3,), (stats_all, pw_all) = _comm_call(
        [_ChipScatter([sum_f1w3]), _GatherDirect([stats, dpw.reshape(N_POOL_GROUPS * GROUP, GROUP)])], "scatter_last"
    )

    big = (
        (ffn1_w1, m_ffn1_w1, v_ffn1_w1, parts_f1w1, True),
        (ffn1_w3, m_ffn1_w3, v_ffn1_w3, parts_f1w3, True),
        (ffn1_w2, m_ffn1_w2, v_ffn1_w2, parts_f1w2, False),
        (w_in, m_w_in, v_w_in, parts_in, True),
        (w_out, m_w_out, v_w_out, parts_out, False),
        (ffn2_w1, m_ffn2_w1, v_ffn2_w1, parts_f2w1, True),
        (ffn2_w3, m_ffn2_w3, v_ffn2_w3, parts_f2w3, True),
        (ffn2_w2, m_ffn2_w2, v_ffn2_w2, parts_f2w2, False),
    )
    big_out = []
    for j, (w, m, v, parts, t) in enumerate(big):
        view = (lambda a: a[0].T) if t else (lambda a: a[0])
        back = (lambda a: a.T[None]) if t else (lambda a: a[None])
        big_out.append([back(a) for a in _adamw_big(view(w), parts, view(m), view(v), "adamw_%d" % j)])

    small_w = (ffn1_norm, mix_norm, ffn2_norm, fnorm, pool_scale, ret_norm, pw3.reshape(-1, GROUP))
    small_m = (m_ffn1_norm, m_mix_norm, m_ffn2_norm, m_final_norm.reshape(1, d), m_pool_scale, m_ret_norm, m_pool_w.reshape(-1, GROUP))
    small_v = (v_ffn1_norm, v_mix_norm, v_ffn2_norm, v_final_norm.reshape(1, d), v_pool_scale, v_ret_norm, v_pool_w.reshape(-1, GROUP))
    res = _adamw_small(stats_all, pw_all, small_w, small_m, small_v, d, pwid)
    loss = res[0][0, 0]
    small_out = [list(res[1 + 4 * j:5 + 4 * j]) for j in range(len(small_w))]
    small_out[3] = [a.reshape(d) for a in small_out[3]]
    small_out[6] = [a.reshape(pool_w.shape) for a in small_out[6]]

    order = [small_out[0], big_out[0], big_out[1], big_out[2], small_out[1], big_out[3], small_out[6], small_out[4],
             small_out[5], big_out[4], small_out[2], big_out[5], big_out[6], big_out[7], small_out[3]]
    result = [loss, dx[None]]
    for kind in range(4):
        result += [t[kind] for t in order]
    return tuple(result)
```

```python
import numpy as np
import jax
import jax.numpy as jnp
from jax import lax
from jax.experimental import pallas as pl
from jax.experimental.pallas import tpu as pltpu

F32 = jnp.float32
BF16 = jnp.bfloat16

NDEV = 8
NCHIP = 4
EPS = 1e-6
N_POOL_GROUPS = 4
POOL_WINDOWS = (2, 4, 8, 16)
MAX_WINDOW = 16
GROUP = 128
RET_HEADS = 4
ROPE_BASE = 10000.0
ADAM_LR = 0.001
ADAM_B1 = 0.9
ADAM_B2 = 0.999
ADAM_EPS = 1e-08
ADAM_WD = 0.01
ADAM_STEP = 10

VMEM_LIMIT = 56 * 1024 * 1024
FFN_CHUNK = 256
LOAD_PIECES = 4
ROW_BAND = 32

NT = (((1,), (1,)), ((), ()))
NN = (((1,), (0,)), ((), ()))
TN = (((0,), (0,)), ((), ()))

ANY = pl.BlockSpec(memory_space=pl.ANY)


def _dot(a, b, dims):
    return lax.dot_general(a, b, dims, preferred_element_type=F32)


def _call(body, **kw):
    return pl.pallas_call(body, **kw)


def _params(**kw):
    return pltpu.CompilerParams(vmem_limit_bytes=VMEM_LIMIT, **kw)


def _seq(n):
    return _params(dimension_semantics=("arbitrary",) * n)


def _peer(k):
    x, y, c = lax.axis_index("x"), lax.axis_index("y"), lax.axis_index("c")
    return (1 - x if k & 4 else x, 1 - y if k & 2 else y, 1 - c if k & 1 else c)


def _flat(pos):
    return 4 * pos[0] + 2 * pos[1] + pos[2]


def _chip(pos):
    return 2 * pos[0] + pos[1]


def _row_tile(rows, cap):
    return max(t for t in range(16, min(rows, cap) + 1, 16) if rows % t == 0)


def _pieces(rows, n):
    tiles = rows // 16
    cuts = [16 * (tiles * q // n) for q in range(n + 1)]
    return [(a, b - a) for a, b in zip(cuts[:-1], cuts[1:])]


def _load_weights(parts, sems):
    copies = []
    for buf, dst in parts:
        rows = buf.shape[1]
        for p in range(NDEV):
            for at, size in _pieces(rows, LOAD_PIECES):
                cp = pltpu.make_async_copy(
                    buf.at[p, pl.ds(at, size), :], dst.at[pl.ds(p * rows + at, size), :], sems.at[len(copies)]
                )
                cp.start()
                copies.append(cp)
    return copies


def _sigmoid(a):
    return 1.0 / (1.0 + jnp.exp(-a))


def _remote(src, dst, send_sem, recv_sem, to):
    return pltpu.make_async_remote_copy(
        src_ref=src, dst_ref=dst, send_sem=send_sem, recv_sem=recv_sem, device_id=to, device_id_type=pl.DeviceIdType.MESH
    )


class _Gather:
    chips = (4, 2, 6)

    def __init__(self, shards):
        n = len(shards)
        self.operands = list(shards)
        self.out_shape = [jax.ShapeDtypeStruct((NDEV,) + a.shape, a.dtype) for a in shards]
        self.sems = [pltpu.SemaphoreType.DMA((7 * n,)), pltpu.SemaphoreType.DMA((7 * n,)), pltpu.SemaphoreType.DMA((n,))]

    def _copy(self, t, k, block, to, ins, outs, sems, own=False):
        dst = outs[t].at[_flat(block)]
        return _remote(ins[t] if own else dst, dst, sems[0].at[7 * t + k], sems[1].at[7 * t + k], to)

    def begin(self, ins, outs, sems):
        me, sibling = _peer(0), _peer(1)
        for t in range(len(ins)):
            pltpu.make_async_copy(ins[t], outs[t].at[_flat(me)], sems[2].at[t]).start()
            self._copy(t, 0, me, sibling, ins, outs, sems, own=True).start()
            for j, k in enumerate(self.chips):
                self._copy(t, 1 + j, me, _peer(k), ins, outs, sems, own=True).start()

    def mid(self, ins, outs, sems):
        me, sibling = _peer(0), _peer(1)
        for t in range(len(ins)):
            for j, k in enumerate(self.chips):
                self._copy(t, 1 + j, _peer(k), me, ins, outs, sems).wait_recv()
                self._copy(t, 4 + j, _peer(k), sibling, ins, outs, sems).start()

    def end(self, ins, outs, sems):
        me, sibling = _peer(0), _peer(1)
        for t in range(len(ins)):
            self._copy(t, 0, sibling, me, ins, outs, sems).wait_recv()
            for j, k in enumerate(self.chips):
                self._copy(t, 4 + j, _peer(k ^ 1), me, ins, outs, sems).wait_recv()
            for k in range(7):
                self._copy(t, k, me, me, ins, outs, sems).wait_send()
            pltpu.make_async_copy(ins[t], outs[t].at[_flat(me)], sems[2].at[t]).wait()


class _GatherDirect:
    def __init__(self, arrays):
        n = len(arrays)
        self.operands = list(arrays)
        self.out_shape = [jax.ShapeDtypeStruct((NDEV,) + a.shape, a.dtype) for a in arrays]
        self.sems = [pltpu.SemaphoreType.DMA((7 * n,)), pltpu.SemaphoreType.DMA((7 * n,)), pltpu.SemaphoreType.DMA((n,))]

    def begin(self, ins, outs, sems):
        mine = _flat(_peer(0))
        for t in range(len(ins)):
            pltpu.make_async_copy(ins[t], outs[t].at[mine], sems[2].at[t]).start()
            for k in range(1, NDEV):
                _remote(ins[t], outs[t].at[mine], sems[0].at[7 * t + k - 1], sems[1].at[7 * t + k - 1], _peer(k)).start()

    def mid(self, ins, outs, sems):
        pass

    def end(self, ins, outs, sems):
        mine = _flat(_peer(0))
        for t in range(len(ins)):
            for k in range(1, NDEV):
                cp = _remote(ins[t], outs[t].at[_flat(_peer(k))], sems[0].at[7 * t + k - 1], sems[1].at[7 * t + k - 1], _peer(k))
                cp.wait_recv()
                cp.wait_send()
            pltpu.make_async_copy(ins[t], outs[t].at[mine], sems[2].at[t]).wait()


class _ChipScatter:
    pieces = 2

    def __init__(self, sums):
        n = len(sums) * NCHIP * self.pieces
        self.operands = list(sums)
        self.out_shape = [jax.ShapeDtypeStruct(a.shape, a.dtype) for a in sums]
        self.sems = [pltpu.SemaphoreType.DMA((n,)), pltpu.SemaphoreType.DMA((n,))]

    def _copies(self, ins, outs, sems, arriving):
        mine = _chip(_peer(0))
        copies = []
        for t in range(len(ins)):
            rows = ins[t].shape[1] // self.pieces
            for k in (0, 4, 2, 6):
                other = _chip(_peer(k))
                for q in range(self.pieces):
                    part = pl.ds(q * rows, rows)
                    at = len(copies)
                    if k == 0:
                        cp = pltpu.make_async_copy(ins[t].at[mine, part, :], outs[t].at[mine, part, :], sems[0].at[at])
                    else:
                        landing = outs[t].at[other if arriving else mine, part, :]
                        cp = _remote(ins[t].at[other, part, :], landing, sems[0].at[at], sems[1].at[at], _peer(k))
                    copies.append(cp)
        return copies

    def begin(self, ins, outs, sems):
        for cp in self._copies(ins, outs, sems, False):
            cp.start()

    def mid(self, ins, outs, sems):
        pass

    def end(self, ins, outs, sems):
        for at, cp in enumerate(self._copies(ins, outs, sems, True)):
            if at % (NCHIP * self.pieces) < self.pieces:
                cp.wait()
            else:
                cp.wait_recv()
                cp.wait_send()


def _split_refs(refs, counts):
    out, at = [], 0
    for n in counts:
        out.append(refs[at:at + n])
        at += n
    return out


def _comm_call(carries, name):
    nin = [len(c.operands) for c in carries]
    nout = [len(c.out_shape) for c in carries]
    nsem = [len(c.sems) for c in carries]

    def body(*refs):
        ins, outs, sems = _split_refs(refs, (sum(nin), sum(nout), sum(nsem)))
        parts = list(zip(carries, _split_refs(ins, nin), _split_refs(outs, nout), _split_refs(sems, nsem)))
        for c, i, o, s in parts:
            c.begin(i, o, s)
        for c, i, o, s in parts:
            c.mid(i, o, s)
        for c, i, o, s in parts:
            c.end(i, o, s)

    res = _call(
        body,
        name=name,
        out_shape=[sh for c in carries for sh in c.out_shape],
        in_specs=[ANY] * sum(nin),
        out_specs=[ANY] * sum(nout),
        scratch_shapes=[sm for c in carries for sm in c.sems],
        compiler_params=pltpu.CompilerParams(has_side_effects=True),
    )(*[a for c in carries for a in c.operands])
    return _split_refs(list(res), nout)


def _grid_call(body, carries, *, name, grid, in_specs, out_specs, out_shape, scratch_shapes, args):
    ni, no, ns = len(in_specs), len(out_specs), len(scratch_shapes)
    nin = [len(c.operands) for c in carries]
    nout = [len(c.out_shape) for c in carries]
    nsem = [len(c.sems) for c in carries]
    steps = int(np.prod(grid))
    mid_step = max(steps - 2, 0)

    def wrapped(*refs):
        ins, cins, outs, couts, scr, csems = _split_refs(refs, (ni, sum(nin), no, sum(nout), ns, sum(nsem)))
        if not carries:
            return body(*ins, *outs, *scr)
        parts = list(zip(carries, _split_refs(cins, nin), _split_refs(couts, nout), _split_refs(csems, nsem)))
        step = pl.program_id(0)
        for axis in range(1, len(grid)):
            step = step * grid[axis] + pl.program_id(axis)

        @pl.when(step == 0)
        def _():
            for c, i, o, s in parts:
                c.begin(i, o, s)

        body(*ins, *outs, *scr)

        @pl.when(step == mid_step)
        def _():
            for c, i, o, s in parts:
                c.mid(i, o, s)

        @pl.when(step == steps - 1)
        def _():
            for c, i, o, s in parts:
                c.end(i, o, s)

    res = _call(
        wrapped,
        name=name,
        grid=tuple(grid),
        in_specs=list(in_specs) + [ANY] * sum(nin),
        out_specs=list(out_specs) + [ANY] * sum(nout),
        out_shape=list(out_shape) + [sh for c in carries for sh in c.out_shape],
        scratch_shapes=list(scratch_shapes) + [sm for c in carries for sm in c.sems],
        compiler_params=_seq(len(grid)),
    )(*args, *[a for c in carries for a in c.operands])
    res = list(res)
    return res[:no], _split_refs(res[no:], nout)


def _chunks(width):
    return [(at, min(FFN_CHUNK, width - at)) for at in range(0, width, FFN_CHUNK)]


def _ffn_fwd(x, gain, weights, ffn, head=None, carries=()):
    s, d = x.shape
    tm = min(512, s)

    def body(*refs):
        if head is None:
            x_ref, g_ref, b1, b3, b2, h_ref, a_ref, b_ref, hm_ref, w1s, w3s, w2s, sems = refs
        else:
            x_ref, g_ref, b1, b3, b2, gf_ref, t_ref, h_ref, a_ref, b_ref, hm_ref, dgf_ref, loss_ref, w1s, w3s, w2s, sems = refs
        i = pl.program_id(0)

        @pl.when(i == 0)
        def _():
            for cp in _load_weights(((b1, w1s), (b3, w3s), (b2, w2s)), sems):
                cp.wait()
            if head is not None:
                dgf_ref[...] = jnp.zeros_like(dgf_ref)
                loss_ref[...] = jnp.zeros_like(loss_ref)

        xv = x_ref[...]
        r = lax.rsqrt(jnp.mean(xv * xv, axis=-1, keepdims=True) + EPS)
        n = (xv * r * g_ref[...]).astype(BF16)
        acc = jnp.zeros((tm, d), F32)
        for at, width in _chunks(ffn):
            cols = slice(at, at + width)
            a = _dot(n, w1s[cols, :], NT)
            b = _dot(n, w3s[cols, :], NT)
            a_ref[:, cols] = a.astype(BF16)
            b_ref[:, cols] = b.astype(BF16)
            hm = (a * _sigmoid(a) * b).astype(BF16)
            hm_ref[:, cols] = hm
            acc = acc + _dot(hm, w2s[cols, :], NN)
        h = xv + 0.5 * acc
        if head is None:
            h_ref[...] = h
        else:
            rf = lax.rsqrt(jnp.mean(h * h, axis=-1, keepdims=True) + EPS)
            nh = h * rf
            gf = gf_ref[...]
            err = nh * gf - t_ref[...]
            loss_ref[...] += jnp.sum(err * err, axis=0, keepdims=True) * (0.5 / d)
            dy = err * (1.0 / d)
            dgf_ref[...] += jnp.sum(dy * nh, axis=0, keepdims=True)
            dn = dy * gf
            h_ref[...] = rf * (dn - nh * jnp.mean(dn * nh, axis=-1, keepdims=True))

    tile = pl.BlockSpec((tm, d), lambda i: (i, 0))
    row = pl.BlockSpec((1, d), lambda i: (0, 0))
    wide = pl.BlockSpec((tm, ffn), lambda i: (i, 0))
    in_specs = [tile, row, ANY, ANY, ANY]
    out_shape = [jax.ShapeDtypeStruct((s, d), F32)] + [jax.ShapeDtypeStruct((s, ffn), BF16)] * 3
    out_specs = [tile, wide, wide, wide]
    args = [x, gain] + list(weights)
    if head is not None:
        in_specs += [row, tile]
        args += list(head)
        out_shape += [jax.ShapeDtypeStruct((1, d), F32)] * 2
        out_specs += [row, row]
    return _grid_call(
        body,
        carries,
        name="ffn_fwd_loss" if head is not None else "ffn_fwd",
        grid=(s // tm,),
        in_specs=in_specs,
        out_specs=out_specs,
        out_shape=out_shape,
        scratch_shapes=[pltpu.VMEM((ffn, d), BF16)] * 3 + [pltpu.SemaphoreType.DMA((3 * NDEV * LOAD_PIECES,))],
        args=args,
    )


def _ffn_bwd(dh, x, a, b, gain, weights, ffn, name, carries=()):
    s, d = x.shape
    tm = min(512, s)
    halves = 2
    fh = ffn // halves

    def body(dh_ref, x_ref, a_ref, b_ref, g_ref, b1, b3, b2, dx_ref, da_ref, db_ref, n_ref, dg_ref, w1s, w3s, w2s, sems):
        i, j = pl.program_id(0), pl.program_id(1)

        @pl.when((i == 0) & (j == 0))
        def _():
            for cp in _load_weights(((b1, w1s), (b3, w3s), (b2, w2s)), sems):
                cp.wait()
            dg_ref[...] = jnp.zeros_like(dg_ref)

        @pl.when(j == 0)
        def _():
            xv = x_ref[...]
            r = lax.rsqrt(jnp.mean(xv * xv, axis=-1, keepdims=True) + EPS)
            n_ref[...] = (xv * r * g_ref[...]).astype(BF16)
            dx_ref[...] = jnp.zeros_like(dx_ref)

        dob = (0.5 * dh_ref[...]).astype(BF16)
        chunks = _chunks(fh)

        def dhm_of(k):
            at, width = chunks[k]
            return _dot(dob, w2s[pl.ds(pl.multiple_of(j * fh + at, GROUP), width), :], NT)

        ahead = dhm_of(0)
        for k, (at, width) in enumerate(chunks):
            cols = slice(at, at + width)
            dhm = ahead
            if k + 1 < len(chunks):
                ahead = dhm_of(k + 1)
            for top in range(0, tm, ROW_BAND):
                band = slice(top, top + ROW_BAND)
                av = a_ref[band, cols].astype(F32)
                bv = b_ref[band, cols].astype(F32)
                sg = _sigmoid(av)
                dv = dhm[band]
                da_ref[band, cols] = (dv * bv * (sg * (1.0 + av * (1.0 - sg)))).astype(BF16)
                db_ref[band, cols] = (dv * (av * sg)).astype(BF16)
        half = pl.ds(pl.multiple_of(j * fh, GROUP), fh)
        dx_ref[...] += _dot(da_ref[...], w1s[half, :], NN) + _dot(db_ref[...], w3s[half, :], NN)

        @pl.when(j == halves - 1)
        def _():
            xv = x_ref[...]
            g = g_ref[...]
            r = lax.rsqrt(jnp.mean(xv * xv, axis=-1, keepdims=True) + EPS)
            nh = xv * r
            total = dx_ref[...]
            dg_ref[...] += jnp.sum(total * nh, axis=0, keepdims=True)
            dnh = total * g
            dx_ref[...] = dh_ref[...] + r * (dnh - nh * jnp.mean(dnh * nh, axis=-1, keepdims=True))

    tile = pl.BlockSpec((tm, d), lambda i, j: (i, 0))
    row = pl.BlockSpec((1, d), lambda i, j: (0, 0))
    wide = pl.BlockSpec((tm, fh), lambda i, j: (i, j))
    return _grid_call(
        body,
        carries,
        name=name,
        grid=(s // tm, halves),
        in_specs=[tile, tile, wide, wide, row, ANY, ANY, ANY],
        out_specs=[tile, wide, wide, tile, row],
        out_shape=[
            jax.ShapeDtypeStruct((s, d), F32),
            jax.ShapeDtypeStruct((s, ffn), BF16),
            jax.ShapeDtypeStruct((s, ffn), BF16),
            jax.ShapeDtypeStruct((s, d), BF16),
            jax.ShapeDtypeStruct((1, d), F32),
        ],
        scratch_shapes=[pltpu.VMEM((ffn, d), BF16)] * 3 + [pltpu.SemaphoreType.DMA((3 * NDEV * LOAD_PIECES,))],
        args=[dh, x, a, b, gain] + list(weights),
    )


SWAP_PIECES = 2


def _wgrad(lhs, rhs, scale, name, carries=()):
    s, m = lhs.shape
    n = rhs.shape[1]
    rs = m // NDEV
    tk = min(512, s)
    steps = s // tk
    pieces = [(j, at, size) for j in range(NCHIP) for at, size in _pieces(rs, SWAP_PIECES)]

    def body(l_ref, r_ref, o_ref, acc, mine, theirs, send_sems, recv_sems):
        k = pl.program_id(0)

        @pl.when(k == 0)
        def _():
            acc[...] = jnp.zeros_like(acc)

        acc[...] += _dot(l_ref[...], r_ref[...].astype(BF16), TN)

        @pl.when(k == steps - 1)
        def _():
            for p in range(NDEV):
                mine[p % 2, p // 2] = (acc[p * rs:(p + 1) * rs, :] * scale).astype(BF16)
            c = lax.axis_index("c")
            copies = [
                _remote(mine.at[1 - c, j, pl.ds(at, size), :], theirs.at[j, pl.ds(at, size), :],
                        send_sems.at[q], recv_sems.at[q], _peer(1))
                for q, (j, at, size) in enumerate(pieces)
            ]
            for cp in copies:
                cp.start()
            for cp in copies:
                cp.wait_recv()
            o_ref[...] = (mine[c].astype(F32) + theirs[...].astype(F32)).astype(BF16)
            for cp in copies:
                cp.wait_send()

    (out,), carried = _grid_call(
        body,
        carries,
        name=name,
        grid=(steps,),
        in_specs=[pl.BlockSpec((tk, m), lambda k: (k, 0)), pl.BlockSpec((tk, n), lambda k: (k, 0))],
        out_specs=[pl.BlockSpec((NCHIP, rs, n), lambda k: (0, 0, 0))],
        out_shape=[jax.ShapeDtypeStruct((NCHIP, rs, n), BF16)],
        scratch_shapes=[
            pltpu.VMEM((m, n), F32), pltpu.VMEM((2, NCHIP, rs, n), BF16), pltpu.VMEM((NCHIP, rs, n), BF16),
            pltpu.SemaphoreType.DMA((len(pieces),)), pltpu.SemaphoreType.DMA((len(pieces),)),
        ],
        args=[lhs, rhs],
    )
    return out, carried


def _mix_constants(s):
    c = GROUP
    lg = np.log1p(-np.exp2(-5.0 - np.arange(RET_HEADS, dtype=np.float32))).astype(np.float32)
    pos = np.arange(c, dtype=np.float32)
    rel = pos[:, None] - pos[None, :]
    decay = np.where(rel[None] >= 0, np.exp(lg[:, None, None] * np.maximum(rel, 0.0)[None]), 0.0).astype(np.float32)
    ktail = np.exp(lg[:, None] * (c - 1 - pos)[None, :]).astype(np.float32)
    qhead = np.exp(lg[:, None] * (pos + 1.0)[None, :]).astype(np.float32)
    chunk_decay = [float(v) for v in np.exp(lg * np.float32(c)).astype(np.float32)]
    ones = np.ones((1, 1, c), np.float32)
    inv_freq = (1.0 / (np.float32(ROPE_BASE) ** (np.arange(0, c, 2, dtype=np.float32) / np.float32(c)))).astype(np.float32)
    ang = (np.arange(s, dtype=np.float32)[:, None] * inv_freq[None, :]).astype(np.float32)
    cos, sin = np.cos(ang).astype(np.float32), np.sin(ang).astype(np.float32)
    return dict(
        decay=jnp.asarray(decay),
        ktail=jnp.asarray(ktail[:, :, None] * ones),
        qhead=jnp.asarray(qhead[:, :, None] * ones),
        chunk_decay=chunk_decay,
        cos=jnp.asarray(np.concatenate([cos, cos], axis=-1)),
        sin=jnp.asarray(np.concatenate([-sin, sin], axis=-1)),
    )


def _rope(t, cos, sin):
    return t * cos + pltpu.roll(t, GROUP // 2, axis=1) * sin


def _rope_bwd(dt, cos, sin):
    return dt * cos + pltpu.roll(dt * sin, GROUP // 2, axis=1)


def _window_sums(ext, w, forward):
    rows = ext.shape[0]
    acc, k = ext, 1
    while k < w:
        acc = acc + pltpu.roll(acc, k if forward else rows - k, axis=0)
        k *= 2
    return acc


def _pool_counts(tile, tm, w):
    t = lax.broadcasted_iota(jnp.int32, (tm, 1), 0) + tile * tm
    return jnp.minimum(t + 1, w).astype(F32)


def _mix_fwd(h1, gain, weights, pool_w, pool_scale, ret_gain, consts, carries=()):
    s, d = h1.shape
    pwid = N_POOL_GROUPS * GROUP
    rwid = RET_HEADS * GROUP
    inw = pwid + 4 * rwid
    tm = min(256, s)
    nck = tm // GROUP
    cd = consts["chunk_decay"]

    def body(h_ref, g_ref, bin_, bout, pw_ref, ps_ref, rg_ref, cos_ref, sin_ref, dec_ref, kt_ref, qh_ref,
             h2_ref, proj_ref, o_ref, rs_ref, wins, wouts, state, carry, mbuf, sems):
        i = pl.program_id(0)

        @pl.when(i == 0)
        def _():
            for cp in _load_weights(((bin_, wins), (bout, wouts)), sems):
                cp.wait()
            state[...] = jnp.zeros_like(state)
            carry[...] = jnp.zeros_like(carry)

        hv = h_ref[...]
        r = lax.rsqrt(jnp.mean(hv * hv, axis=-1, keepdims=True) + EPS)
        u = (hv * r * g_ref[...]).astype(BF16)
        proj_ref[...] = _dot(u, wins[...], NT)

        ext = jnp.concatenate([carry[...], proj_ref[:, 0:pwid]], axis=0)
        carry[...] = proj_ref[tm - MAX_WINDOW:tm, 0:pwid]
        for gi, w in enumerate(POOL_WINDOWS):
            cols = slice(gi * GROUP, (gi + 1) * GROUP)
            xg = ext[:, cols]
            ws = _window_sums(xg, w, True)[MAX_WINDOW:, :]
            pooled = ws / _pool_counts(i, tm, w) - xg[MAX_WINDOW:, :]
            z = _dot(pooled.astype(BF16), pw_ref[gi].astype(BF16), NN)
            mbuf[:, cols] = (z * ps_ref[:, cols]).astype(BF16)

        cos, sin = cos_ref[...], sin_ref[...]
        for h in range(RET_HEADS):
            cq = slice(pwid + h * GROUP, pwid + (h + 1) * GROUP)
            ck = slice(pwid + rwid + h * GROUP, pwid + rwid + (h + 1) * GROUP)
            cv = slice(pwid + 2 * rwid + h * GROUP, pwid + 2 * rwid + (h + 1) * GROUP)
            cg = slice(pwid + 3 * rwid + h * GROUP, pwid + 3 * rwid + (h + 1) * GROUP)
            ch = slice(h * GROUP, (h + 1) * GROUP)
            qr = _rope(proj_ref[:, cq], cos, sin)
            kr = _rope(proj_ref[:, ck], cos, sin) * (GROUP ** -0.5)
            vb = proj_ref[:, cv].astype(BF16)
            for n in range(nck):
                rows = slice(n * GROUP, (n + 1) * GROUP)
                qc, kc, vc = qr[rows], kr[rows], vb[rows]
                rb = state[h]
                rs_ref[n, h] = rb
                p = (_dot(qc.astype(BF16), kc.astype(BF16), NT) * dec_ref[h]).astype(BF16)
                o = _dot(p, vc, NN) + _dot((qc * qh_ref[h]).astype(BF16), rb.astype(BF16), NN)
                state[h] = cd[h] * rb + _dot((kc * kt_ref[h]).astype(BF16), vc, TN)
                o_ref[rows, ch] = o
                on = o * lax.rsqrt(jnp.mean(o * o, axis=-1, keepdims=True) + EPS)
                gv = proj_ref[rows, cg]
                mbuf[rows, pwid + h * GROUP:pwid + (h + 1) * GROUP] = (
                    gv * _sigmoid(gv) * (on * rg_ref[:, ch])
                ).astype(BF16)
        h2_ref[...] = hv + _dot(mbuf[...], wouts[...], NN)

    tile = pl.BlockSpec((tm, d), lambda i: (i, 0))
    full = lambda shape: pl.BlockSpec(shape, lambda i: (0,) * len(shape))
    return _grid_call(
        body,
        carries,
        name="mix_fwd",
        grid=(s // tm,),
        in_specs=[
            tile, full((1, d)), ANY, ANY,
            full((N_POOL_GROUPS, GROUP, GROUP)), full((1, pwid)), full((1, rwid)),
            pl.BlockSpec((tm, GROUP), lambda i: (i, 0)), pl.BlockSpec((tm, GROUP), lambda i: (i, 0)),
            full((RET_HEADS, GROUP, GROUP)), full((RET_HEADS, GROUP, GROUP)), full((RET_HEADS, GROUP, GROUP)),
        ],
        out_specs=[
            tile,
            pl.BlockSpec((tm, inw), lambda i: (i, 0)),
            pl.BlockSpec((tm, rwid), lambda i: (i, 0)),
            pl.BlockSpec((nck, RET_HEADS, GROUP, GROUP), lambda i: (i, 0, 0, 0)),
        ],
        out_shape=[
            jax.ShapeDtypeStruct((s, d), F32),
            jax.ShapeDtypeStruct((s, inw), F32),
            jax.ShapeDtypeStruct((s, rwid), F32),
            jax.ShapeDtypeStruct((s // GROUP, RET_HEADS, GROUP, GROUP), F32),
        ],
        scratch_shapes=[
            pltpu.VMEM((inw, d), BF16), pltpu.VMEM((d, d), BF16),
            pltpu.VMEM((RET_HEADS, GROUP, GROUP), F32), pltpu.VMEM((MAX_WINDOW, pwid), F32),
            pltpu.VMEM((tm, d), BF16), pltpu.SemaphoreType.DMA((2 * NDEV * LOAD_PIECES,)),
        ],
        args=[h1, gain, weights[0], weights[1], pool_w, pool_scale, ret_gain,
              consts["cos"], consts["sin"], consts["decay"], consts["ktail"], consts["qhead"]],
    )


def _mix_bwd(dh2, h1, proj, o_saved, rsave, gain, weights, pool_w, pool_scale, ret_gain, consts, carries=()):
    s, d = h1.shape
    pwid = N_POOL_GROUPS * GROUP
    rwid = RET_HEADS * GROUP
    inw = pwid + 4 * rwid
    tm = min(256, s)
    nck = tm // GROUP
    nt = s // tm
    cd = consts["chunk_decay"]
    halo_per_tile = tm // MAX_WINDOW

    def body(dh2_ref, h_ref, proj_ref, halo_ref, o_ref, rs_ref, g_ref, bin_, bout, pw_ref, ps_ref, rg_ref,
             cos_ref, sin_ref, dec_ref, kt_ref, qh_ref,
             dh1_ref, dproj_ref, u_ref, m_ref, dpw_ref, dps_ref, drg_ref, dg_ref,
             wins, wouts, dstate, carry, dm, dpj, sems):
        i = pl.program_id(0)
        tile = nt - 1 - i

        @pl.when(i == 0)
        def _():
            for cp in _load_weights(((bin_, wins), (bout, wouts)), sems):
                cp.wait()
            dstate[...] = jnp.zeros_like(dstate)
            carry[...] = jnp.zeros_like(carry)
            for ref in (dpw_ref, dps_ref, drg_ref, dg_ref):
                ref[...] = jnp.zeros_like(ref)

        dh2v = dh2_ref[...]
        dm[...] = _dot(dh2v.astype(BF16), wouts[...], NT)
        hv = h_ref[...]
        g = g_ref[...]
        r = lax.rsqrt(jnp.mean(hv * hv, axis=-1, keepdims=True) + EPS)
        uh = hv * r
        u_ref[...] = (uh * g).astype(BF16)

        halo = jnp.where(tile == 0, 0.0, halo_ref[...])
        ext = jnp.concatenate([halo, proj_ref[:, 0:pwid]], axis=0)
        next_dpn = carry[...]
        for gi, w in enumerate(POOL_WINDOWS):
            cols = slice(gi * GROUP, (gi + 1) * GROUP)
            xg = ext[:, cols]
            cnt = _pool_counts(tile, tm, w)
            pooled = (_window_sums(xg, w, True)[MAX_WINDOW:, :] / cnt - xg[MAX_WINDOW:, :]).astype(BF16)
            pwb = pw_ref[gi].astype(BF16)
            z = _dot(pooled, pwb, NN)
            scale = ps_ref[:, cols]
            m_ref[:, cols] = (z * scale).astype(BF16)
            da = dm[:, cols]
            dps_ref[:, cols] += jnp.sum(da * z, axis=0, keepdims=True)
            dz = (da * scale).astype(BF16)
            dpw_ref[gi] += _dot(pooled, dz, TN)
            dpl = _dot(dz, pwb, NT)
            dpn = dpl / cnt
            ext2 = jnp.concatenate([dpn, next_dpn[:, cols]], axis=0)
            dpj[:, cols] = (_window_sums(ext2, w, False)[0:tm, :] - dpl).astype(BF16)
            carry[:, cols] = dpn[0:MAX_WINDOW, :]

        cos, sin = cos_ref[...], sin_ref[...]
        for h in range(RET_HEADS):
            cq = slice(pwid + h * GROUP, pwid + (h + 1) * GROUP)
            ck = slice(pwid + rwid + h * GROUP, pwid + rwid + (h + 1) * GROUP)
            cv = slice(pwid + 2 * rwid + h * GROUP, pwid + 2 * rwid + (h + 1) * GROUP)
            cg = slice(pwid + 3 * rwid + h * GROUP, pwid + 3 * rwid + (h + 1) * GROUP)
            ch = slice(h * GROUP, (h + 1) * GROUP)
            qr = _rope(proj_ref[:, cq], cos, sin)
            kr = _rope(proj_ref[:, ck], cos, sin) * (GROUP ** -0.5)
            vb = proj_ref[:, cv].astype(BF16)
            gv = proj_ref[:, cg]
            ov = o_ref[:, ch]
            ro = lax.rsqrt(jnp.mean(ov * ov, axis=-1, keepdims=True) + EPS)
            on = ov * ro
            rg = rg_ref[:, ch]
            db = dm[:, pwid + h * GROUP:pwid + (h + 1) * GROUP]
            sg = _sigmoid(gv)
            sl = gv * sg
            m_ref[:, pwid + h * GROUP:pwid + (h + 1) * GROUP] = (sl * (on * rg)).astype(BF16)
            dpj[:, cg] = (db * (on * rg) * (sg * (1.0 + gv * (1.0 - sg)))).astype(BF16)
            drg_ref[:, ch] += jnp.sum(db * sl * on, axis=0, keepdims=True)
            don = db * sl * rg
            do = (ro * (don - on * jnp.mean(don * on, axis=-1, keepdims=True))).astype(BF16)
            for n in reversed(range(nck)):
                rows = slice(n * GROUP, (n + 1) * GROUP)
                qc, kc, vc, dob = qr[rows], kr[rows], vb[rows], do[rows]
                qcb, kcb = qc.astype(BF16), kc.astype(BF16)
                qh = (qc * qh_ref[h]).astype(BF16)
                kt = (kc * kt_ref[h]).astype(BF16)
                rn = rs_ref[n, h].astype(BF16)
                dnext = dstate[h]
                dnb = dnext.astype(BF16)
                dec = dec_ref[h]
                p = (_dot(qcb, kcb, NT) * dec).astype(BF16)
                ds = (_dot(dob, vc, NT) * dec).astype(BF16)
                dv = _dot(p, dob, TN) + _dot(kt, dnb, NN)
                dq = _dot(ds, kcb, NN) + _dot(dob, rn, NT) * qh_ref[h]
                dk = _dot(ds, qcb, TN) + _dot(vc, dnb, NT) * kt_ref[h]
                dstate[h] = cd[h] * dnext + _dot(qh, dob, TN)
                dpj[rows, cq] = _rope_bwd(dq, cos[rows], sin[rows]).astype(BF16)
                dpj[rows, ck] = _rope_bwd(dk * (GROUP ** -0.5), cos[rows], sin[rows]).astype(BF16)
                dpj[rows, cv] = dv.astype(BF16)

        dproj_ref[...] = dpj[...]
        du = _dot(dpj[...], wins[...], NN)
        dg_ref[...] += jnp.sum(du * uh, axis=0, keepdims=True)
        dn = du * g
        dh1_ref[...] = dh2v + r * (dn - uh * jnp.mean(dn * uh, axis=-1, keepdims=True))

    rev = lambda i: (nt - 1 - i, 0)
    tile = pl.BlockSpec((tm, d), rev)
    full = lambda shape: pl.BlockSpec(shape, lambda i: (0,) * len(shape))
    return _grid_call(
        body,
        carries,
        name="mix_bwd",
        grid=(nt,),
        in_specs=[
            tile, tile,
            pl.BlockSpec((tm, inw), rev),
            pl.BlockSpec((MAX_WINDOW, pwid), lambda i: (jnp.maximum((nt - 1 - i) * halo_per_tile - 1, 0), 0)),
            pl.BlockSpec((tm, rwid), rev),
            pl.BlockSpec((nck, RET_HEADS, GROUP, GROUP), lambda i: (nt - 1 - i, 0, 0, 0)),
            full((1, d)), ANY, ANY,
            full((N_POOL_GROUPS, GROUP, GROUP)), full((1, pwid)), full((1, rwid)),
            pl.BlockSpec((tm, GROUP), rev), pl.BlockSpec((tm, GROUP), rev),
            full((RET_HEADS, GROUP, GROUP)), full((RET_HEADS, GROUP, GROUP)), full((RET_HEADS, GROUP, GROUP)),
        ],
        out_specs=[
            tile, pl.BlockSpec((tm, inw), rev), tile, tile,
            full((N_POOL_GROUPS, GROUP, GROUP)), full((1, pwid)), full((1, rwid)), full((1, d)),
        ],
        out_shape=[
            jax.ShapeDtypeStruct((s, d), F32),
            jax.ShapeDtypeStruct((s, inw), BF16),
            jax.ShapeDtypeStruct((s, d), BF16),
            jax.ShapeDtypeStruct((s, d), BF16),
            jax.ShapeDtypeStruct((N_POOL_GROUPS, GROUP, GROUP), F32),
            jax.ShapeDtypeStruct((1, pwid), F32),
            jax.ShapeDtypeStruct((1, rwid), F32),
            jax.ShapeDtypeStruct((1, d), F32),
        ],
        scratch_shapes=[
            pltpu.VMEM((inw, d), BF16), pltpu.VMEM((d, d), BF16),
            pltpu.VMEM((RET_HEADS, GROUP, GROUP), F32), pltpu.VMEM((MAX_WINDOW, pwid), F32),
            pltpu.VMEM((tm, d), F32), pltpu.VMEM((tm, inw), BF16), pltpu.SemaphoreType.DMA((2 * NDEV * LOAD_PIECES,)),
        ],
        args=[dh2, h1, proj, proj, o_saved, rsave, gain, weights[0], weights[1], pool_w, pool_scale, ret_gain,
              consts["cos"], consts["sin"], consts["decay"], consts["ktail"], consts["qhead"]],
    )


def _adam(w, g, m, v):
    m = ADAM_B1 * m + (1.0 - ADAM_B1) * g
    v = ADAM_B2 * v + (1.0 - ADAM_B2) * jnp.square(g)
    m_hat = m / (1.0 - ADAM_B1 ** ADAM_STEP)
    v_hat = v / (1.0 - ADAM_B2 ** ADAM_STEP)
    delta = -ADAM_LR * (m_hat / (jnp.sqrt(v_hat) + ADAM_EPS) + ADAM_WD * w)
    return delta, m, v


def _adamw_big(w, parts, m, v, name):
    rows, d = w.shape
    tr = _row_tile(rows, 512)

    def body(w_ref, p_ref, m_ref, v_ref, g_ref, d_ref, nm_ref, nv_ref):
        g = p_ref[0].astype(F32)
        for q in range(1, NCHIP):
            g = g + p_ref[q].astype(F32)
        g_ref[...] = g
        d_ref[...], nm_ref[...], nv_ref[...] = _adam(w_ref[...], g, m_ref[...], v_ref[...])

    spec = pl.BlockSpec((tr, d), lambda i: (i, 0))
    return _call(
        body,
        name=name,
        grid=(rows // tr,),
        in_specs=[spec, pl.BlockSpec((NCHIP, tr, d), lambda i: (0, i, 0)), spec, spec],
        out_specs=[spec] * 4,
        out_shape=[jax.ShapeDtypeStruct((rows, d), F32)] * 4,
        compiler_params=_seq(1),
    )(w, parts, m, v)


def _adamw_small(stats_all, pw_all, ws, ms, vs, d, pwid):
    nsmall = len(ws)

    def body(*refs):
        st_ref, pwa_ref = refs[0], refs[1]
        w_refs = refs[2:2 + nsmall]
        m_refs = refs[2 + nsmall:2 + 2 * nsmall]
        v_refs = refs[2 + 2 * nsmall:2 + 3 * nsmall]
        outs = refs[2 + 3 * nsmall:]
        st = st_ref[0]
        pwg = pwa_ref[0]
        for q in range(1, NDEV):
            st = st + st_ref[q]
            pwg = pwg + pwa_ref[q]
        grads = [st[0:1, :], st[1:2, :], st[2:3, :], st[3:4, :], st[4:5, 0:pwid], st[4:5, pwid:2 * pwid], pwg]
        outs[0][...] = jnp.zeros((1, GROUP), F32) + jnp.sum(st[5:6, :])
        for j in range(nsmall):
            delta, nm, nv = _adam(w_refs[j][...], grads[j], m_refs[j][...], v_refs[j][...])
            outs[1 + 4 * j][...] = grads[j]
            outs[2 + 4 * j][...] = delta
            outs[3 + 4 * j][...] = nm
            outs[4 + 4 * j][...] = nv

    out_shape = [jax.ShapeDtypeStruct((1, GROUP), F32)]
    for w in ws:
        out_shape += [jax.ShapeDtypeStruct(w.shape, F32)] * 4
    return _call(body, name="adamw_small", out_shape=out_shape, compiler_params=_params())(
        stats_all, pw_all, *ws, *ms, *vs
    )


def kernel(x, ffn1_norm, ffn1_w1, ffn1_w3, ffn1_w2, mix_norm, w_in, pool_w, pool_scale, ret_norm, w_out, ffn2_norm, ffn2_w1, ffn2_w3, ffn2_w2, final_norm, loss_target, m_ffn1_norm, m_ffn1_w1, m_ffn1_w3, m_ffn1_w2, m_mix_norm, m_w_in, m_pool_w, m_pool_scale, m_ret_norm, m_w_out, m_ffn2_norm, m_ffn2_w1, m_ffn2_w3, m_ffn2_w2, m_final_norm, v_ffn1_norm, v_ffn1_w1, v_ffn1_w3, v_ffn1_w2, v_mix_norm, v_w_in, v_pool_w, v_pool_scale, v_ret_norm, v_w_out, v_ffn2_norm, v_ffn2_w1, v_ffn2_w3, v_ffn2_w2, v_final_norm):
    s, d = x.shape[1], x.shape[2]
    ffn = ffn1_w1.shape[2] * NDEV
    pwid = pool_scale.shape[1]
    xs, tgt = x[0], loss_target[0]
    consts = _mix_constants(s)
    pw3 = pool_w[0]
    fnorm = final_norm.reshape(1, d)

    rows_of = lambda w, transposed: (w[0].T if transposed else w[0]).astype(BF16)
    send_f1 = [rows_of(ffn1_w1, True), rows_of(ffn1_w3, True), rows_of(ffn1_w2, False)]
    send_mix = [rows_of(w_in, True), rows_of(w_out, False)]
    send_f2 = [rows_of(ffn2_w1, True), rows_of(ffn2_w3, True), rows_of(ffn2_w2, False)]

    (w_f1,) = _comm_call([_Gather(send_f1)], "gather_ffn1")
    (h1, a1, b1, hm1), (more,) = _ffn_fwd(xs, ffn1_norm, w_f1, ffn, carries=[_Gather(send_mix + send_f2[:1])])
    w_mix = more[:2]
    (h2, proj, o_saved, rsave), (rest,) = _mix_fwd(
        h1, mix_norm, w_mix, pw3, pool_scale, ret_norm, consts, carries=[_Gather(send_f2[1:])]
    )
    w_f2 = more[2:] + rest
    (dh3, a2, b2, hm2, dgf, loss_cols), _ = _ffn_fwd(h2, ffn2_norm, w_f2, ffn, head=(fnorm, tgt))

    (dh2, da2, db2, n2, dg2), _ = _ffn_bwd(dh3, h2, a2, b2, ffn2_norm, w_f2, ffn, "ffn2_bwd")
    sum_f2w1, _ = _wgrad(da2, n2, 1.0, "ffn2_w1_grad")
    sum_f2w3, ((parts_f2w1,),) = _wgrad(db2, n2, 1.0, "ffn2_w3_grad", carries=[_ChipScatter([sum_f2w1])])
    sum_f2w2, ((parts_f2w3,),) = _wgrad(hm2, dh3, 0.5, "ffn2_w2_grad", carries=[_ChipScatter([sum_f2w3])])

    (dh1, dproj, u, mm, dpw, dps, drg, dgm), ((parts_f2w2,),) = _mix_bwd(
        dh2, h1, proj, o_saved, rsave, mix_norm, w_mix, pw3, pool_scale, ret_norm, consts,
        carries=[_ChipScatter([sum_f2w2])],
    )
    sum_in, _ = _wgrad(dproj, u, 1.0, "w_in_grad")
    sum_out, _ = _wgrad(mm, dh2, 1.0, "w_out_grad")

    (dx, da1, db1, n1, dg1), ((parts_in, parts_out),) = _ffn_bwd(
        dh1, xs, a1, b1, ffn1_norm, w_f1, ffn, "ffn1_bwd", carries=[_ChipScatter([sum_in, sum_out])]
    )
    sum_f1w2, _ = _wgrad(hm1, dh1, 0.5, "ffn1_w2_grad")
    sum_f1w1, ((parts_f1w2,),) = _wgrad(da1, n1, 1.0, "ffn1_w1_grad", carries=[_ChipScatter([sum_f1w2])])
    sum_f1w3, ((parts_f1w1,),) = _wgrad(db1, n1, 1.0, "ffn1_w3_grad", carries=[_ChipScatter([sum_f1w1])])

    stats = jnp.concatenate(
        [dg1, dgm, dg2, dgf, jnp.concatenate([dps, drg], axis=1), loss_cols, jnp.zeros((2, d), F32)], axis=0
    )
    (parts_f1w3,), (stats_all, pw_all) = _comm_call(
        [_ChipScatter([sum_f1w3]), _GatherDirect([stats, dpw.reshape(N_POOL_GROUPS * GROUP, GROUP)])], "scatter_last"
    )

    big = (
        (ffn1_w1, m_ffn1_w1, v_ffn1_w1, parts_f1w1, True),
        (ffn1_w3, m_ffn1_w3, v_ffn1_w3, parts_f1w3, True),
        (ffn1_w2, m_ffn1_w2, v_ffn1_w2, parts_f1w2, False),
        (w_in, m_w_in, v_w_in, parts_in, True),
        (w_out, m_w_out, v_w_out, parts_out, False),
        (ffn2_w1, m_ffn2_w1, v_ffn2_w1, parts_f2w1, True),
        (ffn2_w3, m_ffn2_w3, v_ffn2_w3, parts_f2w3, True),
        (ffn2_w2, m_ffn2_w2, v_ffn2_w2, parts_f2w2, False),
    )
    big_out = []
    for j, (w, m, v, parts, t) in enumerate(big):
        view = (lambda a: a[0].T) if t else (lambda a: a[0])
        back = (lambda a: a.T[None]) if t else (lambda a: a[None])
        big_out.append([back(a) for a in _adamw_big(view(w), parts, view(m), view(v), "adamw_%d" % j)])

    small_w = (ffn1_norm, mix_norm, ffn2_norm, fnorm, pool_scale, ret_norm, pw3.reshape(-1, GROUP))
    small_m = (m_ffn1_norm, m_mix_norm, m_ffn2_norm, m_final_norm.reshape(1, d), m_pool_scale, m_ret_norm, m_pool_w.reshape(-1, GROUP))
    small_v = (v_ffn1_norm, v_mix_norm, v_ffn2_norm, v_final_norm.reshape(1, d), v_pool_scale, v_ret_norm, v_pool_w.reshape(-1, GROUP))
    res = _adamw_small(stats_all, pw_all, small_w, small_m, small_v, d, pwid)
    loss = res[0][0, 0]
    small_out = [list(res[1 + 4 * j:5 + 4 * j]) for j in range(len(small_w))]
    small_out[3] = [a.reshape(d) for a in small_out[3]]
    small_out[6] = [a.reshape(pool_w.shape) for a in small_out[6]]

    order = [small_out[0], big_out[0], big_out[1], big_out[2], small_out[1], big_out[3], small_out[6], small_out[4],
             small_out[5], big_out[4], small_out[2], big_out[5], big_out[6], big_out[7], small_out[3]]
    result = [loss, dx[None]]
    for kind in range(4):
        result += [t[kind] for t in order]
    return tuple(result)
```

```python
import numpy as np
import jax
import jax.numpy as jnp
from jax import lax
from jax.experimental import pallas as pl
from jax.experimental.pallas import tpu as pltpu

F32 = jnp.float32
BF16 = jnp.bfloat16

NDEV = 8
NCHIP = 4
EPS = 1e-6
N_POOL_GROUPS = 4
POOL_WINDOWS = (2, 4, 8, 16)
MAX_WINDOW = 16
GROUP = 128
RET_HEADS = 4
ROPE_BASE = 10000.0
ADAM_LR = 0.001
ADAM_B1 = 0.9
ADAM_B2 = 0.999
ADAM_EPS = 1e-08
ADAM_WD = 0.01
ADAM_STEP = 10

VMEM_LIMIT = 56 * 1024 * 1024
FFN_CHUNK = 256
LOAD_PIECES = 4
ROW_BAND = 32

NT = (((1,), (1,)), ((), ()))
NN = (((1,), (0,)), ((), ()))
TN = (((0,), (0,)), ((), ()))

ANY = pl.BlockSpec(memory_space=pl.ANY)


def _dot(a, b, dims):
    return lax.dot_general(a, b, dims, preferred_element_type=F32)


def _call(body, **kw):
    return pl.pallas_call(body, **kw)


def _params(**kw):
    return pltpu.CompilerParams(vmem_limit_bytes=VMEM_LIMIT, **kw)


def _seq(n):
    return _params(dimension_semantics=("arbitrary",) * n)


def _peer(k):
    x, y, c = lax.axis_index("x"), lax.axis_index("y"), lax.axis_index("c")
    return (1 - x if k & 4 else x, 1 - y if k & 2 else y, 1 - c if k & 1 else c)


def _flat(pos):
    return 4 * pos[0] + 2 * pos[1] + pos[2]


def _chip(pos):
    return 2 * pos[0] + pos[1]


def _row_tile(rows, cap):
    return max(t for t in range(16, min(rows, cap) + 1, 16) if rows % t == 0)


def _pieces(rows, n):
    tiles = rows // 16
    cuts = [16 * (tiles * q // n) for q in range(n + 1)]
    return [(a, b - a) for a, b in zip(cuts[:-1], cuts[1:])]


def _load_weights(parts, sems):
    copies = []
    for buf, dst in parts:
        rows = buf.shape[1]
        for p in range(NDEV):
            for at, size in _pieces(rows, LOAD_PIECES):
                cp = pltpu.make_async_copy(
                    buf.at[p, pl.ds(at, size), :], dst.at[pl.ds(p * rows + at, size), :], sems.at[len(copies)]
                )
                cp.start()
                copies.append(cp)
    return copies


def _sigmoid(a):
    return 1.0 / (1.0 + jnp.exp(-a))


def _remote(src, dst, send_sem, recv_sem, to):
    return pltpu.make_async_remote_copy(
        src_ref=src, dst_ref=dst, send_sem=send_sem, recv_sem=recv_sem, device_id=to, device_id_type=pl.DeviceIdType.MESH
    )


class _Gather:
    chips = (4, 2, 6)

    def __init__(self, shards):
        n = len(shards)
        self.operands = list(shards)
        self.out_shape = [jax.ShapeDtypeStruct((NDEV,) + a.shape, a.dtype) for a in shards]
        self.sems = [pltpu.SemaphoreType.DMA((7 * n,)), pltpu.SemaphoreType.DMA((7 * n,)), pltpu.SemaphoreType.DMA((n,))]

    def _copy(self, t, k, block, to, ins, outs, sems, own=False):
        dst = outs[t].at[_flat(block)]
        return _remote(ins[t] if own else dst, dst, sems[0].at[7 * t + k], sems[1].at[7 * t + k], to)

    def begin(self, ins, outs, sems):
        me, sibling = _peer(0), _peer(1)
        for t in range(len(ins)):
            pltpu.make_async_copy(ins[t], outs[t].at[_flat(me)], sems[2].at[t]).start()
            self._copy(t, 0, me, sibling, ins, outs, sems, own=True).start()
            for j, k in enumerate(self.chips):
                self._copy(t, 1 + j, me, _peer(k), ins, outs, sems, own=True).start()

    def mid(self, ins, outs, sems):
        me, sibling = _peer(0), _peer(1)
        for t in range(len(ins)):
            for j, k in enumerate(self.chips):
                self._copy(t, 1 + j, _peer(k), me, ins, outs, sems).wait_recv()
                self._copy(t, 4 + j, _peer(k), sibling, ins, outs, sems).start()

    def end(self, ins, outs, sems):
        me, sibling = _peer(0), _peer(1)
        for t in range(len(ins)):
            self._copy(t, 0, sibling, me, ins, outs, sems).wait_recv()
            for j, k in enumerate(self.chips):
                self._copy(t, 4 + j, _peer(k ^ 1), me, ins, outs, sems).wait_recv()
            for k in range(7):
                self._copy(t, k, me, me, ins, outs, sems).wait_send()
            pltpu.make_async_copy(ins[t], outs[t].at[_flat(me)], sems[2].at[t]).wait()


class _GatherDirect:
    def __init__(self, arrays):
        n = len(arrays)
        self.operands = list(arrays)
        self.out_shape = [jax.ShapeDtypeStruct((NDEV,) + a.shape, a.dtype) for a in arrays]
        self.sems = [pltpu.SemaphoreType.DMA((7 * n,)), pltpu.SemaphoreType.DMA((7 * n,)), pltpu.SemaphoreType.DMA((n,))]

    def begin(self, ins, outs, sems):
        mine = _flat(_peer(0))
        for t in range(len(ins)):
            pltpu.make_async_copy(ins[t], outs[t].at[mine], sems[2].at[t]).start()
            for k in range(1, NDEV):
                _remote(ins[t], outs[t].at[mine], sems[0].at[7 * t + k - 1], sems[1].at[7 * t + k - 1], _peer(k)).start()

    def mid(self, ins, outs, sems):
        pass

    def end(self, ins, outs, sems):
        mine = _flat(_peer(0))
        for t in range(len(ins)):
            for k in range(1, NDEV):
                cp = _remote(ins[t], outs[t].at[_flat(_peer(k))], sems[0].at[7 * t + k - 1], sems[1].at[7 * t + k - 1], _peer(k))
                cp.wait_recv()
                cp.wait_send()
            pltpu.make_async_copy(ins[t], outs[t].at[mine], sems[2].at[t]).wait()


class _ChipScatter:
    pieces = 2

    def __init__(self, sums):
        n = len(sums) * NCHIP * self.pieces
        self.operands = list(sums)
        self.out_shape = [jax.ShapeDtypeStruct(a.shape, a.dtype) for a in sums]
        self.sems = [pltpu.SemaphoreType.DMA((n,)), pltpu.SemaphoreType.DMA((n,))]

    def _copies(self, ins, outs, sems, arriving):
        mine = _chip(_peer(0))
        copies = []
        for t in range(len(ins)):
            rows = ins[t].shape[1] // self.pieces
            for k in (0, 4, 2, 6):
                other = _chip(_peer(k))
                for q in range(self.pieces):
                    part = pl.ds(q * rows, rows)
                    at = len(copies)
                    if k == 0:
                        cp = pltpu.make_async_copy(ins[t].at[mine, part, :], outs[t].at[mine, part, :], sems[0].at[at])
                    else:
                        landing = outs[t].at[other if arriving else mine, part, :]
                        cp = _remote(ins[t].at[other, part, :], landing, sems[0].at[at], sems[1].at[at], _peer(k))
                    copies.append(cp)
        return copies

    def begin(self, ins, outs, sems):
        for cp in self._copies(ins, outs, sems, False):
            cp.start()

    def mid(self, ins, outs, sems):
        pass

    def end(self, ins, outs, sems):
        for at, cp in enumerate(self._copies(ins, outs, sems, True)):
            if at % (NCHIP * self.pieces) < self.pieces:
                cp.wait()
            else:
                cp.wait_recv()
                cp.wait_send()


def _split_refs(refs, counts):
    out, at = [], 0
    for n in counts:
        out.append(refs[at:at + n])
        at += n
    return out


def _comm_call(carries, name):
    nin = [len(c.operands) for c in carries]
    nout = [len(c.out_shape) for c in carries]
    nsem = [len(c.sems) for c in carries]

    def body(*refs):
        ins, outs, sems = _split_refs(refs, (sum(nin), sum(nout), sum(nsem)))
        parts = list(zip(carries, _split_refs(ins, nin), _split_refs(outs, nout), _split_refs(sems, nsem)))
        for c, i, o, s in parts:
            c.begin(i, o, s)
        for c, i, o, s in parts:
            c.mid(i, o, s)
        for c, i, o, s in parts:
            c.end(i, o, s)

    res = _call(
        body,
        name=name,
        out_shape=[sh for c in carries for sh in c.out_shape],
        in_specs=[ANY] * sum(nin),
        out_specs=[ANY] * sum(nout),
        scratch_shapes=[sm for c in carries for sm in c.sems],
        compiler_params=pltpu.CompilerParams(has_side_effects=True),
    )(*[a for c in carries for a in c.operands])
    return _split_refs(list(res), nout)


def _grid_call(body, carries, *, name, grid, in_specs, out_specs, out_shape, scratch_shapes, args):
    ni, no, ns = len(in_specs), len(out_specs), len(scratch_shapes)
    nin = [len(c.operands) for c in carries]
    nout = [len(c.out_shape) for c in carries]
    nsem = [len(c.sems) for c in carries]
    steps = int(np.prod(grid))
    mid_step = max(steps - 2, 0)

    def wrapped(*refs):
        ins, cins, outs, couts, scr, csems = _split_refs(refs, (ni, sum(nin), no, sum(nout), ns, sum(nsem)))
        if not carries:
            return body(*ins, *outs, *scr)
        parts = list(zip(carries, _split_refs(cins, nin), _split_refs(couts, nout), _split_refs(csems, nsem)))
        step = pl.program_id(0)
        for axis in range(1, len(grid)):
            step = step * grid[axis] + pl.program_id(axis)

        @pl.when(step == 0)
        def _():
            for c, i, o, s in parts:
                c.begin(i, o, s)

        body(*ins, *outs, *scr)

        @pl.when(step == mid_step)
        def _():
            for c, i, o, s in parts:
                c.mid(i, o, s)

        @pl.when(step == steps - 1)
        def _():
            for c, i, o, s in parts:
                c.end(i, o, s)

    res = _call(
        wrapped,
        name=name,
        grid=tuple(grid),
        in_specs=list(in_specs) + [ANY] * sum(nin),
        out_specs=list(out_specs) + [ANY] * sum(nout),
        out_shape=list(out_shape) + [sh for c in carries for sh in c.out_shape],
        scratch_shapes=list(scratch_shapes) + [sm for c in carries for sm in c.sems],
        compiler_params=_seq(len(grid)),
    )(*args, *[a for c in carries for a in c.operands])
    res = list(res)
    return res[:no], _split_refs(res[no:], nout)


def _chunks(width):
    return [(at, min(FFN_CHUNK, width - at)) for at in range(0, width, FFN_CHUNK)]


def _ffn_fwd(x, gain, weights, ffn, head=None, carries=()):
    s, d = x.shape
    tm = min(512, s)

    def body(*refs):
        if head is None:
            x_ref, g_ref, b1, b3, b2, h_ref, a_ref, b_ref, hm_ref, w1s, w3s, w2s, sems = refs
        else:
            x_ref, g_ref, b1, b3, b2, gf_ref, t_ref, h_ref, a_ref, b_ref, hm_ref, dgf_ref, loss_ref, w1s, w3s, w2s, sems = refs
        i = pl.program_id(0)

        @pl.when(i == 0)
        def _():
            for cp in _load_weights(((b1, w1s), (b3, w3s), (b2, w2s)), sems):
                cp.wait()
            if head is not None:
                dgf_ref[...] = jnp.zeros_like(dgf_ref)
                loss_ref[...] = jnp.zeros_like(loss_ref)

        xv = x_ref[...]
        r = lax.rsqrt(jnp.mean(xv * xv, axis=-1, keepdims=True) + EPS)
        n = (xv * r * g_ref[...]).astype(BF16)
        acc = jnp.zeros((tm, d), F32)
        for at, width in _chunks(ffn):
            cols = slice(at, at + width)
            a = _dot(n, w1s[cols, :], NT)
            b = _dot(n, w3s[cols, :], NT)
            a_ref[:, cols] = a.astype(BF16)
            b_ref[:, cols] = b.astype(BF16)
            hm = (a * _sigmoid(a) * b).astype(BF16)
            hm_ref[:, cols] = hm
            acc = acc + _dot(hm, w2s[cols, :], NN)
        h = xv + 0.5 * acc
        if head is None:
            h_ref[...] = h
        else:
            rf = lax.rsqrt(jnp.mean(h * h, axis=-1, keepdims=True) + EPS)
            nh = h * rf
            gf = gf_ref[...]
            err = nh * gf - t_ref[...]
            loss_ref[...] += jnp.sum(err * err, axis=0, keepdims=True) * (0.5 / d)
            dy = err * (1.0 / d)
            dgf_ref[...] += jnp.sum(dy * nh, axis=0, keepdims=True)
            dn = dy * gf
            h_ref[...] = rf * (dn - nh * jnp.mean(dn * nh, axis=-1, keepdims=True))

    tile = pl.BlockSpec((tm, d), lambda i: (i, 0))
    row = pl.BlockSpec((1, d), lambda i: (0, 0))
    wide = pl.BlockSpec((tm, ffn), lambda i: (i, 0))
    in_specs = [tile, row, ANY, ANY, ANY]
    out_shape = [jax.ShapeDtypeStruct((s, d), F32)] + [jax.ShapeDtypeStruct((s, ffn), BF16)] * 3
    out_specs = [tile, wide, wide, wide]
    args = [x, gain] + list(weights)
    if head is not None:
        in_specs += [row, tile]
        args += list(head)
        out_shape += [jax.ShapeDtypeStruct((1, d), F32)] * 2
        out_specs += [row, row]
    return _grid_call(
        body,
        carries,
        name="ffn_fwd_loss" if head is not None else "ffn_fwd",
        grid=(s // tm,),
        in_specs=in_specs,
        out_specs=out_specs,
        out_shape=out_shape,
        scratch_shapes=[pltpu.VMEM((ffn, d), BF16)] * 3 + [pltpu.SemaphoreType.DMA((3 * NDEV * LOAD_PIECES,))],
        args=args,
    )


def _ffn_bwd(dh, x, a, b, gain, weights, ffn, name, carries=()):
    s, d = x.shape
    tm = min(512, s)
    halves = 2
    fh = ffn // halves

    def body(dh_ref, x_ref, a_ref, b_ref, g_ref, b1, b3, b2, dx_ref, da_ref, db_ref, n_ref, dg_ref, w1s, w3s, w2s, sems):
        i, j = pl.program_id(0), pl.program_id(1)

        @pl.when((i == 0) & (j == 0))
        def _():
            for cp in _load_weights(((b1, w1s), (b3, w3s), (b2, w2s)), sems):
                cp.wait()
            dg_ref[...] = jnp.zeros_like(dg_ref)

        @pl.when(j == 0)
        def _():
            xv = x_ref[...]
            r = lax.rsqrt(jnp.mean(xv * xv, axis=-1, keepdims=True) + EPS)
            n_ref[...] = (xv * r * g_ref[...]).astype(BF16)
            dx_ref[...] = jnp.zeros_like(dx_ref)

        dob = (0.5 * dh_ref[...]).astype(BF16)
        chunks = _chunks(fh)

        def dhm_of(k):
            at, width = chunks[k]
            return _dot(dob, w2s[pl.ds(pl.multiple_of(j * fh + at, GROUP), width), :], NT)

        ahead = dhm_of(0)
        for k, (at, width) in enumerate(chunks):
            cols = slice(at, at + width)
            dhm = ahead
            if k + 1 < len(chunks):
                ahead = dhm_of(k + 1)
            for top in range(0, tm, ROW_BAND):
                band = slice(top, top + ROW_BAND)
                av = a_ref[band, cols].astype(F32)
                bv = b_ref[band, cols].astype(F32)
                sg = _sigmoid(av)
                dv = dhm[band]
                da_ref[band, cols] = (dv * bv * (sg * (1.0 + av * (1.0 - sg)))).astype(BF16)
                db_ref[band, cols] = (dv * (av * sg)).astype(BF16)
        half = pl.ds(pl.multiple_of(j * fh, GROUP), fh)
        dx_ref[...] += _dot(da_ref[...], w1s[half, :], NN) + _dot(db_ref[...], w3s[half, :], NN)

        @pl.when(j == halves - 1)
        def _():
            xv = x_ref[...]
            g = g_ref[...]
            r = lax.rsqrt(jnp.mean(xv * xv, axis=-1, keepdims=True) + EPS)
            nh = xv * r
            total = dx_ref[...]
            dg_ref[...] += jnp.sum(total * nh, axis=0, keepdims=True)
            dnh = total * g
            dx_ref[...] = dh_ref[...] + r * (dnh - nh * jnp.mean(dnh * nh, axis=-1, keepdims=True))

    tile = pl.BlockSpec((tm, d), lambda i, j: (i, 0))
    row = pl.BlockSpec((1, d), lambda i, j: (0, 0))
    wide = pl.BlockSpec((tm, fh), lambda i, j: (i, j))
    return _grid_call(
        body,
        carries,
        name=name,
        grid=(s // tm, halves),
        in_specs=[tile, tile, wide, wide, row, ANY, ANY, ANY],
        out_specs=[tile, wide, wide, tile, row],
        out_shape=[
            jax.ShapeDtypeStruct((s, d), F32),
            jax.ShapeDtypeStruct((s, ffn), BF16),
            jax.ShapeDtypeStruct((s, ffn), BF16),
            jax.ShapeDtypeStruct((s, d), BF16),
            jax.ShapeDtypeStruct((1, d), F32),
        ],
        scratch_shapes=[pltpu.VMEM((ffn, d), BF16)] * 3 + [pltpu.SemaphoreType.DMA((3 * NDEV * LOAD_PIECES,))],
        args=[dh, x, a, b, gain] + list(weights),
    )


SWAP_PIECES = 2


def _wgrad(lhs, rhs, scale, name, carries=()):
    s, m = lhs.shape
    n = rhs.shape[1]
    rs = m // NDEV
    tk = min(512, s)
    steps = s // tk
    pieces = [(j, at, size) for j in range(NCHIP) for at, size in _pieces(rs, SWAP_PIECES)]

    def body(l_ref, r_ref, o_ref, acc, mine, theirs, send_sems, recv_sems):
        k = pl.program_id(0)

        @pl.when(k == 0)
        def _():
            acc[...] = jnp.zeros_like(acc)

        acc[...] += _dot(l_ref[...], r_ref[...].astype(BF16), TN)

        @pl.when(k == steps - 1)
        def _():
            for p in range(NDEV):
                mine[p % 2, p // 2] = (acc[p * rs:(p + 1) * rs, :] * scale).astype(BF16)
            c = lax.axis_index("c")
            copies = [
                _remote(mine.at[1 - c, j, pl.ds(at, size), :], theirs.at[j, pl.ds(at, size), :],
                        send_sems.at[q], recv_sems.at[q], _peer(1))
                for q, (j, at, size) in enumerate(pieces)
            ]
            for cp in copies:
                cp.start()
            for cp in copies:
                cp.wait_recv()
            o_ref[...] = (mine[c].astype(F32) + theirs[...].astype(F32)).astype(BF16)
            for cp in copies:
                cp.wait_send()

    (out,), carried = _grid_call(
        body,
        carries,
        name=name,
        grid=(steps,),
        in_specs=[pl.BlockSpec((tk, m), lambda k: (k, 0)), pl.BlockSpec((tk, n), lambda k: (k, 0))],
        out_specs=[pl.BlockSpec((NCHIP, rs, n), lambda k: (0, 0, 0))],
        out_shape=[jax.ShapeDtypeStruct((NCHIP, rs, n), BF16)],
        scratch_shapes=[
            pltpu.VMEM((m, n), F32), pltpu.VMEM((2, NCHIP, rs, n), BF16), pltpu.VMEM((NCHIP, rs, n), BF16),
            pltpu.SemaphoreType.DMA((len(pieces),)), pltpu.SemaphoreType.DMA((len(pieces),)),
        ],
        args=[lhs, rhs],
    )
    return out, carried


def _mix_constants(s):
    c = GROUP
    lg = np.log1p(-np.exp2(-5.0 - np.arange(RET_HEADS, dtype=np.float32))).astype(np.float32)
    pos = np.arange(c, dtype=np.float32)
    rel = pos[:, None] - pos[None, :]
    decay = np.where(rel[None] >= 0, np.exp(lg[:, None, None] * np.maximum(rel, 0.0)[None]), 0.0).astype(np.float32)
    ktail = np.exp(lg[:, None] * (c - 1 - pos)[None, :]).astype(np.float32)
    qhead = np.exp(lg[:, None] * (pos + 1.0)[None, :]).astype(np.float32)
    chunk_decay = [float(v) for v in np.exp(lg * np.float32(c)).astype(np.float32)]
    ones = np.ones((1, 1, c), np.float32)
    inv_freq = (1.0 / (np.float32(ROPE_BASE) ** (np.arange(0, c, 2, dtype=np.float32) / np.float32(c)))).astype(np.float32)
    ang = (np.arange(s, dtype=np.float32)[:, None] * inv_freq[None, :]).astype(np.float32)
    cos, sin = np.cos(ang).astype(np.float32), np.sin(ang).astype(np.float32)
    return dict(
        decay=jnp.asarray(decay),
        ktail=jnp.asarray(ktail[:, :, None] * ones),
        qhead=jnp.asarray(qhead[:, :, None] * ones),
        chunk_decay=chunk_decay,
        cos=jnp.asarray(np.concatenate([cos, cos], axis=-1)),
        sin=jnp.asarray(np.concatenate([-sin, sin], axis=-1)),
    )


def _rope(t, cos, sin):
    return t * cos + pltpu.roll(t, GROUP // 2, axis=1) * sin


def _rope_bwd(dt, cos, sin):
    return dt * cos + pltpu.roll(dt * sin, GROUP // 2, axis=1)


def _window_sums(ext, w, forward):
    rows = ext.shape[0]
    acc, k = ext, 1
    while k < w:
        acc = acc + pltpu.roll(acc, k if forward else rows - k, axis=0)
        k *= 2
    return acc


def _pool_counts(tile, tm, w):
    t = lax.broadcasted_iota(jnp.int32, (tm, 1), 0) + tile * tm
    return jnp.minimum(t + 1, w).astype(F32)


def _mix_fwd(h1, gain, weights, pool_w, pool_scale, ret_gain, consts, carries=()):
    s, d = h1.shape
    pwid = N_POOL_GROUPS * GROUP
    rwid = RET_HEADS * GROUP
    inw = pwid + 4 * rwid
    tm = min(256, s)
    nck = tm // GROUP
    cd = consts["chunk_decay"]

    def body(h_ref, g_ref, bin_, bout, pw_ref, ps_ref, rg_ref, cos_ref, sin_ref, dec_ref, kt_ref, qh_ref,
             h2_ref, proj_ref, o_ref, rs_ref, wins, wouts, state, carry, mbuf, sems):
        i = pl.program_id(0)

        @pl.when(i == 0)
        def _():
            for cp in _load_weights(((bin_, wins), (bout, wouts)), sems):
                cp.wait()
            state[...] = jnp.zeros_like(state)
            carry[...] = jnp.zeros_like(carry)

        hv = h_ref[...]
        r = lax.rsqrt(jnp.mean(hv * hv, axis=-1, keepdims=True) + EPS)
        u = (hv * r * g_ref[...]).astype(BF16)
        proj_ref[...] = _dot(u, wins[...], NT)

        ext = jnp.concatenate([carry[...], proj_ref[:, 0:pwid]], axis=0)
        carry[...] = proj_ref[tm - MAX_WINDOW:tm, 0:pwid]
        for gi, w in enumerate(POOL_WINDOWS):
            cols = slice(gi * GROUP, (gi + 1) * GROUP)
            xg = ext[:, cols]
            ws = _window_sums(xg, w, True)[MAX_WINDOW:, :]
            pooled = ws / _pool_counts(i, tm, w) - xg[MAX_WINDOW:, :]
            z = _dot(pooled.astype(BF16), pw_ref[gi].astype(BF16), NN)
            mbuf[:, cols] = (z * ps_ref[:, cols]).astype(BF16)

        cos, sin = cos_ref[...], sin_ref[...]
        for h in range(RET_HEADS):
            cq = slice(pwid + h * GROUP, pwid + (h + 1) * GROUP)
            ck = slice(pwid + rwid + h * GROUP, pwid + rwid + (h + 1) * GROUP)
            cv = slice(pwid + 2 * rwid + h * GROUP, pwid + 2 * rwid + (h + 1) * GROUP)
            cg = slice(pwid + 3 * rwid + h * GROUP, pwid + 3 * rwid + (h + 1) * GROUP)
            ch = slice(h * GROUP, (h + 1) * GROUP)
            qr = _rope(proj_ref[:, cq], cos, sin)
            kr = _rope(proj_ref[:, ck], cos, sin) * (GROUP ** -0.5)
            vb = proj_ref[:, cv].astype(BF16)
            for n in range(nck):
                rows = slice(n * GROUP, (n + 1) * GROUP)
                qc, kc, vc = qr[rows], kr[rows], vb[rows]
                rb = state[h]
                rs_ref[n, h] = rb
                p = (_dot(qc.astype(BF16), kc.astype(BF16), NT) * dec_ref[h]).astype(BF16)
                o = _dot(p, vc, NN) + _dot((qc * qh_ref[h]).astype(BF16), rb.astype(BF16), NN)
                state[h] = cd[h] * rb + _dot((kc * kt_ref[h]).astype(BF16), vc, TN)
                o_ref[rows, ch] = o
                on = o * lax.rsqrt(jnp.mean(o * o, axis=-1, keepdims=True) + EPS)
                gv = proj_ref[rows, cg]
                mbuf[rows, pwid + h * GROUP:pwid + (h + 1) * GROUP] = (
                    gv * _sigmoid(gv) * (on * rg_ref[:, ch])
                ).astype(BF16)
        h2_ref[...] = hv + _dot(mbuf[...], wouts[...], NN)

    tile = pl.BlockSpec((tm, d), lambda i: (i, 0))
    full = lambda shape: pl.BlockSpec(shape, lambda i: (0,) * len(shape))
    return _grid_call(
        body,
        carries,
        name="mix_fwd",
        grid=(s // tm,),
        in_specs=[
            tile, full((1, d)), ANY, ANY,
            full((N_POOL_GROUPS, GROUP, GROUP)), full((1, pwid)), full((1, rwid)),
            pl.BlockSpec((tm, GROUP), lambda i: (i, 0)), pl.BlockSpec((tm, GROUP), lambda i: (i, 0)),
            full((RET_HEADS, GROUP, GROUP)), full((RET_HEADS, GROUP, GROUP)), full((RET_HEADS, GROUP, GROUP)),
        ],
        out_specs=[
            tile,
            pl.BlockSpec((tm, inw), lambda i: (i, 0)),
            pl.BlockSpec((tm, rwid), lambda i: (i, 0)),
            pl.BlockSpec((nck, RET_HEADS, GROUP, GROUP), lambda i: (i, 0, 0, 0)),
        ],
        out_shape=[
            jax.ShapeDtypeStruct((s, d), F32),
            jax.ShapeDtypeStruct((s, inw), F32),
            jax.ShapeDtypeStruct((s, rwid), F32),
            jax.ShapeDtypeStruct((s // GROUP, RET_HEADS, GROUP, GROUP), F32),
        ],
        scratch_shapes=[
            pltpu.VMEM((inw, d), BF16), pltpu.VMEM((d, d), BF16),
            pltpu.VMEM((RET_HEADS, GROUP, GROUP), F32), pltpu.VMEM((MAX_WINDOW, pwid), F32),
            pltpu.VMEM((tm, d), BF16), pltpu.SemaphoreType.DMA((2 * NDEV * LOAD_PIECES,)),
        ],
        args=[h1, gain, weights[0], weights[1], pool_w, pool_scale, ret_gain,
              consts["cos"], consts["sin"], consts["decay"], consts["ktail"], consts["qhead"]],
    )


def _mix_bwd(dh2, h1, proj, o_saved, rsave, gain, weights, pool_w, pool_scale, ret_gain, consts, carries=()):
    s, d = h1.shape
    pwid = N_POOL_GROUPS * GROUP
    rwid = RET_HEADS * GROUP
    inw = pwid + 4 * rwid
    tm = min(256, s)
    nck = tm // GROUP
    nt = s // tm
    cd = consts["chunk_decay"]
    halo_per_tile = tm // MAX_WINDOW

    def body(dh2_ref, h_ref, proj_ref, halo_ref, o_ref, rs_ref, g_ref, bin_, bout, pw_ref, ps_ref, rg_ref,
             cos_ref, sin_ref, dec_ref, kt_ref, qh_ref,
             dh1_ref, dproj_ref, u_ref, m_ref, dpw_ref, dps_ref, drg_ref, dg_ref,
             wins, wouts, dstate, carry, dm, dpj, sems):
        i = pl.program_id(0)
        tile = nt - 1 - i

        @pl.when(i == 0)
        def _():
            for cp in _load_weights(((bin_, wins), (bout, wouts)), sems):
                cp.wait()
            dstate[...] = jnp.zeros_like(dstate)
            carry[...] = jnp.zeros_like(carry)
            for ref in (dpw_ref, dps_ref, drg_ref, dg_ref):
                ref[...] = jnp.zeros_like(ref)

        dh2v = dh2_ref[...]
        dm[...] = _dot(dh2v.astype(BF16), wouts[...], NT)
        hv = h_ref[...]
        g = g_ref[...]
        r = lax.rsqrt(jnp.mean(hv * hv, axis=-1, keepdims=True) + EPS)
        uh = hv * r
        u_ref[...] = (uh * g).astype(BF16)

        halo = jnp.where(tile == 0, 0.0, halo_ref[...])
        ext = jnp.concatenate([halo, proj_ref[:, 0:pwid]], axis=0)
        next_dpn = carry[...]
        for gi, w in enumerate(POOL_WINDOWS):
            cols = slice(gi * GROUP, (gi + 1) * GROUP)
            xg = ext[:, cols]
            cnt = _pool_counts(tile, tm, w)
            pooled = (_window_sums(xg, w, True)[MAX_WINDOW:, :] / cnt - xg[MAX_WINDOW:, :]).astype(BF16)
            pwb = pw_ref[gi].astype(BF16)
            z = _dot(pooled, pwb, NN)
            scale = ps_ref[:, cols]
            m_ref[:, cols] = (z * scale).astype(BF16)
            da = dm[:, cols]
            dps_ref[:, cols] += jnp.sum(da * z, axis=0, keepdims=True)
            dz = (da * scale).astype(BF16)
            dpw_ref[gi] += _dot(pooled, dz, TN)
            dpl = _dot(dz, pwb, NT)
            dpn = dpl / cnt
            ext2 = jnp.concatenate([dpn, next_dpn[:, cols]], axis=0)
            dpj[:, cols] = (_window_sums(ext2, w, False)[0:tm, :] - dpl).astype(BF16)
            carry[:, cols] = dpn[0:MAX_WINDOW, :]

        cos, sin = cos_ref[...], sin_ref[...]
        for h in range(RET_HEADS):
            cq = slice(pwid + h * GROUP, pwid + (h + 1) * GROUP)
            ck = slice(pwid + rwid + h * GROUP, pwid + rwid + (h + 1) * GROUP)
            cv = slice(pwid + 2 * rwid + h * GROUP, pwid + 2 * rwid + (h + 1) * GROUP)
            cg = slice(pwid + 3 * rwid + h * GROUP, pwid + 3 * rwid + (h + 1) * GROUP)
            ch = slice(h * GROUP, (h + 1) * GROUP)
            qr = _rope(proj_ref[:, cq], cos, sin)
            kr = _rope(proj_ref[:, ck], cos, sin) * (GROUP ** -0.5)
            vb = proj_ref[:, cv].astype(BF16)
            gv = proj_ref[:, cg]
            ov = o_ref[:, ch]
            ro = lax.rsqrt(jnp.mean(ov * ov, axis=-1, keepdims=True) + EPS)
            on = ov * ro
            rg = rg_ref[:, ch]
            db = dm[:, pwid + h * GROUP:pwid + (h + 1) * GROUP]
            sg = _sigmoid(gv)
            sl = gv * sg
            m_ref[:, pwid + h * GROUP:pwid + (h + 1) * GROUP] = (sl * (on * rg)).astype(BF16)
            dpj[:, cg] = (db * (on * rg) * (sg * (1.0 + gv * (1.0 - sg)))).astype(BF16)
            drg_ref[:, ch] += jnp.sum(db * sl * on, axis=0, keepdims=True)
            don = db * sl * rg
            do = (ro * (don - on * jnp.mean(don * on, axis=-1, keepdims=True))).astype(BF16)
            for n in reversed(range(nck)):
                rows = slice(n * GROUP, (n + 1) * GROUP)
                qc, kc, vc, dob = qr[rows], kr[rows], vb[rows], do[rows]
                qcb, kcb = qc.astype(BF16), kc.astype(BF16)
                qh = (qc * qh_ref[h]).astype(BF16)
                kt = (kc * kt_ref[h]).astype(BF16)
                rn = rs_ref[n, h].astype(BF16)
                dnext = dstate[h]
                dnb = dnext.astype(BF16)
                dec = dec_ref[h]
                p = (_dot(qcb, kcb, NT) * dec).astype(BF16)
                ds = (_dot(dob, vc, NT) * dec).astype(BF16)
                dv = _dot(p, dob, TN) + _dot(kt, dnb, NN)
                dq = _dot(ds, kcb, NN) + _dot(dob, rn, NT) * qh_ref[h]
                dk = _dot(ds, qcb, TN) + _dot(vc, dnb, NT) * kt_ref[h]
                dstate[h] = cd[h] * dnext + _dot(qh, dob, TN)
                dpj[rows, cq] = _rope_bwd(dq, cos[rows], sin[rows]).astype(BF16)
                dpj[rows, ck] = _rope_bwd(dk * (GROUP ** -0.5), cos[rows], sin[rows]).astype(BF16)
                dpj[rows, cv] = dv.astype(BF16)

        dproj_ref[...] = dpj[...]
        du = _dot(dpj[...], wins[...], NN)
        dg_ref[...] += jnp.sum(du * uh, axis=0, keepdims=True)
        dn = du * g
        dh1_ref[...] = dh2v + r * (dn - uh * jnp.mean(dn * uh, axis=-1, keepdims=True))

    rev = lambda i: (nt - 1 - i, 0)
    tile = pl.BlockSpec((tm, d), rev)
    full = lambda shape: pl.BlockSpec(shape, lambda i: (0,) * len(shape))
    return _grid_call(
        body,
        carries,
        name="mix_bwd",
        grid=(nt,),
        in_specs=[
            tile, tile,
            pl.BlockSpec((tm, inw), rev),
            pl.BlockSpec((MAX_WINDOW, pwid), lambda i: (jnp.maximum((nt - 1 - i) * halo_per_tile - 1, 0), 0)),
            pl.BlockSpec((tm, rwid), rev),
            pl.BlockSpec((nck, RET_HEADS, GROUP, GROUP), lambda i: (nt - 1 - i, 0, 0, 0)),
            full((1, d)), ANY, ANY,
            full((N_POOL_GROUPS, GROUP, GROUP)), full((1, pwid)), full((1, rwid)),
            pl.BlockSpec((tm, GROUP), rev), pl.BlockSpec((tm, GROUP), rev),
            full((RET_HEADS, GROUP, GROUP)), full((RET_HEADS, GROUP, GROUP)), full((RET_HEADS, GROUP, GROUP)),
        ],
        out_specs=[
            tile, pl.BlockSpec((tm, inw), rev), tile, tile,
            full((N_POOL_GROUPS, GROUP, GROUP)), full((1, pwid)), full((1, rwid)), full((1, d)),
        ],
        out_shape=[
            jax.ShapeDtypeStruct((s, d), F32),
            jax.ShapeDtypeStruct((s, inw), BF16),
            jax.ShapeDtypeStruct((s, d), BF16),
            jax.ShapeDtypeStruct((s, d), BF16),
            jax.ShapeDtypeStruct((N_POOL_GROUPS, GROUP, GROUP), F32),
            jax.ShapeDtypeStruct((1, pwid), F32),
            jax.ShapeDtypeStruct((1, rwid), F32),
            jax.ShapeDtypeStruct((1, d), F32),
        ],
        scratch_shapes=[
            pltpu.VMEM((inw, d), BF16), pltpu.VMEM((d, d), BF16),
            pltpu.VMEM((RET_HEADS, GROUP, GROUP), F32), pltpu.VMEM((MAX_WINDOW, pwid), F32),
            pltpu.VMEM((tm, d), F32), pltpu.VMEM((tm, inw), BF16), pltpu.SemaphoreType.DMA((2 * NDEV * LOAD_PIECES,)),
        ],
        args=[dh2, h1, proj, proj, o_saved, rsave, gain, weights[0], weights[1], pool_w, pool_scale, ret_gain,
              consts["cos"], consts["sin"], consts["decay"], consts["ktail"], consts["qhead"]],
    )


def _adam(w, g, m, v):
    m = ADAM_B1 * m + (1.0 - ADAM_B1) * g
    v = ADAM_B2 * v + (1.0 - ADAM_B2) * jnp.square(g)
    m_hat = m / (1.0 - ADAM_B1 ** ADAM_STEP)
    v_hat = v / (1.0 - ADAM_B2 ** ADAM_STEP)
    delta = -ADAM_LR * (m_hat / (jnp.sqrt(v_hat) + ADAM_EPS) + ADAM_WD * w)
    return delta, m, v


def _adamw_big(w, parts, m, v, name):
    rows, d = w.shape
    tr = _row_tile(rows, 512)

    def body(w_ref, p_ref, m_ref, v_ref, g_ref, d_ref, nm_ref, nv_ref):
        g = p_ref[0].astype(F32)
        for q in range(1, NCHIP):
            g = g + p_ref[q].astype(F32)
        g_ref[...] = g
        d_ref[...], nm_ref[...], nv_ref[...] = _adam(w_ref[...], g, m_ref[...], v_ref[...])

    spec = pl.BlockSpec((tr, d), lambda i: (i, 0))
    return _call(
        body,
        name=name,
        grid=(rows // tr,),
        in_specs=[spec, pl.BlockSpec((NCHIP, tr, d), lambda i: (0, i, 0)), spec, spec],
        out_specs=[spec] * 4,
        out_shape=[jax.ShapeDtypeStruct((rows, d), F32)] * 4,
        compiler_params=_seq(1),
    )(w, parts, m, v)


def _adamw_small(stats_all, pw_all, ws, ms, vs, d, pwid):
    nsmall = len(ws)

    def body(*refs):
        st_ref, pwa_ref = refs[0], refs[1]
        w_refs = refs[2:2 + nsmall]
        m_refs = refs[2 + nsmall:2 + 2 * nsmall]
        v_refs = refs[2 + 2 * nsmall:2 + 3 * nsmall]
        outs = refs[2 + 3 * nsmall:]
        st = st_ref[0]
        pwg = pwa_ref[0]
        for q in range(1, NDEV):
            st = st + st_ref[q]
            pwg = pwg + pwa_ref[q]
        grads = [st[0:1, :], st[1:2, :], st[2:3, :], st[3:4, :], st[4:5, 0:pwid], st[4:5, pwid:2 * pwid], pwg]
        outs[0][...] = jnp.zeros((1, GROUP), F32) + jnp.sum(st[5:6, :])
        for j in range(nsmall):
            delta, nm, nv = _adam(w_refs[j][...], grads[j], m_refs[j][...], v_refs[j][...])
            outs[1 + 4 * j][...] = grads[j]
            outs[2 + 4 * j][...] = delta
            outs[3 + 4 * j][...] = nm
            outs[4 + 4 * j][...] = nv

    out_shape = [jax.ShapeDtypeStruct((1, GROUP), F32)]
    for w in ws:
        out_shape += [jax.ShapeDtypeStruct(w.shape, F32)] * 4
    return _call(body, name="adamw_small", out_shape=out_shape, compiler_params=_params())(
        stats_all, pw_all, *ws, *ms, *vs
    )


def kernel(x, ffn1_norm, ffn1_w1, ffn1_w3, ffn1_w2, mix_norm, w_in, pool_w, pool_scale, ret_norm, w_out, ffn2_norm, ffn2_w1, ffn2_w3, ffn2_w2, final_norm, loss_target, m_ffn1_norm, m_ffn1_w1, m_ffn1_w3, m_ffn1_w2, m_mix_norm, m_w_in, m_pool_w, m_pool_scale, m_ret_norm, m_w_out, m_ffn2_norm, m_ffn2_w1, m_ffn2_w3, m_ffn2_w2, m_final_norm, v_ffn1_norm, v_ffn1_w1, v_ffn1_w3, v_ffn1_w2, v_mix_norm, v_w_in, v_pool_w, v_pool_scale, v_ret_norm, v_w_out, v_ffn2_norm, v_ffn2_w1, v_ffn2_w3, v_ffn2_w2, v_final_norm):
    s, d = x.shape[1], x.shape[2]
    ffn = ffn1_w1.shape[2] * NDEV
    pwid = pool_scale.shape[1]
    xs, tgt = x[0], loss_target[0]
    consts = _mix_constants(s)
    pw3 = pool_w[0]
    fnorm = final_norm.reshape(1, d)

    rows_of = lambda w, transposed: (w[0].T if transposed else w[0]).astype(BF16)
    send_f1 = [rows_of(ffn1_w1, True), rows_of(ffn1_w3, True), rows_of(ffn1_w2, False)]
    send_mix = [rows_of(w_in, True), rows_of(w_out, False)]
    send_f2 = [rows_of(ffn2_w1, True), rows_of(ffn2_w3, True), rows_of(ffn2_w2, False)]

    (w_f1,) = _comm_call([_Gather(send_f1)], "gather_ffn1")
    (h1, a1, b1, hm1), (more,) = _ffn_fwd(xs, ffn1_norm, w_f1, ffn, carries=[_Gather(send_mix + send_f2[:1])])
    w_mix = more[:2]
    (h2, proj, o_saved, rsave), (rest,) = _mix_fwd(
        h1, mix_norm, w_mix, pw3, pool_scale, ret_norm, consts, carries=[_Gather(send_f2[1:])]
    )
    w_f2 = more[2:] + rest
    (dh3, a2, b2, hm2, dgf, loss_cols), _ = _ffn_fwd(h2, ffn2_norm, w_f2, ffn, head=(fnorm, tgt))

    (dh2, da2, db2, n2, dg2), _ = _ffn_bwd(dh3, h2, a2, b2, ffn2_norm, w_f2, ffn, "ffn2_bwd")
    sum_f2w1, _ = _wgrad(da2, n2, 1.0, "ffn2_w1_grad")
    sum_f2w3, ((parts_f2w1,),) = _wgrad(db2, n2, 1.0, "ffn2_w3_grad", carries=[_ChipScatter([sum_f2w1])])
    sum_f2w2, ((parts_f2w3,),) = _wgrad(hm2, dh3, 0.5, "ffn2_w2_grad", carries=[_ChipScatter([sum_f2w3])])

    (dh1, dproj, u, mm, dpw, dps, drg, dgm), ((parts_f2w2,),) = _mix_bwd(
        dh2, h1, proj, o_saved, rsave, mix_norm, w_mix, pw3, pool_scale, ret_norm, consts,
        carries=[_ChipScatter([sum_f2w2])],
    )
    (dx, da1, db1, n1, dg1), _ = _ffn_bwd(dh1, xs, a1, b1, ffn1_norm, w_f1, ffn, "ffn1_bwd")
    stats = jnp.concatenate(
        [dg1, dgm, dg2, dgf, jnp.concatenate([dps, drg], axis=1), loss_cols, jnp.zeros((2, d), F32)], axis=0
    )
    small = _GatherDirect([stats, dpw.reshape(N_POOL_GROUPS * GROUP, GROUP)])
    sum_f1w2, ((stats_all, pw_all),) = _wgrad(hm1, dh1, 0.5, "ffn1_w2_grad", carries=[small])
    sum_f1w1, ((parts_f1w2,),) = _wgrad(da1, n1, 1.0, "ffn1_w1_grad", carries=[_ChipScatter([sum_f1w2])])
    sum_f1w3, ((parts_f1w1,),) = _wgrad(db1, n1, 1.0, "ffn1_w3_grad", carries=[_ChipScatter([sum_f1w1])])
    sum_in, ((parts_f1w3,),) = _wgrad(dproj, u, 1.0, "w_in_grad", carries=[_ChipScatter([sum_f1w3])])
    sum_out, ((parts_in,),) = _wgrad(mm, dh2, 1.0, "w_out_grad", carries=[_ChipScatter([sum_in])])
    ((parts_out,),) = _comm_call([_ChipScatter([sum_out])], "scatter_last")

    big = (
        (ffn1_w1, m_ffn1_w1, v_ffn1_w1, parts_f1w1, True),
        (ffn1_w3, m_ffn1_w3, v_ffn1_w3, parts_f1w3, True),
        (ffn1_w2, m_ffn1_w2, v_ffn1_w2, parts_f1w2, False),
        (w_in, m_w_in, v_w_in, parts_in, True),
        (w_out, m_w_out, v_w_out, parts_out, False),
        (ffn2_w1, m_ffn2_w1, v_ffn2_w1, parts_f2w1, True),
        (ffn2_w3, m_ffn2_w3, v_ffn2_w3, parts_f2w3, True),
        (ffn2_w2, m_ffn2_w2, v_ffn2_w2, parts_f2w2, False),
    )
    big_out = []
    for j, (w, m, v, parts, t) in enumerate(big):
        view = (lambda a: a[0].T) if t else (lambda a: a[0])
        back = (lambda a: a.T[None]) if t else (lambda a: a[None])
        big_out.append([back(a) for a in _adamw_big(view(w), parts, view(m), view(v), "adamw_%d" % j)])

    small_w = (ffn1_norm, mix_norm, ffn2_norm, fnorm, pool_scale, ret_norm, pw3.reshape(-1, GROUP))
    small_m = (m_ffn1_norm, m_mix_norm, m_ffn2_norm, m_final_norm.reshape(1, d), m_pool_scale, m_ret_norm, m_pool_w.reshape(-1, GROUP))
    small_v = (v_ffn1_norm, v_mix_norm, v_ffn2_norm, v_final_norm.reshape(1, d), v_pool_scale, v_ret_norm, v_pool_w.reshape(-1, GROUP))
    res = _adamw_small(stats_all, pw_all, small_w, small_m, small_v, d, pwid)
    loss = res[0][0, 0]
    small_out = [list(res[1 + 4 * j:5 + 4 * j]) for j in range(len(small_w))]
    small_out[3] = [a.reshape(d) for a in small_out[3]]
    small_out[6] = [a.reshape(pool_w.shape) for a in small_out[6]]

    order = [small_out[0], big_out[0], big_out[1], big_out[2], small_out[1], big_out[3], small_out[6], small_out[4],
             small_out[5], big_out[4], small_out[2], big_out[5], big_out[6], big_out[7], small_out[3]]
    result = [loss, dx[None]]
    for kind in range(4):
        result += [t[kind] for t in order]
    return tuple(result)
```

```python
import functools

import numpy as np
import jax
import jax.numpy as jnp
from jax import lax
from jax.experimental import pallas as pl
from jax.experimental.pallas import tpu as pltpu

F32 = jnp.float32
BF16 = jnp.bfloat16

NDEV = 8
NCHIP = 4
EPS = 1e-6
N_POOL_GROUPS = 4
POOL_WINDOWS = (2, 4, 8, 16)
MAX_WINDOW = 16
GROUP = 128
RET_HEADS = 4
ROPE_BASE = 10000.0
ADAM_LR = 0.001
ADAM_B1 = 0.9
ADAM_B2 = 0.999
ADAM_EPS = 1e-08
ADAM_WD = 0.01
ADAM_STEP = 10

VMEM_LIMIT = 56 * 1024 * 1024
FFN_CHUNK = 256
LOAD_PIECES = 4
ROW_BAND = 32

NT = (((1,), (1,)), ((), ()))
NN = (((1,), (0,)), ((), ()))
TN = (((0,), (0,)), ((), ()))

ANY = pl.BlockSpec(memory_space=pl.ANY)


def _dot(a, b, dims):
    return lax.dot_general(a, b, dims, preferred_element_type=F32)


def _call(body, **kw):
    return pl.pallas_call(body, **kw)


def _params(**kw):
    return pltpu.CompilerParams(vmem_limit_bytes=VMEM_LIMIT, **kw)


def _seq(n):
    return _params(dimension_semantics=("arbitrary",) * n)


def _peer(k):
    x, y, c = lax.axis_index("x"), lax.axis_index("y"), lax.axis_index("c")
    return (1 - x if k & 4 else x, 1 - y if k & 2 else y, 1 - c if k & 1 else c)


def _flat(pos):
    return 4 * pos[0] + 2 * pos[1] + pos[2]


def _chip(pos):
    return 2 * pos[0] + pos[1]


def _row_tile(rows, cap):
    return max(t for t in range(16, min(rows, cap) + 1, 16) if rows % t == 0)


def _pieces(rows, n):
    tiles = rows // 16
    cuts = [16 * (tiles * q // n) for q in range(n + 1)]
    return [(a, b - a) for a, b in zip(cuts[:-1], cuts[1:])]


def _load_weights(parts, sems):
    copies = []
    for buf, dst in parts:
        rows = buf.shape[1]
        for p in range(NDEV):
            for at, size in _pieces(rows, LOAD_PIECES):
                cp = pltpu.make_async_copy(
                    buf.at[p, pl.ds(at, size), :], dst.at[pl.ds(p * rows + at, size), :], sems.at[len(copies)]
                )
                cp.start()
                copies.append(cp)
    return copies


def _sigmoid(a):
    return 1.0 / (1.0 + jnp.exp(-a))


def _remote(src, dst, send_sem, recv_sem, to):
    return pltpu.make_async_remote_copy(
        src_ref=src, dst_ref=dst, send_sem=send_sem, recv_sem=recv_sem, device_id=to, device_id_type=pl.DeviceIdType.MESH
    )


class _Gather:
    X, Y, FAR = 4, 2, 6
    COPIES = 8

    def __init__(self, shards):
        n = len(shards)
        self.operands = list(shards)
        self.out_shape = [jax.ShapeDtypeStruct((NDEV,) + a.shape, a.dtype) for a in shards]
        self.sems = [
            pltpu.SemaphoreType.DMA((self.COPIES * n,)), pltpu.SemaphoreType.DMA((self.COPIES * n,)),
            pltpu.SemaphoreType.DMA((n,)),
        ]
        self.stages = [self.begin, self.relay, self.relay_far, self.end]

    def _copy(self, t, k, block, to, ins, outs, sems, own=False, half=None):
        rows = outs[t].shape[1]
        part = pl.ds(0, rows) if half is None else pl.ds(half * (rows // 2), rows // 2)
        dst = outs[t].at[_flat(block), part, :]
        at = self.COPIES * t + k
        return _remote(ins[t] if own else dst, dst, sems[0].at[at], sems[1].at[at], to)

    def _local(self, t, ins, outs, sems):
        return pltpu.make_async_copy(ins[t], outs[t].at[_flat(_peer(0))], sems[2].at[t])

    def begin(self, ins, outs, sems):
        me = _peer(0)
        for t in range(len(ins)):
            self._local(t, ins, outs, sems).start()
            for k, code in enumerate((1, self.X, self.Y)):
                self._copy(t, k, me, _peer(code), ins, outs, sems, own=True).start()

    def relay(self, ins, outs, sems):
        me, sibling = _peer(0), _peer(1)
        for t in range(len(ins)):
            self._copy(t, 1, _peer(self.X), me, ins, outs, sems).wait_recv()
            self._copy(t, 3, _peer(self.X), _peer(self.Y), ins, outs, sems, half=0).start()
            self._copy(t, 5, _peer(self.X), sibling, ins, outs, sems).start()
            self._copy(t, 2, _peer(self.Y), me, ins, outs, sems).wait_recv()
            self._copy(t, 4, _peer(self.Y), _peer(self.X), ins, outs, sems, half=1).start()
            self._copy(t, 6, _peer(self.Y), sibling, ins, outs, sems).start()

    def relay_far(self, ins, outs, sems):
        me, sibling = _peer(0), _peer(1)
        for t in range(len(ins)):
            self._copy(t, 3, _peer(self.FAR), me, ins, outs, sems, half=0).wait_recv()
            self._copy(t, 4, _peer(self.FAR), me, ins, outs, sems, half=1).wait_recv()
            self._copy(t, 7, _peer(self.FAR), sibling, ins, outs, sems).start()

    def end(self, ins, outs, sems):
        me = _peer(0)
        for t in range(len(ins)):
            self._copy(t, 0, _peer(1), me, ins, outs, sems).wait_recv()
            for k, code in ((5, self.X), (6, self.Y), (7, self.FAR)):
                self._copy(t, k, _peer(code ^ 1), me, ins, outs, sems).wait_recv()
            for k in range(self.COPIES):
                self._copy(t, k, me, me, ins, outs, sems, half=0 if k == 3 else 1 if k == 4 else None).wait_send()
            self._local(t, ins, outs, sems).wait()


class _GatherDirect:
    def __init__(self, arrays):
        n = len(arrays)
        self.operands = list(arrays)
        self.out_shape = [jax.ShapeDtypeStruct((NDEV,) + a.shape, a.dtype) for a in arrays]
        self.sems = [pltpu.SemaphoreType.DMA((7 * n,)), pltpu.SemaphoreType.DMA((7 * n,)), pltpu.SemaphoreType.DMA((n,))]
        self.stages = [self.begin, self.end]

    def begin(self, ins, outs, sems):
        mine = _flat(_peer(0))
        for t in range(len(ins)):
            pltpu.make_async_copy(ins[t], outs[t].at[mine], sems[2].at[t]).start()
            for k in range(1, NDEV):
                _remote(ins[t], outs[t].at[mine], sems[0].at[7 * t + k - 1], sems[1].at[7 * t + k - 1], _peer(k)).start()

    def end(self, ins, outs, sems):
        mine = _flat(_peer(0))
        for t in range(len(ins)):
            for k in range(1, NDEV):
                cp = _remote(ins[t], outs[t].at[_flat(_peer(k))], sems[0].at[7 * t + k - 1], sems[1].at[7 * t + k - 1], _peer(k))
                cp.wait_recv()
                cp.wait_send()
            pltpu.make_async_copy(ins[t], outs[t].at[mine], sems[2].at[t]).wait()


class _ChipScatter:
    pieces = 2

    def __init__(self, sums):
        n = len(sums) * NCHIP * self.pieces
        self.operands = list(sums)
        self.out_shape = [jax.ShapeDtypeStruct(a.shape, a.dtype) for a in sums]
        self.sems = [pltpu.SemaphoreType.DMA((n,)), pltpu.SemaphoreType.DMA((n,))]
        self.stages = [self.begin, self.end]

    def _copies(self, ins, outs, sems, arriving):
        mine = _chip(_peer(0))
        copies = []
        for t in range(len(ins)):
            rows = ins[t].shape[1] // self.pieces
            for k in (0, 4, 2, 6):
                other = _chip(_peer(k))
                for q in range(self.pieces):
                    part = pl.ds(q * rows, rows)
                    at = len(copies)
                    if k == 0:
                        cp = pltpu.make_async_copy(ins[t].at[mine, part, :], outs[t].at[mine, part, :], sems[0].at[at])
                    else:
                        landing = outs[t].at[other if arriving else mine, part, :]
                        cp = _remote(ins[t].at[other, part, :], landing, sems[0].at[at], sems[1].at[at], _peer(k))
                    copies.append(cp)
        return copies

    def begin(self, ins, outs, sems):
        for cp in self._copies(ins, outs, sems, False):
            cp.start()

    def end(self, ins, outs, sems):
        for at, cp in enumerate(self._copies(ins, outs, sems, True)):
            if at % (NCHIP * self.pieces) < self.pieces:
                cp.wait()
            else:
                cp.wait_recv()
                cp.wait_send()


def _split_refs(refs, counts):
    out, at = [], 0
    for n in counts:
        out.append(refs[at:at + n])
        at += n
    return out


def _comm_call(carries, name):
    nin = [len(c.operands) for c in carries]
    nout = [len(c.out_shape) for c in carries]
    nsem = [len(c.sems) for c in carries]

    def body(*refs):
        ins, outs, sems = _split_refs(refs, (sum(nin), sum(nout), sum(nsem)))
        parts = list(zip(carries, _split_refs(ins, nin), _split_refs(outs, nout), _split_refs(sems, nsem)))
        for depth in range(max(len(c.stages) for c in carries)):
            for c, i, o, s in parts:
                if depth < len(c.stages) - 1:
                    c.stages[depth](i, o, s)
        for c, i, o, s in parts:
            c.stages[-1](i, o, s)

    res = _call(
        body,
        name=name,
        out_shape=[sh for c in carries for sh in c.out_shape],
        in_specs=[ANY] * sum(nin),
        out_specs=[ANY] * sum(nout),
        scratch_shapes=[sm for c in carries for sm in c.sems],
        compiler_params=pltpu.CompilerParams(has_side_effects=True),
    )(*[a for c in carries for a in c.operands])
    return _split_refs(list(res), nout)


def _grid_call(body, carries, *, name, grid, in_specs, out_specs, out_shape, scratch_shapes, args):
    ni, no, ns = len(in_specs), len(out_specs), len(scratch_shapes)
    nin = [len(c.operands) for c in carries]
    nout = [len(c.out_shape) for c in carries]
    nsem = [len(c.sems) for c in carries]
    steps = int(np.prod(grid))

    def when_of(stage, count):
        first, last = (5 * steps) // 8 - 1, steps - 2
        return max(0, last if count <= 3 else first + (last - first) * (stage - 1) // (count - 3))

    def wrapped(*refs):
        ins, cins, outs, couts, scr, csems = _split_refs(refs, (ni, sum(nin), no, sum(nout), ns, sum(nsem)))
        if not carries:
            return body(*ins, *outs, *scr)
        parts = list(zip(carries, _split_refs(cins, nin), _split_refs(couts, nout), _split_refs(csems, nsem)))
        step = pl.program_id(0)
        for axis in range(1, len(grid)):
            step = step * grid[axis] + pl.program_id(axis)

        @pl.when(step == 0)
        def _():
            for c, i, o, s in parts:
                c.stages[0](i, o, s)

        body(*ins, *outs, *scr)

        for c, i, o, s in parts:
            for stage in range(1, len(c.stages) - 1):
                pl.when(step == when_of(stage, len(c.stages)))(functools.partial(c.stages[stage], i, o, s))

        @pl.when(step == steps - 1)
        def _():
            for c, i, o, s in parts:
                c.stages[-1](i, o, s)

    res = _call(
        wrapped,
        name=name,
        grid=tuple(grid),
        in_specs=list(in_specs) + [ANY] * sum(nin),
        out_specs=list(out_specs) + [ANY] * sum(nout),
        out_shape=list(out_shape) + [sh for c in carries for sh in c.out_shape],
        scratch_shapes=list(scratch_shapes) + [sm for c in carries for sm in c.sems],
        compiler_params=_seq(len(grid)),
    )(*args, *[a for c in carries for a in c.operands])
    res = list(res)
    return res[:no], _split_refs(res[no:], nout)


def _chunks(width):
    return [(at, min(FFN_CHUNK, width - at)) for at in range(0, width, FFN_CHUNK)]


def _ffn_fwd(x, gain, weights, ffn, head=None, carries=()):
    s, d = x.shape
    tm = min(512, s)

    def body(*refs):
        if head is None:
            x_ref, g_ref, b1, b3, b2, h_ref, a_ref, b_ref, hm_ref, w1s, w3s, w2s, sems = refs
        else:
            x_ref, g_ref, b1, b3, b2, gf_ref, t_ref, h_ref, a_ref, b_ref, hm_ref, dgf_ref, loss_ref, w1s, w3s, w2s, sems = refs
        i = pl.program_id(0)

        @pl.when(i == 0)
        def _():
            for cp in _load_weights(((b1, w1s), (b3, w3s), (b2, w2s)), sems):
                cp.wait()
            if head is not None:
                dgf_ref[...] = jnp.zeros_like(dgf_ref)
                loss_ref[...] = jnp.zeros_like(loss_ref)

        xv = x_ref[...]
        r = lax.rsqrt(jnp.mean(xv * xv, axis=-1, keepdims=True) + EPS)
        n = (xv * r * g_ref[...]).astype(BF16)
        acc = jnp.zeros((tm, d), F32)
        for at, width in _chunks(ffn):
            cols = slice(at, at + width)
            a = _dot(n, w1s[cols, :], NT)
            b = _dot(n, w3s[cols, :], NT)
            a_ref[:, cols] = a.astype(BF16)
            b_ref[:, cols] = b.astype(BF16)
            hm = (a * _sigmoid(a) * b).astype(BF16)
            hm_ref[:, cols] = hm
            acc = acc + _dot(hm, w2s[cols, :], NN)
        h = xv + 0.5 * acc
        if head is None:
            h_ref[...] = h
        else:
            rf = lax.rsqrt(jnp.mean(h * h, axis=-1, keepdims=True) + EPS)
            nh = h * rf
            gf = gf_ref[...]
            err = nh * gf - t_ref[...]
            loss_ref[...] += jnp.sum(err * err, axis=0, keepdims=True) * (0.5 / d)
            dy = err * (1.0 / d)
            dgf_ref[...] += jnp.sum(dy * nh, axis=0, keepdims=True)
            dn = dy * gf
            h_ref[...] = rf * (dn - nh * jnp.mean(dn * nh, axis=-1, keepdims=True))

    tile = pl.BlockSpec((tm, d), lambda i: (i, 0))
    row = pl.BlockSpec((1, d), lambda i: (0, 0))
    wide = pl.BlockSpec((tm, ffn), lambda i: (i, 0))
    in_specs = [tile, row, ANY, ANY, ANY]
    out_shape = [jax.ShapeDtypeStruct((s, d), F32)] + [jax.ShapeDtypeStruct((s, ffn), BF16)] * 3
    out_specs = [tile, wide, wide, wide]
    args = [x, gain] + list(weights)
    if head is not None:
        in_specs += [row, tile]
        args += list(head)
        out_shape += [jax.ShapeDtypeStruct((1, d), F32)] * 2
        out_specs += [row, row]
    return _grid_call(
        body,
        carries,
        name="ffn_fwd_loss" if head is not None else "ffn_fwd",
        grid=(s // tm,),
        in_specs=in_specs,
        out_specs=out_specs,
        out_shape=out_shape,
        scratch_shapes=[pltpu.VMEM((ffn, d), BF16)] * 3 + [pltpu.SemaphoreType.DMA((3 * NDEV * LOAD_PIECES,))],
        args=args,
    )


def _ffn_bwd(dh, x, a, b, gain, weights, ffn, name, carries=()):
    s, d = x.shape
    tm = min(512, s)
    halves = 2
    fh = ffn // halves

    def body(dh_ref, x_ref, a_ref, b_ref, g_ref, b1, b3, b2, dx_ref, da_ref, db_ref, n_ref, dg_ref, w1s, w3s, w2s, sems):
        i, j = pl.program_id(0), pl.program_id(1)

        @pl.when((i == 0) & (j == 0))
        def _():
            for cp in _load_weights(((b1, w1s), (b3, w3s), (b2, w2s)), sems):
                cp.wait()
            dg_ref[...] = jnp.zeros_like(dg_ref)

        @pl.when(j == 0)
        def _():
            xv = x_ref[...]
            r = lax.rsqrt(jnp.mean(xv * xv, axis=-1, keepdims=True) + EPS)
            n_ref[...] = (xv * r * g_ref[...]).astype(BF16)
            dx_ref[...] = jnp.zeros_like(dx_ref)

        dob = (0.5 * dh_ref[...]).astype(BF16)
        chunks = _chunks(fh)

        def dhm_of(k):
            at, width = chunks[k]
            return _dot(dob, w2s[pl.ds(pl.multiple_of(j * fh + at, GROUP), width), :], NT)

        ahead = dhm_of(0)
        for k, (at, width) in enumerate(chunks):
            cols = slice(at, at + width)
            dhm = ahead
            if k + 1 < len(chunks):
                ahead = dhm_of(k + 1)
            for top in range(0, tm, ROW_BAND):
                band = slice(top, top + ROW_BAND)
                av = a_ref[band, cols].astype(F32)
                bv = b_ref[band, cols].astype(F32)
                sg = _sigmoid(av)
                dv = dhm[band]
                da_ref[band, cols] = (dv * bv * (sg * (1.0 + av * (1.0 - sg)))).astype(BF16)
                db_ref[band, cols] = (dv * (av * sg)).astype(BF16)
        half = pl.ds(pl.multiple_of(j * fh, GROUP), fh)
        dx_ref[...] += _dot(da_ref[...], w1s[half, :], NN) + _dot(db_ref[...], w3s[half, :], NN)

        @pl.when(j == halves - 1)
        def _():
            xv = x_ref[...]
            g = g_ref[...]
            r = lax.rsqrt(jnp.mean(xv * xv, axis=-1, keepdims=True) + EPS)
            nh = xv * r
            total = dx_ref[...]
            dg_ref[...] += jnp.sum(total * nh, axis=0, keepdims=True)
            dnh = total * g
            dx_ref[...] = dh_ref[...] + r * (dnh - nh * jnp.mean(dnh * nh, axis=-1, keepdims=True))

    tile = pl.BlockSpec((tm, d), lambda i, j: (i, 0))
    row = pl.BlockSpec((1, d), lambda i, j: (0, 0))
    wide = pl.BlockSpec((tm, fh), lambda i, j: (i, j))
    return _grid_call(
        body,
        carries,
        name=name,
        grid=(s // tm, halves),
        in_specs=[tile, tile, wide, wide, row, ANY, ANY, ANY],
        out_specs=[tile, wide, wide, tile, row],
        out_shape=[
            jax.ShapeDtypeStruct((s, d), F32),
            jax.ShapeDtypeStruct((s, ffn), BF16),
            jax.ShapeDtypeStruct((s, ffn), BF16),
            jax.ShapeDtypeStruct((s, d), BF16),
            jax.ShapeDtypeStruct((1, d), F32),
        ],
        scratch_shapes=[pltpu.VMEM((ffn, d), BF16)] * 3 + [pltpu.SemaphoreType.DMA((3 * NDEV * LOAD_PIECES,))],
        args=[dh, x, a, b, gain] + list(weights),
    )


SWAP_PIECES = 2


def _wgrad(lhs, rhs, scale, name, carries=()):
    s, m = lhs.shape
    n = rhs.shape[1]
    rs = m // NDEV
    tk = min(512, s)
    steps = s // tk
    pieces = [(j, at, size) for j in range(NCHIP) for at, size in _pieces(rs, SWAP_PIECES)]

    def body(l_ref, r_ref, o_ref, acc, mine, theirs, send_sems, recv_sems):
        k = pl.program_id(0)

        @pl.when(k == 0)
        def _():
            acc[...] = jnp.zeros_like(acc)

        acc[...] += _dot(l_ref[...], r_ref[...].astype(BF16), TN)

        @pl.when(k == steps - 1)
        def _():
            for p in range(NDEV):
                mine[p % 2, p // 2] = (acc[p * rs:(p + 1) * rs, :] * scale).astype(BF16)
            c = lax.axis_index("c")
            copies = [
                _remote(mine.at[1 - c, j, pl.ds(at, size), :], theirs.at[j, pl.ds(at, size), :],
                        send_sems.at[q], recv_sems.at[q], _peer(1))
                for q, (j, at, size) in enumerate(pieces)
            ]
            for cp in copies:
                cp.start()
            for cp in copies:
                cp.wait_recv()
            o_ref[...] = (mine[c].astype(F32) + theirs[...].astype(F32)).astype(BF16)
            for cp in copies:
                cp.wait_send()

    (out,), carried = _grid_call(
        body,
        carries,
        name=name,
        grid=(steps,),
        in_specs=[pl.BlockSpec((tk, m), lambda k: (k, 0)), pl.BlockSpec((tk, n), lambda k: (k, 0))],
        out_specs=[pl.BlockSpec((NCHIP, rs, n), lambda k: (0, 0, 0))],
        out_shape=[jax.ShapeDtypeStruct((NCHIP, rs, n), BF16)],
        scratch_shapes=[
            pltpu.VMEM((m, n), F32), pltpu.VMEM((2, NCHIP, rs, n), BF16), pltpu.VMEM((NCHIP, rs, n), BF16),
            pltpu.SemaphoreType.DMA((len(pieces),)), pltpu.SemaphoreType.DMA((len(pieces),)),
        ],
        args=[lhs, rhs],
    )
    return out, carried


def _mix_constants(s):
    c = GROUP
    lg = np.log1p(-np.exp2(-5.0 - np.arange(RET_HEADS, dtype=np.float32))).astype(np.float32)
    pos = np.arange(c, dtype=np.float32)
    rel = pos[:, None] - pos[None, :]
    decay = np.where(rel[None] >= 0, np.exp(lg[:, None, None] * np.maximum(rel, 0.0)[None]), 0.0).astype(np.float32)
    ktail = np.exp(lg[:, None] * (c - 1 - pos)[None, :]).astype(np.float32)
    qhead = np.exp(lg[:, None] * (pos + 1.0)[None, :]).astype(np.float32)
    chunk_decay = [float(v) for v in np.exp(lg * np.float32(c)).astype(np.float32)]
    ones = np.ones((1, 1, c), np.float32)
    inv_freq = (1.0 / (np.float32(ROPE_BASE) ** (np.arange(0, c, 2, dtype=np.float32) / np.float32(c)))).astype(np.float32)
    ang = (np.arange(s, dtype=np.float32)[:, None] * inv_freq[None, :]).astype(np.float32)
    cos, sin = np.cos(ang).astype(np.float32), np.sin(ang).astype(np.float32)
    return dict(
        decay=jnp.asarray(decay),
        ktail=jnp.asarray(ktail[:, :, None] * ones),
        qhead=jnp.asarray(qhead[:, :, None] * ones),
        chunk_decay=chunk_decay,
        cos=jnp.asarray(np.concatenate([cos, cos], axis=-1)),
        sin=jnp.asarray(np.concatenate([-sin, sin], axis=-1)),
    )


def _rope(t, cos, sin):
    return t * cos + pltpu.roll(t, GROUP // 2, axis=1) * sin


def _rope_bwd(dt, cos, sin):
    return dt * cos + pltpu.roll(dt * sin, GROUP // 2, axis=1)


def _window_sums(ext, w, forward):
    rows = ext.shape[0]
    acc, k = ext, 1
    while k < w:
        acc = acc + pltpu.roll(acc, k if forward else rows - k, axis=0)
        k *= 2
    return acc


def _pool_counts(tile, tm, w):
    t = lax.broadcasted_iota(jnp.int32, (tm, 1), 0) + tile * tm
    return jnp.minimum(t + 1, w).astype(F32)


def _mix_fwd(h1, gain, weights, pool_w, pool_scale, ret_gain, consts, carries=()):
    s, d = h1.shape
    pwid = N_POOL_GROUPS * GROUP
    rwid = RET_HEADS * GROUP
    inw = pwid + 4 * rwid
    tm = min(256, s)
    nck = tm // GROUP
    cd = consts["chunk_decay"]

    def body(h_ref, g_ref, bin_, bout, pw_ref, ps_ref, rg_ref, cos_ref, sin_ref, dec_ref, kt_ref, qh_ref,
             h2_ref, proj_ref, o_ref, rs_ref, wins, wouts, state, carry, mbuf, sems):
        i = pl.program_id(0)

        @pl.when(i == 0)
        def _():
            for cp in _load_weights(((bin_, wins), (bout, wouts)), sems):
                cp.wait()
            state[...] = jnp.zeros_like(state)
            carry[...] = jnp.zeros_like(carry)

        hv = h_ref[...]
        r = lax.rsqrt(jnp.mean(hv * hv, axis=-1, keepdims=True) + EPS)
        u = (hv * r * g_ref[...]).astype(BF16)
        proj_ref[...] = _dot(u, wins[...], NT)

        ext = jnp.concatenate([carry[...], proj_ref[:, 0:pwid]], axis=0)
        carry[...] = proj_ref[tm - MAX_WINDOW:tm, 0:pwid]
        for gi, w in enumerate(POOL_WINDOWS):
            cols = slice(gi * GROUP, (gi + 1) * GROUP)
            xg = ext[:, cols]
            ws = _window_sums(xg, w, True)[MAX_WINDOW:, :]
            pooled = ws / _pool_counts(i, tm, w) - xg[MAX_WINDOW:, :]
            z = _dot(pooled.astype(BF16), pw_ref[gi].astype(BF16), NN)
            mbuf[:, cols] = (z * ps_ref[:, cols]).astype(BF16)

        cos, sin = cos_ref[...], sin_ref[...]
        for h in range(RET_HEADS):
            cq = slice(pwid + h * GROUP, pwid + (h + 1) * GROUP)
            ck = slice(pwid + rwid + h * GROUP, pwid + rwid + (h + 1) * GROUP)
            cv = slice(pwid + 2 * rwid + h * GROUP, pwid + 2 * rwid + (h + 1) * GROUP)
            cg = slice(pwid + 3 * rwid + h * GROUP, pwid + 3 * rwid + (h + 1) * GROUP)
            ch = slice(h * GROUP, (h + 1) * GROUP)
            qr = _rope(proj_ref[:, cq], cos, sin)
            kr = _rope(proj_ref[:, ck], cos, sin) * (GROUP ** -0.5)
            vb = proj_ref[:, cv].astype(BF16)
            for n in range(nck):
                rows = slice(n * GROUP, (n + 1) * GROUP)
                qc, kc, vc = qr[rows], kr[rows], vb[rows]
                rb = state[h]
                rs_ref[n, h] = rb
                p = (_dot(qc.astype(BF16), kc.astype(BF16), NT) * dec_ref[h]).astype(BF16)
                o = _dot(p, vc, NN) + _dot((qc * qh_ref[h]).astype(BF16), rb.astype(BF16), NN)
                state[h] = cd[h] * rb + _dot((kc * kt_ref[h]).astype(BF16), vc, TN)
                o_ref[rows, ch] = o
                on = o * lax.rsqrt(jnp.mean(o * o, axis=-1, keepdims=True) + EPS)
                gv = proj_ref[rows, cg]
                mbuf[rows, pwid + h * GROUP:pwid + (h + 1) * GROUP] = (
                    gv * _sigmoid(gv) * (on * rg_ref[:, ch])
                ).astype(BF16)
        h2_ref[...] = hv + _dot(mbuf[...], wouts[...], NN)

    tile = pl.BlockSpec((tm, d), lambda i: (i, 0))
    full = lambda shape: pl.BlockSpec(shape, lambda i: (0,) * len(shape))
    return _grid_call(
        body,
        carries,
        name="mix_fwd",
        grid=(s // tm,),
        in_specs=[
            tile, full((1, d)), ANY, ANY,
            full((N_POOL_GROUPS, GROUP, GROUP)), full((1, pwid)), full((1, rwid)),
            pl.BlockSpec((tm, GROUP), lambda i: (i, 0)), pl.BlockSpec((tm, GROUP), lambda i: (i, 0)),
            full((RET_HEADS, GROUP, GROUP)), full((RET_HEADS, GROUP, GROUP)), full((RET_HEADS, GROUP, GROUP)),
        ],
        out_specs=[
            tile,
            pl.BlockSpec((tm, inw), lambda i: (i, 0)),
            pl.BlockSpec((tm, rwid), lambda i: (i, 0)),
            pl.BlockSpec((nck, RET_HEADS, GROUP, GROUP), lambda i: (i, 0, 0, 0)),
        ],
        out_shape=[
            jax.ShapeDtypeStruct((s, d), F32),
            jax.ShapeDtypeStruct((s, inw), F32),
            jax.ShapeDtypeStruct((s, rwid), F32),
            jax.ShapeDtypeStruct((s // GROUP, RET_HEADS, GROUP, GROUP), F32),
        ],
        scratch_shapes=[
            pltpu.VMEM((inw, d), BF16), pltpu.VMEM((d, d), BF16),
            pltpu.VMEM((RET_HEADS, GROUP, GROUP), F32), pltpu.VMEM((MAX_WINDOW, pwid), F32),
            pltpu.VMEM((tm, d), BF16), pltpu.SemaphoreType.DMA((2 * NDEV * LOAD_PIECES,)),
        ],
        args=[h1, gain, weights[0], weights[1], pool_w, pool_scale, ret_gain,
              consts["cos"], consts["sin"], consts["decay"], consts["ktail"], consts["qhead"]],
    )


def _mix_bwd(dh2, h1, proj, o_saved, rsave, gain, weights, pool_w, pool_scale, ret_gain, consts, carries=()):
    s, d = h1.shape
    pwid = N_POOL_GROUPS * GROUP
    rwid = RET_HEADS * GROUP
    inw = pwid + 4 * rwid
    tm = min(256, s)
    nck = tm // GROUP
    nt = s // tm
    cd = consts["chunk_decay"]
    halo_per_tile = tm // MAX_WINDOW

    def body(dh2_ref, h_ref, proj_ref, halo_ref, o_ref, rs_ref, g_ref, bin_, bout, pw_ref, ps_ref, rg_ref,
             cos_ref, sin_ref, dec_ref, kt_ref, qh_ref,
             dh1_ref, dproj_ref, u_ref, m_ref, dpw_ref, dps_ref, drg_ref, dg_ref,
             wins, wouts, dstate, carry, dm, dpj, sems):
        i = pl.program_id(0)
        tile = nt - 1 - i

        @pl.when(i == 0)
        def _():
            for cp in _load_weights(((bin_, wins), (bout, wouts)), sems):
                cp.wait()
            dstate[...] = jnp.zeros_like(dstate)
            carry[...] = jnp.zeros_like(carry)
            for ref in (dpw_ref, dps_ref, drg_ref, dg_ref):
                ref[...] = jnp.zeros_like(ref)

        dh2v = dh2_ref[...]
        dm[...] = _dot(dh2v.astype(BF16), wouts[...], NT)
        hv = h_ref[...]
        g = g_ref[...]
        r = lax.rsqrt(jnp.mean(hv * hv, axis=-1, keepdims=True) + EPS)
        uh = hv * r
        u_ref[...] = (uh * g).astype(BF16)

        halo = jnp.where(tile == 0, 0.0, halo_ref[...])
        ext = jnp.concatenate([halo, proj_ref[:, 0:pwid]], axis=0)
        next_dpn = carry[...]
        for gi, w in enumerate(POOL_WINDOWS):
            cols = slice(gi * GROUP, (gi + 1) * GROUP)
            xg = ext[:, cols]
            cnt = _pool_counts(tile, tm, w)
            pooled = (_window_sums(xg, w, True)[MAX_WINDOW:, :] / cnt - xg[MAX_WINDOW:, :]).astype(BF16)
            pwb = pw_ref[gi].astype(BF16)
            z = _dot(pooled, pwb, NN)
            scale = ps_ref[:, cols]
            m_ref[:, cols] = (z * scale).astype(BF16)
            da = dm[:, cols]
            dps_ref[:, cols] += jnp.sum(da * z, axis=0, keepdims=True)
            dz = (da * scale).astype(BF16)
            dpw_ref[gi] += _dot(pooled, dz, TN)
            dpl = _dot(dz, pwb, NT)
            dpn = dpl / cnt
            ext2 = jnp.concatenate([dpn, next_dpn[:, cols]], axis=0)
            dpj[:, cols] = (_window_sums(ext2, w, False)[0:tm, :] - dpl).astype(BF16)
            carry[:, cols] = dpn[0:MAX_WINDOW, :]

        cos, sin = cos_ref[...], sin_ref[...]
        for h in range(RET_HEADS):
            cq = slice(pwid + h * GROUP, pwid + (h + 1) * GROUP)
            ck = slice(pwid + rwid + h * GROUP, pwid + rwid + (h + 1) * GROUP)
            cv = slice(pwid + 2 * rwid + h * GROUP, pwid + 2 * rwid + (h + 1) * GROUP)
            cg = slice(pwid + 3 * rwid + h * GROUP, pwid + 3 * rwid + (h + 1) * GROUP)
            ch = slice(h * GROUP, (h + 1) * GROUP)
            qr = _rope(proj_ref[:, cq], cos, sin)
            kr = _rope(proj_ref[:, ck], cos, sin) * (GROUP ** -0.5)
            vb = proj_ref[:, cv].astype(BF16)
            gv = proj_ref[:, cg]
            ov = o_ref[:, ch]
            ro = lax.rsqrt(jnp.mean(ov * ov, axis=-1, keepdims=True) + EPS)
            on = ov * ro
            rg = rg_ref[:, ch]
            db = dm[:, pwid + h * GROUP:pwid + (h + 1) * GROUP]
            sg = _sigmoid(gv)
            sl = gv * sg
            m_ref[:, pwid + h * GROUP:pwid + (h + 1) * GROUP] = (sl * (on * rg)).astype(BF16)
            dpj[:, cg] = (db * (on * rg) * (sg * (1.0 + gv * (1.0 - sg)))).astype(BF16)
            drg_ref[:, ch] += jnp.sum(db * sl * on, axis=0, keepdims=True)
            don = db * sl * rg
            do = (ro * (don - on * jnp.mean(don * on, axis=-1, keepdims=True))).astype(BF16)
            for n in reversed(range(nck)):
                rows = slice(n * GROUP, (n + 1) * GROUP)
                qc, kc, vc, dob = qr[rows], kr[rows], vb[rows], do[rows]
                qcb, kcb = qc.astype(BF16), kc.astype(BF16)
                qh = (qc * qh_ref[h]).astype(BF16)
                kt = (kc * kt_ref[h]).astype(BF16)
                rn = rs_ref[n, h].astype(BF16)
                dnext = dstate[h]
                dnb = dnext.astype(BF16)
                dec = dec_ref[h]
                p = (_dot(qcb, kcb, NT) * dec).astype(BF16)
                ds = (_dot(dob, vc, NT) * dec).astype(BF16)
                dv = _dot(p, dob, TN) + _dot(kt, dnb, NN)
                dq = _dot(ds, kcb, NN) + _dot(dob, rn, NT) * qh_ref[h]
                dk = _dot(ds, qcb, TN) + _dot(vc, dnb, NT) * kt_ref[h]
                dstate[h] = cd[h] * dnext + _dot(qh, dob, TN)
                dpj[rows, cq] = _rope_bwd(dq, cos[rows], sin[rows]).astype(BF16)
                dpj[rows, ck] = _rope_bwd(dk * (GROUP ** -0.5), cos[rows], sin[rows]).astype(BF16)
                dpj[rows, cv] = dv.astype(BF16)

        dproj_ref[...] = dpj[...]
        du = _dot(dpj[...], wins[...], NN)
        dg_ref[...] += jnp.sum(du * uh, axis=0, keepdims=True)
        dn = du * g
        dh1_ref[...] = dh2v + r * (dn - uh * jnp.mean(dn * uh, axis=-1, keepdims=True))

    rev = lambda i: (nt - 1 - i, 0)
    tile = pl.BlockSpec((tm, d), rev)
    full = lambda shape: pl.BlockSpec(shape, lambda i: (0,) * len(shape))
    return _grid_call(
        body,
        carries,
        name="mix_bwd",
        grid=(nt,),
        in_specs=[
            tile, tile,
            pl.BlockSpec((tm, inw), rev),
            pl.BlockSpec((MAX_WINDOW, pwid), lambda i: (jnp.maximum((nt - 1 - i) * halo_per_tile - 1, 0), 0)),
            pl.BlockSpec((tm, rwid), rev),
            pl.BlockSpec((nck, RET_HEADS, GROUP, GROUP), lambda i: (nt - 1 - i, 0, 0, 0)),
            full((1, d)), ANY, ANY,
            full((N_POOL_GROUPS, GROUP, GROUP)), full((1, pwid)), full((1, rwid)),
            pl.BlockSpec((tm, GROUP), rev), pl.BlockSpec((tm, GROUP), rev),
            full((RET_HEADS, GROUP, GROUP)), full((RET_HEADS, GROUP, GROUP)), full((RET_HEADS, GROUP, GROUP)),
        ],
        out_specs=[
            tile, pl.BlockSpec((tm, inw), rev), tile, tile,
            full((N_POOL_GROUPS, GROUP, GROUP)), full((1, pwid)), full((1, rwid)), full((1, d)),
        ],
        out_shape=[
            jax.ShapeDtypeStruct((s, d), F32),
            jax.ShapeDtypeStruct((s, inw), BF16),
            jax.ShapeDtypeStruct((s, d), BF16),
            jax.ShapeDtypeStruct((s, d), BF16),
            jax.ShapeDtypeStruct((N_POOL_GROUPS, GROUP, GROUP), F32),
            jax.ShapeDtypeStruct((1, pwid), F32),
            jax.ShapeDtypeStruct((1, rwid), F32),
            jax.ShapeDtypeStruct((1, d), F32),
        ],
        scratch_shapes=[
            pltpu.VMEM((inw, d), BF16), pltpu.VMEM((d, d), BF16),
            pltpu.VMEM((RET_HEADS, GROUP, GROUP), F32), pltpu.VMEM((MAX_WINDOW, pwid), F32),
            pltpu.VMEM((tm, d), F32), pltpu.VMEM((tm, inw), BF16), pltpu.SemaphoreType.DMA((2 * NDEV * LOAD_PIECES,)),
        ],
        args=[dh2, h1, proj, proj, o_saved, rsave, gain, weights[0], weights[1], pool_w, pool_scale, ret_gain,
              consts["cos"], consts["sin"], consts["decay"], consts["ktail"], consts["qhead"]],
    )


def _adam(w, g, m, v):
    m = ADAM_B1 * m + (1.0 - ADAM_B1) * g
    v = ADAM_B2 * v + (1.0 - ADAM_B2) * jnp.square(g)
    m_hat = m / (1.0 - ADAM_B1 ** ADAM_STEP)
    v_hat = v / (1.0 - ADAM_B2 ** ADAM_STEP)
    delta = -ADAM_LR * (m_hat / (jnp.sqrt(v_hat) + ADAM_EPS) + ADAM_WD * w)
    return delta, m, v


def _adamw_big(w, parts, m, v, name):
    rows, d = w.shape
    tr = _row_tile(rows, 512)

    def body(w_ref, p_ref, m_ref, v_ref, g_ref, d_ref, nm_ref, nv_ref):
        g = p_ref[0].astype(F32)
        for q in range(1, NCHIP):
            g = g + p_ref[q].astype(F32)
        g_ref[...] = g
        d_ref[...], nm_ref[...], nv_ref[...] = _adam(w_ref[...], g, m_ref[...], v_ref[...])

    spec = pl.BlockSpec((tr, d), lambda i: (i, 0))
    return _call(
        body,
        name=name,
        grid=(rows // tr,),
        in_specs=[spec, pl.BlockSpec((NCHIP, tr, d), lambda i: (0, i, 0)), spec, spec],
        out_specs=[spec] * 4,
        out_shape=[jax.ShapeDtypeStruct((rows, d), F32)] * 4,
        compiler_params=_seq(1),
    )(w, parts, m, v)


def _adamw_small(stats_all, pw_all, ws, ms, vs, d, pwid):
    nsmall = len(ws)

    def body(*refs):
        st_ref, pwa_ref = refs[0], refs[1]
        w_refs = refs[2:2 + nsmall]
        m_refs = refs[2 + nsmall:2 + 2 * nsmall]
        v_refs = refs[2 + 2 * nsmall:2 + 3 * nsmall]
        outs = refs[2 + 3 * nsmall:]
        st = st_ref[0]
        pwg = pwa_ref[0]
        for q in range(1, NDEV):
            st = st + st_ref[q]
            pwg = pwg + pwa_ref[q]
        grads = [st[0:1, :], st[1:2, :], st[2:3, :], st[3:4, :], st[4:5, 0:pwid], st[4:5, pwid:2 * pwid], pwg]
        outs[0][...] = jnp.zeros((1, GROUP), F32) + jnp.sum(st[5:6, :])
        for j in range(nsmall):
            delta, nm, nv = _adam(w_refs[j][...], grads[j], m_refs[j][...], v_refs[j][...])
            outs[1 + 4 * j][...] = grads[j]
            outs[2 + 4 * j][...] = delta
            outs[3 + 4 * j][...] = nm
            outs[4 + 4 * j][...] = nv

    out_shape = [jax.ShapeDtypeStruct((1, GROUP), F32)]
    for w in ws:
        out_shape += [jax.ShapeDtypeStruct(w.shape, F32)] * 4
    return _call(body, name="adamw_small", out_shape=out_shape, compiler_params=_params())(
        stats_all, pw_all, *ws, *ms, *vs
    )


def kernel(x, ffn1_norm, ffn1_w1, ffn1_w3, ffn1_w2, mix_norm, w_in, pool_w, pool_scale, ret_norm, w_out, ffn2_norm, ffn2_w1, ffn2_w3, ffn2_w2, final_norm, loss_target, m_ffn1_norm, m_ffn1_w1, m_ffn1_w3, m_ffn1_w2, m_mix_norm, m_w_in, m_pool_w, m_pool_scale, m_ret_norm, m_w_out, m_ffn2_norm, m_ffn2_w1, m_ffn2_w3, m_ffn2_w2, m_final_norm, v_ffn1_norm, v_ffn1_w1, v_ffn1_w3, v_ffn1_w2, v_mix_norm, v_w_in, v_pool_w, v_pool_scale, v_ret_norm, v_w_out, v_ffn2_norm, v_ffn2_w1, v_ffn2_w3, v_ffn2_w2, v_final_norm):
    s, d = x.shape[1], x.shape[2]
    ffn = ffn1_w1.shape[2] * NDEV
    pwid = pool_scale.shape[1]
    xs, tgt = x[0], loss_target[0]
    consts = _mix_constants(s)
    pw3 = pool_w[0]
    fnorm = final_norm.reshape(1, d)

    rows_of = lambda w, transposed: (w[0].T if transposed else w[0]).astype(BF16)
    send_f1 = [rows_of(ffn1_w1, True), rows_of(ffn1_w3, True), rows_of(ffn1_w2, False)]
    send_mix = [rows_of(w_in, True), rows_of(w_out, False)]
    send_f2 = [rows_of(ffn2_w1, True), rows_of(ffn2_w3, True), rows_of(ffn2_w2, False)]

    (w_f1,) = _comm_call([_Gather(send_f1)], "gather_ffn1")
    (h1, a1, b1, hm1), (more,) = _ffn_fwd(xs, ffn1_norm, w_f1, ffn, carries=[_Gather(send_mix + send_f2[:1])])
    w_mix = more[:2]
    (h2, proj, o_saved, rsave), (rest,) = _mix_fwd(
        h1, mix_norm, w_mix, pw3, pool_scale, ret_norm, consts, carries=[_Gather(send_f2[1:])]
    )
    w_f2 = more[2:] + rest
    (dh3, a2, b2, hm2, dgf, loss_cols), _ = _ffn_fwd(h2, ffn2_norm, w_f2, ffn, head=(fnorm, tgt))

    (dh2, da2, db2, n2, dg2), _ = _ffn_bwd(dh3, h2, a2, b2, ffn2_norm, w_f2, ffn, "ffn2_bwd")
    sum_f2w1, _ = _wgrad(da2, n2, 1.0, "ffn2_w1_grad")
    sum_f2w3, ((parts_f2w1,),) = _wgrad(db2, n2, 1.0, "ffn2_w3_grad", carries=[_ChipScatter([sum_f2w1])])
    sum_f2w2, ((parts_f2w3,),) = _wgrad(hm2, dh3, 0.5, "ffn2_w2_grad", carries=[_ChipScatter([sum_f2w3])])

    (dh1, dproj, u, mm, dpw, dps, drg, dgm), ((parts_f2w2,),) = _mix_bwd(
        dh2, h1, proj, o_saved, rsave, mix_norm, w_mix, pw3, pool_scale, ret_norm, consts,
        carries=[_ChipScatter([sum_f2w2])],
    )
    (dx, da1, db1, n1, dg1), _ = _ffn_bwd(dh1, xs, a1, b1, ffn1_norm, w_f1, ffn, "ffn1_bwd")
    stats = jnp.concatenate(
        [dg1, dgm, dg2, dgf, jnp.concatenate([dps, drg], axis=1), loss_cols, jnp.zeros((2, d), F32)], axis=0
    )
    small = _GatherDirect([stats, dpw.reshape(N_POOL_GROUPS * GROUP, GROUP)])
    sum_f1w2, ((stats_all, pw_all),) = _wgrad(hm1, dh1, 0.5, "ffn1_w2_grad", carries=[small])
    sum_f1w1, ((parts_f1w2,),) = _wgrad(da1, n1, 1.0, "ffn1_w1_grad", carries=[_ChipScatter([sum_f1w2])])
    sum_f1w3, ((parts_f1w1,),) = _wgrad(db1, n1, 1.0, "ffn1_w3_grad", carries=[_ChipScatter([sum_f1w1])])
    sum_in, ((parts_f1w3,),) = _wgrad(dproj, u, 1.0, "w_in_grad", carries=[_ChipScatter([sum_f1w3])])
    sum_out, ((parts_in,),) = _wgrad(mm, dh2, 1.0, "w_out_grad", carries=[_ChipScatter([sum_in])])
    ((parts_out,),) = _comm_call([_ChipScatter([sum_out])], "scatter_last")

    big = (
        (ffn1_w1, m_ffn1_w1, v_ffn1_w1, parts_f1w1, True),
        (ffn1_w3, m_ffn1_w3, v_ffn1_w3, parts_f1w3, True),
        (ffn1_w2, m_ffn1_w2, v_ffn1_w2, parts_f1w2, False),
        (w_in, m_w_in, v_w_in, parts_in, True),
        (w_out, m_w_out, v_w_out, parts_out, False),
        (ffn2_w1, m_ffn2_w1, v_ffn2_w1, parts_f2w1, True),
        (ffn2_w3, m_ffn2_w3, v_ffn2_w3, parts_f2w3, True),
        (ffn2_w2, m_ffn2_w2, v_ffn2_w2, parts_f2w2, False),
    )
    big_out = []
    for j, (w, m, v, parts, t) in enumerate(big):
        view = (lambda a: a[0].T) if t else (lambda a: a[0])
        back = (lambda a: a.T[None]) if t else (lambda a: a[None])
        big_out.append([back(a) for a in _adamw_big(view(w), parts, view(m), view(v), "adamw_%d" % j)])

    small_w = (ffn1_norm, mix_norm, ffn2_norm, fnorm, pool_scale, ret_norm, pw3.reshape(-1, GROUP))
    small_m = (m_ffn1_norm, m_mix_norm, m_ffn2_norm, m_final_norm.reshape(1, d), m_pool_scale, m_ret_norm, m_pool_w.reshape(-1, GROUP))
    small_v = (v_ffn1_norm, v_mix_norm, v_ffn2_norm, v_final_norm.reshape(1, d), v_pool_scale, v_ret_norm, v_pool_w.reshape(-1, GROUP))
    res = _adamw_small(stats_all, pw_all, small_w, small_m, small_v, d, pwid)
    loss = res[0][0, 0]
    small_out = [list(res[1 + 4 * j:5 + 4 * j]) for j in range(len(small_w))]
    small_out[3] = [a.reshape(d) for a in small_out[3]]
    small_out[6] = [a.reshape(pool_w.shape) for a in small_out[6]]

    order = [small_out[0], big_out[0], big_out[1], big_out[2], small_out[1], big_out[3], small_out[6], small_out[4],
             small_out[5], big_out[4], small_out[2], big_out[5], big_out[6], big_out[7], small_out[3]]
    result = [loss, dx[None]]
    for kind in range(4):
        result += [t[kind] for t in order]
    return tuple(result)
```

```python
import functools

import numpy as np
import jax
import jax.numpy as jnp
from jax import lax
from jax.experimental import pallas as pl
from jax.experimental.pallas import tpu as pltpu

F32 = jnp.float32
BF16 = jnp.bfloat16

NDEV = 8
NCHIP = 4
EPS = 1e-6
N_POOL_GROUPS = 4
POOL_WINDOWS = (2, 4, 8, 16)
MAX_WINDOW = 16
GROUP = 128
RET_HEADS = 4
ROPE_BASE = 10000.0
ADAM_LR = 0.001
ADAM_B1 = 0.9
ADAM_B2 = 0.999
ADAM_EPS = 1e-08
ADAM_WD = 0.01
ADAM_STEP = 10

VMEM_LIMIT = 56 * 1024 * 1024
FFN_CHUNK = 256
LOAD_PIECES = 4
ROW_BAND = 32

NT = (((1,), (1,)), ((), ()))
NN = (((1,), (0,)), ((), ()))
TN = (((0,), (0,)), ((), ()))

ANY = pl.BlockSpec(memory_space=pl.ANY)


def _dot(a, b, dims):
    return lax.dot_general(a, b, dims, preferred_element_type=F32)


def _call(body, **kw):
    return pl.pallas_call(body, **kw)


def _params(**kw):
    return pltpu.CompilerParams(vmem_limit_bytes=VMEM_LIMIT, **kw)


def _seq(n):
    return _params(dimension_semantics=("arbitrary",) * n)


def _peer(k):
    x, y, c = lax.axis_index("x"), lax.axis_index("y"), lax.axis_index("c")
    return (1 - x if k & 4 else x, 1 - y if k & 2 else y, 1 - c if k & 1 else c)


def _flat(pos):
    return 4 * pos[0] + 2 * pos[1] + pos[2]


def _chip(pos):
    return 2 * pos[0] + pos[1]


def _row_tile(rows, cap):
    return max(t for t in range(16, min(rows, cap) + 1, 16) if rows % t == 0)


def _pieces(rows, n):
    tiles = rows // 16
    cuts = [16 * (tiles * q // n) for q in range(n + 1)]
    return [(a, b - a) for a, b in zip(cuts[:-1], cuts[1:])]


def _load_weights(parts, sems):
    copies = []
    for buf, dst in parts:
        rows = buf.shape[1]
        for p in range(NDEV):
            for at, size in _pieces(rows, LOAD_PIECES):
                cp = pltpu.make_async_copy(
                    buf.at[p, pl.ds(at, size), :], dst.at[pl.ds(p * rows + at, size), :], sems.at[len(copies)]
                )
                cp.start()
                copies.append(cp)
    return copies


def _sigmoid(a):
    return 1.0 / (1.0 + jnp.exp(-a))


def _remote(src, dst, send_sem, recv_sem, to):
    return pltpu.make_async_remote_copy(
        src_ref=src, dst_ref=dst, send_sem=send_sem, recv_sem=recv_sem, device_id=to, device_id_type=pl.DeviceIdType.MESH
    )


class _Gather:
    X, Y, FAR = 4, 2, 6
    COPIES = 8

    def __init__(self, shards):
        n = len(shards)
        self.operands = list(shards)
        self.out_shape = [jax.ShapeDtypeStruct((NDEV,) + a.shape, a.dtype) for a in shards]
        self.sems = [
            pltpu.SemaphoreType.DMA((self.COPIES * n,)), pltpu.SemaphoreType.DMA((self.COPIES * n,)),
            pltpu.SemaphoreType.DMA((n,)),
        ]
        self.stages = [self.begin, self.relay, self.relay_far, self.end]

    def _copy(self, t, k, block, to, ins, outs, sems, own=False, half=None):
        rows = outs[t].shape[1]
        part = pl.ds(0, rows) if half is None else pl.ds(half * (rows // 2), rows // 2)
        dst = outs[t].at[_flat(block), part, :]
        at = self.COPIES * t + k
        return _remote(ins[t] if own else dst, dst, sems[0].at[at], sems[1].at[at], to)

    def _local(self, t, ins, outs, sems):
        return pltpu.make_async_copy(ins[t], outs[t].at[_flat(_peer(0))], sems[2].at[t])

    def begin(self, ins, outs, sems):
        me = _peer(0)
        for t in range(len(ins)):
            self._local(t, ins, outs, sems).start()
            for k, code in enumerate((1, self.X, self.Y)):
                self._copy(t, k, me, _peer(code), ins, outs, sems, own=True).start()

    def relay(self, ins, outs, sems):
        me, sibling = _peer(0), _peer(1)
        for t in range(len(ins)):
            self._copy(t, 1, _peer(self.X), me, ins, outs, sems).wait_recv()
            self._copy(t, 3, _peer(self.X), _peer(self.Y), ins, outs, sems, half=0).start()
            self._copy(t, 5, _peer(self.X), sibling, ins, outs, sems).start()
            self._copy(t, 2, _peer(self.Y), me, ins, outs, sems).wait_recv()
            self._copy(t, 4, _peer(self.Y), _peer(self.X), ins, outs, sems, half=1).start()
            self._copy(t, 6, _peer(self.Y), sibling, ins, outs, sems).start()

    def relay_far(self, ins, outs, sems):
        me, sibling = _peer(0), _peer(1)
        for t in range(len(ins)):
            self._copy(t, 3, _peer(self.FAR), me, ins, outs, sems, half=0).wait_recv()
            self._copy(t, 4, _peer(self.FAR), me, ins, outs, sems, half=1).wait_recv()
            self._copy(t, 7, _peer(self.FAR), sibling, ins, outs, sems).start()

    def end(self, ins, outs, sems):
        me = _peer(0)
        for t in range(len(ins)):
            self._copy(t, 0, _peer(1), me, ins, outs, sems).wait_recv()
            for k, code in ((5, self.X), (6, self.Y), (7, self.FAR)):
                self._copy(t, k, _peer(code ^ 1), me, ins, outs, sems).wait_recv()
            for k in range(self.COPIES):
                self._copy(t, k, me, me, ins, outs, sems, half=0 if k == 3 else 1 if k == 4 else None).wait_send()
            self._local(t, ins, outs, sems).wait()


class _GatherDirect:
    def __init__(self, arrays):
        n = len(arrays)
        self.operands = list(arrays)
        self.out_shape = [jax.ShapeDtypeStruct((NDEV,) + a.shape, a.dtype) for a in arrays]
        self.sems = [pltpu.SemaphoreType.DMA((7 * n,)), pltpu.SemaphoreType.DMA((7 * n,)), pltpu.SemaphoreType.DMA((n,))]
        self.stages = [self.begin, self.end]

    def begin(self, ins, outs, sems):
        mine = _flat(_peer(0))
        for t in range(len(ins)):
            pltpu.make_async_copy(ins[t], outs[t].at[mine], sems[2].at[t]).start()
            for k in range(1, NDEV):
                _remote(ins[t], outs[t].at[mine], sems[0].at[7 * t + k - 1], sems[1].at[7 * t + k - 1], _peer(k)).start()

    def end(self, ins, outs, sems):
        mine = _flat(_peer(0))
        for t in range(len(ins)):
            for k in range(1, NDEV):
                cp = _remote(ins[t], outs[t].at[_flat(_peer(k))], sems[0].at[7 * t + k - 1], sems[1].at[7 * t + k - 1], _peer(k))
                cp.wait_recv()
                cp.wait_send()
            pltpu.make_async_copy(ins[t], outs[t].at[mine], sems[2].at[t]).wait()


class _ChipScatter:
    pieces = 2

    def __init__(self, sums):
        n = len(sums) * NCHIP * self.pieces
        self.operands = list(sums)
        self.out_shape = [jax.ShapeDtypeStruct(a.shape, a.dtype) for a in sums]
        self.sems = [pltpu.SemaphoreType.DMA((n,)), pltpu.SemaphoreType.DMA((n,))]
        self.stages = [self.begin, self.end]

    def _copies(self, ins, outs, sems, arriving):
        mine = _chip(_peer(0))
        copies = []
        for t in range(len(ins)):
            rows = ins[t].shape[1] // self.pieces
            for k in (0, 4, 2, 6):
                other = _chip(_peer(k))
                for q in range(self.pieces):
                    part = pl.ds(q * rows, rows)
                    at = len(copies)
                    if k == 0:
                        cp = pltpu.make_async_copy(ins[t].at[mine, part, :], outs[t].at[mine, part, :], sems[0].at[at])
                    else:
                        landing = outs[t].at[other if arriving else mine, part, :]
                        cp = _remote(ins[t].at[other, part, :], landing, sems[0].at[at], sems[1].at[at], _peer(k))
                    copies.append(cp)
        return copies

    def begin(self, ins, outs, sems):
        for cp in self._copies(ins, outs, sems, False):
            cp.start()

    def end(self, ins, outs, sems):
        for at, cp in enumerate(self._copies(ins, outs, sems, True)):
            if at % (NCHIP * self.pieces) < self.pieces:
                cp.wait()
            else:
                cp.wait_recv()
                cp.wait_send()


def _split_refs(refs, counts):
    out, at = [], 0
    for n in counts:
        out.append(refs[at:at + n])
        at += n
    return out


def _comm_call(carries, name):
    nin = [len(c.operands) for c in carries]
    nout = [len(c.out_shape) for c in carries]
    nsem = [len(c.sems) for c in carries]

    def body(*refs):
        ins, outs, sems = _split_refs(refs, (sum(nin), sum(nout), sum(nsem)))
        parts = list(zip(carries, _split_refs(ins, nin), _split_refs(outs, nout), _split_refs(sems, nsem)))
        for depth in range(max(len(c.stages) for c in carries)):
            for c, i, o, s in parts:
                if depth < len(c.stages) - 1:
                    c.stages[depth](i, o, s)
        for c, i, o, s in parts:
            c.stages[-1](i, o, s)

    res = _call(
        body,
        name=name,
        out_shape=[sh for c in carries for sh in c.out_shape],
        in_specs=[ANY] * sum(nin),
        out_specs=[ANY] * sum(nout),
        scratch_shapes=[sm for c in carries for sm in c.sems],
        compiler_params=pltpu.CompilerParams(has_side_effects=True),
    )(*[a for c in carries for a in c.operands])
    return _split_refs(list(res), nout)


def _grid_call(body, carries, *, name, grid, in_specs, out_specs, out_shape, scratch_shapes, args):
    ni, no, ns = len(in_specs), len(out_specs), len(scratch_shapes)
    nin = [len(c.operands) for c in carries]
    nout = [len(c.out_shape) for c in carries]
    nsem = [len(c.sems) for c in carries]
    steps = int(np.prod(grid))

    def when_of(stage, count):
        first, last = (5 * steps) // 8 - 1, steps - 2
        return max(0, last if count <= 3 else first + (last - first) * (stage - 1) // (count - 3))

    def wrapped(*refs):
        ins, cins, outs, couts, scr, csems = _split_refs(refs, (ni, sum(nin), no, sum(nout), ns, sum(nsem)))
        if not carries:
            return body(*ins, *outs, *scr)
        parts = list(zip(carries, _split_refs(cins, nin), _split_refs(couts, nout), _split_refs(csems, nsem)))
        step = pl.program_id(0)
        for axis in range(1, len(grid)):
            step = step * grid[axis] + pl.program_id(axis)

        @pl.when(step == 0)
        def _():
            for c, i, o, s in parts:
                c.stages[0](i, o, s)

        body(*ins, *outs, *scr)

        for c, i, o, s in parts:
            for stage in range(1, len(c.stages) - 1):
                pl.when(step == when_of(stage, len(c.stages)))(functools.partial(c.stages[stage], i, o, s))

        @pl.when(step == steps - 1)
        def _():
            for c, i, o, s in parts:
                c.stages[-1](i, o, s)

    res = _call(
        wrapped,
        name=name,
        grid=tuple(grid),
        in_specs=list(in_specs) + [ANY] * sum(nin),
        out_specs=list(out_specs) + [ANY] * sum(nout),
        out_shape=list(out_shape) + [sh for c in carries for sh in c.out_shape],
        scratch_shapes=list(scratch_shapes) + [sm for c in carries for sm in c.sems],
        compiler_params=_seq(len(grid)),
    )(*args, *[a for c in carries for a in c.operands])
    res = list(res)
    return res[:no], _split_refs(res[no:], nout)


def _chunks(width):
    return [(at, min(FFN_CHUNK, width - at)) for at in range(0, width, FFN_CHUNK)]


def _ffn_fwd(x, gain, weights, ffn, head=None, carries=()):
    s, d = x.shape
    tm = min(512, s)

    def body(*refs):
        if head is None:
            x_ref, g_ref, b1, b3, b2, h_ref, a_ref, b_ref, hm_ref, w1s, w3s, w2s, sems = refs
        else:
            x_ref, g_ref, b1, b3, b2, gf_ref, t_ref, h_ref, a_ref, b_ref, hm_ref, dgf_ref, loss_ref, w1s, w3s, w2s, sems = refs
        i = pl.program_id(0)

        @pl.when(i == 0)
        def _():
            for cp in _load_weights(((b1, w1s), (b3, w3s), (b2, w2s)), sems):
                cp.wait()
            if head is not None:
                dgf_ref[...] = jnp.zeros_like(dgf_ref)
                loss_ref[...] = jnp.zeros_like(loss_ref)

        xv = x_ref[...]
        r = lax.rsqrt(jnp.mean(xv * xv, axis=-1, keepdims=True) + EPS)
        n = (xv * r * g_ref[...]).astype(BF16)
        acc = jnp.zeros((tm, d), F32)
        for at, width in _chunks(ffn):
            cols = slice(at, at + width)
            a = _dot(n, w1s[cols, :], NT)
            b = _dot(n, w3s[cols, :], NT)
            a_ref[:, cols] = a.astype(BF16)
            b_ref[:, cols] = b.astype(BF16)
            hm = (a * _sigmoid(a) * b).astype(BF16)
            hm_ref[:, cols] = hm
            acc = acc + _dot(hm, w2s[cols, :], NN)
        h = xv + 0.5 * acc
        if head is None:
            h_ref[...] = h
        else:
            rf = lax.rsqrt(jnp.mean(h * h, axis=-1, keepdims=True) + EPS)
            nh = h * rf
            gf = gf_ref[...]
            err = nh * gf - t_ref[...]
            loss_ref[...] += jnp.sum(err * err, axis=0, keepdims=True) * (0.5 / d)
            dy = err * (1.0 / d)
            dgf_ref[...] += jnp.sum(dy * nh, axis=0, keepdims=True)
            dn = dy * gf
            h_ref[...] = rf * (dn - nh * jnp.mean(dn * nh, axis=-1, keepdims=True))

    tile = pl.BlockSpec((tm, d), lambda i: (i, 0))
    row = pl.BlockSpec((1, d), lambda i: (0, 0))
    wide = pl.BlockSpec((tm, ffn), lambda i: (i, 0))
    in_specs = [tile, row, ANY, ANY, ANY]
    out_shape = [jax.ShapeDtypeStruct((s, d), F32)] + [jax.ShapeDtypeStruct((s, ffn), BF16)] * 3
    out_specs = [tile, wide, wide, wide]
    args = [x, gain] + list(weights)
    if head is not None:
        in_specs += [row, tile]
        args += list(head)
        out_shape += [jax.ShapeDtypeStruct((1, d), F32)] * 2
        out_specs += [row, row]
    return _grid_call(
        body,
        carries,
        name="ffn_fwd_loss" if head is not None else "ffn_fwd",
        grid=(s // tm,),
        in_specs=in_specs,
        out_specs=out_specs,
        out_shape=out_shape,
        scratch_shapes=[pltpu.VMEM((ffn, d), BF16)] * 3 + [pltpu.SemaphoreType.DMA((3 * NDEV * LOAD_PIECES,))],
        args=args,
    )


def _ffn_bwd(dh, x, a, b, gain, weights, ffn, name, carries=()):
    s, d = x.shape
    tm = min(512, s)
    halves = 2
    fh = ffn // halves

    def body(dh_ref, x_ref, a_ref, b_ref, g_ref, b1, b3, b2, dx_ref, da_ref, db_ref, n_ref, dg_ref, w1s, w3s, w2s, sems):
        i, j = pl.program_id(0), pl.program_id(1)

        @pl.when((i == 0) & (j == 0))
        def _():
            for cp in _load_weights(((b1, w1s), (b3, w3s), (b2, w2s)), sems):
                cp.wait()
            dg_ref[...] = jnp.zeros_like(dg_ref)

        @pl.when(j == 0)
        def _():
            xv = x_ref[...]
            r = lax.rsqrt(jnp.mean(xv * xv, axis=-1, keepdims=True) + EPS)
            n_ref[...] = (xv * r * g_ref[...]).astype(BF16)
            dx_ref[...] = jnp.zeros_like(dx_ref)

        dob = (0.5 * dh_ref[...]).astype(BF16)
        chunks = _chunks(fh)

        def dhm_of(k):
            at, width = chunks[k]
            return _dot(dob, w2s[pl.ds(pl.multiple_of(j * fh + at, GROUP), width), :], NT)

        ahead = dhm_of(0)
        for k, (at, width) in enumerate(chunks):
            cols = slice(at, at + width)
            dhm = ahead
            if k + 1 < len(chunks):
                ahead = dhm_of(k + 1)
            for top in range(0, tm, ROW_BAND):
                band = slice(top, top + ROW_BAND)
                av = a_ref[band, cols].astype(F32)
                bv = b_ref[band, cols].astype(F32)
                sg = _sigmoid(av)
                dv = dhm[band]
                da_ref[band, cols] = (dv * bv * (sg * (1.0 + av * (1.0 - sg)))).astype(BF16)
                db_ref[band, cols] = (dv * (av * sg)).astype(BF16)
        half = pl.ds(pl.multiple_of(j * fh, GROUP), fh)
        dx_ref[...] += _dot(da_ref[...], w1s[half, :], NN) + _dot(db_ref[...], w3s[half, :], NN)

        @pl.when(j == halves - 1)
        def _():
            xv = x_ref[...]
            g = g_ref[...]
            r = lax.rsqrt(jnp.mean(xv * xv, axis=-1, keepdims=True) + EPS)
            nh = xv * r
            total = dx_ref[...]
            dg_ref[...] += jnp.sum(total * nh, axis=0, keepdims=True)
            dnh = total * g
            dx_ref[...] = dh_ref[...] + r * (dnh - nh * jnp.mean(dnh * nh, axis=-1, keepdims=True))

    tile = pl.BlockSpec((tm, d), lambda i, j: (i, 0))
    row = pl.BlockSpec((1, d), lambda i, j: (0, 0))
    wide = pl.BlockSpec((tm, fh), lambda i, j: (i, j))
    return _grid_call(
        body,
        carries,
        name=name,
        grid=(s // tm, halves),
        in_specs=[tile, tile, wide, wide, row, ANY, ANY, ANY],
        out_specs=[tile, wide, wide, tile, row],
        out_shape=[
            jax.ShapeDtypeStruct((s, d), F32),
            jax.ShapeDtypeStruct((s, ffn), BF16),
            jax.ShapeDtypeStruct((s, ffn), BF16),
            jax.ShapeDtypeStruct((s, d), BF16),
            jax.ShapeDtypeStruct((1, d), F32),
        ],
        scratch_shapes=[pltpu.VMEM((ffn, d), BF16)] * 3 + [pltpu.SemaphoreType.DMA((3 * NDEV * LOAD_PIECES,))],
        args=[dh, x, a, b, gain] + list(weights),
    )


SWAP_PIECES = 2


def _wgrad(lhs, rhs, scale, name, carries=()):
    s, m = lhs.shape
    n = rhs.shape[1]
    rs = m // NDEV
    tk = min(1024, s)
    steps = s // tk
    pieces = [(j, at, size) for j in range(2) for at, size in _pieces(rs, SWAP_PIECES)]

    def body(l_ref, r_ref, o_ref, acc, mine, theirs, send_sems, recv_sems):
        h, k = pl.program_id(0), pl.program_id(1)

        @pl.when(k == 0)
        def _():
            acc[...] = _dot(l_ref[...], r_ref[...].astype(BF16), TN)

        @pl.when(k > 0)
        def _():
            acc[...] += _dot(l_ref[...], r_ref[...].astype(BF16), TN)

        def exchange(half):
            c = lax.axis_index("c")
            return [
                _remote(mine.at[half, 1 - c, j, pl.ds(at, size), :], theirs.at[half, j, pl.ds(at, size), :],
                        send_sems.at[half * len(pieces) + q], recv_sems.at[half * len(pieces) + q], _peer(1))
                for q, (j, at, size) in enumerate(pieces)
            ]

        def settle(half):
            for cp in exchange(half):
                cp.wait_recv()
            both = mine[half, lax.axis_index("c")].astype(F32) + theirs[half].astype(F32)
            o_ref[2 * half:2 * half + 2] = both.astype(BF16)
            for cp in exchange(half):
                cp.wait_send()

        for half in range(2):
            @pl.when((h == half) & (k == steps - 1))
            def _():
                for p in range(NCHIP):
                    mine[half, p % 2, p // 2] = (acc[p * rs:(p + 1) * rs, :] * scale).astype(BF16)
                for cp in exchange(half):
                    cp.start()
                if half == 1:
                    settle(0)
                    settle(1)

    (out,), carried = _grid_call(
        body,
        carries,
        name=name,
        grid=(2, steps),
        in_specs=[pl.BlockSpec((tk, m // 2), lambda h, k: (k, h)), pl.BlockSpec((tk, n), lambda h, k: (k, 0))],
        out_specs=[pl.BlockSpec((NCHIP, rs, n), lambda h, k: (0, 0, 0))],
        out_shape=[jax.ShapeDtypeStruct((NCHIP, rs, n), BF16)],
        scratch_shapes=[
            pltpu.VMEM((m // 2, n), F32), pltpu.VMEM((2, 2, 2, rs, n), BF16), pltpu.VMEM((2, 2, rs, n), BF16),
            pltpu.SemaphoreType.DMA((2 * len(pieces),)), pltpu.SemaphoreType.DMA((2 * len(pieces),)),
        ],
        args=[lhs, rhs],
    )
    return out, carried


def _mix_constants(s):
    c = GROUP
    lg = np.log1p(-np.exp2(-5.0 - np.arange(RET_HEADS, dtype=np.float32))).astype(np.float32)
    pos = np.arange(c, dtype=np.float32)
    rel = pos[:, None] - pos[None, :]
    decay = np.where(rel[None] >= 0, np.exp(lg[:, None, None] * np.maximum(rel, 0.0)[None]), 0.0).astype(np.float32)
    ktail = np.exp(lg[:, None] * (c - 1 - pos)[None, :]).astype(np.float32)
    qhead = np.exp(lg[:, None] * (pos + 1.0)[None, :]).astype(np.float32)
    chunk_decay = [float(v) for v in np.exp(lg * np.float32(c)).astype(np.float32)]
    ones = np.ones((1, 1, c), np.float32)
    inv_freq = (1.0 / (np.float32(ROPE_BASE) ** (np.arange(0, c, 2, dtype=np.float32) / np.float32(c)))).astype(np.float32)
    ang = (np.arange(s, dtype=np.float32)[:, None] * inv_freq[None, :]).astype(np.float32)
    cos, sin = np.cos(ang).astype(np.float32), np.sin(ang).astype(np.float32)
    return dict(
        decay=jnp.asarray(decay),
        ktail=jnp.asarray(ktail[:, :, None] * ones),
        qhead=jnp.asarray(qhead[:, :, None] * ones),
        chunk_decay=chunk_decay,
        cos=jnp.asarray(np.concatenate([cos, cos], axis=-1)),
        sin=jnp.asarray(np.concatenate([-sin, sin], axis=-1)),
    )


def _rope(t, cos, sin):
    return t * cos + pltpu.roll(t, GROUP // 2, axis=1) * sin


def _rope_bwd(dt, cos, sin):
    return dt * cos + pltpu.roll(dt * sin, GROUP // 2, axis=1)


def _window_sums(ext, w, forward):
    rows = ext.shape[0]
    acc, k = ext, 1
    while k < w:
        acc = acc + pltpu.roll(acc, k if forward else rows - k, axis=0)
        k *= 2
    return acc


def _pool_counts(tile, tm, w):
    t = lax.broadcasted_iota(jnp.int32, (tm, 1), 0) + tile * tm
    return jnp.minimum(t + 1, w).astype(F32)


def _mix_fwd(h1, gain, weights, pool_w, pool_scale, ret_gain, consts, carries=()):
    s, d = h1.shape
    pwid = N_POOL_GROUPS * GROUP
    rwid = RET_HEADS * GROUP
    inw = pwid + 4 * rwid
    tm = min(256, s)
    nck = tm // GROUP
    cd = consts["chunk_decay"]

    def body(h_ref, g_ref, bin_, bout, pw_ref, ps_ref, rg_ref, cos_ref, sin_ref, dec_ref, kt_ref, qh_ref,
             h2_ref, proj_ref, o_ref, rs_ref, wins, wouts, state, carry, mbuf, sems):
        i = pl.program_id(0)

        @pl.when(i == 0)
        def _():
            for cp in _load_weights(((bin_, wins), (bout, wouts)), sems):
                cp.wait()
            state[...] = jnp.zeros_like(state)
            carry[...] = jnp.zeros_like(carry)

        hv = h_ref[...]
        r = lax.rsqrt(jnp.mean(hv * hv, axis=-1, keepdims=True) + EPS)
        u = (hv * r * g_ref[...]).astype(BF16)
        proj_ref[...] = _dot(u, wins[...], NT)

        ext = jnp.concatenate([carry[...], proj_ref[:, 0:pwid]], axis=0)
        carry[...] = proj_ref[tm - MAX_WINDOW:tm, 0:pwid]
        for gi, w in enumerate(POOL_WINDOWS):
            cols = slice(gi * GROUP, (gi + 1) * GROUP)
            xg = ext[:, cols]
            ws = _window_sums(xg, w, True)[MAX_WINDOW:, :]
            pooled = ws / _pool_counts(i, tm, w) - xg[MAX_WINDOW:, :]
            z = _dot(pooled.astype(BF16), pw_ref[gi].astype(BF16), NN)
            mbuf[:, cols] = (z * ps_ref[:, cols]).astype(BF16)

        cos, sin = cos_ref[...], sin_ref[...]
        for h in range(RET_HEADS):
            cq = slice(pwid + h * GROUP, pwid + (h + 1) * GROUP)
            ck = slice(pwid + rwid + h * GROUP, pwid + rwid + (h + 1) * GROUP)
            cv = slice(pwid + 2 * rwid + h * GROUP, pwid + 2 * rwid + (h + 1) * GROUP)
            cg = slice(pwid + 3 * rwid + h * GROUP, pwid + 3 * rwid + (h + 1) * GROUP)
            ch = slice(h * GROUP, (h + 1) * GROUP)
            qr = _rope(proj_ref[:, cq], cos, sin)
            kr = _rope(proj_ref[:, ck], cos, sin) * (GROUP ** -0.5)
            vb = proj_ref[:, cv].astype(BF16)
            for n in range(nck):
                rows = slice(n * GROUP, (n + 1) * GROUP)
                qc, kc, vc = qr[rows], kr[rows], vb[rows]
                rb = state[h]
                rs_ref[n, h] = rb
                p = (_dot(qc.astype(BF16), kc.astype(BF16), NT) * dec_ref[h]).astype(BF16)
                o = _dot(p, vc, NN) + _dot((qc * qh_ref[h]).astype(BF16), rb.astype(BF16), NN)
                state[h] = cd[h] * rb + _dot((kc * kt_ref[h]).astype(BF16), vc, TN)
                o_ref[rows, ch] = o
                on = o * lax.rsqrt(jnp.mean(o * o, axis=-1, keepdims=True) + EPS)
                gv = proj_ref[rows, cg]
                mbuf[rows, pwid + h * GROUP:pwid + (h + 1) * GROUP] = (
                    gv * _sigmoid(gv) * (on * rg_ref[:, ch])
                ).astype(BF16)
        h2_ref[...] = hv + _dot(mbuf[...], wouts[...], NN)

    tile = pl.BlockSpec((tm, d), lambda i: (i, 0))
    full = lambda shape: pl.BlockSpec(shape, lambda i: (0,) * len(shape))
    return _grid_call(
        body,
        carries,
        name="mix_fwd",
        grid=(s // tm,),
        in_specs=[
            tile, full((1, d)), ANY, ANY,
            full((N_POOL_GROUPS, GROUP, GROUP)), full((1, pwid)), full((1, rwid)),
            pl.BlockSpec((tm, GROUP), lambda i: (i, 0)), pl.BlockSpec((tm, GROUP), lambda i: (i, 0)),
            full((RET_HEADS, GROUP, GROUP)), full((RET_HEADS, GROUP, GROUP)), full((RET_HEADS, GROUP, GROUP)),
        ],
        out_specs=[
            tile,
            pl.BlockSpec((tm, inw), lambda i: (i, 0)),
            pl.BlockSpec((tm, rwid), lambda i: (i, 0)),
            pl.BlockSpec((nck, RET_HEADS, GROUP, GROUP), lambda i: (i, 0, 0, 0)),
        ],
        out_shape=[
            jax.ShapeDtypeStruct((s, d), F32),
            jax.ShapeDtypeStruct((s, inw), F32),
            jax.ShapeDtypeStruct((s, rwid), F32),
            jax.ShapeDtypeStruct((s // GROUP, RET_HEADS, GROUP, GROUP), F32),
        ],
        scratch_shapes=[
            pltpu.VMEM((inw, d), BF16), pltpu.VMEM((d, d), BF16),
            pltpu.VMEM((RET_HEADS, GROUP, GROUP), F32), pltpu.VMEM((MAX_WINDOW, pwid), F32),
            pltpu.VMEM((tm, d), BF16), pltpu.SemaphoreType.DMA((2 * NDEV * LOAD_PIECES,)),
        ],
        args=[h1, gain, weights[0], weights[1], pool_w, pool_scale, ret_gain,
              consts["cos"], consts["sin"], consts["decay"], consts["ktail"], consts["qhead"]],
    )


def _mix_bwd(dh2, h1, proj, o_saved, rsave, gain, weights, pool_w, pool_scale, ret_gain, consts, carries=()):
    s, d = h1.shape
    pwid = N_POOL_GROUPS * GROUP
    rwid = RET_HEADS * GROUP
    inw = pwid + 4 * rwid
    tm = min(256, s)
    nck = tm // GROUP
    nt = s // tm
    cd = consts["chunk_decay"]
    halo_per_tile = tm // MAX_WINDOW

    def body(dh2_ref, h_ref, proj_ref, halo_ref, o_ref, rs_ref, g_ref, bin_, bout, pw_ref, ps_ref, rg_ref,
             cos_ref, sin_ref, dec_ref, kt_ref, qh_ref,
             dh1_ref, dproj_ref, u_ref, m_ref, dpw_ref, dps_ref, drg_ref, dg_ref,
             wins, wouts, dstate, carry, dm, dpj, sems):
        i = pl.program_id(0)
        tile = nt - 1 - i

        @pl.when(i == 0)
        def _():
            for cp in _load_weights(((bin_, wins), (bout, wouts)), sems):
                cp.wait()
            dstate[...] = jnp.zeros_like(dstate)
            carry[...] = jnp.zeros_like(carry)
            for ref in (dpw_ref, dps_ref, drg_ref, dg_ref):
                ref[...] = jnp.zeros_like(ref)

        dh2v = dh2_ref[...]
        dm[...] = _dot(dh2v.astype(BF16), wouts[...], NT)
        hv = h_ref[...]
        g = g_ref[...]
        r = lax.rsqrt(jnp.mean(hv * hv, axis=-1, keepdims=True) + EPS)
        uh = hv * r
        u_ref[...] = (uh * g).astype(BF16)

        halo = jnp.where(tile == 0, 0.0, halo_ref[...])
        ext = jnp.concatenate([halo, proj_ref[:, 0:pwid]], axis=0)
        next_dpn = carry[...]
        for gi, w in enumerate(POOL_WINDOWS):
            cols = slice(gi * GROUP, (gi + 1) * GROUP)
            xg = ext[:, cols]
            cnt = _pool_counts(tile, tm, w)
            pooled = (_window_sums(xg, w, True)[MAX_WINDOW:, :] / cnt - xg[MAX_WINDOW:, :]).astype(BF16)
            pwb = pw_ref[gi].astype(BF16)
            z = _dot(pooled, pwb, NN)
            scale = ps_ref[:, cols]
            m_ref[:, cols] = (z * scale).astype(BF16)
            da = dm[:, cols]
            dps_ref[:, cols] += jnp.sum(da * z, axis=0, keepdims=True)
            dz = (da * scale).astype(BF16)
            dpw_ref[gi] += _dot(pooled, dz, TN)
            dpl = _dot(dz, pwb, NT)
            dpn = dpl / cnt
            ext2 = jnp.concatenate([dpn, next_dpn[:, cols]], axis=0)
            dpj[:, cols] = (_window_sums(ext2, w, False)[0:tm, :] - dpl).astype(BF16)
            carry[:, cols] = dpn[0:MAX_WINDOW, :]

        cos, sin = cos_ref[...], sin_ref[...]
        for h in range(RET_HEADS):
            cq = slice(pwid + h * GROUP, pwid + (h + 1) * GROUP)
            ck = slice(pwid + rwid + h * GROUP, pwid + rwid + (h + 1) * GROUP)
            cv = slice(pwid + 2 * rwid + h * GROUP, pwid + 2 * rwid + (h + 1) * GROUP)
            cg = slice(pwid + 3 * rwid + h * GROUP, pwid + 3 * rwid + (h + 1) * GROUP)
            ch = slice(h * GROUP, (h + 1) * GROUP)
            qr = _rope(proj_ref[:, cq], cos, sin)
            kr = _rope(proj_ref[:, ck], cos, sin) * (GROUP ** -0.5)
            vb = proj_ref[:, cv].astype(BF16)
            gv = proj_ref[:, cg]
            ov = o_ref[:, ch]
            ro = lax.rsqrt(jnp.mean(ov * ov, axis=-1, keepdims=True) + EPS)
            on = ov * ro
            rg = rg_ref[:, ch]
            db = dm[:, pwid + h * GROUP:pwid + (h + 1) * GROUP]
            sg = _sigmoid(gv)
            sl = gv * sg
            m_ref[:, pwid + h * GROUP:pwid + (h + 1) * GROUP] = (sl * (on * rg)).astype(BF16)
            dpj[:, cg] = (db * (on * rg) * (sg * (1.0 + gv * (1.0 - sg)))).astype(BF16)
            drg_ref[:, ch] += jnp.sum(db * sl * on, axis=0, keepdims=True)
            don = db * sl * rg
            do = (ro * (don - on * jnp.mean(don * on, axis=-1, keepdims=True))).astype(BF16)
            for n in reversed(range(nck)):
                rows = slice(n * GROUP, (n + 1) * GROUP)
                qc, kc, vc, dob = qr[rows], kr[rows], vb[rows], do[rows]
                qcb, kcb = qc.astype(BF16), kc.astype(BF16)
                qh = (qc * qh_ref[h]).astype(BF16)
                kt = (kc * kt_ref[h]).astype(BF16)
                rn = rs_ref[n, h].astype(BF16)
                dnext = dstate[h]
                dnb = dnext.astype(BF16)
                dec = dec_ref[h]
                p = (_dot(qcb, kcb, NT) * dec).astype(BF16)
                ds = (_dot(dob, vc, NT) * dec).astype(BF16)
                dv = _dot(p, dob, TN) + _dot(kt, dnb, NN)
                dq = _dot(ds, kcb, NN) + _dot(dob, rn, NT) * qh_ref[h]
                dk = _dot(ds, qcb, TN) + _dot(vc, dnb, NT) * kt_ref[h]
                dstate[h] = cd[h] * dnext + _dot(qh, dob, TN)
                dpj[rows, cq] = _rope_bwd(dq, cos[rows], sin[rows]).astype(BF16)
                dpj[rows, ck] = _rope_bwd(dk * (GROUP ** -0.5), cos[rows], sin[rows]).astype(BF16)
                dpj[rows, cv] = dv.astype(BF16)

        dproj_ref[...] = dpj[...]
        du = _dot(dpj[...], wins[...], NN)
        dg_ref[...] += jnp.sum(du * uh, axis=0, keepdims=True)
        dn = du * g
        dh1_ref[...] = dh2v + r * (dn - uh * jnp.mean(dn * uh, axis=-1, keepdims=True))

    rev = lambda i: (nt - 1 - i, 0)
    tile = pl.BlockSpec((tm, d), rev)
    full = lambda shape: pl.BlockSpec(shape, lambda i: (0,) * len(shape))
    return _grid_call(
        body,
        carries,
        name="mix_bwd",
        grid=(nt,),
        in_specs=[
            tile, tile,
            pl.BlockSpec((tm, inw), rev),
            pl.BlockSpec((MAX_WINDOW, pwid), lambda i: (jnp.maximum((nt - 1 - i) * halo_per_tile - 1, 0), 0)),
            pl.BlockSpec((tm, rwid), rev),
            pl.BlockSpec((nck, RET_HEADS, GROUP, GROUP), lambda i: (nt - 1 - i, 0, 0, 0)),
            full((1, d)), ANY, ANY,
            full((N_POOL_GROUPS, GROUP, GROUP)), full((1, pwid)), full((1, rwid)),
            pl.BlockSpec((tm, GROUP), rev), pl.BlockSpec((tm, GROUP), rev),
            full((RET_HEADS, GROUP, GROUP)), full((RET_HEADS, GROUP, GROUP)), full((RET_HEADS, GROUP, GROUP)),
        ],
        out_specs=[
            tile, pl.BlockSpec((tm, inw), rev), tile, tile,
            full((N_POOL_GROUPS, GROUP, GROUP)), full((1, pwid)), full((1, rwid)), full((1, d)),
        ],
        out_shape=[
            jax.ShapeDtypeStruct((s, d), F32),
            jax.ShapeDtypeStruct((s, inw), BF16),
            jax.ShapeDtypeStruct((s, d), BF16),
            jax.ShapeDtypeStruct((s, d), BF16),
            jax.ShapeDtypeStruct((N_POOL_GROUPS, GROUP, GROUP), F32),
            jax.ShapeDtypeStruct((1, pwid), F32),
            jax.ShapeDtypeStruct((1, rwid), F32),
            jax.ShapeDtypeStruct((1, d), F32),
        ],
        scratch_shapes=[
            pltpu.VMEM((inw, d), BF16), pltpu.VMEM((d, d), BF16),
            pltpu.VMEM((RET_HEADS, GROUP, GROUP), F32), pltpu.VMEM((MAX_WINDOW, pwid), F32),
            pltpu.VMEM((tm, d), F32), pltpu.VMEM((tm, inw), BF16), pltpu.SemaphoreType.DMA((2 * NDEV * LOAD_PIECES,)),
        ],
        args=[dh2, h1, proj, proj, o_saved, rsave, gain, weights[0], weights[1], pool_w, pool_scale, ret_gain,
              consts["cos"], consts["sin"], consts["decay"], consts["ktail"], consts["qhead"]],
    )


def _adam(w, g, m, v):
    m = ADAM_B1 * m + (1.0 - ADAM_B1) * g
    v = ADAM_B2 * v + (1.0 - ADAM_B2) * jnp.square(g)
    m_hat = m / (1.0 - ADAM_B1 ** ADAM_STEP)
    v_hat = v / (1.0 - ADAM_B2 ** ADAM_STEP)
    delta = -ADAM_LR * (m_hat / (jnp.sqrt(v_hat) + ADAM_EPS) + ADAM_WD * w)
    return delta, m, v


def _adamw_big(w, parts, m, v, name):
    rows, d = w.shape
    tr = _row_tile(rows, 512)

    def body(w_ref, p_ref, m_ref, v_ref, g_ref, d_ref, nm_ref, nv_ref):
        g = p_ref[0].astype(F32)
        for q in range(1, NCHIP):
            g = g + p_ref[q].astype(F32)
        g_ref[...] = g
        d_ref[...], nm_ref[...], nv_ref[...] = _adam(w_ref[...], g, m_ref[...], v_ref[...])

    spec = pl.BlockSpec((tr, d), lambda i: (i, 0))
    return _call(
        body,
        name=name,
        grid=(rows // tr,),
        in_specs=[spec, pl.BlockSpec((NCHIP, tr, d), lambda i: (0, i, 0)), spec, spec],
        out_specs=[spec] * 4,
        out_shape=[jax.ShapeDtypeStruct((rows, d), F32)] * 4,
        compiler_params=_seq(1),
    )(w, parts, m, v)


def _adamw_small(stats_all, pw_all, ws, ms, vs, d, pwid):
    nsmall = len(ws)

    def body(*refs):
        st_ref, pwa_ref = refs[0], refs[1]
        w_refs = refs[2:2 + nsmall]
        m_refs = refs[2 + nsmall:2 + 2 * nsmall]
        v_refs = refs[2 + 2 * nsmall:2 + 3 * nsmall]
        outs = refs[2 + 3 * nsmall:]
        st = st_ref[0]
        pwg = pwa_ref[0]
        for q in range(1, NDEV):
            st = st + st_ref[q]
            pwg = pwg + pwa_ref[q]
        grads = [st[0:1, :], st[1:2, :], st[2:3, :], st[3:4, :], st[4:5, 0:pwid], st[4:5, pwid:2 * pwid], pwg]
        outs[0][...] = jnp.zeros((1, GROUP), F32) + jnp.sum(st[5:6, :])
        for j in range(nsmall):
            delta, nm, nv = _adam(w_refs[j][...], grads[j], m_refs[j][...], v_refs[j][...])
            outs[1 + 4 * j][...] = grads[j]
            outs[2 + 4 * j][...] = delta
            outs[3 + 4 * j][...] = nm
            outs[4 + 4 * j][...] = nv

    out_shape = [jax.ShapeDtypeStruct((1, GROUP), F32)]
    for w in ws:
        out_shape += [jax.ShapeDtypeStruct(w.shape, F32)] * 4
    return _call(body, name="adamw_small", out_shape=out_shape, compiler_params=_params())(
        stats_all, pw_all, *ws, *ms, *vs
    )


def kernel(x, ffn1_norm, ffn1_w1, ffn1_w3, ffn1_w2, mix_norm, w_in, pool_w, pool_scale, ret_norm, w_out, ffn2_norm, ffn2_w1, ffn2_w3, ffn2_w2, final_norm, loss_target, m_ffn1_norm, m_ffn1_w1, m_ffn1_w3, m_ffn1_w2, m_mix_norm, m_w_in, m_pool_w, m_pool_scale, m_ret_norm, m_w_out, m_ffn2_norm, m_ffn2_w1, m_ffn2_w3, m_ffn2_w2, m_final_norm, v_ffn1_norm, v_ffn1_w1, v_ffn1_w3, v_ffn1_w2, v_mix_norm, v_w_in, v_pool_w, v_pool_scale, v_ret_norm, v_w_out, v_ffn2_norm, v_ffn2_w1, v_ffn2_w3, v_ffn2_w2, v_final_norm):
    s, d = x.shape[1], x.shape[2]
    ffn = ffn1_w1.shape[2] * NDEV
    pwid = pool_scale.shape[1]
    xs, tgt = x[0], loss_target[0]
    consts = _mix_constants(s)
    pw3 = pool_w[0]
    fnorm = final_norm.reshape(1, d)

    rows_of = lambda w, transposed: (w[0].T if transposed else w[0]).astype(BF16)
    send_f1 = [rows_of(ffn1_w1, True), rows_of(ffn1_w3, True), rows_of(ffn1_w2, False)]
    send_mix = [rows_of(w_in, True), rows_of(w_out, False)]
    send_f2 = [rows_of(ffn2_w1, True), rows_of(ffn2_w3, True), rows_of(ffn2_w2, False)]

    (w_f1,) = _comm_call([_Gather(send_f1)], "gather_ffn1")
    (h1, a1, b1, hm1), (more,) = _ffn_fwd(xs, ffn1_norm, w_f1, ffn, carries=[_Gather(send_mix + send_f2[:1])])
    w_mix = more[:2]
    (h2, proj, o_saved, rsave), (rest,) = _mix_fwd(
        h1, mix_norm, w_mix, pw3, pool_scale, ret_norm, consts, carries=[_Gather(send_f2[1:])]
    )
    w_f2 = more[2:] + rest
    (dh3, a2, b2, hm2, dgf, loss_cols), _ = _ffn_fwd(h2, ffn2_norm, w_f2, ffn, head=(fnorm, tgt))

    (dh2, da2, db2, n2, dg2), _ = _ffn_bwd(dh3, h2, a2, b2, ffn2_norm, w_f2, ffn, "ffn2_bwd")
    sum_f2w1, _ = _wgrad(da2, n2, 1.0, "ffn2_w1_grad")
    sum_f2w3, ((parts_f2w1,),) = _wgrad(db2, n2, 1.0, "ffn2_w3_grad", carries=[_ChipScatter([sum_f2w1])])
    sum_f2w2, ((parts_f2w3,),) = _wgrad(hm2, dh3, 0.5, "ffn2_w2_grad", carries=[_ChipScatter([sum_f2w3])])

    (dh1, dproj, u, mm, dpw, dps, drg, dgm), ((parts_f2w2,),) = _mix_bwd(
        dh2, h1, proj, o_saved, rsave, mix_norm, w_mix, pw3, pool_scale, ret_norm, consts,
        carries=[_ChipScatter([sum_f2w2])],
    )
    (dx, da1, db1, n1, dg1), _ = _ffn_bwd(dh1, xs, a1, b1, ffn1_norm, w_f1, ffn, "ffn1_bwd")
    stats = jnp.concatenate(
        [dg1, dgm, dg2, dgf, jnp.concatenate([dps, drg], axis=1), loss_cols, jnp.zeros((2, d), F32)], axis=0
    )
    small = _GatherDirect([stats, dpw.reshape(N_POOL_GROUPS * GROUP, GROUP)])
    sum_f1w2, ((stats_all, pw_all),) = _wgrad(hm1, dh1, 0.5, "ffn1_w2_grad", carries=[small])
    sum_f1w1, ((parts_f1w2,),) = _wgrad(da1, n1, 1.0, "ffn1_w1_grad", carries=[_ChipScatter([sum_f1w2])])
    sum_f1w3, ((parts_f1w1,),) = _wgrad(db1, n1, 1.0, "ffn1_w3_grad", carries=[_ChipScatter([sum_f1w1])])
    sum_in, ((parts_f1w3,),) = _wgrad(dproj, u, 1.0, "w_in_grad", carries=[_ChipScatter([sum_f1w3])])
    sum_out, ((parts_in,),) = _wgrad(mm, dh2, 1.0, "w_out_grad", carries=[_ChipScatter([sum_in])])
    ((parts_out,),) = _comm_call([_ChipScatter([sum_out])], "scatter_last")

    big = (
        (ffn1_w1, m_ffn1_w1, v_ffn1_w1, parts_f1w1, True),
        (ffn1_w3, m_ffn1_w3, v_ffn1_w3, parts_f1w3, True),
        (ffn1_w2, m_ffn1_w2, v_ffn1_w2, parts_f1w2, False),
        (w_in, m_w_in, v_w_in, parts_in, True),
        (w_out, m_w_out, v_w_out, parts_out, False),
        (ffn2_w1, m_ffn2_w1, v_ffn2_w1, parts_f2w1, True),
        (ffn2_w3, m_ffn2_w3, v_ffn2_w3, parts_f2w3, True),
        (ffn2_w2, m_ffn2_w2, v_ffn2_w2, parts_f2w2, False),
    )
    big_out = []
    for j, (w, m, v, parts, t) in enumerate(big):
        view = (lambda a: a[0].T) if t else (lambda a: a[0])
        back = (lambda a: a.T[None]) if t else (lambda a: a[None])
        big_out.append([back(a) for a in _adamw_big(view(w), parts, view(m), view(v), "adamw_%d" % j)])

    small_w = (ffn1_norm, mix_norm, ffn2_norm, fnorm, pool_scale, ret_norm, pw3.reshape(-1, GROUP))
    small_m = (m_ffn1_norm, m_mix_norm, m_ffn2_norm, m_final_norm.reshape(1, d), m_pool_scale, m_ret_norm, m_pool_w.reshape(-1, GROUP))
    small_v = (v_ffn1_norm, v_mix_norm, v_ffn2_norm, v_final_norm.reshape(1, d), v_pool_scale, v_ret_norm, v_pool_w.reshape(-1, GROUP))
    res = _adamw_small(stats_all, pw_all, small_w, small_m, small_v, d, pwid)
    loss = res[0][0, 0]
    small_out = [list(res[1 + 4 * j:5 + 4 * j]) for j in range(len(small_w))]
    small_out[3] = [a.reshape(d) for a in small_out[3]]
    small_out[6] = [a.reshape(pool_w.shape) for a in small_out[6]]

    order = [small_out[0], big_out[0], big_out[1], big_out[2], small_out[1], big_out[3], small_out[6], small_out[4],
             small_out[5], big_out[4], small_out[2], big_out[5], big_out[6], big_out[7], small_out[3]]
    result = [loss, dx[None]]
    for kind in range(4):
        result += [t[kind] for t in order]
    return tuple(result)
```

```python
import functools

import numpy as np
import jax
import jax.numpy as jnp
from jax import lax
from jax.experimental import pallas as pl
from jax.experimental.pallas import tpu as pltpu

F32 = jnp.float32
BF16 = jnp.bfloat16

NDEV = 8
NCHIP = 4
EPS = 1e-6
N_POOL_GROUPS = 4
POOL_WINDOWS = (2, 4, 8, 16)
MAX_WINDOW = 16
GROUP = 128
RET_HEADS = 4
ROPE_BASE = 10000.0
ADAM_LR = 0.001
ADAM_B1 = 0.9
ADAM_B2 = 0.999
ADAM_EPS = 1e-08
ADAM_WD = 0.01
ADAM_STEP = 10

VMEM_LIMIT = 56 * 1024 * 1024
FFN_CHUNK = 256
LOAD_PIECES = 4
ROW_BAND = 32

NT = (((1,), (1,)), ((), ()))
NN = (((1,), (0,)), ((), ()))
TN = (((0,), (0,)), ((), ()))

ANY = pl.BlockSpec(memory_space=pl.ANY)


def _dot(a, b, dims):
    return lax.dot_general(a, b, dims, preferred_element_type=F32)


def _call(body, **kw):
    return pl.pallas_call(body, **kw)


def _params(**kw):
    return pltpu.CompilerParams(vmem_limit_bytes=VMEM_LIMIT, **kw)


def _seq(n):
    return _params(dimension_semantics=("arbitrary",) * n)


def _peer(k):
    x, y, c = lax.axis_index("x"), lax.axis_index("y"), lax.axis_index("c")
    return (1 - x if k & 4 else x, 1 - y if k & 2 else y, 1 - c if k & 1 else c)


def _flat(pos):
    return 4 * pos[0] + 2 * pos[1] + pos[2]


def _chip(pos):
    return 2 * pos[0] + pos[1]


def _row_tile(rows, cap):
    return max(t for t in range(16, min(rows, cap) + 1, 16) if rows % t == 0)


def _pieces(rows, n):
    tiles = rows // 16
    cuts = [16 * (tiles * q // n) for q in range(n + 1)]
    return [(a, b - a) for a, b in zip(cuts[:-1], cuts[1:])]


def _load_weights(parts, sems):
    copies = []
    for buf, dst in parts:
        rows = buf.shape[1]
        for p in range(NDEV):
            for at, size in _pieces(rows, LOAD_PIECES):
                cp = pltpu.make_async_copy(
                    buf.at[p, pl.ds(at, size), :], dst.at[pl.ds(p * rows + at, size), :], sems.at[len(copies)]
                )
                cp.start()
                copies.append(cp)
    return copies


def _sigmoid(a):
    return 1.0 / (1.0 + jnp.exp(-a))


def _remote(src, dst, send_sem, recv_sem, to):
    return pltpu.make_async_remote_copy(
        src_ref=src, dst_ref=dst, send_sem=send_sem, recv_sem=recv_sem, device_id=to, device_id_type=pl.DeviceIdType.MESH
    )


class _Gather:
    X, Y, FAR = 4, 2, 6
    COPIES = 8

    def __init__(self, shards):
        n = len(shards)
        self.operands = list(shards)
        self.out_shape = [jax.ShapeDtypeStruct((NDEV,) + a.shape, a.dtype) for a in shards]
        self.sems = [
            pltpu.SemaphoreType.DMA((self.COPIES * n,)), pltpu.SemaphoreType.DMA((self.COPIES * n,)),
            pltpu.SemaphoreType.DMA((n,)),
        ]
        self.stages = [self.begin, self.relay, self.relay_far, self.end]

    def _copy(self, t, k, block, to, ins, outs, sems, own=False, half=None):
        rows = outs[t].shape[1]
        part = pl.ds(0, rows) if half is None else pl.ds(half * (rows // 2), rows // 2)
        dst = outs[t].at[_flat(block), part, :]
        at = self.COPIES * t + k
        return _remote(ins[t] if own else dst, dst, sems[0].at[at], sems[1].at[at], to)

    def _local(self, t, ins, outs, sems):
        return pltpu.make_async_copy(ins[t], outs[t].at[_flat(_peer(0))], sems[2].at[t])

    def begin(self, ins, outs, sems):
        me = _peer(0)
        for t in range(len(ins)):
            self._local(t, ins, outs, sems).start()
            for k, code in enumerate((1, self.X, self.Y)):
                self._copy(t, k, me, _peer(code), ins, outs, sems, own=True).start()

    def relay(self, ins, outs, sems):
        me, sibling = _peer(0), _peer(1)
        for t in range(len(ins)):
            self._copy(t, 1, _peer(self.X), me, ins, outs, sems).wait_recv()
            self._copy(t, 3, _peer(self.X), _peer(self.Y), ins, outs, sems, half=0).start()
            self._copy(t, 5, _peer(self.X), sibling, ins, outs, sems).start()
            self._copy(t, 2, _peer(self.Y), me, ins, outs, sems).wait_recv()
            self._copy(t, 4, _peer(self.Y), _peer(self.X), ins, outs, sems, half=1).start()
            self._copy(t, 6, _peer(self.Y), sibling, ins, outs, sems).start()

    def relay_far(self, ins, outs, sems):
        me, sibling = _peer(0), _peer(1)
        for t in range(len(ins)):
            self._copy(t, 3, _peer(self.FAR), me, ins, outs, sems, half=0).wait_recv()
            self._copy(t, 4, _peer(self.FAR), me, ins, outs, sems, half=1).wait_recv()
            self._copy(t, 7, _peer(self.FAR), sibling, ins, outs, sems).start()

    def end(self, ins, outs, sems):
        me = _peer(0)
        for t in range(len(ins)):
            self._copy(t, 0, _peer(1), me, ins, outs, sems).wait_recv()
            for k, code in ((5, self.X), (6, self.Y), (7, self.FAR)):
                self._copy(t, k, _peer(code ^ 1), me, ins, outs, sems).wait_recv()
            for k in range(self.COPIES):
                self._copy(t, k, me, me, ins, outs, sems, half=0 if k == 3 else 1 if k == 4 else None).wait_send()
            self._local(t, ins, outs, sems).wait()


class _GatherDirect:
    def __init__(self, arrays):
        n = len(arrays)
        self.operands = list(arrays)
        self.out_shape = [jax.ShapeDtypeStruct((NDEV,) + a.shape, a.dtype) for a in arrays]
        self.sems = [pltpu.SemaphoreType.DMA((7 * n,)), pltpu.SemaphoreType.DMA((7 * n,)), pltpu.SemaphoreType.DMA((n,))]
        self.stages = [self.begin, self.end]

    def begin(self, ins, outs, sems):
        mine = _flat(_peer(0))
        for t in range(len(ins)):
            pltpu.make_async_copy(ins[t], outs[t].at[mine], sems[2].at[t]).start()
            for k in range(1, NDEV):
                _remote(ins[t], outs[t].at[mine], sems[0].at[7 * t + k - 1], sems[1].at[7 * t + k - 1], _peer(k)).start()

    def end(self, ins, outs, sems):
        mine = _flat(_peer(0))
        for t in range(len(ins)):
            for k in range(1, NDEV):
                cp = _remote(ins[t], outs[t].at[_flat(_peer(k))], sems[0].at[7 * t + k - 1], sems[1].at[7 * t + k - 1], _peer(k))
                cp.wait_recv()
                cp.wait_send()
            pltpu.make_async_copy(ins[t], outs[t].at[mine], sems[2].at[t]).wait()


class _ChipScatter:
    pieces = 2

    def __init__(self, sums):
        n = len(sums) * NCHIP * self.pieces
        self.operands = list(sums)
        self.out_shape = [jax.ShapeDtypeStruct(a.shape, a.dtype) for a in sums]
        self.sems = [pltpu.SemaphoreType.DMA((n,)), pltpu.SemaphoreType.DMA((n,))]
        self.stages = [self.begin, self.end]

    def _copies(self, ins, outs, sems, arriving):
        mine = _chip(_peer(0))
        copies = []
        for t in range(len(ins)):
            rows = ins[t].shape[1] // self.pieces
            for k in (0, 4, 2, 6):
                other = _chip(_peer(k))
                for q in range(self.pieces):
                    part = pl.ds(q * rows, rows)
                    at = len(copies)
                    if k == 0:
                        cp = pltpu.make_async_copy(ins[t].at[mine, part, :], outs[t].at[mine, part, :], sems[0].at[at])
                    else:
                        landing = outs[t].at[other if arriving else mine, part, :]
                        cp = _remote(ins[t].at[other, part, :], landing, sems[0].at[at], sems[1].at[at], _peer(k))
                    copies.append(cp)
        return copies

    def begin(self, ins, outs, sems):
        for cp in self._copies(ins, outs, sems, False):
            cp.start()

    def end(self, ins, outs, sems):
        for at, cp in enumerate(self._copies(ins, outs, sems, True)):
            if at % (NCHIP * self.pieces) < self.pieces:
                cp.wait()
            else:
                cp.wait_recv()
                cp.wait_send()


def _split_refs(refs, counts):
    out, at = [], 0
    for n in counts:
        out.append(refs[at:at + n])
        at += n
    return out


def _comm_call(carries, name):
    nin = [len(c.operands) for c in carries]
    nout = [len(c.out_shape) for c in carries]
    nsem = [len(c.sems) for c in carries]

    def body(*refs):
        ins, outs, sems = _split_refs(refs, (sum(nin), sum(nout), sum(nsem)))
        parts = list(zip(carries, _split_refs(ins, nin), _split_refs(outs, nout), _split_refs(sems, nsem)))
        for depth in range(max(len(c.stages) for c in carries)):
            for c, i, o, s in parts:
                if depth < len(c.stages) - 1:
                    c.stages[depth](i, o, s)
        for c, i, o, s in parts:
            c.stages[-1](i, o, s)

    res = _call(
        body,
        name=name,
        out_shape=[sh for c in carries for sh in c.out_shape],
        in_specs=[ANY] * sum(nin),
        out_specs=[ANY] * sum(nout),
        scratch_shapes=[sm for c in carries for sm in c.sems],
        compiler_params=pltpu.CompilerParams(has_side_effects=True),
    )(*[a for c in carries for a in c.operands])
    return _split_refs(list(res), nout)


def _grid_call(body, carries, *, name, grid, in_specs, out_specs, out_shape, scratch_shapes, args):
    ni, no, ns = len(in_specs), len(out_specs), len(scratch_shapes)
    nin = [len(c.operands) for c in carries]
    nout = [len(c.out_shape) for c in carries]
    nsem = [len(c.sems) for c in carries]
    steps = int(np.prod(grid))

    def when_of(stage, count):
        first, last = (5 * steps) // 8 - 1, steps - 2
        return max(0, last if count <= 3 else first + (last - first) * (stage - 1) // (count - 3))

    def wrapped(*refs):
        ins, cins, outs, couts, scr, csems = _split_refs(refs, (ni, sum(nin), no, sum(nout), ns, sum(nsem)))
        if not carries:
            return body(*ins, *outs, *scr)
        parts = list(zip(carries, _split_refs(cins, nin), _split_refs(couts, nout), _split_refs(csems, nsem)))
        step = pl.program_id(0)
        for axis in range(1, len(grid)):
            step = step * grid[axis] + pl.program_id(axis)

        @pl.when(step == 0)
        def _():
            for c, i, o, s in parts:
                c.stages[0](i, o, s)

        body(*ins, *outs, *scr)

        for c, i, o, s in parts:
            for stage in range(1, len(c.stages) - 1):
                pl.when(step == when_of(stage, len(c.stages)))(functools.partial(c.stages[stage], i, o, s))

        @pl.when(step == steps - 1)
        def _():
            for c, i, o, s in parts:
                c.stages[-1](i, o, s)

    res = _call(
        wrapped,
        name=name,
        grid=tuple(grid),
        in_specs=list(in_specs) + [ANY] * sum(nin),
        out_specs=list(out_specs) + [ANY] * sum(nout),
        out_shape=list(out_shape) + [sh for c in carries for sh in c.out_shape],
        scratch_shapes=list(scratch_shapes) + [sm for c in carries for sm in c.sems],
        compiler_params=_seq(len(grid)),
    )(*args, *[a for c in carries for a in c.operands])
    res = list(res)
    return res[:no], _split_refs(res[no:], nout)


def _chunks(width):
    return [(at, min(FFN_CHUNK, width - at)) for at in range(0, width, FFN_CHUNK)]


def _ffn_fwd(x, gain, weights, ffn, head=None, carries=()):
    s, d = x.shape
    tm = min(512, s)

    def body(*refs):
        if head is None:
            x_ref, g_ref, b1, b3, b2, h_ref, a_ref, b_ref, hm_ref, w1s, w3s, w2s, sems = refs
        else:
            x_ref, g_ref, b1, b3, b2, gf_ref, t_ref, h_ref, a_ref, b_ref, hm_ref, dgf_ref, loss_ref, w1s, w3s, w2s, sems = refs
        i = pl.program_id(0)

        @pl.when(i == 0)
        def _():
            for cp in _load_weights(((b1, w1s), (b3, w3s), (b2, w2s)), sems):
                cp.wait()
            if head is not None:
                dgf_ref[...] = jnp.zeros_like(dgf_ref)
                loss_ref[...] = jnp.zeros_like(loss_ref)

        xv = x_ref[...]
        r = lax.rsqrt(jnp.mean(xv * xv, axis=-1, keepdims=True) + EPS)
        n = (xv * r * g_ref[...]).astype(BF16)
        acc = jnp.zeros((tm, d), F32)
        for at, width in _chunks(ffn):
            cols = slice(at, at + width)
            a = _dot(n, w1s[cols, :], NT)
            b = _dot(n, w3s[cols, :], NT)
            a_ref[:, cols] = a.astype(BF16)
            b_ref[:, cols] = b.astype(BF16)
            hm = (a * _sigmoid(a) * b).astype(BF16)
            hm_ref[:, cols] = hm
            acc = acc + _dot(hm, w2s[cols, :], NN)
        h = xv + 0.5 * acc
        if head is None:
            h_ref[...] = h
        else:
            rf = lax.rsqrt(jnp.mean(h * h, axis=-1, keepdims=True) + EPS)
            nh = h * rf
            gf = gf_ref[...]
            err = nh * gf - t_ref[...]
            loss_ref[...] += jnp.sum(err * err, axis=0, keepdims=True) * (0.5 / d)
            dy = err * (1.0 / d)
            dgf_ref[...] += jnp.sum(dy * nh, axis=0, keepdims=True)
            dn = dy * gf
            h_ref[...] = rf * (dn - nh * jnp.mean(dn * nh, axis=-1, keepdims=True))

    tile = pl.BlockSpec((tm, d), lambda i: (i, 0))
    row = pl.BlockSpec((1, d), lambda i: (0, 0))
    wide = pl.BlockSpec((tm, ffn), lambda i: (i, 0))
    in_specs = [tile, row, ANY, ANY, ANY]
    out_shape = [jax.ShapeDtypeStruct((s, d), F32)] + [jax.ShapeDtypeStruct((s, ffn), BF16)] * 3
    out_specs = [tile, wide, wide, wide]
    args = [x, gain] + list(weights)
    if head is not None:
        in_specs += [row, tile]
        args += list(head)
        out_shape += [jax.ShapeDtypeStruct((1, d), F32)] * 2
        out_specs += [row, row]
    return _grid_call(
        body,
        carries,
        name="ffn_fwd_loss" if head is not None else "ffn_fwd",
        grid=(s // tm,),
        in_specs=in_specs,
        out_specs=out_specs,
        out_shape=out_shape,
        scratch_shapes=[pltpu.VMEM((ffn, d), BF16)] * 3 + [pltpu.SemaphoreType.DMA((3 * NDEV * LOAD_PIECES,))],
        args=args,
    )


def _ffn_bwd(dh, x, a, b, gain, weights, ffn, name, carries=()):
    s, d = x.shape
    tm = min(512, s)
    halves = 2
    fh = ffn // halves

    def body(dh_ref, x_ref, a_ref, b_ref, g_ref, b1, b3, b2, dx_ref, da_ref, db_ref, n_ref, dg_ref, w1s, w3s, w2s, sems):
        i, j = pl.program_id(0), pl.program_id(1)

        @pl.when((i == 0) & (j == 0))
        def _():
            for cp in _load_weights(((b1, w1s), (b3, w3s), (b2, w2s)), sems):
                cp.wait()
            dg_ref[...] = jnp.zeros_like(dg_ref)

        @pl.when(j == 0)
        def _():
            dx_ref[...] = jnp.zeros_like(dx_ref)

        dob = (0.5 * dh_ref[...]).astype(BF16)
        chunks = _chunks(fh)

        def dhm_of(k):
            at, width = chunks[k]
            return _dot(dob, w2s[pl.ds(pl.multiple_of(j * fh + at, GROUP), width), :], NT)

        ahead = dhm_of(0)
        for k, (at, width) in enumerate(chunks):
            cols = slice(at, at + width)
            dhm = ahead
            if k + 1 < len(chunks):
                ahead = dhm_of(k + 1)
            for top in range(0, tm, ROW_BAND):
                band = slice(top, top + ROW_BAND)
                av = a_ref[band, cols].astype(F32)
                bv = b_ref[band, cols].astype(F32)
                sg = _sigmoid(av)
                dv = dhm[band]
                da_ref[band, cols] = (dv * bv * (sg * (1.0 + av * (1.0 - sg)))).astype(BF16)
                db_ref[band, cols] = (dv * (av * sg)).astype(BF16)
        half = pl.ds(pl.multiple_of(j * fh, GROUP), fh)
        dx_ref[...] += _dot(da_ref[...], w1s[half, :], NN) + _dot(db_ref[...], w3s[half, :], NN)

        @pl.when(j == halves - 1)
        def _():
            xv = x_ref[...]
            g = g_ref[...]
            r = lax.rsqrt(jnp.mean(xv * xv, axis=-1, keepdims=True) + EPS)
            nh = xv * r
            n_ref[...] = (nh * g).astype(BF16)
            total = dx_ref[...]
            dg_ref[...] += jnp.sum(total * nh, axis=0, keepdims=True)
            dnh = total * g
            dx_ref[...] = dh_ref[...] + r * (dnh - nh * jnp.mean(dnh * nh, axis=-1, keepdims=True))

    tile = pl.BlockSpec((tm, d), lambda i, j: (i, 0))
    row = pl.BlockSpec((1, d), lambda i, j: (0, 0))
    wide = pl.BlockSpec((tm, fh), lambda i, j: (i, j))
    return _grid_call(
        body,
        carries,
        name=name,
        grid=(s // tm, halves),
        in_specs=[tile, tile, wide, wide, row, ANY, ANY, ANY],
        out_specs=[tile, wide, wide, tile, row],
        out_shape=[
            jax.ShapeDtypeStruct((s, d), F32),
            jax.ShapeDtypeStruct((s, ffn), BF16),
            jax.ShapeDtypeStruct((s, ffn), BF16),
            jax.ShapeDtypeStruct((s, d), BF16),
            jax.ShapeDtypeStruct((1, d), F32),
        ],
        scratch_shapes=[pltpu.VMEM((ffn, d), BF16)] * 3 + [pltpu.SemaphoreType.DMA((3 * NDEV * LOAD_PIECES,))],
        args=[dh, x, a, b, gain] + list(weights),
    )


SWAP_PIECES = 2


def _wgrad(lhs, rhs, scale, name, carries=()):
    s, m = lhs.shape
    n = rhs.shape[1]
    rs = m // NDEV
    tk = min(1024, s)
    steps = s // tk
    pieces = [(j, at, size) for j in range(2) for at, size in _pieces(rs, SWAP_PIECES)]

    def body(l_ref, r_ref, o_ref, acc, mine, theirs, send_sems, recv_sems):
        h, k = pl.program_id(0), pl.program_id(1)

        @pl.when(k == 0)
        def _():
            acc[...] = _dot(l_ref[...], r_ref[...].astype(BF16), TN)

        @pl.when(k > 0)
        def _():
            acc[...] += _dot(l_ref[...], r_ref[...].astype(BF16), TN)

        def exchange(half):
            c = lax.axis_index("c")
            return [
                _remote(mine.at[half, 1 - c, j, pl.ds(at, size), :], theirs.at[half, j, pl.ds(at, size), :],
                        send_sems.at[half * len(pieces) + q], recv_sems.at[half * len(pieces) + q], _peer(1))
                for q, (j, at, size) in enumerate(pieces)
            ]

        def settle(half):
            for cp in exchange(half):
                cp.wait_recv()
            both = mine[half, lax.axis_index("c")].astype(F32) + theirs[half].astype(F32)
            o_ref[2 * half:2 * half + 2] = both.astype(BF16)
            for cp in exchange(half):
                cp.wait_send()

        for half in range(2):
            @pl.when((h == half) & (k == steps - 1))
            def _():
                for p in range(NCHIP):
                    mine[half, p % 2, p // 2] = (acc[p * rs:(p + 1) * rs, :] * scale).astype(BF16)
                for cp in exchange(half):
                    cp.start()
                if half == 1:
                    settle(0)
                    settle(1)

    (out,), carried = _grid_call(
        body,
        carries,
        name=name,
        grid=(2, steps),
        in_specs=[pl.BlockSpec((tk, m // 2), lambda h, k: (k, h)), pl.BlockSpec((tk, n), lambda h, k: (k, 0))],
        out_specs=[pl.BlockSpec((NCHIP, rs, n), lambda h, k: (0, 0, 0))],
        out_shape=[jax.ShapeDtypeStruct((NCHIP, rs, n), BF16)],
        scratch_shapes=[
            pltpu.VMEM((m // 2, n), F32), pltpu.VMEM((2, 2, 2, rs, n), BF16), pltpu.VMEM((2, 2, rs, n), BF16),
            pltpu.SemaphoreType.DMA((2 * len(pieces),)), pltpu.SemaphoreType.DMA((2 * len(pieces),)),
        ],
        args=[lhs, rhs],
    )
    return out, carried


def _mix_constants(s):
    c = GROUP
    lg = np.log1p(-np.exp2(-5.0 - np.arange(RET_HEADS, dtype=np.float32))).astype(np.float32)
    pos = np.arange(c, dtype=np.float32)
    rel = pos[:, None] - pos[None, :]
    decay = np.where(rel[None] >= 0, np.exp(lg[:, None, None] * np.maximum(rel, 0.0)[None]), 0.0).astype(np.float32)
    ktail = np.exp(lg[:, None] * (c - 1 - pos)[None, :]).astype(np.float32)
    qhead = np.exp(lg[:, None] * (pos + 1.0)[None, :]).astype(np.float32)
    chunk_decay = [float(v) for v in np.exp(lg * np.float32(c)).astype(np.float32)]
    ones = np.ones((1, 1, c), np.float32)
    inv_freq = (1.0 / (np.float32(ROPE_BASE) ** (np.arange(0, c, 2, dtype=np.float32) / np.float32(c)))).astype(np.float32)
    ang = (np.arange(s, dtype=np.float32)[:, None] * inv_freq[None, :]).astype(np.float32)
    cos, sin = np.cos(ang).astype(np.float32), np.sin(ang).astype(np.float32)
    return dict(
        decay=jnp.asarray(decay),
        ktail=jnp.asarray(ktail[:, :, None] * ones),
        qhead=jnp.asarray(qhead[:, :, None] * ones),
        chunk_decay=chunk_decay,
        cos=jnp.asarray(np.concatenate([cos, cos], axis=-1)),
        sin=jnp.asarray(np.concatenate([-sin, sin], axis=-1)),
    )


def _rope(t, cos, sin):
    return t * cos + pltpu.roll(t, GROUP // 2, axis=1) * sin


def _rope_bwd(dt, cos, sin):
    return dt * cos + pltpu.roll(dt * sin, GROUP // 2, axis=1)


def _window_sums(ext, w, forward):
    rows = ext.shape[0]
    acc, k = ext, 1
    while k < w:
        acc = acc + pltpu.roll(acc, k if forward else rows - k, axis=0)
        k *= 2
    return acc


def _pool_counts(tile, tm, w):
    t = lax.broadcasted_iota(jnp.int32, (tm, 1), 0) + tile * tm
    return jnp.minimum(t + 1, w).astype(F32)


def _mix_fwd(h1, gain, weights, pool_w, pool_scale, ret_gain, consts, carries=()):
    s, d = h1.shape
    pwid = N_POOL_GROUPS * GROUP
    rwid = RET_HEADS * GROUP
    inw = pwid + 4 * rwid
    tm = min(256, s)
    nck = tm // GROUP
    cd = consts["chunk_decay"]

    def body(h_ref, g_ref, bin_, bout, pw_ref, ps_ref, rg_ref, cos_ref, sin_ref, dec_ref, kt_ref, qh_ref,
             h2_ref, proj_ref, o_ref, rs_ref, wins, wouts, state, carry, mbuf, sems):
        i = pl.program_id(0)

        @pl.when(i == 0)
        def _():
            for cp in _load_weights(((bin_, wins), (bout, wouts)), sems):
                cp.wait()
            state[...] = jnp.zeros_like(state)
            carry[...] = jnp.zeros_like(carry)

        hv = h_ref[...]
        r = lax.rsqrt(jnp.mean(hv * hv, axis=-1, keepdims=True) + EPS)
        u = (hv * r * g_ref[...]).astype(BF16)
        proj_ref[...] = _dot(u, wins[...], NT)

        ext = jnp.concatenate([carry[...], proj_ref[:, 0:pwid]], axis=0)
        carry[...] = proj_ref[tm - MAX_WINDOW:tm, 0:pwid]
        for gi, w in enumerate(POOL_WINDOWS):
            cols = slice(gi * GROUP, (gi + 1) * GROUP)
            xg = ext[:, cols]
            ws = _window_sums(xg, w, True)[MAX_WINDOW:, :]
            pooled = ws / _pool_counts(i, tm, w) - xg[MAX_WINDOW:, :]
            z = _dot(pooled.astype(BF16), pw_ref[gi].astype(BF16), NN)
            mbuf[:, cols] = (z * ps_ref[:, cols]).astype(BF16)

        cos, sin = cos_ref[...], sin_ref[...]
        for h in range(RET_HEADS):
            cq = slice(pwid + h * GROUP, pwid + (h + 1) * GROUP)
            ck = slice(pwid + rwid + h * GROUP, pwid + rwid + (h + 1) * GROUP)
            cv = slice(pwid + 2 * rwid + h * GROUP, pwid + 2 * rwid + (h + 1) * GROUP)
            cg = slice(pwid + 3 * rwid + h * GROUP, pwid + 3 * rwid + (h + 1) * GROUP)
            ch = slice(h * GROUP, (h + 1) * GROUP)
            qr = _rope(proj_ref[:, cq], cos, sin)
            kr = _rope(proj_ref[:, ck], cos, sin) * (GROUP ** -0.5)
            vb = proj_ref[:, cv].astype(BF16)
            for n in range(nck):
                rows = slice(n * GROUP, (n + 1) * GROUP)
                qc, kc, vc = qr[rows], kr[rows], vb[rows]
                rb = state[h]
                rs_ref[n, h] = rb
                p = (_dot(qc.astype(BF16), kc.astype(BF16), NT) * dec_ref[h]).astype(BF16)
                o = _dot(p, vc, NN) + _dot((qc * qh_ref[h]).astype(BF16), rb.astype(BF16), NN)
                state[h] = cd[h] * rb + _dot((kc * kt_ref[h]).astype(BF16), vc, TN)
                o_ref[rows, ch] = o
                on = o * lax.rsqrt(jnp.mean(o * o, axis=-1, keepdims=True) + EPS)
                gv = proj_ref[rows, cg]
                mbuf[rows, pwid + h * GROUP:pwid + (h + 1) * GROUP] = (
                    gv * _sigmoid(gv) * (on * rg_ref[:, ch])
                ).astype(BF16)
        h2_ref[...] = hv + _dot(mbuf[...], wouts[...], NN)

    tile = pl.BlockSpec((tm, d), lambda i: (i, 0))
    full = lambda shape: pl.BlockSpec(shape, lambda i: (0,) * len(shape))
    return _grid_call(
        body,
        carries,
        name="mix_fwd",
        grid=(s // tm,),
        in_specs=[
            tile, full((1, d)), ANY, ANY,
            full((N_POOL_GROUPS, GROUP, GROUP)), full((1, pwid)), full((1, rwid)),
            pl.BlockSpec((tm, GROUP), lambda i: (i, 0)), pl.BlockSpec((tm, GROUP), lambda i: (i, 0)),
            full((RET_HEADS, GROUP, GROUP)), full((RET_HEADS, GROUP, GROUP)), full((RET_HEADS, GROUP, GROUP)),
        ],
        out_specs=[
            tile,
            pl.BlockSpec((tm, inw), lambda i: (i, 0)),
            pl.BlockSpec((tm, rwid), lambda i: (i, 0)),
            pl.BlockSpec((nck, RET_HEADS, GROUP, GROUP), lambda i: (i, 0, 0, 0)),
        ],
        out_shape=[
            jax.ShapeDtypeStruct((s, d), F32),
            jax.ShapeDtypeStruct((s, inw), F32),
            jax.ShapeDtypeStruct((s, rwid), F32),
            jax.ShapeDtypeStruct((s // GROUP, RET_HEADS, GROUP, GROUP), F32),
        ],
        scratch_shapes=[
            pltpu.VMEM((inw, d), BF16), pltpu.VMEM((d, d), BF16),
            pltpu.VMEM((RET_HEADS, GROUP, GROUP), F32), pltpu.VMEM((MAX_WINDOW, pwid), F32),
            pltpu.VMEM((tm, d), BF16), pltpu.SemaphoreType.DMA((2 * NDEV * LOAD_PIECES,)),
        ],
        args=[h1, gain, weights[0], weights[1], pool_w, pool_scale, ret_gain,
              consts["cos"], consts["sin"], consts["decay"], consts["ktail"], consts["qhead"]],
    )


def _mix_bwd(dh2, h1, proj, o_saved, rsave, gain, weights, pool_w, pool_scale, ret_gain, consts, carries=()):
    s, d = h1.shape
    pwid = N_POOL_GROUPS * GROUP
    rwid = RET_HEADS * GROUP
    inw = pwid + 4 * rwid
    tm = min(256, s)
    nck = tm // GROUP
    nt = s // tm
    cd = consts["chunk_decay"]
    halo_per_tile = tm // MAX_WINDOW

    def body(dh2_ref, h_ref, proj_ref, halo_ref, o_ref, rs_ref, g_ref, bin_, bout, pw_ref, ps_ref, rg_ref,
             cos_ref, sin_ref, dec_ref, kt_ref, qh_ref,
             dh1_ref, dproj_ref, u_ref, m_ref, dpw_ref, dps_ref, drg_ref, dg_ref,
             wins, wouts, dstate, carry, dm, dpj, sems):
        i = pl.program_id(0)
        tile = nt - 1 - i

        @pl.when(i == 0)
        def _():
            for cp in _load_weights(((bin_, wins), (bout, wouts)), sems):
                cp.wait()
            dstate[...] = jnp.zeros_like(dstate)
            carry[...] = jnp.zeros_like(carry)
            for ref in (dpw_ref, dps_ref, drg_ref, dg_ref):
                ref[...] = jnp.zeros_like(ref)

        dh2v = dh2_ref[...]
        dm[...] = _dot(dh2v.astype(BF16), wouts[...], NT)
        hv = h_ref[...]
        g = g_ref[...]
        r = lax.rsqrt(jnp.mean(hv * hv, axis=-1, keepdims=True) + EPS)
        uh = hv * r
        u_ref[...] = (uh * g).astype(BF16)

        halo = jnp.where(tile == 0, 0.0, halo_ref[...])
        ext = jnp.concatenate([halo, proj_ref[:, 0:pwid]], axis=0)
        next_dpn = carry[...]
        for gi, w in enumerate(POOL_WINDOWS):
            cols = slice(gi * GROUP, (gi + 1) * GROUP)
            xg = ext[:, cols]
            cnt = _pool_counts(tile, tm, w)
            pooled = (_window_sums(xg, w, True)[MAX_WINDOW:, :] / cnt - xg[MAX_WINDOW:, :]).astype(BF16)
            pwb = pw_ref[gi].astype(BF16)
            z = _dot(pooled, pwb, NN)
            scale = ps_ref[:, cols]
            m_ref[:, cols] = (z * scale).astype(BF16)
            da = dm[:, cols]
            dps_ref[:, cols] += jnp.sum(da * z, axis=0, keepdims=True)
            dz = (da * scale).astype(BF16)
            dpw_ref[gi] += _dot(pooled, dz, TN)
            dpl = _dot(dz, pwb, NT)
            dpn = dpl / cnt
            ext2 = jnp.concatenate([dpn, next_dpn[:, cols]], axis=0)
            dpj[:, cols] = (_window_sums(ext2, w, False)[0:tm, :] - dpl).astype(BF16)
            carry[:, cols] = dpn[0:MAX_WINDOW, :]

        cos, sin = cos_ref[...], sin_ref[...]
        for h in range(RET_HEADS):
            cq = slice(pwid + h * GROUP, pwid + (h + 1) * GROUP)
            ck = slice(pwid + rwid + h * GROUP, pwid + rwid + (h + 1) * GROUP)
            cv = slice(pwid + 2 * rwid + h * GROUP, pwid + 2 * rwid + (h + 1) * GROUP)
            cg = slice(pwid + 3 * rwid + h * GROUP, pwid + 3 * rwid + (h + 1) * GROUP)
            ch = slice(h * GROUP, (h + 1) * GROUP)
            qr = _rope(proj_ref[:, cq], cos, sin)
            kr = _rope(proj_ref[:, ck], cos, sin) * (GROUP ** -0.5)
            vb = proj_ref[:, cv].astype(BF16)
            gv = proj_ref[:, cg]
            ov = o_ref[:, ch]
            ro = lax.rsqrt(jnp.mean(ov * ov, axis=-1, keepdims=True) + EPS)
            on = ov * ro
            rg = rg_ref[:, ch]
            db = dm[:, pwid + h * GROUP:pwid + (h + 1) * GROUP]
            sg = _sigmoid(gv)
            sl = gv * sg
            m_ref[:, pwid + h * GROUP:pwid + (h + 1) * GROUP] = (sl * (on * rg)).astype(BF16)
            dpj[:, cg] = (db * (on * rg) * (sg * (1.0 + gv * (1.0 - sg)))).astype(BF16)
            drg_ref[:, ch] += jnp.sum(db * sl * on, axis=0, keepdims=True)
            don = db * sl * rg
            do = (ro * (don - on * jnp.mean(don * on, axis=-1, keepdims=True))).astype(BF16)
            for n in reversed(range(nck)):
                rows = slice(n * GROUP, (n + 1) * GROUP)
                qc, kc, vc, dob = qr[rows], kr[rows], vb[rows], do[rows]
                qcb, kcb = qc.astype(BF16), kc.astype(BF16)
                qh = (qc * qh_ref[h]).astype(BF16)
                kt = (kc * kt_ref[h]).astype(BF16)
                rn = rs_ref[n, h].astype(BF16)
                dnext = dstate[h]
                dnb = dnext.astype(BF16)
                dec = dec_ref[h]
                p = (_dot(qcb, kcb, NT) * dec).astype(BF16)
                ds = (_dot(dob, vc, NT) * dec).astype(BF16)
                dv = _dot(p, dob, TN) + _dot(kt, dnb, NN)
                dq = _dot(ds, kcb, NN) + _dot(dob, rn, NT) * qh_ref[h]
                dk = _dot(ds, qcb, TN) + _dot(vc, dnb, NT) * kt_ref[h]
                dstate[h] = cd[h] * dnext + _dot(qh, dob, TN)
                dpj[rows, cq] = _rope_bwd(dq, cos[rows], sin[rows]).astype(BF16)
                dpj[rows, ck] = _rope_bwd(dk * (GROUP ** -0.5), cos[rows], sin[rows]).astype(BF16)
                dpj[rows, cv] = dv.astype(BF16)

        dproj_ref[...] = dpj[...]
        du = _dot(dpj[...], wins[...], NN)
        dg_ref[...] += jnp.sum(du * uh, axis=0, keepdims=True)
        dn = du * g
        dh1_ref[...] = dh2v + r * (dn - uh * jnp.mean(dn * uh, axis=-1, keepdims=True))

    rev = lambda i: (nt - 1 - i, 0)
    tile = pl.BlockSpec((tm, d), rev)
    full = lambda shape: pl.BlockSpec(shape, lambda i: (0,) * len(shape))
    return _grid_call(
        body,
        carries,
        name="mix_bwd",
        grid=(nt,),
        in_specs=[
            tile, tile,
            pl.BlockSpec((tm, inw), rev),
            pl.BlockSpec((MAX_WINDOW, pwid), lambda i: (jnp.maximum((nt - 1 - i) * halo_per_tile - 1, 0), 0)),
            pl.BlockSpec((tm, rwid), rev),
            pl.BlockSpec((nck, RET_HEADS, GROUP, GROUP), lambda i: (nt - 1 - i, 0, 0, 0)),
            full((1, d)), ANY, ANY,
            full((N_POOL_GROUPS, GROUP, GROUP)), full((1, pwid)), full((1, rwid)),
            pl.BlockSpec((tm, GROUP), rev), pl.BlockSpec((tm, GROUP), rev),
            full((RET_HEADS, GROUP, GROUP)), full((RET_HEADS, GROUP, GROUP)), full((RET_HEADS, GROUP, GROUP)),
        ],
        out_specs=[
            tile, pl.BlockSpec((tm, inw), rev), tile, tile,
            full((N_POOL_GROUPS, GROUP, GROUP)), full((1, pwid)), full((1, rwid)), full((1, d)),
        ],
        out_shape=[
            jax.ShapeDtypeStruct((s, d), F32),
            jax.ShapeDtypeStruct((s, inw), BF16),
            jax.ShapeDtypeStruct((s, d), BF16),
            jax.ShapeDtypeStruct((s, d), BF16),
            jax.ShapeDtypeStruct((N_POOL_GROUPS, GROUP, GROUP), F32),
            jax.ShapeDtypeStruct((1, pwid), F32),
            jax.ShapeDtypeStruct((1, rwid), F32),
            jax.ShapeDtypeStruct((1, d), F32),
        ],
        scratch_shapes=[
            pltpu.VMEM((inw, d), BF16), pltpu.VMEM((d, d), BF16),
            pltpu.VMEM((RET_HEADS, GROUP, GROUP), F32), pltpu.VMEM((MAX_WINDOW, pwid), F32),
            pltpu.VMEM((tm, d), F32), pltpu.VMEM((tm, inw), BF16), pltpu.SemaphoreType.DMA((2 * NDEV * LOAD_PIECES,)),
        ],
        args=[dh2, h1, proj, proj, o_saved, rsave, gain, weights[0], weights[1], pool_w, pool_scale, ret_gain,
              consts["cos"], consts["sin"], consts["decay"], consts["ktail"], consts["qhead"]],
    )


def _adam(w, g, m, v):
    m = ADAM_B1 * m + (1.0 - ADAM_B1) * g
    v = ADAM_B2 * v + (1.0 - ADAM_B2) * jnp.square(g)
    m_hat = m / (1.0 - ADAM_B1 ** ADAM_STEP)
    v_hat = v / (1.0 - ADAM_B2 ** ADAM_STEP)
    delta = -ADAM_LR * (m_hat / (jnp.sqrt(v_hat) + ADAM_EPS) + ADAM_WD * w)
    return delta, m, v


def _adamw_big(w, parts, m, v, name):
    rows, d = w.shape
    tr = _row_tile(rows, 512)

    def body(w_ref, p_ref, m_ref, v_ref, g_ref, d_ref, nm_ref, nv_ref):
        g = p_ref[0].astype(F32)
        for q in range(1, NCHIP):
            g = g + p_ref[q].astype(F32)
        g_ref[...] = g
        d_ref[...], nm_ref[...], nv_ref[...] = _adam(w_ref[...], g, m_ref[...], v_ref[...])

    spec = pl.BlockSpec((tr, d), lambda i: (i, 0))
    return _call(
        body,
        name=name,
        grid=(rows // tr,),
        in_specs=[spec, pl.BlockSpec((NCHIP, tr, d), lambda i: (0, i, 0)), spec, spec],
        out_specs=[spec] * 4,
        out_shape=[jax.ShapeDtypeStruct((rows, d), F32)] * 4,
        compiler_params=_seq(1),
    )(w, parts, m, v)


def _adamw_small(stats_all, pw_all, ws, ms, vs, d, pwid):
    nsmall = len(ws)

    def body(*refs):
        st_ref, pwa_ref = refs[0], refs[1]
        w_refs = refs[2:2 + nsmall]
        m_refs = refs[2 + nsmall:2 + 2 * nsmall]
        v_refs = refs[2 + 2 * nsmall:2 + 3 * nsmall]
        outs = refs[2 + 3 * nsmall:]
        st = st_ref[0]
        pwg = pwa_ref[0]
        for q in range(1, NDEV):
            st = st + st_ref[q]
            pwg = pwg + pwa_ref[q]
        grads = [st[0:1, :], st[1:2, :], st[2:3, :], st[3:4, :], st[4:5, 0:pwid], st[4:5, pwid:2 * pwid], pwg]
        outs[0][...] = jnp.zeros((1, GROUP), F32) + jnp.sum(st[5:6, :])
        for j in range(nsmall):
            delta, nm, nv = _adam(w_refs[j][...], grads[j], m_refs[j][...], v_refs[j][...])
            outs[1 + 4 * j][...] = grads[j]
            outs[2 + 4 * j][...] = delta
            outs[3 + 4 * j][...] = nm
            outs[4 + 4 * j][...] = nv

    out_shape = [jax.ShapeDtypeStruct((1, GROUP), F32)]
    for w in ws:
        out_shape += [jax.ShapeDtypeStruct(w.shape, F32)] * 4
    return _call(body, name="adamw_small", out_shape=out_shape, compiler_params=_params())(
        stats_all, pw_all, *ws, *ms, *vs
    )


def kernel(x, ffn1_norm, ffn1_w1, ffn1_w3, ffn1_w2, mix_norm, w_in, pool_w, pool_scale, ret_norm, w_out, ffn2_norm, ffn2_w1, ffn2_w3, ffn2_w2, final_norm, loss_target, m_ffn1_norm, m_ffn1_w1, m_ffn1_w3, m_ffn1_w2, m_mix_norm, m_w_in, m_pool_w, m_pool_scale, m_ret_norm, m_w_out, m_ffn2_norm, m_ffn2_w1, m_ffn2_w3, m_ffn2_w2, m_final_norm, v_ffn1_norm, v_ffn1_w1, v_ffn1_w3, v_ffn1_w2, v_mix_norm, v_w_in, v_pool_w, v_pool_scale, v_ret_norm, v_w_out, v_ffn2_norm, v_ffn2_w1, v_ffn2_w3, v_ffn2_w2, v_final_norm):
    s, d = x.shape[1], x.shape[2]
    ffn = ffn1_w1.shape[2] * NDEV
    pwid = pool_scale.shape[1]
    xs, tgt = x[0], loss_target[0]
    consts = _mix_constants(s)
    pw3 = pool_w[0]
    fnorm = final_norm.reshape(1, d)

    rows_of = lambda w, transposed: (w[0].T if transposed else w[0]).astype(BF16)
    send_f1 = [rows_of(ffn1_w1, True), rows_of(ffn1_w3, True), rows_of(ffn1_w2, False)]
    send_mix = [rows_of(w_in, True), rows_of(w_out, False)]
    send_f2 = [rows_of(ffn2_w1, True), rows_of(ffn2_w3, True), rows_of(ffn2_w2, False)]

    (w_f1,) = _comm_call([_Gather(send_f1)], "gather_ffn1")
    (h1, a1, b1, hm1), (more,) = _ffn_fwd(xs, ffn1_norm, w_f1, ffn, carries=[_Gather(send_mix + send_f2[:1])])
    w_mix = more[:2]
    (h2, proj, o_saved, rsave), (rest,) = _mix_fwd(
        h1, mix_norm, w_mix, pw3, pool_scale, ret_norm, consts, carries=[_Gather(send_f2[1:])]
    )
    w_f2 = more[2:] + rest
    (dh3, a2, b2, hm2, dgf, loss_cols), _ = _ffn_fwd(h2, ffn2_norm, w_f2, ffn, head=(fnorm, tgt))

    (dh2, da2, db2, n2, dg2), _ = _ffn_bwd(dh3, h2, a2, b2, ffn2_norm, w_f2, ffn, "ffn2_bwd")
    sum_f2w1, _ = _wgrad(da2, n2, 1.0, "ffn2_w1_grad")
    sum_f2w3, _ = _wgrad(db2, n2, 1.0, "ffn2_w3_grad")
    sum_f2w2, ((parts_f2w1,),) = _wgrad(hm2, dh3, 0.5, "ffn2_w2_grad", carries=[_ChipScatter([sum_f2w1])])

    (dh1, dproj, u, mm, dpw, dps, drg, dgm), ((parts_f2w3, parts_f2w2),) = _mix_bwd(
        dh2, h1, proj, o_saved, rsave, mix_norm, w_mix, pw3, pool_scale, ret_norm, consts,
        carries=[_ChipScatter([sum_f2w3, sum_f2w2])],
    )
    (dx, da1, db1, n1, dg1), _ = _ffn_bwd(dh1, xs, a1, b1, ffn1_norm, w_f1, ffn, "ffn1_bwd")
    stats = jnp.concatenate(
        [dg1, dgm, dg2, dgf, jnp.concatenate([dps, drg], axis=1), loss_cols, jnp.zeros((2, d), F32)], axis=0
    )
    small = _GatherDirect([stats, dpw.reshape(N_POOL_GROUPS * GROUP, GROUP)])
    sum_f1w2, ((stats_all, pw_all),) = _wgrad(hm1, dh1, 0.5, "ffn1_w2_grad", carries=[small])
    sum_f1w1, ((parts_f1w2,),) = _wgrad(da1, n1, 1.0, "ffn1_w1_grad", carries=[_ChipScatter([sum_f1w2])])
    sum_f1w3, ((parts_f1w1,),) = _wgrad(db1, n1, 1.0, "ffn1_w3_grad", carries=[_ChipScatter([sum_f1w1])])
    sum_in, ((parts_f1w3,),) = _wgrad(dproj, u, 1.0, "w_in_grad", carries=[_ChipScatter([sum_f1w3])])
    sum_out, ((parts_in,),) = _wgrad(mm, dh2, 1.0, "w_out_grad", carries=[_ChipScatter([sum_in])])
    ((parts_out,),) = _comm_call([_ChipScatter([sum_out])], "scatter_last")

    big = (
        (ffn1_w1, m_ffn1_w1, v_ffn1_w1, parts_f1w1, True),
        (ffn1_w3, m_ffn1_w3, v_ffn1_w3, parts_f1w3, True),
        (ffn1_w2, m_ffn1_w2, v_ffn1_w2, parts_f1w2, False),
        (w_in, m_w_in, v_w_in, parts_in, True),
        (w_out, m_w_out, v_w_out, parts_out, False),
        (ffn2_w1, m_ffn2_w1, v_ffn2_w1, parts_f2w1, True),
        (ffn2_w3, m_ffn2_w3, v_ffn2_w3, parts_f2w3, True),
        (ffn2_w2, m_ffn2_w2, v_ffn2_w2, parts_f2w2, False),
    )
    big_out = []
    for j, (w, m, v, parts, t) in enumerate(big):
        view = (lambda a: a[0].T) if t else (lambda a: a[0])
        back = (lambda a: a.T[None]) if t else (lambda a: a[None])
        big_out.append([back(a) for a in _adamw_big(view(w), parts, view(m), view(v), "adamw_%d" % j)])

    small_w = (ffn1_norm, mix_norm, ffn2_norm, fnorm, pool_scale, ret_norm, pw3.reshape(-1, GROUP))
    small_m = (m_ffn1_norm, m_mix_norm, m_ffn2_norm, m_final_norm.reshape(1, d), m_pool_scale, m_ret_norm, m_pool_w.reshape(-1, GROUP))
    small_v = (v_ffn1_norm, v_mix_norm, v_ffn2_norm, v_final_norm.reshape(1, d), v_pool_scale, v_ret_norm, v_pool_w.reshape(-1, GROUP))
    res = _adamw_small(stats_all, pw_all, small_w, small_m, small_v, d, pwid)
    loss = res[0][0, 0]
    small_out = [list(res[1 + 4 * j:5 + 4 * j]) for j in range(len(small_w))]
    small_out[3] = [a.reshape(d) for a in small_out[3]]
    small_out[6] = [a.reshape(pool_w.shape) for a in small_out[6]]

    order = [small_out[0], big_out[0], big_out[1], big_out[2], small_out[1], big_out[3], small_out[6], small_out[4],
             small_out[5], big_out[4], small_out[2], big_out[5], big_out[6], big_out[7], small_out[3]]
    result = [loss, dx[None]]
    for kind in range(4):
        result += [t[kind] for t in order]
    return tuple(result)
```

```python
import functools

import numpy as np
import jax
import jax.numpy as jnp
from jax import lax
from jax.experimental import pallas as pl
from jax.experimental.pallas import tpu as pltpu

F32 = jnp.float32
BF16 = jnp.bfloat16

NDEV = 8
NCHIP = 4
EPS = 1e-6
N_POOL_GROUPS = 4
POOL_WINDOWS = (2, 4, 8, 16)
MAX_WINDOW = 16
GROUP = 128
RET_HEADS = 4
ROPE_BASE = 10000.0
ADAM_LR = 0.001
ADAM_B1 = 0.9
ADAM_B2 = 0.999
ADAM_EPS = 1e-08
ADAM_WD = 0.01
ADAM_STEP = 10

VMEM_LIMIT = 56 * 1024 * 1024
FFN_CHUNK = 256
ROW_BAND = 32

NT = (((1,), (1,)), ((), ()))
NN = (((1,), (0,)), ((), ()))
TN = (((0,), (0,)), ((), ()))

ANY = pl.BlockSpec(memory_space=pl.ANY)


def _dot(a, b, dims):
    return lax.dot_general(a, b, dims, preferred_element_type=F32)


def _call(body, **kw):
    return pl.pallas_call(body, **kw)


def _params(**kw):
    return pltpu.CompilerParams(vmem_limit_bytes=VMEM_LIMIT, **kw)


def _seq(n):
    return _params(dimension_semantics=("arbitrary",) * n)


def _peer(k):
    x, y, c = lax.axis_index("x"), lax.axis_index("y"), lax.axis_index("c")
    return (1 - x if k & 4 else x, 1 - y if k & 2 else y, 1 - c if k & 1 else c)


def _flat(pos):
    return 4 * pos[0] + 2 * pos[1] + pos[2]


def _chip(pos):
    return 2 * pos[0] + pos[1]


def _row_tile(rows, cap):
    return max(t for t in range(16, min(rows, cap) + 1, 16) if rows % t == 0)


def _pieces(rows, n):
    tiles = rows // 16
    cuts = [16 * (tiles * q // n) for q in range(n + 1)]
    return [(a, b - a) for a, b in zip(cuts[:-1], cuts[1:])]


def _load_weights(parts, sems):
    copies = []
    for buf, dst in parts:
        rows = buf.shape[1]
        for p in range(NDEV):
            cp = pltpu.make_async_copy(buf.at[p], dst.at[pl.ds(p * rows, rows), :], sems.at[len(copies)])
            cp.start()
            copies.append(cp)
    return copies


def _sigmoid(a):
    return 1.0 / (1.0 + jnp.exp(-a))


def _remote(src, dst, send_sem, recv_sem, to):
    return pltpu.make_async_remote_copy(
        src_ref=src, dst_ref=dst, send_sem=send_sem, recv_sem=recv_sem, device_id=to, device_id_type=pl.DeviceIdType.MESH
    )


class _Gather:
    X, Y, FAR = 4, 2, 6
    COPIES = 8

    def __init__(self, shards):
        n = len(shards)
        self.operands = list(shards)
        self.out_shape = [jax.ShapeDtypeStruct((NDEV,) + a.shape, a.dtype) for a in shards]
        self.sems = [
            pltpu.SemaphoreType.DMA((self.COPIES * n,)), pltpu.SemaphoreType.DMA((self.COPIES * n,)),
            pltpu.SemaphoreType.DMA((n,)),
        ]
        self.stages = [self.begin, self.relay, self.relay_far, self.end]

    def _copy(self, t, k, block, to, ins, outs, sems, own=False, half=None):
        rows = outs[t].shape[1]
        part = pl.ds(0, rows) if half is None else pl.ds(half * (rows // 2), rows // 2)
        dst = outs[t].at[_flat(block), part, :]
        at = self.COPIES * t + k
        return _remote(ins[t] if own else dst, dst, sems[0].at[at], sems[1].at[at], to)

    def _local(self, t, ins, outs, sems):
        return pltpu.make_async_copy(ins[t], outs[t].at[_flat(_peer(0))], sems[2].at[t])

    def begin(self, ins, outs, sems):
        me = _peer(0)
        for t in range(len(ins)):
            self._local(t, ins, outs, sems).start()
            for k, code in enumerate((1, self.X, self.Y)):
                self._copy(t, k, me, _peer(code), ins, outs, sems, own=True).start()

    def relay(self, ins, outs, sems):
        me, sibling = _peer(0), _peer(1)
        for t in range(len(ins)):
            self._copy(t, 1, _peer(self.X), me, ins, outs, sems).wait_recv()
            self._copy(t, 3, _peer(self.X), _peer(self.Y), ins, outs, sems, half=0).start()
            self._copy(t, 5, _peer(self.X), sibling, ins, outs, sems).start()
            self._copy(t, 2, _peer(self.Y), me, ins, outs, sems).wait_recv()
            self._copy(t, 4, _peer(self.Y), _peer(self.X), ins, outs, sems, half=1).start()
            self._copy(t, 6, _peer(self.Y), sibling, ins, outs, sems).start()

    def relay_far(self, ins, outs, sems):
        me, sibling = _peer(0), _peer(1)
        for t in range(len(ins)):
            self._copy(t, 3, _peer(self.FAR), me, ins, outs, sems, half=0).wait_recv()
            self._copy(t, 4, _peer(self.FAR), me, ins, outs, sems, half=1).wait_recv()
            self._copy(t, 7, _peer(self.FAR), sibling, ins, outs, sems).start()

    def end(self, ins, outs, sems):
        me = _peer(0)
        for t in range(len(ins)):
            self._copy(t, 0, _peer(1), me, ins, outs, sems).wait_recv()
            for k, code in ((5, self.X), (6, self.Y), (7, self.FAR)):
                self._copy(t, k, _peer(code ^ 1), me, ins, outs, sems).wait_recv()
            for k in range(self.COPIES):
                self._copy(t, k, me, me, ins, outs, sems, half=0 if k == 3 else 1 if k == 4 else None).wait_send()
            self._local(t, ins, outs, sems).wait()


class _GatherDirect:
    def __init__(self, arrays):
        n = len(arrays)
        self.operands = list(arrays)
        self.out_shape = [jax.ShapeDtypeStruct((NDEV,) + a.shape, a.dtype) for a in arrays]
        self.sems = [pltpu.SemaphoreType.DMA((7 * n,)), pltpu.SemaphoreType.DMA((7 * n,)), pltpu.SemaphoreType.DMA((n,))]
        self.stages = [self.begin, self.end]

    def begin(self, ins, outs, sems):
        mine = _flat(_peer(0))
        for t in range(len(ins)):
            pltpu.make_async_copy(ins[t], outs[t].at[mine], sems[2].at[t]).start()
            for k in range(1, NDEV):
                _remote(ins[t], outs[t].at[mine], sems[0].at[7 * t + k - 1], sems[1].at[7 * t + k - 1], _peer(k)).start()

    def end(self, ins, outs, sems):
        mine = _flat(_peer(0))
        for t in range(len(ins)):
            for k in range(1, NDEV):
                cp = _remote(ins[t], outs[t].at[_flat(_peer(k))], sems[0].at[7 * t + k - 1], sems[1].at[7 * t + k - 1], _peer(k))
                cp.wait_recv()
                cp.wait_send()
            pltpu.make_async_copy(ins[t], outs[t].at[mine], sems[2].at[t]).wait()


class _ChipScatter:
    pieces = 2

    def __init__(self, sums):
        n = len(sums) * NCHIP * self.pieces
        self.operands = list(sums)
        self.out_shape = [jax.ShapeDtypeStruct(a.shape, a.dtype) for a in sums]
        self.sems = [pltpu.SemaphoreType.DMA((n,)), pltpu.SemaphoreType.DMA((n,))]
        self.stages = [self.begin, self.end]

    def _copies(self, ins, outs, sems, arriving):
        mine = _chip(_peer(0))
        copies = []
        for t in range(len(ins)):
            rows = ins[t].shape[1] // self.pieces
            for k in (0, 4, 2, 6):
                other = _chip(_peer(k))
                for q in range(self.pieces):
                    part = pl.ds(q * rows, rows)
                    at = len(copies)
                    if k == 0:
                        cp = pltpu.make_async_copy(ins[t].at[mine, part, :], outs[t].at[mine, part, :], sems[0].at[at])
                    else:
                        landing = outs[t].at[other if arriving else mine, part, :]
                        cp = _remote(ins[t].at[other, part, :], landing, sems[0].at[at], sems[1].at[at], _peer(k))
                    copies.append(cp)
        return copies

    def begin(self, ins, outs, sems):
        for cp in self._copies(ins, outs, sems, False):
            cp.start()

    def end(self, ins, outs, sems):
        for at, cp in enumerate(self._copies(ins, outs, sems, True)):
            if at % (NCHIP * self.pieces) < self.pieces:
                cp.wait()
            else:
                cp.wait_recv()
                cp.wait_send()


def _split_refs(refs, counts):
    out, at = [], 0
    for n in counts:
        out.append(refs[at:at + n])
        at += n
    return out


def _comm_call(carries, name):
    nin = [len(c.operands) for c in carries]
    nout = [len(c.out_shape) for c in carries]
    nsem = [len(c.sems) for c in carries]

    def body(*refs):
        ins, outs, sems = _split_refs(refs, (sum(nin), sum(nout), sum(nsem)))
        parts = list(zip(carries, _split_refs(ins, nin), _split_refs(outs, nout), _split_refs(sems, nsem)))
        for depth in range(max(len(c.stages) for c in carries)):
            for c, i, o, s in parts:
                if depth < len(c.stages) - 1:
                    c.stages[depth](i, o, s)
        for c, i, o, s in parts:
            c.stages[-1](i, o, s)

    res = _call(
        body,
        name=name,
        out_shape=[sh for c in carries for sh in c.out_shape],
        in_specs=[ANY] * sum(nin),
        out_specs=[ANY] * sum(nout),
        scratch_shapes=[sm for c in carries for sm in c.sems],
        compiler_params=pltpu.CompilerParams(has_side_effects=True),
    )(*[a for c in carries for a in c.operands])
    return _split_refs(list(res), nout)


def _grid_call(body, carries, *, name, grid, in_specs, out_specs, out_shape, scratch_shapes, args):
    ni, no, ns = len(in_specs), len(out_specs), len(scratch_shapes)
    nin = [len(c.operands) for c in carries]
    nout = [len(c.out_shape) for c in carries]
    nsem = [len(c.sems) for c in carries]
    steps = int(np.prod(grid))

    def when_of(stage, count):
        first, last = (5 * steps) // 8 - 1, steps - 2
        return max(0, last if count <= 3 else first + (last - first) * (stage - 1) // (count - 3))

    def wrapped(*refs):
        ins, cins, outs, couts, scr, csems = _split_refs(refs, (ni, sum(nin), no, sum(nout), ns, sum(nsem)))
        if not carries:
            return body(*ins, *outs, *scr)
        parts = list(zip(carries, _split_refs(cins, nin), _split_refs(couts, nout), _split_refs(csems, nsem)))
        step = pl.program_id(0)
        for axis in range(1, len(grid)):
            step = step * grid[axis] + pl.program_id(axis)

        @pl.when(step == 0)
        def _():
            for c, i, o, s in parts:
                c.stages[0](i, o, s)

        body(*ins, *outs, *scr)

        for c, i, o, s in parts:
            for stage in range(1, len(c.stages) - 1):
                pl.when(step == when_of(stage, len(c.stages)))(functools.partial(c.stages[stage], i, o, s))

        @pl.when(step == steps - 1)
        def _():
            for c, i, o, s in parts:
                c.stages[-1](i, o, s)

    res = _call(
        wrapped,
        name=name,
        grid=tuple(grid),
        in_specs=list(in_specs) + [ANY] * sum(nin),
        out_specs=list(out_specs) + [ANY] * sum(nout),
        out_shape=list(out_shape) + [sh for c in carries for sh in c.out_shape],
        scratch_shapes=list(scratch_shapes) + [sm for c in carries for sm in c.sems],
        compiler_params=_seq(len(grid)),
    )(*args, *[a for c in carries for a in c.operands])
    res = list(res)
    return res[:no], _split_refs(res[no:], nout)


def _chunks(width):
    return [(at, min(FFN_CHUNK, width - at)) for at in range(0, width, FFN_CHUNK)]


def _ffn_fwd(x, gain, weights, ffn, head=None, carries=()):
    s, d = x.shape
    tm = min(512, s)

    def body(*refs):
        if head is None:
            x_ref, g_ref, b1, b3, b2, h_ref, a_ref, b_ref, hm_ref, w1s, w3s, w2s, sems = refs
        else:
            x_ref, g_ref, b1, b3, b2, gf_ref, t_ref, h_ref, a_ref, b_ref, hm_ref, dgf_ref, loss_ref, w1s, w3s, w2s, sems = refs
        i = pl.program_id(0)

        @pl.when(i == 0)
        def _():
            for cp in _load_weights(((b1, w1s), (b3, w3s), (b2, w2s)), sems):
                cp.wait()
            if head is not None:
                dgf_ref[...] = jnp.zeros_like(dgf_ref)
                loss_ref[...] = jnp.zeros_like(loss_ref)

        xv = x_ref[...]
        r = lax.rsqrt(jnp.mean(xv * xv, axis=-1, keepdims=True) + EPS)
        n = (xv * r * g_ref[...]).astype(BF16)
        acc = jnp.zeros((tm, d), F32)
        for at, width in _chunks(ffn):
            cols = slice(at, at + width)
            a = _dot(n, w1s[cols, :], NT)
            b = _dot(n, w3s[cols, :], NT)
            a_ref[:, cols] = a.astype(BF16)
            b_ref[:, cols] = b.astype(BF16)
            hm = (a * _sigmoid(a) * b).astype(BF16)
            hm_ref[:, cols] = hm
            acc = acc + _dot(hm, w2s[cols, :], NN)
        h = xv + 0.5 * acc
        if head is None:
            h_ref[...] = h
        else:
            rf = lax.rsqrt(jnp.mean(h * h, axis=-1, keepdims=True) + EPS)
            nh = h * rf
            gf = gf_ref[...]
            err = nh * gf - t_ref[...]
            loss_ref[...] += jnp.sum(err * err, axis=0, keepdims=True) * (0.5 / d)
            dy = err * (1.0 / d)
            dgf_ref[...] += jnp.sum(dy * nh, axis=0, keepdims=True)
            dn = dy * gf
            h_ref[...] = rf * (dn - nh * jnp.mean(dn * nh, axis=-1, keepdims=True))

    tile = pl.BlockSpec((tm, d), lambda i: (i, 0))
    row = pl.BlockSpec((1, d), lambda i: (0, 0))
    wide = pl.BlockSpec((tm, ffn), lambda i: (i, 0))
    in_specs = [tile, row, ANY, ANY, ANY]
    out_shape = [jax.ShapeDtypeStruct((s, d), F32)] + [jax.ShapeDtypeStruct((s, ffn), BF16)] * 3
    out_specs = [tile, wide, wide, wide]
    args = [x, gain] + list(weights)
    if head is not None:
        in_specs += [row, tile]
        args += list(head)
        out_shape += [jax.ShapeDtypeStruct((1, d), F32)] * 2
        out_specs += [row, row]
    return _grid_call(
        body,
        carries,
        name="ffn_fwd_loss" if head is not None else "ffn_fwd",
        grid=(s // tm,),
        in_specs=in_specs,
        out_specs=out_specs,
        out_shape=out_shape,
        scratch_shapes=[pltpu.VMEM((ffn, d), BF16)] * 3 + [pltpu.SemaphoreType.DMA((3 * NDEV,))],
        args=args,
    )


def _ffn_bwd(dh, x, a, b, gain, weights, ffn, name, carries=()):
    s, d = x.shape
    tm = min(512, s)
    halves = 2
    fh = ffn // halves

    def body(dh_ref, x_ref, a_ref, b_ref, g_ref, b1, b3, b2, dx_ref, da_ref, db_ref, n_ref, dg_ref, w1s, w3s, w2s, sems):
        i, j = pl.program_id(0), pl.program_id(1)

        @pl.when((i == 0) & (j == 0))
        def _():
            for cp in _load_weights(((b1, w1s), (b3, w3s), (b2, w2s)), sems):
                cp.wait()
            dg_ref[...] = jnp.zeros_like(dg_ref)

        @pl.when(j == 0)
        def _():
            dx_ref[...] = jnp.zeros_like(dx_ref)

        dob = (0.5 * dh_ref[...]).astype(BF16)
        chunks = _chunks(fh)

        def dhm_of(k):
            at, width = chunks[k]
            return _dot(dob, w2s[pl.ds(pl.multiple_of(j * fh + at, GROUP), width), :], NT)

        ahead = dhm_of(0)
        for k, (at, width) in enumerate(chunks):
            cols = slice(at, at + width)
            dhm = ahead
            if k + 1 < len(chunks):
                ahead = dhm_of(k + 1)
            for top in range(0, tm, ROW_BAND):
                band = slice(top, top + ROW_BAND)
                av = a_ref[band, cols].astype(F32)
                bv = b_ref[band, cols].astype(F32)
                sg = _sigmoid(av)
                dv = dhm[band]
                da_ref[band, cols] = (dv * bv * (sg * (1.0 + av * (1.0 - sg)))).astype(BF16)
                db_ref[band, cols] = (dv * (av * sg)).astype(BF16)
        half = pl.ds(pl.multiple_of(j * fh, GROUP), fh)
        dx_ref[...] += _dot(da_ref[...], w1s[half, :], NN) + _dot(db_ref[...], w3s[half, :], NN)

        @pl.when(j == halves - 1)
        def _():
            xv = x_ref[...]
            g = g_ref[...]
            r = lax.rsqrt(jnp.mean(xv * xv, axis=-1, keepdims=True) + EPS)
            nh = xv * r
            n_ref[...] = (nh * g).astype(BF16)
            total = dx_ref[...]
            dg_ref[...] += jnp.sum(total * nh, axis=0, keepdims=True)
            dnh = total * g
            dx_ref[...] = dh_ref[...] + r * (dnh - nh * jnp.mean(dnh * nh, axis=-1, keepdims=True))

    tile = pl.BlockSpec((tm, d), lambda i, j: (i, 0))
    row = pl.BlockSpec((1, d), lambda i, j: (0, 0))
    wide = pl.BlockSpec((tm, fh), lambda i, j: (i, j))
    return _grid_call(
        body,
        carries,
        name=name,
        grid=(s // tm, halves),
        in_specs=[tile, tile, wide, wide, row, ANY, ANY, ANY],
        out_specs=[tile, wide, wide, tile, row],
        out_shape=[
            jax.ShapeDtypeStruct((s, d), F32),
            jax.ShapeDtypeStruct((s, ffn), BF16),
            jax.ShapeDtypeStruct((s, ffn), BF16),
            jax.ShapeDtypeStruct((s, d), BF16),
            jax.ShapeDtypeStruct((1, d), F32),
        ],
        scratch_shapes=[pltpu.VMEM((ffn, d), BF16)] * 3 + [pltpu.SemaphoreType.DMA((3 * NDEV,))],
        args=[dh, x, a, b, gain] + list(weights),
    )


SWAP_PIECES = 2


def _wgrad(lhs, rhs, scale, name, carries=()):
    s, m = lhs.shape
    n = rhs.shape[1]
    rs = m // NDEV
    tk = min(1024, s)
    steps = s // tk
    pieces = [(j, at, size) for j in range(2) for at, size in _pieces(rs, SWAP_PIECES)]

    def body(l_ref, r_ref, o_ref, acc, mine, theirs, send_sems, recv_sems):
        h, k = pl.program_id(0), pl.program_id(1)

        @pl.when(k == 0)
        def _():
            acc[...] = _dot(l_ref[...], r_ref[...].astype(BF16), TN)

        @pl.when(k > 0)
        def _():
            acc[...] += _dot(l_ref[...], r_ref[...].astype(BF16), TN)

        def exchange(half):
            c = lax.axis_index("c")
            return [
                _remote(mine.at[half, 1 - c, j, pl.ds(at, size), :], theirs.at[half, j, pl.ds(at, size), :],
                        send_sems.at[half * len(pieces) + q], recv_sems.at[half * len(pieces) + q], _peer(1))
                for q, (j, at, size) in enumerate(pieces)
            ]

        def settle(half):
            for cp in exchange(half):
                cp.wait_recv()
            both = mine[half, lax.axis_index("c")].astype(F32) + theirs[half].astype(F32)
            o_ref[2 * half:2 * half + 2] = both.astype(BF16)
            for cp in exchange(half):
                cp.wait_send()

        for half in range(2):
            @pl.when((h == half) & (k == steps - 1))
            def _():
                for p in range(NCHIP):
                    mine[half, p % 2, p // 2] = (acc[p * rs:(p + 1) * rs, :] * scale).astype(BF16)
                for cp in exchange(half):
                    cp.start()
                if half == 1:
                    settle(0)
                    settle(1)

    (out,), carried = _grid_call(
        body,
        carries,
        name=name,
        grid=(2, steps),
        in_specs=[pl.BlockSpec((tk, m // 2), lambda h, k: (k, h)), pl.BlockSpec((tk, n), lambda h, k: (k, 0))],
        out_specs=[pl.BlockSpec((NCHIP, rs, n), lambda h, k: (0, 0, 0))],
        out_shape=[jax.ShapeDtypeStruct((NCHIP, rs, n), BF16)],
        scratch_shapes=[
            pltpu.VMEM((m // 2, n), F32), pltpu.VMEM((2, 2, 2, rs, n), BF16), pltpu.VMEM((2, 2, rs, n), BF16),
            pltpu.SemaphoreType.DMA((2 * len(pieces),)), pltpu.SemaphoreType.DMA((2 * len(pieces),)),
        ],
        args=[lhs, rhs],
    )
    return out, carried


def _mix_constants(s):
    c = GROUP
    lg = np.log1p(-np.exp2(-5.0 - np.arange(RET_HEADS, dtype=np.float32))).astype(np.float32)
    pos = np.arange(c, dtype=np.float32)
    rel = pos[:, None] - pos[None, :]
    decay = np.where(rel[None] >= 0, np.exp(lg[:, None, None] * np.maximum(rel, 0.0)[None]), 0.0).astype(np.float32)
    ktail = np.exp(lg[:, None] * (c - 1 - pos)[None, :]).astype(np.float32)
    qhead = np.exp(lg[:, None] * (pos + 1.0)[None, :]).astype(np.float32)
    chunk_decay = [float(v) for v in np.exp(lg * np.float32(c)).astype(np.float32)]
    ones = np.ones((1, 1, c), np.float32)
    inv_freq = (1.0 / (np.float32(ROPE_BASE) ** (np.arange(0, c, 2, dtype=np.float32) / np.float32(c)))).astype(np.float32)
    ang = (np.arange(s, dtype=np.float32)[:, None] * inv_freq[None, :]).astype(np.float32)
    cos, sin = np.cos(ang).astype(np.float32), np.sin(ang).astype(np.float32)
    return dict(
        decay=jnp.asarray(decay),
        ktail=jnp.asarray(ktail[:, :, None] * ones),
        qhead=jnp.asarray(qhead[:, :, None] * ones),
        chunk_decay=chunk_decay,
        cos=jnp.asarray(np.concatenate([cos, cos], axis=-1)),
        sin=jnp.asarray(np.concatenate([-sin, sin], axis=-1)),
    )


def _rope(t, cos, sin):
    return t * cos + pltpu.roll(t, GROUP // 2, axis=1) * sin


def _rope_bwd(dt, cos, sin):
    return dt * cos + pltpu.roll(dt * sin, GROUP // 2, axis=1)


def _window_sums(ext, w, forward):
    rows = ext.shape[0]
    acc, k = ext, 1
    while k < w:
        acc = acc + pltpu.roll(acc, k if forward else rows - k, axis=0)
        k *= 2
    return acc


def _pool_counts(tile, tm, w):
    t = lax.broadcasted_iota(jnp.int32, (tm, 1), 0) + tile * tm
    return jnp.minimum(t + 1, w).astype(F32)


def _mix_fwd(h1, gain, weights, pool_w, pool_scale, ret_gain, consts, carries=()):
    s, d = h1.shape
    pwid = N_POOL_GROUPS * GROUP
    rwid = RET_HEADS * GROUP
    inw = pwid + 4 * rwid
    tm = min(256, s)
    nck = tm // GROUP
    cd = consts["chunk_decay"]

    def body(h_ref, g_ref, bin_, bout, pw_ref, ps_ref, rg_ref, cos_ref, sin_ref, dec_ref, kt_ref, qh_ref,
             h2_ref, proj_ref, o_ref, rs_ref, wins, wouts, state, carry, mbuf, sems):
        i = pl.program_id(0)

        @pl.when(i == 0)
        def _():
            for cp in _load_weights(((bin_, wins), (bout, wouts)), sems):
                cp.wait()
            state[...] = jnp.zeros_like(state)
            carry[...] = jnp.zeros_like(carry)

        hv = h_ref[...]
        r = lax.rsqrt(jnp.mean(hv * hv, axis=-1, keepdims=True) + EPS)
        u = (hv * r * g_ref[...]).astype(BF16)
        proj_ref[...] = _dot(u, wins[...], NT)

        ext = jnp.concatenate([carry[...], proj_ref[:, 0:pwid]], axis=0)
        carry[...] = proj_ref[tm - MAX_WINDOW:tm, 0:pwid]
        for gi, w in enumerate(POOL_WINDOWS):
            cols = slice(gi * GROUP, (gi + 1) * GROUP)
            xg = ext[:, cols]
            ws = _window_sums(xg, w, True)[MAX_WINDOW:, :]
            pooled = ws / _pool_counts(i, tm, w) - xg[MAX_WINDOW:, :]
            z = _dot(pooled.astype(BF16), pw_ref[gi].astype(BF16), NN)
            mbuf[:, cols] = (z * ps_ref[:, cols]).astype(BF16)

        cos, sin = cos_ref[...], sin_ref[...]
        for h in range(RET_HEADS):
            cq = slice(pwid + h * GROUP, pwid + (h + 1) * GROUP)
            ck = slice(pwid + rwid + h * GROUP, pwid + rwid + (h + 1) * GROUP)
            cv = slice(pwid + 2 * rwid + h * GROUP, pwid + 2 * rwid + (h + 1) * GROUP)
            cg = slice(pwid + 3 * rwid + h * GROUP, pwid + 3 * rwid + (h + 1) * GROUP)
            ch = slice(h * GROUP, (h + 1) * GROUP)
            qr = _rope(proj_ref[:, cq], cos, sin)
            kr = _rope(proj_ref[:, ck], cos, sin) * (GROUP ** -0.5)
            vb = proj_ref[:, cv].astype(BF16)
            for n in range(nck):
                rows = slice(n * GROUP, (n + 1) * GROUP)
                qc, kc, vc = qr[rows], kr[rows], vb[rows]
                rb = state[h]
                rs_ref[n, h] = rb
                p = (_dot(qc.astype(BF16), kc.astype(BF16), NT) * dec_ref[h]).astype(BF16)
                o = _dot(p, vc, NN) + _dot((qc * qh_ref[h]).astype(BF16), rb.astype(BF16), NN)
                state[h] = cd[h] * rb + _dot((kc * kt_ref[h]).astype(BF16), vc, TN)
                o_ref[rows, ch] = o
                on = o * lax.rsqrt(jnp.mean(o * o, axis=-1, keepdims=True) + EPS)
                gv = proj_ref[rows, cg]
                mbuf[rows, pwid + h * GROUP:pwid + (h + 1) * GROUP] = (
                    gv * _sigmoid(gv) * (on * rg_ref[:, ch])
                ).astype(BF16)
        h2_ref[...] = hv + _dot(mbuf[...], wouts[...], NN)

    tile = pl.BlockSpec((tm, d), lambda i: (i, 0))
    full = lambda shape: pl.BlockSpec(shape, lambda i: (0,) * len(shape))
    return _grid_call(
        body,
        carries,
        name="mix_fwd",
        grid=(s // tm,),
        in_specs=[
            tile, full((1, d)), ANY, ANY,
            full((N_POOL_GROUPS, GROUP, GROUP)), full((1, pwid)), full((1, rwid)),
            pl.BlockSpec((tm, GROUP), lambda i: (i, 0)), pl.BlockSpec((tm, GROUP), lambda i: (i, 0)),
            full((RET_HEADS, GROUP, GROUP)), full((RET_HEADS, GROUP, GROUP)), full((RET_HEADS, GROUP, GROUP)),
        ],
        out_specs=[
            tile,
            pl.BlockSpec((tm, inw), lambda i: (i, 0)),
            pl.BlockSpec((tm, rwid), lambda i: (i, 0)),
            pl.BlockSpec((nck, RET_HEADS, GROUP, GROUP), lambda i: (i, 0, 0, 0)),
        ],
        out_shape=[
            jax.ShapeDtypeStruct((s, d), F32),
            jax.ShapeDtypeStruct((s, inw), F32),
            jax.ShapeDtypeStruct((s, rwid), F32),
            jax.ShapeDtypeStruct((s // GROUP, RET_HEADS, GROUP, GROUP), F32),
        ],
        scratch_shapes=[
            pltpu.VMEM((inw, d), BF16), pltpu.VMEM((d, d), BF16),
            pltpu.VMEM((RET_HEADS, GROUP, GROUP), F32), pltpu.VMEM((MAX_WINDOW, pwid), F32),
            pltpu.VMEM((tm, d), BF16), pltpu.SemaphoreType.DMA((2 * NDEV,)),
        ],
        args=[h1, gain, weights[0], weights[1], pool_w, pool_scale, ret_gain,
              consts["cos"], consts["sin"], consts["decay"], consts["ktail"], consts["qhead"]],
    )


def _mix_bwd(dh2, h1, proj, o_saved, rsave, gain, weights, pool_w, pool_scale, ret_gain, consts, carries=()):
    s, d = h1.shape
    pwid = N_POOL_GROUPS * GROUP
    rwid = RET_HEADS * GROUP
    inw = pwid + 4 * rwid
    tm = min(256, s)
    nck = tm // GROUP
    nt = s // tm
    cd = consts["chunk_decay"]
    halo_per_tile = tm // MAX_WINDOW

    def body(dh2_ref, h_ref, proj_ref, halo_ref, o_ref, rs_ref, g_ref, bin_, bout, pw_ref, ps_ref, rg_ref,
             cos_ref, sin_ref, dec_ref, kt_ref, qh_ref,
             dh1_ref, dproj_ref, u_ref, m_ref, dpw_ref, dps_ref, drg_ref, dg_ref,
             wins, wouts, dstate, carry, dm, dpj, sems):
        i = pl.program_id(0)
        tile = nt - 1 - i

        @pl.when(i == 0)
        def _():
            for cp in _load_weights(((bin_, wins), (bout, wouts)), sems):
                cp.wait()
            dstate[...] = jnp.zeros_like(dstate)
            carry[...] = jnp.zeros_like(carry)
            for ref in (dpw_ref, dps_ref, drg_ref, dg_ref):
                ref[...] = jnp.zeros_like(ref)

        dh2v = dh2_ref[...]
        dm[...] = _dot(dh2v.astype(BF16), wouts[...], NT)
        hv = h_ref[...]
        g = g_ref[...]
        r = lax.rsqrt(jnp.mean(hv * hv, axis=-1, keepdims=True) + EPS)
        uh = hv * r
        u_ref[...] = (uh * g).astype(BF16)

        halo = jnp.where(tile == 0, 0.0, halo_ref[...])
        ext = jnp.concatenate([halo, proj_ref[:, 0:pwid]], axis=0)
        next_dpn = carry[...]
        for gi, w in enumerate(POOL_WINDOWS):
            cols = slice(gi * GROUP, (gi + 1) * GROUP)
            xg = ext[:, cols]
            cnt = _pool_counts(tile, tm, w)
            pooled = (_window_sums(xg, w, True)[MAX_WINDOW:, :] / cnt - xg[MAX_WINDOW:, :]).astype(BF16)
            pwb = pw_ref[gi].astype(BF16)
            z = _dot(pooled, pwb, NN)
            scale = ps_ref[:, cols]
            m_ref[:, cols] = (z * scale).astype(BF16)
            da = dm[:, cols]
            dps_ref[:, cols] += jnp.sum(da * z, axis=0, keepdims=True)
            dz = (da * scale).astype(BF16)
            dpw_ref[gi] += _dot(pooled, dz, TN)
            dpl = _dot(dz, pwb, NT)
            dpn = dpl / cnt
            ext2 = jnp.concatenate([dpn, next_dpn[:, cols]], axis=0)
            dpj[:, cols] = (_window_sums(ext2, w, False)[0:tm, :] - dpl).astype(BF16)
            carry[:, cols] = dpn[0:MAX_WINDOW, :]

        cos, sin = cos_ref[...], sin_ref[...]
        for h in range(RET_HEADS):
            cq = slice(pwid + h * GROUP, pwid + (h + 1) * GROUP)
            ck = slice(pwid + rwid + h * GROUP, pwid + rwid + (h + 1) * GROUP)
            cv = slice(pwid + 2 * rwid + h * GROUP, pwid + 2 * rwid + (h + 1) * GROUP)
            cg = slice(pwid + 3 * rwid + h * GROUP, pwid + 3 * rwid + (h + 1) * GROUP)
            ch = slice(h * GROUP, (h + 1) * GROUP)
            qr = _rope(proj_ref[:, cq], cos, sin)
            kr = _rope(proj_ref[:, ck], cos, sin) * (GROUP ** -0.5)
            vb = proj_ref[:, cv].astype(BF16)
            gv = proj_ref[:, cg]
            ov = o_ref[:, ch]
            ro = lax.rsqrt(jnp.mean(ov * ov, axis=-1, keepdims=True) + EPS)
            on = ov * ro
            rg = rg_ref[:, ch]
            db = dm[:, pwid + h * GROUP:pwid + (h + 1) * GROUP]
            sg = _sigmoid(gv)
            sl = gv * sg
            m_ref[:, pwid + h * GROUP:pwid + (h + 1) * GROUP] = (sl * (on * rg)).astype(BF16)
            dpj[:, cg] = (db * (on * rg) * (sg * (1.0 + gv * (1.0 - sg)))).astype(BF16)
            drg_ref[:, ch] += jnp.sum(db * sl * on, axis=0, keepdims=True)
            don = db * sl * rg
            do = (ro * (don - on * jnp.mean(don * on, axis=-1, keepdims=True))).astype(BF16)
            for n in reversed(range(nck)):
                rows = slice(n * GROUP, (n + 1) * GROUP)
                qc, kc, vc, dob = qr[rows], kr[rows], vb[rows], do[rows]
                qcb, kcb = qc.astype(BF16), kc.astype(BF16)
                qh = (qc * qh_ref[h]).astype(BF16)
                kt = (kc * kt_ref[h]).astype(BF16)
                rn = rs_ref[n, h].astype(BF16)
                dnext = dstate[h]
                dnb = dnext.astype(BF16)
                dec = dec_ref[h]
                p = (_dot(qcb, kcb, NT) * dec).astype(BF16)
                ds = (_dot(dob, vc, NT) * dec).astype(BF16)
                dv = _dot(p, dob, TN) + _dot(kt, dnb, NN)
                dq = _dot(ds, kcb, NN) + _dot(dob, rn, NT) * qh_ref[h]
                dk = _dot(ds, qcb, TN) + _dot(vc, dnb, NT) * kt_ref[h]
                dstate[h] = cd[h] * dnext + _dot(qh, dob, TN)
                dpj[rows, cq] = _rope_bwd(dq, cos[rows], sin[rows]).astype(BF16)
                dpj[rows, ck] = _rope_bwd(dk * (GROUP ** -0.5), cos[rows], sin[rows]).astype(BF16)
                dpj[rows, cv] = dv.astype(BF16)

        dproj_ref[...] = dpj[...]
        du = _dot(dpj[...], wins[...], NN)
        dg_ref[...] += jnp.sum(du * uh, axis=0, keepdims=True)
        dn = du * g
        dh1_ref[...] = dh2v + r * (dn - uh * jnp.mean(dn * uh, axis=-1, keepdims=True))

    rev = lambda i: (nt - 1 - i, 0)
    tile = pl.BlockSpec((tm, d), rev)
    full = lambda shape: pl.BlockSpec(shape, lambda i: (0,) * len(shape))
    return _grid_call(
        body,
        carries,
        name="mix_bwd",
        grid=(nt,),
        in_specs=[
            tile, tile,
            pl.BlockSpec((tm, inw), rev),
            pl.BlockSpec((MAX_WINDOW, pwid), lambda i: (jnp.maximum((nt - 1 - i) * halo_per_tile - 1, 0), 0)),
            pl.BlockSpec((tm, rwid), rev),
            pl.BlockSpec((nck, RET_HEADS, GROUP, GROUP), lambda i: (nt - 1 - i, 0, 0, 0)),
            full((1, d)), ANY, ANY,
            full((N_POOL_GROUPS, GROUP, GROUP)), full((1, pwid)), full((1, rwid)),
            pl.BlockSpec((tm, GROUP), rev), pl.BlockSpec((tm, GROUP), rev),
            full((RET_HEADS, GROUP, GROUP)), full((RET_HEADS, GROUP, GROUP)), full((RET_HEADS, GROUP, GROUP)),
        ],
        out_specs=[
            tile, pl.BlockSpec((tm, inw), rev), tile, tile,
            full((N_POOL_GROUPS, GROUP, GROUP)), full((1, pwid)), full((1, rwid)), full((1, d)),
        ],
        out_shape=[
            jax.ShapeDtypeStruct((s, d), F32),
            jax.ShapeDtypeStruct((s, inw), BF16),
            jax.ShapeDtypeStruct((s, d), BF16),
            jax.ShapeDtypeStruct((s, d), BF16),
            jax.ShapeDtypeStruct((N_POOL_GROUPS, GROUP, GROUP), F32),
            jax.ShapeDtypeStruct((1, pwid), F32),
            jax.ShapeDtypeStruct((1, rwid), F32),
            jax.ShapeDtypeStruct((1, d), F32),
        ],
        scratch_shapes=[
            pltpu.VMEM((inw, d), BF16), pltpu.VMEM((d, d), BF16),
            pltpu.VMEM((RET_HEADS, GROUP, GROUP), F32), pltpu.VMEM((MAX_WINDOW, pwid), F32),
            pltpu.VMEM((tm, d), F32), pltpu.VMEM((tm, inw), BF16), pltpu.SemaphoreType.DMA((2 * NDEV,)),
        ],
        args=[dh2, h1, proj, proj, o_saved, rsave, gain, weights[0], weights[1], pool_w, pool_scale, ret_gain,
              consts["cos"], consts["sin"], consts["decay"], consts["ktail"], consts["qhead"]],
    )


def _adam(w, g, m, v):
    m = ADAM_B1 * m + (1.0 - ADAM_B1) * g
    v = ADAM_B2 * v + (1.0 - ADAM_B2) * jnp.square(g)
    m_hat = m / (1.0 - ADAM_B1 ** ADAM_STEP)
    v_hat = v / (1.0 - ADAM_B2 ** ADAM_STEP)
    delta = -ADAM_LR * (m_hat / (jnp.sqrt(v_hat) + ADAM_EPS) + ADAM_WD * w)
    return delta, m, v


def _adamw_big(w, parts, m, v, name):
    rows, d = w.shape
    tr = _row_tile(rows, 176)

    def body(w_ref, p_ref, m_ref, v_ref, g_ref, d_ref, nm_ref, nv_ref):
        g = p_ref[0].astype(F32)
        for q in range(1, NCHIP):
            g = g + p_ref[q].astype(F32)
        g_ref[...] = g
        d_ref[...], nm_ref[...], nv_ref[...] = _adam(w_ref[...], g, m_ref[...], v_ref[...])

    spec = pl.BlockSpec((tr, d), lambda i: (i, 0))
    return _call(
        body,
        name=name,
        grid=(rows // tr,),
        in_specs=[spec, pl.BlockSpec((NCHIP, tr, d), lambda i: (0, i, 0)), spec, spec],
        out_specs=[spec] * 4,
        out_shape=[jax.ShapeDtypeStruct((rows, d), F32)] * 4,
        compiler_params=_seq(1),
    )(w, parts, m, v)


def _adamw_small(stats_all, pw_all, ws, ms, vs, pwid):
    nsmall = len(ws)

    def body(*refs):
        st_ref, pwa_ref = refs[0], refs[1]
        w_refs = refs[2:2 + nsmall]
        m_refs = refs[2 + nsmall:2 + 2 * nsmall]
        v_refs = refs[2 + 2 * nsmall:2 + 3 * nsmall]
        outs = refs[2 + 3 * nsmall:]
        st = st_ref[0]
        pwg = pwa_ref[0]
        for q in range(1, NDEV):
            st = st + st_ref[q]
            pwg = pwg + pwa_ref[q]
        grads = [st[0:1, :], st[1:2, :], st[2:3, :], st[3:4, :], st[4:5, 0:pwid], st[4:5, pwid:2 * pwid], pwg]
        outs[0][...] = jnp.zeros((1, GROUP), F32) + jnp.sum(st[5:6, :])
        for j in range(nsmall):
            delta, nm, nv = _adam(w_refs[j][...], grads[j], m_refs[j][...], v_refs[j][...])
            outs[1 + 4 * j][...] = grads[j]
            outs[2 + 4 * j][...] = delta
            outs[3 + 4 * j][...] = nm
            outs[4 + 4 * j][...] = nv

    out_shape = [jax.ShapeDtypeStruct((1, GROUP), F32)]
    for w in ws:
        out_shape += [jax.ShapeDtypeStruct(w.shape, F32)] * 4
    return _call(body, name="adamw_small", out_shape=out_shape, compiler_params=_params())(
        stats_all, pw_all, *ws, *ms, *vs
    )


def kernel(x, ffn1_norm, ffn1_w1, ffn1_w3, ffn1_w2, mix_norm, w_in, pool_w, pool_scale, ret_norm, w_out, ffn2_norm, ffn2_w1, ffn2_w3, ffn2_w2, final_norm, loss_target, m_ffn1_norm, m_ffn1_w1, m_ffn1_w3, m_ffn1_w2, m_mix_norm, m_w_in, m_pool_w, m_pool_scale, m_ret_norm, m_w_out, m_ffn2_norm, m_ffn2_w1, m_ffn2_w3, m_ffn2_w2, m_final_norm, v_ffn1_norm, v_ffn1_w1, v_ffn1_w3, v_ffn1_w2, v_mix_norm, v_w_in, v_pool_w, v_pool_scale, v_ret_norm, v_w_out, v_ffn2_norm, v_ffn2_w1, v_ffn2_w3, v_ffn2_w2, v_final_norm):
    s, d = x.shape[1], x.shape[2]
    ffn = ffn1_w1.shape[2] * NDEV
    pwid = pool_scale.shape[1]
    xs, tgt = x[0], loss_target[0]
    consts = _mix_constants(s)
    pw3 = pool_w[0]
    fnorm = final_norm.reshape(1, d)

    rows_of = lambda w, transposed: (w[0].T if transposed else w[0]).astype(BF16)
    send_f1 = [rows_of(ffn1_w1, True), rows_of(ffn1_w3, True), rows_of(ffn1_w2, False)]
    send_mix = [rows_of(w_in, True), rows_of(w_out, False)]
    send_f2 = [rows_of(ffn2_w1, True), rows_of(ffn2_w3, True), rows_of(ffn2_w2, False)]

    (w_f1,) = _comm_call([_Gather(send_f1)], "gather_ffn1")
    (h1, a1, b1, hm1), (more,) = _ffn_fwd(xs, ffn1_norm, w_f1, ffn, carries=[_Gather(send_mix + send_f2[:1])])
    w_mix = more[:2]
    (h2, proj, o_saved, rsave), (rest,) = _mix_fwd(
        h1, mix_norm, w_mix, pw3, pool_scale, ret_norm, consts, carries=[_Gather(send_f2[1:])]
    )
    w_f2 = more[2:] + rest
    (dh3, a2, b2, hm2, dgf, loss_cols), _ = _ffn_fwd(h2, ffn2_norm, w_f2, ffn, head=(fnorm, tgt))

    (dh2, da2, db2, n2, dg2), _ = _ffn_bwd(dh3, h2, a2, b2, ffn2_norm, w_f2, ffn, "ffn2_bwd")
    sum_f2w1, _ = _wgrad(da2, n2, 1.0, "ffn2_w1_grad")
    sum_f2w3, _ = _wgrad(db2, n2, 1.0, "ffn2_w3_grad")
    sum_f2w2, ((parts_f2w1,),) = _wgrad(hm2, dh3, 0.5, "ffn2_w2_grad", carries=[_ChipScatter([sum_f2w1])])

    (dh1, dproj, u, mm, dpw, dps, drg, dgm), ((parts_f2w3, parts_f2w2),) = _mix_bwd(
        dh2, h1, proj, o_saved, rsave, mix_norm, w_mix, pw3, pool_scale, ret_norm, consts,
        carries=[_ChipScatter([sum_f2w3, sum_f2w2])],
    )
    (dx, da1, db1, n1, dg1), _ = _ffn_bwd(dh1, xs, a1, b1, ffn1_norm, w_f1, ffn, "ffn1_bwd")
    stats = jnp.concatenate(
        [dg1, dgm, dg2, dgf, jnp.concatenate([dps, drg], axis=1), loss_cols, jnp.zeros((2, d), F32)], axis=0
    )
    small = _GatherDirect([stats, dpw.reshape(N_POOL_GROUPS * GROUP, GROUP)])
    sum_f1w2, ((stats_all, pw_all),) = _wgrad(hm1, dh1, 0.5, "ffn1_w2_grad", carries=[small])
    sum_f1w1, ((parts_f1w2,),) = _wgrad(da1, n1, 1.0, "ffn1_w1_grad", carries=[_ChipScatter([sum_f1w2])])
    sum_f1w3, ((parts_f1w1,),) = _wgrad(db1, n1, 1.0, "ffn1_w3_grad", carries=[_ChipScatter([sum_f1w1])])
    sum_in, ((parts_f1w3,),) = _wgrad(dproj, u, 1.0, "w_in_grad", carries=[_ChipScatter([sum_f1w3])])
    sum_out, ((parts_in,),) = _wgrad(mm, dh2, 1.0, "w_out_grad", carries=[_ChipScatter([sum_in])])
    ((parts_out,),) = _comm_call([_ChipScatter([sum_out])], "scatter_last")

    big = (
        (ffn1_w1, m_ffn1_w1, v_ffn1_w1, parts_f1w1, True),
        (ffn1_w3, m_ffn1_w3, v_ffn1_w3, parts_f1w3, True),
        (ffn1_w2, m_ffn1_w2, v_ffn1_w2, parts_f1w2, False),
        (w_in, m_w_in, v_w_in, parts_in, True),
        (w_out, m_w_out, v_w_out, parts_out, False),
        (ffn2_w1, m_ffn2_w1, v_ffn2_w1, parts_f2w1, True),
        (ffn2_w3, m_ffn2_w3, v_ffn2_w3, parts_f2w3, True),
        (ffn2_w2, m_ffn2_w2, v_ffn2_w2, parts_f2w2, False),
    )
    big_out = []
    for j, (w, m, v, parts, t) in enumerate(big):
        view = (lambda a: a[0].T) if t else (lambda a: a[0])
        back = (lambda a: a.T[None]) if t else (lambda a: a[None])
        big_out.append([back(a) for a in _adamw_big(view(w), parts, view(m), view(v), "adamw_%d" % j)])

    small_w = (ffn1_norm, mix_norm, ffn2_norm, fnorm, pool_scale, ret_norm, pw3.reshape(-1, GROUP))
    small_m = (m_ffn1_norm, m_mix_norm, m_ffn2_norm, m_final_norm.reshape(1, d), m_pool_scale, m_ret_norm, m_pool_w.reshape(-1, GROUP))
    small_v = (v_ffn1_norm, v_mix_norm, v_ffn2_norm, v_final_norm.reshape(1, d), v_pool_scale, v_ret_norm, v_pool_w.reshape(-1, GROUP))
    res = _adamw_small(stats_all, pw_all, small_w, small_m, small_v, pwid)
    loss = res[0][0, 0]
    small_out = [list(res[1 + 4 * j:5 + 4 * j]) for j in range(len(small_w))]
    small_out[3] = [a.reshape(d) for a in small_out[3]]
    small_out[6] = [a.reshape(pool_w.shape) for a in small_out[6]]

    order = [small_out[0], big_out[0], big_out[1], big_out[2], small_out[1], big_out[3], small_out[6], small_out[4],
             small_out[5], big_out[4], small_out[2], big_out[5], big_out[6], big_out[7], small_out[3]]
    result = [loss, dx[None]]
    for kind in range(4):
        result += [t[kind] for t in order]
    return tuple(result)
```

```python
import functools

import numpy as np
import jax
import jax.numpy as jnp
from jax import lax
from jax.experimental import pallas as pl
from jax.experimental.pallas import tpu as pltpu

F32 = jnp.float32
BF16 = jnp.bfloat16

NDEV = 8
NCHIP = 4
EPS = 1e-6
N_POOL_GROUPS = 4
POOL_WINDOWS = (2, 4, 8, 16)
MAX_WINDOW = 16
GROUP = 128
RET_HEADS = 4
ROPE_BASE = 10000.0
ADAM_LR = 0.001
ADAM_B1 = 0.9
ADAM_B2 = 0.999
ADAM_EPS = 1e-08
ADAM_WD = 0.01
ADAM_STEP = 10

VMEM_LIMIT = 56 * 1024 * 1024
FFN_CHUNK = 256
ROW_BAND = 32

NT = (((1,), (1,)), ((), ()))
NN = (((1,), (0,)), ((), ()))
TN = (((0,), (0,)), ((), ()))

ANY = pl.BlockSpec(memory_space=pl.ANY)


def _dot(a, b, dims):
    return lax.dot_general(a, b, dims, preferred_element_type=F32)


def _call(body, **kw):
    return pl.pallas_call(body, **kw)


def _params(**kw):
    return pltpu.CompilerParams(vmem_limit_bytes=VMEM_LIMIT, **kw)


def _seq(n):
    return _params(dimension_semantics=("arbitrary",) * n)


def _peer(k):
    x, y, c = lax.axis_index("x"), lax.axis_index("y"), lax.axis_index("c")
    return (1 - x if k & 4 else x, 1 - y if k & 2 else y, 1 - c if k & 1 else c)


def _flat(pos):
    return 4 * pos[0] + 2 * pos[1] + pos[2]


def _chip(pos):
    return 2 * pos[0] + pos[1]


def _row_tile(rows, cap):
    return max(t for t in range(16, min(rows, cap) + 1, 16) if rows % t == 0)


def _pieces(rows, n):
    tiles = rows // 16
    cuts = [16 * (tiles * q // n) for q in range(n + 1)]
    return [(a, b - a) for a, b in zip(cuts[:-1], cuts[1:]) if b > a]


def _load_weights(parts, sems):
    copies = []
    for buf, dst in parts:
        rows = buf.shape[1]
        for p in range(NDEV):
            cp = pltpu.make_async_copy(buf.at[p], dst.at[pl.ds(p * rows, rows), :], sems.at[len(copies)])
            cp.start()
            copies.append(cp)
    return copies


def _sigmoid(a):
    return 1.0 / (1.0 + jnp.exp(-a))


def _remote(src, dst, send_sem, recv_sem, to):
    return pltpu.make_async_remote_copy(
        src_ref=src, dst_ref=dst, send_sem=send_sem, recv_sem=recv_sem, device_id=to, device_id_type=pl.DeviceIdType.MESH
    )


class _Gather:
    X, Y, FAR = 4, 2, 6
    KINDS = 8
    TO_SIBLING = (0, 5, 6, 7)
    PIECES = 4

    def __init__(self, shards):
        n = len(shards) * self.KINDS * self.PIECES
        self.operands = list(shards)
        self.out_shape = [jax.ShapeDtypeStruct((NDEV,) + a.shape, a.dtype) for a in shards]
        self.sems = [pltpu.SemaphoreType.DMA((n,)), pltpu.SemaphoreType.DMA((n,)), pltpu.SemaphoreType.DMA((len(shards),))]
        self.stages = [self.begin, self.relay, self.relay_far, self.end]

    def _copies(self, t, k, block, to, ins, outs, sems, own=False, half=None):
        rows = outs[t].shape[1]
        if half is not None:
            parts = [(half * (rows // 2), rows // 2)]
        else:
            parts = _pieces(rows, self.PIECES if k in self.TO_SIBLING else 1)
        copies = []
        for q, (at, size) in enumerate(parts):
            dst = outs[t].at[_flat(block), pl.ds(at, size), :]
            src = ins[t].at[pl.ds(at, size), :] if own else dst
            slot = (self.KINDS * t + k) * self.PIECES + q
            copies.append(_remote(src, dst, sems[0].at[slot], sems[1].at[slot], to))
        return copies

    def _local(self, t, ins, outs, sems):
        return pltpu.make_async_copy(ins[t], outs[t].at[_flat(_peer(0))], sems[2].at[t])

    def begin(self, ins, outs, sems):
        me = _peer(0)
        for t in range(len(ins)):
            self._local(t, ins, outs, sems).start()
            for k, code in enumerate((1, self.X, self.Y)):
                for cp in self._copies(t, k, me, _peer(code), ins, outs, sems, own=True):
                    cp.start()

    def relay(self, ins, outs, sems):
        me, sibling = _peer(0), _peer(1)
        for t in range(len(ins)):
            for k, code, on_to, half in ((1, self.X, self.Y, 0), (2, self.Y, self.X, 1)):
                for cp in self._copies(t, k, _peer(code), me, ins, outs, sems):
                    cp.wait_recv()
                for cp in self._copies(t, 3 + half, _peer(code), _peer(on_to), ins, outs, sems, half=half):
                    cp.start()
                for cp in self._copies(t, 4 + k, _peer(code), sibling, ins, outs, sems):
                    cp.start()

    def relay_far(self, ins, outs, sems):
        me, sibling = _peer(0), _peer(1)
        for t in range(len(ins)):
            for half in (0, 1):
                for cp in self._copies(t, 3 + half, _peer(self.FAR), me, ins, outs, sems, half=half):
                    cp.wait_recv()
            for cp in self._copies(t, 7, _peer(self.FAR), sibling, ins, outs, sems):
                cp.start()

    def end(self, ins, outs, sems):
        me = _peer(0)
        for t in range(len(ins)):
            for k, code in ((0, 0), (5, self.X), (6, self.Y), (7, self.FAR)):
                for cp in self._copies(t, k, _peer(code ^ 1), me, ins, outs, sems):
                    cp.wait_recv()
            for k in range(self.KINDS):
                for cp in self._copies(t, k, me, me, ins, outs, sems, half={3: 0, 4: 1}.get(k)):
                    cp.wait_send()
            self._local(t, ins, outs, sems).wait()


class _GatherDirect:
    def __init__(self, arrays):
        n = len(arrays)
        self.operands = list(arrays)
        self.out_shape = [jax.ShapeDtypeStruct((NDEV,) + a.shape, a.dtype) for a in arrays]
        self.sems = [pltpu.SemaphoreType.DMA((7 * n,)), pltpu.SemaphoreType.DMA((7 * n,)), pltpu.SemaphoreType.DMA((n,))]
        self.stages = [self.begin, self.end]

    def begin(self, ins, outs, sems):
        mine = _flat(_peer(0))
        for t in range(len(ins)):
            pltpu.make_async_copy(ins[t], outs[t].at[mine], sems[2].at[t]).start()
            for k in range(1, NDEV):
                _remote(ins[t], outs[t].at[mine], sems[0].at[7 * t + k - 1], sems[1].at[7 * t + k - 1], _peer(k)).start()

    def end(self, ins, outs, sems):
        mine = _flat(_peer(0))
        for t in range(len(ins)):
            for k in range(1, NDEV):
                cp = _remote(ins[t], outs[t].at[_flat(_peer(k))], sems[0].at[7 * t + k - 1], sems[1].at[7 * t + k - 1], _peer(k))
                cp.wait_recv()
                cp.wait_send()
            pltpu.make_async_copy(ins[t], outs[t].at[mine], sems[2].at[t]).wait()


class _ChipScatter:
    pieces = 2

    def __init__(self, sums):
        n = len(sums) * NCHIP * self.pieces
        self.operands = list(sums)
        self.out_shape = [jax.ShapeDtypeStruct(a.shape, a.dtype) for a in sums]
        self.sems = [pltpu.SemaphoreType.DMA((n,)), pltpu.SemaphoreType.DMA((n,))]
        self.stages = [self.begin, self.end]

    def _copies(self, ins, outs, sems, arriving):
        mine = _chip(_peer(0))
        copies = []
        for t in range(len(ins)):
            rows = ins[t].shape[1] // self.pieces
            for k in (0, 4, 2, 6):
                other = _chip(_peer(k))
                for q in range(self.pieces):
                    part = pl.ds(q * rows, rows)
                    at = len(copies)
                    if k == 0:
                        cp = pltpu.make_async_copy(ins[t].at[mine, part, :], outs[t].at[mine, part, :], sems[0].at[at])
                    else:
                        landing = outs[t].at[other if arriving else mine, part, :]
                        cp = _remote(ins[t].at[other, part, :], landing, sems[0].at[at], sems[1].at[at], _peer(k))
                    copies.append(cp)
        return copies

    def begin(self, ins, outs, sems):
        for cp in self._copies(ins, outs, sems, False):
            cp.start()

    def end(self, ins, outs, sems):
        for at, cp in enumerate(self._copies(ins, outs, sems, True)):
            if at % (NCHIP * self.pieces) < self.pieces:
                cp.wait()
            else:
                cp.wait_recv()
                cp.wait_send()


def _split_refs(refs, counts):
    out, at = [], 0
    for n in counts:
        out.append(refs[at:at + n])
        at += n
    return out


def _comm_call(carries, name):
    nin = [len(c.operands) for c in carries]
    nout = [len(c.out_shape) for c in carries]
    nsem = [len(c.sems) for c in carries]

    def body(*refs):
        ins, outs, sems = _split_refs(refs, (sum(nin), sum(nout), sum(nsem)))
        parts = list(zip(carries, _split_refs(ins, nin), _split_refs(outs, nout), _split_refs(sems, nsem)))
        for depth in range(max(len(c.stages) for c in carries)):
            for c, i, o, s in parts:
                if depth < len(c.stages) - 1:
                    c.stages[depth](i, o, s)
        for c, i, o, s in parts:
            c.stages[-1](i, o, s)

    res = _call(
        body,
        name=name,
        out_shape=[sh for c in carries for sh in c.out_shape],
        in_specs=[ANY] * sum(nin),
        out_specs=[ANY] * sum(nout),
        scratch_shapes=[sm for c in carries for sm in c.sems],
        compiler_params=pltpu.CompilerParams(has_side_effects=True),
    )(*[a for c in carries for a in c.operands])
    return _split_refs(list(res), nout)


def _grid_call(body, carries, *, name, grid, in_specs, out_specs, out_shape, scratch_shapes, args):
    ni, no, ns = len(in_specs), len(out_specs), len(scratch_shapes)
    nin = [len(c.operands) for c in carries]
    nout = [len(c.out_shape) for c in carries]
    nsem = [len(c.sems) for c in carries]
    steps = int(np.prod(grid))

    def when_of(stage, count):
        first, last = (5 * steps) // 8 - 1, steps - 2
        return max(0, last if count <= 3 else first + (last - first) * (stage - 1) // (count - 3))

    def wrapped(*refs):
        ins, cins, outs, couts, scr, csems = _split_refs(refs, (ni, sum(nin), no, sum(nout), ns, sum(nsem)))
        if not carries:
            return body(*ins, *outs, *scr)
        parts = list(zip(carries, _split_refs(cins, nin), _split_refs(couts, nout), _split_refs(csems, nsem)))
        step = pl.program_id(0)
        for axis in range(1, len(grid)):
            step = step * grid[axis] + pl.program_id(axis)

        @pl.when(step == 0)
        def _():
            for c, i, o, s in parts:
                c.stages[0](i, o, s)

        body(*ins, *outs, *scr)

        for c, i, o, s in parts:
            for stage in range(1, len(c.stages) - 1):
                pl.when(step == when_of(stage, len(c.stages)))(functools.partial(c.stages[stage], i, o, s))

        @pl.when(step == steps - 1)
        def _():
            for c, i, o, s in parts:
                c.stages[-1](i, o, s)

    res = _call(
        wrapped,
        name=name,
        grid=tuple(grid),
        in_specs=list(in_specs) + [ANY] * sum(nin),
        out_specs=list(out_specs) + [ANY] * sum(nout),
        out_shape=list(out_shape) + [sh for c in carries for sh in c.out_shape],
        scratch_shapes=list(scratch_shapes) + [sm for c in carries for sm in c.sems],
        compiler_params=_seq(len(grid)),
    )(*args, *[a for c in carries for a in c.operands])
    res = list(res)
    return res[:no], _split_refs(res[no:], nout)


def _chunks(width):
    return [(at, min(FFN_CHUNK, width - at)) for at in range(0, width, FFN_CHUNK)]


def _ffn_fwd(x, gain, weights, ffn, head=None, carries=()):
    s, d = x.shape
    tm = min(512, s)

    def body(*refs):
        if head is None:
            x_ref, g_ref, b1, b3, b2, h_ref, a_ref, b_ref, hm_ref, w1s, w3s, w2s, sems = refs
        else:
            x_ref, g_ref, b1, b3, b2, gf_ref, t_ref, h_ref, a_ref, b_ref, hm_ref, dgf_ref, loss_ref, w1s, w3s, w2s, sems = refs
        i = pl.program_id(0)

        @pl.when(i == 0)
        def _():
            for cp in _load_weights(((b1, w1s), (b3, w3s), (b2, w2s)), sems):
                cp.wait()
            if head is not None:
                dgf_ref[...] = jnp.zeros_like(dgf_ref)
                loss_ref[...] = jnp.zeros_like(loss_ref)

        xv = x_ref[...]
        r = lax.rsqrt(jnp.mean(xv * xv, axis=-1, keepdims=True) + EPS)
        n = (xv * r * g_ref[...]).astype(BF16)
        acc = jnp.zeros((tm, d), F32)
        for at, width in _chunks(ffn):
            cols = slice(at, at + width)
            a = _dot(n, w1s[cols, :], NT)
            b = _dot(n, w3s[cols, :], NT)
            a_ref[:, cols] = a.astype(BF16)
            b_ref[:, cols] = b.astype(BF16)
            hm = (a * _sigmoid(a) * b).astype(BF16)
            hm_ref[:, cols] = hm
            acc = acc + _dot(hm, w2s[cols, :], NN)
        h = xv + 0.5 * acc
        if head is None:
            h_ref[...] = h
        else:
            rf = lax.rsqrt(jnp.mean(h * h, axis=-1, keepdims=True) + EPS)
            nh = h * rf
            gf = gf_ref[...]
            err = nh * gf - t_ref[...]
            loss_ref[...] += jnp.sum(err * err, axis=0, keepdims=True) * (0.5 / d)
            dy = err * (1.0 / d)
            dgf_ref[...] += jnp.sum(dy * nh, axis=0, keepdims=True)
            dn = dy * gf
            h_ref[...] = rf * (dn - nh * jnp.mean(dn * nh, axis=-1, keepdims=True))

    tile = pl.BlockSpec((tm, d), lambda i: (i, 0))
    row = pl.BlockSpec((1, d), lambda i: (0, 0))
    wide = pl.BlockSpec((tm, ffn), lambda i: (i, 0))
    in_specs = [tile, row, ANY, ANY, ANY]
    out_shape = [jax.ShapeDtypeStruct((s, d), F32)] + [jax.ShapeDtypeStruct((s, ffn), BF16)] * 3
    out_specs = [tile, wide, wide, wide]
    args = [x, gain] + list(weights)
    if head is not None:
        in_specs += [row, tile]
        args += list(head)
        out_shape += [jax.ShapeDtypeStruct((1, d), F32)] * 2
        out_specs += [row, row]
    return _grid_call(
        body,
        carries,
        name="ffn_fwd_loss" if head is not None else "ffn_fwd",
        grid=(s // tm,),
        in_specs=in_specs,
        out_specs=out_specs,
        out_shape=out_shape,
        scratch_shapes=[pltpu.VMEM((ffn, d), BF16)] * 3 + [pltpu.SemaphoreType.DMA((3 * NDEV,))],
        args=args,
    )


def _ffn_bwd(dh, x, a, b, gain, weights, ffn, name, carries=()):
    s, d = x.shape
    tm = min(512, s)
    halves = 2
    fh = ffn // halves

    def body(dh_ref, x_ref, a_ref, b_ref, g_ref, b1, b3, b2, dx_ref, da_ref, db_ref, n_ref, dg_ref, w1s, w3s, w2s, sems):
        i, j = pl.program_id(0), pl.program_id(1)

        @pl.when((i == 0) & (j == 0))
        def _():
            for cp in _load_weights(((b1, w1s), (b3, w3s), (b2, w2s)), sems):
                cp.wait()
            dg_ref[...] = jnp.zeros_like(dg_ref)

        @pl.when(j == 0)
        def _():
            dx_ref[...] = jnp.zeros_like(dx_ref)

        dob = (0.5 * dh_ref[...]).astype(BF16)
        chunks = _chunks(fh)

        def dhm_of(k):
            at, width = chunks[k]
            return _dot(dob, w2s[pl.ds(pl.multiple_of(j * fh + at, GROUP), width), :], NT)

        ahead = dhm_of(0)
        for k, (at, width) in enumerate(chunks):
            cols = slice(at, at + width)
            dhm = ahead
            if k + 1 < len(chunks):
                ahead = dhm_of(k + 1)
            for top in range(0, tm, ROW_BAND):
                band = slice(top, top + ROW_BAND)
                av = a_ref[band, cols].astype(F32)
                bv = b_ref[band, cols].astype(F32)
                sg = _sigmoid(av)
                dv = dhm[band]
                da_ref[band, cols] = (dv * bv * (sg * (1.0 + av * (1.0 - sg)))).astype(BF16)
                db_ref[band, cols] = (dv * (av * sg)).astype(BF16)
        half = pl.ds(pl.multiple_of(j * fh, GROUP), fh)
        dx_ref[...] += _dot(da_ref[...], w1s[half, :], NN) + _dot(db_ref[...], w3s[half, :], NN)

        @pl.when(j == halves - 1)
        def _():
            xv = x_ref[...]
            g = g_ref[...]
            r = lax.rsqrt(jnp.mean(xv * xv, axis=-1, keepdims=True) + EPS)
            nh = xv * r
            n_ref[...] = (nh * g).astype(BF16)
            total = dx_ref[...]
            dg_ref[...] += jnp.sum(total * nh, axis=0, keepdims=True)
            dnh = total * g
            dx_ref[...] = dh_ref[...] + r * (dnh - nh * jnp.mean(dnh * nh, axis=-1, keepdims=True))

    tile = pl.BlockSpec((tm, d), lambda i, j: (i, 0))
    row = pl.BlockSpec((1, d), lambda i, j: (0, 0))
    wide = pl.BlockSpec((tm, fh), lambda i, j: (i, j))
    return _grid_call(
        body,
        carries,
        name=name,
        grid=(s // tm, halves),
        in_specs=[tile, tile, wide, wide, row, ANY, ANY, ANY],
        out_specs=[tile, wide, wide, tile, row],
        out_shape=[
            jax.ShapeDtypeStruct((s, d), F32),
            jax.ShapeDtypeStruct((s, ffn), BF16),
            jax.ShapeDtypeStruct((s, ffn), BF16),
            jax.ShapeDtypeStruct((s, d), BF16),
            jax.ShapeDtypeStruct((1, d), F32),
        ],
        scratch_shapes=[pltpu.VMEM((ffn, d), BF16)] * 3 + [pltpu.SemaphoreType.DMA((3 * NDEV,))],
        args=[dh, x, a, b, gain] + list(weights),
    )


SWAP_PIECES = 8


def _wgrad(lhs, rhs, scale, name, carries=()):
    s, m = lhs.shape
    n = rhs.shape[1]
    rs = m // NDEV
    tk = min(1024, s)
    steps = s // tk
    pieces = [(j, at, size) for j in range(2) for at, size in _pieces(rs, SWAP_PIECES)]

    def body(l_ref, r_ref, o_ref, acc, mine, theirs, send_sems, recv_sems):
        h, k = pl.program_id(0), pl.program_id(1)

        @pl.when(k == 0)
        def _():
            acc[...] = _dot(l_ref[...], r_ref[...].astype(BF16), TN)

        @pl.when(k > 0)
        def _():
            acc[...] += _dot(l_ref[...], r_ref[...].astype(BF16), TN)

        def exchange(half):
            c = lax.axis_index("c")
            return [
                _remote(mine.at[half, 1 - c, j, pl.ds(at, size), :], theirs.at[half, j, pl.ds(at, size), :],
                        send_sems.at[half * len(pieces) + q], recv_sems.at[half * len(pieces) + q], _peer(1))
                for q, (j, at, size) in enumerate(pieces)
            ]

        def settle(half):
            for cp in exchange(half):
                cp.wait_recv()
            both = mine[half, lax.axis_index("c")].astype(F32) + theirs[half].astype(F32)
            o_ref[2 * half:2 * half + 2] = both.astype(BF16)
            for cp in exchange(half):
                cp.wait_send()

        for half in range(2):
            @pl.when((h == half) & (k == steps - 1))
            def _():
                for p in range(NCHIP):
                    mine[half, p % 2, p // 2] = (acc[p * rs:(p + 1) * rs, :] * scale).astype(BF16)
                for cp in exchange(half):
                    cp.start()
                if half == 1:
                    settle(0)
                    settle(1)

    (out,), carried = _grid_call(
        body,
        carries,
        name=name,
        grid=(2, steps),
        in_specs=[pl.BlockSpec((tk, m // 2), lambda h, k: (k, h)), pl.BlockSpec((tk, n), lambda h, k: (k, 0))],
        out_specs=[pl.BlockSpec((NCHIP, rs, n), lambda h, k: (0, 0, 0))],
        out_shape=[jax.ShapeDtypeStruct((NCHIP, rs, n), BF16)],
        scratch_shapes=[
            pltpu.VMEM((m // 2, n), F32), pltpu.VMEM((2, 2, 2, rs, n), BF16), pltpu.VMEM((2, 2, rs, n), BF16),
            pltpu.SemaphoreType.DMA((2 * len(pieces),)), pltpu.SemaphoreType.DMA((2 * len(pieces),)),
        ],
        args=[lhs, rhs],
    )
    return out, carried


def _mix_constants(s):
    c = GROUP
    lg = np.log1p(-np.exp2(-5.0 - np.arange(RET_HEADS, dtype=np.float32))).astype(np.float32)
    pos = np.arange(c, dtype=np.float32)
    rel = pos[:, None] - pos[None, :]
    decay = np.where(rel[None] >= 0, np.exp(lg[:, None, None] * np.maximum(rel, 0.0)[None]), 0.0).astype(np.float32)
    ktail = np.exp(lg[:, None] * (c - 1 - pos)[None, :]).astype(np.float32)
    qhead = np.exp(lg[:, None] * (pos + 1.0)[None, :]).astype(np.float32)
    chunk_decay = [float(v) for v in np.exp(lg * np.float32(c)).astype(np.float32)]
    ones = np.ones((1, 1, c), np.float32)
    inv_freq = (1.0 / (np.float32(ROPE_BASE) ** (np.arange(0, c, 2, dtype=np.float32) / np.float32(c)))).astype(np.float32)
    ang = (np.arange(s, dtype=np.float32)[:, None] * inv_freq[None, :]).astype(np.float32)
    cos, sin = np.cos(ang).astype(np.float32), np.sin(ang).astype(np.float32)
    return dict(
        decay=jnp.asarray(decay),
        ktail=jnp.asarray(ktail[:, :, None] * ones),
        qhead=jnp.asarray(qhead[:, :, None] * ones),
        chunk_decay=chunk_decay,
        cos=jnp.asarray(np.concatenate([cos, cos], axis=-1)),
        sin=jnp.asarray(np.concatenate([-sin, sin], axis=-1)),
    )


def _rope(t, cos, sin):
    return t * cos + pltpu.roll(t, GROUP // 2, axis=1) * sin


def _rope_bwd(dt, cos, sin):
    return dt * cos + pltpu.roll(dt * sin, GROUP // 2, axis=1)


def _window_sums(ext, w, forward):
    rows = ext.shape[0]
    acc, k = ext, 1
    while k < w:
        acc = acc + pltpu.roll(acc, k if forward else rows - k, axis=0)
        k *= 2
    return acc


def _pool_counts(tile, tm, w):
    t = lax.broadcasted_iota(jnp.int32, (tm, 1), 0) + tile * tm
    return jnp.minimum(t + 1, w).astype(F32)


def _mix_fwd(h1, gain, weights, pool_w, pool_scale, ret_gain, consts, carries=()):
    s, d = h1.shape
    pwid = N_POOL_GROUPS * GROUP
    rwid = RET_HEADS * GROUP
    inw = pwid + 4 * rwid
    tm = min(256, s)
    nck = tm // GROUP
    cd = consts["chunk_decay"]

    def body(h_ref, g_ref, bin_, bout, pw_ref, ps_ref, rg_ref, cos_ref, sin_ref, dec_ref, kt_ref, qh_ref,
             h2_ref, proj_ref, o_ref, rs_ref, wins, wouts, state, carry, mbuf, sems):
        i = pl.program_id(0)

        @pl.when(i == 0)
        def _():
            for cp in _load_weights(((bin_, wins), (bout, wouts)), sems):
                cp.wait()
            state[...] = jnp.zeros_like(state)
            carry[...] = jnp.zeros_like(carry)

        hv = h_ref[...]
        r = lax.rsqrt(jnp.mean(hv * hv, axis=-1, keepdims=True) + EPS)
        u = (hv * r * g_ref[...]).astype(BF16)
        proj_ref[...] = _dot(u, wins[...], NT)

        ext = jnp.concatenate([carry[...], proj_ref[:, 0:pwid]], axis=0)
        carry[...] = proj_ref[tm - MAX_WINDOW:tm, 0:pwid]
        for gi, w in enumerate(POOL_WINDOWS):
            cols = slice(gi * GROUP, (gi + 1) * GROUP)
            xg = ext[:, cols]
            ws = _window_sums(xg, w, True)[MAX_WINDOW:, :]
            pooled = ws / _pool_counts(i, tm, w) - xg[MAX_WINDOW:, :]
            z = _dot(pooled.astype(BF16), pw_ref[gi].astype(BF16), NN)
            mbuf[:, cols] = (z * ps_ref[:, cols]).astype(BF16)

        cos, sin = cos_ref[...], sin_ref[...]
        for h in range(RET_HEADS):
            cq = slice(pwid + h * GROUP, pwid + (h + 1) * GROUP)
            ck = slice(pwid + rwid + h * GROUP, pwid + rwid + (h + 1) * GROUP)
            cv = slice(pwid + 2 * rwid + h * GROUP, pwid + 2 * rwid + (h + 1) * GROUP)
            cg = slice(pwid + 3 * rwid + h * GROUP, pwid + 3 * rwid + (h + 1) * GROUP)
            ch = slice(h * GROUP, (h + 1) * GROUP)
            qr = _rope(proj_ref[:, cq], cos, sin)
            kr = _rope(proj_ref[:, ck], cos, sin) * (GROUP ** -0.5)
            vb = proj_ref[:, cv].astype(BF16)
            for n in range(nck):
                rows = slice(n * GROUP, (n + 1) * GROUP)
                qc, kc, vc = qr[rows], kr[rows], vb[rows]
                rb = state[h]
                rs_ref[n, h] = rb
                p = (_dot(qc.astype(BF16), kc.astype(BF16), NT) * dec_ref[h]).astype(BF16)
                o = _dot(p, vc, NN) + _dot((qc * qh_ref[h]).astype(BF16), rb.astype(BF16), NN)
                state[h] = cd[h] * rb + _dot((kc * kt_ref[h]).astype(BF16), vc, TN)
                o_ref[rows, ch] = o
                on = o * lax.rsqrt(jnp.mean(o * o, axis=-1, keepdims=True) + EPS)
                gv = proj_ref[rows, cg]
                mbuf[rows, pwid + h * GROUP:pwid + (h + 1) * GROUP] = (
                    gv * _sigmoid(gv) * (on * rg_ref[:, ch])
                ).astype(BF16)
        h2_ref[...] = hv + _dot(mbuf[...], wouts[...], NN)

    tile = pl.BlockSpec((tm, d), lambda i: (i, 0))
    full = lambda shape: pl.BlockSpec(shape, lambda i: (0,) * len(shape))
    return _grid_call(
        body,
        carries,
        name="mix_fwd",
        grid=(s // tm,),
        in_specs=[
            tile, full((1, d)), ANY, ANY,
            full((N_POOL_GROUPS, GROUP, GROUP)), full((1, pwid)), full((1, rwid)),
            pl.BlockSpec((tm, GROUP), lambda i: (i, 0)), pl.BlockSpec((tm, GROUP), lambda i: (i, 0)),
            full((RET_HEADS, GROUP, GROUP)), full((RET_HEADS, GROUP, GROUP)), full((RET_HEADS, GROUP, GROUP)),
        ],
        out_specs=[
            tile,
            pl.BlockSpec((tm, inw), lambda i: (i, 0)),
            pl.BlockSpec((tm, rwid), lambda i: (i, 0)),
            pl.BlockSpec((nck, RET_HEADS, GROUP, GROUP), lambda i: (i, 0, 0, 0)),
        ],
        out_shape=[
            jax.ShapeDtypeStruct((s, d), F32),
            jax.ShapeDtypeStruct((s, inw), F32),
            jax.ShapeDtypeStruct((s, rwid), F32),
            jax.ShapeDtypeStruct((s // GROUP, RET_HEADS, GROUP, GROUP), F32),
        ],
        scratch_shapes=[
            pltpu.VMEM((inw, d), BF16), pltpu.VMEM((d, d), BF16),
            pltpu.VMEM((RET_HEADS, GROUP, GROUP), F32), pltpu.VMEM((MAX_WINDOW, pwid), F32),
            pltpu.VMEM((tm, d), BF16), pltpu.SemaphoreType.DMA((2 * NDEV,)),
        ],
        args=[h1, gain, weights[0], weights[1], pool_w, pool_scale, ret_gain,
              consts["cos"], consts["sin"], consts["decay"], consts["ktail"], consts["qhead"]],
    )


def _mix_bwd(dh2, h1, proj, o_saved, rsave, gain, weights, pool_w, pool_scale, ret_gain, consts, carries=()):
    s, d = h1.shape
    pwid = N_POOL_GROUPS * GROUP
    rwid = RET_HEADS * GROUP
    inw = pwid + 4 * rwid
    tm = min(256, s)
    nck = tm // GROUP
    nt = s // tm
    cd = consts["chunk_decay"]
    halo_per_tile = tm // MAX_WINDOW

    def body(dh2_ref, h_ref, proj_ref, halo_ref, o_ref, rs_ref, g_ref, bin_, bout, pw_ref, ps_ref, rg_ref,
             cos_ref, sin_ref, dec_ref, kt_ref, qh_ref,
             dh1_ref, dproj_ref, u_ref, m_ref, dpw_ref, dps_ref, drg_ref, dg_ref,
             wins, wouts, dstate, carry, dm, dpj, sems):
        i = pl.program_id(0)
        tile = nt - 1 - i

        @pl.when(i == 0)
        def _():
            for cp in _load_weights(((bin_, wins), (bout, wouts)), sems):
                cp.wait()
            dstate[...] = jnp.zeros_like(dstate)
            carry[...] = jnp.zeros_like(carry)
            for ref in (dpw_ref, dps_ref, drg_ref, dg_ref):
                ref[...] = jnp.zeros_like(ref)

        dh2v = dh2_ref[...]
        dm[...] = _dot(dh2v.astype(BF16), wouts[...], NT)
        hv = h_ref[...]
        g = g_ref[...]
        r = lax.rsqrt(jnp.mean(hv * hv, axis=-1, keepdims=True) + EPS)
        uh = hv * r
        u_ref[...] = (uh * g).astype(BF16)

        halo = jnp.where(tile == 0, 0.0, halo_ref[...])
        ext = jnp.concatenate([halo, proj_ref[:, 0:pwid]], axis=0)
        next_dpn = carry[...]
        for gi, w in enumerate(POOL_WINDOWS):
            cols = slice(gi * GROUP, (gi + 1) * GROUP)
            xg = ext[:, cols]
            cnt = _pool_counts(tile, tm, w)
            pooled = (_window_sums(xg, w, True)[MAX_WINDOW:, :] / cnt - xg[MAX_WINDOW:, :]).astype(BF16)
            pwb = pw_ref[gi].astype(BF16)
            z = _dot(pooled, pwb, NN)
            scale = ps_ref[:, cols]
            m_ref[:, cols] = (z * scale).astype(BF16)
            da = dm[:, cols]
            dps_ref[:, cols] += jnp.sum(da * z, axis=0, keepdims=True)
            dz = (da * scale).astype(BF16)
            dpw_ref[gi] += _dot(pooled, dz, TN)
            dpl = _dot(dz, pwb, NT)
            dpn = dpl / cnt
            ext2 = jnp.concatenate([dpn, next_dpn[:, cols]], axis=0)
            dpj[:, cols] = (_window_sums(ext2, w, False)[0:tm, :] - dpl).astype(BF16)
            carry[:, cols] = dpn[0:MAX_WINDOW, :]

        cos, sin = cos_ref[...], sin_ref[...]
        for h in range(RET_HEADS):
            cq = slice(pwid + h * GROUP, pwid + (h + 1) * GROUP)
            ck = slice(pwid + rwid + h * GROUP, pwid + rwid + (h + 1) * GROUP)
            cv = slice(pwid + 2 * rwid + h * GROUP, pwid + 2 * rwid + (h + 1) * GROUP)
            cg = slice(pwid + 3 * rwid + h * GROUP, pwid + 3 * rwid + (h + 1) * GROUP)
            ch = slice(h * GROUP, (h + 1) * GROUP)
            qr = _rope(proj_ref[:, cq], cos, sin)
            kr = _rope(proj_ref[:, ck], cos, sin) * (GROUP ** -0.5)
            vb = proj_ref[:, cv].astype(BF16)
            gv = proj_ref[:, cg]
            ov = o_ref[:, ch]
            ro = lax.rsqrt(jnp.mean(ov * ov, axis=-1, keepdims=True) + EPS)
            on = ov * ro
            rg = rg_ref[:, ch]
            db = dm[:, pwid + h * GROUP:pwid + (h + 1) * GROUP]
            sg = _sigmoid(gv)
            sl = gv * sg
            m_ref[:, pwid + h * GROUP:pwid + (h + 1) * GROUP] = (sl * (on * rg)).astype(BF16)
            dpj[:, cg] = (db * (on * rg) * (sg * (1.0 + gv * (1.0 - sg)))).astype(BF16)
            drg_ref[:, ch] += jnp.sum(db * sl * on, axis=0, keepdims=True)
            don = db * sl * rg
            do = (ro * (don - on * jnp.mean(don * on, axis=-1, keepdims=True))).astype(BF16)
            for n in reversed(range(nck)):
                rows = slice(n * GROUP, (n + 1) * GROUP)
                qc, kc, vc, dob = qr[rows], kr[rows], vb[rows], do[rows]
                qcb, kcb = qc.astype(BF16), kc.astype(BF16)
                qh = (qc * qh_ref[h]).astype(BF16)
                kt = (kc * kt_ref[h]).astype(BF16)
                rn = rs_ref[n, h].astype(BF16)
                dnext = dstate[h]
                dnb = dnext.astype(BF16)
                dec = dec_ref[h]
                p = (_dot(qcb, kcb, NT) * dec).astype(BF16)
                ds = (_dot(dob, vc, NT) * dec).astype(BF16)
                dv = _dot(p, dob, TN) + _dot(kt, dnb, NN)
                dq = _dot(ds, kcb, NN) + _dot(dob, rn, NT) * qh_ref[h]
                dk = _dot(ds, qcb, TN) + _dot(vc, dnb, NT) * kt_ref[h]
                dstate[h] = cd[h] * dnext + _dot(qh, dob, TN)
                dpj[rows, cq] = _rope_bwd(dq, cos[rows], sin[rows]).astype(BF16)
                dpj[rows, ck] = _rope_bwd(dk * (GROUP ** -0.5), cos[rows], sin[rows]).astype(BF16)
                dpj[rows, cv] = dv.astype(BF16)

        dproj_ref[...] = dpj[...]
        du = _dot(dpj[...], wins[...], NN)
        dg_ref[...] += jnp.sum(du * uh, axis=0, keepdims=True)
        dn = du * g
        dh1_ref[...] = dh2v + r * (dn - uh * jnp.mean(dn * uh, axis=-1, keepdims=True))

    rev = lambda i: (nt - 1 - i, 0)
    tile = pl.BlockSpec((tm, d), rev)
    full = lambda shape: pl.BlockSpec(shape, lambda i: (0,) * len(shape))
    return _grid_call(
        body,
        carries,
        name="mix_bwd",
        grid=(nt,),
        in_specs=[
            tile, tile,
            pl.BlockSpec((tm, inw), rev),
            pl.BlockSpec((MAX_WINDOW, pwid), lambda i: (jnp.maximum((nt - 1 - i) * halo_per_tile - 1, 0), 0)),
            pl.BlockSpec((tm, rwid), rev),
            pl.BlockSpec((nck, RET_HEADS, GROUP, GROUP), lambda i: (nt - 1 - i, 0, 0, 0)),
            full((1, d)), ANY, ANY,
            full((N_POOL_GROUPS, GROUP, GROUP)), full((1, pwid)), full((1, rwid)),
            pl.BlockSpec((tm, GROUP), rev), pl.BlockSpec((tm, GROUP), rev),
            full((RET_HEADS, GROUP, GROUP)), full((RET_HEADS, GROUP, GROUP)), full((RET_HEADS, GROUP, GROUP)),
        ],
        out_specs=[
            tile, pl.BlockSpec((tm, inw), rev), tile, tile,
            full((N_POOL_GROUPS, GROUP, GROUP)), full((1, pwid)), full((1, rwid)), full((1, d)),
        ],
        out_shape=[
            jax.ShapeDtypeStruct((s, d), F32),
            jax.ShapeDtypeStruct((s, inw), BF16),
            jax.ShapeDtypeStruct((s, d), BF16),
            jax.ShapeDtypeStruct((s, d), BF16),
            jax.ShapeDtypeStruct((N_POOL_GROUPS, GROUP, GROUP), F32),
            jax.ShapeDtypeStruct((1, pwid), F32),
            jax.ShapeDtypeStruct((1, rwid), F32),
            jax.ShapeDtypeStruct((1, d), F32),
        ],
        scratch_shapes=[
            pltpu.VMEM((inw, d), BF16), pltpu.VMEM((d, d), BF16),
            pltpu.VMEM((RET_HEADS, GROUP, GROUP), F32), pltpu.VMEM((MAX_WINDOW, pwid), F32),
            pltpu.VMEM((tm, d), F32), pltpu.VMEM((tm, inw), BF16), pltpu.SemaphoreType.DMA((2 * NDEV,)),
        ],
        args=[dh2, h1, proj, proj, o_saved, rsave, gain, weights[0], weights[1], pool_w, pool_scale, ret_gain,
              consts["cos"], consts["sin"], consts["decay"], consts["ktail"], consts["qhead"]],
    )


def _adam(w, g, m, v):
    m = ADAM_B1 * m + (1.0 - ADAM_B1) * g
    v = ADAM_B2 * v + (1.0 - ADAM_B2) * jnp.square(g)
    m_hat = m / (1.0 - ADAM_B1 ** ADAM_STEP)
    v_hat = v / (1.0 - ADAM_B2 ** ADAM_STEP)
    delta = -ADAM_LR * (m_hat / (jnp.sqrt(v_hat) + ADAM_EPS) + ADAM_WD * w)
    return delta, m, v


def _adamw_big(w, parts, m, v, name):
    rows, d = w.shape
    tr = _row_tile(rows, 176)

    def body(w_ref, p_ref, m_ref, v_ref, g_ref, d_ref, nm_ref, nv_ref):
        g = p_ref[0].astype(F32)
        for q in range(1, NCHIP):
            g = g + p_ref[q].astype(F32)
        g_ref[...] = g
        d_ref[...], nm_ref[...], nv_ref[...] = _adam(w_ref[...], g, m_ref[...], v_ref[...])

    spec = pl.BlockSpec((tr, d), lambda i: (i, 0))
    return _call(
        body,
        name=name,
        grid=(rows // tr,),
        in_specs=[spec, pl.BlockSpec((NCHIP, tr, d), lambda i: (0, i, 0)), spec, spec],
        out_specs=[spec] * 4,
        out_shape=[jax.ShapeDtypeStruct((rows, d), F32)] * 4,
        compiler_params=_seq(1),
    )(w, parts, m, v)


def _adamw_small(stats_all, pw_all, ws, ms, vs, pwid):
    nsmall = len(ws)

    def body(*refs):
        st_ref, pwa_ref = refs[0], refs[1]
        w_refs = refs[2:2 + nsmall]
        m_refs = refs[2 + nsmall:2 + 2 * nsmall]
        v_refs = refs[2 + 2 * nsmall:2 + 3 * nsmall]
        outs = refs[2 + 3 * nsmall:]
        st = st_ref[0]
        pwg = pwa_ref[0]
        for q in range(1, NDEV):
            st = st + st_ref[q]
            pwg = pwg + pwa_ref[q]
        grads = [st[0:1, :], st[1:2, :], st[2:3, :], st[3:4, :], st[4:5, 0:pwid], st[4:5, pwid:2 * pwid], pwg]
        outs[0][...] = jnp.zeros((1, GROUP), F32) + jnp.sum(st[5:6, :])
        for j in range(nsmall):
            delta, nm, nv = _adam(w_refs[j][...], grads[j], m_refs[j][...], v_refs[j][...])
            outs[1 + 4 * j][...] = grads[j]
            outs[2 + 4 * j][...] = delta
            outs[3 + 4 * j][...] = nm
            outs[4 + 4 * j][...] = nv

    out_shape = [jax.ShapeDtypeStruct((1, GROUP), F32)]
    for w in ws:
        out_shape += [jax.ShapeDtypeStruct(w.shape, F32)] * 4
    return _call(body, name="adamw_small", out_shape=out_shape, compiler_params=_params())(
        stats_all, pw_all, *ws, *ms, *vs
    )


def kernel(x, ffn1_norm, ffn1_w1, ffn1_w3, ffn1_w2, mix_norm, w_in, pool_w, pool_scale, ret_norm, w_out, ffn2_norm, ffn2_w1, ffn2_w3, ffn2_w2, final_norm, loss_target, m_ffn1_norm, m_ffn1_w1, m_ffn1_w3, m_ffn1_w2, m_mix_norm, m_w_in, m_pool_w, m_pool_scale, m_ret_norm, m_w_out, m_ffn2_norm, m_ffn2_w1, m_ffn2_w3, m_ffn2_w2, m_final_norm, v_ffn1_norm, v_ffn1_w1, v_ffn1_w3, v_ffn1_w2, v_mix_norm, v_w_in, v_pool_w, v_pool_scale, v_ret_norm, v_w_out, v_ffn2_norm, v_ffn2_w1, v_ffn2_w3, v_ffn2_w2, v_final_norm):
    s, d = x.shape[1], x.shape[2]
    ffn = ffn1_w1.shape[2] * NDEV
    pwid = pool_scale.shape[1]
    xs, tgt = x[0], loss_target[0]
    consts = _mix_constants(s)
    pw3 = pool_w[0]
    fnorm = final_norm.reshape(1, d)

    rows_of = lambda w, transposed: (w[0].T if transposed else w[0]).astype(BF16)
    send_f1 = [rows_of(ffn1_w1, True), rows_of(ffn1_w3, True), rows_of(ffn1_w2, False)]
    send_mix = [rows_of(w_in, True), rows_of(w_out, False)]
    send_f2 = [rows_of(ffn2_w1, True), rows_of(ffn2_w3, True), rows_of(ffn2_w2, False)]

    (w_f1,) = _comm_call([_Gather(send_f1)], "gather_ffn1")
    (h1, a1, b1, hm1), (more,) = _ffn_fwd(xs, ffn1_norm, w_f1, ffn, carries=[_Gather(send_mix + send_f2[:1])])
    w_mix = more[:2]
    (h2, proj, o_saved, rsave), (rest,) = _mix_fwd(
        h1, mix_norm, w_mix, pw3, pool_scale, ret_norm, consts, carries=[_Gather(send_f2[1:])]
    )
    w_f2 = more[2:] + rest
    (dh3, a2, b2, hm2, dgf, loss_cols), _ = _ffn_fwd(h2, ffn2_norm, w_f2, ffn, head=(fnorm, tgt))

    (dh2, da2, db2, n2, dg2), _ = _ffn_bwd(dh3, h2, a2, b2, ffn2_norm, w_f2, ffn, "ffn2_bwd")
    sum_f2w1, _ = _wgrad(da2, n2, 1.0, "ffn2_w1_grad")
    sum_f2w3, _ = _wgrad(db2, n2, 1.0, "ffn2_w3_grad")
    sum_f2w2, ((parts_f2w1,),) = _wgrad(hm2, dh3, 0.5, "ffn2_w2_grad", carries=[_ChipScatter([sum_f2w1])])

    (dh1, dproj, u, mm, dpw, dps, drg, dgm), ((parts_f2w3, parts_f2w2),) = _mix_bwd(
        dh2, h1, proj, o_saved, rsave, mix_norm, w_mix, pw3, pool_scale, ret_norm, consts,
        carries=[_ChipScatter([sum_f2w3, sum_f2w2])],
    )
    (dx, da1, db1, n1, dg1), _ = _ffn_bwd(dh1, xs, a1, b1, ffn1_norm, w_f1, ffn, "ffn1_bwd")
    stats = jnp.concatenate(
        [dg1, dgm, dg2, dgf, jnp.concatenate([dps, drg], axis=1), loss_cols, jnp.zeros((2, d), F32)], axis=0
    )
    small = _GatherDirect([stats, dpw.reshape(N_POOL_GROUPS * GROUP, GROUP)])
    sum_f1w2, ((stats_all, pw_all),) = _wgrad(hm1, dh1, 0.5, "ffn1_w2_grad", carries=[small])
    sum_f1w1, ((parts_f1w2,),) = _wgrad(da1, n1, 1.0, "ffn1_w1_grad", carries=[_ChipScatter([sum_f1w2])])
    sum_f1w3, ((parts_f1w1,),) = _wgrad(db1, n1, 1.0, "ffn1_w3_grad", carries=[_ChipScatter([sum_f1w1])])
    sum_in, ((parts_f1w3,),) = _wgrad(dproj, u, 1.0, "w_in_grad", carries=[_ChipScatter([sum_f1w3])])
    sum_out, ((parts_in,),) = _wgrad(mm, dh2, 1.0, "w_out_grad", carries=[_ChipScatter([sum_in])])
    ((parts_out,),) = _comm_call([_ChipScatter([sum_out])], "scatter_last")

    big = (
        (ffn1_w1, m_ffn1_w1, v_ffn1_w1, parts_f1w1, True),
        (ffn1_w3, m_ffn1_w3, v_ffn1_w3, parts_f1w3, True),
        (ffn1_w2, m_ffn1_w2, v_ffn1_w2, parts_f1w2, False),
        (w_in, m_w_in, v_w_in, parts_in, True),
        (w_out, m_w_out, v_w_out, parts_out, False),
        (ffn2_w1, m_ffn2_w1, v_ffn2_w1, parts_f2w1, True),
        (ffn2_w3, m_ffn2_w3, v_ffn2_w3, parts_f2w3, True),
        (ffn2_w2, m_ffn2_w2, v_ffn2_w2, parts_f2w2, False),
    )
    big_out = []
    for j, (w, m, v, parts, t) in enumerate(big):
        view = (lambda a: a[0].T) if t else (lambda a: a[0])
        back = (lambda a: a.T[None]) if t else (lambda a: a[None])
        big_out.append([back(a) for a in _adamw_big(view(w), parts, view(m), view(v), "adamw_%d" % j)])

    small_w = (ffn1_norm, mix_norm, ffn2_norm, fnorm, pool_scale, ret_norm, pw3.reshape(-1, GROUP))
    small_m = (m_ffn1_norm, m_mix_norm, m_ffn2_norm, m_final_norm.reshape(1, d), m_pool_scale, m_ret_norm, m_pool_w.reshape(-1, GROUP))
    small_v = (v_ffn1_norm, v_mix_norm, v_ffn2_norm, v_final_norm.reshape(1, d), v_pool_scale, v_ret_norm, v_pool_w.reshape(-1, GROUP))
    res = _adamw_small(stats_all, pw_all, small_w, small_m, small_v, pwid)
    loss = res[0][0, 0]
    small_out = [list(res[1 + 4 * j:5 + 4 * j]) for j in range(len(small_w))]
    small_out[3] = [a.reshape(d) for a in small_out[3]]
    small_out[6] = [a.reshape(pool_w.shape) for a in small_out[6]]

    order = [small_out[0], big_out[0], big_out[1], big_out[2], small_out[1], big_out[3], small_out[6], small_out[4],
             small_out[5], big_out[4], small_out[2], big_out[5], big_out[6], big_out[7], small_out[3]]
    result = [loss, dx[None]]
    for kind in range(4):
        result += [t[kind] for t in order]
    return tuple(result)
```

```python
import functools

import numpy as np
import jax
import jax.numpy as jnp
from jax import lax
from jax.experimental import pallas as pl
from jax.experimental.pallas import tpu as pltpu

F32 = jnp.float32
BF16 = jnp.bfloat16

NDEV = 8
NCHIP = 4
EPS = 1e-6
N_POOL_GROUPS = 4
POOL_WINDOWS = (2, 4, 8, 16)
MAX_WINDOW = 16
GROUP = 128
RET_HEADS = 4
ROPE_BASE = 10000.0
ADAM_LR = 0.001
ADAM_B1 = 0.9
ADAM_B2 = 0.999
ADAM_EPS = 1e-08
ADAM_WD = 0.01
ADAM_STEP = 10

VMEM_LIMIT = 56 * 1024 * 1024
FFN_CHUNK = 256
ROW_BAND = 32

NT = (((1,), (1,)), ((), ()))
NN = (((1,), (0,)), ((), ()))
TN = (((0,), (0,)), ((), ()))

ANY = pl.BlockSpec(memory_space=pl.ANY)


def _dot(a, b, dims):
    return lax.dot_general(a, b, dims, preferred_element_type=F32)


def _call(body, **kw):
    return pl.pallas_call(body, **kw)


def _params(**kw):
    return pltpu.CompilerParams(vmem_limit_bytes=VMEM_LIMIT, **kw)


def _seq(n):
    return _params(dimension_semantics=("arbitrary",) * n)


def _peer(k):
    x, y, c = lax.axis_index("x"), lax.axis_index("y"), lax.axis_index("c")
    return (1 - x if k & 4 else x, 1 - y if k & 2 else y, 1 - c if k & 1 else c)


def _flat(pos):
    return 4 * pos[0] + 2 * pos[1] + pos[2]


def _chip(pos):
    return 2 * pos[0] + pos[1]


def _row_tile(rows, cap):
    return max(t for t in range(16, min(rows, cap) + 1, 16) if rows % t == 0)


def _pieces(rows, n):
    tiles = rows // 16
    cuts = [16 * (tiles * q // n) for q in range(n + 1)]
    return [(a, b - a) for a, b in zip(cuts[:-1], cuts[1:])]


def _load_weights(parts, sems):
    copies = []
    for buf, dst in parts:
        rows = buf.shape[1]
        for p in range(NDEV):
            cp = pltpu.make_async_copy(buf.at[p], dst.at[pl.ds(p * rows, rows), :], sems.at[len(copies)])
            cp.start()
            copies.append(cp)
    return copies


def _sigmoid(a):
    return 1.0 / (1.0 + jnp.exp(-a))


def _remote(src, dst, send_sem, recv_sem, to):
    return pltpu.make_async_remote_copy(
        src_ref=src, dst_ref=dst, send_sem=send_sem, recv_sem=recv_sem, device_id=to, device_id_type=pl.DeviceIdType.MESH
    )


class _Gather:
    X, Y, FAR = 4, 2, 6
    COPIES = 8

    def __init__(self, shards):
        n = len(shards)
        self.operands = list(shards)
        self.out_shape = [jax.ShapeDtypeStruct((NDEV,) + a.shape, a.dtype) for a in shards]
        self.sems = [
            pltpu.SemaphoreType.DMA((self.COPIES * n,)), pltpu.SemaphoreType.DMA((self.COPIES * n,)),
            pltpu.SemaphoreType.DMA((n,)),
        ]
        self.stages = [self.begin, self.relay, self.relay_far, self.end]

    def _copy(self, t, k, block, to, ins, outs, sems, own=False, half=None):
        rows = outs[t].shape[1]
        part = pl.ds(0, rows) if half is None else pl.ds(half * (rows // 2), rows // 2)
        dst = outs[t].at[_flat(block), part, :]
        at = self.COPIES * t + k
        return _remote(ins[t] if own else dst, dst, sems[0].at[at], sems[1].at[at], to)

    def _local(self, t, ins, outs, sems):
        return pltpu.make_async_copy(ins[t], outs[t].at[_flat(_peer(0))], sems[2].at[t])

    def begin(self, ins, outs, sems):
        me = _peer(0)
        for t in range(len(ins)):
            self._local(t, ins, outs, sems).start()
            for k, code in enumerate((1, self.X, self.Y)):
                self._copy(t, k, me, _peer(code), ins, outs, sems, own=True).start()

    def relay(self, ins, outs, sems):
        me, sibling = _peer(0), _peer(1)
        for t in range(len(ins)):
            self._copy(t, 1, _peer(self.X), me, ins, outs, sems).wait_recv()
            self._copy(t, 3, _peer(self.X), _peer(self.Y), ins, outs, sems, half=0).start()
            self._copy(t, 5, _peer(self.X), sibling, ins, outs, sems).start()
            self._copy(t, 2, _peer(self.Y), me, ins, outs, sems).wait_recv()
            self._copy(t, 4, _peer(self.Y), _peer(self.X), ins, outs, sems, half=1).start()
            self._copy(t, 6, _peer(self.Y), sibling, ins, outs, sems).start()

    def relay_far(self, ins, outs, sems):
        me, sibling = _peer(0), _peer(1)
        for t in range(len(ins)):
            self._copy(t, 3, _peer(self.FAR), me, ins, outs, sems, half=0).wait_recv()
            self._copy(t, 4, _peer(self.FAR), me, ins, outs, sems, half=1).wait_recv()
            self._copy(t, 7, _peer(self.FAR), sibling, ins, outs, sems).start()

    def end(self, ins, outs, sems):
        me = _peer(0)
        for t in range(len(ins)):
            self._copy(t, 0, _peer(1), me, ins, outs, sems).wait_recv()
            for k, code in ((5, self.X), (6, self.Y), (7, self.FAR)):
                self._copy(t, k, _peer(code ^ 1), me, ins, outs, sems).wait_recv()
            for k in range(self.COPIES):
                self._copy(t, k, me, me, ins, outs, sems, half=0 if k == 3 else 1 if k == 4 else None).wait_send()
            self._local(t, ins, outs, sems).wait()


class _GatherDirect:
    def __init__(self, arrays):
        n = len(arrays)
        self.operands = list(arrays)
        self.out_shape = [jax.ShapeDtypeStruct((NDEV,) + a.shape, a.dtype) for a in arrays]
        self.sems = [pltpu.SemaphoreType.DMA((7 * n,)), pltpu.SemaphoreType.DMA((7 * n,)), pltpu.SemaphoreType.DMA((n,))]
        self.stages = [self.begin, self.end]

    def begin(self, ins, outs, sems):
        mine = _flat(_peer(0))
        for t in range(len(ins)):
            pltpu.make_async_copy(ins[t], outs[t].at[mine], sems[2].at[t]).start()
            for k in range(1, NDEV):
                _remote(ins[t], outs[t].at[mine], sems[0].at[7 * t + k - 1], sems[1].at[7 * t + k - 1], _peer(k)).start()

    def end(self, ins, outs, sems):
        mine = _flat(_peer(0))
        for t in range(len(ins)):
            for k in range(1, NDEV):
                cp = _remote(ins[t], outs[t].at[_flat(_peer(k))], sems[0].at[7 * t + k - 1], sems[1].at[7 * t + k - 1], _peer(k))
                cp.wait_recv()
                cp.wait_send()
            pltpu.make_async_copy(ins[t], outs[t].at[mine], sems[2].at[t]).wait()


class _ChipScatter:
    pieces = 2

    def __init__(self, sums):
        n = len(sums) * NCHIP * self.pieces
        self.operands = list(sums)
        self.out_shape = [jax.ShapeDtypeStruct(a.shape, a.dtype) for a in sums]
        self.sems = [pltpu.SemaphoreType.DMA((n,)), pltpu.SemaphoreType.DMA((n,))]
        self.stages = [self.begin, self.end]

    def _copies(self, ins, outs, sems, arriving):
        mine = _chip(_peer(0))
        copies = []
        for t in range(len(ins)):
            rows = ins[t].shape[1] // self.pieces
            for k in (0, 4, 2, 6):
                other = _chip(_peer(k))
                for q in range(self.pieces):
                    part = pl.ds(q * rows, rows)
                    at = len(copies)
                    if k == 0:
                        cp = pltpu.make_async_copy(ins[t].at[mine, part, :], outs[t].at[mine, part, :], sems[0].at[at])
                    else:
                        landing = outs[t].at[other if arriving else mine, part, :]
                        cp = _remote(ins[t].at[other, part, :], landing, sems[0].at[at], sems[1].at[at], _peer(k))
                    copies.append(cp)
        return copies

    def begin(self, ins, outs, sems):
        for cp in self._copies(ins, outs, sems, False):
            cp.start()

    def end(self, ins, outs, sems):
        for at, cp in enumerate(self._copies(ins, outs, sems, True)):
            if at % (NCHIP * self.pieces) < self.pieces:
                cp.wait()
            else:
                cp.wait_recv()
                cp.wait_send()


def _split_refs(refs, counts):
    out, at = [], 0
    for n in counts:
        out.append(refs[at:at + n])
        at += n
    return out


def _comm_call(carries, name):
    nin = [len(c.operands) for c in carries]
    nout = [len(c.out_shape) for c in carries]
    nsem = [len(c.sems) for c in carries]

    def body(*refs):
        ins, outs, sems = _split_refs(refs, (sum(nin), sum(nout), sum(nsem)))
        parts = list(zip(carries, _split_refs(ins, nin), _split_refs(outs, nout), _split_refs(sems, nsem)))
        for depth in range(max(len(c.stages) for c in carries)):
            for c, i, o, s in parts:
                if depth < len(c.stages) - 1:
                    c.stages[depth](i, o, s)
        for c, i, o, s in parts:
            c.stages[-1](i, o, s)

    res = _call(
        body,
        name=name,
        out_shape=[sh for c in carries for sh in c.out_shape],
        in_specs=[ANY] * sum(nin),
        out_specs=[ANY] * sum(nout),
        scratch_shapes=[sm for c in carries for sm in c.sems],
        compiler_params=pltpu.CompilerParams(has_side_effects=True),
    )(*[a for c in carries for a in c.operands])
    return _split_refs(list(res), nout)


def _grid_call(body, carries, *, name, grid, in_specs, out_specs, out_shape, scratch_shapes, args):
    ni, no, ns = len(in_specs), len(out_specs), len(scratch_shapes)
    nin = [len(c.operands) for c in carries]
    nout = [len(c.out_shape) for c in carries]
    nsem = [len(c.sems) for c in carries]
    steps = int(np.prod(grid))

    def when_of(stage, count):
        first, last = (5 * steps) // 8 - 1, steps - 2
        return max(0, last if count <= 3 else first + (last - first) * (stage - 1) // (count - 3))

    def wrapped(*refs):
        ins, cins, outs, couts, scr, csems = _split_refs(refs, (ni, sum(nin), no, sum(nout), ns, sum(nsem)))
        if not carries:
            return body(*ins, *outs, *scr)
        parts = list(zip(carries, _split_refs(cins, nin), _split_refs(couts, nout), _split_refs(csems, nsem)))
        step = pl.program_id(0)
        for axis in range(1, len(grid)):
            step = step * grid[axis] + pl.program_id(axis)

        @pl.when(step == 0)
        def _():
            for c, i, o, s in parts:
                c.stages[0](i, o, s)

        body(*ins, *outs, *scr)

        for c, i, o, s in parts:
            for stage in range(1, len(c.stages) - 1):
                pl.when(step == when_of(stage, len(c.stages)))(functools.partial(c.stages[stage], i, o, s))

        @pl.when(step == steps - 1)
        def _():
            for c, i, o, s in parts:
                c.stages[-1](i, o, s)

    res = _call(
        wrapped,
        name=name,
        grid=tuple(grid),
        in_specs=list(in_specs) + [ANY] * sum(nin),
        out_specs=list(out_specs) + [ANY] * sum(nout),
        out_shape=list(out_shape) + [sh for c in carries for sh in c.out_shape],
        scratch_shapes=list(scratch_shapes) + [sm for c in carries for sm in c.sems],
        compiler_params=_seq(len(grid)),
    )(*args, *[a for c in carries for a in c.operands])
    res = list(res)
    return res[:no], _split_refs(res[no:], nout)


def _chunks(width):
    return [(at, min(FFN_CHUNK, width - at)) for at in range(0, width, FFN_CHUNK)]


def _start_chunk_loads(bufs, dsts, sems, chunks):
    copies = [[None] * len(chunks) for _ in bufs]
    for c, (at, width) in enumerate(chunks):
        for m, (buf, dst) in enumerate(zip(bufs, dsts)):
            cp = pltpu.make_async_copy(
                buf.at[pl.ds(at, width), :], dst.at[pl.ds(at, width), :], sems.at[m * len(chunks) + c]
            )
            cp.start()
            copies[m][c] = cp
    return copies


def _ffn_fwd(x, gain, weights, ffn, head=None, carries=()):
    s, d = x.shape
    tm = min(512, s)
    chunks = _chunks(ffn)

    def body(*refs):
        if head is None:
            x_ref, g_ref, b1, b3, b2, h_ref, a_ref, b_ref, hm_ref, w1s, w3s, w2s, sems = refs
        else:
            x_ref, g_ref, b1, b3, b2, gf_ref, t_ref, h_ref, a_ref, b_ref, hm_ref, dgf_ref, loss_ref, w1s, w3s, w2s, sems = refs

        def one_tile(loads):
            xv = x_ref[...]
            r = lax.rsqrt(jnp.mean(xv * xv, axis=-1, keepdims=True) + EPS)
            n = (xv * r * g_ref[...]).astype(BF16)
            acc = jnp.zeros((tm, d), F32)
            for c, (at, width) in enumerate(chunks):
                cols = slice(at, at + width)
                if loads is not None:
                    for m in range(3):
                        loads[m][c].wait()
                a = _dot(n, w1s[cols, :], NT)
                b = _dot(n, w3s[cols, :], NT)
                a_ref[:, cols] = a.astype(BF16)
                b_ref[:, cols] = b.astype(BF16)
                hm = (a * _sigmoid(a) * b).astype(BF16)
                hm_ref[:, cols] = hm
                acc = acc + _dot(hm, w2s[cols, :], NN)
            h = xv + 0.5 * acc
            if head is None:
                h_ref[...] = h
            else:
                rf = lax.rsqrt(jnp.mean(h * h, axis=-1, keepdims=True) + EPS)
                nh = h * rf
                gf = gf_ref[...]
                err = nh * gf - t_ref[...]
                loss_ref[...] += jnp.sum(err * err, axis=0, keepdims=True) * (0.5 / d)
                dy = err * (1.0 / d)
                dgf_ref[...] += jnp.sum(dy * nh, axis=0, keepdims=True)
                dn = dy * gf
                h_ref[...] = rf * (dn - nh * jnp.mean(dn * nh, axis=-1, keepdims=True))

        if head is None:
            @pl.when(pl.program_id(0) == 0)
            def _():
                one_tile(_start_chunk_loads((b1, b3, b2), (w1s, w3s, w2s), sems, chunks))

            @pl.when(pl.program_id(0) > 0)
            def _():
                one_tile(None)
        else:
            @pl.when(pl.program_id(0) == 0)
            def _():
                dgf_ref[...] = jnp.zeros_like(dgf_ref)
                loss_ref[...] = jnp.zeros_like(loss_ref)
                for loads in _start_chunk_loads((b1, b3, b2), (w1s, w3s, w2s), sems, chunks):
                    for cp in loads:
                        cp.wait()

            one_tile(None)

    tile = pl.BlockSpec((tm, d), lambda i: (i, 0))
    row = pl.BlockSpec((1, d), lambda i: (0, 0))
    wide = pl.BlockSpec((tm, ffn), lambda i: (i, 0))
    in_specs = [tile, row, ANY, ANY, ANY]
    out_shape = [jax.ShapeDtypeStruct((s, d), F32)] + [jax.ShapeDtypeStruct((s, ffn), BF16)] * 3
    out_specs = [tile, wide, wide, wide]
    args = [x, gain] + [w.reshape(ffn, d) for w in weights]
    if head is not None:
        in_specs += [row, tile]
        args += list(head)
        out_shape += [jax.ShapeDtypeStruct((1, d), F32)] * 2
        out_specs += [row, row]
    return _grid_call(
        body,
        carries,
        name="ffn_fwd_loss" if head is not None else "ffn_fwd",
        grid=(s // tm,),
        in_specs=in_specs,
        out_specs=out_specs,
        out_shape=out_shape,
        scratch_shapes=[pltpu.VMEM((ffn, d), BF16)] * 3 + [pltpu.SemaphoreType.DMA((3 * len(chunks),))],
        args=args,
    )


def _ffn_bwd(dh, x, a, b, gain, weights, ffn, name, carries=()):
    s, d = x.shape
    tm = min(512, s)
    halves = 2
    fh = ffn // halves

    def body(dh_ref, x_ref, a_ref, b_ref, g_ref, b1, b3, b2, dx_ref, da_ref, db_ref, n_ref, dg_ref, w1s, w3s, w2s, sems):
        i, j = pl.program_id(0), pl.program_id(1)

        @pl.when((i == 0) & (j == 0))
        def _():
            for cp in _load_weights(((b1, w1s), (b3, w3s), (b2, w2s)), sems):
                cp.wait()
            dg_ref[...] = jnp.zeros_like(dg_ref)

        @pl.when(j == 0)
        def _():
            dx_ref[...] = jnp.zeros_like(dx_ref)

        dob = (0.5 * dh_ref[...]).astype(BF16)
        chunks = _chunks(fh)

        def dhm_of(k):
            at, width = chunks[k]
            return _dot(dob, w2s[pl.ds(pl.multiple_of(j * fh + at, GROUP), width), :], NT)

        ahead = dhm_of(0)
        for k, (at, width) in enumerate(chunks):
            cols = slice(at, at + width)
            dhm = ahead
            if k + 1 < len(chunks):
                ahead = dhm_of(k + 1)
            for top in range(0, tm, ROW_BAND):
                band = slice(top, top + ROW_BAND)
                av = a_ref[band, cols].astype(F32)
                bv = b_ref[band, cols].astype(F32)
                sg = _sigmoid(av)
                dv = dhm[band]
                da_ref[band, cols] = (dv * bv * (sg * (1.0 + av * (1.0 - sg)))).astype(BF16)
                db_ref[band, cols] = (dv * (av * sg)).astype(BF16)
        half = pl.ds(pl.multiple_of(j * fh, GROUP), fh)
        dx_ref[...] += _dot(da_ref[...], w1s[half, :], NN) + _dot(db_ref[...], w3s[half, :], NN)

        @pl.when(j == halves - 1)
        def _():
            xv = x_ref[...]
            g = g_ref[...]
            r = lax.rsqrt(jnp.mean(xv * xv, axis=-1, keepdims=True) + EPS)
            nh = xv * r
            n_ref[...] = (nh * g).astype(BF16)
            total = dx_ref[...]
            dg_ref[...] += jnp.sum(total * nh, axis=0, keepdims=True)
            dnh = total * g
            dx_ref[...] = dh_ref[...] + r * (dnh - nh * jnp.mean(dnh * nh, axis=-1, keepdims=True))

    tile = pl.BlockSpec((tm, d), lambda i, j: (i, 0))
    row = pl.BlockSpec((1, d), lambda i, j: (0, 0))
    wide = pl.BlockSpec((tm, fh), lambda i, j: (i, j))
    return _grid_call(
        body,
        carries,
        name=name,
        grid=(s // tm, halves),
        in_specs=[tile, tile, wide, wide, row, ANY, ANY, ANY],
        out_specs=[tile, wide, wide, tile, row],
        out_shape=[
            jax.ShapeDtypeStruct((s, d), F32),
            jax.ShapeDtypeStruct((s, ffn), BF16),
            jax.ShapeDtypeStruct((s, ffn), BF16),
            jax.ShapeDtypeStruct((s, d), BF16),
            jax.ShapeDtypeStruct((1, d), F32),
        ],
        scratch_shapes=[pltpu.VMEM((ffn, d), BF16)] * 3 + [pltpu.SemaphoreType.DMA((3 * NDEV,))],
        args=[dh, x, a, b, gain] + list(weights),
    )


SWAP_PIECES = 2


def _wgrad(lhs, rhs, scale, name, carries=()):
    s, m = lhs.shape
    n = rhs.shape[1]
    rs = m // NDEV
    tk = min(1024, s)
    steps = s // tk
    pieces = [(j, at, size) for j in range(2) for at, size in _pieces(rs, SWAP_PIECES)]

    def body(l_ref, r_ref, o_ref, acc, mine, theirs, send_sems, recv_sems):
        h, k = pl.program_id(0), pl.program_id(1)

        @pl.when(k == 0)
        def _():
            acc[...] = _dot(l_ref[...], r_ref[...].astype(BF16), TN)

        @pl.when(k > 0)
        def _():
            acc[...] += _dot(l_ref[...], r_ref[...].astype(BF16), TN)

        def exchange(half):
            c = lax.axis_index("c")
            return [
                _remote(mine.at[half, 1 - c, j, pl.ds(at, size), :], theirs.at[half, j, pl.ds(at, size), :],
                        send_sems.at[half * len(pieces) + q], recv_sems.at[half * len(pieces) + q], _peer(1))
                for q, (j, at, size) in enumerate(pieces)
            ]

        def settle(half):
            for cp in exchange(half):
                cp.wait_recv()
            both = mine[half, lax.axis_index("c")].astype(F32) + theirs[half].astype(F32)
            o_ref[2 * half:2 * half + 2] = both.astype(BF16)
            for cp in exchange(half):
                cp.wait_send()

        for half in range(2):
            @pl.when((h == half) & (k == steps - 1))
            def _():
                for p in range(NCHIP):
                    mine[half, p % 2, p // 2] = (acc[p * rs:(p + 1) * rs, :] * scale).astype(BF16)
                for cp in exchange(half):
                    cp.start()
                if half == 1:
                    settle(0)
                    settle(1)

    (out,), carried = _grid_call(
        body,
        carries,
        name=name,
        grid=(2, steps),
        in_specs=[pl.BlockSpec((tk, m // 2), lambda h, k: (k, h)), pl.BlockSpec((tk, n), lambda h, k: (k, 0))],
        out_specs=[pl.BlockSpec((NCHIP, rs, n), lambda h, k: (0, 0, 0))],
        out_shape=[jax.ShapeDtypeStruct((NCHIP, rs, n), BF16)],
        scratch_shapes=[
            pltpu.VMEM((m // 2, n), F32), pltpu.VMEM((2, 2, 2, rs, n), BF16), pltpu.VMEM((2, 2, rs, n), BF16),
            pltpu.SemaphoreType.DMA((2 * len(pieces),)), pltpu.SemaphoreType.DMA((2 * len(pieces),)),
        ],
        args=[lhs, rhs],
    )
    return out, carried


def _mix_constants(s):
    c = GROUP
    lg = np.log1p(-np.exp2(-5.0 - np.arange(RET_HEADS, dtype=np.float32))).astype(np.float32)
    pos = np.arange(c, dtype=np.float32)
    rel = pos[:, None] - pos[None, :]
    decay = np.where(rel[None] >= 0, np.exp(lg[:, None, None] * np.maximum(rel, 0.0)[None]), 0.0).astype(np.float32)
    ktail = np.exp(lg[:, None] * (c - 1 - pos)[None, :]).astype(np.float32)
    qhead = np.exp(lg[:, None] * (pos + 1.0)[None, :]).astype(np.float32)
    chunk_decay = [float(v) for v in np.exp(lg * np.float32(c)).astype(np.float32)]
    ones = np.ones((1, 1, c), np.float32)
    inv_freq = (1.0 / (np.float32(ROPE_BASE) ** (np.arange(0, c, 2, dtype=np.float32) / np.float32(c)))).astype(np.float32)
    ang = (np.arange(s, dtype=np.float32)[:, None] * inv_freq[None, :]).astype(np.float32)
    cos, sin = np.cos(ang).astype(np.float32), np.sin(ang).astype(np.float32)
    return dict(
        decay=jnp.asarray(decay),
        ktail=jnp.asarray(ktail[:, :, None] * ones),
        qhead=jnp.asarray(qhead[:, :, None] * ones),
        chunk_decay=chunk_decay,
        cos=jnp.asarray(np.concatenate([cos, cos], axis=-1)),
        sin=jnp.asarray(np.concatenate([-sin, sin], axis=-1)),
    )


def _rope(t, cos, sin):
    return t * cos + pltpu.roll(t, GROUP // 2, axis=1) * sin


def _rope_bwd(dt, cos, sin):
    return dt * cos + pltpu.roll(dt * sin, GROUP // 2, axis=1)


def _window_sums(ext, w, forward):
    rows = ext.shape[0]
    acc, k = ext, 1
    while k < w:
        acc = acc + pltpu.roll(acc, k if forward else rows - k, axis=0)
        k *= 2
    return acc


def _pool_counts(tile, tm, w):
    t = lax.broadcasted_iota(jnp.int32, (tm, 1), 0) + tile * tm
    return jnp.minimum(t + 1, w).astype(F32)


def _mix_fwd(h1, gain, weights, pool_w, pool_scale, ret_gain, consts, carries=()):
    s, d = h1.shape
    pwid = N_POOL_GROUPS * GROUP
    rwid = RET_HEADS * GROUP
    inw = pwid + 4 * rwid
    tm = min(256, s)
    nck = tm // GROUP
    cd = consts["chunk_decay"]

    def body(h_ref, g_ref, bin_, bout, pw_ref, ps_ref, rg_ref, cos_ref, sin_ref, dec_ref, kt_ref, qh_ref,
             h2_ref, proj_ref, o_ref, rs_ref, wins, wouts, state, carry, mbuf, sems):
        i = pl.program_id(0)

        @pl.when(i == 0)
        def _():
            for cp in _load_weights(((bin_, wins), (bout, wouts)), sems):
                cp.wait()
            state[...] = jnp.zeros_like(state)
            carry[...] = jnp.zeros_like(carry)

        hv = h_ref[...]
        r = lax.rsqrt(jnp.mean(hv * hv, axis=-1, keepdims=True) + EPS)
        u = (hv * r * g_ref[...]).astype(BF16)
        proj_ref[...] = _dot(u, wins[...], NT)

        ext = jnp.concatenate([carry[...], proj_ref[:, 0:pwid]], axis=0)
        carry[...] = proj_ref[tm - MAX_WINDOW:tm, 0:pwid]
        for gi, w in enumerate(POOL_WINDOWS):
            cols = slice(gi * GROUP, (gi + 1) * GROUP)
            xg = ext[:, cols]
            ws = _window_sums(xg, w, True)[MAX_WINDOW:, :]
            pooled = ws / _pool_counts(i, tm, w) - xg[MAX_WINDOW:, :]
            z = _dot(pooled.astype(BF16), pw_ref[gi].astype(BF16), NN)
            mbuf[:, cols] = (z * ps_ref[:, cols]).astype(BF16)

        cos, sin = cos_ref[...], sin_ref[...]
        for h in range(RET_HEADS):
            cq = slice(pwid + h * GROUP, pwid + (h + 1) * GROUP)
            ck = slice(pwid + rwid + h * GROUP, pwid + rwid + (h + 1) * GROUP)
            cv = slice(pwid + 2 * rwid + h * GROUP, pwid + 2 * rwid + (h + 1) * GROUP)
            cg = slice(pwid + 3 * rwid + h * GROUP, pwid + 3 * rwid + (h + 1) * GROUP)
            ch = slice(h * GROUP, (h + 1) * GROUP)
            qr = _rope(proj_ref[:, cq], cos, sin)
            kr = _rope(proj_ref[:, ck], cos, sin) * (GROUP ** -0.5)
            vb = proj_ref[:, cv].astype(BF16)
            for n in range(nck):
                rows = slice(n * GROUP, (n + 1) * GROUP)
                qc, kc, vc = qr[rows], kr[rows], vb[rows]
                rb = state[h]
                rs_ref[n, h] = rb
                p = (_dot(qc.astype(BF16), kc.astype(BF16), NT) * dec_ref[h]).astype(BF16)
                o = _dot(p, vc, NN) + _dot((qc * qh_ref[h]).astype(BF16), rb.astype(BF16), NN)
                state[h] = cd[h] * rb + _dot((kc * kt_ref[h]).astype(BF16), vc, TN)
                o_ref[rows, ch] = o
                on = o * lax.rsqrt(jnp.mean(o * o, axis=-1, keepdims=True) + EPS)
                gv = proj_ref[rows, cg]
                mbuf[rows, pwid + h * GROUP:pwid + (h + 1) * GROUP] = (
                    gv * _sigmoid(gv) * (on * rg_ref[:, ch])
                ).astype(BF16)
        h2_ref[...] = hv + _dot(mbuf[...], wouts[...], NN)

    tile = pl.BlockSpec((tm, d), lambda i: (i, 0))
    full = lambda shape: pl.BlockSpec(shape, lambda i: (0,) * len(shape))
    return _grid_call(
        body,
        carries,
        name="mix_fwd",
        grid=(s // tm,),
        in_specs=[
            tile, full((1, d)), ANY, ANY,
            full((N_POOL_GROUPS, GROUP, GROUP)), full((1, pwid)), full((1, rwid)),
            pl.BlockSpec((tm, GROUP), lambda i: (i, 0)), pl.BlockSpec((tm, GROUP), lambda i: (i, 0)),
            full((RET_HEADS, GROUP, GROUP)), full((RET_HEADS, GROUP, GROUP)), full((RET_HEADS, GROUP, GROUP)),
        ],
        out_specs=[
            tile,
            pl.BlockSpec((tm, inw), lambda i: (i, 0)),
            pl.BlockSpec((tm, rwid), lambda i: (i, 0)),
            pl.BlockSpec((nck, RET_HEADS, GROUP, GROUP), lambda i: (i, 0, 0, 0)),
        ],
        out_shape=[
            jax.ShapeDtypeStruct((s, d), F32),
            jax.ShapeDtypeStruct((s, inw), F32),
            jax.ShapeDtypeStruct((s, rwid), F32),
            jax.ShapeDtypeStruct((s // GROUP, RET_HEADS, GROUP, GROUP), F32),
        ],
        scratch_shapes=[
            pltpu.VMEM((inw, d), BF16), pltpu.VMEM((d, d), BF16),
            pltpu.VMEM((RET_HEADS, GROUP, GROUP), F32), pltpu.VMEM((MAX_WINDOW, pwid), F32),
            pltpu.VMEM((tm, d), BF16), pltpu.SemaphoreType.DMA((2 * NDEV,)),
        ],
        args=[h1, gain, weights[0], weights[1], pool_w, pool_scale, ret_gain,
              consts["cos"], consts["sin"], consts["decay"], consts["ktail"], consts["qhead"]],
    )


def _mix_bwd(dh2, h1, proj, o_saved, rsave, gain, weights, pool_w, pool_scale, ret_gain, consts, carries=()):
    s, d = h1.shape
    pwid = N_POOL_GROUPS * GROUP
    rwid = RET_HEADS * GROUP
    inw = pwid + 4 * rwid
    tm = min(256, s)
    nck = tm // GROUP
    nt = s // tm
    cd = consts["chunk_decay"]
    halo_per_tile = tm // MAX_WINDOW

    def body(dh2_ref, h_ref, proj_ref, halo_ref, o_ref, rs_ref, g_ref, bin_, bout, pw_ref, ps_ref, rg_ref,
             cos_ref, sin_ref, dec_ref, kt_ref, qh_ref,
             dh1_ref, dproj_ref, u_ref, m_ref, dpw_ref, dps_ref, drg_ref, dg_ref,
             wins, wouts, dstate, carry, dm, dpj, sems):
        i = pl.program_id(0)
        tile = nt - 1 - i

        @pl.when(i == 0)
        def _():
            for cp in _load_weights(((bin_, wins), (bout, wouts)), sems):
                cp.wait()
            dstate[...] = jnp.zeros_like(dstate)
            carry[...] = jnp.zeros_like(carry)
            for ref in (dpw_ref, dps_ref, drg_ref, dg_ref):
                ref[...] = jnp.zeros_like(ref)

        dh2v = dh2_ref[...]
        dm[...] = _dot(dh2v.astype(BF16), wouts[...], NT)
        hv = h_ref[...]
        g = g_ref[...]
        r = lax.rsqrt(jnp.mean(hv * hv, axis=-1, keepdims=True) + EPS)
        uh = hv * r
        u_ref[...] = (uh * g).astype(BF16)

        halo = jnp.where(tile == 0, 0.0, halo_ref[...])
        ext = jnp.concatenate([halo, proj_ref[:, 0:pwid]], axis=0)
        next_dpn = carry[...]
        for gi, w in enumerate(POOL_WINDOWS):
            cols = slice(gi * GROUP, (gi + 1) * GROUP)
            xg = ext[:, cols]
            cnt = _pool_counts(tile, tm, w)
            pooled = (_window_sums(xg, w, True)[MAX_WINDOW:, :] / cnt - xg[MAX_WINDOW:, :]).astype(BF16)
            pwb = pw_ref[gi].astype(BF16)
            z = _dot(pooled, pwb, NN)
            scale = ps_ref[:, cols]
            m_ref[:, cols] = (z * scale).astype(BF16)
            da = dm[:, cols]
            dps_ref[:, cols] += jnp.sum(da * z, axis=0, keepdims=True)
            dz = (da * scale).astype(BF16)
            dpw_ref[gi] += _dot(pooled, dz, TN)
            dpl = _dot(dz, pwb, NT)
            dpn = dpl / cnt
            ext2 = jnp.concatenate([dpn, next_dpn[:, cols]], axis=0)
            dpj[:, cols] = (_window_sums(ext2, w, False)[0:tm, :] - dpl).astype(BF16)
            carry[:, cols] = dpn[0:MAX_WINDOW, :]

        cos, sin = cos_ref[...], sin_ref[...]
        for h in range(RET_HEADS):
            cq = slice(pwid + h * GROUP, pwid + (h + 1) * GROUP)
            ck = slice(pwid + rwid + h * GROUP, pwid + rwid + (h + 1) * GROUP)
            cv = slice(pwid + 2 * rwid + h * GROUP, pwid + 2 * rwid + (h + 1) * GROUP)
            cg = slice(pwid + 3 * rwid + h * GROUP, pwid + 3 * rwid + (h + 1) * GROUP)
            ch = slice(h * GROUP, (h + 1) * GROUP)
            qr = _rope(proj_ref[:, cq], cos, sin)
            kr = _rope(proj_ref[:, ck], cos, sin) * (GROUP ** -0.5)
            vb = proj_ref[:, cv].astype(BF16)
            gv = proj_ref[:, cg]
            ov = o_ref[:, ch]
            ro = lax.rsqrt(jnp.mean(ov * ov, axis=-1, keepdims=True) + EPS)
            on = ov * ro
            rg = rg_ref[:, ch]
            db = dm[:, pwid + h * GROUP:pwid + (h + 1) * GROUP]
            sg = _sigmoid(gv)
            sl = gv * sg
            m_ref[:, pwid + h * GROUP:pwid + (h + 1) * GROUP] = (sl * (on * rg)).astype(BF16)
            dpj[:, cg] = (db * (on * rg) * (sg * (1.0 + gv * (1.0 - sg)))).astype(BF16)
            drg_ref[:, ch] += jnp.sum(db * sl * on, axis=0, keepdims=True)
            don = db * sl * rg
            do = (ro * (don - on * jnp.mean(don * on, axis=-1, keepdims=True))).astype(BF16)
            for n in reversed(range(nck)):
                rows = slice(n * GROUP, (n + 1) * GROUP)
                qc, kc, vc, dob = qr[rows], kr[rows], vb[rows], do[rows]
                qcb, kcb = qc.astype(BF16), kc.astype(BF16)
                qh = (qc * qh_ref[h]).astype(BF16)
                kt = (kc * kt_ref[h]).astype(BF16)
                rn = rs_ref[n, h].astype(BF16)
                dnext = dstate[h]
                dnb = dnext.astype(BF16)
                dec = dec_ref[h]
                p = (_dot(qcb, kcb, NT) * dec).astype(BF16)
                ds = (_dot(dob, vc, NT) * dec).astype(BF16)
                dv = _dot(p, dob, TN) + _dot(kt, dnb, NN)
                dq = _dot(ds, kcb, NN) + _dot(dob, rn, NT) * qh_ref[h]
                dk = _dot(ds, qcb, TN) + _dot(vc, dnb, NT) * kt_ref[h]
                dstate[h] = cd[h] * dnext + _dot(qh, dob, TN)
                dpj[rows, cq] = _rope_bwd(dq, cos[rows], sin[rows]).astype(BF16)
                dpj[rows, ck] = _rope_bwd(dk * (GROUP ** -0.5), cos[rows], sin[rows]).astype(BF16)
                dpj[rows, cv] = dv.astype(BF16)

        dproj_ref[...] = dpj[...]
        du = _dot(dpj[...], wins[...], NN)
        dg_ref[...] += jnp.sum(du * uh, axis=0, keepdims=True)
        dn = du * g
        dh1_ref[...] = dh2v + r * (dn - uh * jnp.mean(dn * uh, axis=-1, keepdims=True))

    rev = lambda i: (nt - 1 - i, 0)
    tile = pl.BlockSpec((tm, d), rev)
    full = lambda shape: pl.BlockSpec(shape, lambda i: (0,) * len(shape))
    return _grid_call(
        body,
        carries,
        name="mix_bwd",
        grid=(nt,),
        in_specs=[
            tile, tile,
            pl.BlockSpec((tm, inw), rev),
            pl.BlockSpec((MAX_WINDOW, pwid), lambda i: (jnp.maximum((nt - 1 - i) * halo_per_tile - 1, 0), 0)),
            pl.BlockSpec((tm, rwid), rev),
            pl.BlockSpec((nck, RET_HEADS, GROUP, GROUP), lambda i: (nt - 1 - i, 0, 0, 0)),
            full((1, d)), ANY, ANY,
            full((N_POOL_GROUPS, GROUP, GROUP)), full((1, pwid)), full((1, rwid)),
            pl.BlockSpec((tm, GROUP), rev), pl.BlockSpec((tm, GROUP), rev),
            full((RET_HEADS, GROUP, GROUP)), full((RET_HEADS, GROUP, GROUP)), full((RET_HEADS, GROUP, GROUP)),
        ],
        out_specs=[
            tile, pl.BlockSpec((tm, inw), rev), tile, tile,
            full((N_POOL_GROUPS, GROUP, GROUP)), full((1, pwid)), full((1, rwid)), full((1, d)),
        ],
        out_shape=[
            jax.ShapeDtypeStruct((s, d), F32),
            jax.ShapeDtypeStruct((s, inw), BF16),
            jax.ShapeDtypeStruct((s, d), BF16),
            jax.ShapeDtypeStruct((s, d), BF16),
            jax.ShapeDtypeStruct((N_POOL_GROUPS, GROUP, GROUP), F32),
            jax.ShapeDtypeStruct((1, pwid), F32),
            jax.ShapeDtypeStruct((1, rwid), F32),
            jax.ShapeDtypeStruct((1, d), F32),
        ],
        scratch_shapes=[
            pltpu.VMEM((inw, d), BF16), pltpu.VMEM((d, d), BF16),
            pltpu.VMEM((RET_HEADS, GROUP, GROUP), F32), pltpu.VMEM((MAX_WINDOW, pwid), F32),
            pltpu.VMEM((tm, d), F32), pltpu.VMEM((tm, inw), BF16), pltpu.SemaphoreType.DMA((2 * NDEV,)),
        ],
        args=[dh2, h1, proj, proj, o_saved, rsave, gain, weights[0], weights[1], pool_w, pool_scale, ret_gain,
              consts["cos"], consts["sin"], consts["decay"], consts["ktail"], consts["qhead"]],
    )


def _adam(w, g, m, v):
    m = ADAM_B1 * m + (1.0 - ADAM_B1) * g
    v = ADAM_B2 * v + (1.0 - ADAM_B2) * jnp.square(g)
    m_hat = m / (1.0 - ADAM_B1 ** ADAM_STEP)
    v_hat = v / (1.0 - ADAM_B2 ** ADAM_STEP)
    delta = -ADAM_LR * (m_hat / (jnp.sqrt(v_hat) + ADAM_EPS) + ADAM_WD * w)
    return delta, m, v


def _adamw_big(w, parts, m, v, name):
    rows, d = w.shape
    tr = _row_tile(rows, 176)

    def body(w_ref, p_ref, m_ref, v_ref, g_ref, d_ref, nm_ref, nv_ref):
        g = p_ref[0].astype(F32)
        for q in range(1, NCHIP):
            g = g + p_ref[q].astype(F32)
        g_ref[...] = g
        d_ref[...], nm_ref[...], nv_ref[...] = _adam(w_ref[...], g, m_ref[...], v_ref[...])

    spec = pl.BlockSpec((tr, d), lambda i: (i, 0))
    return _call(
        body,
        name=name,
        grid=(rows // tr,),
        in_specs=[spec, pl.BlockSpec((NCHIP, tr, d), lambda i: (0, i, 0)), spec, spec],
        out_specs=[spec] * 4,
        out_shape=[jax.ShapeDtypeStruct((rows, d), F32)] * 4,
        compiler_params=_seq(1),
    )(w, parts, m, v)


def _adamw_small(stats_all, pw_all, ws, ms, vs, pwid):
    nsmall = len(ws)

    def body(*refs):
        st_ref, pwa_ref = refs[0], refs[1]
        w_refs = refs[2:2 + nsmall]
        m_refs = refs[2 + nsmall:2 + 2 * nsmall]
        v_refs = refs[2 + 2 * nsmall:2 + 3 * nsmall]
        outs = refs[2 + 3 * nsmall:]
        st = st_ref[0]
        pwg = pwa_ref[0]
        for q in range(1, NDEV):
            st = st + st_ref[q]
            pwg = pwg + pwa_ref[q]
        grads = [st[0:1, :], st[1:2, :], st[2:3, :], st[3:4, :], st[4:5, 0:pwid], st[4:5, pwid:2 * pwid], pwg]
        outs[0][...] = jnp.zeros((1, GROUP), F32) + jnp.sum(st[5:6, :])
        for j in range(nsmall):
            delta, nm, nv = _adam(w_refs[j][...], grads[j], m_refs[j][...], v_refs[j][...])
            outs[1 + 4 * j][...] = grads[j]
            outs[2 + 4 * j][...] = delta
            outs[3 + 4 * j][...] = nm
            outs[4 + 4 * j][...] = nv

    out_shape = [jax.ShapeDtypeStruct((1, GROUP), F32)]
    for w in ws:
        out_shape += [jax.ShapeDtypeStruct(w.shape, F32)] * 4
    return _call(body, name="adamw_small", out_shape=out_shape, compiler_params=_params())(
        stats_all, pw_all, *ws, *ms, *vs
    )


def kernel(x, ffn1_norm, ffn1_w1, ffn1_w3, ffn1_w2, mix_norm, w_in, pool_w, pool_scale, ret_norm, w_out, ffn2_norm, ffn2_w1, ffn2_w3, ffn2_w2, final_norm, loss_target, m_ffn1_norm, m_ffn1_w1, m_ffn1_w3, m_ffn1_w2, m_mix_norm, m_w_in, m_pool_w, m_pool_scale, m_ret_norm, m_w_out, m_ffn2_norm, m_ffn2_w1, m_ffn2_w3, m_ffn2_w2, m_final_norm, v_ffn1_norm, v_ffn1_w1, v_ffn1_w3, v_ffn1_w2, v_mix_norm, v_w_in, v_pool_w, v_pool_scale, v_ret_norm, v_w_out, v_ffn2_norm, v_ffn2_w1, v_ffn2_w3, v_ffn2_w2, v_final_norm):
    s, d = x.shape[1], x.shape[2]
    ffn = ffn1_w1.shape[2] * NDEV
    pwid = pool_scale.shape[1]
    xs, tgt = x[0], loss_target[0]
    consts = _mix_constants(s)
    pw3 = pool_w[0]
    fnorm = final_norm.reshape(1, d)

    rows_of = lambda w, transposed: (w[0].T if transposed else w[0]).astype(BF16)
    send_f1 = [rows_of(ffn1_w1, True), rows_of(ffn1_w3, True), rows_of(ffn1_w2, False)]
    send_mix = [rows_of(w_in, True), rows_of(w_out, False)]
    send_f2 = [rows_of(ffn2_w1, True), rows_of(ffn2_w3, True), rows_of(ffn2_w2, False)]

    (w_f1,) = _comm_call([_Gather(send_f1)], "gather_ffn1")
    (h1, a1, b1, hm1), (more,) = _ffn_fwd(xs, ffn1_norm, w_f1, ffn, carries=[_Gather(send_mix + send_f2[:1])])
    w_mix = more[:2]
    (h2, proj, o_saved, rsave), (rest,) = _mix_fwd(
        h1, mix_norm, w_mix, pw3, pool_scale, ret_norm, consts, carries=[_Gather(send_f2[1:])]
    )
    w_f2 = more[2:] + rest
    (dh3, a2, b2, hm2, dgf, loss_cols), _ = _ffn_fwd(h2, ffn2_norm, w_f2, ffn, head=(fnorm, tgt))

    (dh2, da2, db2, n2, dg2), _ = _ffn_bwd(dh3, h2, a2, b2, ffn2_norm, w_f2, ffn, "ffn2_bwd")
    sum_f2w1, _ = _wgrad(da2, n2, 1.0, "ffn2_w1_grad")
    sum_f2w3, _ = _wgrad(db2, n2, 1.0, "ffn2_w3_grad")
    sum_f2w2, ((parts_f2w1,),) = _wgrad(hm2, dh3, 0.5, "ffn2_w2_grad", carries=[_ChipScatter([sum_f2w1])])

    (dh1, dproj, u, mm, dpw, dps, drg, dgm), ((parts_f2w3, parts_f2w2),) = _mix_bwd(
        dh2, h1, proj, o_saved, rsave, mix_norm, w_mix, pw3, pool_scale, ret_norm, consts,
        carries=[_ChipScatter([sum_f2w3, sum_f2w2])],
    )
    (dx, da1, db1, n1, dg1), _ = _ffn_bwd(dh1, xs, a1, b1, ffn1_norm, w_f1, ffn, "ffn1_bwd")
    stats = jnp.concatenate(
        [dg1, dgm, dg2, dgf, jnp.concatenate([dps, drg], axis=1), loss_cols, jnp.zeros((2, d), F32)], axis=0
    )
    small = _GatherDirect([stats, dpw.reshape(N_POOL_GROUPS * GROUP, GROUP)])
    sum_f1w2, ((stats_all, pw_all),) = _wgrad(hm1, dh1, 0.5, "ffn1_w2_grad", carries=[small])
    sum_f1w1, ((parts_f1w2,),) = _wgrad(da1, n1, 1.0, "ffn1_w1_grad", carries=[_ChipScatter([sum_f1w2])])
    sum_f1w3, ((parts_f1w1,),) = _wgrad(db1, n1, 1.0, "ffn1_w3_grad", carries=[_ChipScatter([sum_f1w1])])
    sum_in, ((parts_f1w3,),) = _wgrad(dproj, u, 1.0, "w_in_grad", carries=[_ChipScatter([sum_f1w3])])
    sum_out, ((parts_in,),) = _wgrad(mm, dh2, 1.0, "w_out_grad", carries=[_ChipScatter([sum_in])])
    ((parts_out,),) = _comm_call([_ChipScatter([sum_out])], "scatter_last")

    big = (
        (ffn1_w1, m_ffn1_w1, v_ffn1_w1, parts_f1w1, True),
        (ffn1_w3, m_ffn1_w3, v_ffn1_w3, parts_f1w3, True),
        (ffn1_w2, m_ffn1_w2, v_ffn1_w2, parts_f1w2, False),
        (w_in, m_w_in, v_w_in, parts_in, True),
        (w_out, m_w_out, v_w_out, parts_out, False),
        (ffn2_w1, m_ffn2_w1, v_ffn2_w1, parts_f2w1, True),
        (ffn2_w3, m_ffn2_w3, v_ffn2_w3, parts_f2w3, True),
        (ffn2_w2, m_ffn2_w2, v_ffn2_w2, parts_f2w2, False),
    )
    big_out = []
    for j, (w, m, v, parts, t) in enumerate(big):
        view = (lambda a: a[0].T) if t else (lambda a: a[0])
        back = (lambda a: a.T[None]) if t else (lambda a: a[None])
        big_out.append([back(a) for a in _adamw_big(view(w), parts, view(m), view(v), "adamw_%d" % j)])

    small_w = (ffn1_norm, mix_norm, ffn2_norm, fnorm, pool_scale, ret_norm, pw3.reshape(-1, GROUP))
    small_m = (m_ffn1_norm, m_mix_norm, m_ffn2_norm, m_final_norm.reshape(1, d), m_pool_scale, m_ret_norm, m_pool_w.reshape(-1, GROUP))
    small_v = (v_ffn1_norm, v_mix_norm, v_ffn2_norm, v_final_norm.reshape(1, d), v_pool_scale, v_ret_norm, v_pool_w.reshape(-1, GROUP))
    res = _adamw_small(stats_all, pw_all, small_w, small_m, small_v, pwid)
    loss = res[0][0, 0]
    small_out = [list(res[1 + 4 * j:5 + 4 * j]) for j in range(len(small_w))]
    small_out[3] = [a.reshape(d) for a in small_out[3]]
    small_out[6] = [a.reshape(pool_w.shape) for a in small_out[6]]

    order = [small_out[0], big_out[0], big_out[1], big_out[2], small_out[1], big_out[3], small_out[6], small_out[4],
             small_out[5], big_out[4], small_out[2], big_out[5], big_out[6], big_out[7], small_out[3]]
    result = [loss, dx[None]]
    for kind in range(4):
        result += [t[kind] for t in order]
    return tuple(result)
```

```python
import functools

import numpy as np
import jax
import jax.numpy as jnp
from jax import lax
from jax.experimental import pallas as pl
from jax.experimental.pallas import tpu as pltpu

F32 = jnp.float32
BF16 = jnp.bfloat16

NDEV = 8
NCHIP = 4
EPS = 1e-6
N_POOL_GROUPS = 4
POOL_WINDOWS = (2, 4, 8, 16)
MAX_WINDOW = 16
GROUP = 128
RET_HEADS = 4
ROPE_BASE = 10000.0
ADAM_LR = 0.001
ADAM_B1 = 0.9
ADAM_B2 = 0.999
ADAM_EPS = 1e-08
ADAM_WD = 0.01
ADAM_STEP = 10

VMEM_LIMIT = 56 * 1024 * 1024
FFN_CHUNK = 256
ROW_BAND = 32

NT = (((1,), (1,)), ((), ()))
NN = (((1,), (0,)), ((), ()))
TN = (((0,), (0,)), ((), ()))

ANY = pl.BlockSpec(memory_space=pl.ANY)


def _dot(a, b, dims):
    return lax.dot_general(a, b, dims, preferred_element_type=F32)


def _call(body, **kw):
    return pl.pallas_call(body, **kw)


def _params(**kw):
    return pltpu.CompilerParams(vmem_limit_bytes=VMEM_LIMIT, **kw)


def _seq(n):
    return _params(dimension_semantics=("arbitrary",) * n)


def _peer(k):
    x, y, c = lax.axis_index("x"), lax.axis_index("y"), lax.axis_index("c")
    return (1 - x if k & 4 else x, 1 - y if k & 2 else y, 1 - c if k & 1 else c)


def _flat(pos):
    return 4 * pos[0] + 2 * pos[1] + pos[2]


def _chip(pos):
    return 2 * pos[0] + pos[1]


def _row_tile(rows, cap):
    return max(t for t in range(16, min(rows, cap) + 1, 16) if rows % t == 0)


def _pieces(rows, n):
    tiles = rows // 16
    cuts = [16 * (tiles * q // n) for q in range(n + 1)]
    return [(a, b - a) for a, b in zip(cuts[:-1], cuts[1:])]


def _load_weights(parts, sems):
    copies = []
    for buf, dst in parts:
        rows = buf.shape[1]
        for p in range(NDEV):
            cp = pltpu.make_async_copy(buf.at[p], dst.at[pl.ds(p * rows, rows), :], sems.at[len(copies)])
            cp.start()
            copies.append(cp)
    return copies


def _sigmoid(a):
    return 1.0 / (1.0 + jnp.exp(-a))


def _remote(src, dst, send_sem, recv_sem, to):
    return pltpu.make_async_remote_copy(
        src_ref=src, dst_ref=dst, send_sem=send_sem, recv_sem=recv_sem, device_id=to, device_id_type=pl.DeviceIdType.MESH
    )


class _Gather:
    X, Y, FAR = 4, 2, 6
    COPIES = 8

    def __init__(self, shards):
        n = len(shards)
        self.operands = list(shards)
        self.out_shape = [jax.ShapeDtypeStruct((NDEV,) + a.shape, a.dtype) for a in shards]
        self.sems = [
            pltpu.SemaphoreType.DMA((self.COPIES * n,)), pltpu.SemaphoreType.DMA((self.COPIES * n,)),
            pltpu.SemaphoreType.DMA((n,)),
        ]
        self.stages = [self.begin, self.relay, self.relay_far, self.end]

    def _copy(self, t, k, block, to, ins, outs, sems, own=False, half=None):
        rows = outs[t].shape[1]
        part = pl.ds(0, rows) if half is None else pl.ds(half * (rows // 2), rows // 2)
        dst = outs[t].at[_flat(block), part, :]
        at = self.COPIES * t + k
        return _remote(ins[t] if own else dst, dst, sems[0].at[at], sems[1].at[at], to)

    def _local(self, t, ins, outs, sems):
        return pltpu.make_async_copy(ins[t], outs[t].at[_flat(_peer(0))], sems[2].at[t])

    def begin(self, ins, outs, sems):
        me = _peer(0)
        for t in range(len(ins)):
            self._local(t, ins, outs, sems).start()
            for k, code in enumerate((1, self.X, self.Y)):
                self._copy(t, k, me, _peer(code), ins, outs, sems, own=True).start()

    def relay(self, ins, outs, sems):
        me, sibling = _peer(0), _peer(1)
        for t in range(len(ins)):
            self._copy(t, 1, _peer(self.X), me, ins, outs, sems).wait_recv()
            self._copy(t, 3, _peer(self.X), _peer(self.Y), ins, outs, sems, half=0).start()
            self._copy(t, 5, _peer(self.X), sibling, ins, outs, sems).start()
            self._copy(t, 2, _peer(self.Y), me, ins, outs, sems).wait_recv()
            self._copy(t, 4, _peer(self.Y), _peer(self.X), ins, outs, sems, half=1).start()
            self._copy(t, 6, _peer(self.Y), sibling, ins, outs, sems).start()

    def relay_far(self, ins, outs, sems):
        me, sibling = _peer(0), _peer(1)
        for t in range(len(ins)):
            self._copy(t, 3, _peer(self.FAR), me, ins, outs, sems, half=0).wait_recv()
            self._copy(t, 4, _peer(self.FAR), me, ins, outs, sems, half=1).wait_recv()
            self._copy(t, 7, _peer(self.FAR), sibling, ins, outs, sems).start()

    def end(self, ins, outs, sems):
        me = _peer(0)
        for t in range(len(ins)):
            self._copy(t, 0, _peer(1), me, ins, outs, sems).wait_recv()
            for k, code in ((5, self.X), (6, self.Y), (7, self.FAR)):
                self._copy(t, k, _peer(code ^ 1), me, ins, outs, sems).wait_recv()
            for k in range(self.COPIES):
                self._copy(t, k, me, me, ins, outs, sems, half=0 if k == 3 else 1 if k == 4 else None).wait_send()
            self._local(t, ins, outs, sems).wait()


class _GatherDirect:
    def __init__(self, arrays):
        n = len(arrays)
        self.operands = list(arrays)
        self.out_shape = [jax.ShapeDtypeStruct((NDEV,) + a.shape, a.dtype) for a in arrays]
        self.sems = [pltpu.SemaphoreType.DMA((7 * n,)), pltpu.SemaphoreType.DMA((7 * n,)), pltpu.SemaphoreType.DMA((n,))]
        self.stages = [self.begin, self.end]

    def begin(self, ins, outs, sems):
        mine = _flat(_peer(0))
        for t in range(len(ins)):
            pltpu.make_async_copy(ins[t], outs[t].at[mine], sems[2].at[t]).start()
            for k in range(1, NDEV):
                _remote(ins[t], outs[t].at[mine], sems[0].at[7 * t + k - 1], sems[1].at[7 * t + k - 1], _peer(k)).start()

    def end(self, ins, outs, sems):
        mine = _flat(_peer(0))
        for t in range(len(ins)):
            for k in range(1, NDEV):
                cp = _remote(ins[t], outs[t].at[_flat(_peer(k))], sems[0].at[7 * t + k - 1], sems[1].at[7 * t + k - 1], _peer(k))
                cp.wait_recv()
                cp.wait_send()
            pltpu.make_async_copy(ins[t], outs[t].at[mine], sems[2].at[t]).wait()


class _ChipScatter:
    pieces = 1

    def __init__(self, sums):
        n = len(sums) * NCHIP * self.pieces
        self.operands = list(sums)
        self.out_shape = [jax.ShapeDtypeStruct(a.shape, a.dtype) for a in sums]
        self.sems = [pltpu.SemaphoreType.DMA((n,)), pltpu.SemaphoreType.DMA((n,))]
        self.stages = [self.begin, self.end]

    def _copies(self, ins, outs, sems, arriving):
        mine = _chip(_peer(0))
        copies = []
        for t in range(len(ins)):
            rows = ins[t].shape[1] // self.pieces
            for k in (0, 4, 2, 6):
                other = _chip(_peer(k))
                for q in range(self.pieces):
                    part = pl.ds(q * rows, rows)
                    at = len(copies)
                    if k == 0:
                        cp = pltpu.make_async_copy(ins[t].at[mine, part, :], outs[t].at[mine, part, :], sems[0].at[at])
                    else:
                        landing = outs[t].at[other if arriving else mine, part, :]
                        cp = _remote(ins[t].at[other, part, :], landing, sems[0].at[at], sems[1].at[at], _peer(k))
                    copies.append(cp)
        return copies

    def begin(self, ins, outs, sems):
        for cp in self._copies(ins, outs, sems, False):
            cp.start()

    def end(self, ins, outs, sems):
        for at, cp in enumerate(self._copies(ins, outs, sems, True)):
            if at % (NCHIP * self.pieces) < self.pieces:
                cp.wait()
            else:
                cp.wait_recv()
                cp.wait_send()


def _split_refs(refs, counts):
    out, at = [], 0
    for n in counts:
        out.append(refs[at:at + n])
        at += n
    return out


def _comm_call(carries, name):
    nin = [len(c.operands) for c in carries]
    nout = [len(c.out_shape) for c in carries]
    nsem = [len(c.sems) for c in carries]

    def body(*refs):
        ins, outs, sems = _split_refs(refs, (sum(nin), sum(nout), sum(nsem)))
        parts = list(zip(carries, _split_refs(ins, nin), _split_refs(outs, nout), _split_refs(sems, nsem)))
        for depth in range(max(len(c.stages) for c in carries)):
            for c, i, o, s in parts:
                if depth < len(c.stages) - 1:
                    c.stages[depth](i, o, s)
        for c, i, o, s in parts:
            c.stages[-1](i, o, s)

    res = _call(
        body,
        name=name,
        out_shape=[sh for c in carries for sh in c.out_shape],
        in_specs=[ANY] * sum(nin),
        out_specs=[ANY] * sum(nout),
        scratch_shapes=[sm for c in carries for sm in c.sems],
        compiler_params=pltpu.CompilerParams(has_side_effects=True),
    )(*[a for c in carries for a in c.operands])
    return _split_refs(list(res), nout)


def _grid_call(body, carries, *, name, grid, in_specs, out_specs, out_shape, scratch_shapes, args):
    ni, no, ns = len(in_specs), len(out_specs), len(scratch_shapes)
    nin = [len(c.operands) for c in carries]
    nout = [len(c.out_shape) for c in carries]
    nsem = [len(c.sems) for c in carries]
    steps = int(np.prod(grid))

    def when_of(stage, count):
        first, last = (5 * steps) // 8 - 1, steps - 2
        return max(0, last if count <= 3 else first + (last - first) * (stage - 1) // (count - 3))

    def wrapped(*refs):
        ins, cins, outs, couts, scr, csems = _split_refs(refs, (ni, sum(nin), no, sum(nout), ns, sum(nsem)))
        if not carries:
            return body(*ins, *outs, *scr)
        parts = list(zip(carries, _split_refs(cins, nin), _split_refs(couts, nout), _split_refs(csems, nsem)))
        step = pl.program_id(0)
        for axis in range(1, len(grid)):
            step = step * grid[axis] + pl.program_id(axis)

        @pl.when(step == 0)
        def _():
            for c, i, o, s in parts:
                c.stages[0](i, o, s)

        body(*ins, *outs, *scr)

        for c, i, o, s in parts:
            for stage in range(1, len(c.stages) - 1):
                pl.when(step == when_of(stage, len(c.stages)))(functools.partial(c.stages[stage], i, o, s))

        @pl.when(step == steps - 1)
        def _():
            for c, i, o, s in parts:
                c.stages[-1](i, o, s)

    res = _call(
        wrapped,
        name=name,
        grid=tuple(grid),
        in_specs=list(in_specs) + [ANY] * sum(nin),
        out_specs=list(out_specs) + [ANY] * sum(nout),
        out_shape=list(out_shape) + [sh for c in carries for sh in c.out_shape],
        scratch_shapes=list(scratch_shapes) + [sm for c in carries for sm in c.sems],
        compiler_params=_seq(len(grid)),
    )(*args, *[a for c in carries for a in c.operands])
    res = list(res)
    return res[:no], _split_refs(res[no:], nout)


def _chunks(width):
    return [(at, min(FFN_CHUNK, width - at)) for at in range(0, width, FFN_CHUNK)]


def _start_chunk_loads(bufs, dsts, sems, chunks):
    copies = [[None] * len(chunks) for _ in bufs]
    for c, (at, width) in enumerate(chunks):
        for m, (buf, dst) in enumerate(zip(bufs, dsts)):
            cp = pltpu.make_async_copy(
                buf.at[pl.ds(at, width), :], dst.at[pl.ds(at, width), :], sems.at[m * len(chunks) + c]
            )
            cp.start()
            copies[m][c] = cp
    return copies


def _ffn_fwd(x, gain, weights, ffn, head=None, carries=()):
    s, d = x.shape
    tm = min(512, s)
    chunks = _chunks(ffn)

    def body(*refs):
        if head is None:
            x_ref, g_ref, b1, b3, b2, h_ref, a_ref, b_ref, hm_ref, w1s, w3s, w2s, sems = refs
        else:
            x_ref, g_ref, b1, b3, b2, gf_ref, t_ref, h_ref, a_ref, b_ref, hm_ref, dgf_ref, loss_ref, w1s, w3s, w2s, sems = refs

        def one_tile(loads):
            xv = x_ref[...]
            r = lax.rsqrt(jnp.mean(xv * xv, axis=-1, keepdims=True) + EPS)
            n = (xv * r * g_ref[...]).astype(BF16)
            acc = jnp.zeros((tm, d), F32)
            for c, (at, width) in enumerate(chunks):
                cols = slice(at, at + width)
                if loads is not None:
                    for m in range(3):
                        loads[m][c].wait()
                a = _dot(n, w1s[cols, :], NT)
                b = _dot(n, w3s[cols, :], NT)
                a_ref[:, cols] = a.astype(BF16)
                b_ref[:, cols] = b.astype(BF16)
                hm = (a * _sigmoid(a) * b).astype(BF16)
                hm_ref[:, cols] = hm
                acc = acc + _dot(hm, w2s[cols, :], NN)
            h = xv + 0.5 * acc
            if head is None:
                h_ref[...] = h
            else:
                rf = lax.rsqrt(jnp.mean(h * h, axis=-1, keepdims=True) + EPS)
                nh = h * rf
                gf = gf_ref[...]
                err = nh * gf - t_ref[...]
                loss_ref[...] += jnp.sum(err * err, axis=0, keepdims=True) * (0.5 / d)
                dy = err * (1.0 / d)
                dgf_ref[...] += jnp.sum(dy * nh, axis=0, keepdims=True)
                dn = dy * gf
                h_ref[...] = rf * (dn - nh * jnp.mean(dn * nh, axis=-1, keepdims=True))

        if head is None:
            @pl.when(pl.program_id(0) == 0)
            def _():
                one_tile(_start_chunk_loads((b1, b3, b2), (w1s, w3s, w2s), sems, chunks))

            @pl.when(pl.program_id(0) > 0)
            def _():
                one_tile(None)
        else:
            @pl.when(pl.program_id(0) == 0)
            def _():
                dgf_ref[...] = jnp.zeros_like(dgf_ref)
                loss_ref[...] = jnp.zeros_like(loss_ref)
                for loads in _start_chunk_loads((b1, b3, b2), (w1s, w3s, w2s), sems, chunks):
                    for cp in loads:
                        cp.wait()

            one_tile(None)

    tile = pl.BlockSpec((tm, d), lambda i: (i, 0))
    row = pl.BlockSpec((1, d), lambda i: (0, 0))
    wide = pl.BlockSpec((tm, ffn), lambda i: (i, 0))
    in_specs = [tile, row, ANY, ANY, ANY]
    out_shape = [jax.ShapeDtypeStruct((s, d), F32)] + [jax.ShapeDtypeStruct((s, ffn), BF16)] * 3
    out_specs = [tile, wide, wide, wide]
    args = [x, gain] + [w.reshape(ffn, d) for w in weights]
    if head is not None:
        in_specs += [row, tile]
        args += list(head)
        out_shape += [jax.ShapeDtypeStruct((1, d), F32)] * 2
        out_specs += [row, row]
    return _grid_call(
        body,
        carries,
        name="ffn_fwd_loss" if head is not None else "ffn_fwd",
        grid=(s // tm,),
        in_specs=in_specs,
        out_specs=out_specs,
        out_shape=out_shape,
        scratch_shapes=[pltpu.VMEM((ffn, d), BF16)] * 3 + [pltpu.SemaphoreType.DMA((3 * len(chunks),))],
        args=args,
    )


def _ffn_bwd(dh, x, a, b, gain, weights, ffn, name, carries=()):
    s, d = x.shape
    tm = min(512, s)
    halves = 2
    fh = ffn // halves

    def body(dh_ref, x_ref, a_ref, b_ref, g_ref, b1, b3, b2, dx_ref, da_ref, db_ref, n_ref, dg_ref, w1s, w3s, w2s, sems):
        i, j = pl.program_id(0), pl.program_id(1)

        @pl.when((i == 0) & (j == 0))
        def _():
            for cp in _load_weights(((b1, w1s), (b3, w3s), (b2, w2s)), sems):
                cp.wait()
            dg_ref[...] = jnp.zeros_like(dg_ref)

        @pl.when(j == 0)
        def _():
            dx_ref[...] = jnp.zeros_like(dx_ref)

        dob = (0.5 * dh_ref[...]).astype(BF16)
        chunks = _chunks(fh)

        def dhm_of(k):
            at, width = chunks[k]
            return _dot(dob, w2s[pl.ds(pl.multiple_of(j * fh + at, GROUP), width), :], NT)

        ahead = dhm_of(0)
        for k, (at, width) in enumerate(chunks):
            cols = slice(at, at + width)
            dhm = ahead
            if k + 1 < len(chunks):
                ahead = dhm_of(k + 1)
            for top in range(0, tm, ROW_BAND):
                band = slice(top, top + ROW_BAND)
                av = a_ref[band, cols].astype(F32)
                bv = b_ref[band, cols].astype(F32)
                sg = _sigmoid(av)
                dv = dhm[band]
                da_ref[band, cols] = (dv * bv * (sg * (1.0 + av * (1.0 - sg)))).astype(BF16)
                db_ref[band, cols] = (dv * (av * sg)).astype(BF16)
        half = pl.ds(pl.multiple_of(j * fh, GROUP), fh)
        dx_ref[...] += _dot(da_ref[...], w1s[half, :], NN) + _dot(db_ref[...], w3s[half, :], NN)

        @pl.when(j == halves - 1)
        def _():
            xv = x_ref[...]
            g = g_ref[...]
            r = lax.rsqrt(jnp.mean(xv * xv, axis=-1, keepdims=True) + EPS)
            nh = xv * r
            n_ref[...] = (nh * g).astype(BF16)
            total = dx_ref[...]
            dg_ref[...] += jnp.sum(total * nh, axis=0, keepdims=True)
            dnh = total * g
            dx_ref[...] = dh_ref[...] + r * (dnh - nh * jnp.mean(dnh * nh, axis=-1, keepdims=True))

    tile = pl.BlockSpec((tm, d), lambda i, j: (i, 0))
    row = pl.BlockSpec((1, d), lambda i, j: (0, 0))
    wide = pl.BlockSpec((tm, fh), lambda i, j: (i, j))
    return _grid_call(
        body,
        carries,
        name=name,
        grid=(s // tm, halves),
        in_specs=[tile, tile, wide, wide, row, ANY, ANY, ANY],
        out_specs=[tile, wide, wide, tile, row],
        out_shape=[
            jax.ShapeDtypeStruct((s, d), F32),
            jax.ShapeDtypeStruct((s, ffn), BF16),
            jax.ShapeDtypeStruct((s, ffn), BF16),
            jax.ShapeDtypeStruct((s, d), BF16),
            jax.ShapeDtypeStruct((1, d), F32),
        ],
        scratch_shapes=[pltpu.VMEM((ffn, d), BF16)] * 3 + [pltpu.SemaphoreType.DMA((3 * NDEV,))],
        args=[dh, x, a, b, gain] + list(weights),
    )


SWAP_PIECES = 1


def _wgrad(lhs, rhs, scale, name, carries=()):
    s, m = lhs.shape
    n = rhs.shape[1]
    rs = m // NDEV
    tk = min(1024, s)
    steps = s // tk
    pieces = [(j, at, size) for j in range(2) for at, size in _pieces(rs, SWAP_PIECES)]

    def body(l_ref, r_ref, o_ref, acc, mine, theirs, send_sems, recv_sems):
        h, k = pl.program_id(0), pl.program_id(1)

        @pl.when(k == 0)
        def _():
            acc[...] = _dot(l_ref[...], r_ref[...].astype(BF16), TN)

        @pl.when(k > 0)
        def _():
            acc[...] += _dot(l_ref[...], r_ref[...].astype(BF16), TN)

        def exchange(half):
            c = lax.axis_index("c")
            return [
                _remote(mine.at[half, 1 - c, j, pl.ds(at, size), :], theirs.at[half, j, pl.ds(at, size), :],
                        send_sems.at[half * len(pieces) + q], recv_sems.at[half * len(pieces) + q], _peer(1))
                for q, (j, at, size) in enumerate(pieces)
            ]

        def settle(half):
            for cp in exchange(half):
                cp.wait_recv()
            both = mine[half, lax.axis_index("c")].astype(F32) + theirs[half].astype(F32)
            o_ref[2 * half:2 * half + 2] = both.astype(BF16)
            for cp in exchange(half):
                cp.wait_send()

        for half in range(2):
            @pl.when((h == half) & (k == steps - 1))
            def _():
                for p in range(NCHIP):
                    mine[half, p % 2, p // 2] = (acc[p * rs:(p + 1) * rs, :] * scale).astype(BF16)
                for cp in exchange(half):
                    cp.start()
                if half == 1:
                    settle(0)
                    settle(1)

    (out,), carried = _grid_call(
        body,
        carries,
        name=name,
        grid=(2, steps),
        in_specs=[pl.BlockSpec((tk, m // 2), lambda h, k: (k, h)), pl.BlockSpec((tk, n), lambda h, k: (k, 0))],
        out_specs=[pl.BlockSpec((NCHIP, rs, n), lambda h, k: (0, 0, 0))],
        out_shape=[jax.ShapeDtypeStruct((NCHIP, rs, n), BF16)],
        scratch_shapes=[
            pltpu.VMEM((m // 2, n), F32), pltpu.VMEM((2, 2, 2, rs, n), BF16), pltpu.VMEM((2, 2, rs, n), BF16),
            pltpu.SemaphoreType.DMA((2 * len(pieces),)), pltpu.SemaphoreType.DMA((2 * len(pieces),)),
        ],
        args=[lhs, rhs],
    )
    return out, carried


def _mix_constants(s):
    c = GROUP
    lg = np.log1p(-np.exp2(-5.0 - np.arange(RET_HEADS, dtype=np.float32))).astype(np.float32)
    pos = np.arange(c, dtype=np.float32)
    rel = pos[:, None] - pos[None, :]
    decay = np.where(rel[None] >= 0, np.exp(lg[:, None, None] * np.maximum(rel, 0.0)[None]), 0.0).astype(np.float32)
    ktail = np.exp(lg[:, None] * (c - 1 - pos)[None, :]).astype(np.float32)
    qhead = np.exp(lg[:, None] * (pos + 1.0)[None, :]).astype(np.float32)
    chunk_decay = [float(v) for v in np.exp(lg * np.float32(c)).astype(np.float32)]
    ones = np.ones((1, 1, c), np.float32)
    inv_freq = (1.0 / (np.float32(ROPE_BASE) ** (np.arange(0, c, 2, dtype=np.float32) / np.float32(c)))).astype(np.float32)
    ang = (np.arange(s, dtype=np.float32)[:, None] * inv_freq[None, :]).astype(np.float32)
    cos, sin = np.cos(ang).astype(np.float32), np.sin(ang).astype(np.float32)
    return dict(
        decay=jnp.asarray(decay),
        ktail=jnp.asarray(ktail[:, :, None] * ones),
        qhead=jnp.asarray(qhead[:, :, None] * ones),
        chunk_decay=chunk_decay,
        cos=jnp.asarray(np.concatenate([cos, cos], axis=-1)),
        sin=jnp.asarray(np.concatenate([-sin, sin], axis=-1)),
    )


def _rope(t, cos, sin):
    return t * cos + pltpu.roll(t, GROUP // 2, axis=1) * sin


def _rope_bwd(dt, cos, sin):
    return dt * cos + pltpu.roll(dt * sin, GROUP // 2, axis=1)


def _window_sums(ext, w, forward):
    rows = ext.shape[0]
    acc, k = ext, 1
    while k < w:
        acc = acc + pltpu.roll(acc, k if forward else rows - k, axis=0)
        k *= 2
    return acc


def _pool_counts(tile, tm, w):
    t = lax.broadcasted_iota(jnp.int32, (tm, 1), 0) + tile * tm
    return jnp.minimum(t + 1, w).astype(F32)


def _mix_fwd(h1, gain, weights, pool_w, pool_scale, ret_gain, consts, carries=()):
    s, d = h1.shape
    pwid = N_POOL_GROUPS * GROUP
    rwid = RET_HEADS * GROUP
    inw = pwid + 4 * rwid
    tm = min(256, s)
    nck = tm // GROUP
    cd = consts["chunk_decay"]

    def body(h_ref, g_ref, bin_, bout, pw_ref, ps_ref, rg_ref, cos_ref, sin_ref, dec_ref, kt_ref, qh_ref,
             h2_ref, proj_ref, o_ref, rs_ref, wins, wouts, state, carry, mbuf, sems):
        i = pl.program_id(0)

        @pl.when(i == 0)
        def _():
            for cp in _load_weights(((bin_, wins), (bout, wouts)), sems):
                cp.wait()
            state[...] = jnp.zeros_like(state)
            carry[...] = jnp.zeros_like(carry)

        hv = h_ref[...]
        r = lax.rsqrt(jnp.mean(hv * hv, axis=-1, keepdims=True) + EPS)
        u = (hv * r * g_ref[...]).astype(BF16)
        proj_ref[...] = _dot(u, wins[...], NT)

        ext = jnp.concatenate([carry[...], proj_ref[:, 0:pwid]], axis=0)
        carry[...] = proj_ref[tm - MAX_WINDOW:tm, 0:pwid]
        for gi, w in enumerate(POOL_WINDOWS):
            cols = slice(gi * GROUP, (gi + 1) * GROUP)
            xg = ext[:, cols]
            ws = _window_sums(xg, w, True)[MAX_WINDOW:, :]
            pooled = ws / _pool_counts(i, tm, w) - xg[MAX_WINDOW:, :]
            z = _dot(pooled.astype(BF16), pw_ref[gi].astype(BF16), NN)
            mbuf[:, cols] = (z * ps_ref[:, cols]).astype(BF16)

        cos, sin = cos_ref[...], sin_ref[...]
        for h in range(RET_HEADS):
            cq = slice(pwid + h * GROUP, pwid + (h + 1) * GROUP)
            ck = slice(pwid + rwid + h * GROUP, pwid + rwid + (h + 1) * GROUP)
            cv = slice(pwid + 2 * rwid + h * GROUP, pwid + 2 * rwid + (h + 1) * GROUP)
            cg = slice(pwid + 3 * rwid + h * GROUP, pwid + 3 * rwid + (h + 1) * GROUP)
            ch = slice(h * GROUP, (h + 1) * GROUP)
            qr = _rope(proj_ref[:, cq], cos, sin)
            kr = _rope(proj_ref[:, ck], cos, sin) * (GROUP ** -0.5)
            vb = proj_ref[:, cv].astype(BF16)
            for n in range(nck):
                rows = slice(n * GROUP, (n + 1) * GROUP)
                qc, kc, vc = qr[rows], kr[rows], vb[rows]
                rb = state[h]
                rs_ref[n, h] = rb
                p = (_dot(qc.astype(BF16), kc.astype(BF16), NT) * dec_ref[h]).astype(BF16)
                o = _dot(p, vc, NN) + _dot((qc * qh_ref[h]).astype(BF16), rb.astype(BF16), NN)
                state[h] = cd[h] * rb + _dot((kc * kt_ref[h]).astype(BF16), vc, TN)
                o_ref[rows, ch] = o
                on = o * lax.rsqrt(jnp.mean(o * o, axis=-1, keepdims=True) + EPS)
                gv = proj_ref[rows, cg]
                mbuf[rows, pwid + h * GROUP:pwid + (h + 1) * GROUP] = (
                    gv * _sigmoid(gv) * (on * rg_ref[:, ch])
                ).astype(BF16)
        h2_ref[...] = hv + _dot(mbuf[...], wouts[...], NN)

    tile = pl.BlockSpec((tm, d), lambda i: (i, 0))
    full = lambda shape: pl.BlockSpec(shape, lambda i: (0,) * len(shape))
    return _grid_call(
        body,
        carries,
        name="mix_fwd",
        grid=(s // tm,),
        in_specs=[
            tile, full((1, d)), ANY, ANY,
            full((N_POOL_GROUPS, GROUP, GROUP)), full((1, pwid)), full((1, rwid)),
            pl.BlockSpec((tm, GROUP), lambda i: (i, 0)), pl.BlockSpec((tm, GROUP), lambda i: (i, 0)),
            full((RET_HEADS, GROUP, GROUP)), full((RET_HEADS, GROUP, GROUP)), full((RET_HEADS, GROUP, GROUP)),
        ],
        out_specs=[
            tile,
            pl.BlockSpec((tm, inw), lambda i: (i, 0)),
            pl.BlockSpec((tm, rwid), lambda i: (i, 0)),
            pl.BlockSpec((nck, RET_HEADS, GROUP, GROUP), lambda i: (i, 0, 0, 0)),
        ],
        out_shape=[
            jax.ShapeDtypeStruct((s, d), F32),
            jax.ShapeDtypeStruct((s, inw), F32),
            jax.ShapeDtypeStruct((s, rwid), F32),
            jax.ShapeDtypeStruct((s // GROUP, RET_HEADS, GROUP, GROUP), F32),
        ],
        scratch_shapes=[
            pltpu.VMEM((inw, d), BF16), pltpu.VMEM((d, d), BF16),
            pltpu.VMEM((RET_HEADS, GROUP, GROUP), F32), pltpu.VMEM((MAX_WINDOW, pwid), F32),
            pltpu.VMEM((tm, d), BF16), pltpu.SemaphoreType.DMA((2 * NDEV,)),
        ],
        args=[h1, gain, weights[0], weights[1], pool_w, pool_scale, ret_gain,
              consts["cos"], consts["sin"], consts["decay"], consts["ktail"], consts["qhead"]],
    )


def _mix_bwd(dh2, h1, proj, o_saved, rsave, gain, weights, pool_w, pool_scale, ret_gain, consts, carries=()):
    s, d = h1.shape
    pwid = N_POOL_GROUPS * GROUP
    rwid = RET_HEADS * GROUP
    inw = pwid + 4 * rwid
    tm = min(256, s)
    nck = tm // GROUP
    nt = s // tm
    cd = consts["chunk_decay"]
    halo_per_tile = tm // MAX_WINDOW

    def body(dh2_ref, h_ref, proj_ref, halo_ref, o_ref, rs_ref, g_ref, bin_, bout, pw_ref, ps_ref, rg_ref,
             cos_ref, sin_ref, dec_ref, kt_ref, qh_ref,
             dh1_ref, dproj_ref, u_ref, m_ref, dpw_ref, dps_ref, drg_ref, dg_ref,
             wins, wouts, dstate, carry, dm, dpj, sems):
        i = pl.program_id(0)
        tile = nt - 1 - i

        @pl.when(i == 0)
        def _():
            for cp in _load_weights(((bin_, wins), (bout, wouts)), sems):
                cp.wait()
            dstate[...] = jnp.zeros_like(dstate)
            carry[...] = jnp.zeros_like(carry)
            for ref in (dpw_ref, dps_ref, drg_ref, dg_ref):
                ref[...] = jnp.zeros_like(ref)

        dh2v = dh2_ref[...]
        dm[...] = _dot(dh2v.astype(BF16), wouts[...], NT)
        hv = h_ref[...]
        g = g_ref[...]
        r = lax.rsqrt(jnp.mean(hv * hv, axis=-1, keepdims=True) + EPS)
        uh = hv * r
        u_ref[...] = (uh * g).astype(BF16)

        halo = jnp.where(tile == 0, 0.0, halo_ref[...])
        ext = jnp.concatenate([halo, proj_ref[:, 0:pwid]], axis=0)
        next_dpn = carry[...]
        for gi, w in enumerate(POOL_WINDOWS):
            cols = slice(gi * GROUP, (gi + 1) * GROUP)
            xg = ext[:, cols]
            cnt = _pool_counts(tile, tm, w)
            pooled = (_window_sums(xg, w, True)[MAX_WINDOW:, :] / cnt - xg[MAX_WINDOW:, :]).astype(BF16)
            pwb = pw_ref[gi].astype(BF16)
            z = _dot(pooled, pwb, NN)
            scale = ps_ref[:, cols]
            m_ref[:, cols] = (z * scale).astype(BF16)
            da = dm[:, cols]
            dps_ref[:, cols] += jnp.sum(da * z, axis=0, keepdims=True)
            dz = (da * scale).astype(BF16)
            dpw_ref[gi] += _dot(pooled, dz, TN)
            dpl = _dot(dz, pwb, NT)
            dpn = dpl / cnt
            ext2 = jnp.concatenate([dpn, next_dpn[:, cols]], axis=0)
            dpj[:, cols] = (_window_sums(ext2, w, False)[0:tm, :] - dpl).astype(BF16)
            carry[:, cols] = dpn[0:MAX_WINDOW, :]

        cos, sin = cos_ref[...], sin_ref[...]
        for h in range(RET_HEADS):
            cq = slice(pwid + h * GROUP, pwid + (h + 1) * GROUP)
            ck = slice(pwid + rwid + h * GROUP, pwid + rwid + (h + 1) * GROUP)
            cv = slice(pwid + 2 * rwid + h * GROUP, pwid + 2 * rwid + (h + 1) * GROUP)
            cg = slice(pwid + 3 * rwid + h * GROUP, pwid + 3 * rwid + (h + 1) * GROUP)
            ch = slice(h * GROUP, (h + 1) * GROUP)
            qr = _rope(proj_ref[:, cq], cos, sin)
            kr = _rope(proj_ref[:, ck], cos, sin) * (GROUP ** -0.5)
            vb = proj_ref[:, cv].astype(BF16)
            gv = proj_ref[:, cg]
            ov = o_ref[:, ch]
            ro = lax.rsqrt(jnp.mean(ov * ov, axis=-1, keepdims=True) + EPS)
            on = ov * ro
            rg = rg_ref[:, ch]
            db = dm[:, pwid + h * GROUP:pwid + (h + 1) * GROUP]
            sg = _sigmoid(gv)
            sl = gv * sg
            m_ref[:, pwid + h * GROUP:pwid + (h + 1) * GROUP] = (sl * (on * rg)).astype(BF16)
            dpj[:, cg] = (db * (on * rg) * (sg * (1.0 + gv * (1.0 - sg)))).astype(BF16)
            drg_ref[:, ch] += jnp.sum(db * sl * on, axis=0, keepdims=True)
            don = db * sl * rg
            do = (ro * (don - on * jnp.mean(don * on, axis=-1, keepdims=True))).astype(BF16)
            for n in reversed(range(nck)):
                rows = slice(n * GROUP, (n + 1) * GROUP)
                qc, kc, vc, dob = qr[rows], kr[rows], vb[rows], do[rows]
                qcb, kcb = qc.astype(BF16), kc.astype(BF16)
                qh = (qc * qh_ref[h]).astype(BF16)
                kt = (kc * kt_ref[h]).astype(BF16)
                rn = rs_ref[n, h].astype(BF16)
                dnext = dstate[h]
                dnb = dnext.astype(BF16)
                dec = dec_ref[h]
                p = (_dot(qcb, kcb, NT) * dec).astype(BF16)
                ds = (_dot(dob, vc, NT) * dec).astype(BF16)
                dv = _dot(p, dob, TN) + _dot(kt, dnb, NN)
                dq = _dot(ds, kcb, NN) + _dot(dob, rn, NT) * qh_ref[h]
                dk = _dot(ds, qcb, TN) + _dot(vc, dnb, NT) * kt_ref[h]
                dstate[h] = cd[h] * dnext + _dot(qh, dob, TN)
                dpj[rows, cq] = _rope_bwd(dq, cos[rows], sin[rows]).astype(BF16)
                dpj[rows, ck] = _rope_bwd(dk * (GROUP ** -0.5), cos[rows], sin[rows]).astype(BF16)
                dpj[rows, cv] = dv.astype(BF16)

        dproj_ref[...] = dpj[...]
        du = _dot(dpj[...], wins[...], NN)
        dg_ref[...] += jnp.sum(du * uh, axis=0, keepdims=True)
        dn = du * g
        dh1_ref[...] = dh2v + r * (dn - uh * jnp.mean(dn * uh, axis=-1, keepdims=True))

    rev = lambda i: (nt - 1 - i, 0)
    tile = pl.BlockSpec((tm, d), rev)
    full = lambda shape: pl.BlockSpec(shape, lambda i: (0,) * len(shape))
    return _grid_call(
        body,
        carries,
        name="mix_bwd",
        grid=(nt,),
        in_specs=[
            tile, tile,
            pl.BlockSpec((tm, inw), rev),
            pl.BlockSpec((MAX_WINDOW, pwid), lambda i: (jnp.maximum((nt - 1 - i) * halo_per_tile - 1, 0), 0)),
            pl.BlockSpec((tm, rwid), rev),
            pl.BlockSpec((nck, RET_HEADS, GROUP, GROUP), lambda i: (nt - 1 - i, 0, 0, 0)),
            full((1, d)), ANY, ANY,
            full((N_POOL_GROUPS, GROUP, GROUP)), full((1, pwid)), full((1, rwid)),
            pl.BlockSpec((tm, GROUP), rev), pl.BlockSpec((tm, GROUP), rev),
            full((RET_HEADS, GROUP, GROUP)), full((RET_HEADS, GROUP, GROUP)), full((RET_HEADS, GROUP, GROUP)),
        ],
        out_specs=[
            tile, pl.BlockSpec((tm, inw), rev), tile, tile,
            full((N_POOL_GROUPS, GROUP, GROUP)), full((1, pwid)), full((1, rwid)), full((1, d)),
        ],
        out_shape=[
            jax.ShapeDtypeStruct((s, d), F32),
            jax.ShapeDtypeStruct((s, inw), BF16),
            jax.ShapeDtypeStruct((s, d), BF16),
            jax.ShapeDtypeStruct((s, d), BF16),
            jax.ShapeDtypeStruct((N_POOL_GROUPS, GROUP, GROUP), F32),
            jax.ShapeDtypeStruct((1, pwid), F32),
            jax.ShapeDtypeStruct((1, rwid), F32),
            jax.ShapeDtypeStruct((1, d), F32),
        ],
        scratch_shapes=[
            pltpu.VMEM((inw, d), BF16), pltpu.VMEM((d, d), BF16),
            pltpu.VMEM((RET_HEADS, GROUP, GROUP), F32), pltpu.VMEM((MAX_WINDOW, pwid), F32),
            pltpu.VMEM((tm, d), F32), pltpu.VMEM((tm, inw), BF16), pltpu.SemaphoreType.DMA((2 * NDEV,)),
        ],
        args=[dh2, h1, proj, proj, o_saved, rsave, gain, weights[0], weights[1], pool_w, pool_scale, ret_gain,
              consts["cos"], consts["sin"], consts["decay"], consts["ktail"], consts["qhead"]],
    )


def _adam(w, g, m, v):
    m = ADAM_B1 * m + (1.0 - ADAM_B1) * g
    v = ADAM_B2 * v + (1.0 - ADAM_B2) * jnp.square(g)
    m_hat = m / (1.0 - ADAM_B1 ** ADAM_STEP)
    v_hat = v / (1.0 - ADAM_B2 ** ADAM_STEP)
    delta = -ADAM_LR * (m_hat / (jnp.sqrt(v_hat) + ADAM_EPS) + ADAM_WD * w)
    return delta, m, v


def _adamw_big(w, parts, m, v, name):
    rows, d = w.shape
    tr = _row_tile(rows, 176)

    def body(w_ref, p_ref, m_ref, v_ref, g_ref, d_ref, nm_ref, nv_ref):
        g = p_ref[0].astype(F32)
        for q in range(1, NCHIP):
            g = g + p_ref[q].astype(F32)
        g_ref[...] = g
        d_ref[...], nm_ref[...], nv_ref[...] = _adam(w_ref[...], g, m_ref[...], v_ref[...])

    spec = pl.BlockSpec((tr, d), lambda i: (i, 0))
    return _call(
        body,
        name=name,
        grid=(rows // tr,),
        in_specs=[spec, pl.BlockSpec((NCHIP, tr, d), lambda i: (0, i, 0)), spec, spec],
        out_specs=[spec] * 4,
        out_shape=[jax.ShapeDtypeStruct((rows, d), F32)] * 4,
        compiler_params=_seq(1),
    )(w, parts, m, v)


def _adamw_small(stats_all, pw_all, ws, ms, vs, pwid):
    nsmall = len(ws)

    def body(*refs):
        st_ref, pwa_ref = refs[0], refs[1]
        w_refs = refs[2:2 + nsmall]
        m_refs = refs[2 + nsmall:2 + 2 * nsmall]
        v_refs = refs[2 + 2 * nsmall:2 + 3 * nsmall]
        outs = refs[2 + 3 * nsmall:]
        st = st_ref[0]
        pwg = pwa_ref[0]
        for q in range(1, NDEV):
            st = st + st_ref[q]
            pwg = pwg + pwa_ref[q]
        grads = [st[0:1, :], st[1:2, :], st[2:3, :], st[3:4, :], st[4:5, 0:pwid], st[4:5, pwid:2 * pwid], pwg]
        outs[0][...] = jnp.zeros((1, GROUP), F32) + jnp.sum(st[5:6, :])
        for j in range(nsmall):
            delta, nm, nv = _adam(w_refs[j][...], grads[j], m_refs[j][...], v_refs[j][...])
            outs[1 + 4 * j][...] = grads[j]
            outs[2 + 4 * j][...] = delta
            outs[3 + 4 * j][...] = nm
            outs[4 + 4 * j][...] = nv

    out_shape = [jax.ShapeDtypeStruct((1, GROUP), F32)]
    for w in ws:
        out_shape += [jax.ShapeDtypeStruct(w.shape, F32)] * 4
    return _call(body, name="adamw_small", out_shape=out_shape, compiler_params=_params())(
        stats_all, pw_all, *ws, *ms, *vs
    )


def kernel(x, ffn1_norm, ffn1_w1, ffn1_w3, ffn1_w2, mix_norm, w_in, pool_w, pool_scale, ret_norm, w_out, ffn2_norm, ffn2_w1, ffn2_w3, ffn2_w2, final_norm, loss_target, m_ffn1_norm, m_ffn1_w1, m_ffn1_w3, m_ffn1_w2, m_mix_norm, m_w_in, m_pool_w, m_pool_scale, m_ret_norm, m_w_out, m_ffn2_norm, m_ffn2_w1, m_ffn2_w3, m_ffn2_w2, m_final_norm, v_ffn1_norm, v_ffn1_w1, v_ffn1_w3, v_ffn1_w2, v_mix_norm, v_w_in, v_pool_w, v_pool_scale, v_ret_norm, v_w_out, v_ffn2_norm, v_ffn2_w1, v_ffn2_w3, v_ffn2_w2, v_final_norm):
    s, d = x.shape[1], x.shape[2]
    ffn = ffn1_w1.shape[2] * NDEV
    pwid = pool_scale.shape[1]
    xs, tgt = x[0], loss_target[0]
    consts = _mix_constants(s)
    pw3 = pool_w[0]
    fnorm = final_norm.reshape(1, d)

    rows_of = lambda w, transposed: (w[0].T if transposed else w[0]).astype(BF16)
    send_f1 = [rows_of(ffn1_w1, True), rows_of(ffn1_w3, True), rows_of(ffn1_w2, False)]
    send_mix = [rows_of(w_in, True), rows_of(w_out, False)]
    send_f2 = [rows_of(ffn2_w1, True), rows_of(ffn2_w3, True), rows_of(ffn2_w2, False)]

    (w_f1,) = _comm_call([_Gather(send_f1)], "gather_ffn1")
    (h1, a1, b1, hm1), (more,) = _ffn_fwd(xs, ffn1_norm, w_f1, ffn, carries=[_Gather(send_mix + send_f2[:1])])
    w_mix = more[:2]
    (h2, proj, o_saved, rsave), (rest,) = _mix_fwd(
        h1, mix_norm, w_mix, pw3, pool_scale, ret_norm, consts, carries=[_Gather(send_f2[1:])]
    )
    w_f2 = more[2:] + rest
    (dh3, a2, b2, hm2, dgf, loss_cols), _ = _ffn_fwd(h2, ffn2_norm, w_f2, ffn, head=(fnorm, tgt))

    (dh2, da2, db2, n2, dg2), _ = _ffn_bwd(dh3, h2, a2, b2, ffn2_norm, w_f2, ffn, "ffn2_bwd")
    sum_f2w1, _ = _wgrad(da2, n2, 1.0, "ffn2_w1_grad")
    sum_f2w3, _ = _wgrad(db2, n2, 1.0, "ffn2_w3_grad")
    sum_f2w2, ((parts_f2w1,),) = _wgrad(hm2, dh3, 0.5, "ffn2_w2_grad", carries=[_ChipScatter([sum_f2w1])])

    (dh1, dproj, u, mm, dpw, dps, drg, dgm), ((parts_f2w3, parts_f2w2),) = _mix_bwd(
        dh2, h1, proj, o_saved, rsave, mix_norm, w_mix, pw3, pool_scale, ret_norm, consts,
        carries=[_ChipScatter([sum_f2w3, sum_f2w2])],
    )
    (dx, da1, db1, n1, dg1), _ = _ffn_bwd(dh1, xs, a1, b1, ffn1_norm, w_f1, ffn, "ffn1_bwd")
    stats = jnp.concatenate(
        [dg1, dgm, dg2, dgf, jnp.concatenate([dps, drg], axis=1), loss_cols, jnp.zeros((2, d), F32)], axis=0
    )
    small = _GatherDirect([stats, dpw.reshape(N_POOL_GROUPS * GROUP, GROUP)])
    sum_f1w2, ((stats_all, pw_all),) = _wgrad(hm1, dh1, 0.5, "ffn1_w2_grad", carries=[small])
    sum_f1w1, ((parts_f1w2,),) = _wgrad(da1, n1, 1.0, "ffn1_w1_grad", carries=[_ChipScatter([sum_f1w2])])
    sum_f1w3, ((parts_f1w1,),) = _wgrad(db1, n1, 1.0, "ffn1_w3_grad", carries=[_ChipScatter([sum_f1w1])])
    sum_in, ((parts_f1w3,),) = _wgrad(dproj, u, 1.0, "w_in_grad", carries=[_ChipScatter([sum_f1w3])])
    sum_out, ((parts_in,),) = _wgrad(mm, dh2, 1.0, "w_out_grad", carries=[_ChipScatter([sum_in])])
    ((parts_out,),) = _comm_call([_ChipScatter([sum_out])], "scatter_last")

    big = (
        (ffn1_w1, m_ffn1_w1, v_ffn1_w1, parts_f1w1, True),
        (ffn1_w3, m_ffn1_w3, v_ffn1_w3, parts_f1w3, True),
        (ffn1_w2, m_ffn1_w2, v_ffn1_w2, parts_f1w2, False),
        (w_in, m_w_in, v_w_in, parts_in, True),
        (w_out, m_w_out, v_w_out, parts_out, False),
        (ffn2_w1, m_ffn2_w1, v_ffn2_w1, parts_f2w1, True),
        (ffn2_w3, m_ffn2_w3, v_ffn2_w3, parts_f2w3, True),
        (ffn2_w2, m_ffn2_w2, v_ffn2_w2, parts_f2w2, False),
    )
    big_out = []
    for j, (w, m, v, parts, t) in enumerate(big):
        view = (lambda a: a[0].T) if t else (lambda a: a[0])
        back = (lambda a: a.T[None]) if t else (lambda a: a[None])
        big_out.append([back(a) for a in _adamw_big(view(w), parts, view(m), view(v), "adamw_%d" % j)])

    small_w = (ffn1_norm, mix_norm, ffn2_norm, fnorm, pool_scale, ret_norm, pw3.reshape(-1, GROUP))
    small_m = (m_ffn1_norm, m_mix_norm, m_ffn2_norm, m_final_norm.reshape(1, d), m_pool_scale, m_ret_norm, m_pool_w.reshape(-1, GROUP))
    small_v = (v_ffn1_norm, v_mix_norm, v_ffn2_norm, v_final_norm.reshape(1, d), v_pool_scale, v_ret_norm, v_pool_w.reshape(-1, GROUP))
    res = _adamw_small(stats_all, pw_all, small_w, small_m, small_v, pwid)
    loss = res[0][0, 0]
    small_out = [list(res[1 + 4 * j:5 + 4 * j]) for j in range(len(small_w))]
    small_out[3] = [a.reshape(d) for a in small_out[3]]
    small_out[6] = [a.reshape(pool_w.shape) for a in small_out[6]]

    order = [small_out[0], big_out[0], big_out[1], big_out[2], small_out[1], big_out[3], small_out[6], small_out[4],
             small_out[5], big_out[4], small_out[2], big_out[5], big_out[6], big_out[7], small_out[3]]
    result = [loss, dx[None]]
    for kind in range(4):
        result += [t[kind] for t in order]
    return tuple(result)
```

```python
import functools

import numpy as np
import jax
import jax.numpy as jnp
from jax import lax
from jax.experimental import pallas as pl
from jax.experimental.pallas import tpu as pltpu

F32 = jnp.float32
BF16 = jnp.bfloat16

NDEV = 8
NCHIP = 4
EPS = 1e-6
N_POOL_GROUPS = 4
POOL_WINDOWS = (2, 4, 8, 16)
MAX_WINDOW = 16
GROUP = 128
RET_HEADS = 4
ROPE_BASE = 10000.0
ADAM_LR = 0.001
ADAM_B1 = 0.9
ADAM_B2 = 0.999
ADAM_EPS = 1e-08
ADAM_WD = 0.01
ADAM_STEP = 10

VMEM_LIMIT = 56 * 1024 * 1024
FFN_CHUNK = 256
ROW_BAND = 32

NT = (((1,), (1,)), ((), ()))
NN = (((1,), (0,)), ((), ()))
TN = (((0,), (0,)), ((), ()))

ANY = pl.BlockSpec(memory_space=pl.ANY)


def _dot(a, b, dims):
    return lax.dot_general(a, b, dims, preferred_element_type=F32)


def _call(body, **kw):
    return pl.pallas_call(body, **kw)


def _params(**kw):
    return pltpu.CompilerParams(vmem_limit_bytes=VMEM_LIMIT, **kw)


def _seq(n):
    return _params(dimension_semantics=("arbitrary",) * n)


def _peer(k):
    x, y, c = lax.axis_index("x"), lax.axis_index("y"), lax.axis_index("c")
    return (1 - x if k & 4 else x, 1 - y if k & 2 else y, 1 - c if k & 1 else c)


def _flat(pos):
    return 4 * pos[0] + 2 * pos[1] + pos[2]


def _chip(pos):
    return 2 * pos[0] + pos[1]


def _row_tile(rows, cap):
    return max(t for t in range(16, min(rows, cap) + 1, 16) if rows % t == 0)


def _pieces(rows, n):
    tiles = rows // 16
    cuts = [16 * (tiles * q // n) for q in range(n + 1)]
    return [(a, b - a) for a, b in zip(cuts[:-1], cuts[1:])]


def _load_weights(parts, sems):
    copies = []
    for buf, dst in parts:
        rows = buf.shape[1]
        for p in range(NDEV):
            cp = pltpu.make_async_copy(buf.at[p], dst.at[pl.ds(p * rows, rows), :], sems.at[len(copies)])
            cp.start()
            copies.append(cp)
    return copies


def _sigmoid(a):
    return 1.0 / (1.0 + jnp.exp(-a))


def _remote(src, dst, send_sem, recv_sem, to):
    return pltpu.make_async_remote_copy(
        src_ref=src, dst_ref=dst, send_sem=send_sem, recv_sem=recv_sem, device_id=to, device_id_type=pl.DeviceIdType.MESH
    )


class _Gather:
    X, Y, FAR = 4, 2, 6
    COPIES = 8

    def __init__(self, shards):
        n = len(shards)
        self.operands = list(shards)
        self.out_shape = [jax.ShapeDtypeStruct((NDEV,) + a.shape, a.dtype) for a in shards]
        self.sems = [
            pltpu.SemaphoreType.DMA((self.COPIES * n,)), pltpu.SemaphoreType.DMA((self.COPIES * n,)),
            pltpu.SemaphoreType.DMA((n,)),
        ]
        self.stages = [self.begin, self.relay, self.relay_far, self.end]

    def _copy(self, t, k, block, to, ins, outs, sems, own=False, half=None):
        rows = outs[t].shape[1]
        part = pl.ds(0, rows) if half is None else pl.ds(half * (rows // 2), rows // 2)
        dst = outs[t].at[_flat(block), part, :]
        at = self.COPIES * t + k
        return _remote(ins[t] if own else dst, dst, sems[0].at[at], sems[1].at[at], to)

    def _local(self, t, ins, outs, sems):
        return pltpu.make_async_copy(ins[t], outs[t].at[_flat(_peer(0))], sems[2].at[t])

    def begin(self, ins, outs, sems):
        me = _peer(0)
        for t in range(len(ins)):
            self._local(t, ins, outs, sems).start()
            for k, code in enumerate((1, self.X, self.Y)):
                self._copy(t, k, me, _peer(code), ins, outs, sems, own=True).start()

    def relay(self, ins, outs, sems):
        me, sibling = _peer(0), _peer(1)
        for t in range(len(ins)):
            self._copy(t, 1, _peer(self.X), me, ins, outs, sems).wait_recv()
            self._copy(t, 3, _peer(self.X), _peer(self.Y), ins, outs, sems, half=0).start()
            self._copy(t, 5, _peer(self.X), sibling, ins, outs, sems).start()
            self._copy(t, 2, _peer(self.Y), me, ins, outs, sems).wait_recv()
            self._copy(t, 4, _peer(self.Y), _peer(self.X), ins, outs, sems, half=1).start()
            self._copy(t, 6, _peer(self.Y), sibling, ins, outs, sems).start()

    def relay_far(self, ins, outs, sems):
        me, sibling = _peer(0), _peer(1)
        for t in range(len(ins)):
            self._copy(t, 3, _peer(self.FAR), me, ins, outs, sems, half=0).wait_recv()
            self._copy(t, 4, _peer(self.FAR), me, ins, outs, sems, half=1).wait_recv()
            self._copy(t, 7, _peer(self.FAR), sibling, ins, outs, sems).start()

    def end(self, ins, outs, sems):
        me = _peer(0)
        for t in range(len(ins)):
            self._copy(t, 0, _peer(1), me, ins, outs, sems).wait_recv()
            for k, code in ((5, self.X), (6, self.Y), (7, self.FAR)):
                self._copy(t, k, _peer(code ^ 1), me, ins, outs, sems).wait_recv()
            for k in range(self.COPIES):
                self._copy(t, k, me, me, ins, outs, sems, half=0 if k == 3 else 1 if k == 4 else None).wait_send()
            self._local(t, ins, outs, sems).wait()


class _GatherDirect:
    def __init__(self, arrays):
        n = len(arrays)
        self.operands = list(arrays)
        self.out_shape = [jax.ShapeDtypeStruct((NDEV,) + a.shape, a.dtype) for a in arrays]
        self.sems = [pltpu.SemaphoreType.DMA((7 * n,)), pltpu.SemaphoreType.DMA((7 * n,)), pltpu.SemaphoreType.DMA((n,))]
        self.stages = [self.begin, self.end]

    def begin(self, ins, outs, sems):
        mine = _flat(_peer(0))
        for t in range(len(ins)):
            pltpu.make_async_copy(ins[t], outs[t].at[mine], sems[2].at[t]).start()
            for k in range(1, NDEV):
                _remote(ins[t], outs[t].at[mine], sems[0].at[7 * t + k - 1], sems[1].at[7 * t + k - 1], _peer(k)).start()

    def end(self, ins, outs, sems):
        mine = _flat(_peer(0))
        for t in range(len(ins)):
            for k in range(1, NDEV):
                cp = _remote(ins[t], outs[t].at[_flat(_peer(k))], sems[0].at[7 * t + k - 1], sems[1].at[7 * t + k - 1], _peer(k))
                cp.wait_recv()
                cp.wait_send()
            pltpu.make_async_copy(ins[t], outs[t].at[mine], sems[2].at[t]).wait()


class _ChipScatter:
    pieces = 1

    def __init__(self, sums):
        n = len(sums) * NCHIP * self.pieces
        self.operands = list(sums)
        self.out_shape = [jax.ShapeDtypeStruct(a.shape, a.dtype) for a in sums]
        self.sems = [pltpu.SemaphoreType.DMA((n,)), pltpu.SemaphoreType.DMA((n,))]
        self.stages = [self.begin, self.end]

    def _copies(self, ins, outs, sems, arriving):
        mine = _chip(_peer(0))
        copies = []
        for t in range(len(ins)):
            rows = ins[t].shape[1] // self.pieces
            for k in (0, 4, 2, 6):
                other = _chip(_peer(k))
                for q in range(self.pieces):
                    part = pl.ds(q * rows, rows)
                    at = len(copies)
                    if k == 0:
                        cp = pltpu.make_async_copy(ins[t].at[mine, part, :], outs[t].at[mine, part, :], sems[0].at[at])
                    else:
                        landing = outs[t].at[other if arriving else mine, part, :]
                        cp = _remote(ins[t].at[other, part, :], landing, sems[0].at[at], sems[1].at[at], _peer(k))
                    copies.append(cp)
        return copies

    def begin(self, ins, outs, sems):
        for cp in self._copies(ins, outs, sems, False):
            cp.start()

    def end(self, ins, outs, sems):
        for at, cp in enumerate(self._copies(ins, outs, sems, True)):
            if at % (NCHIP * self.pieces) < self.pieces:
                cp.wait()
            else:
                cp.wait_recv()
                cp.wait_send()


class _CastRows:
    def __init__(self, arrays):
        n = len(arrays)
        self.operands = list(arrays)
        self.out_shape = [jax.ShapeDtypeStruct(a.shape, BF16) for a in arrays]
        self.sems = [pltpu.SemaphoreType.DMA((n,)), pltpu.SemaphoreType.DMA((n,))]
        self.sems += [pltpu.VMEM(a.shape, F32) for a in arrays] + [pltpu.VMEM(a.shape, BF16) for a in arrays]
        self.stages = [self.begin, self.convert, self.end]

    def _moves(self, t, ins, outs, scratch):
        n = len(ins)
        load = pltpu.make_async_copy(ins[t], scratch[2 + t], scratch[0].at[t])
        store = pltpu.make_async_copy(scratch[2 + n + t], outs[t], scratch[1].at[t])
        return load, store

    def begin(self, ins, outs, scratch):
        for t in range(len(ins)):
            self._moves(t, ins, outs, scratch)[0].start()

    def convert(self, ins, outs, scratch):
        n = len(ins)
        for t in range(n):
            load, store = self._moves(t, ins, outs, scratch)
            load.wait()
            scratch[2 + n + t][...] = scratch[2 + t][...].astype(BF16)
            store.start()

    def end(self, ins, outs, scratch):
        for t in range(len(ins)):
            self._moves(t, ins, outs, scratch)[1].wait()


def _split_refs(refs, counts):
    out, at = [], 0
    for n in counts:
        out.append(refs[at:at + n])
        at += n
    return out


def _comm_call(carries, name):
    nin = [len(c.operands) for c in carries]
    nout = [len(c.out_shape) for c in carries]
    nsem = [len(c.sems) for c in carries]

    def body(*refs):
        ins, outs, sems = _split_refs(refs, (sum(nin), sum(nout), sum(nsem)))
        parts = list(zip(carries, _split_refs(ins, nin), _split_refs(outs, nout), _split_refs(sems, nsem)))
        for depth in range(max(len(c.stages) for c in carries)):
            for c, i, o, s in parts:
                if depth < len(c.stages) - 1:
                    c.stages[depth](i, o, s)
        for c, i, o, s in parts:
            c.stages[-1](i, o, s)

    res = _call(
        body,
        name=name,
        out_shape=[sh for c in carries for sh in c.out_shape],
        in_specs=[ANY] * sum(nin),
        out_specs=[ANY] * sum(nout),
        scratch_shapes=[sm for c in carries for sm in c.sems],
        compiler_params=_params(has_side_effects=True),
    )(*[a for c in carries for a in c.operands])
    return _split_refs(list(res), nout)


def _grid_call(body, carries, *, name, grid, in_specs, out_specs, out_shape, scratch_shapes, args):
    ni, no, ns = len(in_specs), len(out_specs), len(scratch_shapes)
    nin = [len(c.operands) for c in carries]
    nout = [len(c.out_shape) for c in carries]
    nsem = [len(c.sems) for c in carries]
    steps = int(np.prod(grid))

    def when_of(stage, count):
        first, last = (5 * steps) // 8 - 1, steps - 2
        return max(0, last if count <= 3 else first + (last - first) * (stage - 1) // (count - 3))

    def wrapped(*refs):
        ins, cins, outs, couts, scr, csems = _split_refs(refs, (ni, sum(nin), no, sum(nout), ns, sum(nsem)))
        if not carries:
            return body(*ins, *outs, *scr)
        parts = list(zip(carries, _split_refs(cins, nin), _split_refs(couts, nout), _split_refs(csems, nsem)))
        step = pl.program_id(0)
        for axis in range(1, len(grid)):
            step = step * grid[axis] + pl.program_id(axis)

        @pl.when(step == 0)
        def _():
            for c, i, o, s in parts:
                c.stages[0](i, o, s)

        body(*ins, *outs, *scr)

        for c, i, o, s in parts:
            for stage in range(1, len(c.stages) - 1):
                pl.when(step == when_of(stage, len(c.stages)))(functools.partial(c.stages[stage], i, o, s))

        @pl.when(step == steps - 1)
        def _():
            for c, i, o, s in parts:
                c.stages[-1](i, o, s)

    res = _call(
        wrapped,
        name=name,
        grid=tuple(grid),
        in_specs=list(in_specs) + [ANY] * sum(nin),
        out_specs=list(out_specs) + [ANY] * sum(nout),
        out_shape=list(out_shape) + [sh for c in carries for sh in c.out_shape],
        scratch_shapes=list(scratch_shapes) + [sm for c in carries for sm in c.sems],
        compiler_params=_seq(len(grid)),
    )(*args, *[a for c in carries for a in c.operands])
    res = list(res)
    return res[:no], _split_refs(res[no:], nout)


def _chunks(width):
    return [(at, min(FFN_CHUNK, width - at)) for at in range(0, width, FFN_CHUNK)]


def _start_chunk_loads(bufs, dsts, sems, chunks):
    copies = [[None] * len(chunks) for _ in bufs]
    for c, (at, width) in enumerate(chunks):
        for m, (buf, dst) in enumerate(zip(bufs, dsts)):
            cp = pltpu.make_async_copy(
                buf.at[pl.ds(at, width), :], dst.at[pl.ds(at, width), :], sems.at[m * len(chunks) + c]
            )
            cp.start()
            copies[m][c] = cp
    return copies


def _ffn_fwd(x, gain, weights, ffn, head=None, carries=()):
    s, d = x.shape
    tm = min(512, s)
    chunks = _chunks(ffn)

    def body(*refs):
        if head is None:
            x_ref, g_ref, b1, b3, b2, h_ref, a_ref, b_ref, hm_ref, w1s, w3s, w2s, sems = refs
        else:
            x_ref, g_ref, b1, b3, b2, gf_ref, t_ref, h_ref, a_ref, b_ref, hm_ref, dgf_ref, loss_ref, w1s, w3s, w2s, sems = refs

        def one_tile(loads):
            xv = x_ref[...]
            r = lax.rsqrt(jnp.mean(xv * xv, axis=-1, keepdims=True) + EPS)
            n = (xv * r * g_ref[...]).astype(BF16)
            acc = jnp.zeros((tm, d), F32)
            for c, (at, width) in enumerate(chunks):
                cols = slice(at, at + width)
                if loads is not None:
                    for m in range(3):
                        loads[m][c].wait()
                a = _dot(n, w1s[cols, :], NT)
                b = _dot(n, w3s[cols, :], NT)
                a_ref[:, cols] = a.astype(BF16)
                b_ref[:, cols] = b.astype(BF16)
                hm = (a * _sigmoid(a) * b).astype(BF16)
                hm_ref[:, cols] = hm
                acc = acc + _dot(hm, w2s[cols, :], NN)
            h = xv + 0.5 * acc
            if head is None:
                h_ref[...] = h
            else:
                rf = lax.rsqrt(jnp.mean(h * h, axis=-1, keepdims=True) + EPS)
                nh = h * rf
                gf = gf_ref[...]
                err = nh * gf - t_ref[...]
                loss_ref[...] += jnp.sum(err * err, axis=0, keepdims=True) * (0.5 / d)
                dy = err * (1.0 / d)
                dgf_ref[...] += jnp.sum(dy * nh, axis=0, keepdims=True)
                dn = dy * gf
                h_ref[...] = rf * (dn - nh * jnp.mean(dn * nh, axis=-1, keepdims=True))

        if head is None:
            @pl.when(pl.program_id(0) == 0)
            def _():
                one_tile(_start_chunk_loads((b1, b3, b2), (w1s, w3s, w2s), sems, chunks))

            @pl.when(pl.program_id(0) > 0)
            def _():
                one_tile(None)
        else:
            @pl.when(pl.program_id(0) == 0)
            def _():
                dgf_ref[...] = jnp.zeros_like(dgf_ref)
                loss_ref[...] = jnp.zeros_like(loss_ref)
                for loads in _start_chunk_loads((b1, b3, b2), (w1s, w3s, w2s), sems, chunks):
                    for cp in loads:
                        cp.wait()

            one_tile(None)

    tile = pl.BlockSpec((tm, d), lambda i: (i, 0))
    row = pl.BlockSpec((1, d), lambda i: (0, 0))
    wide = pl.BlockSpec((tm, ffn), lambda i: (i, 0))
    in_specs = [tile, row, ANY, ANY, ANY]
    out_shape = [jax.ShapeDtypeStruct((s, d), F32)] + [jax.ShapeDtypeStruct((s, ffn), BF16)] * 3
    out_specs = [tile, wide, wide, wide]
    args = [x, gain] + [w.reshape(ffn, d) for w in weights]
    if head is not None:
        in_specs += [row, tile]
        args += list(head)
        out_shape += [jax.ShapeDtypeStruct((1, d), F32)] * 2
        out_specs += [row, row]
    return _grid_call(
        body,
        carries,
        name="ffn_fwd_loss" if head is not None else "ffn_fwd",
        grid=(s // tm,),
        in_specs=in_specs,
        out_specs=out_specs,
        out_shape=out_shape,
        scratch_shapes=[pltpu.VMEM((ffn, d), BF16)] * 3 + [pltpu.SemaphoreType.DMA((3 * len(chunks),))],
        args=args,
    )


def _ffn_bwd(dh, x, a, b, gain, weights, ffn, name, carries=()):
    s, d = x.shape
    tm = min(512, s)
    halves = 2
    fh = ffn // halves

    def body(dh_ref, x_ref, a_ref, b_ref, g_ref, b1, b3, b2, dx_ref, da_ref, db_ref, n_ref, dg_ref, w1s, w3s, w2s, sems):
        i, j = pl.program_id(0), pl.program_id(1)

        @pl.when((i == 0) & (j == 0))
        def _():
            for cp in _load_weights(((b1, w1s), (b3, w3s), (b2, w2s)), sems):
                cp.wait()
            dg_ref[...] = jnp.zeros_like(dg_ref)

        @pl.when(j == 0)
        def _():
            dx_ref[...] = jnp.zeros_like(dx_ref)

        dob = (0.5 * dh_ref[...]).astype(BF16)
        chunks = _chunks(fh)

        def dhm_of(k):
            at, width = chunks[k]
            return _dot(dob, w2s[pl.ds(pl.multiple_of(j * fh + at, GROUP), width), :], NT)

        ahead = dhm_of(0)
        for k, (at, width) in enumerate(chunks):
            cols = slice(at, at + width)
            dhm = ahead
            if k + 1 < len(chunks):
                ahead = dhm_of(k + 1)
            for top in range(0, tm, ROW_BAND):
                band = slice(top, top + ROW_BAND)
                av = a_ref[band, cols].astype(F32)
                bv = b_ref[band, cols].astype(F32)
                sg = _sigmoid(av)
                dv = dhm[band]
                da_ref[band, cols] = (dv * bv * (sg * (1.0 + av * (1.0 - sg)))).astype(BF16)
                db_ref[band, cols] = (dv * (av * sg)).astype(BF16)
        half = pl.ds(pl.multiple_of(j * fh, GROUP), fh)
        dx_ref[...] += _dot(da_ref[...], w1s[half, :], NN) + _dot(db_ref[...], w3s[half, :], NN)

        @pl.when(j == halves - 1)
        def _():
            xv = x_ref[...]
            g = g_ref[...]
            r = lax.rsqrt(jnp.mean(xv * xv, axis=-1, keepdims=True) + EPS)
            nh = xv * r
            n_ref[...] = (nh * g).astype(BF16)
            total = dx_ref[...]
            dg_ref[...] += jnp.sum(total * nh, axis=0, keepdims=True)
            dnh = total * g
            dx_ref[...] = dh_ref[...] + r * (dnh - nh * jnp.mean(dnh * nh, axis=-1, keepdims=True))

    tile = pl.BlockSpec((tm, d), lambda i, j: (i, 0))
    row = pl.BlockSpec((1, d), lambda i, j: (0, 0))
    wide = pl.BlockSpec((tm, fh), lambda i, j: (i, j))
    return _grid_call(
        body,
        carries,
        name=name,
        grid=(s // tm, halves),
        in_specs=[tile, tile, wide, wide, row, ANY, ANY, ANY],
        out_specs=[tile, wide, wide, tile, row],
        out_shape=[
            jax.ShapeDtypeStruct((s, d), F32),
            jax.ShapeDtypeStruct((s, ffn), BF16),
            jax.ShapeDtypeStruct((s, ffn), BF16),
            jax.ShapeDtypeStruct((s, d), BF16),
            jax.ShapeDtypeStruct((1, d), F32),
        ],
        scratch_shapes=[pltpu.VMEM((ffn, d), BF16)] * 3 + [pltpu.SemaphoreType.DMA((3 * NDEV,))],
        args=[dh, x, a, b, gain] + list(weights),
    )


SWAP_PIECES = 1


def _wgrad(lhs, rhs, scale, name, carries=()):
    s, m = lhs.shape
    n = rhs.shape[1]
    rs = m // NDEV
    tk = min(1024, s)
    steps = s // tk
    pieces = [(j, at, size) for j in range(2) for at, size in _pieces(rs, SWAP_PIECES)]

    def body(l_ref, r_ref, o_ref, acc, mine, theirs, send_sems, recv_sems):
        h, k = pl.program_id(0), pl.program_id(1)

        @pl.when(k == 0)
        def _():
            acc[...] = _dot(l_ref[...], r_ref[...].astype(BF16), TN)

        @pl.when(k > 0)
        def _():
            acc[...] += _dot(l_ref[...], r_ref[...].astype(BF16), TN)

        def exchange(half):
            c = lax.axis_index("c")
            return [
                _remote(mine.at[half, 1 - c, j, pl.ds(at, size), :], theirs.at[half, j, pl.ds(at, size), :],
                        send_sems.at[half * len(pieces) + q], recv_sems.at[half * len(pieces) + q], _peer(1))
                for q, (j, at, size) in enumerate(pieces)
            ]

        def settle(half):
            for cp in exchange(half):
                cp.wait_recv()
            both = mine[half, lax.axis_index("c")].astype(F32) + theirs[half].astype(F32)
            o_ref[2 * half:2 * half + 2] = both.astype(BF16)
            for cp in exchange(half):
                cp.wait_send()

        for half in range(2):
            @pl.when((h == half) & (k == steps - 1))
            def _():
                for p in range(NCHIP):
                    mine[half, p % 2, p // 2] = (acc[p * rs:(p + 1) * rs, :] * scale).astype(BF16)
                for cp in exchange(half):
                    cp.start()
                if half == 1:
                    settle(0)
                    settle(1)

    (out,), carried = _grid_call(
        body,
        carries,
        name=name,
        grid=(2, steps),
        in_specs=[pl.BlockSpec((tk, m // 2), lambda h, k: (k, h)), pl.BlockSpec((tk, n), lambda h, k: (k, 0))],
        out_specs=[pl.BlockSpec((NCHIP, rs, n), lambda h, k: (0, 0, 0))],
        out_shape=[jax.ShapeDtypeStruct((NCHIP, rs, n), BF16)],
        scratch_shapes=[
            pltpu.VMEM((m // 2, n), F32), pltpu.VMEM((2, 2, 2, rs, n), BF16), pltpu.VMEM((2, 2, rs, n), BF16),
            pltpu.SemaphoreType.DMA((2 * len(pieces),)), pltpu.SemaphoreType.DMA((2 * len(pieces),)),
        ],
        args=[lhs, rhs],
    )
    return out, carried


def _mix_constants(s):
    c = GROUP
    lg = np.log1p(-np.exp2(-5.0 - np.arange(RET_HEADS, dtype=np.float32))).astype(np.float32)
    pos = np.arange(c, dtype=np.float32)
    rel = pos[:, None] - pos[None, :]
    decay = np.where(rel[None] >= 0, np.exp(lg[:, None, None] * np.maximum(rel, 0.0)[None]), 0.0).astype(np.float32)
    ktail = np.exp(lg[:, None] * (c - 1 - pos)[None, :]).astype(np.float32)
    qhead = np.exp(lg[:, None] * (pos + 1.0)[None, :]).astype(np.float32)
    chunk_decay = [float(v) for v in np.exp(lg * np.float32(c)).astype(np.float32)]
    ones = np.ones((1, 1, c), np.float32)
    inv_freq = (1.0 / (np.float32(ROPE_BASE) ** (np.arange(0, c, 2, dtype=np.float32) / np.float32(c)))).astype(np.float32)
    ang = (np.arange(s, dtype=np.float32)[:, None] * inv_freq[None, :]).astype(np.float32)
    cos, sin = np.cos(ang).astype(np.float32), np.sin(ang).astype(np.float32)
    return dict(
        decay=jnp.asarray(decay),
        ktail=jnp.asarray(ktail[:, :, None] * ones),
        qhead=jnp.asarray(qhead[:, :, None] * ones),
        chunk_decay=chunk_decay,
        cos=jnp.asarray(np.concatenate([cos, cos], axis=-1)),
        sin=jnp.asarray(np.concatenate([-sin, sin], axis=-1)),
    )


def _rope(t, cos, sin):
    return t * cos + pltpu.roll(t, GROUP // 2, axis=1) * sin


def _rope_bwd(dt, cos, sin):
    return dt * cos + pltpu.roll(dt * sin, GROUP // 2, axis=1)


def _window_sums(ext, w, forward):
    rows = ext.shape[0]
    acc, k = ext, 1
    while k < w:
        acc = acc + pltpu.roll(acc, k if forward else rows - k, axis=0)
        k *= 2
    return acc


def _pool_counts(tile, tm, w):
    t = lax.broadcasted_iota(jnp.int32, (tm, 1), 0) + tile * tm
    return jnp.minimum(t + 1, w).astype(F32)


def _mix_fwd(h1, gain, weights, pool_w, pool_scale, ret_gain, consts, carries=()):
    s, d = h1.shape
    pwid = N_POOL_GROUPS * GROUP
    rwid = RET_HEADS * GROUP
    inw = pwid + 4 * rwid
    tm = min(256, s)
    nck = tm // GROUP
    cd = consts["chunk_decay"]

    def body(h_ref, g_ref, bin_, bout, pw_ref, ps_ref, rg_ref, cos_ref, sin_ref, dec_ref, kt_ref, qh_ref,
             h2_ref, proj_ref, o_ref, rs_ref, wins, wouts, state, carry, mbuf, sems):
        i = pl.program_id(0)

        @pl.when(i == 0)
        def _():
            for cp in _load_weights(((bin_, wins), (bout, wouts)), sems):
                cp.wait()
            state[...] = jnp.zeros_like(state)
            carry[...] = jnp.zeros_like(carry)

        hv = h_ref[...]
        r = lax.rsqrt(jnp.mean(hv * hv, axis=-1, keepdims=True) + EPS)
        u = (hv * r * g_ref[...]).astype(BF16)
        proj_ref[...] = _dot(u, wins[...], NT)

        ext = jnp.concatenate([carry[...], proj_ref[:, 0:pwid]], axis=0)
        carry[...] = proj_ref[tm - MAX_WINDOW:tm, 0:pwid]
        for gi, w in enumerate(POOL_WINDOWS):
            cols = slice(gi * GROUP, (gi + 1) * GROUP)
            xg = ext[:, cols]
            ws = _window_sums(xg, w, True)[MAX_WINDOW:, :]
            pooled = ws / _pool_counts(i, tm, w) - xg[MAX_WINDOW:, :]
            z = _dot(pooled.astype(BF16), pw_ref[gi].astype(BF16), NN)
            mbuf[:, cols] = (z * ps_ref[:, cols]).astype(BF16)

        cos, sin = cos_ref[...], sin_ref[...]
        for h in range(RET_HEADS):
            cq = slice(pwid + h * GROUP, pwid + (h + 1) * GROUP)
            ck = slice(pwid + rwid + h * GROUP, pwid + rwid + (h + 1) * GROUP)
            cv = slice(pwid + 2 * rwid + h * GROUP, pwid + 2 * rwid + (h + 1) * GROUP)
            cg = slice(pwid + 3 * rwid + h * GROUP, pwid + 3 * rwid + (h + 1) * GROUP)
            ch = slice(h * GROUP, (h + 1) * GROUP)
            qr = _rope(proj_ref[:, cq], cos, sin)
            kr = _rope(proj_ref[:, ck], cos, sin) * (GROUP ** -0.5)
            vb = proj_ref[:, cv].astype(BF16)
            for n in range(nck):
                rows = slice(n * GROUP, (n + 1) * GROUP)
                qc, kc, vc = qr[rows], kr[rows], vb[rows]
                rb = state[h]
                rs_ref[n, h] = rb
                p = (_dot(qc.astype(BF16), kc.astype(BF16), NT) * dec_ref[h]).astype(BF16)
                o = _dot(p, vc, NN) + _dot((qc * qh_ref[h]).astype(BF16), rb.astype(BF16), NN)
                state[h] = cd[h] * rb + _dot((kc * kt_ref[h]).astype(BF16), vc, TN)
                o_ref[rows, ch] = o
                on = o * lax.rsqrt(jnp.mean(o * o, axis=-1, keepdims=True) + EPS)
                gv = proj_ref[rows, cg]
                mbuf[rows, pwid + h * GROUP:pwid + (h + 1) * GROUP] = (
                    gv * _sigmoid(gv) * (on * rg_ref[:, ch])
                ).astype(BF16)
        h2_ref[...] = hv + _dot(mbuf[...], wouts[...], NN)

    tile = pl.BlockSpec((tm, d), lambda i: (i, 0))
    full = lambda shape: pl.BlockSpec(shape, lambda i: (0,) * len(shape))
    return _grid_call(
        body,
        carries,
        name="mix_fwd",
        grid=(s // tm,),
        in_specs=[
            tile, full((1, d)), ANY, ANY,
            full((N_POOL_GROUPS, GROUP, GROUP)), full((1, pwid)), full((1, rwid)),
            pl.BlockSpec((tm, GROUP), lambda i: (i, 0)), pl.BlockSpec((tm, GROUP), lambda i: (i, 0)),
            full((RET_HEADS, GROUP, GROUP)), full((RET_HEADS, GROUP, GROUP)), full((RET_HEADS, GROUP, GROUP)),
        ],
        out_specs=[
            tile,
            pl.BlockSpec((tm, inw), lambda i: (i, 0)),
            pl.BlockSpec((tm, rwid), lambda i: (i, 0)),
            pl.BlockSpec((nck, RET_HEADS, GROUP, GROUP), lambda i: (i, 0, 0, 0)),
        ],
        out_shape=[
            jax.ShapeDtypeStruct((s, d), F32),
            jax.ShapeDtypeStruct((s, inw), F32),
            jax.ShapeDtypeStruct((s, rwid), F32),
            jax.ShapeDtypeStruct((s // GROUP, RET_HEADS, GROUP, GROUP), F32),
        ],
        scratch_shapes=[
            pltpu.VMEM((inw, d), BF16), pltpu.VMEM((d, d), BF16),
            pltpu.VMEM((RET_HEADS, GROUP, GROUP), F32), pltpu.VMEM((MAX_WINDOW, pwid), F32),
            pltpu.VMEM((tm, d), BF16), pltpu.SemaphoreType.DMA((2 * NDEV,)),
        ],
        args=[h1, gain, weights[0], weights[1], pool_w, pool_scale, ret_gain,
              consts["cos"], consts["sin"], consts["decay"], consts["ktail"], consts["qhead"]],
    )


def _mix_bwd(dh2, h1, proj, o_saved, rsave, gain, weights, pool_w, pool_scale, ret_gain, consts, carries=()):
    s, d = h1.shape
    pwid = N_POOL_GROUPS * GROUP
    rwid = RET_HEADS * GROUP
    inw = pwid + 4 * rwid
    tm = min(256, s)
    nck = tm // GROUP
    nt = s // tm
    cd = consts["chunk_decay"]
    halo_per_tile = tm // MAX_WINDOW

    def body(dh2_ref, h_ref, proj_ref, halo_ref, o_ref, rs_ref, g_ref, bin_, bout, pw_ref, ps_ref, rg_ref,
             cos_ref, sin_ref, dec_ref, kt_ref, qh_ref,
             dh1_ref, dproj_ref, u_ref, m_ref, dpw_ref, dps_ref, drg_ref, dg_ref,
             wins, wouts, dstate, carry, dm, dpj, sems):
        i = pl.program_id(0)
        tile = nt - 1 - i

        @pl.when(i == 0)
        def _():
            for cp in _load_weights(((bin_, wins), (bout, wouts)), sems):
                cp.wait()
            dstate[...] = jnp.zeros_like(dstate)
            carry[...] = jnp.zeros_like(carry)
            for ref in (dpw_ref, dps_ref, drg_ref, dg_ref):
                ref[...] = jnp.zeros_like(ref)

        dh2v = dh2_ref[...]
        dm[...] = _dot(dh2v.astype(BF16), wouts[...], NT)
        hv = h_ref[...]
        g = g_ref[...]
        r = lax.rsqrt(jnp.mean(hv * hv, axis=-1, keepdims=True) + EPS)
        uh = hv * r
        u_ref[...] = (uh * g).astype(BF16)

        halo = jnp.where(tile == 0, 0.0, halo_ref[...])
        ext = jnp.concatenate([halo, proj_ref[:, 0:pwid]], axis=0)
        next_dpn = carry[...]
        for gi, w in enumerate(POOL_WINDOWS):
            cols = slice(gi * GROUP, (gi + 1) * GROUP)
            xg = ext[:, cols]
            cnt = _pool_counts(tile, tm, w)
            pooled = (_window_sums(xg, w, True)[MAX_WINDOW:, :] / cnt - xg[MAX_WINDOW:, :]).astype(BF16)
            pwb = pw_ref[gi].astype(BF16)
            z = _dot(pooled, pwb, NN)
            scale = ps_ref[:, cols]
            m_ref[:, cols] = (z * scale).astype(BF16)
            da = dm[:, cols]
            dps_ref[:, cols] += jnp.sum(da * z, axis=0, keepdims=True)
            dz = (da * scale).astype(BF16)
            dpw_ref[gi] += _dot(pooled, dz, TN)
            dpl = _dot(dz, pwb, NT)
            dpn = dpl / cnt
            ext2 = jnp.concatenate([dpn, next_dpn[:, cols]], axis=0)
            dpj[:, cols] = (_window_sums(ext2, w, False)[0:tm, :] - dpl).astype(BF16)
            carry[:, cols] = dpn[0:MAX_WINDOW, :]

        cos, sin = cos_ref[...], sin_ref[...]
        for h in range(RET_HEADS):
            cq = slice(pwid + h * GROUP, pwid + (h + 1) * GROUP)
            ck = slice(pwid + rwid + h * GROUP, pwid + rwid + (h + 1) * GROUP)
            cv = slice(pwid + 2 * rwid + h * GROUP, pwid + 2 * rwid + (h + 1) * GROUP)
            cg = slice(pwid + 3 * rwid + h * GROUP, pwid + 3 * rwid + (h + 1) * GROUP)
            ch = slice(h * GROUP, (h + 1) * GROUP)
            qr = _rope(proj_ref[:, cq], cos, sin)
            kr = _rope(proj_ref[:, ck], cos, sin) * (GROUP ** -0.5)
            vb = proj_ref[:, cv].astype(BF16)
            gv = proj_ref[:, cg]
            ov = o_ref[:, ch]
            ro = lax.rsqrt(jnp.mean(ov * ov, axis=-1, keepdims=True) + EPS)
            on = ov * ro
            rg = rg_ref[:, ch]
            db = dm[:, pwid + h * GROUP:pwid + (h + 1) * GROUP]
            sg = _sigmoid(gv)
            sl = gv * sg
            m_ref[:, pwid + h * GROUP:pwid + (h + 1) * GROUP] = (sl * (on * rg)).astype(BF16)
            dpj[:, cg] = (db * (on * rg) * (sg * (1.0 + gv * (1.0 - sg)))).astype(BF16)
            drg_ref[:, ch] += jnp.sum(db * sl * on, axis=0, keepdims=True)
            don = db * sl * rg
            do = (ro * (don - on * jnp.mean(don * on, axis=-1, keepdims=True))).astype(BF16)
            for n in reversed(range(nck)):
                rows = slice(n * GROUP, (n + 1) * GROUP)
                qc, kc, vc, dob = qr[rows], kr[rows], vb[rows], do[rows]
                qcb, kcb = qc.astype(BF16), kc.astype(BF16)
                qh = (qc * qh_ref[h]).astype(BF16)
                kt = (kc * kt_ref[h]).astype(BF16)
                rn = rs_ref[n, h].astype(BF16)
                dnext = dstate[h]
                dnb = dnext.astype(BF16)
                dec = dec_ref[h]
                p = (_dot(qcb, kcb, NT) * dec).astype(BF16)
                ds = (_dot(dob, vc, NT) * dec).astype(BF16)
                dv = _dot(p, dob, TN) + _dot(kt, dnb, NN)
                dq = _dot(ds, kcb, NN) + _dot(dob, rn, NT) * qh_ref[h]
                dk = _dot(ds, qcb, TN) + _dot(vc, dnb, NT) * kt_ref[h]
                dstate[h] = cd[h] * dnext + _dot(qh, dob, TN)
                dpj[rows, cq] = _rope_bwd(dq, cos[rows], sin[rows]).astype(BF16)
                dpj[rows, ck] = _rope_bwd(dk * (GROUP ** -0.5), cos[rows], sin[rows]).astype(BF16)
                dpj[rows, cv] = dv.astype(BF16)

        dproj_ref[...] = dpj[...]
        du = _dot(dpj[...], wins[...], NN)
        dg_ref[...] += jnp.sum(du * uh, axis=0, keepdims=True)
        dn = du * g
        dh1_ref[...] = dh2v + r * (dn - uh * jnp.mean(dn * uh, axis=-1, keepdims=True))

    rev = lambda i: (nt - 1 - i, 0)
    tile = pl.BlockSpec((tm, d), rev)
    full = lambda shape: pl.BlockSpec(shape, lambda i: (0,) * len(shape))
    return _grid_call(
        body,
        carries,
        name="mix_bwd",
        grid=(nt,),
        in_specs=[
            tile, tile,
            pl.BlockSpec((tm, inw), rev),
            pl.BlockSpec((MAX_WINDOW, pwid), lambda i: (jnp.maximum((nt - 1 - i) * halo_per_tile - 1, 0), 0)),
            pl.BlockSpec((tm, rwid), rev),
            pl.BlockSpec((nck, RET_HEADS, GROUP, GROUP), lambda i: (nt - 1 - i, 0, 0, 0)),
            full((1, d)), ANY, ANY,
            full((N_POOL_GROUPS, GROUP, GROUP)), full((1, pwid)), full((1, rwid)),
            pl.BlockSpec((tm, GROUP), rev), pl.BlockSpec((tm, GROUP), rev),
            full((RET_HEADS, GROUP, GROUP)), full((RET_HEADS, GROUP, GROUP)), full((RET_HEADS, GROUP, GROUP)),
        ],
        out_specs=[
            tile, pl.BlockSpec((tm, inw), rev), tile, tile,
            full((N_POOL_GROUPS, GROUP, GROUP)), full((1, pwid)), full((1, rwid)), full((1, d)),
        ],
        out_shape=[
            jax.ShapeDtypeStruct((s, d), F32),
            jax.ShapeDtypeStruct((s, inw), BF16),
            jax.ShapeDtypeStruct((s, d), BF16),
            jax.ShapeDtypeStruct((s, d), BF16),
            jax.ShapeDtypeStruct((N_POOL_GROUPS, GROUP, GROUP), F32),
            jax.ShapeDtypeStruct((1, pwid), F32),
            jax.ShapeDtypeStruct((1, rwid), F32),
            jax.ShapeDtypeStruct((1, d), F32),
        ],
        scratch_shapes=[
            pltpu.VMEM((inw, d), BF16), pltpu.VMEM((d, d), BF16),
            pltpu.VMEM((RET_HEADS, GROUP, GROUP), F32), pltpu.VMEM((MAX_WINDOW, pwid), F32),
            pltpu.VMEM((tm, d), F32), pltpu.VMEM((tm, inw), BF16), pltpu.SemaphoreType.DMA((2 * NDEV,)),
        ],
        args=[dh2, h1, proj, proj, o_saved, rsave, gain, weights[0], weights[1], pool_w, pool_scale, ret_gain,
              consts["cos"], consts["sin"], consts["decay"], consts["ktail"], consts["qhead"]],
    )


def _adam(w, g, m, v):
    m = ADAM_B1 * m + (1.0 - ADAM_B1) * g
    v = ADAM_B2 * v + (1.0 - ADAM_B2) * jnp.square(g)
    m_hat = m / (1.0 - ADAM_B1 ** ADAM_STEP)
    v_hat = v / (1.0 - ADAM_B2 ** ADAM_STEP)
    delta = -ADAM_LR * (m_hat / (jnp.sqrt(v_hat) + ADAM_EPS) + ADAM_WD * w)
    return delta, m, v


def _adamw_big(w, parts, m, v, name):
    rows, d = w.shape
    tr = _row_tile(rows, 176)

    def body(w_ref, p_ref, m_ref, v_ref, g_ref, d_ref, nm_ref, nv_ref):
        g = p_ref[0].astype(F32)
        for q in range(1, NCHIP):
            g = g + p_ref[q].astype(F32)
        g_ref[...] = g
        d_ref[...], nm_ref[...], nv_ref[...] = _adam(w_ref[...], g, m_ref[...], v_ref[...])

    spec = pl.BlockSpec((tr, d), lambda i: (i, 0))
    return _call(
        body,
        name=name,
        grid=(rows // tr,),
        in_specs=[spec, pl.BlockSpec((NCHIP, tr, d), lambda i: (0, i, 0)), spec, spec],
        out_specs=[spec] * 4,
        out_shape=[jax.ShapeDtypeStruct((rows, d), F32)] * 4,
        compiler_params=_seq(1),
    )(w, parts, m, v)


def _adamw_small(stats_all, pw_all, ws, ms, vs, pwid):
    nsmall = len(ws)

    def body(*refs):
        st_ref, pwa_ref = refs[0], refs[1]
        w_refs = refs[2:2 + nsmall]
        m_refs = refs[2 + nsmall:2 + 2 * nsmall]
        v_refs = refs[2 + 2 * nsmall:2 + 3 * nsmall]
        outs = refs[2 + 3 * nsmall:]
        st = st_ref[0]
        pwg = pwa_ref[0]
        for q in range(1, NDEV):
            st = st + st_ref[q]
            pwg = pwg + pwa_ref[q]
        grads = [st[0:1, :], st[1:2, :], st[2:3, :], st[3:4, :], st[4:5, 0:pwid], st[4:5, pwid:2 * pwid], pwg]
        outs[0][...] = jnp.zeros((1, GROUP), F32) + jnp.sum(st[5:6, :])
        for j in range(nsmall):
            delta, nm, nv = _adam(w_refs[j][...], grads[j], m_refs[j][...], v_refs[j][...])
            outs[1 + 4 * j][...] = grads[j]
            outs[2 + 4 * j][...] = delta
            outs[3 + 4 * j][...] = nm
            outs[4 + 4 * j][...] = nv

    out_shape = [jax.ShapeDtypeStruct((1, GROUP), F32)]
    for w in ws:
        out_shape += [jax.ShapeDtypeStruct(w.shape, F32)] * 4
    return _call(body, name="adamw_small", out_shape=out_shape, compiler_params=_params())(
        stats_all, pw_all, *ws, *ms, *vs
    )


def kernel(x, ffn1_norm, ffn1_w1, ffn1_w3, ffn1_w2, mix_norm, w_in, pool_w, pool_scale, ret_norm, w_out, ffn2_norm, ffn2_w1, ffn2_w3, ffn2_w2, final_norm, loss_target, m_ffn1_norm, m_ffn1_w1, m_ffn1_w3, m_ffn1_w2, m_mix_norm, m_w_in, m_pool_w, m_pool_scale, m_ret_norm, m_w_out, m_ffn2_norm, m_ffn2_w1, m_ffn2_w3, m_ffn2_w2, m_final_norm, v_ffn1_norm, v_ffn1_w1, v_ffn1_w3, v_ffn1_w2, v_mix_norm, v_w_in, v_pool_w, v_pool_scale, v_ret_norm, v_w_out, v_ffn2_norm, v_ffn2_w1, v_ffn2_w3, v_ffn2_w2, v_final_norm):
    s, d = x.shape[1], x.shape[2]
    ffn = ffn1_w1.shape[2] * NDEV
    pwid = pool_scale.shape[1]
    xs, tgt = x[0], loss_target[0]
    consts = _mix_constants(s)
    pw3 = pool_w[0]
    fnorm = final_norm.reshape(1, d)

    rows_of = lambda w, transposed: w[0].T if transposed else w[0]
    send_f1 = [rows_of(w, t).astype(BF16) for w, t in ((ffn1_w1, True), (ffn1_w3, True), (ffn1_w2, False))]
    later = [rows_of(w, t) for w, t in ((w_in, True), (w_out, False), (ffn2_w1, True), (ffn2_w3, True), (ffn2_w2, False))]

    sent_later, w_f1 = _comm_call([_CastRows(later), _Gather(send_f1)], "gather_ffn1")
    send_mix, send_f2 = sent_later[:2], sent_later[2:]
    (h1, a1, b1, hm1), (more,) = _ffn_fwd(xs, ffn1_norm, w_f1, ffn, carries=[_Gather(send_mix + send_f2[:1])])
    w_mix = more[:2]
    (h2, proj, o_saved, rsave), (rest,) = _mix_fwd(
        h1, mix_norm, w_mix, pw3, pool_scale, ret_norm, consts, carries=[_Gather(send_f2[1:])]
    )
    w_f2 = more[2:] + rest
    (dh3, a2, b2, hm2, dgf, loss_cols), _ = _ffn_fwd(h2, ffn2_norm, w_f2, ffn, head=(fnorm, tgt))

    (dh2, da2, db2, n2, dg2), _ = _ffn_bwd(dh3, h2, a2, b2, ffn2_norm, w_f2, ffn, "ffn2_bwd")
    sum_f2w1, _ = _wgrad(da2, n2, 1.0, "ffn2_w1_grad")
    sum_f2w3, _ = _wgrad(db2, n2, 1.0, "ffn2_w3_grad")
    sum_f2w2, ((parts_f2w1,),) = _wgrad(hm2, dh3, 0.5, "ffn2_w2_grad", carries=[_ChipScatter([sum_f2w1])])

    (dh1, dproj, u, mm, dpw, dps, drg, dgm), ((parts_f2w3, parts_f2w2),) = _mix_bwd(
        dh2, h1, proj, o_saved, rsave, mix_norm, w_mix, pw3, pool_scale, ret_norm, consts,
        carries=[_ChipScatter([sum_f2w3, sum_f2w2])],
    )
    (dx, da1, db1, n1, dg1), _ = _ffn_bwd(dh1, xs, a1, b1, ffn1_norm, w_f1, ffn, "ffn1_bwd")
    stats = jnp.concatenate(
        [dg1, dgm, dg2, dgf, jnp.concatenate([dps, drg], axis=1), loss_cols, jnp.zeros((2, d), F32)], axis=0
    )
    small = _GatherDirect([stats, dpw.reshape(N_POOL_GROUPS * GROUP, GROUP)])
    sum_f1w2, ((stats_all, pw_all),) = _wgrad(hm1, dh1, 0.5, "ffn1_w2_grad", carries=[small])
    sum_f1w1, ((parts_f1w2,),) = _wgrad(da1, n1, 1.0, "ffn1_w1_grad", carries=[_ChipScatter([sum_f1w2])])
    sum_f1w3, ((parts_f1w1,),) = _wgrad(db1, n1, 1.0, "ffn1_w3_grad", carries=[_ChipScatter([sum_f1w1])])
    sum_in, ((parts_f1w3,),) = _wgrad(dproj, u, 1.0, "w_in_grad", carries=[_ChipScatter([sum_f1w3])])
    sum_out, ((parts_in,),) = _wgrad(mm, dh2, 1.0, "w_out_grad", carries=[_ChipScatter([sum_in])])
    ((parts_out,),) = _comm_call([_ChipScatter([sum_out])], "scatter_last")

    big = (
        (ffn1_w1, m_ffn1_w1, v_ffn1_w1, parts_f1w1, True),
        (ffn1_w3, m_ffn1_w3, v_ffn1_w3, parts_f1w3, True),
        (ffn1_w2, m_ffn1_w2, v_ffn1_w2, parts_f1w2, False),
        (w_in, m_w_in, v_w_in, parts_in, True),
        (w_out, m_w_out, v_w_out, parts_out, False),
        (ffn2_w1, m_ffn2_w1, v_ffn2_w1, parts_f2w1, True),
        (ffn2_w3, m_ffn2_w3, v_ffn2_w3, parts_f2w3, True),
        (ffn2_w2, m_ffn2_w2, v_ffn2_w2, parts_f2w2, False),
    )
    big_out = []
    for j, (w, m, v, parts, t) in enumerate(big):
        view = (lambda a: a[0].T) if t else (lambda a: a[0])
        back = (lambda a: a.T[None]) if t else (lambda a: a[None])
        big_out.append([back(a) for a in _adamw_big(view(w), parts, view(m), view(v), "adamw_%d" % j)])

    small_w = (ffn1_norm, mix_norm, ffn2_norm, fnorm, pool_scale, ret_norm, pw3.reshape(-1, GROUP))
    small_m = (m_ffn1_norm, m_mix_norm, m_ffn2_norm, m_final_norm.reshape(1, d), m_pool_scale, m_ret_norm, m_pool_w.reshape(-1, GROUP))
    small_v = (v_ffn1_norm, v_mix_norm, v_ffn2_norm, v_final_norm.reshape(1, d), v_pool_scale, v_ret_norm, v_pool_w.reshape(-1, GROUP))
    res = _adamw_small(stats_all, pw_all, small_w, small_m, small_v, pwid)
    loss = res[0][0, 0]
    small_out = [list(res[1 + 4 * j:5 + 4 * j]) for j in range(len(small_w))]
    small_out[3] = [a.reshape(d) for a in small_out[3]]
    small_out[6] = [a.reshape(pool_w.shape) for a in small_out[6]]

    order = [small_out[0], big_out[0], big_out[1], big_out[2], small_out[1], big_out[3], small_out[6], small_out[4],
             small_out[5], big_out[4], small_out[2], big_out[5], big_out[6], big_out[7], small_out[3]]
    result = [loss, dx[None]]
    for kind in range(4):
        result += [t[kind] for t in order]
    return tuple(result)
```

```python
import functools

import numpy as np
import jax
import jax.numpy as jnp
from jax import lax
from jax.experimental import pallas as pl
from jax.experimental.pallas import tpu as pltpu

F32 = jnp.float32
BF16 = jnp.bfloat16

NDEV = 8
NCHIP = 4
EPS = 1e-6
N_POOL_GROUPS = 4
POOL_WINDOWS = (2, 4, 8, 16)
MAX_WINDOW = 16
GROUP = 128
RET_HEADS = 4
ROPE_BASE = 10000.0
ADAM_LR = 0.001
ADAM_B1 = 0.9
ADAM_B2 = 0.999
ADAM_EPS = 1e-08
ADAM_WD = 0.01
ADAM_STEP = 10

VMEM_LIMIT = 56 * 1024 * 1024
FFN_CHUNK = 256
ROW_BAND = 32

NT = (((1,), (1,)), ((), ()))
NN = (((1,), (0,)), ((), ()))
TN = (((0,), (0,)), ((), ()))

ANY = pl.BlockSpec(memory_space=pl.ANY)


def _dot(a, b, dims):
    return lax.dot_general(a, b, dims, preferred_element_type=F32)


def _call(body, **kw):
    return pl.pallas_call(body, **kw)


def _params(**kw):
    return pltpu.CompilerParams(vmem_limit_bytes=VMEM_LIMIT, **kw)


def _seq(n):
    return _params(dimension_semantics=("arbitrary",) * n)


def _peer(k):
    x, y, c = lax.axis_index("x"), lax.axis_index("y"), lax.axis_index("c")
    return (1 - x if k & 4 else x, 1 - y if k & 2 else y, 1 - c if k & 1 else c)


def _flat(pos):
    return 4 * pos[0] + 2 * pos[1] + pos[2]


def _chip(pos):
    return 2 * pos[0] + pos[1]


def _row_tile(rows, cap):
    return max(t for t in range(16, min(rows, cap) + 1, 16) if rows % t == 0)


def _pieces(rows, n):
    tiles = rows // 16
    cuts = [16 * (tiles * q // n) for q in range(n + 1)]
    return [(a, b - a) for a, b in zip(cuts[:-1], cuts[1:])]


def _load_weights(parts, sems):
    copies = []
    for buf, dst in parts:
        rows = buf.shape[1]
        for p in range(NDEV):
            cp = pltpu.make_async_copy(buf.at[p], dst.at[pl.ds(p * rows, rows), :], sems.at[len(copies)])
            cp.start()
            copies.append(cp)
    return copies


def _sigmoid(a):
    return 1.0 / (1.0 + jnp.exp(-a))


def _remote(src, dst, send_sem, recv_sem, to):
    return pltpu.make_async_remote_copy(
        src_ref=src, dst_ref=dst, send_sem=send_sem, recv_sem=recv_sem, device_id=to, device_id_type=pl.DeviceIdType.MESH
    )


class _Gather:
    X, Y, FAR = 4, 2, 6
    COPIES = 8

    def __init__(self, shards):
        n = len(shards)
        self.operands = list(shards)
        self.out_shape = [jax.ShapeDtypeStruct((NDEV,) + a.shape, a.dtype) for a in shards]
        self.sems = [
            pltpu.SemaphoreType.DMA((self.COPIES * n,)), pltpu.SemaphoreType.DMA((self.COPIES * n,)),
            pltpu.SemaphoreType.DMA((n,)),
        ]
        self.stages = [self.begin, self.relay, self.relay_far, self.end]

    def _copy(self, t, k, block, to, ins, outs, sems, own=False, half=None):
        rows = outs[t].shape[1]
        part = pl.ds(0, rows) if half is None else pl.ds(half * (rows // 2), rows // 2)
        dst = outs[t].at[_flat(block), part, :]
        at = self.COPIES * t + k
        return _remote(ins[t] if own else dst, dst, sems[0].at[at], sems[1].at[at], to)

    def _local(self, t, ins, outs, sems):
        return pltpu.make_async_copy(ins[t], outs[t].at[_flat(_peer(0))], sems[2].at[t])

    def begin(self, ins, outs, sems):
        me = _peer(0)
        for t in range(len(ins)):
            self._local(t, ins, outs, sems).start()
            for k, code in enumerate((1, self.X, self.Y)):
                self._copy(t, k, me, _peer(code), ins, outs, sems, own=True).start()

    def relay(self, ins, outs, sems):
        me, sibling = _peer(0), _peer(1)
        for t in range(len(ins)):
            self._copy(t, 1, _peer(self.X), me, ins, outs, sems).wait_recv()
            self._copy(t, 3, _peer(self.X), _peer(self.Y), ins, outs, sems, half=0).start()
            self._copy(t, 5, _peer(self.X), sibling, ins, outs, sems).start()
            self._copy(t, 2, _peer(self.Y), me, ins, outs, sems).wait_recv()
            self._copy(t, 4, _peer(self.Y), _peer(self.X), ins, outs, sems, half=1).start()
            self._copy(t, 6, _peer(self.Y), sibling, ins, outs, sems).start()

    def relay_far(self, ins, outs, sems):
        me, sibling = _peer(0), _peer(1)
        for t in range(len(ins)):
            self._copy(t, 3, _peer(self.FAR), me, ins, outs, sems, half=0).wait_recv()
            self._copy(t, 4, _peer(self.FAR), me, ins, outs, sems, half=1).wait_recv()
            self._copy(t, 7, _peer(self.FAR), sibling, ins, outs, sems).start()

    def end(self, ins, outs, sems):
        me = _peer(0)
        for t in range(len(ins)):
            self._copy(t, 0, _peer(1), me, ins, outs, sems).wait_recv()
            for k, code in ((5, self.X), (6, self.Y), (7, self.FAR)):
                self._copy(t, k, _peer(code ^ 1), me, ins, outs, sems).wait_recv()
            for k in range(self.COPIES):
                self._copy(t, k, me, me, ins, outs, sems, half=0 if k == 3 else 1 if k == 4 else None).wait_send()
            self._local(t, ins, outs, sems).wait()


class _GatherDirect:
    def __init__(self, arrays):
        n = len(arrays)
        self.operands = list(arrays)
        self.out_shape = [jax.ShapeDtypeStruct((NDEV,) + a.shape, a.dtype) for a in arrays]
        self.sems = [pltpu.SemaphoreType.DMA((7 * n,)), pltpu.SemaphoreType.DMA((7 * n,)), pltpu.SemaphoreType.DMA((n,))]
        self.stages = [self.begin, self.end]

    def begin(self, ins, outs, sems):
        mine = _flat(_peer(0))
        for t in range(len(ins)):
            pltpu.make_async_copy(ins[t], outs[t].at[mine], sems[2].at[t]).start()
            for k in range(1, NDEV):
                _remote(ins[t], outs[t].at[mine], sems[0].at[7 * t + k - 1], sems[1].at[7 * t + k - 1], _peer(k)).start()

    def end(self, ins, outs, sems):
        mine = _flat(_peer(0))
        for t in range(len(ins)):
            for k in range(1, NDEV):
                cp = _remote(ins[t], outs[t].at[_flat(_peer(k))], sems[0].at[7 * t + k - 1], sems[1].at[7 * t + k - 1], _peer(k))
                cp.wait_recv()
                cp.wait_send()
            pltpu.make_async_copy(ins[t], outs[t].at[mine], sems[2].at[t]).wait()


class _ChipScatter:
    pieces = 1

    def __init__(self, sums):
        n = len(sums) * NCHIP * self.pieces
        self.operands = list(sums)
        self.out_shape = [jax.ShapeDtypeStruct(a.shape, a.dtype) for a in sums]
        self.sems = [pltpu.SemaphoreType.DMA((n,)), pltpu.SemaphoreType.DMA((n,))]
        self.stages = [self.begin, self.end]

    def _copies(self, ins, outs, sems, arriving):
        mine = _chip(_peer(0))
        copies = []
        for t in range(len(ins)):
            rows = ins[t].shape[1] // self.pieces
            for k in (0, 4, 2, 6):
                other = _chip(_peer(k))
                for q in range(self.pieces):
                    part = pl.ds(q * rows, rows)
                    at = len(copies)
                    if k == 0:
                        cp = pltpu.make_async_copy(ins[t].at[mine, part, :], outs[t].at[mine, part, :], sems[0].at[at])
                    else:
                        landing = outs[t].at[other if arriving else mine, part, :]
                        cp = _remote(ins[t].at[other, part, :], landing, sems[0].at[at], sems[1].at[at], _peer(k))
                    copies.append(cp)
        return copies

    def begin(self, ins, outs, sems):
        for cp in self._copies(ins, outs, sems, False):
            cp.start()

    def end(self, ins, outs, sems):
        for at, cp in enumerate(self._copies(ins, outs, sems, True)):
            if at % (NCHIP * self.pieces) < self.pieces:
                cp.wait()
            else:
                cp.wait_recv()
                cp.wait_send()


class _CastRows:
    def __init__(self, arrays):
        n = len(arrays)
        self.operands = list(arrays)
        self.out_shape = [jax.ShapeDtypeStruct(a.shape, BF16) for a in arrays]
        self.sems = [pltpu.SemaphoreType.DMA((n,)), pltpu.SemaphoreType.DMA((n,))]
        self.sems += [pltpu.VMEM(a.shape, F32) for a in arrays] + [pltpu.VMEM(a.shape, BF16) for a in arrays]
        self.stages = [self.begin, self.convert, self.end]

    def _moves(self, t, ins, outs, scratch):
        n = len(ins)
        load = pltpu.make_async_copy(ins[t], scratch[2 + t], scratch[0].at[t])
        store = pltpu.make_async_copy(scratch[2 + n + t], outs[t], scratch[1].at[t])
        return load, store

    def begin(self, ins, outs, scratch):
        for t in range(len(ins)):
            self._moves(t, ins, outs, scratch)[0].start()

    def convert(self, ins, outs, scratch):
        n = len(ins)
        for t in range(n):
            load, store = self._moves(t, ins, outs, scratch)
            load.wait()
            scratch[2 + n + t][...] = scratch[2 + t][...].astype(BF16)
            store.start()

    def end(self, ins, outs, scratch):
        for t in range(len(ins)):
            self._moves(t, ins, outs, scratch)[1].wait()


def _split_refs(refs, counts):
    out, at = [], 0
    for n in counts:
        out.append(refs[at:at + n])
        at += n
    return out


def _comm_call(carries, name):
    nin = [len(c.operands) for c in carries]
    nout = [len(c.out_shape) for c in carries]
    nsem = [len(c.sems) for c in carries]

    def body(*refs):
        ins, outs, sems = _split_refs(refs, (sum(nin), sum(nout), sum(nsem)))
        parts = list(zip(carries, _split_refs(ins, nin), _split_refs(outs, nout), _split_refs(sems, nsem)))
        for depth in range(max(len(c.stages) for c in carries)):
            for c, i, o, s in parts:
                if depth < len(c.stages) - 1:
                    c.stages[depth](i, o, s)
        for c, i, o, s in parts:
            c.stages[-1](i, o, s)

    res = _call(
        body,
        name=name,
        out_shape=[sh for c in carries for sh in c.out_shape],
        in_specs=[ANY] * sum(nin),
        out_specs=[ANY] * sum(nout),
        scratch_shapes=[sm for c in carries for sm in c.sems],
        compiler_params=_params(has_side_effects=True),
    )(*[a for c in carries for a in c.operands])
    return _split_refs(list(res), nout)


def _grid_call(body, carries, *, name, grid, in_specs, out_specs, out_shape, scratch_shapes, args):
    ni, no, ns = len(in_specs), len(out_specs), len(scratch_shapes)
    nin = [len(c.operands) for c in carries]
    nout = [len(c.out_shape) for c in carries]
    nsem = [len(c.sems) for c in carries]
    steps = int(np.prod(grid))

    def when_of(stage, count):
        first, last = steps // 2 - 1, (3 * steps) // 4 - 1
        return max(0, last if count <= 3 else first + (last - first) * (stage - 1) // (count - 3))

    def wrapped(*refs):
        ins, cins, outs, couts, scr, csems = _split_refs(refs, (ni, sum(nin), no, sum(nout), ns, sum(nsem)))
        if not carries:
            return body(*ins, *outs, *scr)
        parts = list(zip(carries, _split_refs(cins, nin), _split_refs(couts, nout), _split_refs(csems, nsem)))
        step = pl.program_id(0)
        for axis in range(1, len(grid)):
            step = step * grid[axis] + pl.program_id(axis)

        @pl.when(step == 0)
        def _():
            for c, i, o, s in parts:
                c.stages[0](i, o, s)

        body(*ins, *outs, *scr)

        for c, i, o, s in parts:
            for stage in range(1, len(c.stages) - 1):
                pl.when(step == when_of(stage, len(c.stages)))(functools.partial(c.stages[stage], i, o, s))

        @pl.when(step == steps - 1)
        def _():
            for c, i, o, s in parts:
                c.stages[-1](i, o, s)

    res = _call(
        wrapped,
        name=name,
        grid=tuple(grid),
        in_specs=list(in_specs) + [ANY] * sum(nin),
        out_specs=list(out_specs) + [ANY] * sum(nout),
        out_shape=list(out_shape) + [sh for c in carries for sh in c.out_shape],
        scratch_shapes=list(scratch_shapes) + [sm for c in carries for sm in c.sems],
        compiler_params=_seq(len(grid)),
    )(*args, *[a for c in carries for a in c.operands])
    res = list(res)
    return res[:no], _split_refs(res[no:], nout)


def _chunks(width):
    return [(at, min(FFN_CHUNK, width - at)) for at in range(0, width, FFN_CHUNK)]


def _start_chunk_loads(bufs, dsts, sems, chunks):
    copies = [[None] * len(chunks) for _ in bufs]
    for c, (at, width) in enumerate(chunks):
        for m, (buf, dst) in enumerate(zip(bufs, dsts)):
            cp = pltpu.make_async_copy(
                buf.at[pl.ds(at, width), :], dst.at[pl.ds(at, width), :], sems.at[m * len(chunks) + c]
            )
            cp.start()
            copies[m][c] = cp
    return copies


def _ffn_fwd(x, gain, weights, ffn, head=None, carries=()):
    s, d = x.shape
    tm = min(512, s)
    chunks = _chunks(ffn)

    def body(*refs):
        if head is None:
            x_ref, g_ref, b1, b3, b2, h_ref, a_ref, b_ref, hm_ref, w1s, w3s, w2s, sems = refs
        else:
            x_ref, g_ref, b1, b3, b2, gf_ref, t_ref, h_ref, a_ref, b_ref, hm_ref, dgf_ref, loss_ref, w1s, w3s, w2s, sems = refs

        def one_tile(loads):
            xv = x_ref[...]
            r = lax.rsqrt(jnp.mean(xv * xv, axis=-1, keepdims=True) + EPS)
            n = (xv * r * g_ref[...]).astype(BF16)
            acc = jnp.zeros((tm, d), F32)
            for c, (at, width) in enumerate(chunks):
                cols = slice(at, at + width)
                if loads is not None:
                    for m in range(3):
                        loads[m][c].wait()
                a = _dot(n, w1s[cols, :], NT)
                b = _dot(n, w3s[cols, :], NT)
                a_ref[:, cols] = a.astype(BF16)
                b_ref[:, cols] = b.astype(BF16)
                hm = (a * _sigmoid(a) * b).astype(BF16)
                hm_ref[:, cols] = hm
                acc = acc + _dot(hm, w2s[cols, :], NN)
            h = xv + 0.5 * acc
            if head is None:
                h_ref[...] = h
            else:
                rf = lax.rsqrt(jnp.mean(h * h, axis=-1, keepdims=True) + EPS)
                nh = h * rf
                gf = gf_ref[...]
                err = nh * gf - t_ref[...]
                loss_ref[...] += jnp.sum(err * err, axis=0, keepdims=True) * (0.5 / d)
                dy = err * (1.0 / d)
                dgf_ref[...] += jnp.sum(dy * nh, axis=0, keepdims=True)
                dn = dy * gf
                h_ref[...] = rf * (dn - nh * jnp.mean(dn * nh, axis=-1, keepdims=True))

        if head is None:
            @pl.when(pl.program_id(0) == 0)
            def _():
                one_tile(_start_chunk_loads((b1, b3, b2), (w1s, w3s, w2s), sems, chunks))

            @pl.when(pl.program_id(0) > 0)
            def _():
                one_tile(None)
        else:
            @pl.when(pl.program_id(0) == 0)
            def _():
                dgf_ref[...] = jnp.zeros_like(dgf_ref)
                loss_ref[...] = jnp.zeros_like(loss_ref)
                for loads in _start_chunk_loads((b1, b3, b2), (w1s, w3s, w2s), sems, chunks):
                    for cp in loads:
                        cp.wait()

            one_tile(None)

    tile = pl.BlockSpec((tm, d), lambda i: (i, 0))
    row = pl.BlockSpec((1, d), lambda i: (0, 0))
    wide = pl.BlockSpec((tm, ffn), lambda i: (i, 0))
    in_specs = [tile, row, ANY, ANY, ANY]
    out_shape = [jax.ShapeDtypeStruct((s, d), F32)] + [jax.ShapeDtypeStruct((s, ffn), BF16)] * 3
    out_specs = [tile, wide, wide, wide]
    args = [x, gain] + [w.reshape(ffn, d) for w in weights]
    if head is not None:
        in_specs += [row, tile]
        args += list(head)
        out_shape += [jax.ShapeDtypeStruct((1, d), F32)] * 2
        out_specs += [row, row]
    return _grid_call(
        body,
        carries,
        name="ffn_fwd_loss" if head is not None else "ffn_fwd",
        grid=(s // tm,),
        in_specs=in_specs,
        out_specs=out_specs,
        out_shape=out_shape,
        scratch_shapes=[pltpu.VMEM((ffn, d), BF16)] * 3 + [pltpu.SemaphoreType.DMA((3 * len(chunks),))],
        args=args,
    )


def _ffn_bwd(dh, x, a, b, gain, weights, ffn, name, carries=()):
    s, d = x.shape
    tm = min(512, s)
    halves = 2
    fh = ffn // halves

    def body(dh_ref, x_ref, a_ref, b_ref, g_ref, b1, b3, b2, dx_ref, da_ref, db_ref, n_ref, dg_ref, w1s, w3s, w2s, sems):
        i, j = pl.program_id(0), pl.program_id(1)

        @pl.when((i == 0) & (j == 0))
        def _():
            for cp in _load_weights(((b1, w1s), (b3, w3s), (b2, w2s)), sems):
                cp.wait()
            dg_ref[...] = jnp.zeros_like(dg_ref)

        @pl.when(j == 0)
        def _():
            dx_ref[...] = jnp.zeros_like(dx_ref)

        dob = (0.5 * dh_ref[...]).astype(BF16)
        chunks = _chunks(fh)

        def dhm_of(k):
            at, width = chunks[k]
            return _dot(dob, w2s[pl.ds(pl.multiple_of(j * fh + at, GROUP), width), :], NT)

        ahead = dhm_of(0)
        for k, (at, width) in enumerate(chunks):
            cols = slice(at, at + width)
            dhm = ahead
            if k + 1 < len(chunks):
                ahead = dhm_of(k + 1)
            for top in range(0, tm, ROW_BAND):
                band = slice(top, top + ROW_BAND)
                av = a_ref[band, cols].astype(F32)
                bv = b_ref[band, cols].astype(F32)
                sg = _sigmoid(av)
                dv = dhm[band]
                da_ref[band, cols] = (dv * bv * (sg * (1.0 + av * (1.0 - sg)))).astype(BF16)
                db_ref[band, cols] = (dv * (av * sg)).astype(BF16)
        half = pl.ds(pl.multiple_of(j * fh, GROUP), fh)
        dx_ref[...] += _dot(da_ref[...], w1s[half, :], NN) + _dot(db_ref[...], w3s[half, :], NN)

        @pl.when(j == halves - 1)
        def _():
            xv = x_ref[...]
            g = g_ref[...]
            r = lax.rsqrt(jnp.mean(xv * xv, axis=-1, keepdims=True) + EPS)
            nh = xv * r
            n_ref[...] = (nh * g).astype(BF16)
            total = dx_ref[...]
            dg_ref[...] += jnp.sum(total * nh, axis=0, keepdims=True)
            dnh = total * g
            dx_ref[...] = dh_ref[...] + r * (dnh - nh * jnp.mean(dnh * nh, axis=-1, keepdims=True))

    tile = pl.BlockSpec((tm, d), lambda i, j: (i, 0))
    row = pl.BlockSpec((1, d), lambda i, j: (0, 0))
    wide = pl.BlockSpec((tm, fh), lambda i, j: (i, j))
    return _grid_call(
        body,
        carries,
        name=name,
        grid=(s // tm, halves),
        in_specs=[tile, tile, wide, wide, row, ANY, ANY, ANY],
        out_specs=[tile, wide, wide, tile, row],
        out_shape=[
            jax.ShapeDtypeStruct((s, d), F32),
            jax.ShapeDtypeStruct((s, ffn), BF16),
            jax.ShapeDtypeStruct((s, ffn), BF16),
            jax.ShapeDtypeStruct((s, d), BF16),
            jax.ShapeDtypeStruct((1, d), F32),
        ],
        scratch_shapes=[pltpu.VMEM((ffn, d), BF16)] * 3 + [pltpu.SemaphoreType.DMA((3 * NDEV,))],
        args=[dh, x, a, b, gain] + list(weights),
    )


SWAP_PIECES = 1


def _wgrad(lhs, rhs, scale, name, carries=()):
    s, m = lhs.shape
    n = rhs.shape[1]
    rs = m // NDEV
    tk = min(1024, s)
    steps = s // tk
    pieces = [(j, at, size) for j in range(2) for at, size in _pieces(rs, SWAP_PIECES)]

    def body(l_ref, r_ref, o_ref, acc, mine, theirs, send_sems, recv_sems):
        h, k = pl.program_id(0), pl.program_id(1)

        @pl.when(k == 0)
        def _():
            acc[...] = _dot(l_ref[...], r_ref[...].astype(BF16), TN)

        @pl.when(k > 0)
        def _():
            acc[...] += _dot(l_ref[...], r_ref[...].astype(BF16), TN)

        def exchange(half):
            c = lax.axis_index("c")
            return [
                _remote(mine.at[half, 1 - c, j, pl.ds(at, size), :], theirs.at[half, j, pl.ds(at, size), :],
                        send_sems.at[half * len(pieces) + q], recv_sems.at[half * len(pieces) + q], _peer(1))
                for q, (j, at, size) in enumerate(pieces)
            ]

        def settle(half):
            for cp in exchange(half):
                cp.wait_recv()
            both = mine[half, lax.axis_index("c")].astype(F32) + theirs[half].astype(F32)
            o_ref[2 * half:2 * half + 2] = both.astype(BF16)
            for cp in exchange(half):
                cp.wait_send()

        for half in range(2):
            @pl.when((h == half) & (k == steps - 1))
            def _():
                for p in range(NCHIP):
                    mine[half, p % 2, p // 2] = (acc[p * rs:(p + 1) * rs, :] * scale).astype(BF16)
                for cp in exchange(half):
                    cp.start()
                if half == 1:
                    settle(0)
                    settle(1)

    (out,), carried = _grid_call(
        body,
        carries,
        name=name,
        grid=(2, steps),
        in_specs=[pl.BlockSpec((tk, m // 2), lambda h, k: (k, h)), pl.BlockSpec((tk, n), lambda h, k: (k, 0))],
        out_specs=[pl.BlockSpec((NCHIP, rs, n), lambda h, k: (0, 0, 0))],
        out_shape=[jax.ShapeDtypeStruct((NCHIP, rs, n), BF16)],
        scratch_shapes=[
            pltpu.VMEM((m // 2, n), F32), pltpu.VMEM((2, 2, 2, rs, n), BF16), pltpu.VMEM((2, 2, rs, n), BF16),
            pltpu.SemaphoreType.DMA((2 * len(pieces),)), pltpu.SemaphoreType.DMA((2 * len(pieces),)),
        ],
        args=[lhs, rhs],
    )
    return out, carried


def _mix_constants(s):
    c = GROUP
    lg = np.log1p(-np.exp2(-5.0 - np.arange(RET_HEADS, dtype=np.float32))).astype(np.float32)
    pos = np.arange(c, dtype=np.float32)
    rel = pos[:, None] - pos[None, :]
    decay = np.where(rel[None] >= 0, np.exp(lg[:, None, None] * np.maximum(rel, 0.0)[None]), 0.0).astype(np.float32)
    ktail = np.exp(lg[:, None] * (c - 1 - pos)[None, :]).astype(np.float32)
    qhead = np.exp(lg[:, None] * (pos + 1.0)[None, :]).astype(np.float32)
    chunk_decay = [float(v) for v in np.exp(lg * np.float32(c)).astype(np.float32)]
    ones = np.ones((1, 1, c), np.float32)
    inv_freq = (1.0 / (np.float32(ROPE_BASE) ** (np.arange(0, c, 2, dtype=np.float32) / np.float32(c)))).astype(np.float32)
    ang = (np.arange(s, dtype=np.float32)[:, None] * inv_freq[None, :]).astype(np.float32)
    cos, sin = np.cos(ang).astype(np.float32), np.sin(ang).astype(np.float32)
    return dict(
        decay=jnp.asarray(decay),
        ktail=jnp.asarray(ktail[:, :, None] * ones),
        qhead=jnp.asarray(qhead[:, :, None] * ones),
        chunk_decay=chunk_decay,
        cos=jnp.asarray(np.concatenate([cos, cos], axis=-1)),
        sin=jnp.asarray(np.concatenate([-sin, sin], axis=-1)),
    )


def _rope(t, cos, sin):
    return t * cos + pltpu.roll(t, GROUP // 2, axis=1) * sin


def _rope_bwd(dt, cos, sin):
    return dt * cos + pltpu.roll(dt * sin, GROUP // 2, axis=1)


def _window_sums(ext, w, forward):
    rows = ext.shape[0]
    acc, k = ext, 1
    while k < w:
        acc = acc + pltpu.roll(acc, k if forward else rows - k, axis=0)
        k *= 2
    return acc


def _pool_counts(tile, tm, w):
    t = lax.broadcasted_iota(jnp.int32, (tm, 1), 0) + tile * tm
    return jnp.minimum(t + 1, w).astype(F32)


def _mix_fwd(h1, gain, weights, pool_w, pool_scale, ret_gain, consts, carries=()):
    s, d = h1.shape
    pwid = N_POOL_GROUPS * GROUP
    rwid = RET_HEADS * GROUP
    inw = pwid + 4 * rwid
    tm = min(256, s)
    nck = tm // GROUP
    cd = consts["chunk_decay"]

    def body(h_ref, g_ref, bin_, bout, pw_ref, ps_ref, rg_ref, cos_ref, sin_ref, dec_ref, kt_ref, qh_ref,
             h2_ref, proj_ref, o_ref, rs_ref, wins, wouts, state, carry, mbuf, sems):
        i = pl.program_id(0)

        @pl.when(i == 0)
        def _():
            for cp in _load_weights(((bin_, wins), (bout, wouts)), sems):
                cp.wait()
            state[...] = jnp.zeros_like(state)
            carry[...] = jnp.zeros_like(carry)

        hv = h_ref[...]
        r = lax.rsqrt(jnp.mean(hv * hv, axis=-1, keepdims=True) + EPS)
        u = (hv * r * g_ref[...]).astype(BF16)
        proj_ref[...] = _dot(u, wins[...], NT)

        ext = jnp.concatenate([carry[...], proj_ref[:, 0:pwid]], axis=0)
        carry[...] = proj_ref[tm - MAX_WINDOW:tm, 0:pwid]
        for gi, w in enumerate(POOL_WINDOWS):
            cols = slice(gi * GROUP, (gi + 1) * GROUP)
            xg = ext[:, cols]
            ws = _window_sums(xg, w, True)[MAX_WINDOW:, :]
            pooled = ws / _pool_counts(i, tm, w) - xg[MAX_WINDOW:, :]
            z = _dot(pooled.astype(BF16), pw_ref[gi].astype(BF16), NN)
            mbuf[:, cols] = (z * ps_ref[:, cols]).astype(BF16)

        cos, sin = cos_ref[...], sin_ref[...]
        for h in range(RET_HEADS):
            cq = slice(pwid + h * GROUP, pwid + (h + 1) * GROUP)
            ck = slice(pwid + rwid + h * GROUP, pwid + rwid + (h + 1) * GROUP)
            cv = slice(pwid + 2 * rwid + h * GROUP, pwid + 2 * rwid + (h + 1) * GROUP)
            cg = slice(pwid + 3 * rwid + h * GROUP, pwid + 3 * rwid + (h + 1) * GROUP)
            ch = slice(h * GROUP, (h + 1) * GROUP)
            qr = _rope(proj_ref[:, cq], cos, sin)
            kr = _rope(proj_ref[:, ck], cos, sin) * (GROUP ** -0.5)
            vb = proj_ref[:, cv].astype(BF16)
            for n in range(nck):
                rows = slice(n * GROUP, (n + 1) * GROUP)
                qc, kc, vc = qr[rows], kr[rows], vb[rows]
                rb = state[h]
                rs_ref[n, h] = rb
                p = (_dot(qc.astype(BF16), kc.astype(BF16), NT) * dec_ref[h]).astype(BF16)
                o = _dot(p, vc, NN) + _dot((qc * qh_ref[h]).astype(BF16), rb.astype(BF16), NN)
                state[h] = cd[h] * rb + _dot((kc * kt_ref[h]).astype(BF16), vc, TN)
                o_ref[rows, ch] = o
                on = o * lax.rsqrt(jnp.mean(o * o, axis=-1, keepdims=True) + EPS)
                gv = proj_ref[rows, cg]
                mbuf[rows, pwid + h * GROUP:pwid + (h + 1) * GROUP] = (
                    gv * _sigmoid(gv) * (on * rg_ref[:, ch])
                ).astype(BF16)
        h2_ref[...] = hv + _dot(mbuf[...], wouts[...], NN)

    tile = pl.BlockSpec((tm, d), lambda i: (i, 0))
    full = lambda shape: pl.BlockSpec(shape, lambda i: (0,) * len(shape))
    return _grid_call(
        body,
        carries,
        name="mix_fwd",
        grid=(s // tm,),
        in_specs=[
            tile, full((1, d)), ANY, ANY,
            full((N_POOL_GROUPS, GROUP, GROUP)), full((1, pwid)), full((1, rwid)),
            pl.BlockSpec((tm, GROUP), lambda i: (i, 0)), pl.BlockSpec((tm, GROUP), lambda i: (i, 0)),
            full((RET_HEADS, GROUP, GROUP)), full((RET_HEADS, GROUP, GROUP)), full((RET_HEADS, GROUP, GROUP)),
        ],
        out_specs=[
            tile,
            pl.BlockSpec((tm, inw), lambda i: (i, 0)),
            pl.BlockSpec((tm, rwid), lambda i: (i, 0)),
            pl.BlockSpec((nck, RET_HEADS, GROUP, GROUP), lambda i: (i, 0, 0, 0)),
        ],
        out_shape=[
            jax.ShapeDtypeStruct((s, d), F32),
            jax.ShapeDtypeStruct((s, inw), F32),
            jax.ShapeDtypeStruct((s, rwid), F32),
            jax.ShapeDtypeStruct((s // GROUP, RET_HEADS, GROUP, GROUP), F32),
        ],
        scratch_shapes=[
            pltpu.VMEM((inw, d), BF16), pltpu.VMEM((d, d), BF16),
            pltpu.VMEM((RET_HEADS, GROUP, GROUP), F32), pltpu.VMEM((MAX_WINDOW, pwid), F32),
            pltpu.VMEM((tm, d), BF16), pltpu.SemaphoreType.DMA((2 * NDEV,)),
        ],
        args=[h1, gain, weights[0], weights[1], pool_w, pool_scale, ret_gain,
              consts["cos"], consts["sin"], consts["decay"], consts["ktail"], consts["qhead"]],
    )


def _mix_bwd(dh2, h1, proj, o_saved, rsave, gain, weights, pool_w, pool_scale, ret_gain, consts, carries=()):
    s, d = h1.shape
    pwid = N_POOL_GROUPS * GROUP
    rwid = RET_HEADS * GROUP
    inw = pwid + 4 * rwid
    tm = min(256, s)
    nck = tm // GROUP
    nt = s // tm
    cd = consts["chunk_decay"]
    halo_per_tile = tm // MAX_WINDOW

    def body(dh2_ref, h_ref, proj_ref, halo_ref, o_ref, rs_ref, g_ref, bin_, bout, pw_ref, ps_ref, rg_ref,
             cos_ref, sin_ref, dec_ref, kt_ref, qh_ref,
             dh1_ref, dproj_ref, u_ref, m_ref, dpw_ref, dps_ref, drg_ref, dg_ref,
             wins, wouts, dstate, carry, dm, dpj, sems):
        i = pl.program_id(0)
        tile = nt - 1 - i

        @pl.when(i == 0)
        def _():
            for cp in _load_weights(((bin_, wins), (bout, wouts)), sems):
                cp.wait()
            dstate[...] = jnp.zeros_like(dstate)
            carry[...] = jnp.zeros_like(carry)
            for ref in (dpw_ref, dps_ref, drg_ref, dg_ref):
                ref[...] = jnp.zeros_like(ref)

        dh2v = dh2_ref[...]
        dm[...] = _dot(dh2v.astype(BF16), wouts[...], NT)
        hv = h_ref[...]
        g = g_ref[...]
        r = lax.rsqrt(jnp.mean(hv * hv, axis=-1, keepdims=True) + EPS)
        uh = hv * r
        u_ref[...] = (uh * g).astype(BF16)

        halo = jnp.where(tile == 0, 0.0, halo_ref[...])
        ext = jnp.concatenate([halo, proj_ref[:, 0:pwid]], axis=0)
        next_dpn = carry[...]
        for gi, w in enumerate(POOL_WINDOWS):
            cols = slice(gi * GROUP, (gi + 1) * GROUP)
            xg = ext[:, cols]
            cnt = _pool_counts(tile, tm, w)
            pooled = (_window_sums(xg, w, True)[MAX_WINDOW:, :] / cnt - xg[MAX_WINDOW:, :]).astype(BF16)
            pwb = pw_ref[gi].astype(BF16)
            z = _dot(pooled, pwb, NN)
            scale = ps_ref[:, cols]
            m_ref[:, cols] = (z * scale).astype(BF16)
            da = dm[:, cols]
            dps_ref[:, cols] += jnp.sum(da * z, axis=0, keepdims=True)
            dz = (da * scale).astype(BF16)
            dpw_ref[gi] += _dot(pooled, dz, TN)
            dpl = _dot(dz, pwb, NT)
            dpn = dpl / cnt
            ext2 = jnp.concatenate([dpn, next_dpn[:, cols]], axis=0)
            dpj[:, cols] = (_window_sums(ext2, w, False)[0:tm, :] - dpl).astype(BF16)
            carry[:, cols] = dpn[0:MAX_WINDOW, :]

        cos, sin = cos_ref[...], sin_ref[...]
        for h in range(RET_HEADS):
            cq = slice(pwid + h * GROUP, pwid + (h + 1) * GROUP)
            ck = slice(pwid + rwid + h * GROUP, pwid + rwid + (h + 1) * GROUP)
            cv = slice(pwid + 2 * rwid + h * GROUP, pwid + 2 * rwid + (h + 1) * GROUP)
            cg = slice(pwid + 3 * rwid + h * GROUP, pwid + 3 * rwid + (h + 1) * GROUP)
            ch = slice(h * GROUP, (h + 1) * GROUP)
            qr = _rope(proj_ref[:, cq], cos, sin)
            kr = _rope(proj_ref[:, ck], cos, sin) * (GROUP ** -0.5)
            vb = proj_ref[:, cv].astype(BF16)
            gv = proj_ref[:, cg]
            ov = o_ref[:, ch]
            ro = lax.rsqrt(jnp.mean(ov * ov, axis=-1, keepdims=True) + EPS)
            on = ov * ro
            rg = rg_ref[:, ch]
            db = dm[:, pwid + h * GROUP:pwid + (h + 1) * GROUP]
            sg = _sigmoid(gv)
            sl = gv * sg
            m_ref[:, pwid + h * GROUP:pwid + (h + 1) * GROUP] = (sl * (on * rg)).astype(BF16)
            dpj[:, cg] = (db * (on * rg) * (sg * (1.0 + gv * (1.0 - sg)))).astype(BF16)
            drg_ref[:, ch] += jnp.sum(db * sl * on, axis=0, keepdims=True)
            don = db * sl * rg
            do = (ro * (don - on * jnp.mean(don * on, axis=-1, keepdims=True))).astype(BF16)
            for n in reversed(range(nck)):
                rows = slice(n * GROUP, (n + 1) * GROUP)
                qc, kc, vc, dob = qr[rows], kr[rows], vb[rows], do[rows]
                qcb, kcb = qc.astype(BF16), kc.astype(BF16)
                qh = (qc * qh_ref[h]).astype(BF16)
                kt = (kc * kt_ref[h]).astype(BF16)
                rn = rs_ref[n, h].astype(BF16)
                dnext = dstate[h]
                dnb = dnext.astype(BF16)
                dec = dec_ref[h]
                p = (_dot(qcb, kcb, NT) * dec).astype(BF16)
                ds = (_dot(dob, vc, NT) * dec).astype(BF16)
                dv = _dot(p, dob, TN) + _dot(kt, dnb, NN)
                dq = _dot(ds, kcb, NN) + _dot(dob, rn, NT) * qh_ref[h]
                dk = _dot(ds, qcb, TN) + _dot(vc, dnb, NT) * kt_ref[h]
                dstate[h] = cd[h] * dnext + _dot(qh, dob, TN)
                dpj[rows, cq] = _rope_bwd(dq, cos[rows], sin[rows]).astype(BF16)
                dpj[rows, ck] = _rope_bwd(dk * (GROUP ** -0.5), cos[rows], sin[rows]).astype(BF16)
                dpj[rows, cv] = dv.astype(BF16)

        dproj_ref[...] = dpj[...]
        du = _dot(dpj[...], wins[...], NN)
        dg_ref[...] += jnp.sum(du * uh, axis=0, keepdims=True)
        dn = du * g
        dh1_ref[...] = dh2v + r * (dn - uh * jnp.mean(dn * uh, axis=-1, keepdims=True))

    rev = lambda i: (nt - 1 - i, 0)
    tile = pl.BlockSpec((tm, d), rev)
    full = lambda shape: pl.BlockSpec(shape, lambda i: (0,) * len(shape))
    return _grid_call(
        body,
        carries,
        name="mix_bwd",
        grid=(nt,),
        in_specs=[
            tile, tile,
            pl.BlockSpec((tm, inw), rev),
            pl.BlockSpec((MAX_WINDOW, pwid), lambda i: (jnp.maximum((nt - 1 - i) * halo_per_tile - 1, 0), 0)),
            pl.BlockSpec((tm, rwid), rev),
            pl.BlockSpec((nck, RET_HEADS, GROUP, GROUP), lambda i: (nt - 1 - i, 0, 0, 0)),
            full((1, d)), ANY, ANY,
            full((N_POOL_GROUPS, GROUP, GROUP)), full((1, pwid)), full((1, rwid)),
            pl.BlockSpec((tm, GROUP), rev), pl.BlockSpec((tm, GROUP), rev),
            full((RET_HEADS, GROUP, GROUP)), full((RET_HEADS, GROUP, GROUP)), full((RET_HEADS, GROUP, GROUP)),
        ],
        out_specs=[
            tile, pl.BlockSpec((tm, inw), rev), tile, tile,
            full((N_POOL_GROUPS, GROUP, GROUP)), full((1, pwid)), full((1, rwid)), full((1, d)),
        ],
        out_shape=[
            jax.ShapeDtypeStruct((s, d), F32),
            jax.ShapeDtypeStruct((s, inw), BF16),
            jax.ShapeDtypeStruct((s, d), BF16),
            jax.ShapeDtypeStruct((s, d), BF16),
            jax.ShapeDtypeStruct((N_POOL_GROUPS, GROUP, GROUP), F32),
            jax.ShapeDtypeStruct((1, pwid), F32),
            jax.ShapeDtypeStruct((1, rwid), F32),
            jax.ShapeDtypeStruct((1, d), F32),
        ],
        scratch_shapes=[
            pltpu.VMEM((inw, d), BF16), pltpu.VMEM((d, d), BF16),
            pltpu.VMEM((RET_HEADS, GROUP, GROUP), F32), pltpu.VMEM((MAX_WINDOW, pwid), F32),
            pltpu.VMEM((tm, d), F32), pltpu.VMEM((tm, inw), BF16), pltpu.SemaphoreType.DMA((2 * NDEV,)),
        ],
        args=[dh2, h1, proj, proj, o_saved, rsave, gain, weights[0], weights[1], pool_w, pool_scale, ret_gain,
              consts["cos"], consts["sin"], consts["decay"], consts["ktail"], consts["qhead"]],
    )


def _adam(w, g, m, v):
    m = ADAM_B1 * m + (1.0 - ADAM_B1) * g
    v = ADAM_B2 * v + (1.0 - ADAM_B2) * jnp.square(g)
    m_hat = m / (1.0 - ADAM_B1 ** ADAM_STEP)
    v_hat = v / (1.0 - ADAM_B2 ** ADAM_STEP)
    delta = -ADAM_LR * (m_hat / (jnp.sqrt(v_hat) + ADAM_EPS) + ADAM_WD * w)
    return delta, m, v


def _adamw_big(w, parts, m, v, name):
    rows, d = w.shape
    tr = _row_tile(rows, 176)

    def body(w_ref, p_ref, m_ref, v_ref, g_ref, d_ref, nm_ref, nv_ref):
        g = p_ref[0].astype(F32)
        for q in range(1, NCHIP):
            g = g + p_ref[q].astype(F32)
        g_ref[...] = g
        d_ref[...], nm_ref[...], nv_ref[...] = _adam(w_ref[...], g, m_ref[...], v_ref[...])

    spec = pl.BlockSpec((tr, d), lambda i: (i, 0))
    return _call(
        body,
        name=name,
        grid=(rows // tr,),
        in_specs=[spec, pl.BlockSpec((NCHIP, tr, d), lambda i: (0, i, 0)), spec, spec],
        out_specs=[spec] * 4,
        out_shape=[jax.ShapeDtypeStruct((rows, d), F32)] * 4,
        compiler_params=_seq(1),
    )(w, parts, m, v)


def _adamw_small(stats_all, pw_all, ws, ms, vs, pwid):
    nsmall = len(ws)

    def body(*refs):
        st_ref, pwa_ref = refs[0], refs[1]
        w_refs = refs[2:2 + nsmall]
        m_refs = refs[2 + nsmall:2 + 2 * nsmall]
        v_refs = refs[2 + 2 * nsmall:2 + 3 * nsmall]
        outs = refs[2 + 3 * nsmall:]
        st = st_ref[0]
        pwg = pwa_ref[0]
        for q in range(1, NDEV):
            st = st + st_ref[q]
            pwg = pwg + pwa_ref[q]
        grads = [st[0:1, :], st[1:2, :], st[2:3, :], st[3:4, :], st[4:5, 0:pwid], st[4:5, pwid:2 * pwid], pwg]
        outs[0][...] = jnp.zeros((1, GROUP), F32) + jnp.sum(st[5:6, :])
        for j in range(nsmall):
            delta, nm, nv = _adam(w_refs[j][...], grads[j], m_refs[j][...], v_refs[j][...])
            outs[1 + 4 * j][...] = grads[j]
            outs[2 + 4 * j][...] = delta
            outs[3 + 4 * j][...] = nm
            outs[4 + 4 * j][...] = nv

    out_shape = [jax.ShapeDtypeStruct((1, GROUP), F32)]
    for w in ws:
        out_shape += [jax.ShapeDtypeStruct(w.shape, F32)] * 4
    return _call(body, name="adamw_small", out_shape=out_shape, compiler_params=_params())(
        stats_all, pw_all, *ws, *ms, *vs
    )


def kernel(x, ffn1_norm, ffn1_w1, ffn1_w3, ffn1_w2, mix_norm, w_in, pool_w, pool_scale, ret_norm, w_out, ffn2_norm, ffn2_w1, ffn2_w3, ffn2_w2, final_norm, loss_target, m_ffn1_norm, m_ffn1_w1, m_ffn1_w3, m_ffn1_w2, m_mix_norm, m_w_in, m_pool_w, m_pool_scale, m_ret_norm, m_w_out, m_ffn2_norm, m_ffn2_w1, m_ffn2_w3, m_ffn2_w2, m_final_norm, v_ffn1_norm, v_ffn1_w1, v_ffn1_w3, v_ffn1_w2, v_mix_norm, v_w_in, v_pool_w, v_pool_scale, v_ret_norm, v_w_out, v_ffn2_norm, v_ffn2_w1, v_ffn2_w3, v_ffn2_w2, v_final_norm):
    s, d = x.shape[1], x.shape[2]
    ffn = ffn1_w1.shape[2] * NDEV
    pwid = pool_scale.shape[1]
    xs, tgt = x[0], loss_target[0]
    consts = _mix_constants(s)
    pw3 = pool_w[0]
    fnorm = final_norm.reshape(1, d)

    rows_of = lambda w, transposed: w[0].T if transposed else w[0]
    send_f1 = [rows_of(w, t).astype(BF16) for w, t in ((ffn1_w1, True), (ffn1_w3, True), (ffn1_w2, False))]
    later = [rows_of(w, t) for w, t in ((w_in, True), (w_out, False), (ffn2_w1, True), (ffn2_w3, True), (ffn2_w2, False))]

    sent_later, w_f1 = _comm_call([_CastRows(later), _Gather(send_f1)], "gather_ffn1")
    send_mix, send_f2 = sent_later[:2], sent_later[2:]
    (h1, a1, b1, hm1), (more,) = _ffn_fwd(xs, ffn1_norm, w_f1, ffn, carries=[_Gather(send_mix + send_f2[:1])])
    w_mix = more[:2]
    (h2, proj, o_saved, rsave), (rest,) = _mix_fwd(
        h1, mix_norm, w_mix, pw3, pool_scale, ret_norm, consts, carries=[_Gather(send_f2[1:])]
    )
    w_f2 = more[2:] + rest
    (dh3, a2, b2, hm2, dgf, loss_cols), _ = _ffn_fwd(h2, ffn2_norm, w_f2, ffn, head=(fnorm, tgt))

    (dh2, da2, db2, n2, dg2), _ = _ffn_bwd(dh3, h2, a2, b2, ffn2_norm, w_f2, ffn, "ffn2_bwd")
    sum_f2w1, _ = _wgrad(da2, n2, 1.0, "ffn2_w1_grad")
    sum_f2w3, _ = _wgrad(db2, n2, 1.0, "ffn2_w3_grad")
    sum_f2w2, ((parts_f2w1,),) = _wgrad(hm2, dh3, 0.5, "ffn2_w2_grad", carries=[_ChipScatter([sum_f2w1])])

    (dh1, dproj, u, mm, dpw, dps, drg, dgm), ((parts_f2w3, parts_f2w2),) = _mix_bwd(
        dh2, h1, proj, o_saved, rsave, mix_norm, w_mix, pw3, pool_scale, ret_norm, consts,
        carries=[_ChipScatter([sum_f2w3, sum_f2w2])],
    )
    (dx, da1, db1, n1, dg1), _ = _ffn_bwd(dh1, xs, a1, b1, ffn1_norm, w_f1, ffn, "ffn1_bwd")
    stats = jnp.concatenate(
        [dg1, dgm, dg2, dgf, jnp.concatenate([dps, drg], axis=1), loss_cols, jnp.zeros((2, d), F32)], axis=0
    )
    small = _GatherDirect([stats, dpw.reshape(N_POOL_GROUPS * GROUP, GROUP)])
    sum_f1w2, ((stats_all, pw_all),) = _wgrad(hm1, dh1, 0.5, "ffn1_w2_grad", carries=[small])
    sum_f1w1, ((parts_f1w2,),) = _wgrad(da1, n1, 1.0, "ffn1_w1_grad", carries=[_ChipScatter([sum_f1w2])])
    sum_f1w3, ((parts_f1w1,),) = _wgrad(db1, n1, 1.0, "ffn1_w3_grad", carries=[_ChipScatter([sum_f1w1])])
    sum_in, ((parts_f1w3,),) = _wgrad(dproj, u, 1.0, "w_in_grad", carries=[_ChipScatter([sum_f1w3])])
    sum_out, ((parts_in,),) = _wgrad(mm, dh2, 1.0, "w_out_grad", carries=[_ChipScatter([sum_in])])
    ((parts_out,),) = _comm_call([_ChipScatter([sum_out])], "scatter_last")

    big = (
        (ffn1_w1, m_ffn1_w1, v_ffn1_w1, parts_f1w1, True),
        (ffn1_w3, m_ffn1_w3, v_ffn1_w3, parts_f1w3, True),
        (ffn1_w2, m_ffn1_w2, v_ffn1_w2, parts_f1w2, False),
        (w_in, m_w_in, v_w_in, parts_in, True),
        (w_out, m_w_out, v_w_out, parts_out, False),
        (ffn2_w1, m_ffn2_w1, v_ffn2_w1, parts_f2w1, True),
        (ffn2_w3, m_ffn2_w3, v_ffn2_w3, parts_f2w3, True),
        (ffn2_w2, m_ffn2_w2, v_ffn2_w2, parts_f2w2, False),
    )
    big_out = []
    for j, (w, m, v, parts, t) in enumerate(big):
        view = (lambda a: a[0].T) if t else (lambda a: a[0])
        back = (lambda a: a.T[None]) if t else (lambda a: a[None])
        big_out.append([back(a) for a in _adamw_big(view(w), parts, view(m), view(v), "adamw_%d" % j)])

    small_w = (ffn1_norm, mix_norm, ffn2_norm, fnorm, pool_scale, ret_norm, pw3.reshape(-1, GROUP))
    small_m = (m_ffn1_norm, m_mix_norm, m_ffn2_norm, m_final_norm.reshape(1, d), m_pool_scale, m_ret_norm, m_pool_w.reshape(-1, GROUP))
    small_v = (v_ffn1_norm, v_mix_norm, v_ffn2_norm, v_final_norm.reshape(1, d), v_pool_scale, v_ret_norm, v_pool_w.reshape(-1, GROUP))
    res = _adamw_small(stats_all, pw_all, small_w, small_m, small_v, pwid)
    loss = res[0][0, 0]
    small_out = [list(res[1 + 4 * j:5 + 4 * j]) for j in range(len(small_w))]
    small_out[3] = [a.reshape(d) for a in small_out[3]]
    small_out[6] = [a.reshape(pool_w.shape) for a in small_out[6]]

    order = [small_out[0], big_out[0], big_out[1], big_out[2], small_out[1], big_out[3], small_out[6], small_out[4],
             small_out[5], big_out[4], small_out[2], big_out[5], big_out[6], big_out[7], small_out[3]]
    result = [loss, dx[None]]
    for kind in range(4):
        result += [t[kind] for t in order]
    return tuple(result)
```

```python
import functools

import numpy as np
import jax
import jax.numpy as jnp
from jax import lax
from jax.experimental import pallas as pl
from jax.experimental.pallas import tpu as pltpu

F32 = jnp.float32
BF16 = jnp.bfloat16

NDEV = 8
NCHIP = 4
EPS = 1e-6
N_POOL_GROUPS = 4
POOL_WINDOWS = (2, 4, 8, 16)
MAX_WINDOW = 16
GROUP = 128
RET_HEADS = 4
ROPE_BASE = 10000.0
ADAM_LR = 0.001
ADAM_B1 = 0.9
ADAM_B2 = 0.999
ADAM_EPS = 1e-08
ADAM_WD = 0.01
ADAM_STEP = 10

VMEM_LIMIT = 56 * 1024 * 1024
FFN_CHUNK = 256
ROW_BAND = 32

NT = (((1,), (1,)), ((), ()))
NN = (((1,), (0,)), ((), ()))
TN = (((0,), (0,)), ((), ()))

ANY = pl.BlockSpec(memory_space=pl.ANY)


def _dot(a, b, dims):
    return lax.dot_general(a, b, dims, preferred_element_type=F32)


def _call(body, **kw):
    return pl.pallas_call(body, **kw)


def _params(**kw):
    return pltpu.CompilerParams(vmem_limit_bytes=VMEM_LIMIT, **kw)


def _seq(n):
    return _params(dimension_semantics=("arbitrary",) * n)


def _peer(k):
    x, y, c = lax.axis_index("x"), lax.axis_index("y"), lax.axis_index("c")
    return (1 - x if k & 4 else x, 1 - y if k & 2 else y, 1 - c if k & 1 else c)


def _flat(pos):
    return 4 * pos[0] + 2 * pos[1] + pos[2]


def _chip(pos):
    return 2 * pos[0] + pos[1]


def _row_tile(rows, cap):
    return max(t for t in range(16, min(rows, cap) + 1, 16) if rows % t == 0)


def _pieces(rows, n):
    tiles = rows // 16
    cuts = [16 * (tiles * q // n) for q in range(n + 1)]
    return [(a, b - a) for a, b in zip(cuts[:-1], cuts[1:])]


def _load_weights(parts, sems):
    copies = []
    for buf, dst in parts:
        rows = buf.shape[1]
        for p in range(NDEV):
            cp = pltpu.make_async_copy(buf.at[p], dst.at[pl.ds(p * rows, rows), :], sems.at[len(copies)])
            cp.start()
            copies.append(cp)
    return copies


def _sigmoid(a):
    return 1.0 / (1.0 + jnp.exp(-a))


def _remote(src, dst, send_sem, recv_sem, to):
    return pltpu.make_async_remote_copy(
        src_ref=src, dst_ref=dst, send_sem=send_sem, recv_sem=recv_sem, device_id=to, device_id_type=pl.DeviceIdType.MESH
    )


class _Gather:
    X, Y, FAR = 4, 2, 6
    COPIES = 8

    def __init__(self, shards):
        n = len(shards)
        self.operands = list(shards)
        self.out_shape = [jax.ShapeDtypeStruct((NDEV,) + a.shape, a.dtype) for a in shards]
        self.sems = [
            pltpu.SemaphoreType.DMA((self.COPIES * n,)), pltpu.SemaphoreType.DMA((self.COPIES * n,)),
            pltpu.SemaphoreType.DMA((n,)),
        ]
        self.stages = [self.begin, self.relay, self.relay_far, self.end]

    def _copy(self, t, k, block, to, ins, outs, sems, own=False, half=None):
        rows = outs[t].shape[1]
        part = pl.ds(0, rows) if half is None else pl.ds(half * (rows // 2), rows // 2)
        dst = outs[t].at[_flat(block), part, :]
        at = self.COPIES * t + k
        return _remote(ins[t] if own else dst, dst, sems[0].at[at], sems[1].at[at], to)

    def _local(self, t, ins, outs, sems):
        return pltpu.make_async_copy(ins[t], outs[t].at[_flat(_peer(0))], sems[2].at[t])

    def begin(self, ins, outs, sems):
        me = _peer(0)
        for t in range(len(ins)):
            self._local(t, ins, outs, sems).start()
            for k, code in enumerate((1, self.X, self.Y)):
                self._copy(t, k, me, _peer(code), ins, outs, sems, own=True).start()

    def relay(self, ins, outs, sems):
        me, sibling = _peer(0), _peer(1)
        for t in range(len(ins)):
            self._copy(t, 1, _peer(self.X), me, ins, outs, sems).wait_recv()
            self._copy(t, 3, _peer(self.X), _peer(self.Y), ins, outs, sems, half=0).start()
            self._copy(t, 5, _peer(self.X), sibling, ins, outs, sems).start()
            self._copy(t, 2, _peer(self.Y), me, ins, outs, sems).wait_recv()
            self._copy(t, 4, _peer(self.Y), _peer(self.X), ins, outs, sems, half=1).start()
            self._copy(t, 6, _peer(self.Y), sibling, ins, outs, sems).start()

    def relay_far(self, ins, outs, sems):
        me, sibling = _peer(0), _peer(1)
        for t in range(len(ins)):
            self._copy(t, 3, _peer(self.FAR), me, ins, outs, sems, half=0).wait_recv()
            self._copy(t, 4, _peer(self.FAR), me, ins, outs, sems, half=1).wait_recv()
            self._copy(t, 7, _peer(self.FAR), sibling, ins, outs, sems).start()

    def end(self, ins, outs, sems):
        me = _peer(0)
        for t in range(len(ins)):
            self._copy(t, 0, _peer(1), me, ins, outs, sems).wait_recv()
            for k, code in ((5, self.X), (6, self.Y), (7, self.FAR)):
                self._copy(t, k, _peer(code ^ 1), me, ins, outs, sems).wait_recv()
            for k in range(self.COPIES):
                self._copy(t, k, me, me, ins, outs, sems, half=0 if k == 3 else 1 if k == 4 else None).wait_send()
            self._local(t, ins, outs, sems).wait()


class _GatherDirect:
    def __init__(self, arrays):
        n = len(arrays)
        self.operands = list(arrays)
        self.out_shape = [jax.ShapeDtypeStruct((NDEV,) + a.shape, a.dtype) for a in arrays]
        self.sems = [pltpu.SemaphoreType.DMA((7 * n,)), pltpu.SemaphoreType.DMA((7 * n,)), pltpu.SemaphoreType.DMA((n,))]
        self.stages = [self.begin, self.end]

    def begin(self, ins, outs, sems):
        mine = _flat(_peer(0))
        for t in range(len(ins)):
            pltpu.make_async_copy(ins[t], outs[t].at[mine], sems[2].at[t]).start()
            for k in range(1, NDEV):
                _remote(ins[t], outs[t].at[mine], sems[0].at[7 * t + k - 1], sems[1].at[7 * t + k - 1], _peer(k)).start()

    def end(self, ins, outs, sems):
        mine = _flat(_peer(0))
        for t in range(len(ins)):
            for k in range(1, NDEV):
                cp = _remote(ins[t], outs[t].at[_flat(_peer(k))], sems[0].at[7 * t + k - 1], sems[1].at[7 * t + k - 1], _peer(k))
                cp.wait_recv()
                cp.wait_send()
            pltpu.make_async_copy(ins[t], outs[t].at[mine], sems[2].at[t]).wait()


class _ChipScatter:
    pieces = 1

    def __init__(self, sums):
        n = len(sums) * NCHIP * self.pieces
        self.operands = list(sums)
        self.out_shape = [jax.ShapeDtypeStruct(a.shape, a.dtype) for a in sums]
        self.sems = [pltpu.SemaphoreType.DMA((n,)), pltpu.SemaphoreType.DMA((n,))]
        self.stages = [self.begin, self.end]

    def _copies(self, ins, outs, sems, arriving):
        mine = _chip(_peer(0))
        copies = []
        for t in range(len(ins)):
            rows = ins[t].shape[1] // self.pieces
            for k in (0, 4, 2, 6):
                other = _chip(_peer(k))
                for q in range(self.pieces):
                    part = pl.ds(q * rows, rows)
                    at = len(copies)
                    if k == 0:
                        cp = pltpu.make_async_copy(ins[t].at[mine, part, :], outs[t].at[mine, part, :], sems[0].at[at])
                    else:
                        landing = outs[t].at[other if arriving else mine, part, :]
                        cp = _remote(ins[t].at[other, part, :], landing, sems[0].at[at], sems[1].at[at], _peer(k))
                    copies.append(cp)
        return copies

    def begin(self, ins, outs, sems):
        for cp in self._copies(ins, outs, sems, False):
            cp.start()

    def end(self, ins, outs, sems):
        for at, cp in enumerate(self._copies(ins, outs, sems, True)):
            if at % (NCHIP * self.pieces) < self.pieces:
                cp.wait()
            else:
                cp.wait_recv()
                cp.wait_send()


class _CastRows:
    def __init__(self, arrays):
        n = len(arrays)
        self.operands = list(arrays)
        self.out_shape = [jax.ShapeDtypeStruct(a.shape, BF16) for a in arrays]
        self.sems = [pltpu.SemaphoreType.DMA((n,)), pltpu.SemaphoreType.DMA((n,))]
        self.sems += [pltpu.VMEM(a.shape, F32) for a in arrays] + [pltpu.VMEM(a.shape, BF16) for a in arrays]
        self.stages = [self.begin, self.convert, self.end]

    def _moves(self, t, ins, outs, scratch):
        n = len(ins)
        load = pltpu.make_async_copy(ins[t], scratch[2 + t], scratch[0].at[t])
        store = pltpu.make_async_copy(scratch[2 + n + t], outs[t], scratch[1].at[t])
        return load, store

    def begin(self, ins, outs, scratch):
        for t in range(len(ins)):
            self._moves(t, ins, outs, scratch)[0].start()

    def convert(self, ins, outs, scratch):
        n = len(ins)
        for t in range(n):
            load, store = self._moves(t, ins, outs, scratch)
            load.wait()
            scratch[2 + n + t][...] = scratch[2 + t][...].astype(BF16)
            store.start()

    def end(self, ins, outs, scratch):
        for t in range(len(ins)):
            self._moves(t, ins, outs, scratch)[1].wait()


def _split_refs(refs, counts):
    out, at = [], 0
    for n in counts:
        out.append(refs[at:at + n])
        at += n
    return out


def _comm_call(carries, name, peers=None, collective_id=None):
    nin = [len(c.operands) for c in carries]
    nout = [len(c.out_shape) for c in carries]
    nsem = [len(c.sems) for c in carries]

    def body(*refs):
        if peers is not None:
            barrier = pltpu.get_barrier_semaphore()
            for k in peers:
                pl.semaphore_signal(barrier, inc=1, device_id=_peer(k), device_id_type=pl.DeviceIdType.MESH)
            pl.semaphore_wait(barrier, len(peers))
        ins, outs, sems = _split_refs(refs, (sum(nin), sum(nout), sum(nsem)))
        parts = list(zip(carries, _split_refs(ins, nin), _split_refs(outs, nout), _split_refs(sems, nsem)))
        for depth in range(max(len(c.stages) for c in carries)):
            for c, i, o, s in parts:
                if depth < len(c.stages) - 1:
                    c.stages[depth](i, o, s)
        for c, i, o, s in parts:
            c.stages[-1](i, o, s)

    res = _call(
        body,
        name=name,
        out_shape=[sh for c in carries for sh in c.out_shape],
        in_specs=[ANY] * sum(nin),
        out_specs=[ANY] * sum(nout),
        scratch_shapes=[sm for c in carries for sm in c.sems],
        compiler_params=_params(has_side_effects=True, collective_id=collective_id),
    )(*[a for c in carries for a in c.operands])
    return _split_refs(list(res), nout)


def _grid_call(body, carries, *, name, grid, in_specs, out_specs, out_shape, scratch_shapes, args):
    ni, no, ns = len(in_specs), len(out_specs), len(scratch_shapes)
    nin = [len(c.operands) for c in carries]
    nout = [len(c.out_shape) for c in carries]
    nsem = [len(c.sems) for c in carries]
    steps = int(np.prod(grid))

    def when_of(stage, count):
        first, last = (5 * steps) // 8 - 1, steps - 2
        return max(0, last if count <= 3 else first + (last - first) * (stage - 1) // (count - 3))

    def wrapped(*refs):
        ins, cins, outs, couts, scr, csems = _split_refs(refs, (ni, sum(nin), no, sum(nout), ns, sum(nsem)))
        if not carries:
            return body(*ins, *outs, *scr)
        parts = list(zip(carries, _split_refs(cins, nin), _split_refs(couts, nout), _split_refs(csems, nsem)))
        step = pl.program_id(0)
        for axis in range(1, len(grid)):
            step = step * grid[axis] + pl.program_id(axis)

        @pl.when(step == 0)
        def _():
            for c, i, o, s in parts:
                c.stages[0](i, o, s)

        body(*ins, *outs, *scr)

        for c, i, o, s in parts:
            for stage in range(1, len(c.stages) - 1):
                pl.when(step == when_of(stage, len(c.stages)))(functools.partial(c.stages[stage], i, o, s))

        @pl.when(step == steps - 1)
        def _():
            for c, i, o, s in parts:
                c.stages[-1](i, o, s)

    res = _call(
        wrapped,
        name=name,
        grid=tuple(grid),
        in_specs=list(in_specs) + [ANY] * sum(nin),
        out_specs=list(out_specs) + [ANY] * sum(nout),
        out_shape=list(out_shape) + [sh for c in carries for sh in c.out_shape],
        scratch_shapes=list(scratch_shapes) + [sm for c in carries for sm in c.sems],
        compiler_params=_seq(len(grid)),
    )(*args, *[a for c in carries for a in c.operands])
    res = list(res)
    return res[:no], _split_refs(res[no:], nout)


def _chunks(width):
    return [(at, min(FFN_CHUNK, width - at)) for at in range(0, width, FFN_CHUNK)]


def _start_chunk_loads(bufs, dsts, sems, chunks):
    copies = [[None] * len(chunks) for _ in bufs]
    for c, (at, width) in enumerate(chunks):
        for m, (buf, dst) in enumerate(zip(bufs, dsts)):
            cp = pltpu.make_async_copy(
                buf.at[pl.ds(at, width), :], dst.at[pl.ds(at, width), :], sems.at[m * len(chunks) + c]
            )
            cp.start()
            copies[m][c] = cp
    return copies


def _ffn_fwd(x, gain, weights, ffn, head=None, carries=()):
    s, d = x.shape
    tm = min(512, s)
    chunks = _chunks(ffn)

    def body(*refs):
        if head is None:
            x_ref, g_ref, b1, b3, b2, h_ref, a_ref, b_ref, hm_ref, w1s, w3s, w2s, sems = refs
        else:
            x_ref, g_ref, b1, b3, b2, gf_ref, t_ref, h_ref, a_ref, b_ref, hm_ref, dgf_ref, loss_ref, w1s, w3s, w2s, sems = refs

        def one_tile(loads):
            xv = x_ref[...]
            r = lax.rsqrt(jnp.mean(xv * xv, axis=-1, keepdims=True) + EPS)
            n = (xv * r * g_ref[...]).astype(BF16)
            acc = jnp.zeros((tm, d), F32)
            for c, (at, width) in enumerate(chunks):
                cols = slice(at, at + width)
                if loads is not None:
                    for m in range(3):
                        loads[m][c].wait()
                a = _dot(n, w1s[cols, :], NT)
                b = _dot(n, w3s[cols, :], NT)
                a_ref[:, cols] = a.astype(BF16)
                b_ref[:, cols] = b.astype(BF16)
                hm = (a * _sigmoid(a) * b).astype(BF16)
                hm_ref[:, cols] = hm
                acc = acc + _dot(hm, w2s[cols, :], NN)
            h = xv + 0.5 * acc
            if head is None:
                h_ref[...] = h
            else:
                rf = lax.rsqrt(jnp.mean(h * h, axis=-1, keepdims=True) + EPS)
                nh = h * rf
                gf = gf_ref[...]
                err = nh * gf - t_ref[...]
                loss_ref[...] += jnp.sum(err * err, axis=0, keepdims=True) * (0.5 / d)
                dy = err * (1.0 / d)
                dgf_ref[...] += jnp.sum(dy * nh, axis=0, keepdims=True)
                dn = dy * gf
                h_ref[...] = rf * (dn - nh * jnp.mean(dn * nh, axis=-1, keepdims=True))

        if head is None:
            @pl.when(pl.program_id(0) == 0)
            def _():
                one_tile(_start_chunk_loads((b1, b3, b2), (w1s, w3s, w2s), sems, chunks))

            @pl.when(pl.program_id(0) > 0)
            def _():
                one_tile(None)
        else:
            @pl.when(pl.program_id(0) == 0)
            def _():
                dgf_ref[...] = jnp.zeros_like(dgf_ref)
                loss_ref[...] = jnp.zeros_like(loss_ref)
                for loads in _start_chunk_loads((b1, b3, b2), (w1s, w3s, w2s), sems, chunks):
                    for cp in loads:
                        cp.wait()

            one_tile(None)

    tile = pl.BlockSpec((tm, d), lambda i: (i, 0))
    row = pl.BlockSpec((1, d), lambda i: (0, 0))
    wide = pl.BlockSpec((tm, ffn), lambda i: (i, 0))
    in_specs = [tile, row, ANY, ANY, ANY]
    out_shape = [jax.ShapeDtypeStruct((s, d), F32)] + [jax.ShapeDtypeStruct((s, ffn), BF16)] * 3
    out_specs = [tile, wide, wide, wide]
    args = [x, gain] + [w.reshape(ffn, d) for w in weights]
    if head is not None:
        in_specs += [row, tile]
        args += list(head)
        out_shape += [jax.ShapeDtypeStruct((1, d), F32)] * 2
        out_specs += [row, row]
    return _grid_call(
        body,
        carries,
        name="ffn_fwd_loss" if head is not None else "ffn_fwd",
        grid=(s // tm,),
        in_specs=in_specs,
        out_specs=out_specs,
        out_shape=out_shape,
        scratch_shapes=[pltpu.VMEM((ffn, d), BF16)] * 3 + [pltpu.SemaphoreType.DMA((3 * len(chunks),))],
        args=args,
    )


def _ffn_bwd(dh, x, a, b, gain, weights, ffn, name, carries=()):
    s, d = x.shape
    tm = min(512, s)
    halves = 2
    fh = ffn // halves

    def body(dh_ref, x_ref, a_ref, b_ref, g_ref, b1, b3, b2, dx_ref, da_ref, db_ref, n_ref, dg_ref, w1s, w3s, w2s, sems):
        i, j = pl.program_id(0), pl.program_id(1)

        @pl.when((i == 0) & (j == 0))
        def _():
            for cp in _load_weights(((b1, w1s), (b3, w3s), (b2, w2s)), sems):
                cp.wait()
            dg_ref[...] = jnp.zeros_like(dg_ref)

        @pl.when(j == 0)
        def _():
            dx_ref[...] = jnp.zeros_like(dx_ref)

        dob = (0.5 * dh_ref[...]).astype(BF16)
        chunks = _chunks(fh)

        def dhm_of(k):
            at, width = chunks[k]
            return _dot(dob, w2s[pl.ds(pl.multiple_of(j * fh + at, GROUP), width), :], NT)

        ahead = dhm_of(0)
        for k, (at, width) in enumerate(chunks):
            cols = slice(at, at + width)
            dhm = ahead
            if k + 1 < len(chunks):
                ahead = dhm_of(k + 1)
            for top in range(0, tm, ROW_BAND):
                band = slice(top, top + ROW_BAND)
                av = a_ref[band, cols].astype(F32)
                bv = b_ref[band, cols].astype(F32)
                sg = _sigmoid(av)
                dv = dhm[band]
                da_ref[band, cols] = (dv * bv * (sg * (1.0 + av * (1.0 - sg)))).astype(BF16)
                db_ref[band, cols] = (dv * (av * sg)).astype(BF16)
        half = pl.ds(pl.multiple_of(j * fh, GROUP), fh)
        dx_ref[...] += _dot(da_ref[...], w1s[half, :], NN) + _dot(db_ref[...], w3s[half, :], NN)

        @pl.when(j == halves - 1)
        def _():
            xv = x_ref[...]
            g = g_ref[...]
            r = lax.rsqrt(jnp.mean(xv * xv, axis=-1, keepdims=True) + EPS)
            nh = xv * r
            n_ref[...] = (nh * g).astype(BF16)
            total = dx_ref[...]
            dg_ref[...] += jnp.sum(total * nh, axis=0, keepdims=True)
            dnh = total * g
            dx_ref[...] = dh_ref[...] + r * (dnh - nh * jnp.mean(dnh * nh, axis=-1, keepdims=True))

    tile = pl.BlockSpec((tm, d), lambda i, j: (i, 0))
    row = pl.BlockSpec((1, d), lambda i, j: (0, 0))
    wide = pl.BlockSpec((tm, fh), lambda i, j: (i, j))
    return _grid_call(
        body,
        carries,
        name=name,
        grid=(s // tm, halves),
        in_specs=[tile, tile, wide, wide, row, ANY, ANY, ANY],
        out_specs=[tile, wide, wide, tile, row],
        out_shape=[
            jax.ShapeDtypeStruct((s, d), F32),
            jax.ShapeDtypeStruct((s, ffn), BF16),
            jax.ShapeDtypeStruct((s, ffn), BF16),
            jax.ShapeDtypeStruct((s, d), BF16),
            jax.ShapeDtypeStruct((1, d), F32),
        ],
        scratch_shapes=[pltpu.VMEM((ffn, d), BF16)] * 3 + [pltpu.SemaphoreType.DMA((3 * NDEV,))],
        args=[dh, x, a, b, gain] + list(weights),
    )


SWAP_PIECES = 1


def _wgrad(lhs, rhs, scale, name, carries=()):
    s, m = lhs.shape
    n = rhs.shape[1]
    rs = m // NDEV
    tk = min(1024, s)
    steps = s // tk
    pieces = [(j, at, size) for j in range(2) for at, size in _pieces(rs, SWAP_PIECES)]

    def body(l_ref, r_ref, o_ref, acc, mine, theirs, send_sems, recv_sems):
        h, k = pl.program_id(0), pl.program_id(1)

        @pl.when(k == 0)
        def _():
            acc[...] = _dot(l_ref[...], r_ref[...].astype(BF16), TN)

        @pl.when(k > 0)
        def _():
            acc[...] += _dot(l_ref[...], r_ref[...].astype(BF16), TN)

        def exchange(half):
            c = lax.axis_index("c")
            return [
                _remote(mine.at[half, 1 - c, j, pl.ds(at, size), :], theirs.at[half, j, pl.ds(at, size), :],
                        send_sems.at[half * len(pieces) + q], recv_sems.at[half * len(pieces) + q], _peer(1))
                for q, (j, at, size) in enumerate(pieces)
            ]

        def settle(half):
            for cp in exchange(half):
                cp.wait_recv()
            both = mine[half, lax.axis_index("c")].astype(F32) + theirs[half].astype(F32)
            o_ref[2 * half:2 * half + 2] = both.astype(BF16)
            for cp in exchange(half):
                cp.wait_send()

        for half in range(2):
            @pl.when((h == half) & (k == steps - 1))
            def _():
                for p in range(NCHIP):
                    mine[half, p % 2, p // 2] = (acc[p * rs:(p + 1) * rs, :] * scale).astype(BF16)
                for cp in exchange(half):
                    cp.start()
                if half == 1:
                    settle(0)
                    settle(1)

    (out,), carried = _grid_call(
        body,
        carries,
        name=name,
        grid=(2, steps),
        in_specs=[pl.BlockSpec((tk, m // 2), lambda h, k: (k, h)), pl.BlockSpec((tk, n), lambda h, k: (k, 0))],
        out_specs=[pl.BlockSpec((NCHIP, rs, n), lambda h, k: (0, 0, 0))],
        out_shape=[jax.ShapeDtypeStruct((NCHIP, rs, n), BF16)],
        scratch_shapes=[
            pltpu.VMEM((m // 2, n), F32), pltpu.VMEM((2, 2, 2, rs, n), BF16), pltpu.VMEM((2, 2, rs, n), BF16),
            pltpu.SemaphoreType.DMA((2 * len(pieces),)), pltpu.SemaphoreType.DMA((2 * len(pieces),)),
        ],
        args=[lhs, rhs],
    )
    return out, carried


def _mix_constants(s):
    c = GROUP
    lg = np.log1p(-np.exp2(-5.0 - np.arange(RET_HEADS, dtype=np.float32))).astype(np.float32)
    pos = np.arange(c, dtype=np.float32)
    rel = pos[:, None] - pos[None, :]
    decay = np.where(rel[None] >= 0, np.exp(lg[:, None, None] * np.maximum(rel, 0.0)[None]), 0.0).astype(np.float32)
    ktail = np.exp(lg[:, None] * (c - 1 - pos)[None, :]).astype(np.float32)
    qhead = np.exp(lg[:, None] * (pos + 1.0)[None, :]).astype(np.float32)
    chunk_decay = [float(v) for v in np.exp(lg * np.float32(c)).astype(np.float32)]
    ones = np.ones((1, 1, c), np.float32)
    inv_freq = (1.0 / (np.float32(ROPE_BASE) ** (np.arange(0, c, 2, dtype=np.float32) / np.float32(c)))).astype(np.float32)
    ang = (np.arange(s, dtype=np.float32)[:, None] * inv_freq[None, :]).astype(np.float32)
    cos, sin = np.cos(ang).astype(np.float32), np.sin(ang).astype(np.float32)
    return dict(
        decay=jnp.asarray(decay),
        ktail=jnp.asarray(ktail[:, :, None] * ones),
        qhead=jnp.asarray(qhead[:, :, None] * ones),
        chunk_decay=chunk_decay,
        cos=jnp.asarray(np.concatenate([cos, cos], axis=-1)),
        sin=jnp.asarray(np.concatenate([-sin, sin], axis=-1)),
    )


def _rope(t, cos, sin):
    return t * cos + pltpu.roll(t, GROUP // 2, axis=1) * sin


def _rope_bwd(dt, cos, sin):
    return dt * cos + pltpu.roll(dt * sin, GROUP // 2, axis=1)


def _window_sums(ext, w, forward):
    rows = ext.shape[0]
    acc, k = ext, 1
    while k < w:
        acc = acc + pltpu.roll(acc, k if forward else rows - k, axis=0)
        k *= 2
    return acc


def _pool_counts(tile, tm, w):
    t = lax.broadcasted_iota(jnp.int32, (tm, 1), 0) + tile * tm
    return jnp.minimum(t + 1, w).astype(F32)


def _mix_fwd(h1, gain, weights, pool_w, pool_scale, ret_gain, consts, carries=()):
    s, d = h1.shape
    pwid = N_POOL_GROUPS * GROUP
    rwid = RET_HEADS * GROUP
    inw = pwid + 4 * rwid
    tm = min(256, s)
    nck = tm // GROUP
    cd = consts["chunk_decay"]

    def body(h_ref, g_ref, bin_, bout, pw_ref, ps_ref, rg_ref, cos_ref, sin_ref, dec_ref, kt_ref, qh_ref,
             h2_ref, proj_ref, o_ref, rs_ref, wins, wouts, state, carry, mbuf, sems):
        i = pl.program_id(0)

        @pl.when(i == 0)
        def _():
            for cp in _load_weights(((bin_, wins), (bout, wouts)), sems):
                cp.wait()
            state[...] = jnp.zeros_like(state)
            carry[...] = jnp.zeros_like(carry)

        hv = h_ref[...]
        r = lax.rsqrt(jnp.mean(hv * hv, axis=-1, keepdims=True) + EPS)
        u = (hv * r * g_ref[...]).astype(BF16)
        proj_ref[...] = _dot(u, wins[...], NT)

        ext = jnp.concatenate([carry[...], proj_ref[:, 0:pwid]], axis=0)
        carry[...] = proj_ref[tm - MAX_WINDOW:tm, 0:pwid]
        for gi, w in enumerate(POOL_WINDOWS):
            cols = slice(gi * GROUP, (gi + 1) * GROUP)
            xg = ext[:, cols]
            ws = _window_sums(xg, w, True)[MAX_WINDOW:, :]
            pooled = ws / _pool_counts(i, tm, w) - xg[MAX_WINDOW:, :]
            z = _dot(pooled.astype(BF16), pw_ref[gi].astype(BF16), NN)
            mbuf[:, cols] = (z * ps_ref[:, cols]).astype(BF16)

        cos, sin = cos_ref[...], sin_ref[...]
        for h in range(RET_HEADS):
            cq = slice(pwid + h * GROUP, pwid + (h + 1) * GROUP)
            ck = slice(pwid + rwid + h * GROUP, pwid + rwid + (h + 1) * GROUP)
            cv = slice(pwid + 2 * rwid + h * GROUP, pwid + 2 * rwid + (h + 1) * GROUP)
            cg = slice(pwid + 3 * rwid + h * GROUP, pwid + 3 * rwid + (h + 1) * GROUP)
            ch = slice(h * GROUP, (h + 1) * GROUP)
            qr = _rope(proj_ref[:, cq], cos, sin)
            kr = _rope(proj_ref[:, ck], cos, sin) * (GROUP ** -0.5)
            vb = proj_ref[:, cv].astype(BF16)
            for n in range(nck):
                rows = slice(n * GROUP, (n + 1) * GROUP)
                qc, kc, vc = qr[rows], kr[rows], vb[rows]
                rb = state[h]
                rs_ref[n, h] = rb
                p = (_dot(qc.astype(BF16), kc.astype(BF16), NT) * dec_ref[h]).astype(BF16)
                o = _dot(p, vc, NN) + _dot((qc * qh_ref[h]).astype(BF16), rb.astype(BF16), NN)
                state[h] = cd[h] * rb + _dot((kc * kt_ref[h]).astype(BF16), vc, TN)
                o_ref[rows, ch] = o
                on = o * lax.rsqrt(jnp.mean(o * o, axis=-1, keepdims=True) + EPS)
                gv = proj_ref[rows, cg]
                mbuf[rows, pwid + h * GROUP:pwid + (h + 1) * GROUP] = (
                    gv * _sigmoid(gv) * (on * rg_ref[:, ch])
                ).astype(BF16)
        h2_ref[...] = hv + _dot(mbuf[...], wouts[...], NN)

    tile = pl.BlockSpec((tm, d), lambda i: (i, 0))
    full = lambda shape: pl.BlockSpec(shape, lambda i: (0,) * len(shape))
    return _grid_call(
        body,
        carries,
        name="mix_fwd",
        grid=(s // tm,),
        in_specs=[
            tile, full((1, d)), ANY, ANY,
            full((N_POOL_GROUPS, GROUP, GROUP)), full((1, pwid)), full((1, rwid)),
            pl.BlockSpec((tm, GROUP), lambda i: (i, 0)), pl.BlockSpec((tm, GROUP), lambda i: (i, 0)),
            full((RET_HEADS, GROUP, GROUP)), full((RET_HEADS, GROUP, GROUP)), full((RET_HEADS, GROUP, GROUP)),
        ],
        out_specs=[
            tile,
            pl.BlockSpec((tm, inw), lambda i: (i, 0)),
            pl.BlockSpec((tm, rwid), lambda i: (i, 0)),
            pl.BlockSpec((nck, RET_HEADS, GROUP, GROUP), lambda i: (i, 0, 0, 0)),
        ],
        out_shape=[
            jax.ShapeDtypeStruct((s, d), F32),
            jax.ShapeDtypeStruct((s, inw), F32),
            jax.ShapeDtypeStruct((s, rwid), F32),
            jax.ShapeDtypeStruct((s // GROUP, RET_HEADS, GROUP, GROUP), F32),
        ],
        scratch_shapes=[
            pltpu.VMEM((inw, d), BF16), pltpu.VMEM((d, d), BF16),
            pltpu.VMEM((RET_HEADS, GROUP, GROUP), F32), pltpu.VMEM((MAX_WINDOW, pwid), F32),
            pltpu.VMEM((tm, d), BF16), pltpu.SemaphoreType.DMA((2 * NDEV,)),
        ],
        args=[h1, gain, weights[0], weights[1], pool_w, pool_scale, ret_gain,
              consts["cos"], consts["sin"], consts["decay"], consts["ktail"], consts["qhead"]],
    )


def _mix_bwd(dh2, h1, proj, o_saved, rsave, gain, weights, pool_w, pool_scale, ret_gain, consts, carries=()):
    s, d = h1.shape
    pwid = N_POOL_GROUPS * GROUP
    rwid = RET_HEADS * GROUP
    inw = pwid + 4 * rwid
    tm = min(256, s)
    nck = tm // GROUP
    nt = s // tm
    cd = consts["chunk_decay"]
    halo_per_tile = tm // MAX_WINDOW

    def body(dh2_ref, h_ref, proj_ref, halo_ref, o_ref, rs_ref, g_ref, bin_, bout, pw_ref, ps_ref, rg_ref,
             cos_ref, sin_ref, dec_ref, kt_ref, qh_ref,
             dh1_ref, dproj_ref, u_ref, m_ref, dpw_ref, dps_ref, drg_ref, dg_ref,
             wins, wouts, dstate, carry, dm, dpj, sems):
        i = pl.program_id(0)
        tile = nt - 1 - i

        @pl.when(i == 0)
        def _():
            for cp in _load_weights(((bin_, wins), (bout, wouts)), sems):
                cp.wait()
            dstate[...] = jnp.zeros_like(dstate)
            carry[...] = jnp.zeros_like(carry)
            for ref in (dpw_ref, dps_ref, drg_ref, dg_ref):
                ref[...] = jnp.zeros_like(ref)

        dh2v = dh2_ref[...]
        dm[...] = _dot(dh2v.astype(BF16), wouts[...], NT)
        hv = h_ref[...]
        g = g_ref[...]
        r = lax.rsqrt(jnp.mean(hv * hv, axis=-1, keepdims=True) + EPS)
        uh = hv * r
        u_ref[...] = (uh * g).astype(BF16)

        halo = jnp.where(tile == 0, 0.0, halo_ref[...])
        ext = jnp.concatenate([halo, proj_ref[:, 0:pwid]], axis=0)
        next_dpn = carry[...]
        for gi, w in enumerate(POOL_WINDOWS):
            cols = slice(gi * GROUP, (gi + 1) * GROUP)
            xg = ext[:, cols]
            cnt = _pool_counts(tile, tm, w)
            pooled = (_window_sums(xg, w, True)[MAX_WINDOW:, :] / cnt - xg[MAX_WINDOW:, :]).astype(BF16)
            pwb = pw_ref[gi].astype(BF16)
            z = _dot(pooled, pwb, NN)
            scale = ps_ref[:, cols]
            m_ref[:, cols] = (z * scale).astype(BF16)
            da = dm[:, cols]
            dps_ref[:, cols] += jnp.sum(da * z, axis=0, keepdims=True)
            dz = (da * scale).astype(BF16)
            dpw_ref[gi] += _dot(pooled, dz, TN)
            dpl = _dot(dz, pwb, NT)
            dpn = dpl / cnt
            ext2 = jnp.concatenate([dpn, next_dpn[:, cols]], axis=0)
            dpj[:, cols] = (_window_sums(ext2, w, False)[0:tm, :] - dpl).astype(BF16)
            carry[:, cols] = dpn[0:MAX_WINDOW, :]

        cos, sin = cos_ref[...], sin_ref[...]
        for h in range(RET_HEADS):
            cq = slice(pwid + h * GROUP, pwid + (h + 1) * GROUP)
            ck = slice(pwid + rwid + h * GROUP, pwid + rwid + (h + 1) * GROUP)
            cv = slice(pwid + 2 * rwid + h * GROUP, pwid + 2 * rwid + (h + 1) * GROUP)
            cg = slice(pwid + 3 * rwid + h * GROUP, pwid + 3 * rwid + (h + 1) * GROUP)
            ch = slice(h * GROUP, (h + 1) * GROUP)
            qr = _rope(proj_ref[:, cq], cos, sin)
            kr = _rope(proj_ref[:, ck], cos, sin) * (GROUP ** -0.5)
            vb = proj_ref[:, cv].astype(BF16)
            gv = proj_ref[:, cg]
            ov = o_ref[:, ch]
            ro = lax.rsqrt(jnp.mean(ov * ov, axis=-1, keepdims=True) + EPS)
            on = ov * ro
            rg = rg_ref[:, ch]
            db = dm[:, pwid + h * GROUP:pwid + (h + 1) * GROUP]
            sg = _sigmoid(gv)
            sl = gv * sg
            m_ref[:, pwid + h * GROUP:pwid + (h + 1) * GROUP] = (sl * (on * rg)).astype(BF16)
            dpj[:, cg] = (db * (on * rg) * (sg * (1.0 + gv * (1.0 - sg)))).astype(BF16)
            drg_ref[:, ch] += jnp.sum(db * sl * on, axis=0, keepdims=True)
            don = db * sl * rg
            do = (ro * (don - on * jnp.mean(don * on, axis=-1, keepdims=True))).astype(BF16)
            for n in reversed(range(nck)):
                rows = slice(n * GROUP, (n + 1) * GROUP)
                qc, kc, vc, dob = qr[rows], kr[rows], vb[rows], do[rows]
                qcb, kcb = qc.astype(BF16), kc.astype(BF16)
                qh = (qc * qh_ref[h]).astype(BF16)
                kt = (kc * kt_ref[h]).astype(BF16)
                rn = rs_ref[n, h].astype(BF16)
                dnext = dstate[h]
                dnb = dnext.astype(BF16)
                dec = dec_ref[h]
                p = (_dot(qcb, kcb, NT) * dec).astype(BF16)
                ds = (_dot(dob, vc, NT) * dec).astype(BF16)
                dv = _dot(p, dob, TN) + _dot(kt, dnb, NN)
                dq = _dot(ds, kcb, NN) + _dot(dob, rn, NT) * qh_ref[h]
                dk = _dot(ds, qcb, TN) + _dot(vc, dnb, NT) * kt_ref[h]
                dstate[h] = cd[h] * dnext + _dot(qh, dob, TN)
                dpj[rows, cq] = _rope_bwd(dq, cos[rows], sin[rows]).astype(BF16)
                dpj[rows, ck] = _rope_bwd(dk * (GROUP ** -0.5), cos[rows], sin[rows]).astype(BF16)
                dpj[rows, cv] = dv.astype(BF16)

        dproj_ref[...] = dpj[...]
        du = _dot(dpj[...], wins[...], NN)
        dg_ref[...] += jnp.sum(du * uh, axis=0, keepdims=True)
        dn = du * g
        dh1_ref[...] = dh2v + r * (dn - uh * jnp.mean(dn * uh, axis=-1, keepdims=True))

    rev = lambda i: (nt - 1 - i, 0)
    tile = pl.BlockSpec((tm, d), rev)
    full = lambda shape: pl.BlockSpec(shape, lambda i: (0,) * len(shape))
    return _grid_call(
        body,
        carries,
        name="mix_bwd",
        grid=(nt,),
        in_specs=[
            tile, tile,
            pl.BlockSpec((tm, inw), rev),
            pl.BlockSpec((MAX_WINDOW, pwid), lambda i: (jnp.maximum((nt - 1 - i) * halo_per_tile - 1, 0), 0)),
            pl.BlockSpec((tm, rwid), rev),
            pl.BlockSpec((nck, RET_HEADS, GROUP, GROUP), lambda i: (nt - 1 - i, 0, 0, 0)),
            full((1, d)), ANY, ANY,
            full((N_POOL_GROUPS, GROUP, GROUP)), full((1, pwid)), full((1, rwid)),
            pl.BlockSpec((tm, GROUP), rev), pl.BlockSpec((tm, GROUP), rev),
            full((RET_HEADS, GROUP, GROUP)), full((RET_HEADS, GROUP, GROUP)), full((RET_HEADS, GROUP, GROUP)),
        ],
        out_specs=[
            tile, pl.BlockSpec((tm, inw), rev), tile, tile,
            full((N_POOL_GROUPS, GROUP, GROUP)), full((1, pwid)), full((1, rwid)), full((1, d)),
        ],
        out_shape=[
            jax.ShapeDtypeStruct((s, d), F32),
            jax.ShapeDtypeStruct((s, inw), BF16),
            jax.ShapeDtypeStruct((s, d), BF16),
            jax.ShapeDtypeStruct((s, d), BF16),
            jax.ShapeDtypeStruct((N_POOL_GROUPS, GROUP, GROUP), F32),
            jax.ShapeDtypeStruct((1, pwid), F32),
            jax.ShapeDtypeStruct((1, rwid), F32),
            jax.ShapeDtypeStruct((1, d), F32),
        ],
        scratch_shapes=[
            pltpu.VMEM((inw, d), BF16), pltpu.VMEM((d, d), BF16),
            pltpu.VMEM((RET_HEADS, GROUP, GROUP), F32), pltpu.VMEM((MAX_WINDOW, pwid), F32),
            pltpu.VMEM((tm, d), F32), pltpu.VMEM((tm, inw), BF16), pltpu.SemaphoreType.DMA((2 * NDEV,)),
        ],
        args=[dh2, h1, proj, proj, o_saved, rsave, gain, weights[0], weights[1], pool_w, pool_scale, ret_gain,
              consts["cos"], consts["sin"], consts["decay"], consts["ktail"], consts["qhead"]],
    )


def _adam(w, g, m, v):
    m = ADAM_B1 * m + (1.0 - ADAM_B1) * g
    v = ADAM_B2 * v + (1.0 - ADAM_B2) * jnp.square(g)
    m_hat = m / (1.0 - ADAM_B1 ** ADAM_STEP)
    v_hat = v / (1.0 - ADAM_B2 ** ADAM_STEP)
    delta = -ADAM_LR * (m_hat / (jnp.sqrt(v_hat) + ADAM_EPS) + ADAM_WD * w)
    return delta, m, v


def _adamw_big(w, parts, m, v, name):
    rows, d = w.shape
    tr = _row_tile(rows, 176)

    def body(w_ref, p_ref, m_ref, v_ref, g_ref, d_ref, nm_ref, nv_ref):
        g = p_ref[0].astype(F32)
        for q in range(1, NCHIP):
            g = g + p_ref[q].astype(F32)
        g_ref[...] = g
        d_ref[...], nm_ref[...], nv_ref[...] = _adam(w_ref[...], g, m_ref[...], v_ref[...])

    spec = pl.BlockSpec((tr, d), lambda i: (i, 0))
    return _call(
        body,
        name=name,
        grid=(rows // tr,),
        in_specs=[spec, pl.BlockSpec((NCHIP, tr, d), lambda i: (0, i, 0)), spec, spec],
        out_specs=[spec] * 4,
        out_shape=[jax.ShapeDtypeStruct((rows, d), F32)] * 4,
        compiler_params=_seq(1),
    )(w, parts, m, v)


def _adamw_small(stats_all, pw_all, ws, ms, vs, pwid):
    nsmall = len(ws)

    def body(*refs):
        st_ref, pwa_ref = refs[0], refs[1]
        w_refs = refs[2:2 + nsmall]
        m_refs = refs[2 + nsmall:2 + 2 * nsmall]
        v_refs = refs[2 + 2 * nsmall:2 + 3 * nsmall]
        outs = refs[2 + 3 * nsmall:]
        st = st_ref[0]
        pwg = pwa_ref[0]
        for q in range(1, NDEV):
            st = st + st_ref[q]
            pwg = pwg + pwa_ref[q]
        grads = [st[0:1, :], st[1:2, :], st[2:3, :], st[3:4, :], st[4:5, 0:pwid], st[4:5, pwid:2 * pwid], pwg]
        outs[0][...] = jnp.zeros((1, GROUP), F32) + jnp.sum(st[5:6, :])
        for j in range(nsmall):
            delta, nm, nv = _adam(w_refs[j][...], grads[j], m_refs[j][...], v_refs[j][...])
            outs[1 + 4 * j][...] = grads[j]
            outs[2 + 4 * j][...] = delta
            outs[3 + 4 * j][...] = nm
            outs[4 + 4 * j][...] = nv

    out_shape = [jax.ShapeDtypeStruct((1, GROUP), F32)]
    for w in ws:
        out_shape += [jax.ShapeDtypeStruct(w.shape, F32)] * 4
    return _call(body, name="adamw_small", out_shape=out_shape, compiler_params=_params())(
        stats_all, pw_all, *ws, *ms, *vs
    )


def kernel(x, ffn1_norm, ffn1_w1, ffn1_w3, ffn1_w2, mix_norm, w_in, pool_w, pool_scale, ret_norm, w_out, ffn2_norm, ffn2_w1, ffn2_w3, ffn2_w2, final_norm, loss_target, m_ffn1_norm, m_ffn1_w1, m_ffn1_w3, m_ffn1_w2, m_mix_norm, m_w_in, m_pool_w, m_pool_scale, m_ret_norm, m_w_out, m_ffn2_norm, m_ffn2_w1, m_ffn2_w3, m_ffn2_w2, m_final_norm, v_ffn1_norm, v_ffn1_w1, v_ffn1_w3, v_ffn1_w2, v_mix_norm, v_w_in, v_pool_w, v_pool_scale, v_ret_norm, v_w_out, v_ffn2_norm, v_ffn2_w1, v_ffn2_w3, v_ffn2_w2, v_final_norm):
    s, d = x.shape[1], x.shape[2]
    ffn = ffn1_w1.shape[2] * NDEV
    pwid = pool_scale.shape[1]
    xs, tgt = x[0], loss_target[0]
    consts = _mix_constants(s)
    pw3 = pool_w[0]
    fnorm = final_norm.reshape(1, d)

    rows_of = lambda w, transposed: w[0].T if transposed else w[0]
    send_f1 = [rows_of(w, t).astype(BF16) for w, t in ((ffn1_w1, True), (ffn1_w3, True), (ffn1_w2, False))]
    later = [rows_of(w, t) for w, t in ((w_in, True), (w_out, False), (ffn2_w1, True), (ffn2_w3, True), (ffn2_w2, False))]

    sent_later, w_f1 = _comm_call([_CastRows(later), _Gather(send_f1)], "gather_ffn1")
    send_mix, send_f2 = sent_later[:2], sent_later[2:]
    (h1, a1, b1, hm1), (more,) = _ffn_fwd(xs, ffn1_norm, w_f1, ffn, carries=[_Gather(send_mix + send_f2[:1])])
    w_mix = more[:2]
    (h2, proj, o_saved, rsave), (rest,) = _mix_fwd(
        h1, mix_norm, w_mix, pw3, pool_scale, ret_norm, consts, carries=[_Gather(send_f2[1:])]
    )
    w_f2 = more[2:] + rest
    (dh3, a2, b2, hm2, dgf, loss_cols), _ = _ffn_fwd(h2, ffn2_norm, w_f2, ffn, head=(fnorm, tgt))

    (dh2, da2, db2, n2, dg2), _ = _ffn_bwd(dh3, h2, a2, b2, ffn2_norm, w_f2, ffn, "ffn2_bwd")
    sum_f2w1, _ = _wgrad(da2, n2, 1.0, "ffn2_w1_grad")
    sum_f2w3, _ = _wgrad(db2, n2, 1.0, "ffn2_w3_grad")
    sum_f2w2, ((parts_f2w1,),) = _wgrad(hm2, dh3, 0.5, "ffn2_w2_grad", carries=[_ChipScatter([sum_f2w1])])

    (dh1, dproj, u, mm, dpw, dps, drg, dgm), ((parts_f2w3, parts_f2w2),) = _mix_bwd(
        dh2, h1, proj, o_saved, rsave, mix_norm, w_mix, pw3, pool_scale, ret_norm, consts,
        carries=[_ChipScatter([sum_f2w3, sum_f2w2])],
    )
    (dx, da1, db1, n1, dg1), _ = _ffn_bwd(dh1, xs, a1, b1, ffn1_norm, w_f1, ffn, "ffn1_bwd")
    stats = jnp.concatenate(
        [dg1, dgm, dg2, dgf, jnp.concatenate([dps, drg], axis=1), loss_cols, jnp.zeros((2, d), F32)], axis=0
    )
    small = _GatherDirect([stats, dpw.reshape(N_POOL_GROUPS * GROUP, GROUP)])
    sum_f1w2, ((stats_all, pw_all),) = _wgrad(hm1, dh1, 0.5, "ffn1_w2_grad", carries=[small])
    sum_f1w1, ((parts_f1w2,),) = _wgrad(da1, n1, 1.0, "ffn1_w1_grad", carries=[_ChipScatter([sum_f1w2])])
    sum_f1w3, ((parts_f1w1,),) = _wgrad(db1, n1, 1.0, "ffn1_w3_grad", carries=[_ChipScatter([sum_f1w1])])
    sum_in, ((parts_f1w3,),) = _wgrad(dproj, u, 1.0, "w_in_grad", carries=[_ChipScatter([sum_f1w3])])
    sum_out, ((parts_in,),) = _wgrad(mm, dh2, 1.0, "w_out_grad", carries=[_ChipScatter([sum_in])])
    ((parts_out,),) = _comm_call([_ChipScatter([sum_out])], "scatter_last", peers=(4, 2, 6), collective_id=0)

    big = (
        (ffn1_w1, m_ffn1_w1, v_ffn1_w1, parts_f1w1, True),
        (ffn1_w3, m_ffn1_w3, v_ffn1_w3, parts_f1w3, True),
        (ffn1_w2, m_ffn1_w2, v_ffn1_w2, parts_f1w2, False),
        (w_in, m_w_in, v_w_in, parts_in, True),
        (w_out, m_w_out, v_w_out, parts_out, False),
        (ffn2_w1, m_ffn2_w1, v_ffn2_w1, parts_f2w1, True),
        (ffn2_w3, m_ffn2_w3, v_ffn2_w3, parts_f2w3, True),
        (ffn2_w2, m_ffn2_w2, v_ffn2_w2, parts_f2w2, False),
    )
    big_out = []
    for j, (w, m, v, parts, t) in enumerate(big):
        view = (lambda a: a[0].T) if t else (lambda a: a[0])
        back = (lambda a: a.T[None]) if t else (lambda a: a[None])
        big_out.append([back(a) for a in _adamw_big(view(w), parts, view(m), view(v), "adamw_%d" % j)])

    small_w = (ffn1_norm, mix_norm, ffn2_norm, fnorm, pool_scale, ret_norm, pw3.reshape(-1, GROUP))
    small_m = (m_ffn1_norm, m_mix_norm, m_ffn2_norm, m_final_norm.reshape(1, d), m_pool_scale, m_ret_norm, m_pool_w.reshape(-1, GROUP))
    small_v = (v_ffn1_norm, v_mix_norm, v_ffn2_norm, v_final_norm.reshape(1, d), v_pool_scale, v_ret_norm, v_pool_w.reshape(-1, GROUP))
    res = _adamw_small(stats_all, pw_all, small_w, small_m, small_v, pwid)
    loss = res[0][0, 0]
    small_out = [list(res[1 + 4 * j:5 + 4 * j]) for j in range(len(small_w))]
    small_out[3] = [a.reshape(d) for a in small_out[3]]
    small_out[6] = [a.reshape(pool_w.shape) for a in small_out[6]]

    order = [small_out[0], big_out[0], big_out[1], big_out[2], small_out[1], big_out[3], small_out[6], small_out[4],
             small_out[5], big_out[4], small_out[2], big_out[5], big_out[6], big_out[7], small_out[3]]
    result = [loss, dx[None]]
    for kind in range(4):
        result += [t[kind] for t in order]
    return tuple(result)
```

```python
import functools

import numpy as np
import jax
import jax.numpy as jnp
from jax import lax
from jax.experimental import pallas as pl
from jax.experimental.pallas import tpu as pltpu

F32 = jnp.float32
BF16 = jnp.bfloat16

NDEV = 8
NCHIP = 4
EPS = 1e-6
N_POOL_GROUPS = 4
POOL_WINDOWS = (2, 4, 8, 16)
MAX_WINDOW = 16
GROUP = 128
RET_HEADS = 4
ROPE_BASE = 10000.0
ADAM_LR = 0.001
ADAM_B1 = 0.9
ADAM_B2 = 0.999
ADAM_EPS = 1e-08
ADAM_WD = 0.01
ADAM_STEP = 10

VMEM_LIMIT = 56 * 1024 * 1024
FFN_CHUNK = 256
ROW_BAND = 32

NT = (((1,), (1,)), ((), ()))
NN = (((1,), (0,)), ((), ()))
TN = (((0,), (0,)), ((), ()))

ANY = pl.BlockSpec(memory_space=pl.ANY)


def _dot(a, b, dims):
    return lax.dot_general(a, b, dims, preferred_element_type=F32)


def _call(body, **kw):
    return pl.pallas_call(body, **kw)


def _params(**kw):
    return pltpu.CompilerParams(vmem_limit_bytes=VMEM_LIMIT, **kw)


def _seq(n):
    return _params(dimension_semantics=("arbitrary",) * n)


def _peer(k):
    x, y, c = lax.axis_index("x"), lax.axis_index("y"), lax.axis_index("c")
    return (1 - x if k & 4 else x, 1 - y if k & 2 else y, 1 - c if k & 1 else c)


def _flat(pos):
    return 4 * pos[0] + 2 * pos[1] + pos[2]


def _chip(pos):
    return 2 * pos[0] + pos[1]


def _row_tile(rows, cap):
    return max(t for t in range(16, min(rows, cap) + 1, 16) if rows % t == 0)


def _pieces(rows, n):
    tiles = rows // 16
    cuts = [16 * (tiles * q // n) for q in range(n + 1)]
    return [(a, b - a) for a, b in zip(cuts[:-1], cuts[1:])]


def _load_weights(parts, sems):
    copies = []
    for buf, dst in parts:
        rows = buf.shape[1]
        for p in range(NDEV):
            cp = pltpu.make_async_copy(buf.at[p], dst.at[pl.ds(p * rows, rows), :], sems.at[len(copies)])
            cp.start()
            copies.append(cp)
    return copies


def _sigmoid(a):
    return 1.0 / (1.0 + jnp.exp(-a))


def _remote(src, dst, send_sem, recv_sem, to):
    return pltpu.make_async_remote_copy(
        src_ref=src, dst_ref=dst, send_sem=send_sem, recv_sem=recv_sem, device_id=to, device_id_type=pl.DeviceIdType.MESH
    )


class _Gather:
    X, Y, FAR = 4, 2, 6
    peers = (1, 2, 4)
    COPIES = 8

    def __init__(self, shards):
        n = len(shards)
        self.operands = list(shards)
        self.out_shape = [jax.ShapeDtypeStruct((NDEV,) + a.shape, a.dtype) for a in shards]
        self.sems = [
            pltpu.SemaphoreType.DMA((self.COPIES * n,)), pltpu.SemaphoreType.DMA((self.COPIES * n,)),
            pltpu.SemaphoreType.DMA((n,)),
        ]
        self.stages = [self.begin, self.relay, self.relay_far, self.end]

    def _copy(self, t, k, block, to, ins, outs, sems, own=False, half=None):
        rows = outs[t].shape[1]
        part = pl.ds(0, rows) if half is None else pl.ds(half * (rows // 2), rows // 2)
        dst = outs[t].at[_flat(block), part, :]
        at = self.COPIES * t + k
        return _remote(ins[t] if own else dst, dst, sems[0].at[at], sems[1].at[at], to)

    def _local(self, t, ins, outs, sems):
        return pltpu.make_async_copy(ins[t], outs[t].at[_flat(_peer(0))], sems[2].at[t])

    def begin(self, ins, outs, sems):
        me = _peer(0)
        for t in range(len(ins)):
            self._local(t, ins, outs, sems).start()
            for k, code in enumerate((1, self.X, self.Y)):
                self._copy(t, k, me, _peer(code), ins, outs, sems, own=True).start()

    def relay(self, ins, outs, sems):
        me, sibling = _peer(0), _peer(1)
        for t in range(len(ins)):
            self._copy(t, 1, _peer(self.X), me, ins, outs, sems).wait_recv()
            self._copy(t, 3, _peer(self.X), _peer(self.Y), ins, outs, sems, half=0).start()
            self._copy(t, 5, _peer(self.X), sibling, ins, outs, sems).start()
            self._copy(t, 2, _peer(self.Y), me, ins, outs, sems).wait_recv()
            self._copy(t, 4, _peer(self.Y), _peer(self.X), ins, outs, sems, half=1).start()
            self._copy(t, 6, _peer(self.Y), sibling, ins, outs, sems).start()

    def relay_far(self, ins, outs, sems):
        me, sibling = _peer(0), _peer(1)
        for t in range(len(ins)):
            self._copy(t, 3, _peer(self.FAR), me, ins, outs, sems, half=0).wait_recv()
            self._copy(t, 4, _peer(self.FAR), me, ins, outs, sems, half=1).wait_recv()
            self._copy(t, 7, _peer(self.FAR), sibling, ins, outs, sems).start()

    def end(self, ins, outs, sems):
        me = _peer(0)
        for t in range(len(ins)):
            self._copy(t, 0, _peer(1), me, ins, outs, sems).wait_recv()
            for k, code in ((5, self.X), (6, self.Y), (7, self.FAR)):
                self._copy(t, k, _peer(code ^ 1), me, ins, outs, sems).wait_recv()
            for k in range(self.COPIES):
                self._copy(t, k, me, me, ins, outs, sems, half=0 if k == 3 else 1 if k == 4 else None).wait_send()
            self._local(t, ins, outs, sems).wait()


class _GatherDirect:
    peers = tuple(range(1, NDEV))

    def __init__(self, arrays):
        n = len(arrays)
        self.operands = list(arrays)
        self.out_shape = [jax.ShapeDtypeStruct((NDEV,) + a.shape, a.dtype) for a in arrays]
        self.sems = [pltpu.SemaphoreType.DMA((7 * n,)), pltpu.SemaphoreType.DMA((7 * n,)), pltpu.SemaphoreType.DMA((n,))]
        self.stages = [self.begin, self.end]

    def begin(self, ins, outs, sems):
        mine = _flat(_peer(0))
        for t in range(len(ins)):
            pltpu.make_async_copy(ins[t], outs[t].at[mine], sems[2].at[t]).start()
            for k in range(1, NDEV):
                _remote(ins[t], outs[t].at[mine], sems[0].at[7 * t + k - 1], sems[1].at[7 * t + k - 1], _peer(k)).start()

    def end(self, ins, outs, sems):
        mine = _flat(_peer(0))
        for t in range(len(ins)):
            for k in range(1, NDEV):
                cp = _remote(ins[t], outs[t].at[_flat(_peer(k))], sems[0].at[7 * t + k - 1], sems[1].at[7 * t + k - 1], _peer(k))
                cp.wait_recv()
                cp.wait_send()
            pltpu.make_async_copy(ins[t], outs[t].at[mine], sems[2].at[t]).wait()


class _ChipScatter:
    peers = (2, 4, 6)
    pieces = 1

    def __init__(self, sums):
        n = len(sums) * NCHIP * self.pieces
        self.operands = list(sums)
        self.out_shape = [jax.ShapeDtypeStruct(a.shape, a.dtype) for a in sums]
        self.sems = [pltpu.SemaphoreType.DMA((n,)), pltpu.SemaphoreType.DMA((n,))]
        self.stages = [self.begin, self.end]

    def _copies(self, ins, outs, sems, arriving):
        mine = _chip(_peer(0))
        copies = []
        for t in range(len(ins)):
            rows = ins[t].shape[1] // self.pieces
            for k in (0, 4, 2, 6):
                other = _chip(_peer(k))
                for q in range(self.pieces):
                    part = pl.ds(q * rows, rows)
                    at = len(copies)
                    if k == 0:
                        cp = pltpu.make_async_copy(ins[t].at[mine, part, :], outs[t].at[mine, part, :], sems[0].at[at])
                    else:
                        landing = outs[t].at[other if arriving else mine, part, :]
                        cp = _remote(ins[t].at[other, part, :], landing, sems[0].at[at], sems[1].at[at], _peer(k))
                    copies.append(cp)
        return copies

    def begin(self, ins, outs, sems):
        for cp in self._copies(ins, outs, sems, False):
            cp.start()

    def end(self, ins, outs, sems):
        for at, cp in enumerate(self._copies(ins, outs, sems, True)):
            if at % (NCHIP * self.pieces) < self.pieces:
                cp.wait()
            else:
                cp.wait_recv()
                cp.wait_send()


class _CastRows:
    peers = ()

    def __init__(self, arrays):
        n = len(arrays)
        self.operands = list(arrays)
        self.out_shape = [jax.ShapeDtypeStruct(a.shape, BF16) for a in arrays]
        self.sems = [pltpu.SemaphoreType.DMA((n,)), pltpu.SemaphoreType.DMA((n,))]
        self.sems += [pltpu.VMEM(a.shape, F32) for a in arrays] + [pltpu.VMEM(a.shape, BF16) for a in arrays]
        self.stages = [self.begin, self.convert, self.end]

    def _moves(self, t, ins, outs, scratch):
        n = len(ins)
        load = pltpu.make_async_copy(ins[t], scratch[2 + t], scratch[0].at[t])
        store = pltpu.make_async_copy(scratch[2 + n + t], outs[t], scratch[1].at[t])
        return load, store

    def begin(self, ins, outs, scratch):
        for t in range(len(ins)):
            self._moves(t, ins, outs, scratch)[0].start()

    def convert(self, ins, outs, scratch):
        n = len(ins)
        for t in range(n):
            load, store = self._moves(t, ins, outs, scratch)
            load.wait()
            scratch[2 + n + t][...] = scratch[2 + t][...].astype(BF16)
            store.start()

    def end(self, ins, outs, scratch):
        for t in range(len(ins)):
            self._moves(t, ins, outs, scratch)[1].wait()


def _split_refs(refs, counts):
    out, at = [], 0
    for n in counts:
        out.append(refs[at:at + n])
        at += n
    return out


BARRIER_IDS = {(2, 4, 6): 0, (1, 2, 4): 1, (1,): 2, (1, 2, 4, 6): 3}


def _peers_of(carries, own=()):
    peers = tuple(sorted(set(own).union(*[c.peers for c in carries])))
    return (peers, BARRIER_IDS[peers]) if peers in BARRIER_IDS else (None, None)


def _handshake(peers):
    barrier = pltpu.get_barrier_semaphore()
    for k in peers:
        pl.semaphore_signal(barrier, inc=1, device_id=_peer(k), device_id_type=pl.DeviceIdType.MESH)
    pl.semaphore_wait(barrier, len(peers))


def _comm_call(carries, name):
    nin = [len(c.operands) for c in carries]
    nout = [len(c.out_shape) for c in carries]
    nsem = [len(c.sems) for c in carries]
    peers, collective_id = _peers_of(carries)

    def body(*refs):
        if peers:
            _handshake(peers)
        ins, outs, sems = _split_refs(refs, (sum(nin), sum(nout), sum(nsem)))
        parts = list(zip(carries, _split_refs(ins, nin), _split_refs(outs, nout), _split_refs(sems, nsem)))
        for depth in range(max(len(c.stages) for c in carries)):
            for c, i, o, s in parts:
                if depth < len(c.stages) - 1:
                    c.stages[depth](i, o, s)
        for c, i, o, s in parts:
            c.stages[-1](i, o, s)

    res = _call(
        body,
        name=name,
        out_shape=[sh for c in carries for sh in c.out_shape],
        in_specs=[ANY] * sum(nin),
        out_specs=[ANY] * sum(nout),
        scratch_shapes=[sm for c in carries for sm in c.sems],
        compiler_params=_params(has_side_effects=True, collective_id=collective_id),
    )(*[a for c in carries for a in c.operands])
    return _split_refs(list(res), nout)


def _grid_call(body, carries, *, name, grid, in_specs, out_specs, out_shape, scratch_shapes, args, own_peers=()):
    ni, no, ns = len(in_specs), len(out_specs), len(scratch_shapes)
    nin = [len(c.operands) for c in carries]
    nout = [len(c.out_shape) for c in carries]
    nsem = [len(c.sems) for c in carries]
    steps = int(np.prod(grid))
    peers, collective_id = _peers_of(carries, own_peers)

    def when_of(stage, count):
        first, last = (5 * steps) // 8 - 1, steps - 2
        return max(0, last if count <= 3 else first + (last - first) * (stage - 1) // (count - 3))

    def wrapped(*refs):
        ins, cins, outs, couts, scr, csems = _split_refs(refs, (ni, sum(nin), no, sum(nout), ns, sum(nsem)))
        if not carries and not peers:
            return body(*ins, *outs, *scr)
        parts = list(zip(carries, _split_refs(cins, nin), _split_refs(couts, nout), _split_refs(csems, nsem)))
        step = pl.program_id(0)
        for axis in range(1, len(grid)):
            step = step * grid[axis] + pl.program_id(axis)

        @pl.when(step == 0)
        def _():
            if peers:
                _handshake(peers)
            for c, i, o, s in parts:
                c.stages[0](i, o, s)

        body(*ins, *outs, *scr)

        for c, i, o, s in parts:
            for stage in range(1, len(c.stages) - 1):
                pl.when(step == when_of(stage, len(c.stages)))(functools.partial(c.stages[stage], i, o, s))

        @pl.when(step == steps - 1)
        def _():
            for c, i, o, s in parts:
                c.stages[-1](i, o, s)

    res = _call(
        wrapped,
        name=name,
        grid=tuple(grid),
        in_specs=list(in_specs) + [ANY] * sum(nin),
        out_specs=list(out_specs) + [ANY] * sum(nout),
        out_shape=list(out_shape) + [sh for c in carries for sh in c.out_shape],
        scratch_shapes=list(scratch_shapes) + [sm for c in carries for sm in c.sems],
        compiler_params=_params(dimension_semantics=("arbitrary",) * len(grid), collective_id=collective_id),
    )(*args, *[a for c in carries for a in c.operands])
    res = list(res)
    return res[:no], _split_refs(res[no:], nout)


def _chunks(width):
    return [(at, min(FFN_CHUNK, width - at)) for at in range(0, width, FFN_CHUNK)]


def _start_chunk_loads(bufs, dsts, sems, chunks):
    copies = [[None] * len(chunks) for _ in bufs]
    for c, (at, width) in enumerate(chunks):
        for m, (buf, dst) in enumerate(zip(bufs, dsts)):
            cp = pltpu.make_async_copy(
                buf.at[pl.ds(at, width), :], dst.at[pl.ds(at, width), :], sems.at[m * len(chunks) + c]
            )
            cp.start()
            copies[m][c] = cp
    return copies


def _ffn_fwd(x, gain, weights, ffn, head=None, carries=()):
    s, d = x.shape
    tm = min(512, s)
    chunks = _chunks(ffn)

    def body(*refs):
        if head is None:
            x_ref, g_ref, b1, b3, b2, h_ref, a_ref, b_ref, hm_ref, w1s, w3s, w2s, sems = refs
        else:
            x_ref, g_ref, b1, b3, b2, gf_ref, t_ref, h_ref, a_ref, b_ref, hm_ref, dgf_ref, loss_ref, w1s, w3s, w2s, sems = refs

        def one_tile(loads):
            xv = x_ref[...]
            r = lax.rsqrt(jnp.mean(xv * xv, axis=-1, keepdims=True) + EPS)
            n = (xv * r * g_ref[...]).astype(BF16)
            acc = jnp.zeros((tm, d), F32)
            for c, (at, width) in enumerate(chunks):
                cols = slice(at, at + width)
                if loads is not None:
                    for m in range(3):
                        loads[m][c].wait()
                a = _dot(n, w1s[cols, :], NT)
                b = _dot(n, w3s[cols, :], NT)
                a_ref[:, cols] = a.astype(BF16)
                b_ref[:, cols] = b.astype(BF16)
                hm = (a * _sigmoid(a) * b).astype(BF16)
                hm_ref[:, cols] = hm
                acc = acc + _dot(hm, w2s[cols, :], NN)
            h = xv + 0.5 * acc
            if head is None:
                h_ref[...] = h
            else:
                rf = lax.rsqrt(jnp.mean(h * h, axis=-1, keepdims=True) + EPS)
                nh = h * rf
                gf = gf_ref[...]
                err = nh * gf - t_ref[...]
                loss_ref[...] += jnp.sum(err * err, axis=0, keepdims=True) * (0.5 / d)
                dy = err * (1.0 / d)
                dgf_ref[...] += jnp.sum(dy * nh, axis=0, keepdims=True)
                dn = dy * gf
                h_ref[...] = rf * (dn - nh * jnp.mean(dn * nh, axis=-1, keepdims=True))

        if head is None:
            @pl.when(pl.program_id(0) == 0)
            def _():
                one_tile(_start_chunk_loads((b1, b3, b2), (w1s, w3s, w2s), sems, chunks))

            @pl.when(pl.program_id(0) > 0)
            def _():
                one_tile(None)
        else:
            @pl.when(pl.program_id(0) == 0)
            def _():
                dgf_ref[...] = jnp.zeros_like(dgf_ref)
                loss_ref[...] = jnp.zeros_like(loss_ref)
                for loads in _start_chunk_loads((b1, b3, b2), (w1s, w3s, w2s), sems, chunks):
                    for cp in loads:
                        cp.wait()

            one_tile(None)

    tile = pl.BlockSpec((tm, d), lambda i: (i, 0))
    row = pl.BlockSpec((1, d), lambda i: (0, 0))
    wide = pl.BlockSpec((tm, ffn), lambda i: (i, 0))
    in_specs = [tile, row, ANY, ANY, ANY]
    out_shape = [jax.ShapeDtypeStruct((s, d), F32)] + [jax.ShapeDtypeStruct((s, ffn), BF16)] * 3
    out_specs = [tile, wide, wide, wide]
    args = [x, gain] + [w.reshape(ffn, d) for w in weights]
    if head is not None:
        in_specs += [row, tile]
        args += list(head)
        out_shape += [jax.ShapeDtypeStruct((1, d), F32)] * 2
        out_specs += [row, row]
    return _grid_call(
        body,
        carries,
        name="ffn_fwd_loss" if head is not None else "ffn_fwd",
        grid=(s // tm,),
        in_specs=in_specs,
        out_specs=out_specs,
        out_shape=out_shape,
        scratch_shapes=[pltpu.VMEM((ffn, d), BF16)] * 3 + [pltpu.SemaphoreType.DMA((3 * len(chunks),))],
        args=args,
    )


def _ffn_bwd(dh, x, a, b, gain, weights, ffn, name, carries=()):
    s, d = x.shape
    tm = min(512, s)
    halves = 2
    fh = ffn // halves

    def body(dh_ref, x_ref, a_ref, b_ref, g_ref, b1, b3, b2, dx_ref, da_ref, db_ref, n_ref, dg_ref, w1s, w3s, w2s, sems):
        i, j = pl.program_id(0), pl.program_id(1)

        @pl.when((i == 0) & (j == 0))
        def _():
            for cp in _load_weights(((b1, w1s), (b3, w3s), (b2, w2s)), sems):
                cp.wait()
            dg_ref[...] = jnp.zeros_like(dg_ref)

        @pl.when(j == 0)
        def _():
            dx_ref[...] = jnp.zeros_like(dx_ref)

        dob = (0.5 * dh_ref[...]).astype(BF16)
        chunks = _chunks(fh)

        def dhm_of(k):
            at, width = chunks[k]
            return _dot(dob, w2s[pl.ds(pl.multiple_of(j * fh + at, GROUP), width), :], NT)

        ahead = dhm_of(0)
        for k, (at, width) in enumerate(chunks):
            cols = slice(at, at + width)
            dhm = ahead
            if k + 1 < len(chunks):
                ahead = dhm_of(k + 1)
            for top in range(0, tm, ROW_BAND):
                band = slice(top, top + ROW_BAND)
                av = a_ref[band, cols].astype(F32)
                bv = b_ref[band, cols].astype(F32)
                sg = _sigmoid(av)
                dv = dhm[band]
                da_ref[band, cols] = (dv * bv * (sg * (1.0 + av * (1.0 - sg)))).astype(BF16)
                db_ref[band, cols] = (dv * (av * sg)).astype(BF16)
        half = pl.ds(pl.multiple_of(j * fh, GROUP), fh)
        dx_ref[...] += _dot(da_ref[...], w1s[half, :], NN) + _dot(db_ref[...], w3s[half, :], NN)

        @pl.when(j == halves - 1)
        def _():
            xv = x_ref[...]
            g = g_ref[...]
            r = lax.rsqrt(jnp.mean(xv * xv, axis=-1, keepdims=True) + EPS)
            nh = xv * r
            n_ref[...] = (nh * g).astype(BF16)
            total = dx_ref[...]
            dg_ref[...] += jnp.sum(total * nh, axis=0, keepdims=True)
            dnh = total * g
            dx_ref[...] = dh_ref[...] + r * (dnh - nh * jnp.mean(dnh * nh, axis=-1, keepdims=True))

    tile = pl.BlockSpec((tm, d), lambda i, j: (i, 0))
    row = pl.BlockSpec((1, d), lambda i, j: (0, 0))
    wide = pl.BlockSpec((tm, fh), lambda i, j: (i, j))
    return _grid_call(
        body,
        carries,
        name=name,
        grid=(s // tm, halves),
        in_specs=[tile, tile, wide, wide, row, ANY, ANY, ANY],
        out_specs=[tile, wide, wide, tile, row],
        out_shape=[
            jax.ShapeDtypeStruct((s, d), F32),
            jax.ShapeDtypeStruct((s, ffn), BF16),
            jax.ShapeDtypeStruct((s, ffn), BF16),
            jax.ShapeDtypeStruct((s, d), BF16),
            jax.ShapeDtypeStruct((1, d), F32),
        ],
        scratch_shapes=[pltpu.VMEM((ffn, d), BF16)] * 3 + [pltpu.SemaphoreType.DMA((3 * NDEV,))],
        args=[dh, x, a, b, gain] + list(weights),
    )


SWAP_PIECES = 1


def _wgrad(lhs, rhs, scale, name, carries=()):
    s, m = lhs.shape
    n = rhs.shape[1]
    rs = m // NDEV
    tk = min(1024, s)
    steps = s // tk
    pieces = [(j, at, size) for j in range(2) for at, size in _pieces(rs, SWAP_PIECES)]

    def body(l_ref, r_ref, o_ref, acc, mine, theirs, send_sems, recv_sems):
        h, k = pl.program_id(0), pl.program_id(1)

        @pl.when(k == 0)
        def _():
            acc[...] = _dot(l_ref[...], r_ref[...].astype(BF16), TN)

        @pl.when(k > 0)
        def _():
            acc[...] += _dot(l_ref[...], r_ref[...].astype(BF16), TN)

        def exchange(half):
            c = lax.axis_index("c")
            return [
                _remote(mine.at[half, 1 - c, j, pl.ds(at, size), :], theirs.at[half, j, pl.ds(at, size), :],
                        send_sems.at[half * len(pieces) + q], recv_sems.at[half * len(pieces) + q], _peer(1))
                for q, (j, at, size) in enumerate(pieces)
            ]

        def settle(half):
            for cp in exchange(half):
                cp.wait_recv()
            both = mine[half, lax.axis_index("c")].astype(F32) + theirs[half].astype(F32)
            o_ref[2 * half:2 * half + 2] = both.astype(BF16)
            for cp in exchange(half):
                cp.wait_send()

        for half in range(2):
            @pl.when((h == half) & (k == steps - 1))
            def _():
                for p in range(NCHIP):
                    mine[half, p % 2, p // 2] = (acc[p * rs:(p + 1) * rs, :] * scale).astype(BF16)
                for cp in exchange(half):
                    cp.start()
                if half == 1:
                    settle(0)
                    settle(1)

    (out,), carried = _grid_call(
        body,
        carries,
        name=name,
        grid=(2, steps),
        in_specs=[pl.BlockSpec((tk, m // 2), lambda h, k: (k, h)), pl.BlockSpec((tk, n), lambda h, k: (k, 0))],
        out_specs=[pl.BlockSpec((NCHIP, rs, n), lambda h, k: (0, 0, 0))],
        out_shape=[jax.ShapeDtypeStruct((NCHIP, rs, n), BF16)],
        scratch_shapes=[
            pltpu.VMEM((m // 2, n), F32), pltpu.VMEM((2, 2, 2, rs, n), BF16), pltpu.VMEM((2, 2, rs, n), BF16),
            pltpu.SemaphoreType.DMA((2 * len(pieces),)), pltpu.SemaphoreType.DMA((2 * len(pieces),)),
        ],
        args=[lhs, rhs],
        own_peers=(1,),
    )
    return out, carried


def _mix_constants(s):
    c = GROUP
    lg = np.log1p(-np.exp2(-5.0 - np.arange(RET_HEADS, dtype=np.float32))).astype(np.float32)
    pos = np.arange(c, dtype=np.float32)
    rel = pos[:, None] - pos[None, :]
    decay = np.where(rel[None] >= 0, np.exp(lg[:, None, None] * np.maximum(rel, 0.0)[None]), 0.0).astype(np.float32)
    ktail = np.exp(lg[:, None] * (c - 1 - pos)[None, :]).astype(np.float32)
    qhead = np.exp(lg[:, None] * (pos + 1.0)[None, :]).astype(np.float32)
    chunk_decay = [float(v) for v in np.exp(lg * np.float32(c)).astype(np.float32)]
    ones = np.ones((1, 1, c), np.float32)
    inv_freq = (1.0 / (np.float32(ROPE_BASE) ** (np.arange(0, c, 2, dtype=np.float32) / np.float32(c)))).astype(np.float32)
    ang = (np.arange(s, dtype=np.float32)[:, None] * inv_freq[None, :]).astype(np.float32)
    cos, sin = np.cos(ang).astype(np.float32), np.sin(ang).astype(np.float32)
    return dict(
        decay=jnp.asarray(decay),
        ktail=jnp.asarray(ktail[:, :, None] * ones),
        qhead=jnp.asarray(qhead[:, :, None] * ones),
        chunk_decay=chunk_decay,
        cos=jnp.asarray(np.concatenate([cos, cos], axis=-1)),
        sin=jnp.asarray(np.concatenate([-sin, sin], axis=-1)),
    )


def _rope(t, cos, sin):
    return t * cos + pltpu.roll(t, GROUP // 2, axis=1) * sin


def _rope_bwd(dt, cos, sin):
    return dt * cos + pltpu.roll(dt * sin, GROUP // 2, axis=1)


def _window_sums(ext, w, forward):
    rows = ext.shape[0]
    acc, k = ext, 1
    while k < w:
        acc = acc + pltpu.roll(acc, k if forward else rows - k, axis=0)
        k *= 2
    return acc


def _pool_counts(tile, tm, w):
    t = lax.broadcasted_iota(jnp.int32, (tm, 1), 0) + tile * tm
    return jnp.minimum(t + 1, w).astype(F32)


def _mix_fwd(h1, gain, weights, pool_w, pool_scale, ret_gain, consts, carries=()):
    s, d = h1.shape
    pwid = N_POOL_GROUPS * GROUP
    rwid = RET_HEADS * GROUP
    inw = pwid + 4 * rwid
    tm = min(256, s)
    nck = tm // GROUP
    cd = consts["chunk_decay"]

    def body(h_ref, g_ref, bin_, bout, pw_ref, ps_ref, rg_ref, cos_ref, sin_ref, dec_ref, kt_ref, qh_ref,
             h2_ref, proj_ref, o_ref, rs_ref, wins, wouts, state, carry, mbuf, sems):
        i = pl.program_id(0)

        @pl.when(i == 0)
        def _():
            for cp in _load_weights(((bin_, wins), (bout, wouts)), sems):
                cp.wait()
            state[...] = jnp.zeros_like(state)
            carry[...] = jnp.zeros_like(carry)

        hv = h_ref[...]
        r = lax.rsqrt(jnp.mean(hv * hv, axis=-1, keepdims=True) + EPS)
        u = (hv * r * g_ref[...]).astype(BF16)
        proj_ref[...] = _dot(u, wins[...], NT)

        ext = jnp.concatenate([carry[...], proj_ref[:, 0:pwid]], axis=0)
        carry[...] = proj_ref[tm - MAX_WINDOW:tm, 0:pwid]
        for gi, w in enumerate(POOL_WINDOWS):
            cols = slice(gi * GROUP, (gi + 1) * GROUP)
            xg = ext[:, cols]
            ws = _window_sums(xg, w, True)[MAX_WINDOW:, :]
            pooled = ws / _pool_counts(i, tm, w) - xg[MAX_WINDOW:, :]
            z = _dot(pooled.astype(BF16), pw_ref[gi].astype(BF16), NN)
            mbuf[:, cols] = (z * ps_ref[:, cols]).astype(BF16)

        cos, sin = cos_ref[...], sin_ref[...]
        for h in range(RET_HEADS):
            cq = slice(pwid + h * GROUP, pwid + (h + 1) * GROUP)
            ck = slice(pwid + rwid + h * GROUP, pwid + rwid + (h + 1) * GROUP)
            cv = slice(pwid + 2 * rwid + h * GROUP, pwid + 2 * rwid + (h + 1) * GROUP)
            cg = slice(pwid + 3 * rwid + h * GROUP, pwid + 3 * rwid + (h + 1) * GROUP)
            ch = slice(h * GROUP, (h + 1) * GROUP)
            qr = _rope(proj_ref[:, cq], cos, sin)
            kr = _rope(proj_ref[:, ck], cos, sin) * (GROUP ** -0.5)
            vb = proj_ref[:, cv].astype(BF16)
            for n in range(nck):
                rows = slice(n * GROUP, (n + 1) * GROUP)
                qc, kc, vc = qr[rows], kr[rows], vb[rows]
                rb = state[h]
                rs_ref[n, h] = rb
                p = (_dot(qc.astype(BF16), kc.astype(BF16), NT) * dec_ref[h]).astype(BF16)
                o = _dot(p, vc, NN) + _dot((qc * qh_ref[h]).astype(BF16), rb.astype(BF16), NN)
                state[h] = cd[h] * rb + _dot((kc * kt_ref[h]).astype(BF16), vc, TN)
                o_ref[rows, ch] = o
                on = o * lax.rsqrt(jnp.mean(o * o, axis=-1, keepdims=True) + EPS)
                gv = proj_ref[rows, cg]
                mbuf[rows, pwid + h * GROUP:pwid + (h + 1) * GROUP] = (
                    gv * _sigmoid(gv) * (on * rg_ref[:, ch])
                ).astype(BF16)
        h2_ref[...] = hv + _dot(mbuf[...], wouts[...], NN)

    tile = pl.BlockSpec((tm, d), lambda i: (i, 0))
    full = lambda shape: pl.BlockSpec(shape, lambda i: (0,) * len(shape))
    return _grid_call(
        body,
        carries,
        name="mix_fwd",
        grid=(s // tm,),
        in_specs=[
            tile, full((1, d)), ANY, ANY,
            full((N_POOL_GROUPS, GROUP, GROUP)), full((1, pwid)), full((1, rwid)),
            pl.BlockSpec((tm, GROUP), lambda i: (i, 0)), pl.BlockSpec((tm, GROUP), lambda i: (i, 0)),
            full((RET_HEADS, GROUP, GROUP)), full((RET_HEADS, GROUP, GROUP)), full((RET_HEADS, GROUP, GROUP)),
        ],
        out_specs=[
            tile,
            pl.BlockSpec((tm, inw), lambda i: (i, 0)),
            pl.BlockSpec((tm, rwid), lambda i: (i, 0)),
            pl.BlockSpec((nck, RET_HEADS, GROUP, GROUP), lambda i: (i, 0, 0, 0)),
        ],
        out_shape=[
            jax.ShapeDtypeStruct((s, d), F32),
            jax.ShapeDtypeStruct((s, inw), F32),
            jax.ShapeDtypeStruct((s, rwid), F32),
            jax.ShapeDtypeStruct((s // GROUP, RET_HEADS, GROUP, GROUP), F32),
        ],
        scratch_shapes=[
            pltpu.VMEM((inw, d), BF16), pltpu.VMEM((d, d), BF16),
            pltpu.VMEM((RET_HEADS, GROUP, GROUP), F32), pltpu.VMEM((MAX_WINDOW, pwid), F32),
            pltpu.VMEM((tm, d), BF16), pltpu.SemaphoreType.DMA((2 * NDEV,)),
        ],
        args=[h1, gain, weights[0], weights[1], pool_w, pool_scale, ret_gain,
              consts["cos"], consts["sin"], consts["decay"], consts["ktail"], consts["qhead"]],
    )


def _mix_bwd(dh2, h1, proj, o_saved, rsave, gain, weights, pool_w, pool_scale, ret_gain, consts, carries=()):
    s, d = h1.shape
    pwid = N_POOL_GROUPS * GROUP
    rwid = RET_HEADS * GROUP
    inw = pwid + 4 * rwid
    tm = min(256, s)
    nck = tm // GROUP
    nt = s // tm
    cd = consts["chunk_decay"]
    halo_per_tile = tm // MAX_WINDOW

    def body(dh2_ref, h_ref, proj_ref, halo_ref, o_ref, rs_ref, g_ref, bin_, bout, pw_ref, ps_ref, rg_ref,
             cos_ref, sin_ref, dec_ref, kt_ref, qh_ref,
             dh1_ref, dproj_ref, u_ref, m_ref, dpw_ref, dps_ref, drg_ref, dg_ref,
             wins, wouts, dstate, carry, dm, dpj, sems):
        i = pl.program_id(0)
        tile = nt - 1 - i

        @pl.when(i == 0)
        def _():
            for cp in _load_weights(((bin_, wins), (bout, wouts)), sems):
                cp.wait()
            dstate[...] = jnp.zeros_like(dstate)
            carry[...] = jnp.zeros_like(carry)
            for ref in (dpw_ref, dps_ref, drg_ref, dg_ref):
                ref[...] = jnp.zeros_like(ref)

        dh2v = dh2_ref[...]
        dm[...] = _dot(dh2v.astype(BF16), wouts[...], NT)
        hv = h_ref[...]
        g = g_ref[...]
        r = lax.rsqrt(jnp.mean(hv * hv, axis=-1, keepdims=True) + EPS)
        uh = hv * r
        u_ref[...] = (uh * g).astype(BF16)

        halo = jnp.where(tile == 0, 0.0, halo_ref[...])
        ext = jnp.concatenate([halo, proj_ref[:, 0:pwid]], axis=0)
        next_dpn = carry[...]
        for gi, w in enumerate(POOL_WINDOWS):
            cols = slice(gi * GROUP, (gi + 1) * GROUP)
            xg = ext[:, cols]
            cnt = _pool_counts(tile, tm, w)
            pooled = (_window_sums(xg, w, True)[MAX_WINDOW:, :] / cnt - xg[MAX_WINDOW:, :]).astype(BF16)
            pwb = pw_ref[gi].astype(BF16)
            z = _dot(pooled, pwb, NN)
            scale = ps_ref[:, cols]
            m_ref[:, cols] = (z * scale).astype(BF16)
            da = dm[:, cols]
            dps_ref[:, cols] += jnp.sum(da * z, axis=0, keepdims=True)
            dz = (da * scale).astype(BF16)
            dpw_ref[gi] += _dot(pooled, dz, TN)
            dpl = _dot(dz, pwb, NT)
            dpn = dpl / cnt
            ext2 = jnp.concatenate([dpn, next_dpn[:, cols]], axis=0)
            dpj[:, cols] = (_window_sums(ext2, w, False)[0:tm, :] - dpl).astype(BF16)
            carry[:, cols] = dpn[0:MAX_WINDOW, :]

        cos, sin = cos_ref[...], sin_ref[...]
        for h in range(RET_HEADS):
            cq = slice(pwid + h * GROUP, pwid + (h + 1) * GROUP)
            ck = slice(pwid + rwid + h * GROUP, pwid + rwid + (h + 1) * GROUP)
            cv = slice(pwid + 2 * rwid + h * GROUP, pwid + 2 * rwid + (h + 1) * GROUP)
            cg = slice(pwid + 3 * rwid + h * GROUP, pwid + 3 * rwid + (h + 1) * GROUP)
            ch = slice(h * GROUP, (h + 1) * GROUP)
            qr = _rope(proj_ref[:, cq], cos, sin)
            kr = _rope(proj_ref[:, ck], cos, sin) * (GROUP ** -0.5)
            vb = proj_ref[:, cv].astype(BF16)
            gv = proj_ref[:, cg]
            ov = o_ref[:, ch]
            ro = lax.rsqrt(jnp.mean(ov * ov, axis=-1, keepdims=True) + EPS)
            on = ov * ro
            rg = rg_ref[:, ch]
            db = dm[:, pwid + h * GROUP:pwid + (h + 1) * GROUP]
            sg = _sigmoid(gv)
            sl = gv * sg
            m_ref[:, pwid + h * GROUP:pwid + (h + 1) * GROUP] = (sl * (on * rg)).astype(BF16)
            dpj[:, cg] = (db * (on * rg) * (sg * (1.0 + gv * (1.0 - sg)))).astype(BF16)
            drg_ref[:, ch] += jnp.sum(db * sl * on, axis=0, keepdims=True)
            don = db * sl * rg
            do = (ro * (don - on * jnp.mean(don * on, axis=-1, keepdims=True))).astype(BF16)
            for n in reversed(range(nck)):
                rows = slice(n * GROUP, (n + 1) * GROUP)
                qc, kc, vc, dob = qr[rows], kr[rows], vb[rows], do[rows]
                qcb, kcb = qc.astype(BF16), kc.astype(BF16)
                qh = (qc * qh_ref[h]).astype(BF16)
                kt = (kc * kt_ref[h]).astype(BF16)
                rn = rs_ref[n, h].astype(BF16)
                dnext = dstate[h]
                dnb = dnext.astype(BF16)
                dec = dec_ref[h]
                p = (_dot(qcb, kcb, NT) * dec).astype(BF16)
                ds = (_dot(dob, vc, NT) * dec).astype(BF16)
                dv = _dot(p, dob, TN) + _dot(kt, dnb, NN)
                dq = _dot(ds, kcb, NN) + _dot(dob, rn, NT) * qh_ref[h]
                dk = _dot(ds, qcb, TN) + _dot(vc, dnb, NT) * kt_ref[h]
                dstate[h] = cd[h] * dnext + _dot(qh, dob, TN)
                dpj[rows, cq] = _rope_bwd(dq, cos[rows], sin[rows]).astype(BF16)
                dpj[rows, ck] = _rope_bwd(dk * (GROUP ** -0.5), cos[rows], sin[rows]).astype(BF16)
                dpj[rows, cv] = dv.astype(BF16)

        dproj_ref[...] = dpj[...]
        du = _dot(dpj[...], wins[...], NN)
        dg_ref[...] += jnp.sum(du * uh, axis=0, keepdims=True)
        dn = du * g
        dh1_ref[...] = dh2v + r * (dn - uh * jnp.mean(dn * uh, axis=-1, keepdims=True))

    rev = lambda i: (nt - 1 - i, 0)
    tile = pl.BlockSpec((tm, d), rev)
    full = lambda shape: pl.BlockSpec(shape, lambda i: (0,) * len(shape))
    return _grid_call(
        body,
        carries,
        name="mix_bwd",
        grid=(nt,),
        in_specs=[
            tile, tile,
            pl.BlockSpec((tm, inw), rev),
            pl.BlockSpec((MAX_WINDOW, pwid), lambda i: (jnp.maximum((nt - 1 - i) * halo_per_tile - 1, 0), 0)),
            pl.BlockSpec((tm, rwid), rev),
            pl.BlockSpec((nck, RET_HEADS, GROUP, GROUP), lambda i: (nt - 1 - i, 0, 0, 0)),
            full((1, d)), ANY, ANY,
            full((N_POOL_GROUPS, GROUP, GROUP)), full((1, pwid)), full((1, rwid)),
            pl.BlockSpec((tm, GROUP), rev), pl.BlockSpec((tm, GROUP), rev),
            full((RET_HEADS, GROUP, GROUP)), full((RET_HEADS, GROUP, GROUP)), full((RET_HEADS, GROUP, GROUP)),
        ],
        out_specs=[
            tile, pl.BlockSpec((tm, inw), rev), tile, tile,
            full((N_POOL_GROUPS, GROUP, GROUP)), full((1, pwid)), full((1, rwid)), full((1, d)),
        ],
        out_shape=[
            jax.ShapeDtypeStruct((s, d), F32),
            jax.ShapeDtypeStruct((s, inw), BF16),
            jax.ShapeDtypeStruct((s, d), BF16),
            jax.ShapeDtypeStruct((s, d), BF16),
            jax.ShapeDtypeStruct((N_POOL_GROUPS, GROUP, GROUP), F32),
            jax.ShapeDtypeStruct((1, pwid), F32),
            jax.ShapeDtypeStruct((1, rwid), F32),
            jax.ShapeDtypeStruct((1, d), F32),
        ],
        scratch_shapes=[
            pltpu.VMEM((inw, d), BF16), pltpu.VMEM((d, d), BF16),
            pltpu.VMEM((RET_HEADS, GROUP, GROUP), F32), pltpu.VMEM((MAX_WINDOW, pwid), F32),
            pltpu.VMEM((tm, d), F32), pltpu.VMEM((tm, inw), BF16), pltpu.SemaphoreType.DMA((2 * NDEV,)),
        ],
        args=[dh2, h1, proj, proj, o_saved, rsave, gain, weights[0], weights[1], pool_w, pool_scale, ret_gain,
              consts["cos"], consts["sin"], consts["decay"], consts["ktail"], consts["qhead"]],
    )


def _adam(w, g, m, v):
    m = ADAM_B1 * m + (1.0 - ADAM_B1) * g
    v = ADAM_B2 * v + (1.0 - ADAM_B2) * jnp.square(g)
    m_hat = m / (1.0 - ADAM_B1 ** ADAM_STEP)
    v_hat = v / (1.0 - ADAM_B2 ** ADAM_STEP)
    delta = -ADAM_LR * (m_hat / (jnp.sqrt(v_hat) + ADAM_EPS) + ADAM_WD * w)
    return delta, m, v


def _adamw_big(w, parts, m, v, name):
    rows, d = w.shape
    tr = _row_tile(rows, 176)

    def body(w_ref, p_ref, m_ref, v_ref, g_ref, d_ref, nm_ref, nv_ref):
        g = p_ref[0].astype(F32)
        for q in range(1, NCHIP):
            g = g + p_ref[q].astype(F32)
        g_ref[...] = g
        d_ref[...], nm_ref[...], nv_ref[...] = _adam(w_ref[...], g, m_ref[...], v_ref[...])

    spec = pl.BlockSpec((tr, d), lambda i: (i, 0))
    return _call(
        body,
        name=name,
        grid=(rows // tr,),
        in_specs=[spec, pl.BlockSpec((NCHIP, tr, d), lambda i: (0, i, 0)), spec, spec],
        out_specs=[spec] * 4,
        out_shape=[jax.ShapeDtypeStruct((rows, d), F32)] * 4,
        compiler_params=_seq(1),
    )(w, parts, m, v)


def _adamw_small(stats_all, pw_all, ws, ms, vs, pwid):
    nsmall = len(ws)

    def body(*refs):
        st_ref, pwa_ref = refs[0], refs[1]
        w_refs = refs[2:2 + nsmall]
        m_refs = refs[2 + nsmall:2 + 2 * nsmall]
        v_refs = refs[2 + 2 * nsmall:2 + 3 * nsmall]
        outs = refs[2 + 3 * nsmall:]
        st = st_ref[0]
        pwg = pwa_ref[0]
        for q in range(1, NDEV):
            st = st + st_ref[q]
            pwg = pwg + pwa_ref[q]
        grads = [st[0:1, :], st[1:2, :], st[2:3, :], st[3:4, :], st[4:5, 0:pwid], st[4:5, pwid:2 * pwid], pwg]
        outs[0][...] = jnp.zeros((1, GROUP), F32) + jnp.sum(st[5:6, :])
        for j in range(nsmall):
            delta, nm, nv = _adam(w_refs[j][...], grads[j], m_refs[j][...], v_refs[j][...])
            outs[1 + 4 * j][...] = grads[j]
            outs[2 + 4 * j][...] = delta
            outs[3 + 4 * j][...] = nm
            outs[4 + 4 * j][...] = nv

    out_shape = [jax.ShapeDtypeStruct((1, GROUP), F32)]
    for w in ws:
        out_shape += [jax.ShapeDtypeStruct(w.shape, F32)] * 4
    return _call(body, name="adamw_small", out_shape=out_shape, compiler_params=_params())(
        stats_all, pw_all, *ws, *ms, *vs
    )


def kernel(x, ffn1_norm, ffn1_w1, ffn1_w3, ffn1_w2, mix_norm, w_in, pool_w, pool_scale, ret_norm, w_out, ffn2_norm, ffn2_w1, ffn2_w3, ffn2_w2, final_norm, loss_target, m_ffn1_norm, m_ffn1_w1, m_ffn1_w3, m_ffn1_w2, m_mix_norm, m_w_in, m_pool_w, m_pool_scale, m_ret_norm, m_w_out, m_ffn2_norm, m_ffn2_w1, m_ffn2_w3, m_ffn2_w2, m_final_norm, v_ffn1_norm, v_ffn1_w1, v_ffn1_w3, v_ffn1_w2, v_mix_norm, v_w_in, v_pool_w, v_pool_scale, v_ret_norm, v_w_out, v_ffn2_norm, v_ffn2_w1, v_ffn2_w3, v_ffn2_w2, v_final_norm):
    s, d = x.shape[1], x.shape[2]
    ffn = ffn1_w1.shape[2] * NDEV
    pwid = pool_scale.shape[1]
    xs, tgt = x[0], loss_target[0]
    consts = _mix_constants(s)
    pw3 = pool_w[0]
    fnorm = final_norm.reshape(1, d)

    rows_of = lambda w, transposed: w[0].T if transposed else w[0]
    send_f1 = [rows_of(w, t).astype(BF16) for w, t in ((ffn1_w1, True), (ffn1_w3, True), (ffn1_w2, False))]
    later = [rows_of(w, t) for w, t in ((w_in, True), (w_out, False), (ffn2_w1, True), (ffn2_w3, True), (ffn2_w2, False))]

    sent_later, w_f1 = _comm_call([_CastRows(later), _Gather(send_f1)], "gather_ffn1")
    send_mix, send_f2 = sent_later[:2], sent_later[2:]
    (h1, a1, b1, hm1), (more,) = _ffn_fwd(xs, ffn1_norm, w_f1, ffn, carries=[_Gather(send_mix + send_f2[:1])])
    w_mix = more[:2]
    (h2, proj, o_saved, rsave), (rest,) = _mix_fwd(
        h1, mix_norm, w_mix, pw3, pool_scale, ret_norm, consts, carries=[_Gather(send_f2[1:])]
    )
    w_f2 = more[2:] + rest
    (dh3, a2, b2, hm2, dgf, loss_cols), _ = _ffn_fwd(h2, ffn2_norm, w_f2, ffn, head=(fnorm, tgt))

    (dh2, da2, db2, n2, dg2), _ = _ffn_bwd(dh3, h2, a2, b2, ffn2_norm, w_f2, ffn, "ffn2_bwd")
    sum_f2w1, _ = _wgrad(da2, n2, 1.0, "ffn2_w1_grad")
    sum_f2w3, _ = _wgrad(db2, n2, 1.0, "ffn2_w3_grad")
    sum_f2w2, ((parts_f2w1,),) = _wgrad(hm2, dh3, 0.5, "ffn2_w2_grad", carries=[_ChipScatter([sum_f2w1])])

    (dh1, dproj, u, mm, dpw, dps, drg, dgm), ((parts_f2w3, parts_f2w2),) = _mix_bwd(
        dh2, h1, proj, o_saved, rsave, mix_norm, w_mix, pw3, pool_scale, ret_norm, consts,
        carries=[_ChipScatter([sum_f2w3, sum_f2w2])],
    )
    (dx, da1, db1, n1, dg1), _ = _ffn_bwd(dh1, xs, a1, b1, ffn1_norm, w_f1, ffn, "ffn1_bwd")
    stats = jnp.concatenate(
        [dg1, dgm, dg2, dgf, jnp.concatenate([dps, drg], axis=1), loss_cols, jnp.zeros((2, d), F32)], axis=0
    )
    small = _GatherDirect([stats, dpw.reshape(N_POOL_GROUPS * GROUP, GROUP)])
    sum_f1w2, ((stats_all, pw_all),) = _wgrad(hm1, dh1, 0.5, "ffn1_w2_grad", carries=[small])
    sum_f1w1, ((parts_f1w2,),) = _wgrad(da1, n1, 1.0, "ffn1_w1_grad", carries=[_ChipScatter([sum_f1w2])])
    sum_f1w3, ((parts_f1w1,),) = _wgrad(db1, n1, 1.0, "ffn1_w3_grad", carries=[_ChipScatter([sum_f1w1])])
    sum_in, ((parts_f1w3,),) = _wgrad(dproj, u, 1.0, "w_in_grad", carries=[_ChipScatter([sum_f1w3])])
    sum_out, ((parts_in,),) = _wgrad(mm, dh2, 1.0, "w_out_grad", carries=[_ChipScatter([sum_in])])
    ((parts_out,),) = _comm_call([_ChipScatter([sum_out])], "scatter_last")

    big = (
        (ffn1_w1, m_ffn1_w1, v_ffn1_w1, parts_f1w1, True),
        (ffn1_w3, m_ffn1_w3, v_ffn1_w3, parts_f1w3, True),
        (ffn1_w2, m_ffn1_w2, v_ffn1_w2, parts_f1w2, False),
        (w_in, m_w_in, v_w_in, parts_in, True),
        (w_out, m_w_out, v_w_out, parts_out, False),
        (ffn2_w1, m_ffn2_w1, v_ffn2_w1, parts_f2w1, True),
        (ffn2_w3, m_ffn2_w3, v_ffn2_w3, parts_f2w3, True),
        (ffn2_w2, m_ffn2_w2, v_ffn2_w2, parts_f2w2, False),
    )
    big_out = []
    for j, (w, m, v, parts, t) in enumerate(big):
        view = (lambda a: a[0].T) if t else (lambda a: a[0])
        back = (lambda a: a.T[None]) if t else (lambda a: a[None])
        big_out.append([back(a) for a in _adamw_big(view(w), parts, view(m), view(v), "adamw_%d" % j)])

    small_w = (ffn1_norm, mix_norm, ffn2_norm, fnorm, pool_scale, ret_norm, pw3.reshape(-1, GROUP))
    small_m = (m_ffn1_norm, m_mix_norm, m_ffn2_norm, m_final_norm.reshape(1, d), m_pool_scale, m_ret_norm, m_pool_w.reshape(-1, GROUP))
    small_v = (v_ffn1_norm, v_mix_norm, v_ffn2_norm, v_final_norm.reshape(1, d), v_pool_scale, v_ret_norm, v_pool_w.reshape(-1, GROUP))
    res = _adamw_small(stats_all, pw_all, small_w, small_m, small_v, pwid)
    loss = res[0][0, 0]
    small_out = [list(res[1 + 4 * j:5 + 4 * j]) for j in range(len(small_w))]
    small_out[3] = [a.reshape(d) for a in small_out[3]]
    small_out[6] = [a.reshape(pool_w.shape) for a in small_out[6]]

    order = [small_out[0], big_out[0], big_out[1], big_out[2], small_out[1], big_out[3], small_out[6], small_out[4],
             small_out[5], big_out[4], small_out[2], big_out[5], big_out[6], big_out[7], small_out[3]]
    result = [loss, dx[None]]
    for kind in range(4):
        result += [t[kind] for t in order]
    return tuple(result)
```

```python
import functools

import numpy as np
import jax
import jax.numpy as jnp
from jax import lax
from jax.experimental import pallas as pl
from jax.experimental.pallas import tpu as pltpu

F32 = jnp.float32
BF16 = jnp.bfloat16

NDEV = 8
NCHIP = 4
EPS = 1e-6
N_POOL_GROUPS = 4
POOL_WINDOWS = (2, 4, 8, 16)
MAX_WINDOW = 16
GROUP = 128
RET_HEADS = 4
ROPE_BASE = 10000.0
ADAM_LR = 0.001
ADAM_B1 = 0.9
ADAM_B2 = 0.999
ADAM_EPS = 1e-08
ADAM_WD = 0.01
ADAM_STEP = 10

VMEM_LIMIT = 56 * 1024 * 1024
FFN_CHUNK = 256
ROW_BAND = 32

NT = (((1,), (1,)), ((), ()))
NN = (((1,), (0,)), ((), ()))
TN = (((0,), (0,)), ((), ()))

ANY = pl.BlockSpec(memory_space=pl.ANY)


def _dot(a, b, dims):
    return lax.dot_general(a, b, dims, preferred_element_type=F32)


def _call(body, **kw):
    return pl.pallas_call(body, **kw)


def _params(**kw):
    return pltpu.CompilerParams(vmem_limit_bytes=VMEM_LIMIT, **kw)


def _seq(n):
    return _params(dimension_semantics=("arbitrary",) * n)


def _peer(k):
    x, y, c = lax.axis_index("x"), lax.axis_index("y"), lax.axis_index("c")
    return (1 - x if k & 4 else x, 1 - y if k & 2 else y, 1 - c if k & 1 else c)


def _flat(pos):
    return 4 * pos[0] + 2 * pos[1] + pos[2]


def _chip(pos):
    return 2 * pos[0] + pos[1]


def _row_tile(rows, cap):
    return max(t for t in range(16, min(rows, cap) + 1, 16) if rows % t == 0)


def _pieces(rows, n):
    tiles = rows // 16
    cuts = [16 * (tiles * q // n) for q in range(n + 1)]
    return [(a, b - a) for a, b in zip(cuts[:-1], cuts[1:])]


def _load_weights(parts, sems):
    copies = []
    for buf, dst in parts:
        rows = buf.shape[1]
        for p in range(NDEV):
            cp = pltpu.make_async_copy(buf.at[p], dst.at[pl.ds(p * rows, rows), :], sems.at[len(copies)])
            cp.start()
            copies.append(cp)
    return copies


def _sigmoid(a):
    return 1.0 / (1.0 + jnp.exp(-a))


def _remote(src, dst, send_sem, recv_sem, to):
    return pltpu.make_async_remote_copy(
        src_ref=src, dst_ref=dst, send_sem=send_sem, recv_sem=recv_sem, device_id=to, device_id_type=pl.DeviceIdType.MESH
    )


class _Gather:
    X, Y, FAR = 4, 2, 6
    peers = (1, 2, 4)
    COPIES = 8

    def __init__(self, shards):
        n = len(shards)
        self.operands = list(shards)
        self.out_shape = [jax.ShapeDtypeStruct((NDEV,) + a.shape, a.dtype) for a in shards]
        self.sems = [
            pltpu.SemaphoreType.DMA((self.COPIES * n,)), pltpu.SemaphoreType.DMA((self.COPIES * n,)),
            pltpu.SemaphoreType.DMA((n,)),
        ]
        self.stages = [self.begin, self.relay, self.relay_far, self.end]

    def _copy(self, t, k, block, to, ins, outs, sems, own=False, half=None):
        rows = outs[t].shape[1]
        part = pl.ds(0, rows) if half is None else pl.ds(half * (rows // 2), rows // 2)
        dst = outs[t].at[_flat(block), part, :]
        at = self.COPIES * t + k
        return _remote(ins[t] if own else dst, dst, sems[0].at[at], sems[1].at[at], to)

    def _local(self, t, ins, outs, sems):
        return pltpu.make_async_copy(ins[t], outs[t].at[_flat(_peer(0))], sems[2].at[t])

    def begin(self, ins, outs, sems):
        me = _peer(0)
        for t in range(len(ins)):
            self._local(t, ins, outs, sems).start()
            for k, code in enumerate((1, self.X, self.Y)):
                self._copy(t, k, me, _peer(code), ins, outs, sems, own=True).start()

    def relay(self, ins, outs, sems):
        me, sibling = _peer(0), _peer(1)
        for t in range(len(ins)):
            self._copy(t, 1, _peer(self.X), me, ins, outs, sems).wait_recv()
            self._copy(t, 3, _peer(self.X), _peer(self.Y), ins, outs, sems, half=0).start()
            self._copy(t, 5, _peer(self.X), sibling, ins, outs, sems).start()
            self._copy(t, 2, _peer(self.Y), me, ins, outs, sems).wait_recv()
            self._copy(t, 4, _peer(self.Y), _peer(self.X), ins, outs, sems, half=1).start()
            self._copy(t, 6, _peer(self.Y), sibling, ins, outs, sems).start()

    def relay_far(self, ins, outs, sems):
        me, sibling = _peer(0), _peer(1)
        for t in range(len(ins)):
            self._copy(t, 3, _peer(self.FAR), me, ins, outs, sems, half=0).wait_recv()
            self._copy(t, 4, _peer(self.FAR), me, ins, outs, sems, half=1).wait_recv()
            self._copy(t, 7, _peer(self.FAR), sibling, ins, outs, sems).start()

    def end(self, ins, outs, sems):
        me = _peer(0)
        for t in range(len(ins)):
            self._copy(t, 0, _peer(1), me, ins, outs, sems).wait_recv()
            for k, code in ((5, self.X), (6, self.Y), (7, self.FAR)):
                self._copy(t, k, _peer(code ^ 1), me, ins, outs, sems).wait_recv()
            for k in range(self.COPIES):
                self._copy(t, k, me, me, ins, outs, sems, half=0 if k == 3 else 1 if k == 4 else None).wait_send()
            self._local(t, ins, outs, sems).wait()


class _GatherDirect:
    peers = tuple(range(1, NDEV))

    def __init__(self, arrays):
        n = len(arrays)
        self.operands = list(arrays)
        self.out_shape = [jax.ShapeDtypeStruct((NDEV,) + a.shape, a.dtype) for a in arrays]
        self.sems = [pltpu.SemaphoreType.DMA((7 * n,)), pltpu.SemaphoreType.DMA((7 * n,)), pltpu.SemaphoreType.DMA((n,))]
        self.stages = [self.begin, self.end]

    def begin(self, ins, outs, sems):
        mine = _flat(_peer(0))
        for t in range(len(ins)):
            pltpu.make_async_copy(ins[t], outs[t].at[mine], sems[2].at[t]).start()
            for k in range(1, NDEV):
                _remote(ins[t], outs[t].at[mine], sems[0].at[7 * t + k - 1], sems[1].at[7 * t + k - 1], _peer(k)).start()

    def end(self, ins, outs, sems):
        mine = _flat(_peer(0))
        for t in range(len(ins)):
            for k in range(1, NDEV):
                cp = _remote(ins[t], outs[t].at[_flat(_peer(k))], sems[0].at[7 * t + k - 1], sems[1].at[7 * t + k - 1], _peer(k))
                cp.wait_recv()
                cp.wait_send()
            pltpu.make_async_copy(ins[t], outs[t].at[mine], sems[2].at[t]).wait()


class _ChipScatter:
    peers = (2, 4, 6)
    pieces = 1

    def __init__(self, sums):
        n = len(sums) * NCHIP * self.pieces
        self.operands = list(sums)
        self.out_shape = [jax.ShapeDtypeStruct(a.shape, a.dtype) for a in sums]
        self.sems = [pltpu.SemaphoreType.DMA((n,)), pltpu.SemaphoreType.DMA((n,))]
        self.stages = [self.begin, self.end]

    def _copies(self, ins, outs, sems, arriving):
        mine = _chip(_peer(0))
        copies = []
        for t in range(len(ins)):
            rows = ins[t].shape[1] // self.pieces
            for k in (0, 4, 2, 6):
                other = _chip(_peer(k))
                for q in range(self.pieces):
                    part = pl.ds(q * rows, rows)
                    at = len(copies)
                    if k == 0:
                        cp = pltpu.make_async_copy(ins[t].at[mine, part, :], outs[t].at[mine, part, :], sems[0].at[at])
                    else:
                        landing = outs[t].at[other if arriving else mine, part, :]
                        cp = _remote(ins[t].at[other, part, :], landing, sems[0].at[at], sems[1].at[at], _peer(k))
                    copies.append(cp)
        return copies

    def begin(self, ins, outs, sems):
        for cp in self._copies(ins, outs, sems, False):
            cp.start()

    def end(self, ins, outs, sems):
        for at, cp in enumerate(self._copies(ins, outs, sems, True)):
            if at % (NCHIP * self.pieces) < self.pieces:
                cp.wait()
            else:
                cp.wait_recv()
                cp.wait_send()


class _CastRows:
    peers = ()

    def __init__(self, arrays):
        n = len(arrays)
        self.operands = list(arrays)
        self.out_shape = [jax.ShapeDtypeStruct(a.shape, BF16) for a in arrays]
        self.sems = [pltpu.SemaphoreType.DMA((n,)), pltpu.SemaphoreType.DMA((n,))]
        self.sems += [pltpu.VMEM(a.shape, F32) for a in arrays] + [pltpu.VMEM(a.shape, BF16) for a in arrays]
        self.stages = [self.begin, self.convert, self.end]

    def _moves(self, t, ins, outs, scratch):
        n = len(ins)
        load = pltpu.make_async_copy(ins[t], scratch[2 + t], scratch[0].at[t])
        store = pltpu.make_async_copy(scratch[2 + n + t], outs[t], scratch[1].at[t])
        return load, store

    def begin(self, ins, outs, scratch):
        for t in range(len(ins)):
            self._moves(t, ins, outs, scratch)[0].start()

    def convert(self, ins, outs, scratch):
        n = len(ins)
        for t in range(n):
            load, store = self._moves(t, ins, outs, scratch)
            load.wait()
            scratch[2 + n + t][...] = scratch[2 + t][...].astype(BF16)
            store.start()

    def end(self, ins, outs, scratch):
        for t in range(len(ins)):
            self._moves(t, ins, outs, scratch)[1].wait()


def _split_refs(refs, counts):
    out, at = [], 0
    for n in counts:
        out.append(refs[at:at + n])
        at += n
    return out


BARRIER_IDS = {(2, 4, 6): 0, (1, 2, 4): 1, (1,): 2, (1, 2, 4, 6): 3, tuple(range(1, NDEV)): 4}


def _peers_of(carries, own=()):
    peers = tuple(sorted(set(own).union(*[c.peers for c in carries])))
    return (peers, BARRIER_IDS[peers]) if peers in BARRIER_IDS else (None, None)


def _handshake(peers):
    barrier = pltpu.get_barrier_semaphore()
    for k in peers:
        pl.semaphore_signal(barrier, inc=1, device_id=_peer(k), device_id_type=pl.DeviceIdType.MESH)
    pl.semaphore_wait(barrier, len(peers))


def _comm_call(carries, name):
    nin = [len(c.operands) for c in carries]
    nout = [len(c.out_shape) for c in carries]
    nsem = [len(c.sems) for c in carries]
    peers, collective_id = _peers_of(carries)

    def body(*refs):
        if peers:
            _handshake(peers)
        ins, outs, sems = _split_refs(refs, (sum(nin), sum(nout), sum(nsem)))
        parts = list(zip(carries, _split_refs(ins, nin), _split_refs(outs, nout), _split_refs(sems, nsem)))
        for depth in range(max(len(c.stages) for c in carries)):
            for c, i, o, s in parts:
                if depth < len(c.stages) - 1:
                    c.stages[depth](i, o, s)
        for c, i, o, s in parts:
            c.stages[-1](i, o, s)

    res = _call(
        body,
        name=name,
        out_shape=[sh for c in carries for sh in c.out_shape],
        in_specs=[ANY] * sum(nin),
        out_specs=[ANY] * sum(nout),
        scratch_shapes=[sm for c in carries for sm in c.sems],
        compiler_params=_params(has_side_effects=True, collective_id=collective_id),
    )(*[a for c in carries for a in c.operands])
    return _split_refs(list(res), nout)


def _grid_call(body, carries, *, name, grid, in_specs, out_specs, out_shape, scratch_shapes, args, own_peers=()):
    ni, no, ns = len(in_specs), len(out_specs), len(scratch_shapes)
    nin = [len(c.operands) for c in carries]
    nout = [len(c.out_shape) for c in carries]
    nsem = [len(c.sems) for c in carries]
    steps = int(np.prod(grid))
    peers, collective_id = _peers_of(carries, own_peers)

    def when_of(stage, count):
        first, last = (5 * steps) // 8 - 1, steps - 2
        return max(0, last if count <= 3 else first + (last - first) * (stage - 1) // (count - 3))

    def wrapped(*refs):
        ins, cins, outs, couts, scr, csems = _split_refs(refs, (ni, sum(nin), no, sum(nout), ns, sum(nsem)))
        if not carries and not peers:
            return body(*ins, *outs, *scr)
        parts = list(zip(carries, _split_refs(cins, nin), _split_refs(couts, nout), _split_refs(csems, nsem)))
        step = pl.program_id(0)
        for axis in range(1, len(grid)):
            step = step * grid[axis] + pl.program_id(axis)

        @pl.when(step == 0)
        def _():
            if peers:
                _handshake(peers)
            for c, i, o, s in parts:
                c.stages[0](i, o, s)

        body(*ins, *outs, *scr)

        for c, i, o, s in parts:
            for stage in range(1, len(c.stages) - 1):
                pl.when(step == when_of(stage, len(c.stages)))(functools.partial(c.stages[stage], i, o, s))

        @pl.when(step == steps - 1)
        def _():
            for c, i, o, s in parts:
                c.stages[-1](i, o, s)

    res = _call(
        wrapped,
        name=name,
        grid=tuple(grid),
        in_specs=list(in_specs) + [ANY] * sum(nin),
        out_specs=list(out_specs) + [ANY] * sum(nout),
        out_shape=list(out_shape) + [sh for c in carries for sh in c.out_shape],
        scratch_shapes=list(scratch_shapes) + [sm for c in carries for sm in c.sems],
        compiler_params=_params(dimension_semantics=("arbitrary",) * len(grid), collective_id=collective_id),
    )(*args, *[a for c in carries for a in c.operands])
    res = list(res)
    return res[:no], _split_refs(res[no:], nout)


def _chunks(width):
    return [(at, min(FFN_CHUNK, width - at)) for at in range(0, width, FFN_CHUNK)]


def _start_chunk_loads(bufs, dsts, sems, chunks):
    copies = [[None] * len(chunks) for _ in bufs]
    for c, (at, width) in enumerate(chunks):
        for m, (buf, dst) in enumerate(zip(bufs, dsts)):
            cp = pltpu.make_async_copy(
                buf.at[pl.ds(at, width), :], dst.at[pl.ds(at, width), :], sems.at[m * len(chunks) + c]
            )
            cp.start()
            copies[m][c] = cp
    return copies


def _ffn_fwd(x, gain, weights, ffn, head=None, carries=()):
    s, d = x.shape
    tm = min(512, s)
    chunks = _chunks(ffn)

    def body(*refs):
        if head is None:
            x_ref, g_ref, b1, b3, b2, h_ref, a_ref, b_ref, hm_ref, w1s, w3s, w2s, sems = refs
        else:
            x_ref, g_ref, b1, b3, b2, gf_ref, t_ref, h_ref, a_ref, b_ref, hm_ref, dgf_ref, loss_ref, w1s, w3s, w2s, sems = refs

        def one_tile(loads):
            xv = x_ref[...]
            r = lax.rsqrt(jnp.mean(xv * xv, axis=-1, keepdims=True) + EPS)
            n = (xv * r * g_ref[...]).astype(BF16)
            acc = jnp.zeros((tm, d), F32)
            for c, (at, width) in enumerate(chunks):
                cols = slice(at, at + width)
                if loads is not None:
                    for m in range(3):
                        loads[m][c].wait()
                a = _dot(n, w1s[cols, :], NT)
                b = _dot(n, w3s[cols, :], NT)
                a_ref[:, cols] = a.astype(BF16)
                b_ref[:, cols] = b.astype(BF16)
                hm = (a * _sigmoid(a) * b).astype(BF16)
                hm_ref[:, cols] = hm
                acc = acc + _dot(hm, w2s[cols, :], NN)
            h = xv + 0.5 * acc
            if head is None:
                h_ref[...] = h
            else:
                rf = lax.rsqrt(jnp.mean(h * h, axis=-1, keepdims=True) + EPS)
                nh = h * rf
                gf = gf_ref[...]
                err = nh * gf - t_ref[...]
                loss_ref[...] += jnp.sum(err * err, axis=0, keepdims=True) * (0.5 / d)
                dy = err * (1.0 / d)
                dgf_ref[...] += jnp.sum(dy * nh, axis=0, keepdims=True)
                dn = dy * gf
                h_ref[...] = rf * (dn - nh * jnp.mean(dn * nh, axis=-1, keepdims=True))

        if head is None:
            @pl.when(pl.program_id(0) == 0)
            def _():
                one_tile(_start_chunk_loads((b1, b3, b2), (w1s, w3s, w2s), sems, chunks))

            @pl.when(pl.program_id(0) > 0)
            def _():
                one_tile(None)
        else:
            @pl.when(pl.program_id(0) == 0)
            def _():
                dgf_ref[...] = jnp.zeros_like(dgf_ref)
                loss_ref[...] = jnp.zeros_like(loss_ref)
                for loads in _start_chunk_loads((b1, b3, b2), (w1s, w3s, w2s), sems, chunks):
                    for cp in loads:
                        cp.wait()

            one_tile(None)

    tile = pl.BlockSpec((tm, d), lambda i: (i, 0))
    row = pl.BlockSpec((1, d), lambda i: (0, 0))
    wide = pl.BlockSpec((tm, ffn), lambda i: (i, 0))
    in_specs = [tile, row, ANY, ANY, ANY]
    out_shape = [jax.ShapeDtypeStruct((s, d), F32)] + [jax.ShapeDtypeStruct((s, ffn), BF16)] * 3
    out_specs = [tile, wide, wide, wide]
    args = [x, gain] + [w.reshape(ffn, d) for w in weights]
    if head is not None:
        in_specs += [row, tile]
        args += list(head)
        out_shape += [jax.ShapeDtypeStruct((1, d), F32)] * 2
        out_specs += [row, row]
    return _grid_call(
        body,
        carries,
        name="ffn_fwd_loss" if head is not None else "ffn_fwd",
        grid=(s // tm,),
        in_specs=in_specs,
        out_specs=out_specs,
        out_shape=out_shape,
        scratch_shapes=[pltpu.VMEM((ffn, d), BF16)] * 3 + [pltpu.SemaphoreType.DMA((3 * len(chunks),))],
        args=args,
    )


def _ffn_bwd(dh, x, a, b, gain, weights, ffn, name, carries=()):
    s, d = x.shape
    tm = min(512, s)
    halves = 2
    fh = ffn // halves

    def body(dh_ref, x_ref, a_ref, b_ref, g_ref, b1, b3, b2, dx_ref, da_ref, db_ref, n_ref, dg_ref, w1s, w3s, w2s, sems):
        i, j = pl.program_id(0), pl.program_id(1)

        @pl.when((i == 0) & (j == 0))
        def _():
            for cp in _load_weights(((b1, w1s), (b3, w3s), (b2, w2s)), sems):
                cp.wait()
            dg_ref[...] = jnp.zeros_like(dg_ref)

        @pl.when(j == 0)
        def _():
            dx_ref[...] = jnp.zeros_like(dx_ref)

        dob = (0.5 * dh_ref[...]).astype(BF16)
        chunks = _chunks(fh)

        def dhm_of(k):
            at, width = chunks[k]
            return _dot(dob, w2s[pl.ds(pl.multiple_of(j * fh + at, GROUP), width), :], NT)

        ahead = dhm_of(0)
        for k, (at, width) in enumerate(chunks):
            cols = slice(at, at + width)
            dhm = ahead
            if k + 1 < len(chunks):
                ahead = dhm_of(k + 1)
            for top in range(0, tm, ROW_BAND):
                band = slice(top, top + ROW_BAND)
                av = a_ref[band, cols].astype(F32)
                bv = b_ref[band, cols].astype(F32)
                sg = _sigmoid(av)
                dv = dhm[band]
                da_ref[band, cols] = (dv * bv * (sg * (1.0 + av * (1.0 - sg)))).astype(BF16)
                db_ref[band, cols] = (dv * (av * sg)).astype(BF16)
        half = pl.ds(pl.multiple_of(j * fh, GROUP), fh)
        dx_ref[...] += _dot(da_ref[...], w1s[half, :], NN) + _dot(db_ref[...], w3s[half, :], NN)

        @pl.when(j == halves - 1)
        def _():
            xv = x_ref[...]
            g = g_ref[...]
            r = lax.rsqrt(jnp.mean(xv * xv, axis=-1, keepdims=True) + EPS)
            nh = xv * r
            n_ref[...] = (nh * g).astype(BF16)
            total = dx_ref[...]
            dg_ref[...] += jnp.sum(total * nh, axis=0, keepdims=True)
            dnh = total * g
            dx_ref[...] = dh_ref[...] + r * (dnh - nh * jnp.mean(dnh * nh, axis=-1, keepdims=True))

    tile = pl.BlockSpec((tm, d), lambda i, j: (i, 0))
    row = pl.BlockSpec((1, d), lambda i, j: (0, 0))
    wide = pl.BlockSpec((tm, fh), lambda i, j: (i, j))
    return _grid_call(
        body,
        carries,
        name=name,
        grid=(s // tm, halves),
        in_specs=[tile, tile, wide, wide, row, ANY, ANY, ANY],
        out_specs=[tile, wide, wide, tile, row],
        out_shape=[
            jax.ShapeDtypeStruct((s, d), F32),
            jax.ShapeDtypeStruct((s, ffn), BF16),
            jax.ShapeDtypeStruct((s, ffn), BF16),
            jax.ShapeDtypeStruct((s, d), BF16),
            jax.ShapeDtypeStruct((1, d), F32),
        ],
        scratch_shapes=[pltpu.VMEM((ffn, d), BF16)] * 3 + [pltpu.SemaphoreType.DMA((3 * NDEV,))],
        args=[dh, x, a, b, gain] + list(weights),
    )


SWAP_PIECES = 1


def _wgrad(lhs, rhs, scale, name, carries=()):
    s, m = lhs.shape
    n = rhs.shape[1]
    rs = m // NDEV
    tk = min(1024, s)
    steps = s // tk
    pieces = [(j, at, size) for j in range(2) for at, size in _pieces(rs, SWAP_PIECES)]

    def body(l_ref, r_ref, o_ref, acc, mine, theirs, send_sems, recv_sems):
        h, k = pl.program_id(0), pl.program_id(1)

        @pl.when(k == 0)
        def _():
            acc[...] = _dot(l_ref[...], r_ref[...].astype(BF16), TN)

        @pl.when(k > 0)
        def _():
            acc[...] += _dot(l_ref[...], r_ref[...].astype(BF16), TN)

        def exchange(half):
            c = lax.axis_index("c")
            return [
                _remote(mine.at[half, 1 - c, j, pl.ds(at, size), :], theirs.at[half, j, pl.ds(at, size), :],
                        send_sems.at[half * len(pieces) + q], recv_sems.at[half * len(pieces) + q], _peer(1))
                for q, (j, at, size) in enumerate(pieces)
            ]

        def settle(half):
            for cp in exchange(half):
                cp.wait_recv()
            both = mine[half, lax.axis_index("c")].astype(F32) + theirs[half].astype(F32)
            o_ref[2 * half:2 * half + 2] = both.astype(BF16)
            for cp in exchange(half):
                cp.wait_send()

        for half in range(2):
            @pl.when((h == half) & (k == steps - 1))
            def _():
                for p in range(NCHIP):
                    mine[half, p % 2, p // 2] = (acc[p * rs:(p + 1) * rs, :] * scale).astype(BF16)
                for cp in exchange(half):
                    cp.start()
                if half == 1:
                    settle(0)
                    settle(1)

    (out,), carried = _grid_call(
        body,
        carries,
        name=name,
        grid=(2, steps),
        in_specs=[pl.BlockSpec((tk, m // 2), lambda h, k: (k, h)), pl.BlockSpec((tk, n), lambda h, k: (k, 0))],
        out_specs=[pl.BlockSpec((NCHIP, rs, n), lambda h, k: (0, 0, 0))],
        out_shape=[jax.ShapeDtypeStruct((NCHIP, rs, n), BF16)],
        scratch_shapes=[
            pltpu.VMEM((m // 2, n), F32), pltpu.VMEM((2, 2, 2, rs, n), BF16), pltpu.VMEM((2, 2, rs, n), BF16),
            pltpu.SemaphoreType.DMA((2 * len(pieces),)), pltpu.SemaphoreType.DMA((2 * len(pieces),)),
        ],
        args=[lhs, rhs],
        own_peers=(1,),
    )
    return out, carried


def _mix_constants(s):
    c = GROUP
    lg = np.log1p(-np.exp2(-5.0 - np.arange(RET_HEADS, dtype=np.float32))).astype(np.float32)
    pos = np.arange(c, dtype=np.float32)
    rel = pos[:, None] - pos[None, :]
    decay = np.where(rel[None] >= 0, np.exp(lg[:, None, None] * np.maximum(rel, 0.0)[None]), 0.0).astype(np.float32)
    ktail = np.exp(lg[:, None] * (c - 1 - pos)[None, :]).astype(np.float32)
    qhead = np.exp(lg[:, None] * (pos + 1.0)[None, :]).astype(np.float32)
    chunk_decay = [float(v) for v in np.exp(lg * np.float32(c)).astype(np.float32)]
    ones = np.ones((1, 1, c), np.float32)
    inv_freq = (1.0 / (np.float32(ROPE_BASE) ** (np.arange(0, c, 2, dtype=np.float32) / np.float32(c)))).astype(np.float32)
    ang = (np.arange(s, dtype=np.float32)[:, None] * inv_freq[None, :]).astype(np.float32)
    cos, sin = np.cos(ang).astype(np.float32), np.sin(ang).astype(np.float32)
    return dict(
        decay=jnp.asarray(decay),
        ktail=jnp.asarray(ktail[:, :, None] * ones),
        qhead=jnp.asarray(qhead[:, :, None] * ones),
        chunk_decay=chunk_decay,
        cos=jnp.asarray(np.concatenate([cos, cos], axis=-1)),
        sin=jnp.asarray(np.concatenate([-sin, sin], axis=-1)),
    )


def _rope(t, cos, sin):
    return t * cos + pltpu.roll(t, GROUP // 2, axis=1) * sin


def _rope_bwd(dt, cos, sin):
    return dt * cos + pltpu.roll(dt * sin, GROUP // 2, axis=1)


def _window_sums(ext, w, forward):
    rows = ext.shape[0]
    acc, k = ext, 1
    while k < w:
        acc = acc + pltpu.roll(acc, k if forward else rows - k, axis=0)
        k *= 2
    return acc


def _pool_counts(tile, tm, w):
    t = lax.broadcasted_iota(jnp.int32, (tm, 1), 0) + tile * tm
    return jnp.minimum(t + 1, w).astype(F32)


def _mix_fwd(h1, gain, weights, pool_w, pool_scale, ret_gain, consts, carries=()):
    s, d = h1.shape
    pwid = N_POOL_GROUPS * GROUP
    rwid = RET_HEADS * GROUP
    inw = pwid + 4 * rwid
    tm = min(256, s)
    nck = tm // GROUP
    cd = consts["chunk_decay"]

    def body(h_ref, g_ref, bin_, bout, pw_ref, ps_ref, rg_ref, cos_ref, sin_ref, dec_ref, kt_ref, qh_ref,
             h2_ref, proj_ref, o_ref, rs_ref, wins, wouts, state, carry, mbuf, sems):
        i = pl.program_id(0)

        @pl.when(i == 0)
        def _():
            for cp in _load_weights(((bin_, wins), (bout, wouts)), sems):
                cp.wait()
            state[...] = jnp.zeros_like(state)
            carry[...] = jnp.zeros_like(carry)

        hv = h_ref[...]
        r = lax.rsqrt(jnp.mean(hv * hv, axis=-1, keepdims=True) + EPS)
        u = (hv * r * g_ref[...]).astype(BF16)
        proj_ref[...] = _dot(u, wins[...], NT)

        ext = jnp.concatenate([carry[...], proj_ref[:, 0:pwid]], axis=0)
        carry[...] = proj_ref[tm - MAX_WINDOW:tm, 0:pwid]
        for gi, w in enumerate(POOL_WINDOWS):
            cols = slice(gi * GROUP, (gi + 1) * GROUP)
            xg = ext[:, cols]
            ws = _window_sums(xg, w, True)[MAX_WINDOW:, :]
            pooled = ws / _pool_counts(i, tm, w) - xg[MAX_WINDOW:, :]
            z = _dot(pooled.astype(BF16), pw_ref[gi].astype(BF16), NN)
            mbuf[:, cols] = (z * ps_ref[:, cols]).astype(BF16)

        cos, sin = cos_ref[...], sin_ref[...]
        for h in range(RET_HEADS):
            cq = slice(pwid + h * GROUP, pwid + (h + 1) * GROUP)
            ck = slice(pwid + rwid + h * GROUP, pwid + rwid + (h + 1) * GROUP)
            cv = slice(pwid + 2 * rwid + h * GROUP, pwid + 2 * rwid + (h + 1) * GROUP)
            cg = slice(pwid + 3 * rwid + h * GROUP, pwid + 3 * rwid + (h + 1) * GROUP)
            ch = slice(h * GROUP, (h + 1) * GROUP)
            qr = _rope(proj_ref[:, cq], cos, sin)
            kr = _rope(proj_ref[:, ck], cos, sin) * (GROUP ** -0.5)
            vb = proj_ref[:, cv].astype(BF16)
            for n in range(nck):
                rows = slice(n * GROUP, (n + 1) * GROUP)
                qc, kc, vc = qr[rows], kr[rows], vb[rows]
                rb = state[h]
                rs_ref[n, h] = rb
                p = (_dot(qc.astype(BF16), kc.astype(BF16), NT) * dec_ref[h]).astype(BF16)
                o = _dot(p, vc, NN) + _dot((qc * qh_ref[h]).astype(BF16), rb.astype(BF16), NN)
                state[h] = cd[h] * rb + _dot((kc * kt_ref[h]).astype(BF16), vc, TN)
                o_ref[rows, ch] = o
                on = o * lax.rsqrt(jnp.mean(o * o, axis=-1, keepdims=True) + EPS)
                gv = proj_ref[rows, cg]
                mbuf[rows, pwid + h * GROUP:pwid + (h + 1) * GROUP] = (
                    gv * _sigmoid(gv) * (on * rg_ref[:, ch])
                ).astype(BF16)
        h2_ref[...] = hv + _dot(mbuf[...], wouts[...], NN)

    tile = pl.BlockSpec((tm, d), lambda i: (i, 0))
    full = lambda shape: pl.BlockSpec(shape, lambda i: (0,) * len(shape))
    return _grid_call(
        body,
        carries,
        name="mix_fwd",
        grid=(s // tm,),
        in_specs=[
            tile, full((1, d)), ANY, ANY,
            full((N_POOL_GROUPS, GROUP, GROUP)), full((1, pwid)), full((1, rwid)),
            pl.BlockSpec((tm, GROUP), lambda i: (i, 0)), pl.BlockSpec((tm, GROUP), lambda i: (i, 0)),
            full((RET_HEADS, GROUP, GROUP)), full((RET_HEADS, GROUP, GROUP)), full((RET_HEADS, GROUP, GROUP)),
        ],
        out_specs=[
            tile,
            pl.BlockSpec((tm, inw), lambda i: (i, 0)),
            pl.BlockSpec((tm, rwid), lambda i: (i, 0)),
            pl.BlockSpec((nck, RET_HEADS, GROUP, GROUP), lambda i: (i, 0, 0, 0)),
        ],
        out_shape=[
            jax.ShapeDtypeStruct((s, d), F32),
            jax.ShapeDtypeStruct((s, inw), F32),
            jax.ShapeDtypeStruct((s, rwid), F32),
            jax.ShapeDtypeStruct((s // GROUP, RET_HEADS, GROUP, GROUP), F32),
        ],
        scratch_shapes=[
            pltpu.VMEM((inw, d), BF16), pltpu.VMEM((d, d), BF16),
            pltpu.VMEM((RET_HEADS, GROUP, GROUP), F32), pltpu.VMEM((MAX_WINDOW, pwid), F32),
            pltpu.VMEM((tm, d), BF16), pltpu.SemaphoreType.DMA((2 * NDEV,)),
        ],
        args=[h1, gain, weights[0], weights[1], pool_w, pool_scale, ret_gain,
              consts["cos"], consts["sin"], consts["decay"], consts["ktail"], consts["qhead"]],
    )


def _mix_bwd(dh2, h1, proj, o_saved, rsave, gain, weights, pool_w, pool_scale, ret_gain, consts, carries=()):
    s, d = h1.shape
    pwid = N_POOL_GROUPS * GROUP
    rwid = RET_HEADS * GROUP
    inw = pwid + 4 * rwid
    tm = min(256, s)
    nck = tm // GROUP
    nt = s // tm
    cd = consts["chunk_decay"]
    halo_per_tile = tm // MAX_WINDOW

    def body(dh2_ref, h_ref, proj_ref, halo_ref, o_ref, rs_ref, g_ref, bin_, bout, pw_ref, ps_ref, rg_ref,
             cos_ref, sin_ref, dec_ref, kt_ref, qh_ref,
             dh1_ref, dproj_ref, u_ref, m_ref, dpw_ref, dps_ref, drg_ref, dg_ref,
             wins, wouts, dstate, carry, dm, dpj, sems):
        i = pl.program_id(0)
        tile = nt - 1 - i

        @pl.when(i == 0)
        def _():
            for cp in _load_weights(((bin_, wins), (bout, wouts)), sems):
                cp.wait()
            dstate[...] = jnp.zeros_like(dstate)
            carry[...] = jnp.zeros_like(carry)
            for ref in (dpw_ref, dps_ref, drg_ref, dg_ref):
                ref[...] = jnp.zeros_like(ref)

        dh2v = dh2_ref[...]
        dm[...] = _dot(dh2v.astype(BF16), wouts[...], NT)
        hv = h_ref[...]
        g = g_ref[...]
        r = lax.rsqrt(jnp.mean(hv * hv, axis=-1, keepdims=True) + EPS)
        uh = hv * r
        u_ref[...] = (uh * g).astype(BF16)

        halo = jnp.where(tile == 0, 0.0, halo_ref[...])
        ext = jnp.concatenate([halo, proj_ref[:, 0:pwid]], axis=0)
        next_dpn = carry[...]
        for gi, w in enumerate(POOL_WINDOWS):
            cols = slice(gi * GROUP, (gi + 1) * GROUP)
            xg = ext[:, cols]
            cnt = _pool_counts(tile, tm, w)
            pooled = (_window_sums(xg, w, True)[MAX_WINDOW:, :] / cnt - xg[MAX_WINDOW:, :]).astype(BF16)
            pwb = pw_ref[gi].astype(BF16)
            z = _dot(pooled, pwb, NN)
            scale = ps_ref[:, cols]
            m_ref[:, cols] = (z * scale).astype(BF16)
            da = dm[:, cols]
            dps_ref[:, cols] += jnp.sum(da * z, axis=0, keepdims=True)
            dz = (da * scale).astype(BF16)
            dpw_ref[gi] += _dot(pooled, dz, TN)
            dpl = _dot(dz, pwb, NT)
            dpn = dpl / cnt
            ext2 = jnp.concatenate([dpn, next_dpn[:, cols]], axis=0)
            dpj[:, cols] = (_window_sums(ext2, w, False)[0:tm, :] - dpl).astype(BF16)
            carry[:, cols] = dpn[0:MAX_WINDOW, :]

        cos, sin = cos_ref[...], sin_ref[...]
        for h in range(RET_HEADS):
            cq = slice(pwid + h * GROUP, pwid + (h + 1) * GROUP)
            ck = slice(pwid + rwid + h * GROUP, pwid + rwid + (h + 1) * GROUP)
            cv = slice(pwid + 2 * rwid + h * GROUP, pwid + 2 * rwid + (h + 1) * GROUP)
            cg = slice(pwid + 3 * rwid + h * GROUP, pwid + 3 * rwid + (h + 1) * GROUP)
            ch = slice(h * GROUP, (h + 1) * GROUP)
            qr = _rope(proj_ref[:, cq], cos, sin)
            kr = _rope(proj_ref[:, ck], cos, sin) * (GROUP ** -0.5)
            vb = proj_ref[:, cv].astype(BF16)
            gv = proj_ref[:, cg]
            ov = o_ref[:, ch]
            ro = lax.rsqrt(jnp.mean(ov * ov, axis=-1, keepdims=True) + EPS)
            on = ov * ro
            rg = rg_ref[:, ch]
            db = dm[:, pwid + h * GROUP:pwid + (h + 1) * GROUP]
            sg = _sigmoid(gv)
            sl = gv * sg
            m_ref[:, pwid + h * GROUP:pwid + (h + 1) * GROUP] = (sl * (on * rg)).astype(BF16)
            dpj[:, cg] = (db * (on * rg) * (sg * (1.0 + gv * (1.0 - sg)))).astype(BF16)
            drg_ref[:, ch] += jnp.sum(db * sl * on, axis=0, keepdims=True)
            don = db * sl * rg
            do = (ro * (don - on * jnp.mean(don * on, axis=-1, keepdims=True))).astype(BF16)
            for n in reversed(range(nck)):
                rows = slice(n * GROUP, (n + 1) * GROUP)
                qc, kc, vc, dob = qr[rows], kr[rows], vb[rows], do[rows]
                qcb, kcb = qc.astype(BF16), kc.astype(BF16)
                qh = (qc * qh_ref[h]).astype(BF16)
                kt = (kc * kt_ref[h]).astype(BF16)
                rn = rs_ref[n, h].astype(BF16)
                dnext = dstate[h]
                dnb = dnext.astype(BF16)
                dec = dec_ref[h]
                p = (_dot(qcb, kcb, NT) * dec).astype(BF16)
                ds = (_dot(dob, vc, NT) * dec).astype(BF16)
                dv = _dot(p, dob, TN) + _dot(kt, dnb, NN)
                dq = _dot(ds, kcb, NN) + _dot(dob, rn, NT) * qh_ref[h]
                dk = _dot(ds, qcb, TN) + _dot(vc, dnb, NT) * kt_ref[h]
                dstate[h] = cd[h] * dnext + _dot(qh, dob, TN)
                dpj[rows, cq] = _rope_bwd(dq, cos[rows], sin[rows]).astype(BF16)
                dpj[rows, ck] = _rope_bwd(dk * (GROUP ** -0.5), cos[rows], sin[rows]).astype(BF16)
                dpj[rows, cv] = dv.astype(BF16)

        dproj_ref[...] = dpj[...]
        du = _dot(dpj[...], wins[...], NN)
        dg_ref[...] += jnp.sum(du * uh, axis=0, keepdims=True)
        dn = du * g
        dh1_ref[...] = dh2v + r * (dn - uh * jnp.mean(dn * uh, axis=-1, keepdims=True))

    rev = lambda i: (nt - 1 - i, 0)
    tile = pl.BlockSpec((tm, d), rev)
    full = lambda shape: pl.BlockSpec(shape, lambda i: (0,) * len(shape))
    return _grid_call(
        body,
        carries,
        name="mix_bwd",
        grid=(nt,),
        in_specs=[
            tile, tile,
            pl.BlockSpec((tm, inw), rev),
            pl.BlockSpec((MAX_WINDOW, pwid), lambda i: (jnp.maximum((nt - 1 - i) * halo_per_tile - 1, 0), 0)),
            pl.BlockSpec((tm, rwid), rev),
            pl.BlockSpec((nck, RET_HEADS, GROUP, GROUP), lambda i: (nt - 1 - i, 0, 0, 0)),
            full((1, d)), ANY, ANY,
            full((N_POOL_GROUPS, GROUP, GROUP)), full((1, pwid)), full((1, rwid)),
            pl.BlockSpec((tm, GROUP), rev), pl.BlockSpec((tm, GROUP), rev),
            full((RET_HEADS, GROUP, GROUP)), full((RET_HEADS, GROUP, GROUP)), full((RET_HEADS, GROUP, GROUP)),
        ],
        out_specs=[
            tile, pl.BlockSpec((tm, inw), rev), tile, tile,
            full((N_POOL_GROUPS, GROUP, GROUP)), full((1, pwid)), full((1, rwid)), full((1, d)),
        ],
        out_shape=[
            jax.ShapeDtypeStruct((s, d), F32),
            jax.ShapeDtypeStruct((s, inw), BF16),
            jax.ShapeDtypeStruct((s, d), BF16),
            jax.ShapeDtypeStruct((s, d), BF16),
            jax.ShapeDtypeStruct((N_POOL_GROUPS, GROUP, GROUP), F32),
            jax.ShapeDtypeStruct((1, pwid), F32),
            jax.ShapeDtypeStruct((1, rwid), F32),
            jax.ShapeDtypeStruct((1, d), F32),
        ],
        scratch_shapes=[
            pltpu.VMEM((inw, d), BF16), pltpu.VMEM((d, d), BF16),
            pltpu.VMEM((RET_HEADS, GROUP, GROUP), F32), pltpu.VMEM((MAX_WINDOW, pwid), F32),
            pltpu.VMEM((tm, d), F32), pltpu.VMEM((tm, inw), BF16), pltpu.SemaphoreType.DMA((2 * NDEV,)),
        ],
        args=[dh2, h1, proj, proj, o_saved, rsave, gain, weights[0], weights[1], pool_w, pool_scale, ret_gain,
              consts["cos"], consts["sin"], consts["decay"], consts["ktail"], consts["qhead"]],
    )


def _adam(w, g, m, v):
    m = ADAM_B1 * m + (1.0 - ADAM_B1) * g
    v = ADAM_B2 * v + (1.0 - ADAM_B2) * jnp.square(g)
    m_hat = m / (1.0 - ADAM_B1 ** ADAM_STEP)
    v_hat = v / (1.0 - ADAM_B2 ** ADAM_STEP)
    delta = -ADAM_LR * (m_hat / (jnp.sqrt(v_hat) + ADAM_EPS) + ADAM_WD * w)
    return delta, m, v


def _adamw_big(w, parts, m, v, name):
    rows, d = w.shape
    tr = _row_tile(rows, 176)

    def body(w_ref, p_ref, m_ref, v_ref, g_ref, d_ref, nm_ref, nv_ref):
        g = p_ref[0].astype(F32)
        for q in range(1, NCHIP):
            g = g + p_ref[q].astype(F32)
        g_ref[...] = g
        d_ref[...], nm_ref[...], nv_ref[...] = _adam(w_ref[...], g, m_ref[...], v_ref[...])

    spec = pl.BlockSpec((tr, d), lambda i: (i, 0))
    return _call(
        body,
        name=name,
        grid=(rows // tr,),
        in_specs=[spec, pl.BlockSpec((NCHIP, tr, d), lambda i: (0, i, 0)), spec, spec],
        out_specs=[spec] * 4,
        out_shape=[jax.ShapeDtypeStruct((rows, d), F32)] * 4,
        compiler_params=_seq(1),
    )(w, parts, m, v)


def _adamw_small(stats_all, pw_all, ws, ms, vs, pwid):
    nsmall = len(ws)

    def body(*refs):
        st_ref, pwa_ref = refs[0], refs[1]
        w_refs = refs[2:2 + nsmall]
        m_refs = refs[2 + nsmall:2 + 2 * nsmall]
        v_refs = refs[2 + 2 * nsmall:2 + 3 * nsmall]
        outs = refs[2 + 3 * nsmall:]
        st = st_ref[0]
        pwg = pwa_ref[0]
        for q in range(1, NDEV):
            st = st + st_ref[q]
            pwg = pwg + pwa_ref[q]
        grads = [st[0:1, :], st[1:2, :], st[2:3, :], st[3:4, :], st[4:5, 0:pwid], st[4:5, pwid:2 * pwid], pwg]
        outs[0][...] = jnp.zeros((1, GROUP), F32) + jnp.sum(st[5:6, :])
        for j in range(nsmall):
            delta, nm, nv = _adam(w_refs[j][...], grads[j], m_refs[j][...], v_refs[j][...])
            outs[1 + 4 * j][...] = grads[j]
            outs[2 + 4 * j][...] = delta
            outs[3 + 4 * j][...] = nm
            outs[4 + 4 * j][...] = nv

    out_shape = [jax.ShapeDtypeStruct((1, GROUP), F32)]
    for w in ws:
        out_shape += [jax.ShapeDtypeStruct(w.shape, F32)] * 4
    return _call(body, name="adamw_small", out_shape=out_shape, compiler_params=_params())(
        stats_all, pw_all, *ws, *ms, *vs
    )


def kernel(x, ffn1_norm, ffn1_w1, ffn1_w3, ffn1_w2, mix_norm, w_in, pool_w, pool_scale, ret_norm, w_out, ffn2_norm, ffn2_w1, ffn2_w3, ffn2_w2, final_norm, loss_target, m_ffn1_norm, m_ffn1_w1, m_ffn1_w3, m_ffn1_w2, m_mix_norm, m_w_in, m_pool_w, m_pool_scale, m_ret_norm, m_w_out, m_ffn2_norm, m_ffn2_w1, m_ffn2_w3, m_ffn2_w2, m_final_norm, v_ffn1_norm, v_ffn1_w1, v_ffn1_w3, v_ffn1_w2, v_mix_norm, v_w_in, v_pool_w, v_pool_scale, v_ret_norm, v_w_out, v_ffn2_norm, v_ffn2_w1, v_ffn2_w3, v_ffn2_w2, v_final_norm):
    s, d = x.shape[1], x.shape[2]
    ffn = ffn1_w1.shape[2] * NDEV
    pwid = pool_scale.shape[1]
    xs, tgt = x[0], loss_target[0]
    consts = _mix_constants(s)
    pw3 = pool_w[0]
    fnorm = final_norm.reshape(1, d)

    rows_of = lambda w, transposed: w[0].T if transposed else w[0]
    send_f1 = [rows_of(w, t).astype(BF16) for w, t in ((ffn1_w1, True), (ffn1_w3, True), (ffn1_w2, False))]
    later = [rows_of(w, t) for w, t in ((w_in, True), (w_out, False), (ffn2_w1, True), (ffn2_w3, True), (ffn2_w2, False))]

    sent_later, w_f1 = _comm_call([_CastRows(later), _Gather(send_f1)], "gather_ffn1")
    send_mix, send_f2 = sent_later[:2], sent_later[2:]
    (h1, a1, b1, hm1), (more,) = _ffn_fwd(xs, ffn1_norm, w_f1, ffn, carries=[_Gather(send_mix + send_f2[:1])])
    w_mix = more[:2]
    (h2, proj, o_saved, rsave), (rest,) = _mix_fwd(
        h1, mix_norm, w_mix, pw3, pool_scale, ret_norm, consts, carries=[_Gather(send_f2[1:])]
    )
    w_f2 = more[2:] + rest
    (dh3, a2, b2, hm2, dgf, loss_cols), _ = _ffn_fwd(h2, ffn2_norm, w_f2, ffn, head=(fnorm, tgt))

    (dh2, da2, db2, n2, dg2), _ = _ffn_bwd(dh3, h2, a2, b2, ffn2_norm, w_f2, ffn, "ffn2_bwd")
    sum_f2w1, _ = _wgrad(da2, n2, 1.0, "ffn2_w1_grad")
    sum_f2w3, _ = _wgrad(db2, n2, 1.0, "ffn2_w3_grad")
    sum_f2w2, ((parts_f2w1,),) = _wgrad(hm2, dh3, 0.5, "ffn2_w2_grad", carries=[_ChipScatter([sum_f2w1])])

    (dh1, dproj, u, mm, dpw, dps, drg, dgm), ((parts_f2w3, parts_f2w2),) = _mix_bwd(
        dh2, h1, proj, o_saved, rsave, mix_norm, w_mix, pw3, pool_scale, ret_norm, consts,
        carries=[_ChipScatter([sum_f2w3, sum_f2w2])],
    )
    (dx, da1, db1, n1, dg1), _ = _ffn_bwd(dh1, xs, a1, b1, ffn1_norm, w_f1, ffn, "ffn1_bwd")
    stats = jnp.concatenate(
        [dg1, dgm, dg2, dgf, jnp.concatenate([dps, drg], axis=1), loss_cols, jnp.zeros((2, d), F32)], axis=0
    )
    small = _GatherDirect([stats, dpw.reshape(N_POOL_GROUPS * GROUP, GROUP)])
    sum_f1w2, ((stats_all, pw_all),) = _wgrad(hm1, dh1, 0.5, "ffn1_w2_grad", carries=[small])
    sum_f1w1, ((parts_f1w2,),) = _wgrad(da1, n1, 1.0, "ffn1_w1_grad", carries=[_ChipScatter([sum_f1w2])])
    sum_f1w3, ((parts_f1w1,),) = _wgrad(db1, n1, 1.0, "ffn1_w3_grad", carries=[_ChipScatter([sum_f1w1])])
    sum_in, ((parts_f1w3,),) = _wgrad(dproj, u, 1.0, "w_in_grad", carries=[_ChipScatter([sum_f1w3])])
    sum_out, ((parts_in,),) = _wgrad(mm, dh2, 1.0, "w_out_grad", carries=[_ChipScatter([sum_in])])
    ((parts_out,),) = _comm_call([_ChipScatter([sum_out])], "scatter_last")

    big = (
        (ffn1_w1, m_ffn1_w1, v_ffn1_w1, parts_f1w1, True),
        (ffn1_w3, m_ffn1_w3, v_ffn1_w3, parts_f1w3, True),
        (ffn1_w2, m_ffn1_w2, v_ffn1_w2, parts_f1w2, False),
        (w_in, m_w_in, v_w_in, parts_in, True),
        (w_out, m_w_out, v_w_out, parts_out, False),
        (ffn2_w1, m_ffn2_w1, v_ffn2_w1, parts_f2w1, True),
        (ffn2_w3, m_ffn2_w3, v_ffn2_w3, parts_f2w3, True),
        (ffn2_w2, m_ffn2_w2, v_ffn2_w2, parts_f2w2, False),
    )
    big_out = []
    for j, (w, m, v, parts, t) in enumerate(big):
        view = (lambda a: a[0].T) if t else (lambda a: a[0])
        back = (lambda a: a.T[None]) if t else (lambda a: a[None])
        big_out.append([back(a) for a in _adamw_big(view(w), parts, view(m), view(v), "adamw_%d" % j)])

    small_w = (ffn1_norm, mix_norm, ffn2_norm, fnorm, pool_scale, ret_norm, pw3.reshape(-1, GROUP))
    small_m = (m_ffn1_norm, m_mix_norm, m_ffn2_norm, m_final_norm.reshape(1, d), m_pool_scale, m_ret_norm, m_pool_w.reshape(-1, GROUP))
    small_v = (v_ffn1_norm, v_mix_norm, v_ffn2_norm, v_final_norm.reshape(1, d), v_pool_scale, v_ret_norm, v_pool_w.reshape(-1, GROUP))
    res = _adamw_small(stats_all, pw_all, small_w, small_m, small_v, pwid)
    loss = res[0][0, 0]
    small_out = [list(res[1 + 4 * j:5 + 4 * j]) for j in range(len(small_w))]
    small_out[3] = [a.reshape(d) for a in small_out[3]]
    small_out[6] = [a.reshape(pool_w.shape) for a in small_out[6]]

    order = [small_out[0], big_out[0], big_out[1], big_out[2], small_out[1], big_out[3], small_out[6], small_out[4],
             small_out[5], big_out[4], small_out[2], big_out[5], big_out[6], big_out[7], small_out[3]]
    result = [loss, dx[None]]
    for kind in range(4):
        result += [t[kind] for t in order]
    return tuple(result)
```

```python
import functools

import numpy as np
import jax
import jax.numpy as jnp
from jax import lax
from jax.experimental import pallas as pl
from jax.experimental.pallas import tpu as pltpu

F32 = jnp.float32
BF16 = jnp.bfloat16

NDEV = 8
NCHIP = 4
EPS = 1e-6
N_POOL_GROUPS = 4
POOL_WINDOWS = (2, 4, 8, 16)
MAX_WINDOW = 16
GROUP = 128
RET_HEADS = 4
ROPE_BASE = 10000.0
ADAM_LR = 0.001
ADAM_B1 = 0.9
ADAM_B2 = 0.999
ADAM_EPS = 1e-08
ADAM_WD = 0.01
ADAM_STEP = 10

VMEM_LIMIT = 56 * 1024 * 1024
FFN_CHUNK = 256
ROW_BAND = 32

NT = (((1,), (1,)), ((), ()))
NN = (((1,), (0,)), ((), ()))
TN = (((0,), (0,)), ((), ()))

ANY = pl.BlockSpec(memory_space=pl.ANY)


def _dot(a, b, dims):
    return lax.dot_general(a, b, dims, preferred_element_type=F32)


def _call(body, **kw):
    return pl.pallas_call(body, **kw)


def _params(**kw):
    return pltpu.CompilerParams(vmem_limit_bytes=VMEM_LIMIT, **kw)


def _seq(n):
    return _params(dimension_semantics=("arbitrary",) * n)


def _peer(k):
    x, y, c = lax.axis_index("x"), lax.axis_index("y"), lax.axis_index("c")
    return (1 - x if k & 4 else x, 1 - y if k & 2 else y, 1 - c if k & 1 else c)


def _flat(pos):
    return 4 * pos[0] + 2 * pos[1] + pos[2]


def _chip(pos):
    return 2 * pos[0] + pos[1]


def _row_tile(rows, cap):
    return max(t for t in range(16, min(rows, cap) + 1, 16) if rows % t == 0)


def _pieces(rows, n):
    tiles = rows // 16
    cuts = [16 * (tiles * q // n) for q in range(n + 1)]
    return [(a, b - a) for a, b in zip(cuts[:-1], cuts[1:])]


def _load_weights(parts, sems):
    copies = []
    for buf, dst in parts:
        rows = buf.shape[1]
        for p in range(NDEV):
            cp = pltpu.make_async_copy(buf.at[p], dst.at[pl.ds(p * rows, rows), :], sems.at[len(copies)])
            cp.start()
            copies.append(cp)
    return copies


def _sigmoid(a):
    return 1.0 / (1.0 + jnp.exp(-a))


def _remote(src, dst, send_sem, recv_sem, to):
    return pltpu.make_async_remote_copy(
        src_ref=src, dst_ref=dst, send_sem=send_sem, recv_sem=recv_sem, device_id=to, device_id_type=pl.DeviceIdType.MESH
    )


class _Gather:
    X, Y, FAR = 4, 2, 6
    peers = (1, 2, 4)
    COPIES = 8

    def __init__(self, shards):
        n = len(shards)
        self.operands = list(shards)
        self.out_shape = [jax.ShapeDtypeStruct((NDEV,) + a.shape, a.dtype) for a in shards]
        self.sems = [
            pltpu.SemaphoreType.DMA((self.COPIES * n,)), pltpu.SemaphoreType.DMA((self.COPIES * n,)),
            pltpu.SemaphoreType.DMA((n,)),
        ]
        self.stages = [self.begin, self.relay, self.relay_far, self.end]

    def _copy(self, t, k, block, to, ins, outs, sems, own=False, half=None):
        rows = outs[t].shape[1]
        part = pl.ds(0, rows) if half is None else pl.ds(half * (rows // 2), rows // 2)
        dst = outs[t].at[_flat(block), part, :]
        at = self.COPIES * t + k
        return _remote(ins[t] if own else dst, dst, sems[0].at[at], sems[1].at[at], to)

    def _local(self, t, ins, outs, sems):
        return pltpu.make_async_copy(ins[t], outs[t].at[_flat(_peer(0))], sems[2].at[t])

    def begin(self, ins, outs, sems):
        me = _peer(0)
        for t in range(len(ins)):
            self._local(t, ins, outs, sems).start()
            for k, code in enumerate((1, self.X, self.Y)):
                self._copy(t, k, me, _peer(code), ins, outs, sems, own=True).start()

    def relay(self, ins, outs, sems):
        me, sibling = _peer(0), _peer(1)
        for t in range(len(ins)):
            self._copy(t, 1, _peer(self.X), me, ins, outs, sems).wait_recv()
            self._copy(t, 3, _peer(self.X), _peer(self.Y), ins, outs, sems, half=0).start()
            self._copy(t, 5, _peer(self.X), sibling, ins, outs, sems).start()
            self._copy(t, 2, _peer(self.Y), me, ins, outs, sems).wait_recv()
            self._copy(t, 4, _peer(self.Y), _peer(self.X), ins, outs, sems, half=1).start()
            self._copy(t, 6, _peer(self.Y), sibling, ins, outs, sems).start()

    def relay_far(self, ins, outs, sems):
        me, sibling = _peer(0), _peer(1)
        for t in range(len(ins)):
            self._copy(t, 3, _peer(self.FAR), me, ins, outs, sems, half=0).wait_recv()
            self._copy(t, 4, _peer(self.FAR), me, ins, outs, sems, half=1).wait_recv()
            self._copy(t, 7, _peer(self.FAR), sibling, ins, outs, sems).start()

    def end(self, ins, outs, sems):
        me = _peer(0)
        for t in range(len(ins)):
            self._copy(t, 0, _peer(1), me, ins, outs, sems).wait_recv()
            for k, code in ((5, self.X), (6, self.Y), (7, self.FAR)):
                self._copy(t, k, _peer(code ^ 1), me, ins, outs, sems).wait_recv()
            for k in range(self.COPIES):
                self._copy(t, k, me, me, ins, outs, sems, half=0 if k == 3 else 1 if k == 4 else None).wait_send()
            self._local(t, ins, outs, sems).wait()


class _GatherDirect:
    peers = tuple(range(1, NDEV))

    def __init__(self, arrays):
        n = len(arrays)
        self.operands = list(arrays)
        self.out_shape = [jax.ShapeDtypeStruct((NDEV,) + a.shape, a.dtype) for a in arrays]
        self.sems = [pltpu.SemaphoreType.DMA((7 * n,)), pltpu.SemaphoreType.DMA((7 * n,)), pltpu.SemaphoreType.DMA((n,))]
        self.stages = [self.begin, self.end]

    def begin(self, ins, outs, sems):
        mine = _flat(_peer(0))
        for t in range(len(ins)):
            pltpu.make_async_copy(ins[t], outs[t].at[mine], sems[2].at[t]).start()
            for k in range(1, NDEV):
                _remote(ins[t], outs[t].at[mine], sems[0].at[7 * t + k - 1], sems[1].at[7 * t + k - 1], _peer(k)).start()

    def end(self, ins, outs, sems):
        mine = _flat(_peer(0))
        for t in range(len(ins)):
            for k in range(1, NDEV):
                cp = _remote(ins[t], outs[t].at[_flat(_peer(k))], sems[0].at[7 * t + k - 1], sems[1].at[7 * t + k - 1], _peer(k))
                cp.wait_recv()
                cp.wait_send()
            pltpu.make_async_copy(ins[t], outs[t].at[mine], sems[2].at[t]).wait()


class _ChipScatter:
    peers = (2, 4, 6)
    pieces = 1

    def __init__(self, sums):
        n = len(sums) * NCHIP * self.pieces
        self.operands = list(sums)
        self.out_shape = [jax.ShapeDtypeStruct(a.shape, a.dtype) for a in sums]
        self.sems = [pltpu.SemaphoreType.DMA((n,)), pltpu.SemaphoreType.DMA((n,))]
        self.stages = [self.begin, self.end]

    def _copies(self, ins, outs, sems, arriving):
        mine = _chip(_peer(0))
        copies = []
        for t in range(len(ins)):
            rows = ins[t].shape[1] // self.pieces
            for k in (0, 4, 2, 6):
                other = _chip(_peer(k))
                for q in range(self.pieces):
                    part = pl.ds(q * rows, rows)
                    at = len(copies)
                    if k == 0:
                        cp = pltpu.make_async_copy(ins[t].at[mine, part, :], outs[t].at[mine, part, :], sems[0].at[at])
                    else:
                        landing = outs[t].at[other if arriving else mine, part, :]
                        cp = _remote(ins[t].at[other, part, :], landing, sems[0].at[at], sems[1].at[at], _peer(k))
                    copies.append(cp)
        return copies

    def begin(self, ins, outs, sems):
        for cp in self._copies(ins, outs, sems, False):
            cp.start()

    def end(self, ins, outs, sems):
        for at, cp in enumerate(self._copies(ins, outs, sems, True)):
            if at % (NCHIP * self.pieces) < self.pieces:
                cp.wait()
            else:
                cp.wait_recv()
                cp.wait_send()


class _CastRows:
    peers = ()

    def __init__(self, arrays):
        n = len(arrays)
        self.operands = list(arrays)
        self.out_shape = [jax.ShapeDtypeStruct(a.shape, BF16) for a in arrays]
        self.sems = [pltpu.SemaphoreType.DMA((n,)), pltpu.SemaphoreType.DMA((n,))]
        self.sems += [pltpu.VMEM(a.shape, F32) for a in arrays] + [pltpu.VMEM(a.shape, BF16) for a in arrays]
        self.stages = [self.begin, self.convert, self.end]

    def _moves(self, t, ins, outs, scratch):
        n = len(ins)
        load = pltpu.make_async_copy(ins[t], scratch[2 + t], scratch[0].at[t])
        store = pltpu.make_async_copy(scratch[2 + n + t], outs[t], scratch[1].at[t])
        return load, store

    def begin(self, ins, outs, scratch):
        for t in range(len(ins)):
            self._moves(t, ins, outs, scratch)[0].start()

    def convert(self, ins, outs, scratch):
        n = len(ins)
        for t in range(n):
            load, store = self._moves(t, ins, outs, scratch)
            load.wait()
            scratch[2 + n + t][...] = scratch[2 + t][...].astype(BF16)
            store.start()

    def end(self, ins, outs, scratch):
        for t in range(len(ins)):
            self._moves(t, ins, outs, scratch)[1].wait()


def _split_refs(refs, counts):
    out, at = [], 0
    for n in counts:
        out.append(refs[at:at + n])
        at += n
    return out


BARRIER_IDS = {(2, 4, 6): 0, (1, 2, 4): 1, (1,): 2, (1, 2, 4, 6): 3, tuple(range(1, NDEV)): 4}


def _peers_of(carries, own=()):
    peers = tuple(sorted(set(own).union(*[c.peers for c in carries])))
    return (peers, BARRIER_IDS[peers]) if peers in BARRIER_IDS else (None, None)


def _handshake(peers):
    barrier = pltpu.get_barrier_semaphore()
    for k in peers:
        pl.semaphore_signal(barrier, inc=1, device_id=_peer(k), device_id_type=pl.DeviceIdType.MESH)
    pl.semaphore_wait(barrier, len(peers))


def _comm_call(carries, name):
    nin = [len(c.operands) for c in carries]
    nout = [len(c.out_shape) for c in carries]
    nsem = [len(c.sems) for c in carries]
    peers, collective_id = _peers_of(carries)

    def body(*refs):
        if peers:
            _handshake(peers)
        ins, outs, sems = _split_refs(refs, (sum(nin), sum(nout), sum(nsem)))
        parts = list(zip(carries, _split_refs(ins, nin), _split_refs(outs, nout), _split_refs(sems, nsem)))
        for depth in range(max(len(c.stages) for c in carries)):
            for c, i, o, s in parts:
                if depth < len(c.stages) - 1:
                    c.stages[depth](i, o, s)
        for c, i, o, s in parts:
            c.stages[-1](i, o, s)

    res = _call(
        body,
        name=name,
        out_shape=[sh for c in carries for sh in c.out_shape],
        in_specs=[ANY] * sum(nin),
        out_specs=[ANY] * sum(nout),
        scratch_shapes=[sm for c in carries for sm in c.sems],
        compiler_params=_params(has_side_effects=True, collective_id=collective_id),
    )(*[a for c in carries for a in c.operands])
    return _split_refs(list(res), nout)


def _grid_call(body, carries, *, name, grid, in_specs, out_specs, out_shape, scratch_shapes, args, own_peers=()):
    ni, no, ns = len(in_specs), len(out_specs), len(scratch_shapes)
    nin = [len(c.operands) for c in carries]
    nout = [len(c.out_shape) for c in carries]
    nsem = [len(c.sems) for c in carries]
    steps = int(np.prod(grid))
    peers, collective_id = _peers_of(carries, own_peers)

    def when_of(stage, count):
        first, last = (5 * steps) // 8 - 1, steps - 2
        return max(0, last if count <= 3 else first + (last - first) * (stage - 1) // (count - 3))

    def wrapped(*refs):
        ins, cins, outs, couts, scr, csems = _split_refs(refs, (ni, sum(nin), no, sum(nout), ns, sum(nsem)))
        if not carries and not peers:
            return body(*ins, *outs, *scr)
        parts = list(zip(carries, _split_refs(cins, nin), _split_refs(couts, nout), _split_refs(csems, nsem)))
        step = pl.program_id(0)
        for axis in range(1, len(grid)):
            step = step * grid[axis] + pl.program_id(axis)

        @pl.when(step == 0)
        def _():
            if peers:
                _handshake(peers)
            for c, i, o, s in parts:
                c.stages[0](i, o, s)

        body(*ins, *outs, *scr)

        for c, i, o, s in parts:
            for stage in range(1, len(c.stages) - 1):
                pl.when(step == when_of(stage, len(c.stages)))(functools.partial(c.stages[stage], i, o, s))

        @pl.when(step == steps - 1)
        def _():
            for c, i, o, s in parts:
                c.stages[-1](i, o, s)

    res = _call(
        wrapped,
        name=name,
        grid=tuple(grid),
        in_specs=list(in_specs) + [ANY] * sum(nin),
        out_specs=list(out_specs) + [ANY] * sum(nout),
        out_shape=list(out_shape) + [sh for c in carries for sh in c.out_shape],
        scratch_shapes=list(scratch_shapes) + [sm for c in carries for sm in c.sems],
        compiler_params=_params(dimension_semantics=("arbitrary",) * len(grid), collective_id=collective_id),
    )(*args, *[a for c in carries for a in c.operands])
    res = list(res)
    return res[:no], _split_refs(res[no:], nout)


def _chunks(width):
    return [(at, min(FFN_CHUNK, width - at)) for at in range(0, width, FFN_CHUNK)]


def _start_chunk_loads(bufs, dsts, sems, chunks):
    copies = [[None] * len(chunks) for _ in bufs]
    for c, (at, width) in enumerate(chunks):
        for m, (buf, dst) in enumerate(zip(bufs, dsts)):
            cp = pltpu.make_async_copy(
                buf.at[pl.ds(at, width), :], dst.at[pl.ds(at, width), :], sems.at[m * len(chunks) + c]
            )
            cp.start()
            copies[m][c] = cp
    return copies


def _ffn_fwd(x, gain, weights, ffn, head=None, carries=()):
    s, d = x.shape
    tm = min(512, s)
    chunks = _chunks(ffn)

    def body(*refs):
        if head is None:
            x_ref, g_ref, b1, b3, b2, h_ref, a_ref, b_ref, hm_ref, w1s, w3s, w2s, sems = refs
        else:
            x_ref, g_ref, b1, b3, b2, gf_ref, t_ref, h_ref, a_ref, b_ref, hm_ref, dgf_ref, loss_ref, w1s, w3s, w2s, sems = refs

        def one_tile(loads):
            xv = x_ref[...]
            r = lax.rsqrt(jnp.mean(xv * xv, axis=-1, keepdims=True) + EPS)
            n = (xv * r * g_ref[...]).astype(BF16)
            acc = jnp.zeros((tm, d), F32)
            for c, (at, width) in enumerate(chunks):
                cols = slice(at, at + width)
                if loads is not None:
                    for m in range(3):
                        loads[m][c].wait()
                a = _dot(n, w1s[cols, :], NT)
                b = _dot(n, w3s[cols, :], NT)
                a_ref[:, cols] = a.astype(BF16)
                b_ref[:, cols] = b.astype(BF16)
                hm = (a * _sigmoid(a) * b).astype(BF16)
                hm_ref[:, cols] = hm
                acc = acc + _dot(hm, w2s[cols, :], NN)
            h = xv + 0.5 * acc
            if head is None:
                h_ref[...] = h
            else:
                rf = lax.rsqrt(jnp.mean(h * h, axis=-1, keepdims=True) + EPS)
                nh = h * rf
                gf = gf_ref[...]
                err = nh * gf - t_ref[...]
                loss_ref[...] += jnp.sum(err * err, axis=0, keepdims=True) * (0.5 / d)
                dy = err * (1.0 / d)
                dgf_ref[...] += jnp.sum(dy * nh, axis=0, keepdims=True)
                dn = dy * gf
                h_ref[...] = rf * (dn - nh * jnp.mean(dn * nh, axis=-1, keepdims=True))

        if head is None:
            @pl.when(pl.program_id(0) == 0)
            def _():
                one_tile(_start_chunk_loads((b1, b3, b2), (w1s, w3s, w2s), sems, chunks))

            @pl.when(pl.program_id(0) > 0)
            def _():
                one_tile(None)
        else:
            @pl.when(pl.program_id(0) == 0)
            def _():
                dgf_ref[...] = jnp.zeros_like(dgf_ref)
                loss_ref[...] = jnp.zeros_like(loss_ref)
                for loads in _start_chunk_loads((b1, b3, b2), (w1s, w3s, w2s), sems, chunks):
                    for cp in loads:
                        cp.wait()

            one_tile(None)

    tile = pl.BlockSpec((tm, d), lambda i: (i, 0))
    row = pl.BlockSpec((1, d), lambda i: (0, 0))
    wide = pl.BlockSpec((tm, ffn), lambda i: (i, 0))
    in_specs = [tile, row, ANY, ANY, ANY]
    out_shape = [jax.ShapeDtypeStruct((s, d), F32)] + [jax.ShapeDtypeStruct((s, ffn), BF16)] * 3
    out_specs = [tile, wide, wide, wide]
    args = [x, gain] + [w.reshape(ffn, d) for w in weights]
    if head is not None:
        in_specs += [row, tile]
        args += list(head)
        out_shape += [jax.ShapeDtypeStruct((1, d), F32)] * 2
        out_specs += [row, row]
    return _grid_call(
        body,
        carries,
        name="ffn_fwd_loss" if head is not None else "ffn_fwd",
        grid=(s // tm,),
        in_specs=in_specs,
        out_specs=out_specs,
        out_shape=out_shape,
        scratch_shapes=[pltpu.VMEM((ffn, d), BF16)] * 3 + [pltpu.SemaphoreType.DMA((3 * len(chunks),))],
        args=args,
    )


def _ffn_up(x, gain, w1, w3, carries=()):
    s, d = x.shape
    ffn = w1.shape[0] * w1.shape[1]
    tm = min(512, s)

    def body(x_ref, g_ref, b1, b3, a_ref, b_ref, hm_ref, w1s, w3s, sems):
        @pl.when(pl.program_id(0) == 0)
        def _():
            for cp in _load_weights(((b1, w1s), (b3, w3s)), sems):
                cp.wait()

        xv = x_ref[...]
        r = lax.rsqrt(jnp.mean(xv * xv, axis=-1, keepdims=True) + EPS)
        n = (xv * r * g_ref[...]).astype(BF16)
        for at, width in _chunks(ffn):
            cols = slice(at, at + width)
            a = _dot(n, w1s[cols, :], NT)
            b = _dot(n, w3s[cols, :], NT)
            a_ref[:, cols] = a.astype(BF16)
            b_ref[:, cols] = b.astype(BF16)
            hm_ref[:, cols] = (a * _sigmoid(a) * b).astype(BF16)

    wide = pl.BlockSpec((tm, ffn), lambda i: (i, 0))
    return _grid_call(
        body,
        carries,
        name="ffn_up",
        grid=(s // tm,),
        in_specs=[pl.BlockSpec((tm, d), lambda i: (i, 0)), pl.BlockSpec((1, d), lambda i: (0, 0)), ANY, ANY],
        out_specs=[wide] * 3,
        out_shape=[jax.ShapeDtypeStruct((s, ffn), BF16)] * 3,
        scratch_shapes=[pltpu.VMEM((ffn, d), BF16)] * 2 + [pltpu.SemaphoreType.DMA((2 * NDEV,))],
        args=[x, gain, w1, w3],
    )


def _ffn_down(x, hm, w2, carries=()):
    s, d = x.shape
    ffn = w2.shape[0] * w2.shape[1]
    tm = min(512, s)

    def body(x_ref, hm_ref, b2, h_ref, w2s, sems):
        @pl.when(pl.program_id(0) == 0)
        def _():
            for cp in _load_weights(((b2, w2s),), sems):
                cp.wait()

        acc = jnp.zeros((tm, d), F32)
        for at, width in _chunks(ffn):
            cols = slice(at, at + width)
            acc = acc + _dot(hm_ref[:, cols], w2s[cols, :], NN)
        h_ref[...] = x_ref[...] + 0.5 * acc

    tile = pl.BlockSpec((tm, d), lambda i: (i, 0))
    return _grid_call(
        body,
        carries,
        name="ffn_down",
        grid=(s // tm,),
        in_specs=[tile, pl.BlockSpec((tm, ffn), lambda i: (i, 0)), ANY],
        out_specs=[tile],
        out_shape=[jax.ShapeDtypeStruct((s, d), F32)],
        scratch_shapes=[pltpu.VMEM((ffn, d), BF16), pltpu.SemaphoreType.DMA((NDEV,))],
        args=[x, hm, w2],
    )


def _ffn_bwd(dh, x, a, b, gain, weights, ffn, name, carries=()):
    s, d = x.shape
    tm = min(512, s)
    halves = 2
    fh = ffn // halves

    def body(dh_ref, x_ref, a_ref, b_ref, g_ref, b1, b3, b2, dx_ref, da_ref, db_ref, n_ref, dg_ref, w1s, w3s, w2s, sems):
        i, j = pl.program_id(0), pl.program_id(1)

        @pl.when((i == 0) & (j == 0))
        def _():
            for cp in _load_weights(((b1, w1s), (b3, w3s), (b2, w2s)), sems):
                cp.wait()
            dg_ref[...] = jnp.zeros_like(dg_ref)

        @pl.when(j == 0)
        def _():
            dx_ref[...] = jnp.zeros_like(dx_ref)

        dob = (0.5 * dh_ref[...]).astype(BF16)
        chunks = _chunks(fh)

        def dhm_of(k):
            at, width = chunks[k]
            return _dot(dob, w2s[pl.ds(pl.multiple_of(j * fh + at, GROUP), width), :], NT)

        ahead = dhm_of(0)
        for k, (at, width) in enumerate(chunks):
            cols = slice(at, at + width)
            dhm = ahead
            if k + 1 < len(chunks):
                ahead = dhm_of(k + 1)
            for top in range(0, tm, ROW_BAND):
                band = slice(top, top + ROW_BAND)
                av = a_ref[band, cols].astype(F32)
                bv = b_ref[band, cols].astype(F32)
                sg = _sigmoid(av)
                dv = dhm[band]
                da_ref[band, cols] = (dv * bv * (sg * (1.0 + av * (1.0 - sg)))).astype(BF16)
                db_ref[band, cols] = (dv * (av * sg)).astype(BF16)
        half = pl.ds(pl.multiple_of(j * fh, GROUP), fh)
        dx_ref[...] += _dot(da_ref[...], w1s[half, :], NN) + _dot(db_ref[...], w3s[half, :], NN)

        @pl.when(j == halves - 1)
        def _():
            xv = x_ref[...]
            g = g_ref[...]
            r = lax.rsqrt(jnp.mean(xv * xv, axis=-1, keepdims=True) + EPS)
            nh = xv * r
            n_ref[...] = (nh * g).astype(BF16)
            total = dx_ref[...]
            dg_ref[...] += jnp.sum(total * nh, axis=0, keepdims=True)
            dnh = total * g
            dx_ref[...] = dh_ref[...] + r * (dnh - nh * jnp.mean(dnh * nh, axis=-1, keepdims=True))

    tile = pl.BlockSpec((tm, d), lambda i, j: (i, 0))
    row = pl.BlockSpec((1, d), lambda i, j: (0, 0))
    wide = pl.BlockSpec((tm, fh), lambda i, j: (i, j))
    return _grid_call(
        body,
        carries,
        name=name,
        grid=(s // tm, halves),
        in_specs=[tile, tile, wide, wide, row, ANY, ANY, ANY],
        out_specs=[tile, wide, wide, tile, row],
        out_shape=[
            jax.ShapeDtypeStruct((s, d), F32),
            jax.ShapeDtypeStruct((s, ffn), BF16),
            jax.ShapeDtypeStruct((s, ffn), BF16),
            jax.ShapeDtypeStruct((s, d), BF16),
            jax.ShapeDtypeStruct((1, d), F32),
        ],
        scratch_shapes=[pltpu.VMEM((ffn, d), BF16)] * 3 + [pltpu.SemaphoreType.DMA((3 * NDEV,))],
        args=[dh, x, a, b, gain] + list(weights),
    )


SWAP_PIECES = 1


def _wgrad(lhs, rhs, scale, name, carries=()):
    s, m = lhs.shape
    n = rhs.shape[1]
    rs = m // NDEV
    tk = min(1024, s)
    steps = s // tk
    pieces = [(j, at, size) for j in range(2) for at, size in _pieces(rs, SWAP_PIECES)]

    def body(l_ref, r_ref, o_ref, acc, mine, theirs, send_sems, recv_sems):
        h, k = pl.program_id(0), pl.program_id(1)

        @pl.when(k == 0)
        def _():
            acc[...] = _dot(l_ref[...], r_ref[...].astype(BF16), TN)

        @pl.when(k > 0)
        def _():
            acc[...] += _dot(l_ref[...], r_ref[...].astype(BF16), TN)

        def exchange(half):
            c = lax.axis_index("c")
            return [
                _remote(mine.at[half, 1 - c, j, pl.ds(at, size), :], theirs.at[half, j, pl.ds(at, size), :],
                        send_sems.at[half * len(pieces) + q], recv_sems.at[half * len(pieces) + q], _peer(1))
                for q, (j, at, size) in enumerate(pieces)
            ]

        def settle(half):
            for cp in exchange(half):
                cp.wait_recv()
            both = mine[half, lax.axis_index("c")].astype(F32) + theirs[half].astype(F32)
            o_ref[2 * half:2 * half + 2] = both.astype(BF16)
            for cp in exchange(half):
                cp.wait_send()

        for half in range(2):
            @pl.when((h == half) & (k == steps - 1))
            def _():
                for p in range(NCHIP):
                    mine[half, p % 2, p // 2] = (acc[p * rs:(p + 1) * rs, :] * scale).astype(BF16)
                for cp in exchange(half):
                    cp.start()
                if half == 1:
                    settle(0)
                    settle(1)

    (out,), carried = _grid_call(
        body,
        carries,
        name=name,
        grid=(2, steps),
        in_specs=[pl.BlockSpec((tk, m // 2), lambda h, k: (k, h)), pl.BlockSpec((tk, n), lambda h, k: (k, 0))],
        out_specs=[pl.BlockSpec((NCHIP, rs, n), lambda h, k: (0, 0, 0))],
        out_shape=[jax.ShapeDtypeStruct((NCHIP, rs, n), BF16)],
        scratch_shapes=[
            pltpu.VMEM((m // 2, n), F32), pltpu.VMEM((2, 2, 2, rs, n), BF16), pltpu.VMEM((2, 2, rs, n), BF16),
            pltpu.SemaphoreType.DMA((2 * len(pieces),)), pltpu.SemaphoreType.DMA((2 * len(pieces),)),
        ],
        args=[lhs, rhs],
        own_peers=(1,),
    )
    return out, carried


def _mix_constants(s):
    c = GROUP
    lg = np.log1p(-np.exp2(-5.0 - np.arange(RET_HEADS, dtype=np.float32))).astype(np.float32)
    pos = np.arange(c, dtype=np.float32)
    rel = pos[:, None] - pos[None, :]
    decay = np.where(rel[None] >= 0, np.exp(lg[:, None, None] * np.maximum(rel, 0.0)[None]), 0.0).astype(np.float32)
    ktail = np.exp(lg[:, None] * (c - 1 - pos)[None, :]).astype(np.float32)
    qhead = np.exp(lg[:, None] * (pos + 1.0)[None, :]).astype(np.float32)
    chunk_decay = [float(v) for v in np.exp(lg * np.float32(c)).astype(np.float32)]
    ones = np.ones((1, 1, c), np.float32)
    inv_freq = (1.0 / (np.float32(ROPE_BASE) ** (np.arange(0, c, 2, dtype=np.float32) / np.float32(c)))).astype(np.float32)
    ang = (np.arange(s, dtype=np.float32)[:, None] * inv_freq[None, :]).astype(np.float32)
    cos, sin = np.cos(ang).astype(np.float32), np.sin(ang).astype(np.float32)
    return dict(
        decay=jnp.asarray(decay),
        ktail=jnp.asarray(ktail[:, :, None] * ones),
        qhead=jnp.asarray(qhead[:, :, None] * ones),
        chunk_decay=chunk_decay,
        cos=jnp.asarray(np.concatenate([cos, cos], axis=-1)),
        sin=jnp.asarray(np.concatenate([-sin, sin], axis=-1)),
    )


def _rope(t, cos, sin):
    return t * cos + pltpu.roll(t, GROUP // 2, axis=1) * sin


def _rope_bwd(dt, cos, sin):
    return dt * cos + pltpu.roll(dt * sin, GROUP // 2, axis=1)


def _window_sums(ext, w, forward):
    rows = ext.shape[0]
    acc, k = ext, 1
    while k < w:
        acc = acc + pltpu.roll(acc, k if forward else rows - k, axis=0)
        k *= 2
    return acc


def _pool_counts(tile, tm, w):
    t = lax.broadcasted_iota(jnp.int32, (tm, 1), 0) + tile * tm
    return jnp.minimum(t + 1, w).astype(F32)


def _mix_fwd(h1, gain, weights, pool_w, pool_scale, ret_gain, consts, carries=()):
    s, d = h1.shape
    pwid = N_POOL_GROUPS * GROUP
    rwid = RET_HEADS * GROUP
    inw = pwid + 4 * rwid
    tm = min(256, s)
    nck = tm // GROUP
    cd = consts["chunk_decay"]

    def body(h_ref, g_ref, bin_, bout, pw_ref, ps_ref, rg_ref, cos_ref, sin_ref, dec_ref, kt_ref, qh_ref,
             h2_ref, proj_ref, o_ref, rs_ref, wins, wouts, state, carry, mbuf, sems):
        i = pl.program_id(0)

        @pl.when(i == 0)
        def _():
            for cp in _load_weights(((bin_, wins), (bout, wouts)), sems):
                cp.wait()
            state[...] = jnp.zeros_like(state)
            carry[...] = jnp.zeros_like(carry)

        hv = h_ref[...]
        r = lax.rsqrt(jnp.mean(hv * hv, axis=-1, keepdims=True) + EPS)
        u = (hv * r * g_ref[...]).astype(BF16)
        proj_ref[...] = _dot(u, wins[...], NT)

        ext = jnp.concatenate([carry[...], proj_ref[:, 0:pwid]], axis=0)
        carry[...] = proj_ref[tm - MAX_WINDOW:tm, 0:pwid]
        for gi, w in enumerate(POOL_WINDOWS):
            cols = slice(gi * GROUP, (gi + 1) * GROUP)
            xg = ext[:, cols]
            ws = _window_sums(xg, w, True)[MAX_WINDOW:, :]
            pooled = ws / _pool_counts(i, tm, w) - xg[MAX_WINDOW:, :]
            z = _dot(pooled.astype(BF16), pw_ref[gi].astype(BF16), NN)
            mbuf[:, cols] = (z * ps_ref[:, cols]).astype(BF16)

        cos, sin = cos_ref[...], sin_ref[...]
        for h in range(RET_HEADS):
            cq = slice(pwid + h * GROUP, pwid + (h + 1) * GROUP)
            ck = slice(pwid + rwid + h * GROUP, pwid + rwid + (h + 1) * GROUP)
            cv = slice(pwid + 2 * rwid + h * GROUP, pwid + 2 * rwid + (h + 1) * GROUP)
            cg = slice(pwid + 3 * rwid + h * GROUP, pwid + 3 * rwid + (h + 1) * GROUP)
            ch = slice(h * GROUP, (h + 1) * GROUP)
            qr = _rope(proj_ref[:, cq], cos, sin)
            kr = _rope(proj_ref[:, ck], cos, sin) * (GROUP ** -0.5)
            vb = proj_ref[:, cv].astype(BF16)
            for n in range(nck):
                rows = slice(n * GROUP, (n + 1) * GROUP)
                qc, kc, vc = qr[rows], kr[rows], vb[rows]
                rb = state[h]
                rs_ref[n, h] = rb
                p = (_dot(qc.astype(BF16), kc.astype(BF16), NT) * dec_ref[h]).astype(BF16)
                o = _dot(p, vc, NN) + _dot((qc * qh_ref[h]).astype(BF16), rb.astype(BF16), NN)
                state[h] = cd[h] * rb + _dot((kc * kt_ref[h]).astype(BF16), vc, TN)
                o_ref[rows, ch] = o
                on = o * lax.rsqrt(jnp.mean(o * o, axis=-1, keepdims=True) + EPS)
                gv = proj_ref[rows, cg]
                mbuf[rows, pwid + h * GROUP:pwid + (h + 1) * GROUP] = (
                    gv * _sigmoid(gv) * (on * rg_ref[:, ch])
                ).astype(BF16)
        h2_ref[...] = hv + _dot(mbuf[...], wouts[...], NN)

    tile = pl.BlockSpec((tm, d), lambda i: (i, 0))
    full = lambda shape: pl.BlockSpec(shape, lambda i: (0,) * len(shape))
    return _grid_call(
        body,
        carries,
        name="mix_fwd",
        grid=(s // tm,),
        in_specs=[
            tile, full((1, d)), ANY, ANY,
            full((N_POOL_GROUPS, GROUP, GROUP)), full((1, pwid)), full((1, rwid)),
            pl.BlockSpec((tm, GROUP), lambda i: (i, 0)), pl.BlockSpec((tm, GROUP), lambda i: (i, 0)),
            full((RET_HEADS, GROUP, GROUP)), full((RET_HEADS, GROUP, GROUP)), full((RET_HEADS, GROUP, GROUP)),
        ],
        out_specs=[
            tile,
            pl.BlockSpec((tm, inw), lambda i: (i, 0)),
            pl.BlockSpec((tm, rwid), lambda i: (i, 0)),
            pl.BlockSpec((nck, RET_HEADS, GROUP, GROUP), lambda i: (i, 0, 0, 0)),
        ],
        out_shape=[
            jax.ShapeDtypeStruct((s, d), F32),
            jax.ShapeDtypeStruct((s, inw), F32),
            jax.ShapeDtypeStruct((s, rwid), F32),
            jax.ShapeDtypeStruct((s // GROUP, RET_HEADS, GROUP, GROUP), F32),
        ],
        scratch_shapes=[
            pltpu.VMEM((inw, d), BF16), pltpu.VMEM((d, d), BF16),
            pltpu.VMEM((RET_HEADS, GROUP, GROUP), F32), pltpu.VMEM((MAX_WINDOW, pwid), F32),
            pltpu.VMEM((tm, d), BF16), pltpu.SemaphoreType.DMA((2 * NDEV,)),
        ],
        args=[h1, gain, weights[0], weights[1], pool_w, pool_scale, ret_gain,
              consts["cos"], consts["sin"], consts["decay"], consts["ktail"], consts["qhead"]],
    )


def _mix_bwd(dh2, h1, proj, o_saved, rsave, gain, weights, pool_w, pool_scale, ret_gain, consts, carries=()):
    s, d = h1.shape
    pwid = N_POOL_GROUPS * GROUP
    rwid = RET_HEADS * GROUP
    inw = pwid + 4 * rwid
    tm = min(256, s)
    nck = tm // GROUP
    nt = s // tm
    cd = consts["chunk_decay"]
    halo_per_tile = tm // MAX_WINDOW

    def body(dh2_ref, h_ref, proj_ref, halo_ref, o_ref, rs_ref, g_ref, bin_, bout, pw_ref, ps_ref, rg_ref,
             cos_ref, sin_ref, dec_ref, kt_ref, qh_ref,
             dh1_ref, dproj_ref, u_ref, m_ref, dpw_ref, dps_ref, drg_ref, dg_ref,
             wins, wouts, dstate, carry, dm, dpj, sems):
        i = pl.program_id(0)
        tile = nt - 1 - i

        @pl.when(i == 0)
        def _():
            for cp in _load_weights(((bin_, wins), (bout, wouts)), sems):
                cp.wait()
            dstate[...] = jnp.zeros_like(dstate)
            carry[...] = jnp.zeros_like(carry)
            for ref in (dpw_ref, dps_ref, drg_ref, dg_ref):
                ref[...] = jnp.zeros_like(ref)

        dh2v = dh2_ref[...]
        dm[...] = _dot(dh2v.astype(BF16), wouts[...], NT)
        hv = h_ref[...]
        g = g_ref[...]
        r = lax.rsqrt(jnp.mean(hv * hv, axis=-1, keepdims=True) + EPS)
        uh = hv * r
        u_ref[...] = (uh * g).astype(BF16)

        halo = jnp.where(tile == 0, 0.0, halo_ref[...])
        ext = jnp.concatenate([halo, proj_ref[:, 0:pwid]], axis=0)
        next_dpn = carry[...]
        for gi, w in enumerate(POOL_WINDOWS):
            cols = slice(gi * GROUP, (gi + 1) * GROUP)
            xg = ext[:, cols]
            cnt = _pool_counts(tile, tm, w)
            pooled = (_window_sums(xg, w, True)[MAX_WINDOW:, :] / cnt - xg[MAX_WINDOW:, :]).astype(BF16)
            pwb = pw_ref[gi].astype(BF16)
            z = _dot(pooled, pwb, NN)
            scale = ps_ref[:, cols]
            m_ref[:, cols] = (z * scale).astype(BF16)
            da = dm[:, cols]
            dps_ref[:, cols] += jnp.sum(da * z, axis=0, keepdims=True)
            dz = (da * scale).astype(BF16)
            dpw_ref[gi] += _dot(pooled, dz, TN)
            dpl = _dot(dz, pwb, NT)
            dpn = dpl / cnt
            ext2 = jnp.concatenate([dpn, next_dpn[:, cols]], axis=0)
            dpj[:, cols] = (_window_sums(ext2, w, False)[0:tm, :] - dpl).astype(BF16)
            carry[:, cols] = dpn[0:MAX_WINDOW, :]

        cos, sin = cos_ref[...], sin_ref[...]
        for h in range(RET_HEADS):
            cq = slice(pwid + h * GROUP, pwid + (h + 1) * GROUP)
            ck = slice(pwid + rwid + h * GROUP, pwid + rwid + (h + 1) * GROUP)
            cv = slice(pwid + 2 * rwid + h * GROUP, pwid + 2 * rwid + (h + 1) * GROUP)
            cg = slice(pwid + 3 * rwid + h * GROUP, pwid + 3 * rwid + (h + 1) * GROUP)
            ch = slice(h * GROUP, (h + 1) * GROUP)
            qr = _rope(proj_ref[:, cq], cos, sin)
            kr = _rope(proj_ref[:, ck], cos, sin) * (GROUP ** -0.5)
            vb = proj_ref[:, cv].astype(BF16)
            gv = proj_ref[:, cg]
            ov = o_ref[:, ch]
            ro = lax.rsqrt(jnp.mean(ov * ov, axis=-1, keepdims=True) + EPS)
            on = ov * ro
            rg = rg_ref[:, ch]
            db = dm[:, pwid + h * GROUP:pwid + (h + 1) * GROUP]
            sg = _sigmoid(gv)
            sl = gv * sg
            m_ref[:, pwid + h * GROUP:pwid + (h + 1) * GROUP] = (sl * (on * rg)).astype(BF16)
            dpj[:, cg] = (db * (on * rg) * (sg * (1.0 + gv * (1.0 - sg)))).astype(BF16)
            drg_ref[:, ch] += jnp.sum(db * sl * on, axis=0, keepdims=True)
            don = db * sl * rg
            do = (ro * (don - on * jnp.mean(don * on, axis=-1, keepdims=True))).astype(BF16)
            for n in reversed(range(nck)):
                rows = slice(n * GROUP, (n + 1) * GROUP)
                qc, kc, vc, dob = qr[rows], kr[rows], vb[rows], do[rows]
                qcb, kcb = qc.astype(BF16), kc.astype(BF16)
                qh = (qc * qh_ref[h]).astype(BF16)
                kt = (kc * kt_ref[h]).astype(BF16)
                rn = rs_ref[n, h].astype(BF16)
                dnext = dstate[h]
                dnb = dnext.astype(BF16)
                dec = dec_ref[h]
                p = (_dot(qcb, kcb, NT) * dec).astype(BF16)
                ds = (_dot(dob, vc, NT) * dec).astype(BF16)
                dv = _dot(p, dob, TN) + _dot(kt, dnb, NN)
                dq = _dot(ds, kcb, NN) + _dot(dob, rn, NT) * qh_ref[h]
                dk = _dot(ds, qcb, TN) + _dot(vc, dnb, NT) * kt_ref[h]
                dstate[h] = cd[h] * dnext + _dot(qh, dob, TN)
                dpj[rows, cq] = _rope_bwd(dq, cos[rows], sin[rows]).astype(BF16)
                dpj[rows, ck] = _rope_bwd(dk * (GROUP ** -0.5), cos[rows], sin[rows]).astype(BF16)
                dpj[rows, cv] = dv.astype(BF16)

        dproj_ref[...] = dpj[...]
        du = _dot(dpj[...], wins[...], NN)
        dg_ref[...] += jnp.sum(du * uh, axis=0, keepdims=True)
        dn = du * g
        dh1_ref[...] = dh2v + r * (dn - uh * jnp.mean(dn * uh, axis=-1, keepdims=True))

    rev = lambda i: (nt - 1 - i, 0)
    tile = pl.BlockSpec((tm, d), rev)
    full = lambda shape: pl.BlockSpec(shape, lambda i: (0,) * len(shape))
    return _grid_call(
        body,
        carries,
        name="mix_bwd",
        grid=(nt,),
        in_specs=[
            tile, tile,
            pl.BlockSpec((tm, inw), rev),
            pl.BlockSpec((MAX_WINDOW, pwid), lambda i: (jnp.maximum((nt - 1 - i) * halo_per_tile - 1, 0), 0)),
            pl.BlockSpec((tm, rwid), rev),
            pl.BlockSpec((nck, RET_HEADS, GROUP, GROUP), lambda i: (nt - 1 - i, 0, 0, 0)),
            full((1, d)), ANY, ANY,
            full((N_POOL_GROUPS, GROUP, GROUP)), full((1, pwid)), full((1, rwid)),
            pl.BlockSpec((tm, GROUP), rev), pl.BlockSpec((tm, GROUP), rev),
            full((RET_HEADS, GROUP, GROUP)), full((RET_HEADS, GROUP, GROUP)), full((RET_HEADS, GROUP, GROUP)),
        ],
        out_specs=[
            tile, pl.BlockSpec((tm, inw), rev), tile, tile,
            full((N_POOL_GROUPS, GROUP, GROUP)), full((1, pwid)), full((1, rwid)), full((1, d)),
        ],
        out_shape=[
            jax.ShapeDtypeStruct((s, d), F32),
            jax.ShapeDtypeStruct((s, inw), BF16),
            jax.ShapeDtypeStruct((s, d), BF16),
            jax.ShapeDtypeStruct((s, d), BF16),
            jax.ShapeDtypeStruct((N_POOL_GROUPS, GROUP, GROUP), F32),
            jax.ShapeDtypeStruct((1, pwid), F32),
            jax.ShapeDtypeStruct((1, rwid), F32),
            jax.ShapeDtypeStruct((1, d), F32),
        ],
        scratch_shapes=[
            pltpu.VMEM((inw, d), BF16), pltpu.VMEM((d, d), BF16),
            pltpu.VMEM((RET_HEADS, GROUP, GROUP), F32), pltpu.VMEM((MAX_WINDOW, pwid), F32),
            pltpu.VMEM((tm, d), F32), pltpu.VMEM((tm, inw), BF16), pltpu.SemaphoreType.DMA((2 * NDEV,)),
        ],
        args=[dh2, h1, proj, proj, o_saved, rsave, gain, weights[0], weights[1], pool_w, pool_scale, ret_gain,
              consts["cos"], consts["sin"], consts["decay"], consts["ktail"], consts["qhead"]],
    )


def _adam(w, g, m, v):
    m = ADAM_B1 * m + (1.0 - ADAM_B1) * g
    v = ADAM_B2 * v + (1.0 - ADAM_B2) * jnp.square(g)
    m_hat = m / (1.0 - ADAM_B1 ** ADAM_STEP)
    v_hat = v / (1.0 - ADAM_B2 ** ADAM_STEP)
    delta = -ADAM_LR * (m_hat / (jnp.sqrt(v_hat) + ADAM_EPS) + ADAM_WD * w)
    return delta, m, v


def _adamw_big(w, parts, m, v, name):
    rows, d = w.shape
    tr = _row_tile(rows, 176)

    def body(w_ref, p_ref, m_ref, v_ref, g_ref, d_ref, nm_ref, nv_ref):
        g = p_ref[0].astype(F32)
        for q in range(1, NCHIP):
            g = g + p_ref[q].astype(F32)
        g_ref[...] = g
        d_ref[...], nm_ref[...], nv_ref[...] = _adam(w_ref[...], g, m_ref[...], v_ref[...])

    spec = pl.BlockSpec((tr, d), lambda i: (i, 0))
    return _call(
        body,
        name=name,
        grid=(rows // tr,),
        in_specs=[spec, pl.BlockSpec((NCHIP, tr, d), lambda i: (0, i, 0)), spec, spec],
        out_specs=[spec] * 4,
        out_shape=[jax.ShapeDtypeStruct((rows, d), F32)] * 4,
        compiler_params=_seq(1),
    )(w, parts, m, v)


def _adamw_small(stats_all, pw_all, ws, ms, vs, pwid):
    nsmall = len(ws)

    def body(*refs):
        st_ref, pwa_ref = refs[0], refs[1]
        w_refs = refs[2:2 + nsmall]
        m_refs = refs[2 + nsmall:2 + 2 * nsmall]
        v_refs = refs[2 + 2 * nsmall:2 + 3 * nsmall]
        outs = refs[2 + 3 * nsmall:]
        st = st_ref[0]
        pwg = pwa_ref[0]
        for q in range(1, NDEV):
            st = st + st_ref[q]
            pwg = pwg + pwa_ref[q]
        grads = [st[0:1, :], st[1:2, :], st[2:3, :], st[3:4, :], st[4:5, 0:pwid], st[4:5, pwid:2 * pwid], pwg]
        outs[0][...] = jnp.zeros((1, GROUP), F32) + jnp.sum(st[5:6, :])
        for j in range(nsmall):
            delta, nm, nv = _adam(w_refs[j][...], grads[j], m_refs[j][...], v_refs[j][...])
            outs[1 + 4 * j][...] = grads[j]
            outs[2 + 4 * j][...] = delta
            outs[3 + 4 * j][...] = nm
            outs[4 + 4 * j][...] = nv

    out_shape = [jax.ShapeDtypeStruct((1, GROUP), F32)]
    for w in ws:
        out_shape += [jax.ShapeDtypeStruct(w.shape, F32)] * 4
    return _call(body, name="adamw_small", out_shape=out_shape, compiler_params=_params())(
        stats_all, pw_all, *ws, *ms, *vs
    )


def kernel(x, ffn1_norm, ffn1_w1, ffn1_w3, ffn1_w2, mix_norm, w_in, pool_w, pool_scale, ret_norm, w_out, ffn2_norm, ffn2_w1, ffn2_w3, ffn2_w2, final_norm, loss_target, m_ffn1_norm, m_ffn1_w1, m_ffn1_w3, m_ffn1_w2, m_mix_norm, m_w_in, m_pool_w, m_pool_scale, m_ret_norm, m_w_out, m_ffn2_norm, m_ffn2_w1, m_ffn2_w3, m_ffn2_w2, m_final_norm, v_ffn1_norm, v_ffn1_w1, v_ffn1_w3, v_ffn1_w2, v_mix_norm, v_w_in, v_pool_w, v_pool_scale, v_ret_norm, v_w_out, v_ffn2_norm, v_ffn2_w1, v_ffn2_w3, v_ffn2_w2, v_final_norm):
    s, d = x.shape[1], x.shape[2]
    ffn = ffn1_w1.shape[2] * NDEV
    pwid = pool_scale.shape[1]
    xs, tgt = x[0], loss_target[0]
    consts = _mix_constants(s)
    pw3 = pool_w[0]
    fnorm = final_norm.reshape(1, d)

    rows_of = lambda w, transposed: w[0].T if transposed else w[0]
    send_f1 = [rows_of(w, t).astype(BF16) for w, t in ((ffn1_w1, True), (ffn1_w3, True), (ffn1_w2, False))]
    later = [rows_of(w, t) for w, t in ((w_in, True), (w_out, False), (ffn2_w1, True), (ffn2_w3, True), (ffn2_w2, False))]

    sent_later, w13_f1 = _comm_call([_CastRows(later), _Gather(send_f1[:2])], "gather_ffn1")
    send_mix, send_f2 = sent_later[:2], sent_later[2:]
    (a1, b1, hm1), (got,) = _ffn_up(xs, ffn1_norm, *w13_f1, carries=[_Gather(send_f1[2:] + send_mix)])
    w_f1, w_mix = w13_f1 + got[:1], got[1:]
    (h1,), (w1_f2,) = _ffn_down(xs, hm1, w_f1[2], carries=[_Gather(send_f2[:1])])
    (h2, proj, o_saved, rsave), (rest,) = _mix_fwd(
        h1, mix_norm, w_mix, pw3, pool_scale, ret_norm, consts, carries=[_Gather(send_f2[1:])]
    )
    w_f2 = w1_f2 + rest
    (dh3, a2, b2, hm2, dgf, loss_cols), _ = _ffn_fwd(h2, ffn2_norm, w_f2, ffn, head=(fnorm, tgt))

    (dh2, da2, db2, n2, dg2), _ = _ffn_bwd(dh3, h2, a2, b2, ffn2_norm, w_f2, ffn, "ffn2_bwd")
    sum_f2w1, _ = _wgrad(da2, n2, 1.0, "ffn2_w1_grad")
    sum_f2w3, _ = _wgrad(db2, n2, 1.0, "ffn2_w3_grad")
    sum_f2w2, ((parts_f2w1,),) = _wgrad(hm2, dh3, 0.5, "ffn2_w2_grad", carries=[_ChipScatter([sum_f2w1])])

    (dh1, dproj, u, mm, dpw, dps, drg, dgm), ((parts_f2w3, parts_f2w2),) = _mix_bwd(
        dh2, h1, proj, o_saved, rsave, mix_norm, w_mix, pw3, pool_scale, ret_norm, consts,
        carries=[_ChipScatter([sum_f2w3, sum_f2w2])],
    )
    (dx, da1, db1, n1, dg1), _ = _ffn_bwd(dh1, xs, a1, b1, ffn1_norm, w_f1, ffn, "ffn1_bwd")
    stats = jnp.concatenate(
        [dg1, dgm, dg2, dgf, jnp.concatenate([dps, drg], axis=1), loss_cols, jnp.zeros((2, d), F32)], axis=0
    )
    small = _GatherDirect([stats, dpw.reshape(N_POOL_GROUPS * GROUP, GROUP)])
    sum_f1w2, ((stats_all, pw_all),) = _wgrad(hm1, dh1, 0.5, "ffn1_w2_grad", carries=[small])
    sum_f1w1, ((parts_f1w2,),) = _wgrad(da1, n1, 1.0, "ffn1_w1_grad", carries=[_ChipScatter([sum_f1w2])])
    sum_f1w3, ((parts_f1w1,),) = _wgrad(db1, n1, 1.0, "ffn1_w3_grad", carries=[_ChipScatter([sum_f1w1])])
    sum_in, ((parts_f1w3,),) = _wgrad(dproj, u, 1.0, "w_in_grad", carries=[_ChipScatter([sum_f1w3])])
    sum_out, ((parts_in,),) = _wgrad(mm, dh2, 1.0, "w_out_grad", carries=[_ChipScatter([sum_in])])
    ((parts_out,),) = _comm_call([_ChipScatter([sum_out])], "scatter_last")

    big = (
        (ffn1_w1, m_ffn1_w1, v_ffn1_w1, parts_f1w1, True),
        (ffn1_w3, m_ffn1_w3, v_ffn1_w3, parts_f1w3, True),
        (ffn1_w2, m_ffn1_w2, v_ffn1_w2, parts_f1w2, False),
        (w_in, m_w_in, v_w_in, parts_in, True),
        (w_out, m_w_out, v_w_out, parts_out, False),
        (ffn2_w1, m_ffn2_w1, v_ffn2_w1, parts_f2w1, True),
        (ffn2_w3, m_ffn2_w3, v_ffn2_w3, parts_f2w3, True),
        (ffn2_w2, m_ffn2_w2, v_ffn2_w2, parts_f2w2, False),
    )
    big_out = []
    for j, (w, m, v, parts, t) in enumerate(big):
        view = (lambda a: a[0].T) if t else (lambda a: a[0])
        back = (lambda a: a.T[None]) if t else (lambda a: a[None])
        big_out.append([back(a) for a in _adamw_big(view(w), parts, view(m), view(v), "adamw_%d" % j)])

    small_w = (ffn1_norm, mix_norm, ffn2_norm, fnorm, pool_scale, ret_norm, pw3.reshape(-1, GROUP))
    small_m = (m_ffn1_norm, m_mix_norm, m_ffn2_norm, m_final_norm.reshape(1, d), m_pool_scale, m_ret_norm, m_pool_w.reshape(-1, GROUP))
    small_v = (v_ffn1_norm, v_mix_norm, v_ffn2_norm, v_final_norm.reshape(1, d), v_pool_scale, v_ret_norm, v_pool_w.reshape(-1, GROUP))
    res = _adamw_small(stats_all, pw_all, small_w, small_m, small_v, pwid)
    loss = res[0][0, 0]
    small_out = [list(res[1 + 4 * j:5 + 4 * j]) for j in range(len(small_w))]
    small_out[3] = [a.reshape(d) for a in small_out[3]]
    small_out[6] = [a.reshape(pool_w.shape) for a in small_out[6]]

    order = [small_out[0], big_out[0], big_out[1], big_out[2], small_out[1], big_out[3], small_out[6], small_out[4],
             small_out[5], big_out[4], small_out[2], big_out[5], big_out[6], big_out[7], small_out[3]]
    result = [loss, dx[None]]
    for kind in range(4):
        result += [t[kind] for t in order]
    return tuple(result)
```

```python
import functools

import numpy as np
import jax
import jax.numpy as jnp
from jax import lax
from jax.experimental import pallas as pl
from jax.experimental.pallas import tpu as pltpu

F32 = jnp.float32
BF16 = jnp.bfloat16

NDEV = 8
NCHIP = 4
EPS = 1e-6
N_POOL_GROUPS = 4
POOL_WINDOWS = (2, 4, 8, 16)
MAX_WINDOW = 16
GROUP = 128
RET_HEADS = 4
ROPE_BASE = 10000.0
ADAM_LR = 0.001
ADAM_B1 = 0.9
ADAM_B2 = 0.999
ADAM_EPS = 1e-08
ADAM_WD = 0.01
ADAM_STEP = 10

VMEM_LIMIT = 56 * 1024 * 1024
FFN_CHUNK = 256
ROW_BAND = 32

NT = (((1,), (1,)), ((), ()))
NN = (((1,), (0,)), ((), ()))
TN = (((0,), (0,)), ((), ()))

ANY = pl.BlockSpec(memory_space=pl.ANY)


def _dot(a, b, dims):
    return lax.dot_general(a, b, dims, preferred_element_type=F32)


def _call(body, **kw):
    return pl.pallas_call(body, **kw)


def _params(**kw):
    return pltpu.CompilerParams(vmem_limit_bytes=VMEM_LIMIT, **kw)


def _seq(n):
    return _params(dimension_semantics=("arbitrary",) * n)


def _peer(k):
    x, y, c = lax.axis_index("x"), lax.axis_index("y"), lax.axis_index("c")
    return (1 - x if k & 4 else x, 1 - y if k & 2 else y, 1 - c if k & 1 else c)


def _flat(pos):
    return 4 * pos[0] + 2 * pos[1] + pos[2]


def _chip(pos):
    return 2 * pos[0] + pos[1]


def _row_tile(rows, cap):
    return max(t for t in range(16, min(rows, cap) + 1, 16) if rows % t == 0)


def _pieces(rows, n):
    tiles = rows // 16
    cuts = [16 * (tiles * q // n) for q in range(n + 1)]
    return [(a, b - a) for a, b in zip(cuts[:-1], cuts[1:])]


def _load_weights(parts, sems):
    copies = []
    for buf, dst in parts:
        rows = buf.shape[1]
        for p in range(NDEV):
            cp = pltpu.make_async_copy(buf.at[p], dst.at[pl.ds(p * rows, rows), :], sems.at[len(copies)])
            cp.start()
            copies.append(cp)
    return copies


def _sigmoid(a):
    return 1.0 / (1.0 + jnp.exp(-a))


def _remote(src, dst, send_sem, recv_sem, to):
    return pltpu.make_async_remote_copy(
        src_ref=src, dst_ref=dst, send_sem=send_sem, recv_sem=recv_sem, device_id=to, device_id_type=pl.DeviceIdType.MESH
    )


class _Gather:
    X, Y, FAR = 4, 2, 6
    peers = (1, 2, 4)
    COPIES = 8

    def __init__(self, shards):
        n = len(shards)
        self.operands = list(shards)
        self.out_shape = [jax.ShapeDtypeStruct((NDEV,) + a.shape, a.dtype) for a in shards]
        self.sems = [
            pltpu.SemaphoreType.DMA((self.COPIES * n,)), pltpu.SemaphoreType.DMA((self.COPIES * n,)),
            pltpu.SemaphoreType.DMA((n,)),
        ]
        self.stages = [self.begin, self.relay, self.relay_far, self.end]

    def _copy(self, t, k, block, to, ins, outs, sems, own=False, half=None):
        rows = outs[t].shape[1]
        part = pl.ds(0, rows) if half is None else pl.ds(half * (rows // 2), rows // 2)
        dst = outs[t].at[_flat(block), part, :]
        at = self.COPIES * t + k
        return _remote(ins[t] if own else dst, dst, sems[0].at[at], sems[1].at[at], to)

    def _local(self, t, ins, outs, sems):
        return pltpu.make_async_copy(ins[t], outs[t].at[_flat(_peer(0))], sems[2].at[t])

    def begin(self, ins, outs, sems):
        me = _peer(0)
        for t in range(len(ins)):
            self._local(t, ins, outs, sems).start()
            for k, code in enumerate((1, self.X, self.Y)):
                self._copy(t, k, me, _peer(code), ins, outs, sems, own=True).start()

    def relay(self, ins, outs, sems):
        me, sibling = _peer(0), _peer(1)
        for t in range(len(ins)):
            self._copy(t, 1, _peer(self.X), me, ins, outs, sems).wait_recv()
            self._copy(t, 3, _peer(self.X), _peer(self.Y), ins, outs, sems, half=0).start()
            self._copy(t, 5, _peer(self.X), sibling, ins, outs, sems).start()
            self._copy(t, 2, _peer(self.Y), me, ins, outs, sems).wait_recv()
            self._copy(t, 4, _peer(self.Y), _peer(self.X), ins, outs, sems, half=1).start()
            self._copy(t, 6, _peer(self.Y), sibling, ins, outs, sems).start()

    def relay_far(self, ins, outs, sems):
        me, sibling = _peer(0), _peer(1)
        for t in range(len(ins)):
            self._copy(t, 3, _peer(self.FAR), me, ins, outs, sems, half=0).wait_recv()
            self._copy(t, 4, _peer(self.FAR), me, ins, outs, sems, half=1).wait_recv()
            self._copy(t, 7, _peer(self.FAR), sibling, ins, outs, sems).start()

    def end(self, ins, outs, sems):
        me = _peer(0)
        for t in range(len(ins)):
            self._copy(t, 0, _peer(1), me, ins, outs, sems).wait_recv()
            for k, code in ((5, self.X), (6, self.Y), (7, self.FAR)):
                self._copy(t, k, _peer(code ^ 1), me, ins, outs, sems).wait_recv()
            for k in range(self.COPIES):
                self._copy(t, k, me, me, ins, outs, sems, half=0 if k == 3 else 1 if k == 4 else None).wait_send()
            self._local(t, ins, outs, sems).wait()


class _GatherDirect:
    peers = tuple(range(1, NDEV))

    def __init__(self, arrays):
        n = len(arrays)
        self.operands = list(arrays)
        self.out_shape = [jax.ShapeDtypeStruct((NDEV,) + a.shape, a.dtype) for a in arrays]
        self.sems = [pltpu.SemaphoreType.DMA((7 * n,)), pltpu.SemaphoreType.DMA((7 * n,)), pltpu.SemaphoreType.DMA((n,))]
        self.stages = [self.begin, self.end]

    def begin(self, ins, outs, sems):
        mine = _flat(_peer(0))
        for t in range(len(ins)):
            pltpu.make_async_copy(ins[t], outs[t].at[mine], sems[2].at[t]).start()
            for k in range(1, NDEV):
                _remote(ins[t], outs[t].at[mine], sems[0].at[7 * t + k - 1], sems[1].at[7 * t + k - 1], _peer(k)).start()

    def end(self, ins, outs, sems):
        mine = _flat(_peer(0))
        for t in range(len(ins)):
            for k in range(1, NDEV):
                cp = _remote(ins[t], outs[t].at[_flat(_peer(k))], sems[0].at[7 * t + k - 1], sems[1].at[7 * t + k - 1], _peer(k))
                cp.wait_recv()
                cp.wait_send()
            pltpu.make_async_copy(ins[t], outs[t].at[mine], sems[2].at[t]).wait()


class _ChipScatter:
    peers = (2, 4, 6)
    pieces = 1

    def __init__(self, sums):
        n = len(sums) * NCHIP * self.pieces
        self.operands = list(sums)
        self.out_shape = [jax.ShapeDtypeStruct(a.shape, a.dtype) for a in sums]
        self.sems = [pltpu.SemaphoreType.DMA((n,)), pltpu.SemaphoreType.DMA((n,))]
        self.stages = [self.begin, self.end]

    def _copies(self, ins, outs, sems, arriving):
        mine = _chip(_peer(0))
        copies = []
        for t in range(len(ins)):
            rows = ins[t].shape[1] // self.pieces
            for k in (0, 4, 2, 6):
                other = _chip(_peer(k))
                for q in range(self.pieces):
                    part = pl.ds(q * rows, rows)
                    at = len(copies)
                    if k == 0:
                        cp = pltpu.make_async_copy(ins[t].at[mine, part, :], outs[t].at[mine, part, :], sems[0].at[at])
                    else:
                        landing = outs[t].at[other if arriving else mine, part, :]
                        cp = _remote(ins[t].at[other, part, :], landing, sems[0].at[at], sems[1].at[at], _peer(k))
                    copies.append(cp)
        return copies

    def begin(self, ins, outs, sems):
        for cp in self._copies(ins, outs, sems, False):
            cp.start()

    def end(self, ins, outs, sems):
        for at, cp in enumerate(self._copies(ins, outs, sems, True)):
            if at % (NCHIP * self.pieces) < self.pieces:
                cp.wait()
            else:
                cp.wait_recv()
                cp.wait_send()


class _CastRows:
    peers = ()

    def __init__(self, arrays):
        n = len(arrays)
        self.operands = list(arrays)
        self.out_shape = [jax.ShapeDtypeStruct(a.shape, BF16) for a in arrays]
        self.sems = [pltpu.SemaphoreType.DMA((n,)), pltpu.SemaphoreType.DMA((n,))]
        self.sems += [pltpu.VMEM(a.shape, F32) for a in arrays] + [pltpu.VMEM(a.shape, BF16) for a in arrays]
        self.stages = [self.begin, self.convert, self.end]

    def _moves(self, t, ins, outs, scratch):
        n = len(ins)
        load = pltpu.make_async_copy(ins[t], scratch[2 + t], scratch[0].at[t])
        store = pltpu.make_async_copy(scratch[2 + n + t], outs[t], scratch[1].at[t])
        return load, store

    def begin(self, ins, outs, scratch):
        for t in range(len(ins)):
            self._moves(t, ins, outs, scratch)[0].start()

    def convert(self, ins, outs, scratch):
        n = len(ins)
        for t in range(n):
            load, store = self._moves(t, ins, outs, scratch)
            load.wait()
            scratch[2 + n + t][...] = scratch[2 + t][...].astype(BF16)
            store.start()

    def end(self, ins, outs, scratch):
        for t in range(len(ins)):
            self._moves(t, ins, outs, scratch)[1].wait()


def _split_refs(refs, counts):
    out, at = [], 0
    for n in counts:
        out.append(refs[at:at + n])
        at += n
    return out


BARRIER_IDS = {(2, 4, 6): 0, (1, 2, 4): 1, (1,): 2, (1, 2, 4, 6): 3, tuple(range(1, NDEV)): 4}


def _peers_of(carries, own=()):
    peers = tuple(sorted(set(own).union(*[c.peers for c in carries])))
    return (peers, BARRIER_IDS[peers]) if peers in BARRIER_IDS else (None, None)


def _handshake(peers):
    barrier = pltpu.get_barrier_semaphore()
    for k in peers:
        pl.semaphore_signal(barrier, inc=1, device_id=_peer(k), device_id_type=pl.DeviceIdType.MESH)
    pl.semaphore_wait(barrier, len(peers))


def _comm_call(carries, name):
    nin = [len(c.operands) for c in carries]
    nout = [len(c.out_shape) for c in carries]
    nsem = [len(c.sems) for c in carries]
    peers, collective_id = _peers_of(carries)

    def body(*refs):
        if peers:
            _handshake(peers)
        ins, outs, sems = _split_refs(refs, (sum(nin), sum(nout), sum(nsem)))
        parts = list(zip(carries, _split_refs(ins, nin), _split_refs(outs, nout), _split_refs(sems, nsem)))
        for depth in range(max(len(c.stages) for c in carries)):
            for c, i, o, s in parts:
                if depth < len(c.stages) - 1:
                    c.stages[depth](i, o, s)
        for c, i, o, s in parts:
            c.stages[-1](i, o, s)

    res = _call(
        body,
        name=name,
        out_shape=[sh for c in carries for sh in c.out_shape],
        in_specs=[ANY] * sum(nin),
        out_specs=[ANY] * sum(nout),
        scratch_shapes=[sm for c in carries for sm in c.sems],
        compiler_params=_params(has_side_effects=True, collective_id=collective_id),
    )(*[a for c in carries for a in c.operands])
    return _split_refs(list(res), nout)


def _grid_call(body, carries, *, name, grid, in_specs, out_specs, out_shape, scratch_shapes, args, own_peers=()):
    ni, no, ns = len(in_specs), len(out_specs), len(scratch_shapes)
    nin = [len(c.operands) for c in carries]
    nout = [len(c.out_shape) for c in carries]
    nsem = [len(c.sems) for c in carries]
    steps = int(np.prod(grid))
    peers, collective_id = _peers_of(carries, own_peers)

    def when_of(stage, count):
        first, last = (5 * steps) // 8 - 1, steps - 2
        return max(0, last if count <= 3 else first + (last - first) * (stage - 1) // (count - 3))

    def wrapped(*refs):
        ins, cins, outs, couts, scr, csems = _split_refs(refs, (ni, sum(nin), no, sum(nout), ns, sum(nsem)))
        if not carries and not peers:
            return body(*ins, *outs, *scr)
        parts = list(zip(carries, _split_refs(cins, nin), _split_refs(couts, nout), _split_refs(csems, nsem)))
        step = pl.program_id(0)
        for axis in range(1, len(grid)):
            step = step * grid[axis] + pl.program_id(axis)

        @pl.when(step == 0)
        def _():
            if peers:
                _handshake(peers)
            for c, i, o, s in parts:
                c.stages[0](i, o, s)

        body(*ins, *outs, *scr)

        for c, i, o, s in parts:
            for stage in range(1, len(c.stages) - 1):
                pl.when(step == when_of(stage, len(c.stages)))(functools.partial(c.stages[stage], i, o, s))

        @pl.when(step == steps - 1)
        def _():
            for c, i, o, s in parts:
                c.stages[-1](i, o, s)

    res = _call(
        wrapped,
        name=name,
        grid=tuple(grid),
        in_specs=list(in_specs) + [ANY] * sum(nin),
        out_specs=list(out_specs) + [ANY] * sum(nout),
        out_shape=list(out_shape) + [sh for c in carries for sh in c.out_shape],
        scratch_shapes=list(scratch_shapes) + [sm for c in carries for sm in c.sems],
        compiler_params=_params(dimension_semantics=("arbitrary",) * len(grid), collective_id=collective_id),
    )(*args, *[a for c in carries for a in c.operands])
    res = list(res)
    return res[:no], _split_refs(res[no:], nout)


def _chunks(width):
    return [(at, min(FFN_CHUNK, width - at)) for at in range(0, width, FFN_CHUNK)]


def _start_chunk_loads(bufs, dsts, sems, chunks):
    copies = [[None] * len(chunks) for _ in bufs]
    for c, (at, width) in enumerate(chunks):
        for m, (buf, dst) in enumerate(zip(bufs, dsts)):
            cp = pltpu.make_async_copy(
                buf.at[pl.ds(at, width), :], dst.at[pl.ds(at, width), :], sems.at[m * len(chunks) + c]
            )
            cp.start()
            copies[m][c] = cp
    return copies


def _ffn_fwd(x, gain, weights, ffn, head=None, carries=()):
    s, d = x.shape
    tm = min(512, s)
    chunks = _chunks(ffn)

    def body(*refs):
        if head is None:
            x_ref, g_ref, b1, b3, b2, h_ref, a_ref, b_ref, hm_ref, w1s, w3s, w2s, sems = refs
        else:
            x_ref, g_ref, b1, b3, b2, gf_ref, t_ref, h_ref, a_ref, b_ref, hm_ref, dgf_ref, loss_ref, w1s, w3s, w2s, sems = refs

        def one_tile(loads):
            xv = x_ref[...]
            r = lax.rsqrt(jnp.mean(xv * xv, axis=-1, keepdims=True) + EPS)
            n = (xv * r * g_ref[...]).astype(BF16)
            acc = jnp.zeros((tm, d), F32)
            for c, (at, width) in enumerate(chunks):
                cols = slice(at, at + width)
                if loads is not None:
                    for m in range(3):
                        loads[m][c].wait()
                a = _dot(n, w1s[cols, :], NT)
                b = _dot(n, w3s[cols, :], NT)
                a_ref[:, cols] = a.astype(BF16)
                b_ref[:, cols] = b.astype(BF16)
                hm = (a * _sigmoid(a) * b).astype(BF16)
                hm_ref[:, cols] = hm
                acc = acc + _dot(hm, w2s[cols, :], NN)
            h = xv + 0.5 * acc
            if head is None:
                h_ref[...] = h
            else:
                rf = lax.rsqrt(jnp.mean(h * h, axis=-1, keepdims=True) + EPS)
                nh = h * rf
                gf = gf_ref[...]
                err = nh * gf - t_ref[...]
                loss_ref[...] += jnp.sum(err * err, axis=0, keepdims=True) * (0.5 / d)
                dy = err * (1.0 / d)
                dgf_ref[...] += jnp.sum(dy * nh, axis=0, keepdims=True)
                dn = dy * gf
                h_ref[...] = rf * (dn - nh * jnp.mean(dn * nh, axis=-1, keepdims=True))

        if head is None:
            @pl.when(pl.program_id(0) == 0)
            def _():
                one_tile(_start_chunk_loads((b1, b3, b2), (w1s, w3s, w2s), sems, chunks))

            @pl.when(pl.program_id(0) > 0)
            def _():
                one_tile(None)
        else:
            @pl.when(pl.program_id(0) == 0)
            def _():
                dgf_ref[...] = jnp.zeros_like(dgf_ref)
                loss_ref[...] = jnp.zeros_like(loss_ref)
                for loads in _start_chunk_loads((b1, b3, b2), (w1s, w3s, w2s), sems, chunks):
                    for cp in loads:
                        cp.wait()

            one_tile(None)

    tile = pl.BlockSpec((tm, d), lambda i: (i, 0))
    row = pl.BlockSpec((1, d), lambda i: (0, 0))
    wide = pl.BlockSpec((tm, ffn), lambda i: (i, 0))
    in_specs = [tile, row, ANY, ANY, ANY]
    out_shape = [jax.ShapeDtypeStruct((s, d), F32)] + [jax.ShapeDtypeStruct((s, ffn), BF16)] * 3
    out_specs = [tile, wide, wide, wide]
    args = [x, gain] + [w.reshape(ffn, d) for w in weights]
    if head is not None:
        in_specs += [row, tile]
        args += list(head)
        out_shape += [jax.ShapeDtypeStruct((1, d), F32)] * 2
        out_specs += [row, row]
    return _grid_call(
        body,
        carries,
        name="ffn_fwd_loss" if head is not None else "ffn_fwd",
        grid=(s // tm,),
        in_specs=in_specs,
        out_specs=out_specs,
        out_shape=out_shape,
        scratch_shapes=[pltpu.VMEM((ffn, d), BF16)] * 3 + [pltpu.SemaphoreType.DMA((3 * len(chunks),))],
        args=args,
    )


def _ffn_up(x, gain, w1, w3, carries=()):
    s, d = x.shape
    ffn = w1.shape[0] * w1.shape[1]
    tm = min(512, s)

    def body(x_ref, g_ref, b1, b3, a_ref, b_ref, hm_ref, w1s, w3s, sems):
        @pl.when(pl.program_id(0) == 0)
        def _():
            for cp in _load_weights(((b1, w1s), (b3, w3s)), sems):
                cp.wait()

        xv = x_ref[...]
        r = lax.rsqrt(jnp.mean(xv * xv, axis=-1, keepdims=True) + EPS)
        n = (xv * r * g_ref[...]).astype(BF16)
        for at, width in _chunks(ffn):
            cols = slice(at, at + width)
            a = _dot(n, w1s[cols, :], NT)
            b = _dot(n, w3s[cols, :], NT)
            a_ref[:, cols] = a.astype(BF16)
            b_ref[:, cols] = b.astype(BF16)
            hm_ref[:, cols] = (a * _sigmoid(a) * b).astype(BF16)

    wide = pl.BlockSpec((tm, ffn), lambda i: (i, 0))
    return _grid_call(
        body,
        carries,
        name="ffn_up",
        grid=(s // tm,),
        in_specs=[pl.BlockSpec((tm, d), lambda i: (i, 0)), pl.BlockSpec((1, d), lambda i: (0, 0)), ANY, ANY],
        out_specs=[wide] * 3,
        out_shape=[jax.ShapeDtypeStruct((s, ffn), BF16)] * 3,
        scratch_shapes=[pltpu.VMEM((ffn, d), BF16)] * 2 + [pltpu.SemaphoreType.DMA((2 * NDEV,))],
        args=[x, gain, w1, w3],
    )


def _ffn_down(x, hm, w2, carries=()):
    s, d = x.shape
    ffn = w2.shape[0] * w2.shape[1]
    tm = min(512, s)

    def body(x_ref, hm_ref, b2, h_ref, w2s, sems):
        @pl.when(pl.program_id(0) == 0)
        def _():
            for cp in _load_weights(((b2, w2s),), sems):
                cp.wait()

        acc = jnp.zeros((tm, d), F32)
        for at, width in _chunks(ffn):
            cols = slice(at, at + width)
            acc = acc + _dot(hm_ref[:, cols], w2s[cols, :], NN)
        h_ref[...] = x_ref[...] + 0.5 * acc

    tile = pl.BlockSpec((tm, d), lambda i: (i, 0))
    return _grid_call(
        body,
        carries,
        name="ffn_down",
        grid=(s // tm,),
        in_specs=[tile, pl.BlockSpec((tm, ffn), lambda i: (i, 0)), ANY],
        out_specs=[tile],
        out_shape=[jax.ShapeDtypeStruct((s, d), F32)],
        scratch_shapes=[pltpu.VMEM((ffn, d), BF16), pltpu.SemaphoreType.DMA((NDEV,))],
        args=[x, hm, w2],
    )


def _ffn_bwd(dh, x, a, b, gain, weights, ffn, name, carries=()):
    s, d = x.shape
    tm = min(512, s)
    halves = 2
    fh = ffn // halves

    def body(dh_ref, x_ref, a_ref, b_ref, g_ref, b1, b3, b2, dx_ref, da_ref, db_ref, n_ref, dg_ref, w1s, w3s, w2s, sems):
        i, j = pl.program_id(0), pl.program_id(1)

        @pl.when((i == 0) & (j == 0))
        def _():
            for cp in _load_weights(((b1, w1s), (b3, w3s), (b2, w2s)), sems):
                cp.wait()
            dg_ref[...] = jnp.zeros_like(dg_ref)

        @pl.when(j == 0)
        def _():
            dx_ref[...] = jnp.zeros_like(dx_ref)

        dob = (0.5 * dh_ref[...]).astype(BF16)
        chunks = _chunks(fh)

        def dhm_of(k):
            at, width = chunks[k]
            return _dot(dob, w2s[pl.ds(pl.multiple_of(j * fh + at, GROUP), width), :], NT)

        ahead = dhm_of(0)
        for k, (at, width) in enumerate(chunks):
            cols = slice(at, at + width)
            dhm = ahead
            if k + 1 < len(chunks):
                ahead = dhm_of(k + 1)
            for top in range(0, tm, ROW_BAND):
                band = slice(top, top + ROW_BAND)
                av = a_ref[band, cols].astype(F32)
                bv = b_ref[band, cols].astype(F32)
                sg = _sigmoid(av)
                dv = dhm[band]
                da_ref[band, cols] = (dv * bv * (sg * (1.0 + av * (1.0 - sg)))).astype(BF16)
                db_ref[band, cols] = (dv * (av * sg)).astype(BF16)
        half = pl.ds(pl.multiple_of(j * fh, GROUP), fh)
        dx_ref[...] += _dot(da_ref[...], w1s[half, :], NN) + _dot(db_ref[...], w3s[half, :], NN)

        @pl.when(j == halves - 1)
        def _():
            xv = x_ref[...]
            g = g_ref[...]
            r = lax.rsqrt(jnp.mean(xv * xv, axis=-1, keepdims=True) + EPS)
            nh = xv * r
            n_ref[...] = (nh * g).astype(BF16)
            total = dx_ref[...]
            dg_ref[...] += jnp.sum(total * nh, axis=0, keepdims=True)
            dnh = total * g
            dx_ref[...] = dh_ref[...] + r * (dnh - nh * jnp.mean(dnh * nh, axis=-1, keepdims=True))

    tile = pl.BlockSpec((tm, d), lambda i, j: (i, 0))
    row = pl.BlockSpec((1, d), lambda i, j: (0, 0))
    wide = pl.BlockSpec((tm, fh), lambda i, j: (i, j))
    return _grid_call(
        body,
        carries,
        name=name,
        grid=(s // tm, halves),
        in_specs=[tile, tile, wide, wide, row, ANY, ANY, ANY],
        out_specs=[tile, wide, wide, tile, row],
        out_shape=[
            jax.ShapeDtypeStruct((s, d), F32),
            jax.ShapeDtypeStruct((s, ffn), BF16),
            jax.ShapeDtypeStruct((s, ffn), BF16),
            jax.ShapeDtypeStruct((s, d), BF16),
            jax.ShapeDtypeStruct((1, d), F32),
        ],
        scratch_shapes=[pltpu.VMEM((ffn, d), BF16)] * 3 + [pltpu.SemaphoreType.DMA((3 * NDEV,))],
        args=[dh, x, a, b, gain] + list(weights),
    )


SWAP_PIECES = 1


def _wgrad(lhs, rhs, scale, name, carries=()):
    s, m = lhs.shape
    n = rhs.shape[1]
    rs = m // NDEV
    tk = min(1024, s)
    steps = s // tk
    pieces = [(j, at, size) for j in range(2) for at, size in _pieces(rs, SWAP_PIECES)]

    def body(l_ref, r_ref, o_ref, acc, mine, theirs, send_sems, recv_sems):
        h, k = pl.program_id(0), pl.program_id(1)

        @pl.when(k == 0)
        def _():
            acc[...] = _dot(l_ref[...], r_ref[...].astype(BF16), TN)

        @pl.when(k > 0)
        def _():
            acc[...] += _dot(l_ref[...], r_ref[...].astype(BF16), TN)

        def exchange(half):
            c = lax.axis_index("c")
            return [
                _remote(mine.at[half, 1 - c, j, pl.ds(at, size), :], theirs.at[half, j, pl.ds(at, size), :],
                        send_sems.at[half * len(pieces) + q], recv_sems.at[half * len(pieces) + q], _peer(1))
                for q, (j, at, size) in enumerate(pieces)
            ]

        def settle(half):
            for cp in exchange(half):
                cp.wait_recv()
            both = mine[half, lax.axis_index("c")].astype(F32) + theirs[half].astype(F32)
            o_ref[2 * half:2 * half + 2] = both.astype(BF16)
            for cp in exchange(half):
                cp.wait_send()

        for half in range(2):
            @pl.when((h == half) & (k == steps - 1))
            def _():
                for p in range(NCHIP):
                    mine[half, p % 2, p // 2] = (acc[p * rs:(p + 1) * rs, :] * scale).astype(BF16)
                for cp in exchange(half):
                    cp.start()
                if half == 1:
                    settle(0)
                    settle(1)

    (out,), carried = _grid_call(
        body,
        carries,
        name=name,
        grid=(2, steps),
        in_specs=[pl.BlockSpec((tk, m // 2), lambda h, k: (k, h)), pl.BlockSpec((tk, n), lambda h, k: (k, 0))],
        out_specs=[pl.BlockSpec((NCHIP, rs, n), lambda h, k: (0, 0, 0))],
        out_shape=[jax.ShapeDtypeStruct((NCHIP, rs, n), BF16)],
        scratch_shapes=[
            pltpu.VMEM((m // 2, n), F32), pltpu.VMEM((2, 2, 2, rs, n), BF16), pltpu.VMEM((2, 2, rs, n), BF16),
            pltpu.SemaphoreType.DMA((2 * len(pieces),)), pltpu.SemaphoreType.DMA((2 * len(pieces),)),
        ],
        args=[lhs, rhs],
        own_peers=(1,),
    )
    return out, carried


def _mix_constants(s):
    c = GROUP
    lg = np.log1p(-np.exp2(-5.0 - np.arange(RET_HEADS, dtype=np.float32))).astype(np.float32)
    pos = np.arange(c, dtype=np.float32)
    rel = pos[:, None] - pos[None, :]
    decay = np.where(rel[None] >= 0, np.exp(lg[:, None, None] * np.maximum(rel, 0.0)[None]), 0.0).astype(np.float32)
    ktail = np.exp(lg[:, None] * (c - 1 - pos)[None, :]).astype(np.float32)
    qhead = np.exp(lg[:, None] * (pos + 1.0)[None, :]).astype(np.float32)
    chunk_decay = [float(v) for v in np.exp(lg * np.float32(c)).astype(np.float32)]
    ones = np.ones((1, 1, c), np.float32)
    inv_freq = (1.0 / (np.float32(ROPE_BASE) ** (np.arange(0, c, 2, dtype=np.float32) / np.float32(c)))).astype(np.float32)
    ang = (np.arange(s, dtype=np.float32)[:, None] * inv_freq[None, :]).astype(np.float32)
    cos, sin = np.cos(ang).astype(np.float32), np.sin(ang).astype(np.float32)
    return dict(
        decay=jnp.asarray(decay),
        ktail=jnp.asarray(ktail[:, :, None] * ones),
        qhead=jnp.asarray(qhead[:, :, None] * ones),
        chunk_decay=chunk_decay,
        cos=jnp.asarray(np.concatenate([cos, cos], axis=-1)),
        sin=jnp.asarray(np.concatenate([-sin, sin], axis=-1)),
    )


def _rope(t, cos, sin):
    return t * cos + pltpu.roll(t, GROUP // 2, axis=1) * sin


def _rope_bwd(dt, cos, sin):
    return dt * cos + pltpu.roll(dt * sin, GROUP // 2, axis=1)


def _window_sums(ext, w, forward):
    rows = ext.shape[0]
    acc, k = ext, 1
    while k < w:
        acc = acc + pltpu.roll(acc, k if forward else rows - k, axis=0)
        k *= 2
    return acc


def _pool_counts(tile, tm, w):
    t = lax.broadcasted_iota(jnp.int32, (tm, 1), 0) + tile * tm
    return jnp.minimum(t + 1, w).astype(F32)


def _mix_fwd(h1, gain, weights, pool_w, pool_scale, ret_gain, consts, carries=()):
    s, d = h1.shape
    pwid = N_POOL_GROUPS * GROUP
    rwid = RET_HEADS * GROUP
    inw = pwid + 4 * rwid
    tm = min(256, s)
    nck = tm // GROUP
    cd = consts["chunk_decay"]

    def body(h_ref, g_ref, bin_, bout, pw_ref, ps_ref, rg_ref, cos_ref, sin_ref, dec_ref, kt_ref, qh_ref,
             h2_ref, proj_ref, o_ref, rs_ref, wins, wouts, state, carry, mbuf, sems):
        i = pl.program_id(0)

        @pl.when(i == 0)
        def _():
            for cp in _load_weights(((bin_, wins), (bout, wouts)), sems):
                cp.wait()
            state[...] = jnp.zeros_like(state)
            carry[...] = jnp.zeros_like(carry)

        hv = h_ref[...]
        r = lax.rsqrt(jnp.mean(hv * hv, axis=-1, keepdims=True) + EPS)
        u = (hv * r * g_ref[...]).astype(BF16)
        proj_ref[...] = _dot(u, wins[...], NT)

        ext = jnp.concatenate([carry[...], proj_ref[:, 0:pwid]], axis=0)
        carry[...] = proj_ref[tm - MAX_WINDOW:tm, 0:pwid]
        for gi, w in enumerate(POOL_WINDOWS):
            cols = slice(gi * GROUP, (gi + 1) * GROUP)
            xg = ext[:, cols]
            ws = _window_sums(xg, w, True)[MAX_WINDOW:, :]
            pooled = ws / _pool_counts(i, tm, w) - xg[MAX_WINDOW:, :]
            z = _dot(pooled.astype(BF16), pw_ref[gi].astype(BF16), NN)
            mbuf[:, cols] = (z * ps_ref[:, cols]).astype(BF16)

        cos, sin = cos_ref[...], sin_ref[...]
        for h in range(RET_HEADS):
            cq = slice(pwid + h * GROUP, pwid + (h + 1) * GROUP)
            ck = slice(pwid + rwid + h * GROUP, pwid + rwid + (h + 1) * GROUP)
            cv = slice(pwid + 2 * rwid + h * GROUP, pwid + 2 * rwid + (h + 1) * GROUP)
            cg = slice(pwid + 3 * rwid + h * GROUP, pwid + 3 * rwid + (h + 1) * GROUP)
            ch = slice(h * GROUP, (h + 1) * GROUP)
            qr = _rope(proj_ref[:, cq], cos, sin)
            kr = _rope(proj_ref[:, ck], cos, sin) * (GROUP ** -0.5)
            vb = proj_ref[:, cv].astype(BF16)
            for n in range(nck):
                rows = slice(n * GROUP, (n + 1) * GROUP)
                qc, kc, vc = qr[rows], kr[rows], vb[rows]
                rb = state[h]
                rs_ref[n, h] = rb
                p = (_dot(qc.astype(BF16), kc.astype(BF16), NT) * dec_ref[h]).astype(BF16)
                o = _dot(p, vc, NN) + _dot((qc * qh_ref[h]).astype(BF16), rb.astype(BF16), NN)
                state[h] = cd[h] * rb + _dot((kc * kt_ref[h]).astype(BF16), vc, TN)
                o_ref[rows, ch] = o
                on = o * lax.rsqrt(jnp.mean(o * o, axis=-1, keepdims=True) + EPS)
                gv = proj_ref[rows, cg]
                mbuf[rows, pwid + h * GROUP:pwid + (h + 1) * GROUP] = (
                    gv * _sigmoid(gv) * (on * rg_ref[:, ch])
                ).astype(BF16)
        h2_ref[...] = hv + _dot(mbuf[...], wouts[...], NN)

    tile = pl.BlockSpec((tm, d), lambda i: (i, 0))
    full = lambda shape: pl.BlockSpec(shape, lambda i: (0,) * len(shape))
    return _grid_call(
        body,
        carries,
        name="mix_fwd",
        grid=(s // tm,),
        in_specs=[
            tile, full((1, d)), ANY, ANY,
            full((N_POOL_GROUPS, GROUP, GROUP)), full((1, pwid)), full((1, rwid)),
            pl.BlockSpec((tm, GROUP), lambda i: (i, 0)), pl.BlockSpec((tm, GROUP), lambda i: (i, 0)),
            full((RET_HEADS, GROUP, GROUP)), full((RET_HEADS, GROUP, GROUP)), full((RET_HEADS, GROUP, GROUP)),
        ],
        out_specs=[
            tile,
            pl.BlockSpec((tm, inw), lambda i: (i, 0)),
            pl.BlockSpec((tm, rwid), lambda i: (i, 0)),
            pl.BlockSpec((nck, RET_HEADS, GROUP, GROUP), lambda i: (i, 0, 0, 0)),
        ],
        out_shape=[
            jax.ShapeDtypeStruct((s, d), F32),
            jax.ShapeDtypeStruct((s, inw), F32),
            jax.ShapeDtypeStruct((s, rwid), F32),
            jax.ShapeDtypeStruct((s // GROUP, RET_HEADS, GROUP, GROUP), F32),
        ],
        scratch_shapes=[
            pltpu.VMEM((inw, d), BF16), pltpu.VMEM((d, d), BF16),
            pltpu.VMEM((RET_HEADS, GROUP, GROUP), F32), pltpu.VMEM((MAX_WINDOW, pwid), F32),
            pltpu.VMEM((tm, d), BF16), pltpu.SemaphoreType.DMA((2 * NDEV,)),
        ],
        args=[h1, gain, weights[0], weights[1], pool_w, pool_scale, ret_gain,
              consts["cos"], consts["sin"], consts["decay"], consts["ktail"], consts["qhead"]],
    )


def _mix_bwd(dh2, h1, proj, o_saved, rsave, gain, weights, pool_w, pool_scale, ret_gain, consts, carries=()):
    s, d = h1.shape
    pwid = N_POOL_GROUPS * GROUP
    rwid = RET_HEADS * GROUP
    inw = pwid + 4 * rwid
    tm = min(256, s)
    nck = tm // GROUP
    nt = s // tm
    cd = consts["chunk_decay"]
    halo_per_tile = tm // MAX_WINDOW

    def body(dh2_ref, h_ref, proj_ref, halo_ref, o_ref, rs_ref, g_ref, bin_, bout, pw_ref, ps_ref, rg_ref,
             cos_ref, sin_ref, dec_ref, kt_ref, qh_ref,
             dh1_ref, dproj_ref, u_ref, m_ref, dpw_ref, dps_ref, drg_ref, dg_ref,
             wins, wouts, dstate, carry, dm, dpj, sems):
        i = pl.program_id(0)
        tile = nt - 1 - i

        @pl.when(i == 0)
        def _():
            for cp in _load_weights(((bin_, wins), (bout, wouts)), sems):
                cp.wait()
            dstate[...] = jnp.zeros_like(dstate)
            carry[...] = jnp.zeros_like(carry)
            for ref in (dpw_ref, dps_ref, drg_ref, dg_ref):
                ref[...] = jnp.zeros_like(ref)

        dh2v = dh2_ref[...]
        dm[...] = _dot(dh2v.astype(BF16), wouts[...], NT)
        hv = h_ref[...]
        g = g_ref[...]
        r = lax.rsqrt(jnp.mean(hv * hv, axis=-1, keepdims=True) + EPS)
        uh = hv * r
        u_ref[...] = (uh * g).astype(BF16)

        halo = jnp.where(tile == 0, 0.0, halo_ref[...])
        ext = jnp.concatenate([halo, proj_ref[:, 0:pwid]], axis=0)
        next_dpn = carry[...]
        for gi, w in enumerate(POOL_WINDOWS):
            cols = slice(gi * GROUP, (gi + 1) * GROUP)
            xg = ext[:, cols]
            cnt = _pool_counts(tile, tm, w)
            pooled = (_window_sums(xg, w, True)[MAX_WINDOW:, :] / cnt - xg[MAX_WINDOW:, :]).astype(BF16)
            pwb = pw_ref[gi].astype(BF16)
            z = _dot(pooled, pwb, NN)
            scale = ps_ref[:, cols]
            m_ref[:, cols] = (z * scale).astype(BF16)
            da = dm[:, cols]
            dps_ref[:, cols] += jnp.sum(da * z, axis=0, keepdims=True)
            dz = (da * scale).astype(BF16)
            dpw_ref[gi] += _dot(pooled, dz, TN)
            dpl = _dot(dz, pwb, NT)
            dpn = dpl / cnt
            ext2 = jnp.concatenate([dpn, next_dpn[:, cols]], axis=0)
            dpj[:, cols] = (_window_sums(ext2, w, False)[0:tm, :] - dpl).astype(BF16)
            carry[:, cols] = dpn[0:MAX_WINDOW, :]

        cos, sin = cos_ref[...], sin_ref[...]
        for h in range(RET_HEADS):
            cq = slice(pwid + h * GROUP, pwid + (h + 1) * GROUP)
            ck = slice(pwid + rwid + h * GROUP, pwid + rwid + (h + 1) * GROUP)
            cv = slice(pwid + 2 * rwid + h * GROUP, pwid + 2 * rwid + (h + 1) * GROUP)
            cg = slice(pwid + 3 * rwid + h * GROUP, pwid + 3 * rwid + (h + 1) * GROUP)
            ch = slice(h * GROUP, (h + 1) * GROUP)
            qr = _rope(proj_ref[:, cq], cos, sin)
            kr = _rope(proj_ref[:, ck], cos, sin) * (GROUP ** -0.5)
            vb = proj_ref[:, cv].astype(BF16)
            gv = proj_ref[:, cg]
            ov = o_ref[:, ch]
            ro = lax.rsqrt(jnp.mean(ov * ov, axis=-1, keepdims=True) + EPS)
            on = ov * ro
            rg = rg_ref[:, ch]
            db = dm[:, pwid + h * GROUP:pwid + (h + 1) * GROUP]
            sg = _sigmoid(gv)
            sl = gv * sg
            m_ref[:, pwid + h * GROUP:pwid + (h + 1) * GROUP] = (sl * (on * rg)).astype(BF16)
            dpj[:, cg] = (db * (on * rg) * (sg * (1.0 + gv * (1.0 - sg)))).astype(BF16)
            drg_ref[:, ch] += jnp.sum(db * sl * on, axis=0, keepdims=True)
            don = db * sl * rg
            do = (ro * (don - on * jnp.mean(don * on, axis=-1, keepdims=True))).astype(BF16)
            for n in reversed(range(nck)):
                rows = slice(n * GROUP, (n + 1) * GROUP)
                qc, kc, vc, dob = qr[rows], kr[rows], vb[rows], do[rows]
                qcb, kcb = qc.astype(BF16), kc.astype(BF16)
                qh = (qc * qh_ref[h]).astype(BF16)
                kt = (kc * kt_ref[h]).astype(BF16)
                rn = rs_ref[n, h].astype(BF16)
                dnext = dstate[h]
                dnb = dnext.astype(BF16)
                dec = dec_ref[h]
                p = (_dot(qcb, kcb, NT) * dec).astype(BF16)
                ds = (_dot(dob, vc, NT) * dec).astype(BF16)
                dv = _dot(p, dob, TN) + _dot(kt, dnb, NN)
                dq = _dot(ds, kcb, NN) + _dot(dob, rn, NT) * qh_ref[h]
                dk = _dot(ds, qcb, TN) + _dot(vc, dnb, NT) * kt_ref[h]
                dstate[h] = cd[h] * dnext + _dot(qh, dob, TN)
                dpj[rows, cq] = _rope_bwd(dq, cos[rows], sin[rows]).astype(BF16)
                dpj[rows, ck] = _rope_bwd(dk * (GROUP ** -0.5), cos[rows], sin[rows]).astype(BF16)
                dpj[rows, cv] = dv.astype(BF16)

        dproj_ref[...] = dpj[...]
        du = _dot(dpj[...], wins[...], NN)
        dg_ref[...] += jnp.sum(du * uh, axis=0, keepdims=True)
        dn = du * g
        dh1_ref[...] = dh2v + r * (dn - uh * jnp.mean(dn * uh, axis=-1, keepdims=True))

    rev = lambda i: (nt - 1 - i, 0)
    tile = pl.BlockSpec((tm, d), rev)
    full = lambda shape: pl.BlockSpec(shape, lambda i: (0,) * len(shape))
    return _grid_call(
        body,
        carries,
        name="mix_bwd",
        grid=(nt,),
        in_specs=[
            tile, tile,
            pl.BlockSpec((tm, inw), rev),
            pl.BlockSpec((MAX_WINDOW, pwid), lambda i: (jnp.maximum((nt - 1 - i) * halo_per_tile - 1, 0), 0)),
            pl.BlockSpec((tm, rwid), rev),
            pl.BlockSpec((nck, RET_HEADS, GROUP, GROUP), lambda i: (nt - 1 - i, 0, 0, 0)),
            full((1, d)), ANY, ANY,
            full((N_POOL_GROUPS, GROUP, GROUP)), full((1, pwid)), full((1, rwid)),
            pl.BlockSpec((tm, GROUP), rev), pl.BlockSpec((tm, GROUP), rev),
            full((RET_HEADS, GROUP, GROUP)), full((RET_HEADS, GROUP, GROUP)), full((RET_HEADS, GROUP, GROUP)),
        ],
        out_specs=[
            tile, pl.BlockSpec((tm, inw), rev), tile, tile,
            full((N_POOL_GROUPS, GROUP, GROUP)), full((1, pwid)), full((1, rwid)), full((1, d)),
        ],
        out_shape=[
            jax.ShapeDtypeStruct((s, d), F32),
            jax.ShapeDtypeStruct((s, inw), BF16),
            jax.ShapeDtypeStruct((s, d), BF16),
            jax.ShapeDtypeStruct((s, d), BF16),
            jax.ShapeDtypeStruct((N_POOL_GROUPS, GROUP, GROUP), F32),
            jax.ShapeDtypeStruct((1, pwid), F32),
            jax.ShapeDtypeStruct((1, rwid), F32),
            jax.ShapeDtypeStruct((1, d), F32),
        ],
        scratch_shapes=[
            pltpu.VMEM((inw, d), BF16), pltpu.VMEM((d, d), BF16),
            pltpu.VMEM((RET_HEADS, GROUP, GROUP), F32), pltpu.VMEM((MAX_WINDOW, pwid), F32),
            pltpu.VMEM((tm, d), F32), pltpu.VMEM((tm, inw), BF16), pltpu.SemaphoreType.DMA((2 * NDEV,)),
        ],
        args=[dh2, h1, proj, proj, o_saved, rsave, gain, weights[0], weights[1], pool_w, pool_scale, ret_gain,
              consts["cos"], consts["sin"], consts["decay"], consts["ktail"], consts["qhead"]],
    )


def _adam(w, g, m, v):
    m = ADAM_B1 * m + (1.0 - ADAM_B1) * g
    v = ADAM_B2 * v + (1.0 - ADAM_B2) * jnp.square(g)
    m_hat = m / (1.0 - ADAM_B1 ** ADAM_STEP)
    v_hat = v / (1.0 - ADAM_B2 ** ADAM_STEP)
    delta = -ADAM_LR * (m_hat / (jnp.sqrt(v_hat) + ADAM_EPS) + ADAM_WD * w)
    return delta, m, v


def _adamw_big(w, parts, m, v, name):
    rows, d = w.shape
    tr = _row_tile(rows, 176)

    def body(w_ref, p_ref, m_ref, v_ref, g_ref, d_ref, nm_ref, nv_ref):
        g = p_ref[0].astype(F32)
        for q in range(1, NCHIP):
            g = g + p_ref[q].astype(F32)
        g_ref[...] = g
        d_ref[...], nm_ref[...], nv_ref[...] = _adam(w_ref[...], g, m_ref[...], v_ref[...])

    spec = pl.BlockSpec((tr, d), lambda i: (i, 0))
    return _call(
        body,
        name=name,
        grid=(rows // tr,),
        in_specs=[spec, pl.BlockSpec((NCHIP, tr, d), lambda i: (0, i, 0)), spec, spec],
        out_specs=[spec] * 4,
        out_shape=[jax.ShapeDtypeStruct((rows, d), F32)] * 4,
        compiler_params=_seq(1),
    )(w, parts, m, v)


def _adamw_small(stats_all, pw_all, ws, ms, vs, pwid):
    nsmall = len(ws)

    def body(*refs):
        st_ref, pwa_ref = refs[0], refs[1]
        w_refs = refs[2:2 + nsmall]
        m_refs = refs[2 + nsmall:2 + 2 * nsmall]
        v_refs = refs[2 + 2 * nsmall:2 + 3 * nsmall]
        outs = refs[2 + 3 * nsmall:]
        st = st_ref[0]
        pwg = pwa_ref[0]
        for q in range(1, NDEV):
            st = st + st_ref[q]
            pwg = pwg + pwa_ref[q]
        grads = [st[0:1, :], st[1:2, :], st[2:3, :], st[3:4, :], st[4:5, 0:pwid], st[4:5, pwid:2 * pwid], pwg]
        outs[0][...] = jnp.zeros((1, GROUP), F32) + jnp.sum(st[5:6, :])
        for j in range(nsmall):
            delta, nm, nv = _adam(w_refs[j][...], grads[j], m_refs[j][...], v_refs[j][...])
            outs[1 + 4 * j][...] = grads[j]
            outs[2 + 4 * j][...] = delta
            outs[3 + 4 * j][...] = nm
            outs[4 + 4 * j][...] = nv

    out_shape = [jax.ShapeDtypeStruct((1, GROUP), F32)]
    for w in ws:
        out_shape += [jax.ShapeDtypeStruct(w.shape, F32)] * 4
    return _call(body, name="adamw_small", out_shape=out_shape, compiler_params=_params())(
        stats_all, pw_all, *ws, *ms, *vs
    )


def kernel(x, ffn1_norm, ffn1_w1, ffn1_w3, ffn1_w2, mix_norm, w_in, pool_w, pool_scale, ret_norm, w_out, ffn2_norm, ffn2_w1, ffn2_w3, ffn2_w2, final_norm, loss_target, m_ffn1_norm, m_ffn1_w1, m_ffn1_w3, m_ffn1_w2, m_mix_norm, m_w_in, m_pool_w, m_pool_scale, m_ret_norm, m_w_out, m_ffn2_norm, m_ffn2_w1, m_ffn2_w3, m_ffn2_w2, m_final_norm, v_ffn1_norm, v_ffn1_w1, v_ffn1_w3, v_ffn1_w2, v_mix_norm, v_w_in, v_pool_w, v_pool_scale, v_ret_norm, v_w_out, v_ffn2_norm, v_ffn2_w1, v_ffn2_w3, v_ffn2_w2, v_final_norm):
    s, d = x.shape[1], x.shape[2]
    ffn = ffn1_w1.shape[2] * NDEV
    pwid = pool_scale.shape[1]
    xs, tgt = x[0], loss_target[0]
    consts = _mix_constants(s)
    pw3 = pool_w[0]
    fnorm = final_norm.reshape(1, d)

    rows_of = lambda w, transposed: w[0].T if transposed else w[0]
    send_f1 = [rows_of(w, t).astype(BF16) for w, t in ((ffn1_w1, True), (ffn1_w3, True), (ffn1_w2, False))]
    later = [rows_of(w, t) for w, t in ((w_in, True), (w_out, False), (ffn2_w1, True), (ffn2_w3, True), (ffn2_w2, False))]

    sent_later, w13_f1 = _comm_call([_CastRows(later), _Gather(send_f1[:2])], "gather_ffn1")
    send_mix, send_f2 = sent_later[:2], sent_later[2:]
    (a1, b1, hm1), ((w2_f1, w_in_all),) = _ffn_up(xs, ffn1_norm, *w13_f1, carries=[_Gather(send_f1[2:] + send_mix[:1])])
    w_f1 = w13_f1 + [w2_f1]
    (h1,), ((w_out_all, w1_f2),) = _ffn_down(xs, hm1, w2_f1, carries=[_Gather(send_mix[1:] + send_f2[:1])])
    w_mix = [w_in_all, w_out_all]
    (h2, proj, o_saved, rsave), (rest,) = _mix_fwd(
        h1, mix_norm, w_mix, pw3, pool_scale, ret_norm, consts, carries=[_Gather(send_f2[1:])]
    )
    w_f2 = [w1_f2] + rest
    (dh3, a2, b2, hm2, dgf, loss_cols), _ = _ffn_fwd(h2, ffn2_norm, w_f2, ffn, head=(fnorm, tgt))

    (dh2, da2, db2, n2, dg2), _ = _ffn_bwd(dh3, h2, a2, b2, ffn2_norm, w_f2, ffn, "ffn2_bwd")
    sum_f2w1, _ = _wgrad(da2, n2, 1.0, "ffn2_w1_grad")
    sum_f2w3, _ = _wgrad(db2, n2, 1.0, "ffn2_w3_grad")
    sum_f2w2, ((parts_f2w1,),) = _wgrad(hm2, dh3, 0.5, "ffn2_w2_grad", carries=[_ChipScatter([sum_f2w1])])

    (dh1, dproj, u, mm, dpw, dps, drg, dgm), ((parts_f2w3, parts_f2w2),) = _mix_bwd(
        dh2, h1, proj, o_saved, rsave, mix_norm, w_mix, pw3, pool_scale, ret_norm, consts,
        carries=[_ChipScatter([sum_f2w3, sum_f2w2])],
    )
    (dx, da1, db1, n1, dg1), _ = _ffn_bwd(dh1, xs, a1, b1, ffn1_norm, w_f1, ffn, "ffn1_bwd")
    stats = jnp.concatenate(
        [dg1, dgm, dg2, dgf, jnp.concatenate([dps, drg], axis=1), loss_cols, jnp.zeros((2, d), F32)], axis=0
    )
    small = _GatherDirect([stats, dpw.reshape(N_POOL_GROUPS * GROUP, GROUP)])
    sum_f1w2, ((stats_all, pw_all),) = _wgrad(hm1, dh1, 0.5, "ffn1_w2_grad", carries=[small])
    sum_f1w1, ((parts_f1w2,),) = _wgrad(da1, n1, 1.0, "ffn1_w1_grad", carries=[_ChipScatter([sum_f1w2])])
    sum_f1w3, ((parts_f1w1,),) = _wgrad(db1, n1, 1.0, "ffn1_w3_grad", carries=[_ChipScatter([sum_f1w1])])
    sum_in, ((parts_f1w3,),) = _wgrad(dproj, u, 1.0, "w_in_grad", carries=[_ChipScatter([sum_f1w3])])
    sum_out, ((parts_in,),) = _wgrad(mm, dh2, 1.0, "w_out_grad", carries=[_ChipScatter([sum_in])])
    ((parts_out,),) = _comm_call([_ChipScatter([sum_out])], "scatter_last")

    big = (
        (ffn1_w1, m_ffn1_w1, v_ffn1_w1, parts_f1w1, True),
        (ffn1_w3, m_ffn1_w3, v_ffn1_w3, parts_f1w3, True),
        (ffn1_w2, m_ffn1_w2, v_ffn1_w2, parts_f1w2, False),
        (w_in, m_w_in, v_w_in, parts_in, True),
        (w_out, m_w_out, v_w_out, parts_out, False),
        (ffn2_w1, m_ffn2_w1, v_ffn2_w1, parts_f2w1, True),
        (ffn2_w3, m_ffn2_w3, v_ffn2_w3, parts_f2w3, True),
        (ffn2_w2, m_ffn2_w2, v_ffn2_w2, parts_f2w2, False),
    )
    big_out = []
    for j, (w, m, v, parts, t) in enumerate(big):
        view = (lambda a: a[0].T) if t else (lambda a: a[0])
        back = (lambda a: a.T[None]) if t else (lambda a: a[None])
        big_out.append([back(a) for a in _adamw_big(view(w), parts, view(m), view(v), "adamw_%d" % j)])

    small_w = (ffn1_norm, mix_norm, ffn2_norm, fnorm, pool_scale, ret_norm, pw3.reshape(-1, GROUP))
    small_m = (m_ffn1_norm, m_mix_norm, m_ffn2_norm, m_final_norm.reshape(1, d), m_pool_scale, m_ret_norm, m_pool_w.reshape(-1, GROUP))
    small_v = (v_ffn1_norm, v_mix_norm, v_ffn2_norm, v_final_norm.reshape(1, d), v_pool_scale, v_ret_norm, v_pool_w.reshape(-1, GROUP))
    res = _adamw_small(stats_all, pw_all, small_w, small_m, small_v, pwid)
    loss = res[0][0, 0]
    small_out = [list(res[1 + 4 * j:5 + 4 * j]) for j in range(len(small_w))]
    small_out[3] = [a.reshape(d) for a in small_out[3]]
    small_out[6] = [a.reshape(pool_w.shape) for a in small_out[6]]

    order = [small_out[0], big_out[0], big_out[1], big_out[2], small_out[1], big_out[3], small_out[6], small_out[4],
             small_out[5], big_out[4], small_out[2], big_out[5], big_out[6], big_out[7], small_out[3]]
    result = [loss, dx[None]]
    for kind in range(4):
        result += [t[kind] for t in order]
    return tuple(result)
```

```python
import functools

import numpy as np
import jax
import jax.numpy as jnp
from jax import lax
from jax.experimental import pallas as pl
from jax.experimental.pallas import tpu as pltpu

F32 = jnp.float32
BF16 = jnp.bfloat16

NDEV = 8
NCHIP = 4
EPS = 1e-6
N_POOL_GROUPS = 4
POOL_WINDOWS = (2, 4, 8, 16)
MAX_WINDOW = 16
GROUP = 128
RET_HEADS = 4
ROPE_BASE = 10000.0
ADAM_LR = 0.001
ADAM_B1 = 0.9
ADAM_B2 = 0.999
ADAM_EPS = 1e-08
ADAM_WD = 0.01
ADAM_STEP = 10

VMEM_LIMIT = 56 * 1024 * 1024
FFN_CHUNK = 256
ROW_BAND = 32

NT = (((1,), (1,)), ((), ()))
NN = (((1,), (0,)), ((), ()))
TN = (((0,), (0,)), ((), ()))

ANY = pl.BlockSpec(memory_space=pl.ANY)


def _dot(a, b, dims):
    return lax.dot_general(a, b, dims, preferred_element_type=F32)


def _call(body, **kw):
    return pl.pallas_call(body, **kw)


def _params(**kw):
    return pltpu.CompilerParams(vmem_limit_bytes=VMEM_LIMIT, **kw)


def _seq(n):
    return _params(dimension_semantics=("arbitrary",) * n)


def _peer(k):
    x, y, c = lax.axis_index("x"), lax.axis_index("y"), lax.axis_index("c")
    return (1 - x if k & 4 else x, 1 - y if k & 2 else y, 1 - c if k & 1 else c)


def _flat(pos):
    return 4 * pos[0] + 2 * pos[1] + pos[2]


def _chip(pos):
    return 2 * pos[0] + pos[1]


def _row_tile(rows, cap):
    return max(t for t in range(16, min(rows, cap) + 1, 16) if rows % t == 0)


def _pieces(rows, n):
    tiles = rows // 16
    cuts = [16 * (tiles * q // n) for q in range(n + 1)]
    return [(a, b - a) for a, b in zip(cuts[:-1], cuts[1:])]


def _load_weights(parts, sems):
    copies = []
    for buf, dst in parts:
        rows = buf.shape[1]
        for p in range(NDEV):
            cp = pltpu.make_async_copy(buf.at[p], dst.at[pl.ds(p * rows, rows), :], sems.at[len(copies)])
            cp.start()
            copies.append(cp)
    return copies


def _sigmoid(a):
    return 1.0 / (1.0 + jnp.exp(-a))


def _remote(src, dst, send_sem, recv_sem, to):
    return pltpu.make_async_remote_copy(
        src_ref=src, dst_ref=dst, send_sem=send_sem, recv_sem=recv_sem, device_id=to, device_id_type=pl.DeviceIdType.MESH
    )


class _Gather:
    X, Y, FAR = 4, 2, 6
    peers = (1, 2, 4)
    COPIES = 8

    def __init__(self, shards):
        n = len(shards)
        self.operands = list(shards)
        self.out_shape = [jax.ShapeDtypeStruct((NDEV,) + a.shape, a.dtype) for a in shards]
        self.sems = [
            pltpu.SemaphoreType.DMA((self.COPIES * n,)), pltpu.SemaphoreType.DMA((self.COPIES * n,)),
            pltpu.SemaphoreType.DMA((n,)),
        ]
        self.stages = [self.begin, self.relay, self.relay_far, self.end]

    def _copy(self, t, k, block, to, ins, outs, sems, own=False, half=None):
        rows = outs[t].shape[1]
        part = pl.ds(0, rows) if half is None else pl.ds(half * (rows // 2), rows // 2)
        dst = outs[t].at[_flat(block), part, :]
        at = self.COPIES * t + k
        return _remote(ins[t] if own else dst, dst, sems[0].at[at], sems[1].at[at], to)

    def _local(self, t, ins, outs, sems):
        return pltpu.make_async_copy(ins[t], outs[t].at[_flat(_peer(0))], sems[2].at[t])

    def begin(self, ins, outs, sems):
        me = _peer(0)
        for t in range(len(ins)):
            self._local(t, ins, outs, sems).start()
            for k, code in enumerate((1, self.X, self.Y)):
                self._copy(t, k, me, _peer(code), ins, outs, sems, own=True).start()

    def relay(self, ins, outs, sems):
        me, sibling = _peer(0), _peer(1)
        for t in range(len(ins)):
            self._copy(t, 1, _peer(self.X), me, ins, outs, sems).wait_recv()
            self._copy(t, 3, _peer(self.X), _peer(self.Y), ins, outs, sems, half=0).start()
            self._copy(t, 5, _peer(self.X), sibling, ins, outs, sems).start()
            self._copy(t, 2, _peer(self.Y), me, ins, outs, sems).wait_recv()
            self._copy(t, 4, _peer(self.Y), _peer(self.X), ins, outs, sems, half=1).start()
            self._copy(t, 6, _peer(self.Y), sibling, ins, outs, sems).start()

    def relay_far(self, ins, outs, sems):
        me, sibling = _peer(0), _peer(1)
        for t in range(len(ins)):
            self._copy(t, 3, _peer(self.FAR), me, ins, outs, sems, half=0).wait_recv()
            self._copy(t, 4, _peer(self.FAR), me, ins, outs, sems, half=1).wait_recv()
            self._copy(t, 7, _peer(self.FAR), sibling, ins, outs, sems).start()

    def end(self, ins, outs, sems):
        me = _peer(0)
        for t in range(len(ins)):
            self._copy(t, 0, _peer(1), me, ins, outs, sems).wait_recv()
            for k, code in ((5, self.X), (6, self.Y), (7, self.FAR)):
                self._copy(t, k, _peer(code ^ 1), me, ins, outs, sems).wait_recv()
            for k in range(self.COPIES):
                self._copy(t, k, me, me, ins, outs, sems, half=0 if k == 3 else 1 if k == 4 else None).wait_send()
            self._local(t, ins, outs, sems).wait()


class _GatherDirect:
    peers = tuple(range(1, NDEV))

    def __init__(self, arrays):
        n = len(arrays)
        self.operands = list(arrays)
        self.out_shape = [jax.ShapeDtypeStruct((NDEV,) + a.shape, a.dtype) for a in arrays]
        self.sems = [pltpu.SemaphoreType.DMA((7 * n,)), pltpu.SemaphoreType.DMA((7 * n,)), pltpu.SemaphoreType.DMA((n,))]
        self.stages = [self.begin, self.end]

    def begin(self, ins, outs, sems):
        mine = _flat(_peer(0))
        for t in range(len(ins)):
            pltpu.make_async_copy(ins[t], outs[t].at[mine], sems[2].at[t]).start()
            for k in range(1, NDEV):
                _remote(ins[t], outs[t].at[mine], sems[0].at[7 * t + k - 1], sems[1].at[7 * t + k - 1], _peer(k)).start()

    def end(self, ins, outs, sems):
        mine = _flat(_peer(0))
        for t in range(len(ins)):
            for k in range(1, NDEV):
                cp = _remote(ins[t], outs[t].at[_flat(_peer(k))], sems[0].at[7 * t + k - 1], sems[1].at[7 * t + k - 1], _peer(k))
                cp.wait_recv()
                cp.wait_send()
            pltpu.make_async_copy(ins[t], outs[t].at[mine], sems[2].at[t]).wait()


class _ChipScatter:
    peers = (2, 4, 6)
    pieces = 1

    def __init__(self, sums):
        n = len(sums) * NCHIP * self.pieces
        self.operands = list(sums)
        self.out_shape = [jax.ShapeDtypeStruct(a.shape, a.dtype) for a in sums]
        self.sems = [pltpu.SemaphoreType.DMA((n,)), pltpu.SemaphoreType.DMA((n,))]
        self.stages = [self.begin, self.end]

    def _copies(self, ins, outs, sems, arriving):
        mine = _chip(_peer(0))
        copies = []
        for t in range(len(ins)):
            rows = ins[t].shape[1] // self.pieces
            for k in (0, 4, 2, 6):
                other = _chip(_peer(k))
                for q in range(self.pieces):
                    part = pl.ds(q * rows, rows)
                    at = len(copies)
                    if k == 0:
                        cp = pltpu.make_async_copy(ins[t].at[mine, part, :], outs[t].at[mine, part, :], sems[0].at[at])
                    else:
                        landing = outs[t].at[other if arriving else mine, part, :]
                        cp = _remote(ins[t].at[other, part, :], landing, sems[0].at[at], sems[1].at[at], _peer(k))
                    copies.append(cp)
        return copies

    def begin(self, ins, outs, sems):
        for cp in self._copies(ins, outs, sems, False):
            cp.start()

    def end(self, ins, outs, sems):
        for at, cp in enumerate(self._copies(ins, outs, sems, True)):
            if at % (NCHIP * self.pieces) < self.pieces:
                cp.wait()
            else:
                cp.wait_recv()
                cp.wait_send()


class _CastRows:
    peers = ()

    def __init__(self, arrays):
        n = len(arrays)
        self.operands = list(arrays)
        self.out_shape = [jax.ShapeDtypeStruct(a.shape, BF16) for a in arrays]
        self.sems = [pltpu.SemaphoreType.DMA((n,)), pltpu.SemaphoreType.DMA((n,))]
        self.sems += [pltpu.VMEM(a.shape, F32) for a in arrays] + [pltpu.VMEM(a.shape, BF16) for a in arrays]
        self.stages = [self.begin, self.convert, self.end]

    def _moves(self, t, ins, outs, scratch):
        n = len(ins)
        load = pltpu.make_async_copy(ins[t], scratch[2 + t], scratch[0].at[t])
        store = pltpu.make_async_copy(scratch[2 + n + t], outs[t], scratch[1].at[t])
        return load, store

    def begin(self, ins, outs, scratch):
        for t in range(len(ins)):
            self._moves(t, ins, outs, scratch)[0].start()

    def convert(self, ins, outs, scratch):
        n = len(ins)
        for t in range(n):
            load, store = self._moves(t, ins, outs, scratch)
            load.wait()
            scratch[2 + n + t][...] = scratch[2 + t][...].astype(BF16)
            store.start()

    def end(self, ins, outs, scratch):
        for t in range(len(ins)):
            self._moves(t, ins, outs, scratch)[1].wait()


def _split_refs(refs, counts):
    out, at = [], 0
    for n in counts:
        out.append(refs[at:at + n])
        at += n
    return out


BARRIER_IDS = {(2, 4, 6): 0, (1, 2, 4): 1, (1,): 2, (1, 2, 4, 6): 3, tuple(range(1, NDEV)): 4}


def _peers_of(carries, own=()):
    peers = tuple(sorted(set(own).union(*[c.peers for c in carries])))
    return (peers, BARRIER_IDS[peers]) if peers in BARRIER_IDS else (None, None)


def _handshake(peers):
    barrier = pltpu.get_barrier_semaphore()
    for k in peers:
        pl.semaphore_signal(barrier, inc=1, device_id=_peer(k), device_id_type=pl.DeviceIdType.MESH)
    pl.semaphore_wait(barrier, len(peers))


def _comm_call(carries, name):
    nin = [len(c.operands) for c in carries]
    nout = [len(c.out_shape) for c in carries]
    nsem = [len(c.sems) for c in carries]
    peers, collective_id = _peers_of(carries)

    def body(*refs):
        if peers:
            _handshake(peers)
        ins, outs, sems = _split_refs(refs, (sum(nin), sum(nout), sum(nsem)))
        parts = list(zip(carries, _split_refs(ins, nin), _split_refs(outs, nout), _split_refs(sems, nsem)))
        for depth in range(max(len(c.stages) for c in carries)):
            for c, i, o, s in parts:
                if depth < len(c.stages) - 1:
                    c.stages[depth](i, o, s)
        for c, i, o, s in parts:
            c.stages[-1](i, o, s)

    res = _call(
        body,
        name=name,
        out_shape=[sh for c in carries for sh in c.out_shape],
        in_specs=[ANY] * sum(nin),
        out_specs=[ANY] * sum(nout),
        scratch_shapes=[sm for c in carries for sm in c.sems],
        compiler_params=_params(has_side_effects=True, collective_id=collective_id),
    )(*[a for c in carries for a in c.operands])
    return _split_refs(list(res), nout)


def _grid_call(body, carries, *, name, grid, in_specs, out_specs, out_shape, scratch_shapes, args, own_peers=()):
    ni, no, ns = len(in_specs), len(out_specs), len(scratch_shapes)
    nin = [len(c.operands) for c in carries]
    nout = [len(c.out_shape) for c in carries]
    nsem = [len(c.sems) for c in carries]
    steps = int(np.prod(grid))
    peers, collective_id = _peers_of(carries, own_peers)

    def when_of(stage, count):
        first, last = (5 * steps) // 8 - 1, steps - 2
        return max(0, last if count <= 3 else first + (last - first) * (stage - 1) // (count - 3))

    def wrapped(*refs):
        ins, cins, outs, couts, scr, csems = _split_refs(refs, (ni, sum(nin), no, sum(nout), ns, sum(nsem)))
        if not carries and not peers:
            return body(*ins, *outs, *scr)
        parts = list(zip(carries, _split_refs(cins, nin), _split_refs(couts, nout), _split_refs(csems, nsem)))
        step = pl.program_id(0)
        for axis in range(1, len(grid)):
            step = step * grid[axis] + pl.program_id(axis)

        @pl.when(step == 0)
        def _():
            if peers:
                _handshake(peers)
            for c, i, o, s in parts:
                c.stages[0](i, o, s)

        body(*ins, *outs, *scr)

        for c, i, o, s in parts:
            for stage in range(1, len(c.stages) - 1):
                pl.when(step == when_of(stage, len(c.stages)))(functools.partial(c.stages[stage], i, o, s))

        @pl.when(step == steps - 1)
        def _():
            for c, i, o, s in parts:
                c.stages[-1](i, o, s)

    res = _call(
        wrapped,
        name=name,
        grid=tuple(grid),
        in_specs=list(in_specs) + [ANY] * sum(nin),
        out_specs=list(out_specs) + [ANY] * sum(nout),
        out_shape=list(out_shape) + [sh for c in carries for sh in c.out_shape],
        scratch_shapes=list(scratch_shapes) + [sm for c in carries for sm in c.sems],
        compiler_params=_params(dimension_semantics=("arbitrary",) * len(grid), collective_id=collective_id),
    )(*args, *[a for c in carries for a in c.operands])
    res = list(res)
    return res[:no], _split_refs(res[no:], nout)


def _chunks(width):
    return [(at, min(FFN_CHUNK, width - at)) for at in range(0, width, FFN_CHUNK)]


def _ffn_fwd_loss(x, gain, weights, final_gain, target):
    s, d = x.shape
    ffn = weights[0].shape[0] * weights[0].shape[1]
    tm = min(512, s)

    def body(x_ref, g_ref, b1, b3, b2, gf_ref, t_ref, dh_ref, a_ref, b_ref, hm_ref, dgf_ref, loss_ref, w1s, w3s, w2s, sems):
        @pl.when(pl.program_id(0) == 0)
        def _():
            for cp in _load_weights(((b1, w1s), (b3, w3s), (b2, w2s)), sems):
                cp.wait()
            dgf_ref[...] = jnp.zeros_like(dgf_ref)
            loss_ref[...] = jnp.zeros_like(loss_ref)

        xv = x_ref[...]
        r = lax.rsqrt(jnp.mean(xv * xv, axis=-1, keepdims=True) + EPS)
        n = (xv * r * g_ref[...]).astype(BF16)
        acc = jnp.zeros((tm, d), F32)
        for at, width in _chunks(ffn):
            cols = slice(at, at + width)
            a = _dot(n, w1s[cols, :], NT)
            b = _dot(n, w3s[cols, :], NT)
            a_ref[:, cols] = a.astype(BF16)
            b_ref[:, cols] = b.astype(BF16)
            hm = (a * _sigmoid(a) * b).astype(BF16)
            hm_ref[:, cols] = hm
            acc = acc + _dot(hm, w2s[cols, :], NN)
        h = xv + 0.5 * acc
        rf = lax.rsqrt(jnp.mean(h * h, axis=-1, keepdims=True) + EPS)
        nh = h * rf
        gf = gf_ref[...]
        err = nh * gf - t_ref[...]
        loss_ref[...] += jnp.sum(err * err, axis=0, keepdims=True) * (0.5 / d)
        dy = err * (1.0 / d)
        dgf_ref[...] += jnp.sum(dy * nh, axis=0, keepdims=True)
        dn = dy * gf
        dh_ref[...] = rf * (dn - nh * jnp.mean(dn * nh, axis=-1, keepdims=True))

    tile = pl.BlockSpec((tm, d), lambda i: (i, 0))
    row = pl.BlockSpec((1, d), lambda i: (0, 0))
    wide = pl.BlockSpec((tm, ffn), lambda i: (i, 0))
    return _call(
        body,
        name="ffn_fwd_loss",
        grid=(s // tm,),
        in_specs=[tile, row, ANY, ANY, ANY, row, tile],
        out_specs=[tile, wide, wide, wide, row, row],
        out_shape=[jax.ShapeDtypeStruct((s, d), F32)] + [jax.ShapeDtypeStruct((s, ffn), BF16)] * 3
        + [jax.ShapeDtypeStruct((1, d), F32)] * 2,
        scratch_shapes=[pltpu.VMEM((ffn, d), BF16)] * 3 + [pltpu.SemaphoreType.DMA((3 * NDEV,))],
        compiler_params=_seq(1),
    )(x, gain, *weights, final_gain, target)


def _ffn_up(x, gain, w1, w3, carries=()):
    s, d = x.shape
    ffn = w1.shape[0] * w1.shape[1]
    tm = min(512, s)

    def body(x_ref, g_ref, b1, b3, a_ref, b_ref, hm_ref, w1s, w3s, sems):
        @pl.when(pl.program_id(0) == 0)
        def _():
            for cp in _load_weights(((b1, w1s), (b3, w3s)), sems):
                cp.wait()

        xv = x_ref[...]
        r = lax.rsqrt(jnp.mean(xv * xv, axis=-1, keepdims=True) + EPS)
        n = (xv * r * g_ref[...]).astype(BF16)
        for at, width in _chunks(ffn):
            cols = slice(at, at + width)
            a = _dot(n, w1s[cols, :], NT)
            b = _dot(n, w3s[cols, :], NT)
            a_ref[:, cols] = a.astype(BF16)
            b_ref[:, cols] = b.astype(BF16)
            hm_ref[:, cols] = (a * _sigmoid(a) * b).astype(BF16)

    wide = pl.BlockSpec((tm, ffn), lambda i: (i, 0))
    return _grid_call(
        body,
        carries,
        name="ffn_up",
        grid=(s // tm,),
        in_specs=[pl.BlockSpec((tm, d), lambda i: (i, 0)), pl.BlockSpec((1, d), lambda i: (0, 0)), ANY, ANY],
        out_specs=[wide] * 3,
        out_shape=[jax.ShapeDtypeStruct((s, ffn), BF16)] * 3,
        scratch_shapes=[pltpu.VMEM((ffn, d), BF16)] * 2 + [pltpu.SemaphoreType.DMA((2 * NDEV,))],
        args=[x, gain, w1, w3],
    )


def _ffn_down(x, hm, w2, carries=()):
    s, d = x.shape
    ffn = w2.shape[0] * w2.shape[1]
    tm = min(512, s)

    def body(x_ref, hm_ref, b2, h_ref, w2s, sems):
        @pl.when(pl.program_id(0) == 0)
        def _():
            for cp in _load_weights(((b2, w2s),), sems):
                cp.wait()

        acc = jnp.zeros((tm, d), F32)
        for at, width in _chunks(ffn):
            cols = slice(at, at + width)
            acc = acc + _dot(hm_ref[:, cols], w2s[cols, :], NN)
        h_ref[...] = x_ref[...] + 0.5 * acc

    tile = pl.BlockSpec((tm, d), lambda i: (i, 0))
    return _grid_call(
        body,
        carries,
        name="ffn_down",
        grid=(s // tm,),
        in_specs=[tile, pl.BlockSpec((tm, ffn), lambda i: (i, 0)), ANY],
        out_specs=[tile],
        out_shape=[jax.ShapeDtypeStruct((s, d), F32)],
        scratch_shapes=[pltpu.VMEM((ffn, d), BF16), pltpu.SemaphoreType.DMA((NDEV,))],
        args=[x, hm, w2],
    )


def _ffn_bwd(dh, x, a, b, gain, weights, ffn, name, carries=()):
    s, d = x.shape
    tm = min(512, s)
    halves = 2
    fh = ffn // halves

    def body(dh_ref, x_ref, a_ref, b_ref, g_ref, b1, b3, b2, dx_ref, da_ref, db_ref, n_ref, dg_ref, w1s, w3s, w2s, sems):
        i, j = pl.program_id(0), pl.program_id(1)

        @pl.when((i == 0) & (j == 0))
        def _():
            for cp in _load_weights(((b1, w1s), (b3, w3s), (b2, w2s)), sems):
                cp.wait()
            dg_ref[...] = jnp.zeros_like(dg_ref)

        @pl.when(j == 0)
        def _():
            dx_ref[...] = jnp.zeros_like(dx_ref)

        dob = (0.5 * dh_ref[...]).astype(BF16)
        chunks = _chunks(fh)

        def dhm_of(k):
            at, width = chunks[k]
            return _dot(dob, w2s[pl.ds(pl.multiple_of(j * fh + at, GROUP), width), :], NT)

        ahead = dhm_of(0)
        for k, (at, width) in enumerate(chunks):
            cols = slice(at, at + width)
            dhm = ahead
            if k + 1 < len(chunks):
                ahead = dhm_of(k + 1)
            for top in range(0, tm, ROW_BAND):
                band = slice(top, top + ROW_BAND)
                av = a_ref[band, cols].astype(F32)
                bv = b_ref[band, cols].astype(F32)
                sg = _sigmoid(av)
                dv = dhm[band]
                da_ref[band, cols] = (dv * bv * (sg * (1.0 + av * (1.0 - sg)))).astype(BF16)
                db_ref[band, cols] = (dv * (av * sg)).astype(BF16)
        half = pl.ds(pl.multiple_of(j * fh, GROUP), fh)
        dx_ref[...] += _dot(da_ref[...], w1s[half, :], NN) + _dot(db_ref[...], w3s[half, :], NN)

        @pl.when(j == halves - 1)
        def _():
            xv = x_ref[...]
            g = g_ref[...]
            r = lax.rsqrt(jnp.mean(xv * xv, axis=-1, keepdims=True) + EPS)
            nh = xv * r
            n_ref[...] = (nh * g).astype(BF16)
            total = dx_ref[...]
            dg_ref[...] += jnp.sum(total * nh, axis=0, keepdims=True)
            dnh = total * g
            dx_ref[...] = dh_ref[...] + r * (dnh - nh * jnp.mean(dnh * nh, axis=-1, keepdims=True))

    tile = pl.BlockSpec((tm, d), lambda i, j: (i, 0))
    row = pl.BlockSpec((1, d), lambda i, j: (0, 0))
    wide = pl.BlockSpec((tm, fh), lambda i, j: (i, j))
    return _grid_call(
        body,
        carries,
        name=name,
        grid=(s // tm, halves),
        in_specs=[tile, tile, wide, wide, row, ANY, ANY, ANY],
        out_specs=[tile, wide, wide, tile, row],
        out_shape=[
            jax.ShapeDtypeStruct((s, d), F32),
            jax.ShapeDtypeStruct((s, ffn), BF16),
            jax.ShapeDtypeStruct((s, ffn), BF16),
            jax.ShapeDtypeStruct((s, d), BF16),
            jax.ShapeDtypeStruct((1, d), F32),
        ],
        scratch_shapes=[pltpu.VMEM((ffn, d), BF16)] * 3 + [pltpu.SemaphoreType.DMA((3 * NDEV,))],
        args=[dh, x, a, b, gain] + list(weights),
    )


SWAP_PIECES = 1


def _wgrad(lhs, rhs, scale, name, carries=()):
    s, m = lhs.shape
    n = rhs.shape[1]
    rs = m // NDEV
    tk = min(1024, s)
    steps = s // tk
    pieces = [(j, at, size) for j in range(2) for at, size in _pieces(rs, SWAP_PIECES)]

    def body(l_ref, r_ref, o_ref, acc, mine, theirs, send_sems, recv_sems):
        h, k = pl.program_id(0), pl.program_id(1)

        @pl.when(k == 0)
        def _():
            acc[...] = _dot(l_ref[...], r_ref[...].astype(BF16), TN)

        @pl.when(k > 0)
        def _():
            acc[...] += _dot(l_ref[...], r_ref[...].astype(BF16), TN)

        def exchange(half):
            c = lax.axis_index("c")
            return [
                _remote(mine.at[half, 1 - c, j, pl.ds(at, size), :], theirs.at[half, j, pl.ds(at, size), :],
                        send_sems.at[half * len(pieces) + q], recv_sems.at[half * len(pieces) + q], _peer(1))
                for q, (j, at, size) in enumerate(pieces)
            ]

        def settle(half):
            for cp in exchange(half):
                cp.wait_recv()
            both = mine[half, lax.axis_index("c")].astype(F32) + theirs[half].astype(F32)
            o_ref[2 * half:2 * half + 2] = both.astype(BF16)
            for cp in exchange(half):
                cp.wait_send()

        for half in range(2):
            @pl.when((h == half) & (k == steps - 1))
            def _():
                for p in range(NCHIP):
                    mine[half, p % 2, p // 2] = (acc[p * rs:(p + 1) * rs, :] * scale).astype(BF16)
                for cp in exchange(half):
                    cp.start()
                if half == 1:
                    settle(0)
                    settle(1)

    (out,), carried = _grid_call(
        body,
        carries,
        name=name,
        grid=(2, steps),
        in_specs=[pl.BlockSpec((tk, m // 2), lambda h, k: (k, h)), pl.BlockSpec((tk, n), lambda h, k: (k, 0))],
        out_specs=[pl.BlockSpec((NCHIP, rs, n), lambda h, k: (0, 0, 0))],
        out_shape=[jax.ShapeDtypeStruct((NCHIP, rs, n), BF16)],
        scratch_shapes=[
            pltpu.VMEM((m // 2, n), F32), pltpu.VMEM((2, 2, 2, rs, n), BF16), pltpu.VMEM((2, 2, rs, n), BF16),
            pltpu.SemaphoreType.DMA((2 * len(pieces),)), pltpu.SemaphoreType.DMA((2 * len(pieces),)),
        ],
        args=[lhs, rhs],
        own_peers=(1,),
    )
    return out, carried


def _mix_constants(s):
    c = GROUP
    lg = np.log1p(-np.exp2(-5.0 - np.arange(RET_HEADS, dtype=np.float32))).astype(np.float32)
    pos = np.arange(c, dtype=np.float32)
    rel = pos[:, None] - pos[None, :]
    decay = np.where(rel[None] >= 0, np.exp(lg[:, None, None] * np.maximum(rel, 0.0)[None]), 0.0).astype(np.float32)
    ktail = np.exp(lg[:, None] * (c - 1 - pos)[None, :]).astype(np.float32)
    qhead = np.exp(lg[:, None] * (pos + 1.0)[None, :]).astype(np.float32)
    chunk_decay = [float(v) for v in np.exp(lg * np.float32(c)).astype(np.float32)]
    ones = np.ones((1, 1, c), np.float32)
    inv_freq = (1.0 / (np.float32(ROPE_BASE) ** (np.arange(0, c, 2, dtype=np.float32) / np.float32(c)))).astype(np.float32)
    ang = (np.arange(s, dtype=np.float32)[:, None] * inv_freq[None, :]).astype(np.float32)
    cos, sin = np.cos(ang).astype(np.float32), np.sin(ang).astype(np.float32)
    return dict(
        decay=jnp.asarray(decay),
        ktail=jnp.asarray(ktail[:, :, None] * ones),
        qhead=jnp.asarray(qhead[:, :, None] * ones),
        chunk_decay=chunk_decay,
        cos=jnp.asarray(np.concatenate([cos, cos], axis=-1)),
        sin=jnp.asarray(np.concatenate([-sin, sin], axis=-1)),
    )


def _rope(t, cos, sin):
    return t * cos + pltpu.roll(t, GROUP // 2, axis=1) * sin


def _rope_bwd(dt, cos, sin):
    return dt * cos + pltpu.roll(dt * sin, GROUP // 2, axis=1)


def _window_sums(ext, w, forward):
    rows = ext.shape[0]
    acc, k = ext, 1
    while k < w:
        acc = acc + pltpu.roll(acc, k if forward else rows - k, axis=0)
        k *= 2
    return acc


def _pool_counts(tile, tm, w):
    t = lax.broadcasted_iota(jnp.int32, (tm, 1), 0) + tile * tm
    return jnp.minimum(t + 1, w).astype(F32)


def _mix_fwd(h1, gain, weights, pool_w, pool_scale, ret_gain, consts, carries=()):
    s, d = h1.shape
    pwid = N_POOL_GROUPS * GROUP
    rwid = RET_HEADS * GROUP
    inw = pwid + 4 * rwid
    tm = min(256, s)
    nck = tm // GROUP
    cd = consts["chunk_decay"]

    def body(h_ref, g_ref, bin_, bout, pw_ref, ps_ref, rg_ref, cos_ref, sin_ref, dec_ref, kt_ref, qh_ref,
             h2_ref, proj_ref, o_ref, rs_ref, wins, wouts, state, carry, mbuf, sems):
        i = pl.program_id(0)

        @pl.when(i == 0)
        def _():
            for cp in _load_weights(((bin_, wins), (bout, wouts)), sems):
                cp.wait()
            state[...] = jnp.zeros_like(state)
            carry[...] = jnp.zeros_like(carry)

        hv = h_ref[...]
        r = lax.rsqrt(jnp.mean(hv * hv, axis=-1, keepdims=True) + EPS)
        u = (hv * r * g_ref[...]).astype(BF16)
        proj_ref[...] = _dot(u, wins[...], NT)

        ext = jnp.concatenate([carry[...], proj_ref[:, 0:pwid]], axis=0)
        carry[...] = proj_ref[tm - MAX_WINDOW:tm, 0:pwid]
        for gi, w in enumerate(POOL_WINDOWS):
            cols = slice(gi * GROUP, (gi + 1) * GROUP)
            xg = ext[:, cols]
            ws = _window_sums(xg, w, True)[MAX_WINDOW:, :]
            pooled = ws / _pool_counts(i, tm, w) - xg[MAX_WINDOW:, :]
            z = _dot(pooled.astype(BF16), pw_ref[gi].astype(BF16), NN)
            mbuf[:, cols] = (z * ps_ref[:, cols]).astype(BF16)

        cos, sin = cos_ref[...], sin_ref[...]
        for h in range(RET_HEADS):
            cq = slice(pwid + h * GROUP, pwid + (h + 1) * GROUP)
            ck = slice(pwid + rwid + h * GROUP, pwid + rwid + (h + 1) * GROUP)
            cv = slice(pwid + 2 * rwid + h * GROUP, pwid + 2 * rwid + (h + 1) * GROUP)
            cg = slice(pwid + 3 * rwid + h * GROUP, pwid + 3 * rwid + (h + 1) * GROUP)
            ch = slice(h * GROUP, (h + 1) * GROUP)
            qr = _rope(proj_ref[:, cq], cos, sin)
            kr = _rope(proj_ref[:, ck], cos, sin) * (GROUP ** -0.5)
            vb = proj_ref[:, cv].astype(BF16)
            for n in range(nck):
                rows = slice(n * GROUP, (n + 1) * GROUP)
                qc, kc, vc = qr[rows], kr[rows], vb[rows]
                rb = state[h]
                rs_ref[n, h] = rb
                p = (_dot(qc.astype(BF16), kc.astype(BF16), NT) * dec_ref[h]).astype(BF16)
                o = _dot(p, vc, NN) + _dot((qc * qh_ref[h]).astype(BF16), rb.astype(BF16), NN)
                state[h] = cd[h] * rb + _dot((kc * kt_ref[h]).astype(BF16), vc, TN)
                o_ref[rows, ch] = o
                on = o * lax.rsqrt(jnp.mean(o * o, axis=-1, keepdims=True) + EPS)
                gv = proj_ref[rows, cg]
                mbuf[rows, pwid + h * GROUP:pwid + (h + 1) * GROUP] = (
                    gv * _sigmoid(gv) * (on * rg_ref[:, ch])
                ).astype(BF16)
        h2_ref[...] = hv + _dot(mbuf[...], wouts[...], NN)

    tile = pl.BlockSpec((tm, d), lambda i: (i, 0))
    full = lambda shape: pl.BlockSpec(shape, lambda i: (0,) * len(shape))
    return _grid_call(
        body,
        carries,
        name="mix_fwd",
        grid=(s // tm,),
        in_specs=[
            tile, full((1, d)), ANY, ANY,
            full((N_POOL_GROUPS, GROUP, GROUP)), full((1, pwid)), full((1, rwid)),
            pl.BlockSpec((tm, GROUP), lambda i: (i, 0)), pl.BlockSpec((tm, GROUP), lambda i: (i, 0)),
            full((RET_HEADS, GROUP, GROUP)), full((RET_HEADS, GROUP, GROUP)), full((RET_HEADS, GROUP, GROUP)),
        ],
        out_specs=[
            tile,
            pl.BlockSpec((tm, inw), lambda i: (i, 0)),
            pl.BlockSpec((tm, rwid), lambda i: (i, 0)),
            pl.BlockSpec((nck, RET_HEADS, GROUP, GROUP), lambda i: (i, 0, 0, 0)),
        ],
        out_shape=[
            jax.ShapeDtypeStruct((s, d), F32),
            jax.ShapeDtypeStruct((s, inw), F32),
            jax.ShapeDtypeStruct((s, rwid), F32),
            jax.ShapeDtypeStruct((s // GROUP, RET_HEADS, GROUP, GROUP), F32),
        ],
        scratch_shapes=[
            pltpu.VMEM((inw, d), BF16), pltpu.VMEM((d, d), BF16),
            pltpu.VMEM((RET_HEADS, GROUP, GROUP), F32), pltpu.VMEM((MAX_WINDOW, pwid), F32),
            pltpu.VMEM((tm, d), BF16), pltpu.SemaphoreType.DMA((2 * NDEV,)),
        ],
        args=[h1, gain, weights[0], weights[1], pool_w, pool_scale, ret_gain,
              consts["cos"], consts["sin"], consts["decay"], consts["ktail"], consts["qhead"]],
    )


def _mix_bwd(dh2, h1, proj, o_saved, rsave, gain, weights, pool_w, pool_scale, ret_gain, consts, carries=()):
    s, d = h1.shape
    pwid = N_POOL_GROUPS * GROUP
    rwid = RET_HEADS * GROUP
    inw = pwid + 4 * rwid
    tm = min(256, s)
    nck = tm // GROUP
    nt = s // tm
    cd = consts["chunk_decay"]
    halo_per_tile = tm // MAX_WINDOW

    def body(dh2_ref, h_ref, proj_ref, halo_ref, o_ref, rs_ref, g_ref, bin_, bout, pw_ref, ps_ref, rg_ref,
             cos_ref, sin_ref, dec_ref, kt_ref, qh_ref,
             dh1_ref, dproj_ref, u_ref, m_ref, dpw_ref, dps_ref, drg_ref, dg_ref,
             wins, wouts, dstate, carry, dm, dpj, sems):
        i = pl.program_id(0)
        tile = nt - 1 - i

        @pl.when(i == 0)
        def _():
            for cp in _load_weights(((bin_, wins), (bout, wouts)), sems):
                cp.wait()
            dstate[...] = jnp.zeros_like(dstate)
            carry[...] = jnp.zeros_like(carry)
            for ref in (dpw_ref, dps_ref, drg_ref, dg_ref):
                ref[...] = jnp.zeros_like(ref)

        dh2v = dh2_ref[...]
        dm[...] = _dot(dh2v.astype(BF16), wouts[...], NT)
        hv = h_ref[...]
        g = g_ref[...]
        r = lax.rsqrt(jnp.mean(hv * hv, axis=-1, keepdims=True) + EPS)
        uh = hv * r
        u_ref[...] = (uh * g).astype(BF16)

        halo = jnp.where(tile == 0, 0.0, halo_ref[...])
        ext = jnp.concatenate([halo, proj_ref[:, 0:pwid]], axis=0)
        next_dpn = carry[...]
        for gi, w in enumerate(POOL_WINDOWS):
            cols = slice(gi * GROUP, (gi + 1) * GROUP)
            xg = ext[:, cols]
            cnt = _pool_counts(tile, tm, w)
            pooled = (_window_sums(xg, w, True)[MAX_WINDOW:, :] / cnt - xg[MAX_WINDOW:, :]).astype(BF16)
            pwb = pw_ref[gi].astype(BF16)
            z = _dot(pooled, pwb, NN)
            scale = ps_ref[:, cols]
            m_ref[:, cols] = (z * scale).astype(BF16)
            da = dm[:, cols]
            dps_ref[:, cols] += jnp.sum(da * z, axis=0, keepdims=True)
            dz = (da * scale).astype(BF16)
            dpw_ref[gi] += _dot(pooled, dz, TN)
            dpl = _dot(dz, pwb, NT)
            dpn = dpl / cnt
            ext2 = jnp.concatenate([dpn, next_dpn[:, cols]], axis=0)
            dpj[:, cols] = (_window_sums(ext2, w, False)[0:tm, :] - dpl).astype(BF16)
            carry[:, cols] = dpn[0:MAX_WINDOW, :]

        cos, sin = cos_ref[...], sin_ref[...]
        for h in range(RET_HEADS):
            cq = slice(pwid + h * GROUP, pwid + (h + 1) * GROUP)
            ck = slice(pwid + rwid + h * GROUP, pwid + rwid + (h + 1) * GROUP)
            cv = slice(pwid + 2 * rwid + h * GROUP, pwid + 2 * rwid + (h + 1) * GROUP)
            cg = slice(pwid + 3 * rwid + h * GROUP, pwid + 3 * rwid + (h + 1) * GROUP)
            ch = slice(h * GROUP, (h + 1) * GROUP)
            qr = _rope(proj_ref[:, cq], cos, sin)
            kr = _rope(proj_ref[:, ck], cos, sin) * (GROUP ** -0.5)
            vb = proj_ref[:, cv].astype(BF16)
            gv = proj_ref[:, cg]
            ov = o_ref[:, ch]
            ro = lax.rsqrt(jnp.mean(ov * ov, axis=-1, keepdims=True) + EPS)
            on = ov * ro
            rg = rg_ref[:, ch]
            db = dm[:, pwid + h * GROUP:pwid + (h + 1) * GROUP]
            sg = _sigmoid(gv)
            sl = gv * sg
            m_ref[:, pwid + h * GROUP:pwid + (h + 1) * GROUP] = (sl * (on * rg)).astype(BF16)
            dpj[:, cg] = (db * (on * rg) * (sg * (1.0 + gv * (1.0 - sg)))).astype(BF16)
            drg_ref[:, ch] += jnp.sum(db * sl * on, axis=0, keepdims=True)
            don = db * sl * rg
            do = (ro * (don - on * jnp.mean(don * on, axis=-1, keepdims=True))).astype(BF16)
            for n in reversed(range(nck)):
                rows = slice(n * GROUP, (n + 1) * GROUP)
                qc, kc, vc, dob = qr[rows], kr[rows], vb[rows], do[rows]
                qcb, kcb = qc.astype(BF16), kc.astype(BF16)
                qh = (qc * qh_ref[h]).astype(BF16)
                kt = (kc * kt_ref[h]).astype(BF16)
                rn = rs_ref[n, h].astype(BF16)
                dnext = dstate[h]
                dnb = dnext.astype(BF16)
                dec = dec_ref[h]
                p = (_dot(qcb, kcb, NT) * dec).astype(BF16)
                ds = (_dot(dob, vc, NT) * dec).astype(BF16)
                dv = _dot(p, dob, TN) + _dot(kt, dnb, NN)
                dq = _dot(ds, kcb, NN) + _dot(dob, rn, NT) * qh_ref[h]
                dk = _dot(ds, qcb, TN) + _dot(vc, dnb, NT) * kt_ref[h]
                dstate[h] = cd[h] * dnext + _dot(qh, dob, TN)
                dpj[rows, cq] = _rope_bwd(dq, cos[rows], sin[rows]).astype(BF16)
                dpj[rows, ck] = _rope_bwd(dk * (GROUP ** -0.5), cos[rows], sin[rows]).astype(BF16)
                dpj[rows, cv] = dv.astype(BF16)

        dproj_ref[...] = dpj[...]
        du = _dot(dpj[...], wins[...], NN)
        dg_ref[...] += jnp.sum(du * uh, axis=0, keepdims=True)
        dn = du * g
        dh1_ref[...] = dh2v + r * (dn - uh * jnp.mean(dn * uh, axis=-1, keepdims=True))

    rev = lambda i: (nt - 1 - i, 0)
    tile = pl.BlockSpec((tm, d), rev)
    full = lambda shape: pl.BlockSpec(shape, lambda i: (0,) * len(shape))
    return _grid_call(
        body,
        carries,
        name="mix_bwd",
        grid=(nt,),
        in_specs=[
            tile, tile,
            pl.BlockSpec((tm, inw), rev),
            pl.BlockSpec((MAX_WINDOW, pwid), lambda i: (jnp.maximum((nt - 1 - i) * halo_per_tile - 1, 0), 0)),
            pl.BlockSpec((tm, rwid), rev),
            pl.BlockSpec((nck, RET_HEADS, GROUP, GROUP), lambda i: (nt - 1 - i, 0, 0, 0)),
            full((1, d)), ANY, ANY,
            full((N_POOL_GROUPS, GROUP, GROUP)), full((1, pwid)), full((1, rwid)),
            pl.BlockSpec((tm, GROUP), rev), pl.BlockSpec((tm, GROUP), rev),
            full((RET_HEADS, GROUP, GROUP)), full((RET_HEADS, GROUP, GROUP)), full((RET_HEADS, GROUP, GROUP)),
        ],
        out_specs=[
            tile, pl.BlockSpec((tm, inw), rev), tile, tile,
            full((N_POOL_GROUPS, GROUP, GROUP)), full((1, pwid)), full((1, rwid)), full((1, d)),
        ],
        out_shape=[
            jax.ShapeDtypeStruct((s, d), F32),
            jax.ShapeDtypeStruct((s, inw), BF16),
            jax.ShapeDtypeStruct((s, d), BF16),
            jax.ShapeDtypeStruct((s, d), BF16),
            jax.ShapeDtypeStruct((N_POOL_GROUPS, GROUP, GROUP), F32),
            jax.ShapeDtypeStruct((1, pwid), F32),
            jax.ShapeDtypeStruct((1, rwid), F32),
            jax.ShapeDtypeStruct((1, d), F32),
        ],
        scratch_shapes=[
            pltpu.VMEM((inw, d), BF16), pltpu.VMEM((d, d), BF16),
            pltpu.VMEM((RET_HEADS, GROUP, GROUP), F32), pltpu.VMEM((MAX_WINDOW, pwid), F32),
            pltpu.VMEM((tm, d), F32), pltpu.VMEM((tm, inw), BF16), pltpu.SemaphoreType.DMA((2 * NDEV,)),
        ],
        args=[dh2, h1, proj, proj, o_saved, rsave, gain, weights[0], weights[1], pool_w, pool_scale, ret_gain,
              consts["cos"], consts["sin"], consts["decay"], consts["ktail"], consts["qhead"]],
    )


def _adam(w, g, m, v):
    m = ADAM_B1 * m + (1.0 - ADAM_B1) * g
    v = ADAM_B2 * v + (1.0 - ADAM_B2) * jnp.square(g)
    m_hat = m / (1.0 - ADAM_B1 ** ADAM_STEP)
    v_hat = v / (1.0 - ADAM_B2 ** ADAM_STEP)
    delta = -ADAM_LR * (m_hat / (jnp.sqrt(v_hat) + ADAM_EPS) + ADAM_WD * w)
    return delta, m, v


def _adamw_big(w, parts, m, v, name):
    rows, d = w.shape
    tr = _row_tile(rows, 176)

    def body(w_ref, p_ref, m_ref, v_ref, g_ref, d_ref, nm_ref, nv_ref):
        g = p_ref[0].astype(F32)
        for q in range(1, NCHIP):
            g = g + p_ref[q].astype(F32)
        g_ref[...] = g
        d_ref[...], nm_ref[...], nv_ref[...] = _adam(w_ref[...], g, m_ref[...], v_ref[...])

    spec = pl.BlockSpec((tr, d), lambda i: (i, 0))
    return _call(
        body,
        name=name,
        grid=(rows // tr,),
        in_specs=[spec, pl.BlockSpec((NCHIP, tr, d), lambda i: (0, i, 0)), spec, spec],
        out_specs=[spec] * 4,
        out_shape=[jax.ShapeDtypeStruct((rows, d), F32)] * 4,
        compiler_params=_seq(1),
    )(w, parts, m, v)


def _adamw_small(stats_all, pw_all, ws, ms, vs, pwid):
    nsmall = len(ws)

    def body(*refs):
        st_ref, pwa_ref = refs[0], refs[1]
        w_refs = refs[2:2 + nsmall]
        m_refs = refs[2 + nsmall:2 + 2 * nsmall]
        v_refs = refs[2 + 2 * nsmall:2 + 3 * nsmall]
        outs = refs[2 + 3 * nsmall:]
        st = st_ref[0]
        pwg = pwa_ref[0]
        for q in range(1, NDEV):
            st = st + st_ref[q]
            pwg = pwg + pwa_ref[q]
        grads = [st[0:1, :], st[1:2, :], st[2:3, :], st[3:4, :], st[4:5, 0:pwid], st[4:5, pwid:2 * pwid], pwg]
        outs[0][...] = jnp.zeros((1, GROUP), F32) + jnp.sum(st[5:6, :])
        for j in range(nsmall):
            delta, nm, nv = _adam(w_refs[j][...], grads[j], m_refs[j][...], v_refs[j][...])
            outs[1 + 4 * j][...] = grads[j]
            outs[2 + 4 * j][...] = delta
            outs[3 + 4 * j][...] = nm
            outs[4 + 4 * j][...] = nv

    out_shape = [jax.ShapeDtypeStruct((1, GROUP), F32)]
    for w in ws:
        out_shape += [jax.ShapeDtypeStruct(w.shape, F32)] * 4
    return _call(body, name="adamw_small", out_shape=out_shape, compiler_params=_params())(
        stats_all, pw_all, *ws, *ms, *vs
    )


def kernel(x, ffn1_norm, ffn1_w1, ffn1_w3, ffn1_w2, mix_norm, w_in, pool_w, pool_scale, ret_norm, w_out, ffn2_norm, ffn2_w1, ffn2_w3, ffn2_w2, final_norm, loss_target, m_ffn1_norm, m_ffn1_w1, m_ffn1_w3, m_ffn1_w2, m_mix_norm, m_w_in, m_pool_w, m_pool_scale, m_ret_norm, m_w_out, m_ffn2_norm, m_ffn2_w1, m_ffn2_w3, m_ffn2_w2, m_final_norm, v_ffn1_norm, v_ffn1_w1, v_ffn1_w3, v_ffn1_w2, v_mix_norm, v_w_in, v_pool_w, v_pool_scale, v_ret_norm, v_w_out, v_ffn2_norm, v_ffn2_w1, v_ffn2_w3, v_ffn2_w2, v_final_norm):
    s, d = x.shape[1], x.shape[2]
    ffn = ffn1_w1.shape[2] * NDEV
    pwid = pool_scale.shape[1]
    xs, tgt = x[0], loss_target[0]
    consts = _mix_constants(s)
    pw3 = pool_w[0]
    fnorm = final_norm.reshape(1, d)

    rows_of = lambda w, transposed: w[0].T if transposed else w[0]
    send_f1 = [rows_of(w, t).astype(BF16) for w, t in ((ffn1_w1, True), (ffn1_w3, True), (ffn1_w2, False))]
    later = [rows_of(w, t) for w, t in ((w_in, True), (w_out, False), (ffn2_w1, True), (ffn2_w3, True), (ffn2_w2, False))]

    sent_later, w13_f1 = _comm_call([_CastRows(later), _Gather(send_f1[:2])], "gather_ffn1")
    send_mix, send_f2 = sent_later[:2], sent_later[2:]
    (a1, b1, hm1), ((w2_f1, w_in_all),) = _ffn_up(xs, ffn1_norm, *w13_f1, carries=[_Gather(send_f1[2:] + send_mix[:1])])
    w_f1 = w13_f1 + [w2_f1]
    (h1,), ((w_out_all, w1_f2),) = _ffn_down(xs, hm1, w2_f1, carries=[_Gather(send_mix[1:] + send_f2[:1])])
    w_mix = [w_in_all, w_out_all]
    (h2, proj, o_saved, rsave), (rest,) = _mix_fwd(
        h1, mix_norm, w_mix, pw3, pool_scale, ret_norm, consts, carries=[_Gather(send_f2[1:])]
    )
    w_f2 = [w1_f2] + rest
    dh3, a2, b2, hm2, dgf, loss_cols = _ffn_fwd_loss(h2, ffn2_norm, w_f2, fnorm, tgt)

    (dh2, da2, db2, n2, dg2), _ = _ffn_bwd(dh3, h2, a2, b2, ffn2_norm, w_f2, ffn, "ffn2_bwd")
    sum_f2w1, _ = _wgrad(da2, n2, 1.0, "ffn2_w1_grad")
    sum_f2w3, _ = _wgrad(db2, n2, 1.0, "ffn2_w3_grad")
    sum_f2w2, ((parts_f2w1,),) = _wgrad(hm2, dh3, 0.5, "ffn2_w2_grad", carries=[_ChipScatter([sum_f2w1])])

    (dh1, dproj, u, mm, dpw, dps, drg, dgm), ((parts_f2w3, parts_f2w2),) = _mix_bwd(
        dh2, h1, proj, o_saved, rsave, mix_norm, w_mix, pw3, pool_scale, ret_norm, consts,
        carries=[_ChipScatter([sum_f2w3, sum_f2w2])],
    )
    (dx, da1, db1, n1, dg1), _ = _ffn_bwd(dh1, xs, a1, b1, ffn1_norm, w_f1, ffn, "ffn1_bwd")
    stats = jnp.concatenate(
        [dg1, dgm, dg2, dgf, jnp.concatenate([dps, drg], axis=1), loss_cols, jnp.zeros((2, d), F32)], axis=0
    )
    small = _GatherDirect([stats, dpw.reshape(N_POOL_GROUPS * GROUP, GROUP)])
    sum_f1w2, ((stats_all, pw_all),) = _wgrad(hm1, dh1, 0.5, "ffn1_w2_grad", carries=[small])
    sum_f1w1, ((parts_f1w2,),) = _wgrad(da1, n1, 1.0, "ffn1_w1_grad", carries=[_ChipScatter([sum_f1w2])])
    sum_f1w3, ((parts_f1w1,),) = _wgrad(db1, n1, 1.0, "ffn1_w3_grad", carries=[_ChipScatter([sum_f1w1])])
    sum_in, ((parts_f1w3,),) = _wgrad(dproj, u, 1.0, "w_in_grad", carries=[_ChipScatter([sum_f1w3])])
    sum_out, ((parts_in,),) = _wgrad(mm, dh2, 1.0, "w_out_grad", carries=[_ChipScatter([sum_in])])
    ((parts_out,),) = _comm_call([_ChipScatter([sum_out])], "scatter_last")

    big = (
        (ffn1_w1, m_ffn1_w1, v_ffn1_w1, parts_f1w1, True),
        (ffn1_w3, m_ffn1_w3, v_ffn1_w3, parts_f1w3, True),
        (ffn1_w2, m_ffn1_w2, v_ffn1_w2, parts_f1w2, False),
        (w_in, m_w_in, v_w_in, parts_in, True),
        (w_out, m_w_out, v_w_out, parts_out, False),
        (ffn2_w1, m_ffn2_w1, v_ffn2_w1, parts_f2w1, True),
        (ffn2_w3, m_ffn2_w3, v_ffn2_w3, parts_f2w3, True),
        (ffn2_w2, m_ffn2_w2, v_ffn2_w2, parts_f2w2, False),
    )
    big_out = []
    for j, (w, m, v, parts, t) in enumerate(big):
        view = (lambda a: a[0].T) if t else (lambda a: a[0])
        back = (lambda a: a.T[None]) if t else (lambda a: a[None])
        big_out.append([back(a) for a in _adamw_big(view(w), parts, view(m), view(v), "adamw_%d" % j)])

    small_w = (ffn1_norm, mix_norm, ffn2_norm, fnorm, pool_scale, ret_norm, pw3.reshape(-1, GROUP))
    small_m = (m_ffn1_norm, m_mix_norm, m_ffn2_norm, m_final_norm.reshape(1, d), m_pool_scale, m_ret_norm, m_pool_w.reshape(-1, GROUP))
    small_v = (v_ffn1_norm, v_mix_norm, v_ffn2_norm, v_final_norm.reshape(1, d), v_pool_scale, v_ret_norm, v_pool_w.reshape(-1, GROUP))
    res = _adamw_small(stats_all, pw_all, small_w, small_m, small_v, pwid)
    loss = res[0][0, 0]
    small_out = [list(res[1 + 4 * j:5 + 4 * j]) for j in range(len(small_w))]
    small_out[3] = [a.reshape(d) for a in small_out[3]]
    small_out[6] = [a.reshape(pool_w.shape) for a in small_out[6]]

    order = [small_out[0], big_out[0], big_out[1], big_out[2], small_out[1], big_out[3], small_out[6], small_out[4],
             small_out[5], big_out[4], small_out[2], big_out[5], big_out[6], big_out[7], small_out[3]]
    result = [loss, dx[None]]
    for kind in range(4):
        result += [t[kind] for t in order]
    return tuple(result)
```

```python
import functools

import numpy as np
import jax
import jax.numpy as jnp
from jax import lax
from jax.experimental import pallas as pl
from jax.experimental.pallas import tpu as pltpu

F32 = jnp.float32
BF16 = jnp.bfloat16

NDEV = 8
NCHIP = 4
EPS = 1e-6
N_POOL_GROUPS = 4
POOL_WINDOWS = (2, 4, 8, 16)
MAX_WINDOW = 16
GROUP = 128
RET_HEADS = 4
ROPE_BASE = 10000.0
ADAM_LR = 0.001
ADAM_B1 = 0.9
ADAM_B2 = 0.999
ADAM_EPS = 1e-08
ADAM_WD = 0.01
ADAM_STEP = 10

VMEM_LIMIT = 56 * 1024 * 1024
FFN_CHUNK = 256
ROW_BAND = 32

NT = (((1,), (1,)), ((), ()))
NN = (((1,), (0,)), ((), ()))
TN = (((0,), (0,)), ((), ()))

ANY = pl.BlockSpec(memory_space=pl.ANY)


def _dot(a, b, dims):
    return lax.dot_general(a, b, dims, preferred_element_type=F32)


def _call(body, **kw):
    return pl.pallas_call(body, **kw)


def _params(**kw):
    return pltpu.CompilerParams(vmem_limit_bytes=VMEM_LIMIT, **kw)


def _seq(n):
    return _params(dimension_semantics=("arbitrary",) * n)


def _peer(k):
    x, y, c = lax.axis_index("x"), lax.axis_index("y"), lax.axis_index("c")
    return (1 - x if k & 4 else x, 1 - y if k & 2 else y, 1 - c if k & 1 else c)


def _flat(pos):
    return 4 * pos[0] + 2 * pos[1] + pos[2]


def _chip(pos):
    return 2 * pos[0] + pos[1]


def _row_tile(rows, cap):
    return max(t for t in range(16, min(rows, cap) + 1, 16) if rows % t == 0)


def _pieces(rows, n):
    tiles = rows // 16
    cuts = [16 * (tiles * q // n) for q in range(n + 1)]
    return [(a, b - a) for a, b in zip(cuts[:-1], cuts[1:])]


def _load_weights(parts, sems):
    copies = []
    for buf, dst in parts:
        rows = buf.shape[1]
        for p in range(NDEV):
            cp = pltpu.make_async_copy(buf.at[p], dst.at[pl.ds(p * rows, rows), :], sems.at[len(copies)])
            cp.start()
            copies.append(cp)
    return copies


def _sigmoid(a):
    return 1.0 / (1.0 + jnp.exp(-a))


def _remote(src, dst, send_sem, recv_sem, to):
    return pltpu.make_async_remote_copy(
        src_ref=src, dst_ref=dst, send_sem=send_sem, recv_sem=recv_sem, device_id=to, device_id_type=pl.DeviceIdType.MESH
    )


class _Gather:
    X, Y, FAR = 4, 2, 6
    peers = (1, 2, 4)
    COPIES = 8

    def __init__(self, shards):
        n = len(shards)
        self.operands = list(shards)
        self.out_shape = [jax.ShapeDtypeStruct((NDEV,) + a.shape, a.dtype) for a in shards]
        self.sems = [
            pltpu.SemaphoreType.DMA((self.COPIES * n,)), pltpu.SemaphoreType.DMA((self.COPIES * n,)),
            pltpu.SemaphoreType.DMA((n,)),
        ]
        self.stages = [self.begin, self.relay, self.relay_far, self.end]

    def _copy(self, t, k, block, to, ins, outs, sems, own=False, half=None):
        rows = outs[t].shape[1]
        part = pl.ds(0, rows) if half is None else pl.ds(half * (rows // 2), rows // 2)
        dst = outs[t].at[_flat(block), part, :]
        at = self.COPIES * t + k
        return _remote(ins[t] if own else dst, dst, sems[0].at[at], sems[1].at[at], to)

    def _local(self, t, ins, outs, sems):
        return pltpu.make_async_copy(ins[t], outs[t].at[_flat(_peer(0))], sems[2].at[t])

    def begin(self, ins, outs, sems):
        me = _peer(0)
        for t in range(len(ins)):
            self._local(t, ins, outs, sems).start()
            for k, code in enumerate((1, self.X, self.Y)):
                self._copy(t, k, me, _peer(code), ins, outs, sems, own=True).start()

    def relay(self, ins, outs, sems):
        me, sibling = _peer(0), _peer(1)
        for t in range(len(ins)):
            self._copy(t, 1, _peer(self.X), me, ins, outs, sems).wait_recv()
            self._copy(t, 3, _peer(self.X), _peer(self.Y), ins, outs, sems, half=0).start()
            self._copy(t, 5, _peer(self.X), sibling, ins, outs, sems).start()
            self._copy(t, 2, _peer(self.Y), me, ins, outs, sems).wait_recv()
            self._copy(t, 4, _peer(self.Y), _peer(self.X), ins, outs, sems, half=1).start()
            self._copy(t, 6, _peer(self.Y), sibling, ins, outs, sems).start()

    def relay_far(self, ins, outs, sems):
        me, sibling = _peer(0), _peer(1)
        for t in range(len(ins)):
            self._copy(t, 3, _peer(self.FAR), me, ins, outs, sems, half=0).wait_recv()
            self._copy(t, 4, _peer(self.FAR), me, ins, outs, sems, half=1).wait_recv()
            self._copy(t, 7, _peer(self.FAR), sibling, ins, outs, sems).start()

    def end(self, ins, outs, sems):
        me = _peer(0)
        for t in range(len(ins)):
            self._copy(t, 0, _peer(1), me, ins, outs, sems).wait_recv()
            for k, code in ((5, self.X), (6, self.Y), (7, self.FAR)):
                self._copy(t, k, _peer(code ^ 1), me, ins, outs, sems).wait_recv()
            for k in range(self.COPIES):
                self._copy(t, k, me, me, ins, outs, sems, half=0 if k == 3 else 1 if k == 4 else None).wait_send()
            self._local(t, ins, outs, sems).wait()


class _GatherDirect:
    peers = tuple(range(1, NDEV))

    def __init__(self, arrays):
        n = len(arrays)
        self.operands = list(arrays)
        self.out_shape = [jax.ShapeDtypeStruct((NDEV,) + a.shape, a.dtype) for a in arrays]
        self.sems = [pltpu.SemaphoreType.DMA((7 * n,)), pltpu.SemaphoreType.DMA((7 * n,)), pltpu.SemaphoreType.DMA((n,))]
        self.stages = [self.begin, self.end]

    def begin(self, ins, outs, sems):
        mine = _flat(_peer(0))
        for t in range(len(ins)):
            pltpu.make_async_copy(ins[t], outs[t].at[mine], sems[2].at[t]).start()
            for k in range(1, NDEV):
                _remote(ins[t], outs[t].at[mine], sems[0].at[7 * t + k - 1], sems[1].at[7 * t + k - 1], _peer(k)).start()

    def end(self, ins, outs, sems):
        mine = _flat(_peer(0))
        for t in range(len(ins)):
            for k in range(1, NDEV):
                cp = _remote(ins[t], outs[t].at[_flat(_peer(k))], sems[0].at[7 * t + k - 1], sems[1].at[7 * t + k - 1], _peer(k))
                cp.wait_recv()
                cp.wait_send()
            pltpu.make_async_copy(ins[t], outs[t].at[mine], sems[2].at[t]).wait()


class _ChipScatter:
    peers = (2, 4, 6)
    across = (4, 2, 6)

    def __init__(self, sums):
        n = len(sums) * len(self.across)
        self.operands = list(sums)
        self.out_shape = [jax.ShapeDtypeStruct((len(self.across),) + a.shape[1:], a.dtype) for a in sums]
        self.sems = [pltpu.SemaphoreType.DMA((n,)), pltpu.SemaphoreType.DMA((n,))]
        self.stages = [self.begin, self.end]

    def _copies(self, ins, outs, sems):
        copies = []
        for t in range(len(ins)):
            for slot, k in enumerate(self.across):
                at = len(copies)
                copies.append(
                    _remote(ins[t].at[_chip(_peer(k))], outs[t].at[slot], sems[0].at[at], sems[1].at[at], _peer(k))
                )
        return copies

    def begin(self, ins, outs, sems):
        for cp in self._copies(ins, outs, sems):
            cp.start()

    def end(self, ins, outs, sems):
        for cp in self._copies(ins, outs, sems):
            cp.wait_recv()
            cp.wait_send()


class _CastRows:
    peers = ()

    def __init__(self, arrays):
        n = len(arrays)
        self.operands = list(arrays)
        self.out_shape = [jax.ShapeDtypeStruct(a.shape, BF16) for a in arrays]
        self.sems = [pltpu.SemaphoreType.DMA((n,)), pltpu.SemaphoreType.DMA((n,))]
        self.sems += [pltpu.VMEM(a.shape, F32) for a in arrays] + [pltpu.VMEM(a.shape, BF16) for a in arrays]
        self.stages = [self.begin, self.convert, self.end]

    def _moves(self, t, ins, outs, scratch):
        n = len(ins)
        load = pltpu.make_async_copy(ins[t], scratch[2 + t], scratch[0].at[t])
        store = pltpu.make_async_copy(scratch[2 + n + t], outs[t], scratch[1].at[t])
        return load, store

    def begin(self, ins, outs, scratch):
        for t in range(len(ins)):
            self._moves(t, ins, outs, scratch)[0].start()

    def convert(self, ins, outs, scratch):
        n = len(ins)
        for t in range(n):
            load, store = self._moves(t, ins, outs, scratch)
            load.wait()
            scratch[2 + n + t][...] = scratch[2 + t][...].astype(BF16)
            store.start()

    def end(self, ins, outs, scratch):
        for t in range(len(ins)):
            self._moves(t, ins, outs, scratch)[1].wait()


def _split_refs(refs, counts):
    out, at = [], 0
    for n in counts:
        out.append(refs[at:at + n])
        at += n
    return out


BARRIER_IDS = {(2, 4, 6): 0, (1, 2, 4): 1, (1,): 2, (1, 2, 4, 6): 3, tuple(range(1, NDEV)): 4}


def _peers_of(carries, own=()):
    peers = tuple(sorted(set(own).union(*[c.peers for c in carries])))
    return (peers, BARRIER_IDS[peers]) if peers in BARRIER_IDS else (None, None)


def _handshake(peers):
    barrier = pltpu.get_barrier_semaphore()
    for k in peers:
        pl.semaphore_signal(barrier, inc=1, device_id=_peer(k), device_id_type=pl.DeviceIdType.MESH)
    pl.semaphore_wait(barrier, len(peers))


def _comm_call(carries, name):
    nin = [len(c.operands) for c in carries]
    nout = [len(c.out_shape) for c in carries]
    nsem = [len(c.sems) for c in carries]
    peers, collective_id = _peers_of(carries)

    def body(*refs):
        if peers:
            _handshake(peers)
        ins, outs, sems = _split_refs(refs, (sum(nin), sum(nout), sum(nsem)))
        parts = list(zip(carries, _split_refs(ins, nin), _split_refs(outs, nout), _split_refs(sems, nsem)))
        for depth in range(max(len(c.stages) for c in carries)):
            for c, i, o, s in parts:
                if depth < len(c.stages) - 1:
                    c.stages[depth](i, o, s)
        for c, i, o, s in parts:
            c.stages[-1](i, o, s)

    res = _call(
        body,
        name=name,
        out_shape=[sh for c in carries for sh in c.out_shape],
        in_specs=[ANY] * sum(nin),
        out_specs=[ANY] * sum(nout),
        scratch_shapes=[sm for c in carries for sm in c.sems],
        compiler_params=_params(has_side_effects=True, collective_id=collective_id),
    )(*[a for c in carries for a in c.operands])
    return _split_refs(list(res), nout)


def _grid_call(body, carries, *, name, grid, in_specs, out_specs, out_shape, scratch_shapes, args, own_peers=()):
    ni, no, ns = len(in_specs), len(out_specs), len(scratch_shapes)
    nin = [len(c.operands) for c in carries]
    nout = [len(c.out_shape) for c in carries]
    nsem = [len(c.sems) for c in carries]
    steps = int(np.prod(grid))
    peers, collective_id = _peers_of(carries, own_peers)

    def when_of(stage, count):
        first, last = (5 * steps) // 8 - 1, steps - 2
        return max(0, last if count <= 3 else first + (last - first) * (stage - 1) // (count - 3))

    def wrapped(*refs):
        ins, cins, outs, couts, scr, csems = _split_refs(refs, (ni, sum(nin), no, sum(nout), ns, sum(nsem)))
        if not carries and not peers:
            return body(*ins, *outs, *scr)
        parts = list(zip(carries, _split_refs(cins, nin), _split_refs(couts, nout), _split_refs(csems, nsem)))
        step = pl.program_id(0)
        for axis in range(1, len(grid)):
            step = step * grid[axis] + pl.program_id(axis)

        @pl.when(step == 0)
        def _():
            if peers:
                _handshake(peers)
            for c, i, o, s in parts:
                c.stages[0](i, o, s)

        body(*ins, *outs, *scr)

        for c, i, o, s in parts:
            for stage in range(1, len(c.stages) - 1):
                pl.when(step == when_of(stage, len(c.stages)))(functools.partial(c.stages[stage], i, o, s))

        @pl.when(step == steps - 1)
        def _():
            for c, i, o, s in parts:
                c.stages[-1](i, o, s)

    res = _call(
        wrapped,
        name=name,
        grid=tuple(grid),
        in_specs=list(in_specs) + [ANY] * sum(nin),
        out_specs=list(out_specs) + [ANY] * sum(nout),
        out_shape=list(out_shape) + [sh for c in carries for sh in c.out_shape],
        scratch_shapes=list(scratch_shapes) + [sm for c in carries for sm in c.sems],
        compiler_params=_params(dimension_semantics=("arbitrary",) * len(grid), collective_id=collective_id),
    )(*args, *[a for c in carries for a in c.operands])
    res = list(res)
    return res[:no], _split_refs(res[no:], nout)


def _chunks(width):
    return [(at, min(FFN_CHUNK, width - at)) for at in range(0, width, FFN_CHUNK)]


def _ffn_fwd_loss(x, gain, weights, final_gain, target):
    s, d = x.shape
    ffn = weights[0].shape[0] * weights[0].shape[1]
    tm = min(512, s)

    def body(x_ref, g_ref, b1, b3, b2, gf_ref, t_ref, dh_ref, a_ref, b_ref, hm_ref, dgf_ref, loss_ref, w1s, w3s, w2s, sems):
        @pl.when(pl.program_id(0) == 0)
        def _():
            for cp in _load_weights(((b1, w1s), (b3, w3s), (b2, w2s)), sems):
                cp.wait()
            dgf_ref[...] = jnp.zeros_like(dgf_ref)
            loss_ref[...] = jnp.zeros_like(loss_ref)

        xv = x_ref[...]
        r = lax.rsqrt(jnp.mean(xv * xv, axis=-1, keepdims=True) + EPS)
        n = (xv * r * g_ref[...]).astype(BF16)
        acc = jnp.zeros((tm, d), F32)
        for at, width in _chunks(ffn):
            cols = slice(at, at + width)
            a = _dot(n, w1s[cols, :], NT)
            b = _dot(n, w3s[cols, :], NT)
            a_ref[:, cols] = a.astype(BF16)
            b_ref[:, cols] = b.astype(BF16)
            hm = (a * _sigmoid(a) * b).astype(BF16)
            hm_ref[:, cols] = hm
            acc = acc + _dot(hm, w2s[cols, :], NN)
        h = xv + 0.5 * acc
        rf = lax.rsqrt(jnp.mean(h * h, axis=-1, keepdims=True) + EPS)
        nh = h * rf
        gf = gf_ref[...]
        err = nh * gf - t_ref[...]
        loss_ref[...] += jnp.sum(err * err, axis=0, keepdims=True) * (0.5 / d)
        dy = err * (1.0 / d)
        dgf_ref[...] += jnp.sum(dy * nh, axis=0, keepdims=True)
        dn = dy * gf
        dh_ref[...] = rf * (dn - nh * jnp.mean(dn * nh, axis=-1, keepdims=True))

    tile = pl.BlockSpec((tm, d), lambda i: (i, 0))
    row = pl.BlockSpec((1, d), lambda i: (0, 0))
    wide = pl.BlockSpec((tm, ffn), lambda i: (i, 0))
    return _call(
        body,
        name="ffn_fwd_loss",
        grid=(s // tm,),
        in_specs=[tile, row, ANY, ANY, ANY, row, tile],
        out_specs=[tile, wide, wide, wide, row, row],
        out_shape=[jax.ShapeDtypeStruct((s, d), F32)] + [jax.ShapeDtypeStruct((s, ffn), BF16)] * 3
        + [jax.ShapeDtypeStruct((1, d), F32)] * 2,
        scratch_shapes=[pltpu.VMEM((ffn, d), BF16)] * 3 + [pltpu.SemaphoreType.DMA((3 * NDEV,))],
        compiler_params=_seq(1),
    )(x, gain, *weights, final_gain, target)


def _ffn_up(x, gain, w1, w3, carries=()):
    s, d = x.shape
    ffn = w1.shape[0] * w1.shape[1]
    tm = min(512, s)

    def body(x_ref, g_ref, b1, b3, a_ref, b_ref, hm_ref, w1s, w3s, sems):
        @pl.when(pl.program_id(0) == 0)
        def _():
            for cp in _load_weights(((b1, w1s), (b3, w3s)), sems):
                cp.wait()

        xv = x_ref[...]
        r = lax.rsqrt(jnp.mean(xv * xv, axis=-1, keepdims=True) + EPS)
        n = (xv * r * g_ref[...]).astype(BF16)
        for at, width in _chunks(ffn):
            cols = slice(at, at + width)
            a = _dot(n, w1s[cols, :], NT)
            b = _dot(n, w3s[cols, :], NT)
            a_ref[:, cols] = a.astype(BF16)
            b_ref[:, cols] = b.astype(BF16)
            hm_ref[:, cols] = (a * _sigmoid(a) * b).astype(BF16)

    wide = pl.BlockSpec((tm, ffn), lambda i: (i, 0))
    return _grid_call(
        body,
        carries,
        name="ffn_up",
        grid=(s // tm,),
        in_specs=[pl.BlockSpec((tm, d), lambda i: (i, 0)), pl.BlockSpec((1, d), lambda i: (0, 0)), ANY, ANY],
        out_specs=[wide] * 3,
        out_shape=[jax.ShapeDtypeStruct((s, ffn), BF16)] * 3,
        scratch_shapes=[pltpu.VMEM((ffn, d), BF16)] * 2 + [pltpu.SemaphoreType.DMA((2 * NDEV,))],
        args=[x, gain, w1, w3],
    )


def _ffn_down(x, hm, w2, carries=()):
    s, d = x.shape
    ffn = w2.shape[0] * w2.shape[1]
    tm = min(512, s)

    def body(x_ref, hm_ref, b2, h_ref, w2s, sems):
        @pl.when(pl.program_id(0) == 0)
        def _():
            for cp in _load_weights(((b2, w2s),), sems):
                cp.wait()

        acc = jnp.zeros((tm, d), F32)
        for at, width in _chunks(ffn):
            cols = slice(at, at + width)
            acc = acc + _dot(hm_ref[:, cols], w2s[cols, :], NN)
        h_ref[...] = x_ref[...] + 0.5 * acc

    tile = pl.BlockSpec((tm, d), lambda i: (i, 0))
    return _grid_call(
        body,
        carries,
        name="ffn_down",
        grid=(s // tm,),
        in_specs=[tile, pl.BlockSpec((tm, ffn), lambda i: (i, 0)), ANY],
        out_specs=[tile],
        out_shape=[jax.ShapeDtypeStruct((s, d), F32)],
        scratch_shapes=[pltpu.VMEM((ffn, d), BF16), pltpu.SemaphoreType.DMA((NDEV,))],
        args=[x, hm, w2],
    )


def _ffn_bwd(dh, x, a, b, gain, weights, ffn, name, carries=()):
    s, d = x.shape
    tm = min(512, s)
    halves = 2
    fh = ffn // halves

    def body(dh_ref, x_ref, a_ref, b_ref, g_ref, b1, b3, b2, dx_ref, da_ref, db_ref, n_ref, dg_ref, w1s, w3s, w2s, sems):
        i, j = pl.program_id(0), pl.program_id(1)

        @pl.when((i == 0) & (j == 0))
        def _():
            for cp in _load_weights(((b1, w1s), (b3, w3s), (b2, w2s)), sems):
                cp.wait()
            dg_ref[...] = jnp.zeros_like(dg_ref)

        @pl.when(j == 0)
        def _():
            dx_ref[...] = jnp.zeros_like(dx_ref)

        dob = (0.5 * dh_ref[...]).astype(BF16)
        chunks = _chunks(fh)

        def dhm_of(k):
            at, width = chunks[k]
            return _dot(dob, w2s[pl.ds(pl.multiple_of(j * fh + at, GROUP), width), :], NT)

        ahead = dhm_of(0)
        for k, (at, width) in enumerate(chunks):
            cols = slice(at, at + width)
            dhm = ahead
            if k + 1 < len(chunks):
                ahead = dhm_of(k + 1)
            for top in range(0, tm, ROW_BAND):
                band = slice(top, top + ROW_BAND)
                av = a_ref[band, cols].astype(F32)
                bv = b_ref[band, cols].astype(F32)
                sg = _sigmoid(av)
                dv = dhm[band]
                da_ref[band, cols] = (dv * bv * (sg * (1.0 + av * (1.0 - sg)))).astype(BF16)
                db_ref[band, cols] = (dv * (av * sg)).astype(BF16)
        half = pl.ds(pl.multiple_of(j * fh, GROUP), fh)
        dx_ref[...] += _dot(da_ref[...], w1s[half, :], NN) + _dot(db_ref[...], w3s[half, :], NN)

        @pl.when(j == halves - 1)
        def _():
            xv = x_ref[...]
            g = g_ref[...]
            r = lax.rsqrt(jnp.mean(xv * xv, axis=-1, keepdims=True) + EPS)
            nh = xv * r
            n_ref[...] = (nh * g).astype(BF16)
            total = dx_ref[...]
            dg_ref[...] += jnp.sum(total * nh, axis=0, keepdims=True)
            dnh = total * g
            dx_ref[...] = dh_ref[...] + r * (dnh - nh * jnp.mean(dnh * nh, axis=-1, keepdims=True))

    tile = pl.BlockSpec((tm, d), lambda i, j: (i, 0))
    row = pl.BlockSpec((1, d), lambda i, j: (0, 0))
    wide = pl.BlockSpec((tm, fh), lambda i, j: (i, j))
    return _grid_call(
        body,
        carries,
        name=name,
        grid=(s // tm, halves),
        in_specs=[tile, tile, wide, wide, row, ANY, ANY, ANY],
        out_specs=[tile, wide, wide, tile, row],
        out_shape=[
            jax.ShapeDtypeStruct((s, d), F32),
            jax.ShapeDtypeStruct((s, ffn), BF16),
            jax.ShapeDtypeStruct((s, ffn), BF16),
            jax.ShapeDtypeStruct((s, d), BF16),
            jax.ShapeDtypeStruct((1, d), F32),
        ],
        scratch_shapes=[pltpu.VMEM((ffn, d), BF16)] * 3 + [pltpu.SemaphoreType.DMA((3 * NDEV,))],
        args=[dh, x, a, b, gain] + list(weights),
    )


SWAP_PIECES = 1


def _wgrad(lhs, rhs, scale, name, carries=()):
    s, m = lhs.shape
    n = rhs.shape[1]
    rs = m // NDEV
    tk = min(1024, s)
    steps = s // tk
    pieces = [(j, at, size) for j in range(2) for at, size in _pieces(rs, SWAP_PIECES)]

    def body(l_ref, r_ref, o_ref, acc, mine, theirs, send_sems, recv_sems):
        h, k = pl.program_id(0), pl.program_id(1)

        @pl.when(k == 0)
        def _():
            acc[...] = _dot(l_ref[...], r_ref[...].astype(BF16), TN)

        @pl.when(k > 0)
        def _():
            acc[...] += _dot(l_ref[...], r_ref[...].astype(BF16), TN)

        def exchange(half):
            c = lax.axis_index("c")
            return [
                _remote(mine.at[half, 1 - c, j, pl.ds(at, size), :], theirs.at[half, j, pl.ds(at, size), :],
                        send_sems.at[half * len(pieces) + q], recv_sems.at[half * len(pieces) + q], _peer(1))
                for q, (j, at, size) in enumerate(pieces)
            ]

        def settle(half):
            for cp in exchange(half):
                cp.wait_recv()
            both = mine[half, lax.axis_index("c")].astype(F32) + theirs[half].astype(F32)
            o_ref[2 * half:2 * half + 2] = both.astype(BF16)
            for cp in exchange(half):
                cp.wait_send()

        for half in range(2):
            @pl.when((h == half) & (k == steps - 1))
            def _():
                for p in range(NCHIP):
                    mine[half, p % 2, p // 2] = (acc[p * rs:(p + 1) * rs, :] * scale).astype(BF16)
                for cp in exchange(half):
                    cp.start()
                if half == 1:
                    settle(0)
                    settle(1)

    (out,), carried = _grid_call(
        body,
        carries,
        name=name,
        grid=(2, steps),
        in_specs=[pl.BlockSpec((tk, m // 2), lambda h, k: (k, h)), pl.BlockSpec((tk, n), lambda h, k: (k, 0))],
        out_specs=[pl.BlockSpec((NCHIP, rs, n), lambda h, k: (0, 0, 0))],
        out_shape=[jax.ShapeDtypeStruct((NCHIP, rs, n), BF16)],
        scratch_shapes=[
            pltpu.VMEM((m // 2, n), F32), pltpu.VMEM((2, 2, 2, rs, n), BF16), pltpu.VMEM((2, 2, rs, n), BF16),
            pltpu.SemaphoreType.DMA((2 * len(pieces),)), pltpu.SemaphoreType.DMA((2 * len(pieces),)),
        ],
        args=[lhs, rhs],
        own_peers=(1,),
    )
    return out, carried


def _mix_constants(s):
    c = GROUP
    lg = np.log1p(-np.exp2(-5.0 - np.arange(RET_HEADS, dtype=np.float32))).astype(np.float32)
    pos = np.arange(c, dtype=np.float32)
    rel = pos[:, None] - pos[None, :]
    decay = np.where(rel[None] >= 0, np.exp(lg[:, None, None] * np.maximum(rel, 0.0)[None]), 0.0).astype(np.float32)
    ktail = np.exp(lg[:, None] * (c - 1 - pos)[None, :]).astype(np.float32)
    qhead = np.exp(lg[:, None] * (pos + 1.0)[None, :]).astype(np.float32)
    chunk_decay = [float(v) for v in np.exp(lg * np.float32(c)).astype(np.float32)]
    ones = np.ones((1, 1, c), np.float32)
    inv_freq = (1.0 / (np.float32(ROPE_BASE) ** (np.arange(0, c, 2, dtype=np.float32) / np.float32(c)))).astype(np.float32)
    ang = (np.arange(s, dtype=np.float32)[:, None] * inv_freq[None, :]).astype(np.float32)
    cos, sin = np.cos(ang).astype(np.float32), np.sin(ang).astype(np.float32)
    return dict(
        decay=jnp.asarray(decay),
        ktail=jnp.asarray(ktail[:, :, None] * ones),
        qhead=jnp.asarray(qhead[:, :, None] * ones),
        chunk_decay=chunk_decay,
        cos=jnp.asarray(np.concatenate([cos, cos], axis=-1)),
        sin=jnp.asarray(np.concatenate([-sin, sin], axis=-1)),
    )


def _rope(t, cos, sin):
    return t * cos + pltpu.roll(t, GROUP // 2, axis=1) * sin


def _rope_bwd(dt, cos, sin):
    return dt * cos + pltpu.roll(dt * sin, GROUP // 2, axis=1)


def _window_sums(ext, w, forward):
    rows = ext.shape[0]
    acc, k = ext, 1
    while k < w:
        acc = acc + pltpu.roll(acc, k if forward else rows - k, axis=0)
        k *= 2
    return acc


def _pool_counts(tile, tm, w):
    t = lax.broadcasted_iota(jnp.int32, (tm, 1), 0) + tile * tm
    return jnp.minimum(t + 1, w).astype(F32)


def _mix_fwd(h1, gain, weights, pool_w, pool_scale, ret_gain, consts, carries=()):
    s, d = h1.shape
    pwid = N_POOL_GROUPS * GROUP
    rwid = RET_HEADS * GROUP
    inw = pwid + 4 * rwid
    tm = min(256, s)
    nck = tm // GROUP
    cd = consts["chunk_decay"]

    def body(h_ref, g_ref, bin_, bout, pw_ref, ps_ref, rg_ref, cos_ref, sin_ref, dec_ref, kt_ref, qh_ref,
             h2_ref, proj_ref, o_ref, rs_ref, wins, wouts, state, carry, mbuf, sems):
        i = pl.program_id(0)

        @pl.when(i == 0)
        def _():
            for cp in _load_weights(((bin_, wins), (bout, wouts)), sems):
                cp.wait()
            state[...] = jnp.zeros_like(state)
            carry[...] = jnp.zeros_like(carry)

        hv = h_ref[...]
        r = lax.rsqrt(jnp.mean(hv * hv, axis=-1, keepdims=True) + EPS)
        u = (hv * r * g_ref[...]).astype(BF16)
        proj_ref[...] = _dot(u, wins[...], NT)

        ext = jnp.concatenate([carry[...], proj_ref[:, 0:pwid]], axis=0)
        carry[...] = proj_ref[tm - MAX_WINDOW:tm, 0:pwid]
        for gi, w in enumerate(POOL_WINDOWS):
            cols = slice(gi * GROUP, (gi + 1) * GROUP)
            xg = ext[:, cols]
            ws = _window_sums(xg, w, True)[MAX_WINDOW:, :]
            pooled = ws / _pool_counts(i, tm, w) - xg[MAX_WINDOW:, :]
            z = _dot(pooled.astype(BF16), pw_ref[gi].astype(BF16), NN)
            mbuf[:, cols] = (z * ps_ref[:, cols]).astype(BF16)

        cos, sin = cos_ref[...], sin_ref[...]
        for h in range(RET_HEADS):
            cq = slice(pwid + h * GROUP, pwid + (h + 1) * GROUP)
            ck = slice(pwid + rwid + h * GROUP, pwid + rwid + (h + 1) * GROUP)
            cv = slice(pwid + 2 * rwid + h * GROUP, pwid + 2 * rwid + (h + 1) * GROUP)
            cg = slice(pwid + 3 * rwid + h * GROUP, pwid + 3 * rwid + (h + 1) * GROUP)
            ch = slice(h * GROUP, (h + 1) * GROUP)
            qr = _rope(proj_ref[:, cq], cos, sin)
            kr = _rope(proj_ref[:, ck], cos, sin) * (GROUP ** -0.5)
            vb = proj_ref[:, cv].astype(BF16)
            for n in range(nck):
                rows = slice(n * GROUP, (n + 1) * GROUP)
                qc, kc, vc = qr[rows], kr[rows], vb[rows]
                rb = state[h]
                rs_ref[n, h] = rb
                p = (_dot(qc.astype(BF16), kc.astype(BF16), NT) * dec_ref[h]).astype(BF16)
                o = _dot(p, vc, NN) + _dot((qc * qh_ref[h]).astype(BF16), rb.astype(BF16), NN)
                state[h] = cd[h] * rb + _dot((kc * kt_ref[h]).astype(BF16), vc, TN)
                o_ref[rows, ch] = o
                on = o * lax.rsqrt(jnp.mean(o * o, axis=-1, keepdims=True) + EPS)
                gv = proj_ref[rows, cg]
                mbuf[rows, pwid + h * GROUP:pwid + (h + 1) * GROUP] = (
                    gv * _sigmoid(gv) * (on * rg_ref[:, ch])
                ).astype(BF16)
        h2_ref[...] = hv + _dot(mbuf[...], wouts[...], NN)

    tile = pl.BlockSpec((tm, d), lambda i: (i, 0))
    full = lambda shape: pl.BlockSpec(shape, lambda i: (0,) * len(shape))
    return _grid_call(
        body,
        carries,
        name="mix_fwd",
        grid=(s // tm,),
        in_specs=[
            tile, full((1, d)), ANY, ANY,
            full((N_POOL_GROUPS, GROUP, GROUP)), full((1, pwid)), full((1, rwid)),
            pl.BlockSpec((tm, GROUP), lambda i: (i, 0)), pl.BlockSpec((tm, GROUP), lambda i: (i, 0)),
            full((RET_HEADS, GROUP, GROUP)), full((RET_HEADS, GROUP, GROUP)), full((RET_HEADS, GROUP, GROUP)),
        ],
        out_specs=[
            tile,
            pl.BlockSpec((tm, inw), lambda i: (i, 0)),
            pl.BlockSpec((tm, rwid), lambda i: (i, 0)),
            pl.BlockSpec((nck, RET_HEADS, GROUP, GROUP), lambda i: (i, 0, 0, 0)),
        ],
        out_shape=[
            jax.ShapeDtypeStruct((s, d), F32),
            jax.ShapeDtypeStruct((s, inw), F32),
            jax.ShapeDtypeStruct((s, rwid), F32),
            jax.ShapeDtypeStruct((s // GROUP, RET_HEADS, GROUP, GROUP), F32),
        ],
        scratch_shapes=[
            pltpu.VMEM((inw, d), BF16), pltpu.VMEM((d, d), BF16),
            pltpu.VMEM((RET_HEADS, GROUP, GROUP), F32), pltpu.VMEM((MAX_WINDOW, pwid), F32),
            pltpu.VMEM((tm, d), BF16), pltpu.SemaphoreType.DMA((2 * NDEV,)),
        ],
        args=[h1, gain, weights[0], weights[1], pool_w, pool_scale, ret_gain,
              consts["cos"], consts["sin"], consts["decay"], consts["ktail"], consts["qhead"]],
    )


def _mix_bwd(dh2, h1, proj, o_saved, rsave, gain, weights, pool_w, pool_scale, ret_gain, consts, carries=()):
    s, d = h1.shape
    pwid = N_POOL_GROUPS * GROUP
    rwid = RET_HEADS * GROUP
    inw = pwid + 4 * rwid
    tm = min(256, s)
    nck = tm // GROUP
    nt = s // tm
    cd = consts["chunk_decay"]
    halo_per_tile = tm // MAX_WINDOW

    def body(dh2_ref, h_ref, proj_ref, halo_ref, o_ref, rs_ref, g_ref, bin_, bout, pw_ref, ps_ref, rg_ref,
             cos_ref, sin_ref, dec_ref, kt_ref, qh_ref,
             dh1_ref, dproj_ref, u_ref, m_ref, dpw_ref, dps_ref, drg_ref, dg_ref,
             wins, wouts, dstate, carry, dm, dpj, sems):
        i = pl.program_id(0)
        tile = nt - 1 - i

        @pl.when(i == 0)
        def _():
            for cp in _load_weights(((bin_, wins), (bout, wouts)), sems):
                cp.wait()
            dstate[...] = jnp.zeros_like(dstate)
            carry[...] = jnp.zeros_like(carry)
            for ref in (dpw_ref, dps_ref, drg_ref, dg_ref):
                ref[...] = jnp.zeros_like(ref)

        dh2v = dh2_ref[...]
        dm[...] = _dot(dh2v.astype(BF16), wouts[...], NT)
        hv = h_ref[...]
        g = g_ref[...]
        r = lax.rsqrt(jnp.mean(hv * hv, axis=-1, keepdims=True) + EPS)
        uh = hv * r
        u_ref[...] = (uh * g).astype(BF16)

        halo = jnp.where(tile == 0, 0.0, halo_ref[...])
        ext = jnp.concatenate([halo, proj_ref[:, 0:pwid]], axis=0)
        next_dpn = carry[...]
        for gi, w in enumerate(POOL_WINDOWS):
            cols = slice(gi * GROUP, (gi + 1) * GROUP)
            xg = ext[:, cols]
            cnt = _pool_counts(tile, tm, w)
            pooled = (_window_sums(xg, w, True)[MAX_WINDOW:, :] / cnt - xg[MAX_WINDOW:, :]).astype(BF16)
            pwb = pw_ref[gi].astype(BF16)
            z = _dot(pooled, pwb, NN)
            scale = ps_ref[:, cols]
            m_ref[:, cols] = (z * scale).astype(BF16)
            da = dm[:, cols]
            dps_ref[:, cols] += jnp.sum(da * z, axis=0, keepdims=True)
            dz = (da * scale).astype(BF16)
            dpw_ref[gi] += _dot(pooled, dz, TN)
            dpl = _dot(dz, pwb, NT)
            dpn = dpl / cnt
            ext2 = jnp.concatenate([dpn, next_dpn[:, cols]], axis=0)
            dpj[:, cols] = (_window_sums(ext2, w, False)[0:tm, :] - dpl).astype(BF16)
            carry[:, cols] = dpn[0:MAX_WINDOW, :]

        cos, sin = cos_ref[...], sin_ref[...]
        for h in range(RET_HEADS):
            cq = slice(pwid + h * GROUP, pwid + (h + 1) * GROUP)
            ck = slice(pwid + rwid + h * GROUP, pwid + rwid + (h + 1) * GROUP)
            cv = slice(pwid + 2 * rwid + h * GROUP, pwid + 2 * rwid + (h + 1) * GROUP)
            cg = slice(pwid + 3 * rwid + h * GROUP, pwid + 3 * rwid + (h + 1) * GROUP)
            ch = slice(h * GROUP, (h + 1) * GROUP)
            qr = _rope(proj_ref[:, cq], cos, sin)
            kr = _rope(proj_ref[:, ck], cos, sin) * (GROUP ** -0.5)
            vb = proj_ref[:, cv].astype(BF16)
            gv = proj_ref[:, cg]
            ov = o_ref[:, ch]
            ro = lax.rsqrt(jnp.mean(ov * ov, axis=-1, keepdims=True) + EPS)
            on = ov * ro
            rg = rg_ref[:, ch]
            db = dm[:, pwid + h * GROUP:pwid + (h + 1) * GROUP]
            sg = _sigmoid(gv)
            sl = gv * sg
            m_ref[:, pwid + h * GROUP:pwid + (h + 1) * GROUP] = (sl * (on * rg)).astype(BF16)
            dpj[:, cg] = (db * (on * rg) * (sg * (1.0 + gv * (1.0 - sg)))).astype(BF16)
            drg_ref[:, ch] += jnp.sum(db * sl * on, axis=0, keepdims=True)
            don = db * sl * rg
            do = (ro * (don - on * jnp.mean(don * on, axis=-1, keepdims=True))).astype(BF16)
            for n in reversed(range(nck)):
                rows = slice(n * GROUP, (n + 1) * GROUP)
                qc, kc, vc, dob = qr[rows], kr[rows], vb[rows], do[rows]
                qcb, kcb = qc.astype(BF16), kc.astype(BF16)
                qh = (qc * qh_ref[h]).astype(BF16)
                kt = (kc * kt_ref[h]).astype(BF16)
                rn = rs_ref[n, h].astype(BF16)
                dnext = dstate[h]
                dnb = dnext.astype(BF16)
                dec = dec_ref[h]
                p = (_dot(qcb, kcb, NT) * dec).astype(BF16)
                ds = (_dot(dob, vc, NT) * dec).astype(BF16)
                dv = _dot(p, dob, TN) + _dot(kt, dnb, NN)
                dq = _dot(ds, kcb, NN) + _dot(dob, rn, NT) * qh_ref[h]
                dk = _dot(ds, qcb, TN) + _dot(vc, dnb, NT) * kt_ref[h]
                dstate[h] = cd[h] * dnext + _dot(qh, dob, TN)
                dpj[rows, cq] = _rope_bwd(dq, cos[rows], sin[rows]).astype(BF16)
                dpj[rows, ck] = _rope_bwd(dk * (GROUP ** -0.5), cos[rows], sin[rows]).astype(BF16)
                dpj[rows, cv] = dv.astype(BF16)

        dproj_ref[...] = dpj[...]
        du = _dot(dpj[...], wins[...], NN)
        dg_ref[...] += jnp.sum(du * uh, axis=0, keepdims=True)
        dn = du * g
        dh1_ref[...] = dh2v + r * (dn - uh * jnp.mean(dn * uh, axis=-1, keepdims=True))

    rev = lambda i: (nt - 1 - i, 0)
    tile = pl.BlockSpec((tm, d), rev)
    full = lambda shape: pl.BlockSpec(shape, lambda i: (0,) * len(shape))
    return _grid_call(
        body,
        carries,
        name="mix_bwd",
        grid=(nt,),
        in_specs=[
            tile, tile,
            pl.BlockSpec((tm, inw), rev),
            pl.BlockSpec((MAX_WINDOW, pwid), lambda i: (jnp.maximum((nt - 1 - i) * halo_per_tile - 1, 0), 0)),
            pl.BlockSpec((tm, rwid), rev),
            pl.BlockSpec((nck, RET_HEADS, GROUP, GROUP), lambda i: (nt - 1 - i, 0, 0, 0)),
            full((1, d)), ANY, ANY,
            full((N_POOL_GROUPS, GROUP, GROUP)), full((1, pwid)), full((1, rwid)),
            pl.BlockSpec((tm, GROUP), rev), pl.BlockSpec((tm, GROUP), rev),
            full((RET_HEADS, GROUP, GROUP)), full((RET_HEADS, GROUP, GROUP)), full((RET_HEADS, GROUP, GROUP)),
        ],
        out_specs=[
            tile, pl.BlockSpec((tm, inw), rev), tile, tile,
            full((N_POOL_GROUPS, GROUP, GROUP)), full((1, pwid)), full((1, rwid)), full((1, d)),
        ],
        out_shape=[
            jax.ShapeDtypeStruct((s, d), F32),
            jax.ShapeDtypeStruct((s, inw), BF16),
            jax.ShapeDtypeStruct((s, d), BF16),
            jax.ShapeDtypeStruct((s, d), BF16),
            jax.ShapeDtypeStruct((N_POOL_GROUPS, GROUP, GROUP), F32),
            jax.ShapeDtypeStruct((1, pwid), F32),
            jax.ShapeDtypeStruct((1, rwid), F32),
            jax.ShapeDtypeStruct((1, d), F32),
        ],
        scratch_shapes=[
            pltpu.VMEM((inw, d), BF16), pltpu.VMEM((d, d), BF16),
            pltpu.VMEM((RET_HEADS, GROUP, GROUP), F32), pltpu.VMEM((MAX_WINDOW, pwid), F32),
            pltpu.VMEM((tm, d), F32), pltpu.VMEM((tm, inw), BF16), pltpu.SemaphoreType.DMA((2 * NDEV,)),
        ],
        args=[dh2, h1, proj, proj, o_saved, rsave, gain, weights[0], weights[1], pool_w, pool_scale, ret_gain,
              consts["cos"], consts["sin"], consts["decay"], consts["ktail"], consts["qhead"]],
    )


def _adam(w, g, m, v):
    m = ADAM_B1 * m + (1.0 - ADAM_B1) * g
    v = ADAM_B2 * v + (1.0 - ADAM_B2) * jnp.square(g)
    m_hat = m / (1.0 - ADAM_B1 ** ADAM_STEP)
    v_hat = v / (1.0 - ADAM_B2 ** ADAM_STEP)
    delta = -ADAM_LR * (m_hat / (jnp.sqrt(v_hat) + ADAM_EPS) + ADAM_WD * w)
    return delta, m, v


def _adamw_big(my_chip, w, sums, parts, m, v, name):
    rows, d = w.shape
    tr = _row_tile(rows, 176)

    def body(chip_ref, w_ref, own_ref, p_ref, m_ref, v_ref, g_ref, d_ref, nm_ref, nv_ref):
        g = own_ref[0].astype(F32)
        for q in range(NCHIP - 1):
            g = g + p_ref[q].astype(F32)
        g_ref[...] = g
        d_ref[...], nm_ref[...], nv_ref[...] = _adam(w_ref[...], g, m_ref[...], v_ref[...])

    spec = pl.BlockSpec((tr, d), lambda i, chip: (i, 0))
    return _call(
        body,
        name=name,
        grid_spec=pltpu.PrefetchScalarGridSpec(
            num_scalar_prefetch=1,
            grid=(rows // tr,),
            in_specs=[
                spec,
                pl.BlockSpec((1, tr, d), lambda i, chip: (chip[0], i, 0)),
                pl.BlockSpec((NCHIP - 1, tr, d), lambda i, chip: (0, i, 0)),
                spec,
                spec,
            ],
            out_specs=[spec] * 4,
        ),
        out_shape=[jax.ShapeDtypeStruct((rows, d), F32)] * 4,
        compiler_params=_seq(1),
    )(my_chip, w, sums, parts, m, v)


def _adamw_small(stats_all, pw_all, ws, ms, vs, pwid):
    nsmall = len(ws)

    def body(*refs):
        st_ref, pwa_ref = refs[0], refs[1]
        w_refs = refs[2:2 + nsmall]
        m_refs = refs[2 + nsmall:2 + 2 * nsmall]
        v_refs = refs[2 + 2 * nsmall:2 + 3 * nsmall]
        outs = refs[2 + 3 * nsmall:]
        st = st_ref[0]
        pwg = pwa_ref[0]
        for q in range(1, NDEV):
            st = st + st_ref[q]
            pwg = pwg + pwa_ref[q]
        grads = [st[0:1, :], st[1:2, :], st[2:3, :], st[3:4, :], st[4:5, 0:pwid], st[4:5, pwid:2 * pwid], pwg]
        outs[0][...] = jnp.zeros((1, GROUP), F32) + jnp.sum(st[5:6, :])
        for j in range(nsmall):
            delta, nm, nv = _adam(w_refs[j][...], grads[j], m_refs[j][...], v_refs[j][...])
            outs[1 + 4 * j][...] = grads[j]
            outs[2 + 4 * j][...] = delta
            outs[3 + 4 * j][...] = nm
            outs[4 + 4 * j][...] = nv

    out_shape = [jax.ShapeDtypeStruct((1, GROUP), F32)]
    for w in ws:
        out_shape += [jax.ShapeDtypeStruct(w.shape, F32)] * 4
    return _call(body, name="adamw_small", out_shape=out_shape, compiler_params=_params())(
        stats_all, pw_all, *ws, *ms, *vs
    )


def kernel(x, ffn1_norm, ffn1_w1, ffn1_w3, ffn1_w2, mix_norm, w_in, pool_w, pool_scale, ret_norm, w_out, ffn2_norm, ffn2_w1, ffn2_w3, ffn2_w2, final_norm, loss_target, m_ffn1_norm, m_ffn1_w1, m_ffn1_w3, m_ffn1_w2, m_mix_norm, m_w_in, m_pool_w, m_pool_scale, m_ret_norm, m_w_out, m_ffn2_norm, m_ffn2_w1, m_ffn2_w3, m_ffn2_w2, m_final_norm, v_ffn1_norm, v_ffn1_w1, v_ffn1_w3, v_ffn1_w2, v_mix_norm, v_w_in, v_pool_w, v_pool_scale, v_ret_norm, v_w_out, v_ffn2_norm, v_ffn2_w1, v_ffn2_w3, v_ffn2_w2, v_final_norm):
    s, d = x.shape[1], x.shape[2]
    ffn = ffn1_w1.shape[2] * NDEV
    pwid = pool_scale.shape[1]
    xs, tgt = x[0], loss_target[0]
    consts = _mix_constants(s)
    pw3 = pool_w[0]
    fnorm = final_norm.reshape(1, d)

    rows_of = lambda w, transposed: w[0].T if transposed else w[0]
    send_f1 = [rows_of(w, t).astype(BF16) for w, t in ((ffn1_w1, True), (ffn1_w3, True), (ffn1_w2, False))]
    later = [rows_of(w, t) for w, t in ((w_in, True), (w_out, False), (ffn2_w1, True), (ffn2_w3, True), (ffn2_w2, False))]

    sent_later, w13_f1 = _comm_call([_CastRows(later), _Gather(send_f1[:2])], "gather_ffn1")
    send_mix, send_f2 = sent_later[:2], sent_later[2:]
    (a1, b1, hm1), ((w2_f1, w_in_all),) = _ffn_up(xs, ffn1_norm, *w13_f1, carries=[_Gather(send_f1[2:] + send_mix[:1])])
    w_f1 = w13_f1 + [w2_f1]
    (h1,), ((w_out_all, w1_f2),) = _ffn_down(xs, hm1, w2_f1, carries=[_Gather(send_mix[1:] + send_f2[:1])])
    w_mix = [w_in_all, w_out_all]
    (h2, proj, o_saved, rsave), (rest,) = _mix_fwd(
        h1, mix_norm, w_mix, pw3, pool_scale, ret_norm, consts, carries=[_Gather(send_f2[1:])]
    )
    w_f2 = [w1_f2] + rest
    dh3, a2, b2, hm2, dgf, loss_cols = _ffn_fwd_loss(h2, ffn2_norm, w_f2, fnorm, tgt)

    (dh2, da2, db2, n2, dg2), _ = _ffn_bwd(dh3, h2, a2, b2, ffn2_norm, w_f2, ffn, "ffn2_bwd")
    sum_f2w1, _ = _wgrad(da2, n2, 1.0, "ffn2_w1_grad")
    sum_f2w3, _ = _wgrad(db2, n2, 1.0, "ffn2_w3_grad")
    sum_f2w2, ((parts_f2w1,),) = _wgrad(hm2, dh3, 0.5, "ffn2_w2_grad", carries=[_ChipScatter([sum_f2w1])])

    (dh1, dproj, u, mm, dpw, dps, drg, dgm), ((parts_f2w3, parts_f2w2),) = _mix_bwd(
        dh2, h1, proj, o_saved, rsave, mix_norm, w_mix, pw3, pool_scale, ret_norm, consts,
        carries=[_ChipScatter([sum_f2w3, sum_f2w2])],
    )
    (dx, da1, db1, n1, dg1), _ = _ffn_bwd(dh1, xs, a1, b1, ffn1_norm, w_f1, ffn, "ffn1_bwd")
    stats = jnp.concatenate(
        [dg1, dgm, dg2, dgf, jnp.concatenate([dps, drg], axis=1), loss_cols, jnp.zeros((2, d), F32)], axis=0
    )
    small = _GatherDirect([stats, dpw.reshape(N_POOL_GROUPS * GROUP, GROUP)])
    sum_f1w2, ((stats_all, pw_all),) = _wgrad(hm1, dh1, 0.5, "ffn1_w2_grad", carries=[small])
    sum_f1w1, ((parts_f1w2,),) = _wgrad(da1, n1, 1.0, "ffn1_w1_grad", carries=[_ChipScatter([sum_f1w2])])
    sum_f1w3, ((parts_f1w1,),) = _wgrad(db1, n1, 1.0, "ffn1_w3_grad", carries=[_ChipScatter([sum_f1w1])])
    sum_in, ((parts_f1w3,),) = _wgrad(dproj, u, 1.0, "w_in_grad", carries=[_ChipScatter([sum_f1w3])])
    sum_out, ((parts_in,),) = _wgrad(mm, dh2, 1.0, "w_out_grad", carries=[_ChipScatter([sum_in])])
    ((parts_out,),) = _comm_call([_ChipScatter([sum_out])], "scatter_last")

    big = (
        (ffn1_w1, m_ffn1_w1, v_ffn1_w1, sum_f1w1, parts_f1w1, True),
        (ffn1_w3, m_ffn1_w3, v_ffn1_w3, sum_f1w3, parts_f1w3, True),
        (ffn1_w2, m_ffn1_w2, v_ffn1_w2, sum_f1w2, parts_f1w2, False),
        (w_in, m_w_in, v_w_in, sum_in, parts_in, True),
        (w_out, m_w_out, v_w_out, sum_out, parts_out, False),
        (ffn2_w1, m_ffn2_w1, v_ffn2_w1, sum_f2w1, parts_f2w1, True),
        (ffn2_w3, m_ffn2_w3, v_ffn2_w3, sum_f2w3, parts_f2w3, True),
        (ffn2_w2, m_ffn2_w2, v_ffn2_w2, sum_f2w2, parts_f2w2, False),
    )
    my_chip = jnp.reshape(_chip(_peer(0)), (1,)).astype(jnp.int32)
    big_out = []
    for j, (w, m, v, sums, parts, t) in enumerate(big):
        view = (lambda a: a[0].T) if t else (lambda a: a[0])
        back = (lambda a: a.T[None]) if t else (lambda a: a[None])
        big_out.append([back(a) for a in _adamw_big(my_chip, view(w), sums, parts, view(m), view(v), "adamw_%d" % j)])

    small_w = (ffn1_norm, mix_norm, ffn2_norm, fnorm, pool_scale, ret_norm, pw3.reshape(-1, GROUP))
    small_m = (m_ffn1_norm, m_mix_norm, m_ffn2_norm, m_final_norm.reshape(1, d), m_pool_scale, m_ret_norm, m_pool_w.reshape(-1, GROUP))
    small_v = (v_ffn1_norm, v_mix_norm, v_ffn2_norm, v_final_norm.reshape(1, d), v_pool_scale, v_ret_norm, v_pool_w.reshape(-1, GROUP))
    res = _adamw_small(stats_all, pw_all, small_w, small_m, small_v, pwid)
    loss = res[0][0, 0]
    small_out = [list(res[1 + 4 * j:5 + 4 * j]) for j in range(len(small_w))]
    small_out[3] = [a.reshape(d) for a in small_out[3]]
    small_out[6] = [a.reshape(pool_w.shape) for a in small_out[6]]

    order = [small_out[0], big_out[0], big_out[1], big_out[2], small_out[1], big_out[3], small_out[6], small_out[4],
             small_out[5], big_out[4], small_out[2], big_out[5], big_out[6], big_out[7], small_out[3]]
    result = [loss, dx[None]]
    for kind in range(4):
        result += [t[kind] for t in order]
    return tuple(result)
```

```python
import functools

import numpy as np
import jax
import jax.numpy as jnp
from jax import lax
from jax.experimental import pallas as pl
from jax.experimental.pallas import tpu as pltpu

F32 = jnp.float32
BF16 = jnp.bfloat16

NDEV = 8
NCHIP = 4
EPS = 1e-6
N_POOL_GROUPS = 4
POOL_WINDOWS = (2, 4, 8, 16)
MAX_WINDOW = 16
GROUP = 128
RET_HEADS = 4
ROPE_BASE = 10000.0
ADAM_LR = 0.001
ADAM_B1 = 0.9
ADAM_B2 = 0.999
ADAM_EPS = 1e-08
ADAM_WD = 0.01
ADAM_STEP = 10

VMEM_LIMIT = 56 * 1024 * 1024
FFN_CHUNK = 256
ROW_BAND = 32

NT = (((1,), (1,)), ((), ()))
NN = (((1,), (0,)), ((), ()))
TN = (((0,), (0,)), ((), ()))

ANY = pl.BlockSpec(memory_space=pl.ANY)


def _dot(a, b, dims):
    return lax.dot_general(a, b, dims, preferred_element_type=F32)


def _call(body, **kw):
    return pl.pallas_call(body, **kw)


def _params(**kw):
    return pltpu.CompilerParams(vmem_limit_bytes=VMEM_LIMIT, **kw)


def _seq(n):
    return _params(dimension_semantics=("arbitrary",) * n)


def _peer(k):
    x, y, c = lax.axis_index("x"), lax.axis_index("y"), lax.axis_index("c")
    return (1 - x if k & 4 else x, 1 - y if k & 2 else y, 1 - c if k & 1 else c)


def _flat(pos):
    return 4 * pos[0] + 2 * pos[1] + pos[2]


def _chip(pos):
    return 2 * pos[0] + pos[1]


def _row_tile(rows, cap):
    return max(t for t in range(16, min(rows, cap) + 1, 16) if rows % t == 0)


def _pieces(rows, n):
    tiles = rows // 16
    cuts = [16 * (tiles * q // n) for q in range(n + 1)]
    return [(a, b - a) for a, b in zip(cuts[:-1], cuts[1:])]


def _load_weights(parts, sems):
    copies = []
    for buf, dst in parts:
        rows = buf.shape[1]
        for p in range(NDEV):
            cp = pltpu.make_async_copy(buf.at[p], dst.at[pl.ds(p * rows, rows), :], sems.at[len(copies)])
            cp.start()
            copies.append(cp)
    return copies


def _sigmoid(a):
    return 1.0 / (1.0 + jnp.exp(-a))


def _remote(src, dst, send_sem, recv_sem, to):
    return pltpu.make_async_remote_copy(
        src_ref=src, dst_ref=dst, send_sem=send_sem, recv_sem=recv_sem, device_id=to, device_id_type=pl.DeviceIdType.MESH
    )


class _Gather:
    X, Y, FAR = 4, 2, 6
    peers = (1, 2, 4)
    COPIES = 8

    def __init__(self, shards):
        n = len(shards)
        self.operands = list(shards)
        self.out_shape = [jax.ShapeDtypeStruct((NDEV,) + a.shape, a.dtype) for a in shards]
        self.sems = [
            pltpu.SemaphoreType.DMA((self.COPIES * n,)), pltpu.SemaphoreType.DMA((self.COPIES * n,)),
            pltpu.SemaphoreType.DMA((n,)),
        ]
        self.stages = [self.begin, self.relay, self.relay_far, self.end]

    def _copy(self, t, k, block, to, ins, outs, sems, own=False, half=None):
        rows = outs[t].shape[1]
        part = pl.ds(0, rows) if half is None else pl.ds(half * (rows // 2), rows // 2)
        dst = outs[t].at[_flat(block), part, :]
        at = self.COPIES * t + k
        return _remote(ins[t] if own else dst, dst, sems[0].at[at], sems[1].at[at], to)

    def _local(self, t, ins, outs, sems):
        return pltpu.make_async_copy(ins[t], outs[t].at[_flat(_peer(0))], sems[2].at[t])

    def begin(self, ins, outs, sems):
        me = _peer(0)
        for t in range(len(ins)):
            self._local(t, ins, outs, sems).start()
            for k, code in enumerate((1, self.X, self.Y)):
                self._copy(t, k, me, _peer(code), ins, outs, sems, own=True).start()

    def relay(self, ins, outs, sems):
        me, sibling = _peer(0), _peer(1)
        for t in range(len(ins)):
            self._copy(t, 1, _peer(self.X), me, ins, outs, sems).wait_recv()
            self._copy(t, 3, _peer(self.X), _peer(self.Y), ins, outs, sems, half=0).start()
            self._copy(t, 5, _peer(self.X), sibling, ins, outs, sems).start()
            self._copy(t, 2, _peer(self.Y), me, ins, outs, sems).wait_recv()
            self._copy(t, 4, _peer(self.Y), _peer(self.X), ins, outs, sems, half=1).start()
            self._copy(t, 6, _peer(self.Y), sibling, ins, outs, sems).start()

    def relay_far(self, ins, outs, sems):
        me, sibling = _peer(0), _peer(1)
        for t in range(len(ins)):
            self._copy(t, 3, _peer(self.FAR), me, ins, outs, sems, half=0).wait_recv()
            self._copy(t, 4, _peer(self.FAR), me, ins, outs, sems, half=1).wait_recv()
            self._copy(t, 7, _peer(self.FAR), sibling, ins, outs, sems).start()

    def end(self, ins, outs, sems):
        me = _peer(0)
        for t in range(len(ins)):
            self._copy(t, 0, _peer(1), me, ins, outs, sems).wait_recv()
            for k, code in ((5, self.X), (6, self.Y), (7, self.FAR)):
                self._copy(t, k, _peer(code ^ 1), me, ins, outs, sems).wait_recv()
            for k in range(self.COPIES):
                self._copy(t, k, me, me, ins, outs, sems, half=0 if k == 3 else 1 if k == 4 else None).wait_send()
            self._local(t, ins, outs, sems).wait()


class _GatherDirect:
    peers = tuple(range(1, NDEV))

    def __init__(self, arrays):
        n = len(arrays)
        self.operands = list(arrays)
        self.out_shape = [jax.ShapeDtypeStruct((NDEV,) + a.shape, a.dtype) for a in arrays]
        self.sems = [pltpu.SemaphoreType.DMA((7 * n,)), pltpu.SemaphoreType.DMA((7 * n,)), pltpu.SemaphoreType.DMA((n,))]
        self.stages = [self.begin, self.end]

    def begin(self, ins, outs, sems):
        mine = _flat(_peer(0))
        for t in range(len(ins)):
            pltpu.make_async_copy(ins[t], outs[t].at[mine], sems[2].at[t]).start()
            for k in range(1, NDEV):
                _remote(ins[t], outs[t].at[mine], sems[0].at[7 * t + k - 1], sems[1].at[7 * t + k - 1], _peer(k)).start()

    def end(self, ins, outs, sems):
        mine = _flat(_peer(0))
        for t in range(len(ins)):
            for k in range(1, NDEV):
                cp = _remote(ins[t], outs[t].at[_flat(_peer(k))], sems[0].at[7 * t + k - 1], sems[1].at[7 * t + k - 1], _peer(k))
                cp.wait_recv()
                cp.wait_send()
            pltpu.make_async_copy(ins[t], outs[t].at[mine], sems[2].at[t]).wait()


class _ChipScatter:
    peers = (2, 4, 6)
    across = (4, 2, 6)

    def __init__(self, sums):
        n = len(sums) * len(self.across)
        self.operands = list(sums)
        self.out_shape = [jax.ShapeDtypeStruct((len(self.across),) + a.shape[1:], a.dtype) for a in sums]
        self.sems = [pltpu.SemaphoreType.DMA((n,)), pltpu.SemaphoreType.DMA((n,))]
        self.stages = [self.begin, self.end]

    def _copies(self, ins, outs, sems):
        copies = []
        for t in range(len(ins)):
            for slot, k in enumerate(self.across):
                at = len(copies)
                copies.append(
                    _remote(ins[t].at[_chip(_peer(k))], outs[t].at[slot], sems[0].at[at], sems[1].at[at], _peer(k))
                )
        return copies

    def begin(self, ins, outs, sems):
        for cp in self._copies(ins, outs, sems):
            cp.start()

    def end(self, ins, outs, sems):
        for cp in self._copies(ins, outs, sems):
            cp.wait_recv()
            cp.wait_send()


class _CastRows:
    peers = ()

    def __init__(self, arrays):
        n = len(arrays)
        self.operands = list(arrays)
        self.out_shape = [jax.ShapeDtypeStruct(a.shape, BF16) for a in arrays]
        self.sems = [pltpu.SemaphoreType.DMA((n,)), pltpu.SemaphoreType.DMA((n,))]
        self.sems += [pltpu.VMEM(a.shape, F32) for a in arrays] + [pltpu.VMEM(a.shape, BF16) for a in arrays]
        self.stages = [self.begin, self.convert, self.end]

    def _moves(self, t, ins, outs, scratch):
        n = len(ins)
        load = pltpu.make_async_copy(ins[t], scratch[2 + t], scratch[0].at[t])
        store = pltpu.make_async_copy(scratch[2 + n + t], outs[t], scratch[1].at[t])
        return load, store

    def begin(self, ins, outs, scratch):
        for t in range(len(ins)):
            self._moves(t, ins, outs, scratch)[0].start()

    def convert(self, ins, outs, scratch):
        n = len(ins)
        for t in range(n):
            load, store = self._moves(t, ins, outs, scratch)
            load.wait()
            scratch[2 + n + t][...] = scratch[2 + t][...].astype(BF16)
            store.start()

    def end(self, ins, outs, scratch):
        for t in range(len(ins)):
            self._moves(t, ins, outs, scratch)[1].wait()


def _split_refs(refs, counts):
    out, at = [], 0
    for n in counts:
        out.append(refs[at:at + n])
        at += n
    return out


BARRIER_IDS = {(2, 4, 6): 0, (1, 2, 4): 1, (1,): 2, (1, 2, 4, 6): 3, tuple(range(1, NDEV)): 4}


def _peers_of(carries, own=()):
    peers = tuple(sorted(set(own).union(*[c.peers for c in carries])))
    return (peers, BARRIER_IDS[peers]) if peers in BARRIER_IDS else (None, None)


def _announce(peers):
    barrier = pltpu.get_barrier_semaphore()
    for k in peers:
        pl.semaphore_signal(barrier, inc=1, device_id=_peer(k), device_id_type=pl.DeviceIdType.MESH)


def _await(peers):
    pl.semaphore_wait(pltpu.get_barrier_semaphore(), len(peers))


def _handshake(peers):
    _announce(peers)
    _await(peers)


def _comm_call(carries, name):
    nin = [len(c.operands) for c in carries]
    nout = [len(c.out_shape) for c in carries]
    nsem = [len(c.sems) for c in carries]
    peers, collective_id = _peers_of(carries)

    def body(*refs):
        if peers:
            _handshake(peers)
        ins, outs, sems = _split_refs(refs, (sum(nin), sum(nout), sum(nsem)))
        parts = list(zip(carries, _split_refs(ins, nin), _split_refs(outs, nout), _split_refs(sems, nsem)))
        for depth in range(max(len(c.stages) for c in carries)):
            for c, i, o, s in parts:
                if depth < len(c.stages) - 1:
                    c.stages[depth](i, o, s)
        for c, i, o, s in parts:
            c.stages[-1](i, o, s)

    res = _call(
        body,
        name=name,
        out_shape=[sh for c in carries for sh in c.out_shape],
        in_specs=[ANY] * sum(nin),
        out_specs=[ANY] * sum(nout),
        scratch_shapes=[sm for c in carries for sm in c.sems],
        compiler_params=_params(has_side_effects=True, collective_id=collective_id),
    )(*[a for c in carries for a in c.operands])
    return _split_refs(list(res), nout)


def _grid_call(body, carries, *, name, grid, in_specs, out_specs, out_shape, scratch_shapes, args, own_peers=(),
               work_first=False):
    ni, no, ns = len(in_specs), len(out_specs), len(scratch_shapes)
    nin = [len(c.operands) for c in carries]
    nout = [len(c.out_shape) for c in carries]
    nsem = [len(c.sems) for c in carries]
    steps = int(np.prod(grid))
    peers, collective_id = _peers_of(carries, own_peers)

    def when_of(stage, count):
        first, last = (5 * steps) // 8 - 1, steps - 2
        return max(0, last if count <= 3 else first + (last - first) * (stage - 1) // (count - 3))

    def wrapped(*refs):
        ins, cins, outs, couts, scr, csems = _split_refs(refs, (ni, sum(nin), no, sum(nout), ns, sum(nsem)))
        if not carries and not peers:
            return body(*ins, *outs, *scr)
        parts = list(zip(carries, _split_refs(cins, nin), _split_refs(couts, nout), _split_refs(csems, nsem)))
        step = pl.program_id(0)
        for axis in range(1, len(grid)):
            step = step * grid[axis] + pl.program_id(axis)

        def first_stage():
            if peers:
                _await(peers)
            for c, i, o, s in parts:
                c.stages[0](i, o, s)

        if peers:
            pl.when(step == 0)(functools.partial(_announce, peers))
        if not work_first:
            pl.when(step == 0)(first_stage)

        body(*ins, *outs, *scr)

        if work_first:
            pl.when(step == 0)(first_stage)

        for c, i, o, s in parts:
            for stage in range(1, len(c.stages) - 1):
                pl.when(step == when_of(stage, len(c.stages)))(functools.partial(c.stages[stage], i, o, s))

        @pl.when(step == steps - 1)
        def _():
            for c, i, o, s in parts:
                c.stages[-1](i, o, s)

    res = _call(
        wrapped,
        name=name,
        grid=tuple(grid),
        in_specs=list(in_specs) + [ANY] * sum(nin),
        out_specs=list(out_specs) + [ANY] * sum(nout),
        out_shape=list(out_shape) + [sh for c in carries for sh in c.out_shape],
        scratch_shapes=list(scratch_shapes) + [sm for c in carries for sm in c.sems],
        compiler_params=_params(dimension_semantics=("arbitrary",) * len(grid), collective_id=collective_id),
    )(*args, *[a for c in carries for a in c.operands])
    res = list(res)
    return res[:no], _split_refs(res[no:], nout)


def _chunks(width):
    return [(at, min(FFN_CHUNK, width - at)) for at in range(0, width, FFN_CHUNK)]


def _ffn_fwd_loss(x, gain, weights, final_gain, target):
    s, d = x.shape
    ffn = weights[0].shape[0] * weights[0].shape[1]
    tm = min(512, s)

    def body(x_ref, g_ref, b1, b3, b2, gf_ref, t_ref, dh_ref, a_ref, b_ref, hm_ref, dgf_ref, loss_ref, w1s, w3s, w2s, sems):
        @pl.when(pl.program_id(0) == 0)
        def _():
            for cp in _load_weights(((b1, w1s), (b3, w3s), (b2, w2s)), sems):
                cp.wait()
            dgf_ref[...] = jnp.zeros_like(dgf_ref)
            loss_ref[...] = jnp.zeros_like(loss_ref)

        xv = x_ref[...]
        r = lax.rsqrt(jnp.mean(xv * xv, axis=-1, keepdims=True) + EPS)
        n = (xv * r * g_ref[...]).astype(BF16)
        acc = jnp.zeros((tm, d), F32)
        for at, width in _chunks(ffn):
            cols = slice(at, at + width)
            a = _dot(n, w1s[cols, :], NT)
            b = _dot(n, w3s[cols, :], NT)
            a_ref[:, cols] = a.astype(BF16)
            b_ref[:, cols] = b.astype(BF16)
            hm = (a * _sigmoid(a) * b).astype(BF16)
            hm_ref[:, cols] = hm
            acc = acc + _dot(hm, w2s[cols, :], NN)
        h = xv + 0.5 * acc
        rf = lax.rsqrt(jnp.mean(h * h, axis=-1, keepdims=True) + EPS)
        nh = h * rf
        gf = gf_ref[...]
        err = nh * gf - t_ref[...]
        loss_ref[...] += jnp.sum(err * err, axis=0, keepdims=True) * (0.5 / d)
        dy = err * (1.0 / d)
        dgf_ref[...] += jnp.sum(dy * nh, axis=0, keepdims=True)
        dn = dy * gf
        dh_ref[...] = rf * (dn - nh * jnp.mean(dn * nh, axis=-1, keepdims=True))

    tile = pl.BlockSpec((tm, d), lambda i: (i, 0))
    row = pl.BlockSpec((1, d), lambda i: (0, 0))
    wide = pl.BlockSpec((tm, ffn), lambda i: (i, 0))
    return _call(
        body,
        name="ffn_fwd_loss",
        grid=(s // tm,),
        in_specs=[tile, row, ANY, ANY, ANY, row, tile],
        out_specs=[tile, wide, wide, wide, row, row],
        out_shape=[jax.ShapeDtypeStruct((s, d), F32)] + [jax.ShapeDtypeStruct((s, ffn), BF16)] * 3
        + [jax.ShapeDtypeStruct((1, d), F32)] * 2,
        scratch_shapes=[pltpu.VMEM((ffn, d), BF16)] * 3 + [pltpu.SemaphoreType.DMA((3 * NDEV,))],
        compiler_params=_seq(1),
    )(x, gain, *weights, final_gain, target)


def _ffn_up(x, gain, w1, w3, carries=()):
    s, d = x.shape
    ffn = w1.shape[0] * w1.shape[1]
    tm = min(512, s)

    def body(x_ref, g_ref, b1, b3, a_ref, b_ref, hm_ref, w1s, w3s, sems):
        @pl.when(pl.program_id(0) == 0)
        def _():
            for cp in _load_weights(((b1, w1s), (b3, w3s)), sems):
                cp.wait()

        xv = x_ref[...]
        r = lax.rsqrt(jnp.mean(xv * xv, axis=-1, keepdims=True) + EPS)
        n = (xv * r * g_ref[...]).astype(BF16)
        for at, width in _chunks(ffn):
            cols = slice(at, at + width)
            a = _dot(n, w1s[cols, :], NT)
            b = _dot(n, w3s[cols, :], NT)
            a_ref[:, cols] = a.astype(BF16)
            b_ref[:, cols] = b.astype(BF16)
            hm_ref[:, cols] = (a * _sigmoid(a) * b).astype(BF16)

    wide = pl.BlockSpec((tm, ffn), lambda i: (i, 0))
    return _grid_call(
        body,
        carries,
        name="ffn_up",
        grid=(s // tm,),
        in_specs=[pl.BlockSpec((tm, d), lambda i: (i, 0)), pl.BlockSpec((1, d), lambda i: (0, 0)), ANY, ANY],
        out_specs=[wide] * 3,
        out_shape=[jax.ShapeDtypeStruct((s, ffn), BF16)] * 3,
        scratch_shapes=[pltpu.VMEM((ffn, d), BF16)] * 2 + [pltpu.SemaphoreType.DMA((2 * NDEV,))],
        args=[x, gain, w1, w3],
    )


def _ffn_down(x, hm, w2, carries=()):
    s, d = x.shape
    ffn = w2.shape[0] * w2.shape[1]
    tm = min(512, s)

    def body(x_ref, hm_ref, b2, h_ref, w2s, sems):
        @pl.when(pl.program_id(0) == 0)
        def _():
            for cp in _load_weights(((b2, w2s),), sems):
                cp.wait()

        acc = jnp.zeros((tm, d), F32)
        for at, width in _chunks(ffn):
            cols = slice(at, at + width)
            acc = acc + _dot(hm_ref[:, cols], w2s[cols, :], NN)
        h_ref[...] = x_ref[...] + 0.5 * acc

    tile = pl.BlockSpec((tm, d), lambda i: (i, 0))
    return _grid_call(
        body,
        carries,
        name="ffn_down",
        grid=(s // tm,),
        in_specs=[tile, pl.BlockSpec((tm, ffn), lambda i: (i, 0)), ANY],
        out_specs=[tile],
        out_shape=[jax.ShapeDtypeStruct((s, d), F32)],
        scratch_shapes=[pltpu.VMEM((ffn, d), BF16), pltpu.SemaphoreType.DMA((NDEV,))],
        args=[x, hm, w2],
    )


def _ffn_bwd(dh, x, a, b, gain, weights, ffn, name, carries=()):
    s, d = x.shape
    tm = min(512, s)
    halves = 2
    fh = ffn // halves

    def body(dh_ref, x_ref, a_ref, b_ref, g_ref, b1, b3, b2, dx_ref, da_ref, db_ref, n_ref, dg_ref, w1s, w3s, w2s, sems):
        i, j = pl.program_id(0), pl.program_id(1)

        @pl.when((i == 0) & (j == 0))
        def _():
            for cp in _load_weights(((b1, w1s), (b3, w3s), (b2, w2s)), sems):
                cp.wait()
            dg_ref[...] = jnp.zeros_like(dg_ref)

        @pl.when(j == 0)
        def _():
            dx_ref[...] = jnp.zeros_like(dx_ref)

        dob = (0.5 * dh_ref[...]).astype(BF16)
        chunks = _chunks(fh)

        def dhm_of(k):
            at, width = chunks[k]
            return _dot(dob, w2s[pl.ds(pl.multiple_of(j * fh + at, GROUP), width), :], NT)

        ahead = dhm_of(0)
        for k, (at, width) in enumerate(chunks):
            cols = slice(at, at + width)
            dhm = ahead
            if k + 1 < len(chunks):
                ahead = dhm_of(k + 1)
            for top in range(0, tm, ROW_BAND):
                band = slice(top, top + ROW_BAND)
                av = a_ref[band, cols].astype(F32)
                bv = b_ref[band, cols].astype(F32)
                sg = _sigmoid(av)
                dv = dhm[band]
                da_ref[band, cols] = (dv * bv * (sg * (1.0 + av * (1.0 - sg)))).astype(BF16)
                db_ref[band, cols] = (dv * (av * sg)).astype(BF16)
        half = pl.ds(pl.multiple_of(j * fh, GROUP), fh)
        dx_ref[...] += _dot(da_ref[...], w1s[half, :], NN) + _dot(db_ref[...], w3s[half, :], NN)

        @pl.when(j == halves - 1)
        def _():
            xv = x_ref[...]
            g = g_ref[...]
            r = lax.rsqrt(jnp.mean(xv * xv, axis=-1, keepdims=True) + EPS)
            nh = xv * r
            n_ref[...] = (nh * g).astype(BF16)
            total = dx_ref[...]
            dg_ref[...] += jnp.sum(total * nh, axis=0, keepdims=True)
            dnh = total * g
            dx_ref[...] = dh_ref[...] + r * (dnh - nh * jnp.mean(dnh * nh, axis=-1, keepdims=True))

    tile = pl.BlockSpec((tm, d), lambda i, j: (i, 0))
    row = pl.BlockSpec((1, d), lambda i, j: (0, 0))
    wide = pl.BlockSpec((tm, fh), lambda i, j: (i, j))
    return _grid_call(
        body,
        carries,
        name=name,
        grid=(s // tm, halves),
        in_specs=[tile, tile, wide, wide, row, ANY, ANY, ANY],
        out_specs=[tile, wide, wide, tile, row],
        out_shape=[
            jax.ShapeDtypeStruct((s, d), F32),
            jax.ShapeDtypeStruct((s, ffn), BF16),
            jax.ShapeDtypeStruct((s, ffn), BF16),
            jax.ShapeDtypeStruct((s, d), BF16),
            jax.ShapeDtypeStruct((1, d), F32),
        ],
        scratch_shapes=[pltpu.VMEM((ffn, d), BF16)] * 3 + [pltpu.SemaphoreType.DMA((3 * NDEV,))],
        args=[dh, x, a, b, gain] + list(weights),
    )


SWAP_PIECES = 1


def _wgrad(lhs, rhs, scale, name, carries=()):
    s, m = lhs.shape
    n = rhs.shape[1]
    rs = m // NDEV
    tk = min(1024, s)
    steps = s // tk
    pieces = [(j, at, size) for j in range(2) for at, size in _pieces(rs, SWAP_PIECES)]

    def body(l_ref, r_ref, o_ref, acc, mine, theirs, send_sems, recv_sems):
        h, k = pl.program_id(0), pl.program_id(1)

        @pl.when(k == 0)
        def _():
            acc[...] = _dot(l_ref[...], r_ref[...].astype(BF16), TN)

        @pl.when(k > 0)
        def _():
            acc[...] += _dot(l_ref[...], r_ref[...].astype(BF16), TN)

        def exchange(half):
            c = lax.axis_index("c")
            return [
                _remote(mine.at[half, 1 - c, j, pl.ds(at, size), :], theirs.at[half, j, pl.ds(at, size), :],
                        send_sems.at[half * len(pieces) + q], recv_sems.at[half * len(pieces) + q], _peer(1))
                for q, (j, at, size) in enumerate(pieces)
            ]

        def settle(half):
            for cp in exchange(half):
                cp.wait_recv()
            both = mine[half, lax.axis_index("c")].astype(F32) + theirs[half].astype(F32)
            o_ref[2 * half:2 * half + 2] = both.astype(BF16)
            for cp in exchange(half):
                cp.wait_send()

        for half in range(2):
            @pl.when((h == half) & (k == steps - 1))
            def _():
                for p in range(NCHIP):
                    mine[half, p % 2, p // 2] = (acc[p * rs:(p + 1) * rs, :] * scale).astype(BF16)
                for cp in exchange(half):
                    cp.start()
                if half == 1:
                    settle(0)
                    settle(1)

    (out,), carried = _grid_call(
        body,
        carries,
        name=name,
        grid=(2, steps),
        in_specs=[pl.BlockSpec((tk, m // 2), lambda h, k: (k, h)), pl.BlockSpec((tk, n), lambda h, k: (k, 0))],
        out_specs=[pl.BlockSpec((NCHIP, rs, n), lambda h, k: (0, 0, 0))],
        out_shape=[jax.ShapeDtypeStruct((NCHIP, rs, n), BF16)],
        scratch_shapes=[
            pltpu.VMEM((m // 2, n), F32), pltpu.VMEM((2, 2, 2, rs, n), BF16), pltpu.VMEM((2, 2, rs, n), BF16),
            pltpu.SemaphoreType.DMA((2 * len(pieces),)), pltpu.SemaphoreType.DMA((2 * len(pieces),)),
        ],
        args=[lhs, rhs],
        own_peers=(1,),
    )
    return out, carried


def _mix_constants(s):
    c = GROUP
    lg = np.log1p(-np.exp2(-5.0 - np.arange(RET_HEADS, dtype=np.float32))).astype(np.float32)
    pos = np.arange(c, dtype=np.float32)
    rel = pos[:, None] - pos[None, :]
    decay = np.where(rel[None] >= 0, np.exp(lg[:, None, None] * np.maximum(rel, 0.0)[None]), 0.0).astype(np.float32)
    ktail = np.exp(lg[:, None] * (c - 1 - pos)[None, :]).astype(np.float32)
    qhead = np.exp(lg[:, None] * (pos + 1.0)[None, :]).astype(np.float32)
    chunk_decay = [float(v) for v in np.exp(lg * np.float32(c)).astype(np.float32)]
    ones = np.ones((1, 1, c), np.float32)
    inv_freq = (1.0 / (np.float32(ROPE_BASE) ** (np.arange(0, c, 2, dtype=np.float32) / np.float32(c)))).astype(np.float32)
    ang = (np.arange(s, dtype=np.float32)[:, None] * inv_freq[None, :]).astype(np.float32)
    cos, sin = np.cos(ang).astype(np.float32), np.sin(ang).astype(np.float32)
    return dict(
        decay=jnp.asarray(decay),
        ktail=jnp.asarray(ktail[:, :, None] * ones),
        qhead=jnp.asarray(qhead[:, :, None] * ones),
        chunk_decay=chunk_decay,
        cos=jnp.asarray(np.concatenate([cos, cos], axis=-1)),
        sin=jnp.asarray(np.concatenate([-sin, sin], axis=-1)),
    )


def _rope(t, cos, sin):
    return t * cos + pltpu.roll(t, GROUP // 2, axis=1) * sin


def _rope_bwd(dt, cos, sin):
    return dt * cos + pltpu.roll(dt * sin, GROUP // 2, axis=1)


def _window_sums(ext, w, forward):
    rows = ext.shape[0]
    acc, k = ext, 1
    while k < w:
        acc = acc + pltpu.roll(acc, k if forward else rows - k, axis=0)
        k *= 2
    return acc


def _pool_counts(tile, tm, w):
    t = lax.broadcasted_iota(jnp.int32, (tm, 1), 0) + tile * tm
    return jnp.minimum(t + 1, w).astype(F32)


def _mix_fwd(h1, gain, weights, pool_w, pool_scale, ret_gain, consts, carries=()):
    s, d = h1.shape
    pwid = N_POOL_GROUPS * GROUP
    rwid = RET_HEADS * GROUP
    inw = pwid + 4 * rwid
    tm = min(256, s)
    nck = tm // GROUP
    cd = consts["chunk_decay"]

    def body(h_ref, g_ref, bin_, bout, pw_ref, ps_ref, rg_ref, cos_ref, sin_ref, dec_ref, kt_ref, qh_ref,
             h2_ref, proj_ref, o_ref, rs_ref, wins, wouts, state, carry, mbuf, sems):
        i = pl.program_id(0)

        @pl.when(i == 0)
        def _():
            for cp in _load_weights(((bin_, wins), (bout, wouts)), sems):
                cp.wait()
            state[...] = jnp.zeros_like(state)
            carry[...] = jnp.zeros_like(carry)

        hv = h_ref[...]
        r = lax.rsqrt(jnp.mean(hv * hv, axis=-1, keepdims=True) + EPS)
        u = (hv * r * g_ref[...]).astype(BF16)
        proj_ref[...] = _dot(u, wins[...], NT)

        ext = jnp.concatenate([carry[...], proj_ref[:, 0:pwid]], axis=0)
        carry[...] = proj_ref[tm - MAX_WINDOW:tm, 0:pwid]
        for gi, w in enumerate(POOL_WINDOWS):
            cols = slice(gi * GROUP, (gi + 1) * GROUP)
            xg = ext[:, cols]
            ws = _window_sums(xg, w, True)[MAX_WINDOW:, :]
            pooled = ws / _pool_counts(i, tm, w) - xg[MAX_WINDOW:, :]
            z = _dot(pooled.astype(BF16), pw_ref[gi].astype(BF16), NN)
            mbuf[:, cols] = (z * ps_ref[:, cols]).astype(BF16)

        cos, sin = cos_ref[...], sin_ref[...]
        for h in range(RET_HEADS):
            cq = slice(pwid + h * GROUP, pwid + (h + 1) * GROUP)
            ck = slice(pwid + rwid + h * GROUP, pwid + rwid + (h + 1) * GROUP)
            cv = slice(pwid + 2 * rwid + h * GROUP, pwid + 2 * rwid + (h + 1) * GROUP)
            cg = slice(pwid + 3 * rwid + h * GROUP, pwid + 3 * rwid + (h + 1) * GROUP)
            ch = slice(h * GROUP, (h + 1) * GROUP)
            qr = _rope(proj_ref[:, cq], cos, sin)
            kr = _rope(proj_ref[:, ck], cos, sin) * (GROUP ** -0.5)
            vb = proj_ref[:, cv].astype(BF16)
            for n in range(nck):
                rows = slice(n * GROUP, (n + 1) * GROUP)
                qc, kc, vc = qr[rows], kr[rows], vb[rows]
                rb = state[h]
                rs_ref[n, h] = rb
                p = (_dot(qc.astype(BF16), kc.astype(BF16), NT) * dec_ref[h]).astype(BF16)
                o = _dot(p, vc, NN) + _dot((qc * qh_ref[h]).astype(BF16), rb.astype(BF16), NN)
                state[h] = cd[h] * rb + _dot((kc * kt_ref[h]).astype(BF16), vc, TN)
                o_ref[rows, ch] = o
                on = o * lax.rsqrt(jnp.mean(o * o, axis=-1, keepdims=True) + EPS)
                gv = proj_ref[rows, cg]
                mbuf[rows, pwid + h * GROUP:pwid + (h + 1) * GROUP] = (
                    gv * _sigmoid(gv) * (on * rg_ref[:, ch])
                ).astype(BF16)
        h2_ref[...] = hv + _dot(mbuf[...], wouts[...], NN)

    tile = pl.BlockSpec((tm, d), lambda i: (i, 0))
    full = lambda shape: pl.BlockSpec(shape, lambda i: (0,) * len(shape))
    return _grid_call(
        body,
        carries,
        name="mix_fwd",
        work_first=True,
        grid=(s // tm,),
        in_specs=[
            tile, full((1, d)), ANY, ANY,
            full((N_POOL_GROUPS, GROUP, GROUP)), full((1, pwid)), full((1, rwid)),
            pl.BlockSpec((tm, GROUP), lambda i: (i, 0)), pl.BlockSpec((tm, GROUP), lambda i: (i, 0)),
            full((RET_HEADS, GROUP, GROUP)), full((RET_HEADS, GROUP, GROUP)), full((RET_HEADS, GROUP, GROUP)),
        ],
        out_specs=[
            tile,
            pl.BlockSpec((tm, inw), lambda i: (i, 0)),
            pl.BlockSpec((tm, rwid), lambda i: (i, 0)),
            pl.BlockSpec((nck, RET_HEADS, GROUP, GROUP), lambda i: (i, 0, 0, 0)),
        ],
        out_shape=[
            jax.ShapeDtypeStruct((s, d), F32),
            jax.ShapeDtypeStruct((s, inw), F32),
            jax.ShapeDtypeStruct((s, rwid), F32),
            jax.ShapeDtypeStruct((s // GROUP, RET_HEADS, GROUP, GROUP), F32),
        ],
        scratch_shapes=[
            pltpu.VMEM((inw, d), BF16), pltpu.VMEM((d, d), BF16),
            pltpu.VMEM((RET_HEADS, GROUP, GROUP), F32), pltpu.VMEM((MAX_WINDOW, pwid), F32),
            pltpu.VMEM((tm, d), BF16), pltpu.SemaphoreType.DMA((2 * NDEV,)),
        ],
        args=[h1, gain, weights[0], weights[1], pool_w, pool_scale, ret_gain,
              consts["cos"], consts["sin"], consts["decay"], consts["ktail"], consts["qhead"]],
    )


def _mix_bwd(dh2, h1, proj, o_saved, rsave, gain, weights, pool_w, pool_scale, ret_gain, consts, carries=()):
    s, d = h1.shape
    pwid = N_POOL_GROUPS * GROUP
    rwid = RET_HEADS * GROUP
    inw = pwid + 4 * rwid
    tm = min(256, s)
    nck = tm // GROUP
    nt = s // tm
    cd = consts["chunk_decay"]
    halo_per_tile = tm // MAX_WINDOW

    def body(dh2_ref, h_ref, proj_ref, halo_ref, o_ref, rs_ref, g_ref, bin_, bout, pw_ref, ps_ref, rg_ref,
             cos_ref, sin_ref, dec_ref, kt_ref, qh_ref,
             dh1_ref, dproj_ref, u_ref, m_ref, dpw_ref, dps_ref, drg_ref, dg_ref,
             wins, wouts, dstate, carry, dm, dpj, sems):
        i = pl.program_id(0)
        tile = nt - 1 - i

        @pl.when(i == 0)
        def _():
            for cp in _load_weights(((bin_, wins), (bout, wouts)), sems):
                cp.wait()
            dstate[...] = jnp.zeros_like(dstate)
            carry[...] = jnp.zeros_like(carry)
            for ref in (dpw_ref, dps_ref, drg_ref, dg_ref):
                ref[...] = jnp.zeros_like(ref)

        dh2v = dh2_ref[...]
        dm[...] = _dot(dh2v.astype(BF16), wouts[...], NT)
        hv = h_ref[...]
        g = g_ref[...]
        r = lax.rsqrt(jnp.mean(hv * hv, axis=-1, keepdims=True) + EPS)
        uh = hv * r
        u_ref[...] = (uh * g).astype(BF16)

        halo = jnp.where(tile == 0, 0.0, halo_ref[...])
        ext = jnp.concatenate([halo, proj_ref[:, 0:pwid]], axis=0)
        next_dpn = carry[...]
        for gi, w in enumerate(POOL_WINDOWS):
            cols = slice(gi * GROUP, (gi + 1) * GROUP)
            xg = ext[:, cols]
            cnt = _pool_counts(tile, tm, w)
            pooled = (_window_sums(xg, w, True)[MAX_WINDOW:, :] / cnt - xg[MAX_WINDOW:, :]).astype(BF16)
            pwb = pw_ref[gi].astype(BF16)
            z = _dot(pooled, pwb, NN)
            scale = ps_ref[:, cols]
            m_ref[:, cols] = (z * scale).astype(BF16)
            da = dm[:, cols]
            dps_ref[:, cols] += jnp.sum(da * z, axis=0, keepdims=True)
            dz = (da * scale).astype(BF16)
            dpw_ref[gi] += _dot(pooled, dz, TN)
            dpl = _dot(dz, pwb, NT)
            dpn = dpl / cnt
            ext2 = jnp.concatenate([dpn, next_dpn[:, cols]], axis=0)
            dpj[:, cols] = (_window_sums(ext2, w, False)[0:tm, :] - dpl).astype(BF16)
            carry[:, cols] = dpn[0:MAX_WINDOW, :]

        cos, sin = cos_ref[...], sin_ref[...]
        for h in range(RET_HEADS):
            cq = slice(pwid + h * GROUP, pwid + (h + 1) * GROUP)
            ck = slice(pwid + rwid + h * GROUP, pwid + rwid + (h + 1) * GROUP)
            cv = slice(pwid + 2 * rwid + h * GROUP, pwid + 2 * rwid + (h + 1) * GROUP)
            cg = slice(pwid + 3 * rwid + h * GROUP, pwid + 3 * rwid + (h + 1) * GROUP)
            ch = slice(h * GROUP, (h + 1) * GROUP)
            qr = _rope(proj_ref[:, cq], cos, sin)
            kr = _rope(proj_ref[:, ck], cos, sin) * (GROUP ** -0.5)
            vb = proj_ref[:, cv].astype(BF16)
            gv = proj_ref[:, cg]
            ov = o_ref[:, ch]
            ro = lax.rsqrt(jnp.mean(ov * ov, axis=-1, keepdims=True) + EPS)
            on = ov * ro
            rg = rg_ref[:, ch]
            db = dm[:, pwid + h * GROUP:pwid + (h + 1) * GROUP]
            sg = _sigmoid(gv)
            sl = gv * sg
            m_ref[:, pwid + h * GROUP:pwid + (h + 1) * GROUP] = (sl * (on * rg)).astype(BF16)
            dpj[:, cg] = (db * (on * rg) * (sg * (1.0 + gv * (1.0 - sg)))).astype(BF16)
            drg_ref[:, ch] += jnp.sum(db * sl * on, axis=0, keepdims=True)
            don = db * sl * rg
            do = (ro * (don - on * jnp.mean(don * on, axis=-1, keepdims=True))).astype(BF16)
            for n in reversed(range(nck)):
                rows = slice(n * GROUP, (n + 1) * GROUP)
                qc, kc, vc, dob = qr[rows], kr[rows], vb[rows], do[rows]
                qcb, kcb = qc.astype(BF16), kc.astype(BF16)
                qh = (qc * qh_ref[h]).astype(BF16)
                kt = (kc * kt_ref[h]).astype(BF16)
                rn = rs_ref[n, h].astype(BF16)
                dnext = dstate[h]
                dnb = dnext.astype(BF16)
                dec = dec_ref[h]
                p = (_dot(qcb, kcb, NT) * dec).astype(BF16)
                ds = (_dot(dob, vc, NT) * dec).astype(BF16)
                dv = _dot(p, dob, TN) + _dot(kt, dnb, NN)
                dq = _dot(ds, kcb, NN) + _dot(dob, rn, NT) * qh_ref[h]
                dk = _dot(ds, qcb, TN) + _dot(vc, dnb, NT) * kt_ref[h]
                dstate[h] = cd[h] * dnext + _dot(qh, dob, TN)
                dpj[rows, cq] = _rope_bwd(dq, cos[rows], sin[rows]).astype(BF16)
                dpj[rows, ck] = _rope_bwd(dk * (GROUP ** -0.5), cos[rows], sin[rows]).astype(BF16)
                dpj[rows, cv] = dv.astype(BF16)

        dproj_ref[...] = dpj[...]
        du = _dot(dpj[...], wins[...], NN)
        dg_ref[...] += jnp.sum(du * uh, axis=0, keepdims=True)
        dn = du * g
        dh1_ref[...] = dh2v + r * (dn - uh * jnp.mean(dn * uh, axis=-1, keepdims=True))

    rev = lambda i: (nt - 1 - i, 0)
    tile = pl.BlockSpec((tm, d), rev)
    full = lambda shape: pl.BlockSpec(shape, lambda i: (0,) * len(shape))
    return _grid_call(
        body,
        carries,
        name="mix_bwd",
        work_first=True,
        grid=(nt,),
        in_specs=[
            tile, tile,
            pl.BlockSpec((tm, inw), rev),
            pl.BlockSpec((MAX_WINDOW, pwid), lambda i: (jnp.maximum((nt - 1 - i) * halo_per_tile - 1, 0), 0)),
            pl.BlockSpec((tm, rwid), rev),
            pl.BlockSpec((nck, RET_HEADS, GROUP, GROUP), lambda i: (nt - 1 - i, 0, 0, 0)),
            full((1, d)), ANY, ANY,
            full((N_POOL_GROUPS, GROUP, GROUP)), full((1, pwid)), full((1, rwid)),
            pl.BlockSpec((tm, GROUP), rev), pl.BlockSpec((tm, GROUP), rev),
            full((RET_HEADS, GROUP, GROUP)), full((RET_HEADS, GROUP, GROUP)), full((RET_HEADS, GROUP, GROUP)),
        ],
        out_specs=[
            tile, pl.BlockSpec((tm, inw), rev), tile, tile,
            full((N_POOL_GROUPS, GROUP, GROUP)), full((1, pwid)), full((1, rwid)), full((1, d)),
        ],
        out_shape=[
            jax.ShapeDtypeStruct((s, d), F32),
            jax.ShapeDtypeStruct((s, inw), BF16),
            jax.ShapeDtypeStruct((s, d), BF16),
            jax.ShapeDtypeStruct((s, d), BF16),
            jax.ShapeDtypeStruct((N_POOL_GROUPS, GROUP, GROUP), F32),
            jax.ShapeDtypeStruct((1, pwid), F32),
            jax.ShapeDtypeStruct((1, rwid), F32),
            jax.ShapeDtypeStruct((1, d), F32),
        ],
        scratch_shapes=[
            pltpu.VMEM((inw, d), BF16), pltpu.VMEM((d, d), BF16),
            pltpu.VMEM((RET_HEADS, GROUP, GROUP), F32), pltpu.VMEM((MAX_WINDOW, pwid), F32),
            pltpu.VMEM((tm, d), F32), pltpu.VMEM((tm, inw), BF16), pltpu.SemaphoreType.DMA((2 * NDEV,)),
        ],
        args=[dh2, h1, proj, proj, o_saved, rsave, gain, weights[0], weights[1], pool_w, pool_scale, ret_gain,
              consts["cos"], consts["sin"], consts["decay"], consts["ktail"], consts["qhead"]],
    )


def _adam(w, g, m, v):
    m = ADAM_B1 * m + (1.0 - ADAM_B1) * g
    v = ADAM_B2 * v + (1.0 - ADAM_B2) * jnp.square(g)
    m_hat = m / (1.0 - ADAM_B1 ** ADAM_STEP)
    v_hat = v / (1.0 - ADAM_B2 ** ADAM_STEP)
    delta = -ADAM_LR * (m_hat / (jnp.sqrt(v_hat) + ADAM_EPS) + ADAM_WD * w)
    return delta, m, v


def _adamw_big(my_chip, w, sums, parts, m, v, name):
    rows, d = w.shape
    tr = _row_tile(rows, 176)

    def body(chip_ref, w_ref, own_ref, p_ref, m_ref, v_ref, g_ref, d_ref, nm_ref, nv_ref):
        g = own_ref[0].astype(F32)
        for q in range(NCHIP - 1):
            g = g + p_ref[q].astype(F32)
        g_ref[...] = g
        d_ref[...], nm_ref[...], nv_ref[...] = _adam(w_ref[...], g, m_ref[...], v_ref[...])

    spec = pl.BlockSpec((tr, d), lambda i, chip: (i, 0))
    return _call(
        body,
        name=name,
        grid_spec=pltpu.PrefetchScalarGridSpec(
            num_scalar_prefetch=1,
            grid=(rows // tr,),
            in_specs=[
                spec,
                pl.BlockSpec((1, tr, d), lambda i, chip: (chip[0], i, 0)),
                pl.BlockSpec((NCHIP - 1, tr, d), lambda i, chip: (0, i, 0)),
                spec,
                spec,
            ],
            out_specs=[spec] * 4,
        ),
        out_shape=[jax.ShapeDtypeStruct((rows, d), F32)] * 4,
        compiler_params=_seq(1),
    )(my_chip, w, sums, parts, m, v)


def _adamw_small(stats_all, pw_all, ws, ms, vs, pwid):
    nsmall = len(ws)

    def body(*refs):
        st_ref, pwa_ref = refs[0], refs[1]
        w_refs = refs[2:2 + nsmall]
        m_refs = refs[2 + nsmall:2 + 2 * nsmall]
        v_refs = refs[2 + 2 * nsmall:2 + 3 * nsmall]
        outs = refs[2 + 3 * nsmall:]
        st = st_ref[0]
        pwg = pwa_ref[0]
        for q in range(1, NDEV):
            st = st + st_ref[q]
            pwg = pwg + pwa_ref[q]
        grads = [st[0:1, :], st[1:2, :], st[2:3, :], st[3:4, :], st[4:5, 0:pwid], st[4:5, pwid:2 * pwid], pwg]
        outs[0][...] = jnp.zeros((1, GROUP), F32) + jnp.sum(st[5:6, :])
        for j in range(nsmall):
            delta, nm, nv = _adam(w_refs[j][...], grads[j], m_refs[j][...], v_refs[j][...])
            outs[1 + 4 * j][...] = grads[j]
            outs[2 + 4 * j][...] = delta
            outs[3 + 4 * j][...] = nm
            outs[4 + 4 * j][...] = nv

    out_shape = [jax.ShapeDtypeStruct((1, GROUP), F32)]
    for w in ws:
        out_shape += [jax.ShapeDtypeStruct(w.shape, F32)] * 4
    return _call(body, name="adamw_small", out_shape=out_shape, compiler_params=_params())(
        stats_all, pw_all, *ws, *ms, *vs
    )


def kernel(x, ffn1_norm, ffn1_w1, ffn1_w3, ffn1_w2, mix_norm, w_in, pool_w, pool_scale, ret_norm, w_out, ffn2_norm, ffn2_w1, ffn2_w3, ffn2_w2, final_norm, loss_target, m_ffn1_norm, m_ffn1_w1, m_ffn1_w3, m_ffn1_w2, m_mix_norm, m_w_in, m_pool_w, m_pool_scale, m_ret_norm, m_w_out, m_ffn2_norm, m_ffn2_w1, m_ffn2_w3, m_ffn2_w2, m_final_norm, v_ffn1_norm, v_ffn1_w1, v_ffn1_w3, v_ffn1_w2, v_mix_norm, v_w_in, v_pool_w, v_pool_scale, v_ret_norm, v_w_out, v_ffn2_norm, v_ffn2_w1, v_ffn2_w3, v_ffn2_w2, v_final_norm):
    s, d = x.shape[1], x.shape[2]
    ffn = ffn1_w1.shape[2] * NDEV
    pwid = pool_scale.shape[1]
    xs, tgt = x[0], loss_target[0]
    consts = _mix_constants(s)
    pw3 = pool_w[0]
    fnorm = final_norm.reshape(1, d)

    rows_of = lambda w, transposed: w[0].T if transposed else w[0]
    send_f1 = [rows_of(w, t).astype(BF16) for w, t in ((ffn1_w1, True), (ffn1_w3, True), (ffn1_w2, False))]
    later = [rows_of(w, t) for w, t in ((w_in, True), (w_out, False), (ffn2_w1, True), (ffn2_w3, True), (ffn2_w2, False))]

    sent_later, w13_f1 = _comm_call([_CastRows(later), _Gather(send_f1[:2])], "gather_ffn1")
    send_mix, send_f2 = sent_later[:2], sent_later[2:]
    (a1, b1, hm1), ((w2_f1, w_in_all),) = _ffn_up(xs, ffn1_norm, *w13_f1, carries=[_Gather(send_f1[2:] + send_mix[:1])])
    w_f1 = w13_f1 + [w2_f1]
    (h1,), ((w_out_all, w1_f2),) = _ffn_down(xs, hm1, w2_f1, carries=[_Gather(send_mix[1:] + send_f2[:1])])
    w_mix = [w_in_all, w_out_all]
    (h2, proj, o_saved, rsave), (rest,) = _mix_fwd(
        h1, mix_norm, w_mix, pw3, pool_scale, ret_norm, consts, carries=[_Gather(send_f2[1:])]
    )
    w_f2 = [w1_f2] + rest
    dh3, a2, b2, hm2, dgf, loss_cols = _ffn_fwd_loss(h2, ffn2_norm, w_f2, fnorm, tgt)

    (dh2, da2, db2, n2, dg2), _ = _ffn_bwd(dh3, h2, a2, b2, ffn2_norm, w_f2, ffn, "ffn2_bwd")
    sum_f2w1, _ = _wgrad(da2, n2, 1.0, "ffn2_w1_grad")
    sum_f2w3, _ = _wgrad(db2, n2, 1.0, "ffn2_w3_grad")
    sum_f2w2, ((parts_f2w1,),) = _wgrad(hm2, dh3, 0.5, "ffn2_w2_grad", carries=[_ChipScatter([sum_f2w1])])

    (dh1, dproj, u, mm, dpw, dps, drg, dgm), ((parts_f2w3, parts_f2w2),) = _mix_bwd(
        dh2, h1, proj, o_saved, rsave, mix_norm, w_mix, pw3, pool_scale, ret_norm, consts,
        carries=[_ChipScatter([sum_f2w3, sum_f2w2])],
    )
    (dx, da1, db1, n1, dg1), _ = _ffn_bwd(dh1, xs, a1, b1, ffn1_norm, w_f1, ffn, "ffn1_bwd")
    stats = jnp.concatenate(
        [dg1, dgm, dg2, dgf, jnp.concatenate([dps, drg], axis=1), loss_cols, jnp.zeros((2, d), F32)], axis=0
    )
    small = _GatherDirect([stats, dpw.reshape(N_POOL_GROUPS * GROUP, GROUP)])
    sum_f1w2, ((stats_all, pw_all),) = _wgrad(hm1, dh1, 0.5, "ffn1_w2_grad", carries=[small])
    sum_f1w1, ((parts_f1w2,),) = _wgrad(da1, n1, 1.0, "ffn1_w1_grad", carries=[_ChipScatter([sum_f1w2])])
    sum_f1w3, ((parts_f1w1,),) = _wgrad(db1, n1, 1.0, "ffn1_w3_grad", carries=[_ChipScatter([sum_f1w1])])
    sum_in, ((parts_f1w3,),) = _wgrad(dproj, u, 1.0, "w_in_grad", carries=[_ChipScatter([sum_f1w3])])
    sum_out, ((parts_in,),) = _wgrad(mm, dh2, 1.0, "w_out_grad", carries=[_ChipScatter([sum_in])])
    ((parts_out,),) = _comm_call([_ChipScatter([sum_out])], "scatter_last")

    big = (
        (ffn1_w1, m_ffn1_w1, v_ffn1_w1, sum_f1w1, parts_f1w1, True),
        (ffn1_w3, m_ffn1_w3, v_ffn1_w3, sum_f1w3, parts_f1w3, True),
        (ffn1_w2, m_ffn1_w2, v_ffn1_w2, sum_f1w2, parts_f1w2, False),
        (w_in, m_w_in, v_w_in, sum_in, parts_in, True),
        (w_out, m_w_out, v_w_out, sum_out, parts_out, False),
        (ffn2_w1, m_ffn2_w1, v_ffn2_w1, sum_f2w1, parts_f2w1, True),
        (ffn2_w3, m_ffn2_w3, v_ffn2_w3, sum_f2w3, parts_f2w3, True),
        (ffn2_w2, m_ffn2_w2, v_ffn2_w2, sum_f2w2, parts_f2w2, False),
    )
    my_chip = jnp.reshape(_chip(_peer(0)), (1,)).astype(jnp.int32)
    big_out = []
    for j, (w, m, v, sums, parts, t) in enumerate(big):
        view = (lambda a: a[0].T) if t else (lambda a: a[0])
        back = (lambda a: a.T[None]) if t else (lambda a: a[None])
        big_out.append([back(a) for a in _adamw_big(my_chip, view(w), sums, parts, view(m), view(v), "adamw_%d" % j)])

    small_w = (ffn1_norm, mix_norm, ffn2_norm, fnorm, pool_scale, ret_norm, pw3.reshape(-1, GROUP))
    small_m = (m_ffn1_norm, m_mix_norm, m_ffn2_norm, m_final_norm.reshape(1, d), m_pool_scale, m_ret_norm, m_pool_w.reshape(-1, GROUP))
    small_v = (v_ffn1_norm, v_mix_norm, v_ffn2_norm, v_final_norm.reshape(1, d), v_pool_scale, v_ret_norm, v_pool_w.reshape(-1, GROUP))
    res = _adamw_small(stats_all, pw_all, small_w, small_m, small_v, pwid)
    loss = res[0][0, 0]
    small_out = [list(res[1 + 4 * j:5 + 4 * j]) for j in range(len(small_w))]
    small_out[3] = [a.reshape(d) for a in small_out[3]]
    small_out[6] = [a.reshape(pool_w.shape) for a in small_out[6]]

    order = [small_out[0], big_out[0], big_out[1], big_out[2], small_out[1], big_out[3], small_out[6], small_out[4],
             small_out[5], big_out[4], small_out[2], big_out[5], big_out[6], big_out[7], small_out[3]]
    result = [loss, dx[None]]
    for kind in range(4):
        result += [t[kind] for t in order]
    return tuple(result)
```

```python
import functools

import numpy as np
import jax
import jax.numpy as jnp
from jax import lax
from jax.experimental import pallas as pl
from jax.experimental.pallas import tpu as pltpu

F32 = jnp.float32
BF16 = jnp.bfloat16

NDEV = 8
NCHIP = 4
EPS = 1e-6
N_POOL_GROUPS = 4
POOL_WINDOWS = (2, 4, 8, 16)
MAX_WINDOW = 16
GROUP = 128
RET_HEADS = 4
ROPE_BASE = 10000.0
ADAM_LR = 0.001
ADAM_B1 = 0.9
ADAM_B2 = 0.999
ADAM_EPS = 1e-08
ADAM_WD = 0.01
ADAM_STEP = 10

VMEM_LIMIT = 56 * 1024 * 1024
FFN_CHUNK = 256
ROW_BAND = 32

NT = (((1,), (1,)), ((), ()))
NN = (((1,), (0,)), ((), ()))
TN = (((0,), (0,)), ((), ()))

ANY = pl.BlockSpec(memory_space=pl.ANY)


def _dot(a, b, dims):
    return lax.dot_general(a, b, dims, preferred_element_type=F32)


def _call(body, **kw):
    return pl.pallas_call(body, **kw)


def _params(**kw):
    return pltpu.CompilerParams(vmem_limit_bytes=VMEM_LIMIT, **kw)


def _seq(n):
    return _params(dimension_semantics=("arbitrary",) * n)


def _peer(k):
    x, y, c = lax.axis_index("x"), lax.axis_index("y"), lax.axis_index("c")
    return (1 - x if k & 4 else x, 1 - y if k & 2 else y, 1 - c if k & 1 else c)


def _flat(pos):
    return 4 * pos[0] + 2 * pos[1] + pos[2]


def _chip(pos):
    return 2 * pos[0] + pos[1]


def _row_tile(rows, cap):
    return max(t for t in range(16, min(rows, cap) + 1, 16) if rows % t == 0)


def _pieces(rows, n):
    tiles = rows // 16
    cuts = [16 * (tiles * q // n) for q in range(n + 1)]
    return [(a, b - a) for a, b in zip(cuts[:-1], cuts[1:])]


def _load_weights(parts, sems):
    copies = []
    for buf, dst in parts:
        rows = buf.shape[1]
        for p in range(NDEV):
            cp = pltpu.make_async_copy(buf.at[p], dst.at[pl.ds(p * rows, rows), :], sems.at[len(copies)])
            cp.start()
            copies.append(cp)
    return copies


def _sigmoid(a):
    return 1.0 / (1.0 + jnp.exp(-a))


def _remote(src, dst, send_sem, recv_sem, to):
    return pltpu.make_async_remote_copy(
        src_ref=src, dst_ref=dst, send_sem=send_sem, recv_sem=recv_sem, device_id=to, device_id_type=pl.DeviceIdType.MESH
    )


class _Gather:
    X, Y, FAR = 4, 2, 6
    peers = (1, 2, 4)
    COPIES = 8

    def __init__(self, shards):
        n = len(shards)
        self.operands = list(shards)
        self.out_shape = [jax.ShapeDtypeStruct((NDEV,) + a.shape, a.dtype) for a in shards]
        self.sems = [
            pltpu.SemaphoreType.DMA((self.COPIES * n,)), pltpu.SemaphoreType.DMA((self.COPIES * n,)),
            pltpu.SemaphoreType.DMA((n,)),
        ]
        self.stages = [self.begin, self.relay, self.relay_far, self.end]

    def _copy(self, t, k, block, to, ins, outs, sems, own=False, half=None):
        rows = outs[t].shape[1]
        part = pl.ds(0, rows) if half is None else pl.ds(half * (rows // 2), rows // 2)
        dst = outs[t].at[_flat(block), part, :]
        at = self.COPIES * t + k
        return _remote(ins[t] if own else dst, dst, sems[0].at[at], sems[1].at[at], to)

    def _local(self, t, ins, outs, sems):
        return pltpu.make_async_copy(ins[t], outs[t].at[_flat(_peer(0))], sems[2].at[t])

    def begin(self, ins, outs, sems):
        me = _peer(0)
        for t in range(len(ins)):
            self._local(t, ins, outs, sems).start()
            for k, code in enumerate((1, self.X, self.Y)):
                self._copy(t, k, me, _peer(code), ins, outs, sems, own=True).start()

    def relay(self, ins, outs, sems):
        me, sibling = _peer(0), _peer(1)
        for t in range(len(ins)):
            self._copy(t, 1, _peer(self.X), me, ins, outs, sems).wait_recv()
            self._copy(t, 3, _peer(self.X), _peer(self.Y), ins, outs, sems, half=0).start()
            self._copy(t, 5, _peer(self.X), sibling, ins, outs, sems).start()
            self._copy(t, 2, _peer(self.Y), me, ins, outs, sems).wait_recv()
            self._copy(t, 4, _peer(self.Y), _peer(self.X), ins, outs, sems, half=1).start()
            self._copy(t, 6, _peer(self.Y), sibling, ins, outs, sems).start()

    def relay_far(self, ins, outs, sems):
        me, sibling = _peer(0), _peer(1)
        for t in range(len(ins)):
            self._copy(t, 3, _peer(self.FAR), me, ins, outs, sems, half=0).wait_recv()
            self._copy(t, 4, _peer(self.FAR), me, ins, outs, sems, half=1).wait_recv()
            self._copy(t, 7, _peer(self.FAR), sibling, ins, outs, sems).start()

    def end(self, ins, outs, sems):
        me = _peer(0)
        for t in range(len(ins)):
            self._copy(t, 0, _peer(1), me, ins, outs, sems).wait_recv()
            for k, code in ((5, self.X), (6, self.Y), (7, self.FAR)):
                self._copy(t, k, _peer(code ^ 1), me, ins, outs, sems).wait_recv()
            for k in range(self.COPIES):
                self._copy(t, k, me, me, ins, outs, sems, half=0 if k == 3 else 1 if k == 4 else None).wait_send()
            self._local(t, ins, outs, sems).wait()


class _GatherDirect:
    peers = tuple(range(1, NDEV))

    def __init__(self, arrays):
        n = len(arrays)
        self.operands = list(arrays)
        self.out_shape = [jax.ShapeDtypeStruct((NDEV,) + a.shape, a.dtype) for a in arrays]
        self.sems = [pltpu.SemaphoreType.DMA((7 * n,)), pltpu.SemaphoreType.DMA((7 * n,)), pltpu.SemaphoreType.DMA((n,))]
        self.stages = [self.begin, self.end]

    def begin(self, ins, outs, sems):
        mine = _flat(_peer(0))
        for t in range(len(ins)):
            pltpu.make_async_copy(ins[t], outs[t].at[mine], sems[2].at[t]).start()
            for k in range(1, NDEV):
                _remote(ins[t], outs[t].at[mine], sems[0].at[7 * t + k - 1], sems[1].at[7 * t + k - 1], _peer(k)).start()

    def end(self, ins, outs, sems):
        mine = _flat(_peer(0))
        for t in range(len(ins)):
            for k in range(1, NDEV):
                cp = _remote(ins[t], outs[t].at[_flat(_peer(k))], sems[0].at[7 * t + k - 1], sems[1].at[7 * t + k - 1], _peer(k))
                cp.wait_recv()
                cp.wait_send()
            pltpu.make_async_copy(ins[t], outs[t].at[mine], sems[2].at[t]).wait()


class _ChipScatter:
    peers = (2, 4, 6)
    across = (4, 2, 6)

    def __init__(self, sums):
        n = len(sums) * len(self.across)
        self.operands = list(sums)
        self.out_shape = [jax.ShapeDtypeStruct((len(self.across),) + a.shape[1:], a.dtype) for a in sums]
        self.sems = [pltpu.SemaphoreType.DMA((n,)), pltpu.SemaphoreType.DMA((n,))]
        self.stages = [self.begin, self.end]

    def _copies(self, ins, outs, sems):
        copies = []
        for t in range(len(ins)):
            for slot, k in enumerate(self.across):
                at = len(copies)
                copies.append(
                    _remote(ins[t].at[_chip(_peer(k))], outs[t].at[slot], sems[0].at[at], sems[1].at[at], _peer(k))
                )
        return copies

    def begin(self, ins, outs, sems):
        for cp in self._copies(ins, outs, sems):
            cp.start()

    def end(self, ins, outs, sems):
        for cp in self._copies(ins, outs, sems):
            cp.wait_recv()
            cp.wait_send()


class _CastRows:
    peers = ()

    def __init__(self, arrays):
        n = len(arrays)
        self.operands = list(arrays)
        self.out_shape = [jax.ShapeDtypeStruct(a.shape, BF16) for a in arrays]
        self.sems = [pltpu.SemaphoreType.DMA((n,)), pltpu.SemaphoreType.DMA((n,))]
        self.sems += [pltpu.VMEM(a.shape, F32) for a in arrays] + [pltpu.VMEM(a.shape, BF16) for a in arrays]
        self.stages = [self.begin, self.convert, self.end]

    def _moves(self, t, ins, outs, scratch):
        n = len(ins)
        load = pltpu.make_async_copy(ins[t], scratch[2 + t], scratch[0].at[t])
        store = pltpu.make_async_copy(scratch[2 + n + t], outs[t], scratch[1].at[t])
        return load, store

    def begin(self, ins, outs, scratch):
        for t in range(len(ins)):
            self._moves(t, ins, outs, scratch)[0].start()

    def convert(self, ins, outs, scratch):
        n = len(ins)
        for t in range(n):
            load, store = self._moves(t, ins, outs, scratch)
            load.wait()
            scratch[2 + n + t][...] = scratch[2 + t][...].astype(BF16)
            store.start()

    def end(self, ins, outs, scratch):
        for t in range(len(ins)):
            self._moves(t, ins, outs, scratch)[1].wait()


def _split_refs(refs, counts):
    out, at = [], 0
    for n in counts:
        out.append(refs[at:at + n])
        at += n
    return out


BARRIER_IDS = {(2, 4, 6): 0, (1, 2, 4): 1, (1,): 2, (1, 2, 4, 6): 3, tuple(range(1, NDEV)): 4}


def _peers_of(carries, own=()):
    peers = tuple(sorted(set(own).union(*[c.peers for c in carries])))
    return (peers, BARRIER_IDS[peers]) if peers in BARRIER_IDS else (None, None)


def _announce(peers):
    barrier = pltpu.get_barrier_semaphore()
    for k in peers:
        pl.semaphore_signal(barrier, inc=1, device_id=_peer(k), device_id_type=pl.DeviceIdType.MESH)


def _await(peers):
    pl.semaphore_wait(pltpu.get_barrier_semaphore(), len(peers))


def _handshake(peers):
    _announce(peers)
    _await(peers)


def _comm_call(carries, name):
    nin = [len(c.operands) for c in carries]
    nout = [len(c.out_shape) for c in carries]
    nsem = [len(c.sems) for c in carries]
    peers, collective_id = _peers_of(carries)

    def body(*refs):
        if peers:
            _handshake(peers)
        ins, outs, sems = _split_refs(refs, (sum(nin), sum(nout), sum(nsem)))
        parts = list(zip(carries, _split_refs(ins, nin), _split_refs(outs, nout), _split_refs(sems, nsem)))
        for depth in range(max(len(c.stages) for c in carries)):
            for c, i, o, s in parts:
                if depth < len(c.stages) - 1:
                    c.stages[depth](i, o, s)
        for c, i, o, s in parts:
            c.stages[-1](i, o, s)

    res = _call(
        body,
        name=name,
        out_shape=[sh for c in carries for sh in c.out_shape],
        in_specs=[ANY] * sum(nin),
        out_specs=[ANY] * sum(nout),
        scratch_shapes=[sm for c in carries for sm in c.sems],
        compiler_params=_params(has_side_effects=True, collective_id=collective_id),
    )(*[a for c in carries for a in c.operands])
    return _split_refs(list(res), nout)


def _grid_call(body, carries, *, name, grid, in_specs, out_specs, out_shape, scratch_shapes, args, own_peers=(),
               work_first=False):
    ni, no, ns = len(in_specs), len(out_specs), len(scratch_shapes)
    nin = [len(c.operands) for c in carries]
    nout = [len(c.out_shape) for c in carries]
    nsem = [len(c.sems) for c in carries]
    steps = int(np.prod(grid))
    peers, collective_id = _peers_of(carries, own_peers)

    def when_of(stage, count):
        first, last = (5 * steps) // 8 - 1, steps - 2
        return max(0, last if count <= 3 else first + (last - first) * (stage - 1) // (count - 3))

    def wrapped(*refs):
        ins, cins, outs, couts, scr, csems = _split_refs(refs, (ni, sum(nin), no, sum(nout), ns, sum(nsem)))
        if not carries and not peers:
            return body(*ins, *outs, *scr)
        parts = list(zip(carries, _split_refs(cins, nin), _split_refs(couts, nout), _split_refs(csems, nsem)))
        step = pl.program_id(0)
        for axis in range(1, len(grid)):
            step = step * grid[axis] + pl.program_id(axis)

        def first_stage():
            if peers:
                _await(peers)
            for c, i, o, s in parts:
                c.stages[0](i, o, s)

        if peers:
            pl.when(step == 0)(functools.partial(_announce, peers))
        if not work_first:
            pl.when(step == 0)(first_stage)

        body(*ins, *outs, *scr)

        if work_first:
            pl.when(step == 0)(first_stage)

        for c, i, o, s in parts:
            for stage in range(1, len(c.stages) - 1):
                pl.when(step == when_of(stage, len(c.stages)))(functools.partial(c.stages[stage], i, o, s))

        @pl.when(step == steps - 1)
        def _():
            for c, i, o, s in parts:
                c.stages[-1](i, o, s)

    res = _call(
        wrapped,
        name=name,
        grid=tuple(grid),
        in_specs=list(in_specs) + [ANY] * sum(nin),
        out_specs=list(out_specs) + [ANY] * sum(nout),
        out_shape=list(out_shape) + [sh for c in carries for sh in c.out_shape],
        scratch_shapes=list(scratch_shapes) + [sm for c in carries for sm in c.sems],
        compiler_params=_params(dimension_semantics=("arbitrary",) * len(grid), collective_id=collective_id),
    )(*args, *[a for c in carries for a in c.operands])
    res = list(res)
    return res[:no], _split_refs(res[no:], nout)


def _chunks(width):
    return [(at, min(FFN_CHUNK, width - at)) for at in range(0, width, FFN_CHUNK)]


def _ffn_fwd_loss(x, gain, weights, final_gain, target):
    s, d = x.shape
    ffn = weights[0].shape[0] * weights[0].shape[1]
    tm = min(512, s)

    def body(x_ref, g_ref, b1, b3, b2, gf_ref, t_ref, dh_ref, a_ref, b_ref, hm_ref, dgf_ref, loss_ref, w1s, w3s, w2s, sems):
        @pl.when(pl.program_id(0) == 0)
        def _():
            for cp in _load_weights(((b1, w1s), (b3, w3s), (b2, w2s)), sems):
                cp.wait()
            dgf_ref[...] = jnp.zeros_like(dgf_ref)
            loss_ref[...] = jnp.zeros_like(loss_ref)

        xv = x_ref[...]
        r = lax.rsqrt(jnp.mean(xv * xv, axis=-1, keepdims=True) + EPS)
        n = (xv * r * g_ref[...]).astype(BF16)
        acc = jnp.zeros((tm, d), F32)
        for at, width in _chunks(ffn):
            cols = slice(at, at + width)
            a = _dot(n, w1s[cols, :], NT)
            b = _dot(n, w3s[cols, :], NT)
            a_ref[:, cols] = a.astype(BF16)
            b_ref[:, cols] = b.astype(BF16)
            hm = (a * _sigmoid(a) * b).astype(BF16)
            hm_ref[:, cols] = hm
            acc = acc + _dot(hm, w2s[cols, :], NN)
        h = xv + 0.5 * acc
        rf = lax.rsqrt(jnp.mean(h * h, axis=-1, keepdims=True) + EPS)
        nh = h * rf
        gf = gf_ref[...]
        err = nh * gf - t_ref[...]
        loss_ref[...] += jnp.sum(err * err, axis=0, keepdims=True) * (0.5 / d)
        dy = err * (1.0 / d)
        dgf_ref[...] += jnp.sum(dy * nh, axis=0, keepdims=True)
        dn = dy * gf
        dh_ref[...] = rf * (dn - nh * jnp.mean(dn * nh, axis=-1, keepdims=True))

    tile = pl.BlockSpec((tm, d), lambda i: (i, 0))
    row = pl.BlockSpec((1, d), lambda i: (0, 0))
    wide = pl.BlockSpec((tm, ffn), lambda i: (i, 0))
    return _call(
        body,
        name="ffn_fwd_loss",
        grid=(s // tm,),
        in_specs=[tile, row, ANY, ANY, ANY, row, tile],
        out_specs=[tile, wide, wide, wide, row, row],
        out_shape=[jax.ShapeDtypeStruct((s, d), F32)] + [jax.ShapeDtypeStruct((s, ffn), BF16)] * 3
        + [jax.ShapeDtypeStruct((1, d), F32)] * 2,
        scratch_shapes=[pltpu.VMEM((ffn, d), BF16)] * 3 + [pltpu.SemaphoreType.DMA((3 * NDEV,))],
        compiler_params=_seq(1),
    )(x, gain, *weights, final_gain, target)


def _ffn_up(x, gain, w1, w3, carries=()):
    s, d = x.shape
    ffn = w1.shape[0] * w1.shape[1]
    tm = min(512, s)

    def body(x_ref, g_ref, b1, b3, a_ref, b_ref, hm_ref, w1s, w3s, sems):
        @pl.when(pl.program_id(0) == 0)
        def _():
            for cp in _load_weights(((b1, w1s), (b3, w3s)), sems):
                cp.wait()

        xv = x_ref[...]
        r = lax.rsqrt(jnp.mean(xv * xv, axis=-1, keepdims=True) + EPS)
        n = (xv * r * g_ref[...]).astype(BF16)
        for at, width in _chunks(ffn):
            cols = slice(at, at + width)
            a = _dot(n, w1s[cols, :], NT)
            b = _dot(n, w3s[cols, :], NT)
            a_ref[:, cols] = a.astype(BF16)
            b_ref[:, cols] = b.astype(BF16)
            hm_ref[:, cols] = (a * _sigmoid(a) * b).astype(BF16)

    wide = pl.BlockSpec((tm, ffn), lambda i: (i, 0))
    return _grid_call(
        body,
        carries,
        name="ffn_up",
        grid=(s // tm,),
        in_specs=[pl.BlockSpec((tm, d), lambda i: (i, 0)), pl.BlockSpec((1, d), lambda i: (0, 0)), ANY, ANY],
        out_specs=[wide] * 3,
        out_shape=[jax.ShapeDtypeStruct((s, ffn), BF16)] * 3,
        scratch_shapes=[pltpu.VMEM((ffn, d), BF16)] * 2 + [pltpu.SemaphoreType.DMA((2 * NDEV,))],
        args=[x, gain, w1, w3],
    )


def _ffn_down(x, hm, w2, carries=()):
    s, d = x.shape
    ffn = w2.shape[0] * w2.shape[1]
    tm = min(512, s)

    def body(x_ref, hm_ref, b2, h_ref, w2s, sems):
        @pl.when(pl.program_id(0) == 0)
        def _():
            for cp in _load_weights(((b2, w2s),), sems):
                cp.wait()

        acc = jnp.zeros((tm, d), F32)
        for at, width in _chunks(ffn):
            cols = slice(at, at + width)
            acc = acc + _dot(hm_ref[:, cols], w2s[cols, :], NN)
        h_ref[...] = x_ref[...] + 0.5 * acc

    tile = pl.BlockSpec((tm, d), lambda i: (i, 0))
    return _grid_call(
        body,
        carries,
        name="ffn_down",
        grid=(s // tm,),
        in_specs=[tile, pl.BlockSpec((tm, ffn), lambda i: (i, 0)), ANY],
        out_specs=[tile],
        out_shape=[jax.ShapeDtypeStruct((s, d), F32)],
        scratch_shapes=[pltpu.VMEM((ffn, d), BF16), pltpu.SemaphoreType.DMA((NDEV,))],
        args=[x, hm, w2],
    )


def _ffn_bwd(dh, x, a, b, gain, weights, ffn, name, carries=()):
    s, d = x.shape
    tm = min(512, s)
    halves = 2
    fh = ffn // halves

    def body(dh_ref, x_ref, a_ref, b_ref, g_ref, b1, b3, b2, dx_ref, da_ref, db_ref, n_ref, dg_ref, w1s, w3s, w2s, sems):
        i, j = pl.program_id(0), pl.program_id(1)

        @pl.when((i == 0) & (j == 0))
        def _():
            for cp in _load_weights(((b1, w1s), (b3, w3s), (b2, w2s)), sems):
                cp.wait()
            dg_ref[...] = jnp.zeros_like(dg_ref)

        @pl.when(j == 0)
        def _():
            dx_ref[...] = jnp.zeros_like(dx_ref)

        dob = (0.5 * dh_ref[...]).astype(BF16)
        chunks = _chunks(fh)

        def dhm_of(k):
            at, width = chunks[k]
            return _dot(dob, w2s[pl.ds(pl.multiple_of(j * fh + at, GROUP), width), :], NT)

        ahead = dhm_of(0)
        for k, (at, width) in enumerate(chunks):
            cols = slice(at, at + width)
            dhm = ahead
            if k + 1 < len(chunks):
                ahead = dhm_of(k + 1)
            for top in range(0, tm, ROW_BAND):
                band = slice(top, top + ROW_BAND)
                av = a_ref[band, cols].astype(F32)
                bv = b_ref[band, cols].astype(F32)
                sg = _sigmoid(av)
                dv = dhm[band]
                da_ref[band, cols] = (dv * bv * (sg * (1.0 + av * (1.0 - sg)))).astype(BF16)
                db_ref[band, cols] = (dv * (av * sg)).astype(BF16)
        half = pl.ds(pl.multiple_of(j * fh, GROUP), fh)
        dx_ref[...] += _dot(da_ref[...], w1s[half, :], NN) + _dot(db_ref[...], w3s[half, :], NN)

        @pl.when(j == halves - 1)
        def _():
            xv = x_ref[...]
            g = g_ref[...]
            r = lax.rsqrt(jnp.mean(xv * xv, axis=-1, keepdims=True) + EPS)
            nh = xv * r
            n_ref[...] = (nh * g).astype(BF16)
            total = dx_ref[...]
            dg_ref[...] += jnp.sum(total * nh, axis=0, keepdims=True)
            dnh = total * g
            dx_ref[...] = dh_ref[...] + r * (dnh - nh * jnp.mean(dnh * nh, axis=-1, keepdims=True))

    tile = pl.BlockSpec((tm, d), lambda i, j: (i, 0))
    row = pl.BlockSpec((1, d), lambda i, j: (0, 0))
    wide = pl.BlockSpec((tm, fh), lambda i, j: (i, j))
    return _grid_call(
        body,
        carries,
        name=name,
        grid=(s // tm, halves),
        in_specs=[tile, tile, wide, wide, row, ANY, ANY, ANY],
        out_specs=[tile, wide, wide, tile, row],
        out_shape=[
            jax.ShapeDtypeStruct((s, d), F32),
            jax.ShapeDtypeStruct((s, ffn), BF16),
            jax.ShapeDtypeStruct((s, ffn), BF16),
            jax.ShapeDtypeStruct((s, d), BF16),
            jax.ShapeDtypeStruct((1, d), F32),
        ],
        scratch_shapes=[pltpu.VMEM((ffn, d), BF16)] * 3 + [pltpu.SemaphoreType.DMA((3 * NDEV,))],
        args=[dh, x, a, b, gain] + list(weights),
    )


SWAP_PIECES = 1
RING = 3


def _wgrad(lhs, rhs, scale, name, carries=()):
    s, m = lhs.shape
    n = rhs.shape[1]
    rs = m // NDEV
    tk = min(1024, s)
    steps = s // tk
    pieces = [(j, at, size) for j in range(2) for at, size in _pieces(rs, SWAP_PIECES)]

    def body(l_hbm, r_hbm, o_ref, acc, mine, theirs, send_sems, recv_sems, l_buf, r_buf, load_sems):
        h, k = pl.program_id(0), pl.program_id(1)
        step = h * steps + k

        def loads(t):
            if isinstance(t, int):
                slot, at, half = t % RING, (t % steps) * tk, (t // steps) * (m // 2)
            else:
                slot = lax.rem(t, RING)
                at = pl.multiple_of(lax.rem(t, steps) * tk, tk)
                half = pl.multiple_of(lax.div(t, steps) * (m // 2), GROUP)
            return (
                pltpu.make_async_copy(l_hbm.at[pl.ds(at, tk), pl.ds(half, m // 2)], l_buf.at[slot], load_sems.at[0, slot]),
                pltpu.make_async_copy(r_hbm.at[pl.ds(at, tk), :], r_buf.at[slot], load_sems.at[1, slot]),
            )

        @pl.when(step == 0)
        def _():
            for t in range(min(RING - 1, 2 * steps)):
                for cp in loads(t):
                    cp.start()

        @pl.when(step + RING - 1 < 2 * steps)
        def _():
            for cp in loads(step + RING - 1):
                cp.start()

        for cp in loads(step):
            cp.wait()
        slot = lax.rem(step, RING)

        @pl.when(k == 0)
        def _():
            acc[...] = _dot(l_buf[slot], r_buf[slot].astype(BF16), TN)

        @pl.when(k > 0)
        def _():
            acc[...] += _dot(l_buf[slot], r_buf[slot].astype(BF16), TN)

        def exchange(half):
            c = lax.axis_index("c")
            return [
                _remote(mine.at[half, 1 - c, j, pl.ds(at, size), :], theirs.at[half, j, pl.ds(at, size), :],
                        send_sems.at[half * len(pieces) + q], recv_sems.at[half * len(pieces) + q], _peer(1))
                for q, (j, at, size) in enumerate(pieces)
            ]

        def settle(half):
            for cp in exchange(half):
                cp.wait_recv()
            both = mine[half, lax.axis_index("c")].astype(F32) + theirs[half].astype(F32)
            o_ref[2 * half:2 * half + 2] = both.astype(BF16)
            for cp in exchange(half):
                cp.wait_send()

        for half in range(2):
            @pl.when((h == half) & (k == steps - 1))
            def _():
                for p in range(NCHIP):
                    mine[half, p % 2, p // 2] = (acc[p * rs:(p + 1) * rs, :] * scale).astype(BF16)
                for cp in exchange(half):
                    cp.start()
                if half == 1:
                    settle(0)
                    settle(1)

    (out,), carried = _grid_call(
        body,
        carries,
        name=name,
        grid=(2, steps),
        in_specs=[ANY, ANY],
        out_specs=[pl.BlockSpec((NCHIP, rs, n), lambda h, k: (0, 0, 0))],
        out_shape=[jax.ShapeDtypeStruct((NCHIP, rs, n), BF16)],
        scratch_shapes=[
            pltpu.VMEM((m // 2, n), F32), pltpu.VMEM((2, 2, 2, rs, n), BF16), pltpu.VMEM((2, 2, rs, n), BF16),
            pltpu.SemaphoreType.DMA((2 * len(pieces),)), pltpu.SemaphoreType.DMA((2 * len(pieces),)),
            pltpu.VMEM((RING, tk, m // 2), lhs.dtype), pltpu.VMEM((RING, tk, n), rhs.dtype),
            pltpu.SemaphoreType.DMA((2, RING)),
        ],
        args=[lhs, rhs],
        own_peers=(1,),
    )
    return out, carried


def _mix_constants(s):
    c = GROUP
    lg = np.log1p(-np.exp2(-5.0 - np.arange(RET_HEADS, dtype=np.float32))).astype(np.float32)
    pos = np.arange(c, dtype=np.float32)
    rel = pos[:, None] - pos[None, :]
    decay = np.where(rel[None] >= 0, np.exp(lg[:, None, None] * np.maximum(rel, 0.0)[None]), 0.0).astype(np.float32)
    ktail = np.exp(lg[:, None] * (c - 1 - pos)[None, :]).astype(np.float32)
    qhead = np.exp(lg[:, None] * (pos + 1.0)[None, :]).astype(np.float32)
    chunk_decay = [float(v) for v in np.exp(lg * np.float32(c)).astype(np.float32)]
    ones = np.ones((1, 1, c), np.float32)
    inv_freq = (1.0 / (np.float32(ROPE_BASE) ** (np.arange(0, c, 2, dtype=np.float32) / np.float32(c)))).astype(np.float32)
    ang = (np.arange(s, dtype=np.float32)[:, None] * inv_freq[None, :]).astype(np.float32)
    cos, sin = np.cos(ang).astype(np.float32), np.sin(ang).astype(np.float32)
    return dict(
        decay=jnp.asarray(decay),
        ktail=jnp.asarray(ktail[:, :, None] * ones),
        qhead=jnp.asarray(qhead[:, :, None] * ones),
        chunk_decay=chunk_decay,
        cos=jnp.asarray(np.concatenate([cos, cos], axis=-1)),
        sin=jnp.asarray(np.concatenate([-sin, sin], axis=-1)),
    )


def _rope(t, cos, sin):
    return t * cos + pltpu.roll(t, GROUP // 2, axis=1) * sin


def _rope_bwd(dt, cos, sin):
    return dt * cos + pltpu.roll(dt * sin, GROUP // 2, axis=1)


def _window_sums(ext, w, forward):
    rows = ext.shape[0]
    acc, k = ext, 1
    while k < w:
        acc = acc + pltpu.roll(acc, k if forward else rows - k, axis=0)
        k *= 2
    return acc


def _pool_counts(tile, tm, w):
    t = lax.broadcasted_iota(jnp.int32, (tm, 1), 0) + tile * tm
    return jnp.minimum(t + 1, w).astype(F32)


def _mix_fwd(h1, gain, weights, pool_w, pool_scale, ret_gain, consts, carries=()):
    s, d = h1.shape
    pwid = N_POOL_GROUPS * GROUP
    rwid = RET_HEADS * GROUP
    inw = pwid + 4 * rwid
    tm = min(256, s)
    nck = tm // GROUP
    cd = consts["chunk_decay"]

    def body(h_ref, g_ref, bin_, bout, pw_ref, ps_ref, rg_ref, cos_ref, sin_ref, dec_ref, kt_ref, qh_ref,
             h2_ref, proj_ref, o_ref, rs_ref, wins, wouts, state, carry, mbuf, sems):
        i = pl.program_id(0)

        @pl.when(i == 0)
        def _():
            for cp in _load_weights(((bin_, wins), (bout, wouts)), sems):
                cp.wait()
            state[...] = jnp.zeros_like(state)
            carry[...] = jnp.zeros_like(carry)

        hv = h_ref[...]
        r = lax.rsqrt(jnp.mean(hv * hv, axis=-1, keepdims=True) + EPS)
        u = (hv * r * g_ref[...]).astype(BF16)
        proj_ref[...] = _dot(u, wins[...], NT)

        ext = jnp.concatenate([carry[...], proj_ref[:, 0:pwid]], axis=0)
        carry[...] = proj_ref[tm - MAX_WINDOW:tm, 0:pwid]
        for gi, w in enumerate(POOL_WINDOWS):
            cols = slice(gi * GROUP, (gi + 1) * GROUP)
            xg = ext[:, cols]
            ws = _window_sums(xg, w, True)[MAX_WINDOW:, :]
            pooled = ws / _pool_counts(i, tm, w) - xg[MAX_WINDOW:, :]
            z = _dot(pooled.astype(BF16), pw_ref[gi].astype(BF16), NN)
            mbuf[:, cols] = (z * ps_ref[:, cols]).astype(BF16)

        cos, sin = cos_ref[...], sin_ref[...]
        for h in range(RET_HEADS):
            cq = slice(pwid + h * GROUP, pwid + (h + 1) * GROUP)
            ck = slice(pwid + rwid + h * GROUP, pwid + rwid + (h + 1) * GROUP)
            cv = slice(pwid + 2 * rwid + h * GROUP, pwid + 2 * rwid + (h + 1) * GROUP)
            cg = slice(pwid + 3 * rwid + h * GROUP, pwid + 3 * rwid + (h + 1) * GROUP)
            ch = slice(h * GROUP, (h + 1) * GROUP)
            qr = _rope(proj_ref[:, cq], cos, sin)
            kr = _rope(proj_ref[:, ck], cos, sin) * (GROUP ** -0.5)
            vb = proj_ref[:, cv].astype(BF16)
            for n in range(nck):
                rows = slice(n * GROUP, (n + 1) * GROUP)
                qc, kc, vc = qr[rows], kr[rows], vb[rows]
                rb = state[h]
                rs_ref[n, h] = rb
                p = (_dot(qc.astype(BF16), kc.astype(BF16), NT) * dec_ref[h]).astype(BF16)
                o = _dot(p, vc, NN) + _dot((qc * qh_ref[h]).astype(BF16), rb.astype(BF16), NN)
                state[h] = cd[h] * rb + _dot((kc * kt_ref[h]).astype(BF16), vc, TN)
                o_ref[rows, ch] = o
                on = o * lax.rsqrt(jnp.mean(o * o, axis=-1, keepdims=True) + EPS)
                gv = proj_ref[rows, cg]
                mbuf[rows, pwid + h * GROUP:pwid + (h + 1) * GROUP] = (
                    gv * _sigmoid(gv) * (on * rg_ref[:, ch])
                ).astype(BF16)
        h2_ref[...] = hv + _dot(mbuf[...], wouts[...], NN)

    tile = pl.BlockSpec((tm, d), lambda i: (i, 0))
    full = lambda shape: pl.BlockSpec(shape, lambda i: (0,) * len(shape))
    return _grid_call(
        body,
        carries,
        name="mix_fwd",
        work_first=True,
        grid=(s // tm,),
        in_specs=[
            tile, full((1, d)), ANY, ANY,
            full((N_POOL_GROUPS, GROUP, GROUP)), full((1, pwid)), full((1, rwid)),
            pl.BlockSpec((tm, GROUP), lambda i: (i, 0)), pl.BlockSpec((tm, GROUP), lambda i: (i, 0)),
            full((RET_HEADS, GROUP, GROUP)), full((RET_HEADS, GROUP, GROUP)), full((RET_HEADS, GROUP, GROUP)),
        ],
        out_specs=[
            tile,
            pl.BlockSpec((tm, inw), lambda i: (i, 0)),
            pl.BlockSpec((tm, rwid), lambda i: (i, 0)),
            pl.BlockSpec((nck, RET_HEADS, GROUP, GROUP), lambda i: (i, 0, 0, 0)),
        ],
        out_shape=[
            jax.ShapeDtypeStruct((s, d), F32),
            jax.ShapeDtypeStruct((s, inw), F32),
            jax.ShapeDtypeStruct((s, rwid), F32),
            jax.ShapeDtypeStruct((s // GROUP, RET_HEADS, GROUP, GROUP), F32),
        ],
        scratch_shapes=[
            pltpu.VMEM((inw, d), BF16), pltpu.VMEM((d, d), BF16),
            pltpu.VMEM((RET_HEADS, GROUP, GROUP), F32), pltpu.VMEM((MAX_WINDOW, pwid), F32),
            pltpu.VMEM((tm, d), BF16), pltpu.SemaphoreType.DMA((2 * NDEV,)),
        ],
        args=[h1, gain, weights[0], weights[1], pool_w, pool_scale, ret_gain,
              consts["cos"], consts["sin"], consts["decay"], consts["ktail"], consts["qhead"]],
    )


def _mix_bwd(dh2, h1, proj, o_saved, rsave, gain, weights, pool_w, pool_scale, ret_gain, consts, carries=()):
    s, d = h1.shape
    pwid = N_POOL_GROUPS * GROUP
    rwid = RET_HEADS * GROUP
    inw = pwid + 4 * rwid
    tm = min(256, s)
    nck = tm // GROUP
    nt = s // tm
    cd = consts["chunk_decay"]
    halo_per_tile = tm // MAX_WINDOW

    def body(dh2_ref, h_ref, proj_ref, halo_ref, o_ref, rs_ref, g_ref, bin_, bout, pw_ref, ps_ref, rg_ref,
             cos_ref, sin_ref, dec_ref, kt_ref, qh_ref,
             dh1_ref, dproj_ref, u_ref, m_ref, dpw_ref, dps_ref, drg_ref, dg_ref,
             wins, wouts, dstate, carry, dm, dpj, sems):
        i = pl.program_id(0)
        tile = nt - 1 - i

        @pl.when(i == 0)
        def _():
            for cp in _load_weights(((bin_, wins), (bout, wouts)), sems):
                cp.wait()
            dstate[...] = jnp.zeros_like(dstate)
            carry[...] = jnp.zeros_like(carry)
            for ref in (dpw_ref, dps_ref, drg_ref, dg_ref):
                ref[...] = jnp.zeros_like(ref)

        dh2v = dh2_ref[...]
        dm[...] = _dot(dh2v.astype(BF16), wouts[...], NT)
        hv = h_ref[...]
        g = g_ref[...]
        r = lax.rsqrt(jnp.mean(hv * hv, axis=-1, keepdims=True) + EPS)
        uh = hv * r
        u_ref[...] = (uh * g).astype(BF16)

        halo = jnp.where(tile == 0, 0.0, halo_ref[...])
        ext = jnp.concatenate([halo, proj_ref[:, 0:pwid]], axis=0)
        next_dpn = carry[...]
        for gi, w in enumerate(POOL_WINDOWS):
            cols = slice(gi * GROUP, (gi + 1) * GROUP)
            xg = ext[:, cols]
            cnt = _pool_counts(tile, tm, w)
            pooled = (_window_sums(xg, w, True)[MAX_WINDOW:, :] / cnt - xg[MAX_WINDOW:, :]).astype(BF16)
            pwb = pw_ref[gi].astype(BF16)
            z = _dot(pooled, pwb, NN)
            scale = ps_ref[:, cols]
            m_ref[:, cols] = (z * scale).astype(BF16)
            da = dm[:, cols]
            dps_ref[:, cols] += jnp.sum(da * z, axis=0, keepdims=True)
            dz = (da * scale).astype(BF16)
            dpw_ref[gi] += _dot(pooled, dz, TN)
            dpl = _dot(dz, pwb, NT)
            dpn = dpl / cnt
            ext2 = jnp.concatenate([dpn, next_dpn[:, cols]], axis=0)
            dpj[:, cols] = (_window_sums(ext2, w, False)[0:tm, :] - dpl).astype(BF16)
            carry[:, cols] = dpn[0:MAX_WINDOW, :]

        cos, sin = cos_ref[...], sin_ref[...]
        for h in range(RET_HEADS):
            cq = slice(pwid + h * GROUP, pwid + (h + 1) * GROUP)
            ck = slice(pwid + rwid + h * GROUP, pwid + rwid + (h + 1) * GROUP)
            cv = slice(pwid + 2 * rwid + h * GROUP, pwid + 2 * rwid + (h + 1) * GROUP)
            cg = slice(pwid + 3 * rwid + h * GROUP, pwid + 3 * rwid + (h + 1) * GROUP)
            ch = slice(h * GROUP, (h + 1) * GROUP)
            qr = _rope(proj_ref[:, cq], cos, sin)
            kr = _rope(proj_ref[:, ck], cos, sin) * (GROUP ** -0.5)
            vb = proj_ref[:, cv].astype(BF16)
            gv = proj_ref[:, cg]
            ov = o_ref[:, ch]
            ro = lax.rsqrt(jnp.mean(ov * ov, axis=-1, keepdims=True) + EPS)
            on = ov * ro
            rg = rg_ref[:, ch]
            db = dm[:, pwid + h * GROUP:pwid + (h + 1) * GROUP]
            sg = _sigmoid(gv)
            sl = gv * sg
            m_ref[:, pwid + h * GROUP:pwid + (h + 1) * GROUP] = (sl * (on * rg)).astype(BF16)
            dpj[:, cg] = (db * (on * rg) * (sg * (1.0 + gv * (1.0 - sg)))).astype(BF16)
            drg_ref[:, ch] += jnp.sum(db * sl * on, axis=0, keepdims=True)
            don = db * sl * rg
            do = (ro * (don - on * jnp.mean(don * on, axis=-1, keepdims=True))).astype(BF16)
            for n in reversed(range(nck)):
                rows = slice(n * GROUP, (n + 1) * GROUP)
                qc, kc, vc, dob = qr[rows], kr[rows], vb[rows], do[rows]
                qcb, kcb = qc.astype(BF16), kc.astype(BF16)
                qh = (qc * qh_ref[h]).astype(BF16)
                kt = (kc * kt_ref[h]).astype(BF16)
                rn = rs_ref[n, h].astype(BF16)
                dnext = dstate[h]
                dnb = dnext.astype(BF16)
                dec = dec_ref[h]
                p = (_dot(qcb, kcb, NT) * dec).astype(BF16)
                ds = (_dot(dob, vc, NT) * dec).astype(BF16)
                dv = _dot(p, dob, TN) + _dot(kt, dnb, NN)
                dq = _dot(ds, kcb, NN) + _dot(dob, rn, NT) * qh_ref[h]
                dk = _dot(ds, qcb, TN) + _dot(vc, dnb, NT) * kt_ref[h]
                dstate[h] = cd[h] * dnext + _dot(qh, dob, TN)
                dpj[rows, cq] = _rope_bwd(dq, cos[rows], sin[rows]).astype(BF16)
                dpj[rows, ck] = _rope_bwd(dk * (GROUP ** -0.5), cos[rows], sin[rows]).astype(BF16)
                dpj[rows, cv] = dv.astype(BF16)

        dproj_ref[...] = dpj[...]
        du = _dot(dpj[...], wins[...], NN)
        dg_ref[...] += jnp.sum(du * uh, axis=0, keepdims=True)
        dn = du * g
        dh1_ref[...] = dh2v + r * (dn - uh * jnp.mean(dn * uh, axis=-1, keepdims=True))

    rev = lambda i: (nt - 1 - i, 0)
    tile = pl.BlockSpec((tm, d), rev)
    full = lambda shape: pl.BlockSpec(shape, lambda i: (0,) * len(shape))
    return _grid_call(
        body,
        carries,
        name="mix_bwd",
        work_first=True,
        grid=(nt,),
        in_specs=[
            tile, tile,
            pl.BlockSpec((tm, inw), rev),
            pl.BlockSpec((MAX_WINDOW, pwid), lambda i: (jnp.maximum((nt - 1 - i) * halo_per_tile - 1, 0), 0)),
            pl.BlockSpec((tm, rwid), rev),
            pl.BlockSpec((nck, RET_HEADS, GROUP, GROUP), lambda i: (nt - 1 - i, 0, 0, 0)),
            full((1, d)), ANY, ANY,
            full((N_POOL_GROUPS, GROUP, GROUP)), full((1, pwid)), full((1, rwid)),
            pl.BlockSpec((tm, GROUP), rev), pl.BlockSpec((tm, GROUP), rev),
            full((RET_HEADS, GROUP, GROUP)), full((RET_HEADS, GROUP, GROUP)), full((RET_HEADS, GROUP, GROUP)),
        ],
        out_specs=[
            tile, pl.BlockSpec((tm, inw), rev), tile, tile,
            full((N_POOL_GROUPS, GROUP, GROUP)), full((1, pwid)), full((1, rwid)), full((1, d)),
        ],
        out_shape=[
            jax.ShapeDtypeStruct((s, d), F32),
            jax.ShapeDtypeStruct((s, inw), BF16),
            jax.ShapeDtypeStruct((s, d), BF16),
            jax.ShapeDtypeStruct((s, d), BF16),
            jax.ShapeDtypeStruct((N_POOL_GROUPS, GROUP, GROUP), F32),
            jax.ShapeDtypeStruct((1, pwid), F32),
            jax.ShapeDtypeStruct((1, rwid), F32),
            jax.ShapeDtypeStruct((1, d), F32),
        ],
        scratch_shapes=[
            pltpu.VMEM((inw, d), BF16), pltpu.VMEM((d, d), BF16),
            pltpu.VMEM((RET_HEADS, GROUP, GROUP), F32), pltpu.VMEM((MAX_WINDOW, pwid), F32),
            pltpu.VMEM((tm, d), F32), pltpu.VMEM((tm, inw), BF16), pltpu.SemaphoreType.DMA((2 * NDEV,)),
        ],
        args=[dh2, h1, proj, proj, o_saved, rsave, gain, weights[0], weights[1], pool_w, pool_scale, ret_gain,
              consts["cos"], consts["sin"], consts["decay"], consts["ktail"], consts["qhead"]],
    )


def _adam(w, g, m, v):
    m = ADAM_B1 * m + (1.0 - ADAM_B1) * g
    v = ADAM_B2 * v + (1.0 - ADAM_B2) * jnp.square(g)
    m_hat = m / (1.0 - ADAM_B1 ** ADAM_STEP)
    v_hat = v / (1.0 - ADAM_B2 ** ADAM_STEP)
    delta = -ADAM_LR * (m_hat / (jnp.sqrt(v_hat) + ADAM_EPS) + ADAM_WD * w)
    return delta, m, v


def _adamw_big(my_chip, w, sums, parts, m, v, name):
    rows, d = w.shape
    tr = _row_tile(rows, 176)

    def body(chip_ref, w_ref, own_ref, p_ref, m_ref, v_ref, g_ref, d_ref, nm_ref, nv_ref):
        g = own_ref[0].astype(F32)
        for q in range(NCHIP - 1):
            g = g + p_ref[q].astype(F32)
        g_ref[...] = g
        d_ref[...], nm_ref[...], nv_ref[...] = _adam(w_ref[...], g, m_ref[...], v_ref[...])

    spec = pl.BlockSpec((tr, d), lambda i, chip: (i, 0))
    return _call(
        body,
        name=name,
        grid_spec=pltpu.PrefetchScalarGridSpec(
            num_scalar_prefetch=1,
            grid=(rows // tr,),
            in_specs=[
                spec,
                pl.BlockSpec((1, tr, d), lambda i, chip: (chip[0], i, 0)),
                pl.BlockSpec((NCHIP - 1, tr, d), lambda i, chip: (0, i, 0)),
                spec,
                spec,
            ],
            out_specs=[spec] * 4,
        ),
        out_shape=[jax.ShapeDtypeStruct((rows, d), F32)] * 4,
        compiler_params=_seq(1),
    )(my_chip, w, sums, parts, m, v)


def _adamw_small(stats_all, pw_all, ws, ms, vs, pwid):
    nsmall = len(ws)

    def body(*refs):
        st_ref, pwa_ref = refs[0], refs[1]
        w_refs = refs[2:2 + nsmall]
        m_refs = refs[2 + nsmall:2 + 2 * nsmall]
        v_refs = refs[2 + 2 * nsmall:2 + 3 * nsmall]
        outs = refs[2 + 3 * nsmall:]
        st = st_ref[0]
        pwg = pwa_ref[0]
        for q in range(1, NDEV):
            st = st + st_ref[q]
            pwg = pwg + pwa_ref[q]
        grads = [st[0:1, :], st[1:2, :], st[2:3, :], st[3:4, :], st[4:5, 0:pwid], st[4:5, pwid:2 * pwid], pwg]
        outs[0][...] = jnp.zeros((1, GROUP), F32) + jnp.sum(st[5:6, :])
        for j in range(nsmall):
            delta, nm, nv = _adam(w_refs[j][...], grads[j], m_refs[j][...], v_refs[j][...])
            outs[1 + 4 * j][...] = grads[j]
            outs[2 + 4 * j][...] = delta
            outs[3 + 4 * j][...] = nm
            outs[4 + 4 * j][...] = nv

    out_shape = [jax.ShapeDtypeStruct((1, GROUP), F32)]
    for w in ws:
        out_shape += [jax.ShapeDtypeStruct(w.shape, F32)] * 4
    return _call(body, name="adamw_small", out_shape=out_shape, compiler_params=_params())(
        stats_all, pw_all, *ws, *ms, *vs
    )


def kernel(x, ffn1_norm, ffn1_w1, ffn1_w3, ffn1_w2, mix_norm, w_in, pool_w, pool_scale, ret_norm, w_out, ffn2_norm, ffn2_w1, ffn2_w3, ffn2_w2, final_norm, loss_target, m_ffn1_norm, m_ffn1_w1, m_ffn1_w3, m_ffn1_w2, m_mix_norm, m_w_in, m_pool_w, m_pool_scale, m_ret_norm, m_w_out, m_ffn2_norm, m_ffn2_w1, m_ffn2_w3, m_ffn2_w2, m_final_norm, v_ffn1_norm, v_ffn1_w1, v_ffn1_w3, v_ffn1_w2, v_mix_norm, v_w_in, v_pool_w, v_pool_scale, v_ret_norm, v_w_out, v_ffn2_norm, v_ffn2_w1, v_ffn2_w3, v_ffn2_w2, v_final_norm):
    s, d = x.shape[1], x.shape[2]
    ffn = ffn1_w1.shape[2] * NDEV
    pwid = pool_scale.shape[1]
    xs, tgt = x[0], loss_target[0]
    consts = _mix_constants(s)
    pw3 = pool_w[0]
    fnorm = final_norm.reshape(1, d)

    rows_of = lambda w, transposed: w[0].T if transposed else w[0]
    send_f1 = [rows_of(w, t).astype(BF16) for w, t in ((ffn1_w1, True), (ffn1_w3, True), (ffn1_w2, False))]
    later = [rows_of(w, t) for w, t in ((w_in, True), (w_out, False), (ffn2_w1, True), (ffn2_w3, True), (ffn2_w2, False))]

    sent_later, w13_f1 = _comm_call([_CastRows(later), _Gather(send_f1[:2])], "gather_ffn1")
    send_mix, send_f2 = sent_later[:2], sent_later[2:]
    (a1, b1, hm1), ((w2_f1, w_in_all),) = _ffn_up(xs, ffn1_norm, *w13_f1, carries=[_Gather(send_f1[2:] + send_mix[:1])])
    w_f1 = w13_f1 + [w2_f1]
    (h1,), ((w_out_all, w1_f2),) = _ffn_down(xs, hm1, w2_f1, carries=[_Gather(send_mix[1:] + send_f2[:1])])
    w_mix = [w_in_all, w_out_all]
    (h2, proj, o_saved, rsave), (rest,) = _mix_fwd(
        h1, mix_norm, w_mix, pw3, pool_scale, ret_norm, consts, carries=[_Gather(send_f2[1:])]
    )
    w_f2 = [w1_f2] + rest
    dh3, a2, b2, hm2, dgf, loss_cols = _ffn_fwd_loss(h2, ffn2_norm, w_f2, fnorm, tgt)

    (dh2, da2, db2, n2, dg2), _ = _ffn_bwd(dh3, h2, a2, b2, ffn2_norm, w_f2, ffn, "ffn2_bwd")
    sum_f2w1, _ = _wgrad(da2, n2, 1.0, "ffn2_w1_grad")
    sum_f2w3, _ = _wgrad(db2, n2, 1.0, "ffn2_w3_grad")
    sum_f2w2, ((parts_f2w1,),) = _wgrad(hm2, dh3, 0.5, "ffn2_w2_grad", carries=[_ChipScatter([sum_f2w1])])

    (dh1, dproj, u, mm, dpw, dps, drg, dgm), ((parts_f2w3, parts_f2w2),) = _mix_bwd(
        dh2, h1, proj, o_saved, rsave, mix_norm, w_mix, pw3, pool_scale, ret_norm, consts,
        carries=[_ChipScatter([sum_f2w3, sum_f2w2])],
    )
    (dx, da1, db1, n1, dg1), _ = _ffn_bwd(dh1, xs, a1, b1, ffn1_norm, w_f1, ffn, "ffn1_bwd")
    stats = jnp.concatenate(
        [dg1, dgm, dg2, dgf, jnp.concatenate([dps, drg], axis=1), loss_cols, jnp.zeros((2, d), F32)], axis=0
    )
    small = _GatherDirect([stats, dpw.reshape(N_POOL_GROUPS * GROUP, GROUP)])
    sum_f1w2, ((stats_all, pw_all),) = _wgrad(hm1, dh1, 0.5, "ffn1_w2_grad", carries=[small])
    sum_f1w1, ((parts_f1w2,),) = _wgrad(da1, n1, 1.0, "ffn1_w1_grad", carries=[_ChipScatter([sum_f1w2])])
    sum_f1w3, ((parts_f1w1,),) = _wgrad(db1, n1, 1.0, "ffn1_w3_grad", carries=[_ChipScatter([sum_f1w1])])
    sum_in, ((parts_f1w3,),) = _wgrad(dproj, u, 1.0, "w_in_grad", carries=[_ChipScatter([sum_f1w3])])
    sum_out, ((parts_in,),) = _wgrad(mm, dh2, 1.0, "w_out_grad", carries=[_ChipScatter([sum_in])])
    ((parts_out,),) = _comm_call([_ChipScatter([sum_out])], "scatter_last")

    big = (
        (ffn1_w1, m_ffn1_w1, v_ffn1_w1, sum_f1w1, parts_f1w1, True),
        (ffn1_w3, m_ffn1_w3, v_ffn1_w3, sum_f1w3, parts_f1w3, True),
        (ffn1_w2, m_ffn1_w2, v_ffn1_w2, sum_f1w2, parts_f1w2, False),
        (w_in, m_w_in, v_w_in, sum_in, parts_in, True),
        (w_out, m_w_out, v_w_out, sum_out, parts_out, False),
        (ffn2_w1, m_ffn2_w1, v_ffn2_w1, sum_f2w1, parts_f2w1, True),
        (ffn2_w3, m_ffn2_w3, v_ffn2_w3, sum_f2w3, parts_f2w3, True),
        (ffn2_w2, m_ffn2_w2, v_ffn2_w2, sum_f2w2, parts_f2w2, False),
    )
    my_chip = jnp.reshape(_chip(_peer(0)), (1,)).astype(jnp.int32)
    big_out = []
    for j, (w, m, v, sums, parts, t) in enumerate(big):
        view = (lambda a: a[0].T) if t else (lambda a: a[0])
        back = (lambda a: a.T[None]) if t else (lambda a: a[None])
        big_out.append([back(a) for a in _adamw_big(my_chip, view(w), sums, parts, view(m), view(v), "adamw_%d" % j)])

    small_w = (ffn1_norm, mix_norm, ffn2_norm, fnorm, pool_scale, ret_norm, pw3.reshape(-1, GROUP))
    small_m = (m_ffn1_norm, m_mix_norm, m_ffn2_norm, m_final_norm.reshape(1, d), m_pool_scale, m_ret_norm, m_pool_w.reshape(-1, GROUP))
    small_v = (v_ffn1_norm, v_mix_norm, v_ffn2_norm, v_final_norm.reshape(1, d), v_pool_scale, v_ret_norm, v_pool_w.reshape(-1, GROUP))
    res = _adamw_small(stats_all, pw_all, small_w, small_m, small_v, pwid)
    loss = res[0][0, 0]
    small_out = [list(res[1 + 4 * j:5 + 4 * j]) for j in range(len(small_w))]
    small_out[3] = [a.reshape(d) for a in small_out[3]]
    small_out[6] = [a.reshape(pool_w.shape) for a in small_out[6]]

    order = [small_out[0], big_out[0], big_out[1], big_out[2], small_out[1], big_out[3], small_out[6], small_out[4],
             small_out[5], big_out[4], small_out[2], big_out[5], big_out[6], big_out[7], small_out[3]]
    result = [loss, dx[None]]
    for kind in range(4):
        result += [t[kind] for t in order]
    return tuple(result)
```

```python
import functools

import numpy as np
import jax
import jax.numpy as jnp
from jax import lax
from jax.experimental import pallas as pl
from jax.experimental.pallas import tpu as pltpu

F32 = jnp.float32
BF16 = jnp.bfloat16

NDEV = 8
NCHIP = 4
EPS = 1e-6
N_POOL_GROUPS = 4
POOL_WINDOWS = (2, 4, 8, 16)
MAX_WINDOW = 16
GROUP = 128
RET_HEADS = 4
ROPE_BASE = 10000.0
ADAM_LR = 0.001
ADAM_B1 = 0.9
ADAM_B2 = 0.999
ADAM_EPS = 1e-08
ADAM_WD = 0.01
ADAM_STEP = 10

VMEM_LIMIT = 56 * 1024 * 1024
FFN_CHUNK = 256
ROW_BAND = 32

NT = (((1,), (1,)), ((), ()))
NN = (((1,), (0,)), ((), ()))
TN = (((0,), (0,)), ((), ()))

ANY = pl.BlockSpec(memory_space=pl.ANY)


def _dot(a, b, dims):
    return lax.dot_general(a, b, dims, preferred_element_type=F32)


def _call(body, **kw):
    return pl.pallas_call(body, **kw)


def _params(**kw):
    return pltpu.CompilerParams(vmem_limit_bytes=VMEM_LIMIT, **kw)


def _seq(n):
    return _params(dimension_semantics=("arbitrary",) * n)


def _peer(k):
    x, y, c = lax.axis_index("x"), lax.axis_index("y"), lax.axis_index("c")
    return (1 - x if k & 4 else x, 1 - y if k & 2 else y, 1 - c if k & 1 else c)


def _flat(pos):
    return 4 * pos[0] + 2 * pos[1] + pos[2]


def _chip(pos):
    return 2 * pos[0] + pos[1]


def _row_tile(rows, cap):
    return max(t for t in range(16, min(rows, cap) + 1, 16) if rows % t == 0)


def _pieces(rows, n):
    tiles = rows // 16
    cuts = [16 * (tiles * q // n) for q in range(n + 1)]
    return [(a, b - a) for a, b in zip(cuts[:-1], cuts[1:])]


def _load_weights(parts, sems):
    copies = []
    for buf, dst in parts:
        rows = buf.shape[1]
        for p in range(NDEV):
            cp = pltpu.make_async_copy(buf.at[p], dst.at[pl.ds(p * rows, rows), :], sems.at[len(copies)])
            cp.start()
            copies.append(cp)
    return copies


def _sigmoid(a):
    return 1.0 / (1.0 + jnp.exp(-a))


def _remote(src, dst, send_sem, recv_sem, to):
    return pltpu.make_async_remote_copy(
        src_ref=src, dst_ref=dst, send_sem=send_sem, recv_sem=recv_sem, device_id=to, device_id_type=pl.DeviceIdType.MESH
    )


class _Gather:
    X, Y, FAR = 4, 2, 6
    peers = (1, 2, 4)
    COPIES = 8

    def __init__(self, shards):
        n = len(shards)
        self.operands = list(shards)
        self.out_shape = [jax.ShapeDtypeStruct((NDEV,) + a.shape, a.dtype) for a in shards]
        self.sems = [
            pltpu.SemaphoreType.DMA((self.COPIES * n,)), pltpu.SemaphoreType.DMA((self.COPIES * n,)),
            pltpu.SemaphoreType.DMA((n,)),
        ]
        self.stages = [self.begin, self.relay, self.relay_far, self.end]

    def _copy(self, t, k, block, to, ins, outs, sems, own=False, half=None):
        rows = outs[t].shape[1]
        part = pl.ds(0, rows) if half is None else pl.ds(half * (rows // 2), rows // 2)
        dst = outs[t].at[_flat(block), part, :]
        at = self.COPIES * t + k
        return _remote(ins[t] if own else dst, dst, sems[0].at[at], sems[1].at[at], to)

    def _local(self, t, ins, outs, sems):
        return pltpu.make_async_copy(ins[t], outs[t].at[_flat(_peer(0))], sems[2].at[t])

    def begin(self, ins, outs, sems):
        me = _peer(0)
        for t in range(len(ins)):
            self._local(t, ins, outs, sems).start()
            for k, code in enumerate((1, self.X, self.Y)):
                self._copy(t, k, me, _peer(code), ins, outs, sems, own=True).start()

    def relay(self, ins, outs, sems):
        me, sibling = _peer(0), _peer(1)
        for t in range(len(ins)):
            self._copy(t, 1, _peer(self.X), me, ins, outs, sems).wait_recv()
            self._copy(t, 3, _peer(self.X), _peer(self.Y), ins, outs, sems, half=0).start()
            self._copy(t, 5, _peer(self.X), sibling, ins, outs, sems).start()
            self._copy(t, 2, _peer(self.Y), me, ins, outs, sems).wait_recv()
            self._copy(t, 4, _peer(self.Y), _peer(self.X), ins, outs, sems, half=1).start()
            self._copy(t, 6, _peer(self.Y), sibling, ins, outs, sems).start()

    def relay_far(self, ins, outs, sems):
        me, sibling = _peer(0), _peer(1)
        for t in range(len(ins)):
            self._copy(t, 3, _peer(self.FAR), me, ins, outs, sems, half=0).wait_recv()
            self._copy(t, 4, _peer(self.FAR), me, ins, outs, sems, half=1).wait_recv()
            self._copy(t, 7, _peer(self.FAR), sibling, ins, outs, sems).start()

    def end(self, ins, outs, sems):
        me = _peer(0)
        for t in range(len(ins)):
            self._copy(t, 0, _peer(1), me, ins, outs, sems).wait_recv()
            for k, code in ((5, self.X), (6, self.Y), (7, self.FAR)):
                self._copy(t, k, _peer(code ^ 1), me, ins, outs, sems).wait_recv()
            for k in range(self.COPIES):
                self._copy(t, k, me, me, ins, outs, sems, half=0 if k == 3 else 1 if k == 4 else None).wait_send()
            self._local(t, ins, outs, sems).wait()


class _GatherDirect:
    peers = tuple(range(1, NDEV))

    def __init__(self, arrays):
        n = len(arrays)
        self.operands = list(arrays)
        self.out_shape = [jax.ShapeDtypeStruct((NDEV,) + a.shape, a.dtype) for a in arrays]
        self.sems = [pltpu.SemaphoreType.DMA((7 * n,)), pltpu.SemaphoreType.DMA((7 * n,)), pltpu.SemaphoreType.DMA((n,))]
        self.stages = [self.begin, self.end]

    def begin(self, ins, outs, sems):
        mine = _flat(_peer(0))
        for t in range(len(ins)):
            pltpu.make_async_copy(ins[t], outs[t].at[mine], sems[2].at[t]).start()
            for k in range(1, NDEV):
                _remote(ins[t], outs[t].at[mine], sems[0].at[7 * t + k - 1], sems[1].at[7 * t + k - 1], _peer(k)).start()

    def end(self, ins, outs, sems):
        mine = _flat(_peer(0))
        for t in range(len(ins)):
            for k in range(1, NDEV):
                cp = _remote(ins[t], outs[t].at[_flat(_peer(k))], sems[0].at[7 * t + k - 1], sems[1].at[7 * t + k - 1], _peer(k))
                cp.wait_recv()
                cp.wait_send()
            pltpu.make_async_copy(ins[t], outs[t].at[mine], sems[2].at[t]).wait()


class _ChipScatter:
    peers = (2, 4, 6)
    across = (4, 2, 6)

    def __init__(self, sums):
        n = len(sums) * len(self.across)
        self.operands = list(sums)
        self.out_shape = [jax.ShapeDtypeStruct((len(self.across),) + a.shape[1:], a.dtype) for a in sums]
        self.sems = [pltpu.SemaphoreType.DMA((n,)), pltpu.SemaphoreType.DMA((n,))]
        self.stages = [self.begin, self.end]

    def _copies(self, ins, outs, sems):
        copies = []
        for t in range(len(ins)):
            for slot, k in enumerate(self.across):
                at = len(copies)
                copies.append(
                    _remote(ins[t].at[_chip(_peer(k))], outs[t].at[slot], sems[0].at[at], sems[1].at[at], _peer(k))
                )
        return copies

    def begin(self, ins, outs, sems):
        for cp in self._copies(ins, outs, sems):
            cp.start()

    def end(self, ins, outs, sems):
        for cp in self._copies(ins, outs, sems):
            cp.wait_recv()
            cp.wait_send()


class _CastRows:
    peers = ()

    def __init__(self, arrays):
        n = len(arrays)
        self.operands = list(arrays)
        self.out_shape = [jax.ShapeDtypeStruct(a.shape, BF16) for a in arrays]
        self.sems = [pltpu.SemaphoreType.DMA((n,)), pltpu.SemaphoreType.DMA((n,))]
        self.sems += [pltpu.VMEM(a.shape, F32) for a in arrays] + [pltpu.VMEM(a.shape, BF16) for a in arrays]
        self.stages = [self.begin, self.convert, self.end]

    def _moves(self, t, ins, outs, scratch):
        n = len(ins)
        load = pltpu.make_async_copy(ins[t], scratch[2 + t], scratch[0].at[t])
        store = pltpu.make_async_copy(scratch[2 + n + t], outs[t], scratch[1].at[t])
        return load, store

    def begin(self, ins, outs, scratch):
        for t in range(len(ins)):
            self._moves(t, ins, outs, scratch)[0].start()

    def convert(self, ins, outs, scratch):
        n = len(ins)
        for t in range(n):
            load, store = self._moves(t, ins, outs, scratch)
            load.wait()
            scratch[2 + n + t][...] = scratch[2 + t][...].astype(BF16)
            store.start()

    def end(self, ins, outs, scratch):
        for t in range(len(ins)):
            self._moves(t, ins, outs, scratch)[1].wait()


def _split_refs(refs, counts):
    out, at = [], 0
    for n in counts:
        out.append(refs[at:at + n])
        at += n
    return out


BARRIER_IDS = {(2, 4, 6): 0, (1, 2, 4): 1, (1,): 2, (1, 2, 4, 6): 3, tuple(range(1, NDEV)): 4}


def _peers_of(carries, own=()):
    peers = tuple(sorted(set(own).union(*[c.peers for c in carries])))
    return (peers, BARRIER_IDS[peers]) if peers in BARRIER_IDS else (None, None)


def _announce(peers):
    barrier = pltpu.get_barrier_semaphore()
    for k in peers:
        pl.semaphore_signal(barrier, inc=1, device_id=_peer(k), device_id_type=pl.DeviceIdType.MESH)


def _await(peers):
    pl.semaphore_wait(pltpu.get_barrier_semaphore(), len(peers))


def _handshake(peers):
    _announce(peers)
    _await(peers)


def _comm_call(carries, name):
    nin = [len(c.operands) for c in carries]
    nout = [len(c.out_shape) for c in carries]
    nsem = [len(c.sems) for c in carries]
    peers, collective_id = _peers_of(carries)

    def body(*refs):
        if peers:
            _handshake(peers)
        ins, outs, sems = _split_refs(refs, (sum(nin), sum(nout), sum(nsem)))
        parts = list(zip(carries, _split_refs(ins, nin), _split_refs(outs, nout), _split_refs(sems, nsem)))
        for depth in range(max(len(c.stages) for c in carries)):
            for c, i, o, s in parts:
                if depth < len(c.stages) - 1:
                    c.stages[depth](i, o, s)
        for c, i, o, s in parts:
            c.stages[-1](i, o, s)

    res = _call(
        body,
        name=name,
        out_shape=[sh for c in carries for sh in c.out_shape],
        in_specs=[ANY] * sum(nin),
        out_specs=[ANY] * sum(nout),
        scratch_shapes=[sm for c in carries for sm in c.sems],
        compiler_params=_params(has_side_effects=True, collective_id=collective_id),
    )(*[a for c in carries for a in c.operands])
    return _split_refs(list(res), nout)


def _grid_call(body, carries, *, name, grid, in_specs, out_specs, out_shape, scratch_shapes, args, own_peers=(),
               work_first=False):
    ni, no, ns = len(in_specs), len(out_specs), len(scratch_shapes)
    nin = [len(c.operands) for c in carries]
    nout = [len(c.out_shape) for c in carries]
    nsem = [len(c.sems) for c in carries]
    steps = int(np.prod(grid))
    peers, collective_id = _peers_of(carries, own_peers)

    def when_of(stage, count):
        first, last = (5 * steps) // 8 - 1, steps - 2
        return max(0, last if count <= 3 else first + (last - first) * (stage - 1) // (count - 3))

    def wrapped(*refs):
        ins, cins, outs, couts, scr, csems = _split_refs(refs, (ni, sum(nin), no, sum(nout), ns, sum(nsem)))
        if not carries and not peers:
            return body(*ins, *outs, *scr)
        parts = list(zip(carries, _split_refs(cins, nin), _split_refs(couts, nout), _split_refs(csems, nsem)))
        step = pl.program_id(0)
        for axis in range(1, len(grid)):
            step = step * grid[axis] + pl.program_id(axis)

        def first_stage():
            if peers:
                _await(peers)
            for c, i, o, s in parts:
                c.stages[0](i, o, s)

        if peers:
            pl.when(step == 0)(functools.partial(_announce, peers))
        if not work_first:
            pl.when(step == 0)(first_stage)

        body(*ins, *outs, *scr)

        if work_first:
            pl.when(step == 0)(first_stage)

        for c, i, o, s in parts:
            for stage in range(1, len(c.stages) - 1):
                pl.when(step == when_of(stage, len(c.stages)))(functools.partial(c.stages[stage], i, o, s))

        @pl.when(step == steps - 1)
        def _():
            for c, i, o, s in parts:
                c.stages[-1](i, o, s)

    res = _call(
        wrapped,
        name=name,
        grid=tuple(grid),
        in_specs=list(in_specs) + [ANY] * sum(nin),
        out_specs=list(out_specs) + [ANY] * sum(nout),
        out_shape=list(out_shape) + [sh for c in carries for sh in c.out_shape],
        scratch_shapes=list(scratch_shapes) + [sm for c in carries for sm in c.sems],
        compiler_params=_params(dimension_semantics=("arbitrary",) * len(grid), collective_id=collective_id),
    )(*args, *[a for c in carries for a in c.operands])
    res = list(res)
    return res[:no], _split_refs(res[no:], nout)


def _chunks(width):
    return [(at, min(FFN_CHUNK, width - at)) for at in range(0, width, FFN_CHUNK)]


def _ffn_fwd_loss(x, gain, weights, final_gain, target):
    s, d = x.shape
    ffn = weights[0].shape[0] * weights[0].shape[1]
    tm = min(512, s)

    def body(x_ref, g_ref, b1, b3, b2, gf_ref, t_ref, dh_ref, a_ref, b_ref, hm_ref, dgf_ref, loss_ref, w1s, w3s, w2s, sems):
        @pl.when(pl.program_id(0) == 0)
        def _():
            for cp in _load_weights(((b1, w1s), (b3, w3s), (b2, w2s)), sems):
                cp.wait()
            dgf_ref[...] = jnp.zeros_like(dgf_ref)
            loss_ref[...] = jnp.zeros_like(loss_ref)

        xv = x_ref[...]
        r = lax.rsqrt(jnp.mean(xv * xv, axis=-1, keepdims=True) + EPS)
        n = (xv * r * g_ref[...]).astype(BF16)
        acc = jnp.zeros((tm, d), F32)
        for at, width in _chunks(ffn):
            cols = slice(at, at + width)
            a = _dot(n, w1s[cols, :], NT)
            b = _dot(n, w3s[cols, :], NT)
            a_ref[:, cols] = a.astype(BF16)
            b_ref[:, cols] = b.astype(BF16)
            hm = (a * _sigmoid(a) * b).astype(BF16)
            hm_ref[:, cols] = hm
            acc = acc + _dot(hm, w2s[cols, :], NN)
        h = xv + 0.5 * acc
        rf = lax.rsqrt(jnp.mean(h * h, axis=-1, keepdims=True) + EPS)
        nh = h * rf
        gf = gf_ref[...]
        err = nh * gf - t_ref[...]
        loss_ref[...] += jnp.sum(err * err, axis=0, keepdims=True) * (0.5 / d)
        dy = err * (1.0 / d)
        dgf_ref[...] += jnp.sum(dy * nh, axis=0, keepdims=True)
        dn = dy * gf
        dh_ref[...] = rf * (dn - nh * jnp.mean(dn * nh, axis=-1, keepdims=True))

    tile = pl.BlockSpec((tm, d), lambda i: (i, 0))
    row = pl.BlockSpec((1, d), lambda i: (0, 0))
    wide = pl.BlockSpec((tm, ffn), lambda i: (i, 0))
    return _call(
        body,
        name="ffn_fwd_loss",
        grid=(s // tm,),
        in_specs=[tile, row, ANY, ANY, ANY, row, tile],
        out_specs=[tile, wide, wide, wide, row, row],
        out_shape=[jax.ShapeDtypeStruct((s, d), F32)] + [jax.ShapeDtypeStruct((s, ffn), BF16)] * 3
        + [jax.ShapeDtypeStruct((1, d), F32)] * 2,
        scratch_shapes=[pltpu.VMEM((ffn, d), BF16)] * 3 + [pltpu.SemaphoreType.DMA((3 * NDEV,))],
        compiler_params=_seq(1),
    )(x, gain, *weights, final_gain, target)


def _ffn_up(x, gain, w1, w3, carries=()):
    s, d = x.shape
    ffn = w1.shape[0] * w1.shape[1]
    tm = min(512, s)

    def body(x_ref, g_ref, b1, b3, a_ref, b_ref, hm_ref, w1s, w3s, sems):
        @pl.when(pl.program_id(0) == 0)
        def _():
            for cp in _load_weights(((b1, w1s), (b3, w3s)), sems):
                cp.wait()

        xv = x_ref[...]
        r = lax.rsqrt(jnp.mean(xv * xv, axis=-1, keepdims=True) + EPS)
        n = (xv * r * g_ref[...]).astype(BF16)
        for at, width in _chunks(ffn):
            cols = slice(at, at + width)
            a = _dot(n, w1s[cols, :], NT)
            b = _dot(n, w3s[cols, :], NT)
            a_ref[:, cols] = a.astype(BF16)
            b_ref[:, cols] = b.astype(BF16)
            hm_ref[:, cols] = (a * _sigmoid(a) * b).astype(BF16)

    wide = pl.BlockSpec((tm, ffn), lambda i: (i, 0))
    return _grid_call(
        body,
        carries,
        name="ffn_up",
        grid=(s // tm,),
        in_specs=[pl.BlockSpec((tm, d), lambda i: (i, 0)), pl.BlockSpec((1, d), lambda i: (0, 0)), ANY, ANY],
        out_specs=[wide] * 3,
        out_shape=[jax.ShapeDtypeStruct((s, ffn), BF16)] * 3,
        scratch_shapes=[pltpu.VMEM((ffn, d), BF16)] * 2 + [pltpu.SemaphoreType.DMA((2 * NDEV,))],
        args=[x, gain, w1, w3],
    )


def _ffn_down(x, hm, w2, carries=()):
    s, d = x.shape
    ffn = w2.shape[0] * w2.shape[1]
    tm = min(512, s)

    def body(x_ref, hm_ref, b2, h_ref, w2s, sems):
        @pl.when(pl.program_id(0) == 0)
        def _():
            for cp in _load_weights(((b2, w2s),), sems):
                cp.wait()

        acc = jnp.zeros((tm, d), F32)
        for at, width in _chunks(ffn):
            cols = slice(at, at + width)
            acc = acc + _dot(hm_ref[:, cols], w2s[cols, :], NN)
        h_ref[...] = x_ref[...] + 0.5 * acc

    tile = pl.BlockSpec((tm, d), lambda i: (i, 0))
    return _grid_call(
        body,
        carries,
        name="ffn_down",
        grid=(s // tm,),
        in_specs=[tile, pl.BlockSpec((tm, ffn), lambda i: (i, 0)), ANY],
        out_specs=[tile],
        out_shape=[jax.ShapeDtypeStruct((s, d), F32)],
        scratch_shapes=[pltpu.VMEM((ffn, d), BF16), pltpu.SemaphoreType.DMA((NDEV,))],
        args=[x, hm, w2],
    )


def _ffn_bwd(dh, x, a, b, gain, weights, ffn, name, carries=()):
    s, d = x.shape
    tm = min(512, s)
    halves = 2
    fh = ffn // halves

    def body(dh_ref, x_ref, a_ref, b_ref, g_ref, b1, b3, b2, dx_ref, da_ref, db_ref, n_ref, dg_ref, w1s, w3s, w2s, sems):
        i, j = pl.program_id(0), pl.program_id(1)

        @pl.when((i == 0) & (j == 0))
        def _():
            for cp in _load_weights(((b1, w1s), (b3, w3s), (b2, w2s)), sems):
                cp.wait()
            dg_ref[...] = jnp.zeros_like(dg_ref)

        @pl.when(j == 0)
        def _():
            dx_ref[...] = jnp.zeros_like(dx_ref)

        dob = (0.5 * dh_ref[...]).astype(BF16)
        chunks = _chunks(fh)

        def dhm_of(k):
            at, width = chunks[k]
            return _dot(dob, w2s[pl.ds(pl.multiple_of(j * fh + at, GROUP), width), :], NT)

        ahead = dhm_of(0)
        for k, (at, width) in enumerate(chunks):
            cols = slice(at, at + width)
            dhm = ahead
            if k + 1 < len(chunks):
                ahead = dhm_of(k + 1)
            for top in range(0, tm, ROW_BAND):
                band = slice(top, top + ROW_BAND)
                av = a_ref[band, cols].astype(F32)
                bv = b_ref[band, cols].astype(F32)
                sg = _sigmoid(av)
                dv = dhm[band]
                da_ref[band, cols] = (dv * bv * (sg * (1.0 + av * (1.0 - sg)))).astype(BF16)
                db_ref[band, cols] = (dv * (av * sg)).astype(BF16)
        half = pl.ds(pl.multiple_of(j * fh, GROUP), fh)
        dx_ref[...] += _dot(da_ref[...], w1s[half, :], NN) + _dot(db_ref[...], w3s[half, :], NN)

        @pl.when(j == halves - 1)
        def _():
            xv = x_ref[...]
            g = g_ref[...]
            r = lax.rsqrt(jnp.mean(xv * xv, axis=-1, keepdims=True) + EPS)
            nh = xv * r
            n_ref[...] = (nh * g).astype(BF16)
            total = dx_ref[...]
            dg_ref[...] += jnp.sum(total * nh, axis=0, keepdims=True)
            dnh = total * g
            dx_ref[...] = dh_ref[...] + r * (dnh - nh * jnp.mean(dnh * nh, axis=-1, keepdims=True))

    tile = pl.BlockSpec((tm, d), lambda i, j: (i, 0))
    row = pl.BlockSpec((1, d), lambda i, j: (0, 0))
    wide = pl.BlockSpec((tm, fh), lambda i, j: (i, j))
    return _grid_call(
        body,
        carries,
        name=name,
        grid=(s // tm, halves),
        in_specs=[tile, tile, wide, wide, row, ANY, ANY, ANY],
        out_specs=[tile, wide, wide, tile, row],
        out_shape=[
            jax.ShapeDtypeStruct((s, d), F32),
            jax.ShapeDtypeStruct((s, ffn), BF16),
            jax.ShapeDtypeStruct((s, ffn), BF16),
            jax.ShapeDtypeStruct((s, d), BF16),
            jax.ShapeDtypeStruct((1, d), F32),
        ],
        scratch_shapes=[pltpu.VMEM((ffn, d), BF16)] * 3 + [pltpu.SemaphoreType.DMA((3 * NDEV,))],
        args=[dh, x, a, b, gain] + list(weights),
    )


SWAP_PIECES = 1
RING = 3


def _wgrad(lhs, rhs, scale, name, carries=(), scatter=False):
    s, m = lhs.shape
    n = rhs.shape[1]
    rs = m // NDEV
    tk = min(1024, s)
    steps = s // tk
    pieces = [(j, at, size) for j in range(2) for at, size in _pieces(rs, SWAP_PIECES)]

    def body(l_hbm, r_hbm, o_ref, *rest):
        far_ref = rest[0] if scatter else None
        acc, mine, theirs, send_sems, recv_sems, l_buf, r_buf, load_sems, *far_sems = rest[1:] if scatter else rest
        h, k = pl.program_id(0), pl.program_id(1)
        step = h * steps + k

        def loads(t):
            if isinstance(t, int):
                slot, at, half = t % RING, (t % steps) * tk, (t // steps) * (m // 2)
            else:
                slot = lax.rem(t, RING)
                at = pl.multiple_of(lax.rem(t, steps) * tk, tk)
                half = pl.multiple_of(lax.div(t, steps) * (m // 2), GROUP)
            return (
                pltpu.make_async_copy(l_hbm.at[pl.ds(at, tk), pl.ds(half, m // 2)], l_buf.at[slot], load_sems.at[0, slot]),
                pltpu.make_async_copy(r_hbm.at[pl.ds(at, tk), :], r_buf.at[slot], load_sems.at[1, slot]),
            )

        @pl.when(step == 0)
        def _():
            for t in range(min(RING - 1, 2 * steps)):
                for cp in loads(t):
                    cp.start()

        @pl.when(step + RING - 1 < 2 * steps)
        def _():
            for cp in loads(step + RING - 1):
                cp.start()

        for cp in loads(step):
            cp.wait()
        slot = lax.rem(step, RING)

        @pl.when(k == 0)
        def _():
            acc[...] = _dot(l_buf[slot], r_buf[slot].astype(BF16), TN)

        @pl.when(k > 0)
        def _():
            acc[...] += _dot(l_buf[slot], r_buf[slot].astype(BF16), TN)

        def exchange(half):
            c = lax.axis_index("c")
            return [
                _remote(mine.at[half, 1 - c, j, pl.ds(at, size), :], theirs.at[half, j, pl.ds(at, size), :],
                        send_sems.at[half * len(pieces) + q], recv_sems.at[half * len(pieces) + q], _peer(1))
                for q, (j, at, size) in enumerate(pieces)
            ]

        def settle(half):
            for cp in exchange(half):
                cp.wait_recv()
            both = mine[half, lax.axis_index("c")].astype(F32) + theirs[half].astype(F32)
            o_ref[2 * half:2 * half + 2] = both.astype(BF16)
            for cp in exchange(half):
                cp.wait_send()

        for half in range(2):
            @pl.when((h == half) & (k == steps - 1))
            def _():
                for p in range(NCHIP):
                    mine[half, p % 2, p // 2] = (acc[p * rs:(p + 1) * rs, :] * scale).astype(BF16)
                for cp in exchange(half):
                    cp.start()
                if half == 1:
                    settle(0)
                    settle(1)
                    if scatter:
                        onward = [
                            _remote(o_ref.at[_chip(_peer(code))], far_ref.at[slot], far_sems[0].at[slot],
                                    far_sems[1].at[slot], _peer(code))
                            for slot, code in enumerate(_ChipScatter.across)
                        ]
                        for cp in onward:
                            cp.start()
                        for cp in onward:
                            cp.wait_recv()
                            cp.wait_send()

    far = len(_ChipScatter.across)
    outs, carried = _grid_call(
        body,
        carries,
        name=name,
        grid=(2, steps),
        in_specs=[ANY, ANY],
        out_specs=[pl.BlockSpec((NCHIP, rs, n), lambda h, k: (0, 0, 0))] + [ANY] * scatter,
        out_shape=[jax.ShapeDtypeStruct((NCHIP, rs, n), BF16)] + [jax.ShapeDtypeStruct((far, rs, n), BF16)] * scatter,
        scratch_shapes=[
            pltpu.VMEM((m // 2, n), F32), pltpu.VMEM((2, 2, 2, rs, n), BF16), pltpu.VMEM((2, 2, rs, n), BF16),
            pltpu.SemaphoreType.DMA((2 * len(pieces),)), pltpu.SemaphoreType.DMA((2 * len(pieces),)),
            pltpu.VMEM((RING, tk, m // 2), lhs.dtype), pltpu.VMEM((RING, tk, n), rhs.dtype),
            pltpu.SemaphoreType.DMA((2, RING)),
        ] + [pltpu.SemaphoreType.DMA((far,))] * (2 * scatter),
        args=[lhs, rhs],
        own_peers=(1,) + (_ChipScatter.peers if scatter else ()),
    )
    return (tuple(outs) if scatter else outs[0]), carried


def _mix_constants(s):
    c = GROUP
    lg = np.log1p(-np.exp2(-5.0 - np.arange(RET_HEADS, dtype=np.float32))).astype(np.float32)
    pos = np.arange(c, dtype=np.float32)
    rel = pos[:, None] - pos[None, :]
    decay = np.where(rel[None] >= 0, np.exp(lg[:, None, None] * np.maximum(rel, 0.0)[None]), 0.0).astype(np.float32)
    ktail = np.exp(lg[:, None] * (c - 1 - pos)[None, :]).astype(np.float32)
    qhead = np.exp(lg[:, None] * (pos + 1.0)[None, :]).astype(np.float32)
    chunk_decay = [float(v) for v in np.exp(lg * np.float32(c)).astype(np.float32)]
    ones = np.ones((1, 1, c), np.float32)
    inv_freq = (1.0 / (np.float32(ROPE_BASE) ** (np.arange(0, c, 2, dtype=np.float32) / np.float32(c)))).astype(np.float32)
    ang = (np.arange(s, dtype=np.float32)[:, None] * inv_freq[None, :]).astype(np.float32)
    cos, sin = np.cos(ang).astype(np.float32), np.sin(ang).astype(np.float32)
    return dict(
        decay=jnp.asarray(decay),
        ktail=jnp.asarray(ktail[:, :, None] * ones),
        qhead=jnp.asarray(qhead[:, :, None] * ones),
        chunk_decay=chunk_decay,
        cos=jnp.asarray(np.concatenate([cos, cos], axis=-1)),
        sin=jnp.asarray(np.concatenate([-sin, sin], axis=-1)),
    )


def _rope(t, cos, sin):
    return t * cos + pltpu.roll(t, GROUP // 2, axis=1) * sin


def _rope_bwd(dt, cos, sin):
    return dt * cos + pltpu.roll(dt * sin, GROUP // 2, axis=1)


def _window_sums(ext, w, forward):
    rows = ext.shape[0]
    acc, k = ext, 1
    while k < w:
        acc = acc + pltpu.roll(acc, k if forward else rows - k, axis=0)
        k *= 2
    return acc


def _pool_counts(tile, tm, w):
    t = lax.broadcasted_iota(jnp.int32, (tm, 1), 0) + tile * tm
    return jnp.minimum(t + 1, w).astype(F32)


def _mix_fwd(h1, gain, weights, pool_w, pool_scale, ret_gain, consts, carries=()):
    s, d = h1.shape
    pwid = N_POOL_GROUPS * GROUP
    rwid = RET_HEADS * GROUP
    inw = pwid + 4 * rwid
    tm = min(256, s)
    nck = tm // GROUP
    cd = consts["chunk_decay"]

    def body(h_ref, g_ref, bin_, bout, pw_ref, ps_ref, rg_ref, cos_ref, sin_ref, dec_ref, kt_ref, qh_ref,
             h2_ref, proj_ref, o_ref, rs_ref, wins, wouts, state, carry, mbuf, sems):
        i = pl.program_id(0)

        @pl.when(i == 0)
        def _():
            for cp in _load_weights(((bin_, wins), (bout, wouts)), sems):
                cp.wait()
            state[...] = jnp.zeros_like(state)
            carry[...] = jnp.zeros_like(carry)

        hv = h_ref[...]
        r = lax.rsqrt(jnp.mean(hv * hv, axis=-1, keepdims=True) + EPS)
        u = (hv * r * g_ref[...]).astype(BF16)
        proj_ref[...] = _dot(u, wins[...], NT)

        ext = jnp.concatenate([carry[...], proj_ref[:, 0:pwid]], axis=0)
        carry[...] = proj_ref[tm - MAX_WINDOW:tm, 0:pwid]
        for gi, w in enumerate(POOL_WINDOWS):
            cols = slice(gi * GROUP, (gi + 1) * GROUP)
            xg = ext[:, cols]
            ws = _window_sums(xg, w, True)[MAX_WINDOW:, :]
            pooled = ws / _pool_counts(i, tm, w) - xg[MAX_WINDOW:, :]
            z = _dot(pooled.astype(BF16), pw_ref[gi].astype(BF16), NN)
            mbuf[:, cols] = (z * ps_ref[:, cols]).astype(BF16)

        cos, sin = cos_ref[...], sin_ref[...]
        for h in range(RET_HEADS):
            cq = slice(pwid + h * GROUP, pwid + (h + 1) * GROUP)
            ck = slice(pwid + rwid + h * GROUP, pwid + rwid + (h + 1) * GROUP)
            cv = slice(pwid + 2 * rwid + h * GROUP, pwid + 2 * rwid + (h + 1) * GROUP)
            cg = slice(pwid + 3 * rwid + h * GROUP, pwid + 3 * rwid + (h + 1) * GROUP)
            ch = slice(h * GROUP, (h + 1) * GROUP)
            qr = _rope(proj_ref[:, cq], cos, sin)
            kr = _rope(proj_ref[:, ck], cos, sin) * (GROUP ** -0.5)
            vb = proj_ref[:, cv].astype(BF16)
            for n in range(nck):
                rows = slice(n * GROUP, (n + 1) * GROUP)
                qc, kc, vc = qr[rows], kr[rows], vb[rows]
                rb = state[h]
                rs_ref[n, h] = rb
                p = (_dot(qc.astype(BF16), kc.astype(BF16), NT) * dec_ref[h]).astype(BF16)
                o = _dot(p, vc, NN) + _dot((qc * qh_ref[h]).astype(BF16), rb.astype(BF16), NN)
                state[h] = cd[h] * rb + _dot((kc * kt_ref[h]).astype(BF16), vc, TN)
                o_ref[rows, ch] = o
                on = o * lax.rsqrt(jnp.mean(o * o, axis=-1, keepdims=True) + EPS)
                gv = proj_ref[rows, cg]
                mbuf[rows, pwid + h * GROUP:pwid + (h + 1) * GROUP] = (
                    gv * _sigmoid(gv) * (on * rg_ref[:, ch])
                ).astype(BF16)
        h2_ref[...] = hv + _dot(mbuf[...], wouts[...], NN)

    tile = pl.BlockSpec((tm, d), lambda i: (i, 0))
    full = lambda shape: pl.BlockSpec(shape, lambda i: (0,) * len(shape))
    return _grid_call(
        body,
        carries,
        name="mix_fwd",
        work_first=True,
        grid=(s // tm,),
        in_specs=[
            tile, full((1, d)), ANY, ANY,
            full((N_POOL_GROUPS, GROUP, GROUP)), full((1, pwid)), full((1, rwid)),
            pl.BlockSpec((tm, GROUP), lambda i: (i, 0)), pl.BlockSpec((tm, GROUP), lambda i: (i, 0)),
            full((RET_HEADS, GROUP, GROUP)), full((RET_HEADS, GROUP, GROUP)), full((RET_HEADS, GROUP, GROUP)),
        ],
        out_specs=[
            tile,
            pl.BlockSpec((tm, inw), lambda i: (i, 0)),
            pl.BlockSpec((tm, rwid), lambda i: (i, 0)),
            pl.BlockSpec((nck, RET_HEADS, GROUP, GROUP), lambda i: (i, 0, 0, 0)),
        ],
        out_shape=[
            jax.ShapeDtypeStruct((s, d), F32),
            jax.ShapeDtypeStruct((s, inw), F32),
            jax.ShapeDtypeStruct((s, rwid), F32),
            jax.ShapeDtypeStruct((s // GROUP, RET_HEADS, GROUP, GROUP), F32),
        ],
        scratch_shapes=[
            pltpu.VMEM((inw, d), BF16), pltpu.VMEM((d, d), BF16),
            pltpu.VMEM((RET_HEADS, GROUP, GROUP), F32), pltpu.VMEM((MAX_WINDOW, pwid), F32),
            pltpu.VMEM((tm, d), BF16), pltpu.SemaphoreType.DMA((2 * NDEV,)),
        ],
        args=[h1, gain, weights[0], weights[1], pool_w, pool_scale, ret_gain,
              consts["cos"], consts["sin"], consts["decay"], consts["ktail"], consts["qhead"]],
    )


def _mix_bwd(dh2, h1, proj, o_saved, rsave, gain, weights, pool_w, pool_scale, ret_gain, consts, carries=()):
    s, d = h1.shape
    pwid = N_POOL_GROUPS * GROUP
    rwid = RET_HEADS * GROUP
    inw = pwid + 4 * rwid
    tm = min(256, s)
    nck = tm // GROUP
    nt = s // tm
    cd = consts["chunk_decay"]
    halo_per_tile = tm // MAX_WINDOW

    def body(dh2_ref, h_ref, proj_ref, halo_ref, o_ref, rs_ref, g_ref, bin_, bout, pw_ref, ps_ref, rg_ref,
             cos_ref, sin_ref, dec_ref, kt_ref, qh_ref,
             dh1_ref, dproj_ref, u_ref, m_ref, dpw_ref, dps_ref, drg_ref, dg_ref,
             wins, wouts, dstate, carry, dm, dpj, sems):
        i = pl.program_id(0)
        tile = nt - 1 - i

        @pl.when(i == 0)
        def _():
            for cp in _load_weights(((bin_, wins), (bout, wouts)), sems):
                cp.wait()
            dstate[...] = jnp.zeros_like(dstate)
            carry[...] = jnp.zeros_like(carry)
            for ref in (dpw_ref, dps_ref, drg_ref, dg_ref):
                ref[...] = jnp.zeros_like(ref)

        dh2v = dh2_ref[...]
        dm[...] = _dot(dh2v.astype(BF16), wouts[...], NT)
        hv = h_ref[...]
        g = g_ref[...]
        r = lax.rsqrt(jnp.mean(hv * hv, axis=-1, keepdims=True) + EPS)
        uh = hv * r
        u_ref[...] = (uh * g).astype(BF16)

        halo = jnp.where(tile == 0, 0.0, halo_ref[...])
        ext = jnp.concatenate([halo, proj_ref[:, 0:pwid]], axis=0)
        next_dpn = carry[...]
        for gi, w in enumerate(POOL_WINDOWS):
            cols = slice(gi * GROUP, (gi + 1) * GROUP)
            xg = ext[:, cols]
            cnt = _pool_counts(tile, tm, w)
            pooled = (_window_sums(xg, w, True)[MAX_WINDOW:, :] / cnt - xg[MAX_WINDOW:, :]).astype(BF16)
            pwb = pw_ref[gi].astype(BF16)
            z = _dot(pooled, pwb, NN)
            scale = ps_ref[:, cols]
            m_ref[:, cols] = (z * scale).astype(BF16)
            da = dm[:, cols]
            dps_ref[:, cols] += jnp.sum(da * z, axis=0, keepdims=True)
            dz = (da * scale).astype(BF16)
            dpw_ref[gi] += _dot(pooled, dz, TN)
            dpl = _dot(dz, pwb, NT)
            dpn = dpl / cnt
            ext2 = jnp.concatenate([dpn, next_dpn[:, cols]], axis=0)
            dpj[:, cols] = (_window_sums(ext2, w, False)[0:tm, :] - dpl).astype(BF16)
            carry[:, cols] = dpn[0:MAX_WINDOW, :]

        cos, sin = cos_ref[...], sin_ref[...]
        for h in range(RET_HEADS):
            cq = slice(pwid + h * GROUP, pwid + (h + 1) * GROUP)
            ck = slice(pwid + rwid + h * GROUP, pwid + rwid + (h + 1) * GROUP)
            cv = slice(pwid + 2 * rwid + h * GROUP, pwid + 2 * rwid + (h + 1) * GROUP)
            cg = slice(pwid + 3 * rwid + h * GROUP, pwid + 3 * rwid + (h + 1) * GROUP)
            ch = slice(h * GROUP, (h + 1) * GROUP)
            qr = _rope(proj_ref[:, cq], cos, sin)
            kr = _rope(proj_ref[:, ck], cos, sin) * (GROUP ** -0.5)
            vb = proj_ref[:, cv].astype(BF16)
            gv = proj_ref[:, cg]
            ov = o_ref[:, ch]
            ro = lax.rsqrt(jnp.mean(ov * ov, axis=-1, keepdims=True) + EPS)
            on = ov * ro
            rg = rg_ref[:, ch]
            db = dm[:, pwid + h * GROUP:pwid + (h + 1) * GROUP]
            sg = _sigmoid(gv)
            sl = gv * sg
            m_ref[:, pwid + h * GROUP:pwid + (h + 1) * GROUP] = (sl * (on * rg)).astype(BF16)
            dpj[:, cg] = (db * (on * rg) * (sg * (1.0 + gv * (1.0 - sg)))).astype(BF16)
            drg_ref[:, ch] += jnp.sum(db * sl * on, axis=0, keepdims=True)
            don = db * sl * rg
            do = (ro * (don - on * jnp.mean(don * on, axis=-1, keepdims=True))).astype(BF16)
            for n in reversed(range(nck)):
                rows = slice(n * GROUP, (n + 1) * GROUP)
                qc, kc, vc, dob = qr[rows], kr[rows], vb[rows], do[rows]
                qcb, kcb = qc.astype(BF16), kc.astype(BF16)
                qh = (qc * qh_ref[h]).astype(BF16)
                kt = (kc * kt_ref[h]).astype(BF16)
                rn = rs_ref[n, h].astype(BF16)
                dnext = dstate[h]
                dnb = dnext.astype(BF16)
                dec = dec_ref[h]
                p = (_dot(qcb, kcb, NT) * dec).astype(BF16)
                ds = (_dot(dob, vc, NT) * dec).astype(BF16)
                dv = _dot(p, dob, TN) + _dot(kt, dnb, NN)
                dq = _dot(ds, kcb, NN) + _dot(dob, rn, NT) * qh_ref[h]
                dk = _dot(ds, qcb, TN) + _dot(vc, dnb, NT) * kt_ref[h]
                dstate[h] = cd[h] * dnext + _dot(qh, dob, TN)
                dpj[rows, cq] = _rope_bwd(dq, cos[rows], sin[rows]).astype(BF16)
                dpj[rows, ck] = _rope_bwd(dk * (GROUP ** -0.5), cos[rows], sin[rows]).astype(BF16)
                dpj[rows, cv] = dv.astype(BF16)

        dproj_ref[...] = dpj[...]
        du = _dot(dpj[...], wins[...], NN)
        dg_ref[...] += jnp.sum(du * uh, axis=0, keepdims=True)
        dn = du * g
        dh1_ref[...] = dh2v + r * (dn - uh * jnp.mean(dn * uh, axis=-1, keepdims=True))

    rev = lambda i: (nt - 1 - i, 0)
    tile = pl.BlockSpec((tm, d), rev)
    full = lambda shape: pl.BlockSpec(shape, lambda i: (0,) * len(shape))
    return _grid_call(
        body,
        carries,
        name="mix_bwd",
        work_first=True,
        grid=(nt,),
        in_specs=[
            tile, tile,
            pl.BlockSpec((tm, inw), rev),
            pl.BlockSpec((MAX_WINDOW, pwid), lambda i: (jnp.maximum((nt - 1 - i) * halo_per_tile - 1, 0), 0)),
            pl.BlockSpec((tm, rwid), rev),
            pl.BlockSpec((nck, RET_HEADS, GROUP, GROUP), lambda i: (nt - 1 - i, 0, 0, 0)),
            full((1, d)), ANY, ANY,
            full((N_POOL_GROUPS, GROUP, GROUP)), full((1, pwid)), full((1, rwid)),
            pl.BlockSpec((tm, GROUP), rev), pl.BlockSpec((tm, GROUP), rev),
            full((RET_HEADS, GROUP, GROUP)), full((RET_HEADS, GROUP, GROUP)), full((RET_HEADS, GROUP, GROUP)),
        ],
        out_specs=[
            tile, pl.BlockSpec((tm, inw), rev), tile, tile,
            full((N_POOL_GROUPS, GROUP, GROUP)), full((1, pwid)), full((1, rwid)), full((1, d)),
        ],
        out_shape=[
            jax.ShapeDtypeStruct((s, d), F32),
            jax.ShapeDtypeStruct((s, inw), BF16),
            jax.ShapeDtypeStruct((s, d), BF16),
            jax.ShapeDtypeStruct((s, d), BF16),
            jax.ShapeDtypeStruct((N_POOL_GROUPS, GROUP, GROUP), F32),
            jax.ShapeDtypeStruct((1, pwid), F32),
            jax.ShapeDtypeStruct((1, rwid), F32),
            jax.ShapeDtypeStruct((1, d), F32),
        ],
        scratch_shapes=[
            pltpu.VMEM((inw, d), BF16), pltpu.VMEM((d, d), BF16),
            pltpu.VMEM((RET_HEADS, GROUP, GROUP), F32), pltpu.VMEM((MAX_WINDOW, pwid), F32),
            pltpu.VMEM((tm, d), F32), pltpu.VMEM((tm, inw), BF16), pltpu.SemaphoreType.DMA((2 * NDEV,)),
        ],
        args=[dh2, h1, proj, proj, o_saved, rsave, gain, weights[0], weights[1], pool_w, pool_scale, ret_gain,
              consts["cos"], consts["sin"], consts["decay"], consts["ktail"], consts["qhead"]],
    )


def _adam(w, g, m, v):
    m = ADAM_B1 * m + (1.0 - ADAM_B1) * g
    v = ADAM_B2 * v + (1.0 - ADAM_B2) * jnp.square(g)
    m_hat = m / (1.0 - ADAM_B1 ** ADAM_STEP)
    v_hat = v / (1.0 - ADAM_B2 ** ADAM_STEP)
    delta = -ADAM_LR * (m_hat / (jnp.sqrt(v_hat) + ADAM_EPS) + ADAM_WD * w)
    return delta, m, v


def _adamw_big(my_chip, w, sums, parts, m, v, name):
    rows, d = w.shape
    tr = _row_tile(rows, 176)

    def body(chip_ref, w_ref, own_ref, p_ref, m_ref, v_ref, g_ref, d_ref, nm_ref, nv_ref):
        g = own_ref[0].astype(F32)
        for q in range(NCHIP - 1):
            g = g + p_ref[q].astype(F32)
        g_ref[...] = g
        d_ref[...], nm_ref[...], nv_ref[...] = _adam(w_ref[...], g, m_ref[...], v_ref[...])

    spec = pl.BlockSpec((tr, d), lambda i, chip: (i, 0))
    return _call(
        body,
        name=name,
        grid_spec=pltpu.PrefetchScalarGridSpec(
            num_scalar_prefetch=1,
            grid=(rows // tr,),
            in_specs=[
                spec,
                pl.BlockSpec((1, tr, d), lambda i, chip: (chip[0], i, 0)),
                pl.BlockSpec((NCHIP - 1, tr, d), lambda i, chip: (0, i, 0)),
                spec,
                spec,
            ],
            out_specs=[spec] * 4,
        ),
        out_shape=[jax.ShapeDtypeStruct((rows, d), F32)] * 4,
        compiler_params=_seq(1),
    )(my_chip, w, sums, parts, m, v)


def _adamw_small(stats_all, pw_all, ws, ms, vs, pwid):
    nsmall = len(ws)

    def body(*refs):
        st_ref, pwa_ref = refs[0], refs[1]
        w_refs = refs[2:2 + nsmall]
        m_refs = refs[2 + nsmall:2 + 2 * nsmall]
        v_refs = refs[2 + 2 * nsmall:2 + 3 * nsmall]
        outs = refs[2 + 3 * nsmall:]
        st = st_ref[0]
        pwg = pwa_ref[0]
        for q in range(1, NDEV):
            st = st + st_ref[q]
            pwg = pwg + pwa_ref[q]
        grads = [st[0:1, :], st[1:2, :], st[2:3, :], st[3:4, :], st[4:5, 0:pwid], st[4:5, pwid:2 * pwid], pwg]
        outs[0][...] = jnp.zeros((1, GROUP), F32) + jnp.sum(st[5:6, :])
        for j in range(nsmall):
            delta, nm, nv = _adam(w_refs[j][...], grads[j], m_refs[j][...], v_refs[j][...])
            outs[1 + 4 * j][...] = grads[j]
            outs[2 + 4 * j][...] = delta
            outs[3 + 4 * j][...] = nm
            outs[4 + 4 * j][...] = nv

    out_shape = [jax.ShapeDtypeStruct((1, GROUP), F32)]
    for w in ws:
        out_shape += [jax.ShapeDtypeStruct(w.shape, F32)] * 4
    return _call(body, name="adamw_small", out_shape=out_shape, compiler_params=_params())(
        stats_all, pw_all, *ws, *ms, *vs
    )


def kernel(x, ffn1_norm, ffn1_w1, ffn1_w3, ffn1_w2, mix_norm, w_in, pool_w, pool_scale, ret_norm, w_out, ffn2_norm, ffn2_w1, ffn2_w3, ffn2_w2, final_norm, loss_target, m_ffn1_norm, m_ffn1_w1, m_ffn1_w3, m_ffn1_w2, m_mix_norm, m_w_in, m_pool_w, m_pool_scale, m_ret_norm, m_w_out, m_ffn2_norm, m_ffn2_w1, m_ffn2_w3, m_ffn2_w2, m_final_norm, v_ffn1_norm, v_ffn1_w1, v_ffn1_w3, v_ffn1_w2, v_mix_norm, v_w_in, v_pool_w, v_pool_scale, v_ret_norm, v_w_out, v_ffn2_norm, v_ffn2_w1, v_ffn2_w3, v_ffn2_w2, v_final_norm):
    s, d = x.shape[1], x.shape[2]
    ffn = ffn1_w1.shape[2] * NDEV
    pwid = pool_scale.shape[1]
    xs, tgt = x[0], loss_target[0]
    consts = _mix_constants(s)
    pw3 = pool_w[0]
    fnorm = final_norm.reshape(1, d)

    rows_of = lambda w, transposed: w[0].T if transposed else w[0]
    send_f1 = [rows_of(w, t).astype(BF16) for w, t in ((ffn1_w1, True), (ffn1_w3, True), (ffn1_w2, False))]
    later = [rows_of(w, t) for w, t in ((w_in, True), (w_out, False), (ffn2_w1, True), (ffn2_w3, True), (ffn2_w2, False))]

    sent_later, w13_f1 = _comm_call([_CastRows(later), _Gather(send_f1[:2])], "gather_ffn1")
    send_mix, send_f2 = sent_later[:2], sent_later[2:]
    (a1, b1, hm1), ((w2_f1, w_in_all),) = _ffn_up(xs, ffn1_norm, *w13_f1, carries=[_Gather(send_f1[2:] + send_mix[:1])])
    w_f1 = w13_f1 + [w2_f1]
    (h1,), ((w_out_all, w1_f2),) = _ffn_down(xs, hm1, w2_f1, carries=[_Gather(send_mix[1:] + send_f2[:1])])
    w_mix = [w_in_all, w_out_all]
    (h2, proj, o_saved, rsave), (rest,) = _mix_fwd(
        h1, mix_norm, w_mix, pw3, pool_scale, ret_norm, consts, carries=[_Gather(send_f2[1:])]
    )
    w_f2 = [w1_f2] + rest
    dh3, a2, b2, hm2, dgf, loss_cols = _ffn_fwd_loss(h2, ffn2_norm, w_f2, fnorm, tgt)

    (dh2, da2, db2, n2, dg2), _ = _ffn_bwd(dh3, h2, a2, b2, ffn2_norm, w_f2, ffn, "ffn2_bwd")
    sum_f2w1, _ = _wgrad(da2, n2, 1.0, "ffn2_w1_grad")
    sum_f2w3, _ = _wgrad(db2, n2, 1.0, "ffn2_w3_grad")
    sum_f2w2, ((parts_f2w1,),) = _wgrad(hm2, dh3, 0.5, "ffn2_w2_grad", carries=[_ChipScatter([sum_f2w1])])

    (dh1, dproj, u, mm, dpw, dps, drg, dgm), ((parts_f2w3, parts_f2w2),) = _mix_bwd(
        dh2, h1, proj, o_saved, rsave, mix_norm, w_mix, pw3, pool_scale, ret_norm, consts,
        carries=[_ChipScatter([sum_f2w3, sum_f2w2])],
    )
    (dx, da1, db1, n1, dg1), _ = _ffn_bwd(dh1, xs, a1, b1, ffn1_norm, w_f1, ffn, "ffn1_bwd")
    stats = jnp.concatenate(
        [dg1, dgm, dg2, dgf, jnp.concatenate([dps, drg], axis=1), loss_cols, jnp.zeros((2, d), F32)], axis=0
    )
    small = _GatherDirect([stats, dpw.reshape(N_POOL_GROUPS * GROUP, GROUP)])
    sum_f1w2, ((stats_all, pw_all),) = _wgrad(hm1, dh1, 0.5, "ffn1_w2_grad", carries=[small])
    sum_f1w1, ((parts_f1w2,),) = _wgrad(da1, n1, 1.0, "ffn1_w1_grad", carries=[_ChipScatter([sum_f1w2])])
    sum_f1w3, ((parts_f1w1,),) = _wgrad(db1, n1, 1.0, "ffn1_w3_grad", carries=[_ChipScatter([sum_f1w1])])
    sum_in, ((parts_f1w3,),) = _wgrad(dproj, u, 1.0, "w_in_grad", carries=[_ChipScatter([sum_f1w3])])
    (sum_out, parts_out), ((parts_in,),) = _wgrad(
        mm, dh2, 1.0, "w_out_grad", carries=[_ChipScatter([sum_in])], scatter=True
    )

    big = (
        (ffn1_w1, m_ffn1_w1, v_ffn1_w1, sum_f1w1, parts_f1w1, True),
        (ffn1_w3, m_ffn1_w3, v_ffn1_w3, sum_f1w3, parts_f1w3, True),
        (ffn1_w2, m_ffn1_w2, v_ffn1_w2, sum_f1w2, parts_f1w2, False),
        (w_in, m_w_in, v_w_in, sum_in, parts_in, True),
        (w_out, m_w_out, v_w_out, sum_out, parts_out, False),
        (ffn2_w1, m_ffn2_w1, v_ffn2_w1, sum_f2w1, parts_f2w1, True),
        (ffn2_w3, m_ffn2_w3, v_ffn2_w3, sum_f2w3, parts_f2w3, True),
        (ffn2_w2, m_ffn2_w2, v_ffn2_w2, sum_f2w2, parts_f2w2, False),
    )
    my_chip = jnp.reshape(_chip(_peer(0)), (1,)).astype(jnp.int32)
    big_out = []
    for j, (w, m, v, sums, parts, t) in enumerate(big):
        view = (lambda a: a[0].T) if t else (lambda a: a[0])
        back = (lambda a: a.T[None]) if t else (lambda a: a[None])
        big_out.append([back(a) for a in _adamw_big(my_chip, view(w), sums, parts, view(m), view(v), "adamw_%d" % j)])

    small_w = (ffn1_norm, mix_norm, ffn2_norm, fnorm, pool_scale, ret_norm, pw3.reshape(-1, GROUP))
    small_m = (m_ffn1_norm, m_mix_norm, m_ffn2_norm, m_final_norm.reshape(1, d), m_pool_scale, m_ret_norm, m_pool_w.reshape(-1, GROUP))
    small_v = (v_ffn1_norm, v_mix_norm, v_ffn2_norm, v_final_norm.reshape(1, d), v_pool_scale, v_ret_norm, v_pool_w.reshape(-1, GROUP))
    res = _adamw_small(stats_all, pw_all, small_w, small_m, small_v, pwid)
    loss = res[0][0, 0]
    small_out = [list(res[1 + 4 * j:5 + 4 * j]) for j in range(len(small_w))]
    small_out[3] = [a.reshape(d) for a in small_out[3]]
    small_out[6] = [a.reshape(pool_w.shape) for a in small_out[6]]

    order = [small_out[0], big_out[0], big_out[1], big_out[2], small_out[1], big_out[3], small_out[6], small_out[4],
             small_out[5], big_out[4], small_out[2], big_out[5], big_out[6], big_out[7], small_out[3]]
    result = [loss, dx[None]]
    for kind in range(4):
        result += [t[kind] for t in order]
    return tuple(result)
```

```python
import functools

import numpy as np
import jax
import jax.numpy as jnp
from jax import lax
from jax.experimental import pallas as pl
from jax.experimental.pallas import tpu as pltpu

F32 = jnp.float32
BF16 = jnp.bfloat16

NDEV = 8
NCHIP = 4
EPS = 1e-6
N_POOL_GROUPS = 4
POOL_WINDOWS = (2, 4, 8, 16)
MAX_WINDOW = 16
GROUP = 128
RET_HEADS = 4
ROPE_BASE = 10000.0
ADAM_LR = 0.001
ADAM_B1 = 0.9
ADAM_B2 = 0.999
ADAM_EPS = 1e-08
ADAM_WD = 0.01
ADAM_STEP = 10

VMEM_LIMIT = 56 * 1024 * 1024
FFN_CHUNK = 256
ROW_BAND = 32

NT = (((1,), (1,)), ((), ()))
NN = (((1,), (0,)), ((), ()))
TN = (((0,), (0,)), ((), ()))

ANY = pl.BlockSpec(memory_space=pl.ANY)


def _dot(a, b, dims):
    return lax.dot_general(a, b, dims, preferred_element_type=F32)


def _call(body, **kw):
    return pl.pallas_call(body, **kw)


def _params(**kw):
    return pltpu.CompilerParams(vmem_limit_bytes=VMEM_LIMIT, **kw)


def _seq(n):
    return _params(dimension_semantics=("arbitrary",) * n)


def _peer(k):
    x, y, c = lax.axis_index("x"), lax.axis_index("y"), lax.axis_index("c")
    return (1 - x if k & 4 else x, 1 - y if k & 2 else y, 1 - c if k & 1 else c)


def _flat(pos):
    return 4 * pos[0] + 2 * pos[1] + pos[2]


def _chip(pos):
    return 2 * pos[0] + pos[1]


def _row_tile(rows, cap):
    return max(t for t in range(16, min(rows, cap) + 1, 16) if rows % t == 0)


def _pieces(rows, n):
    tiles = rows // 16
    cuts = [16 * (tiles * q // n) for q in range(n + 1)]
    return [(a, b - a) for a, b in zip(cuts[:-1], cuts[1:])]


def _load_weights(parts, sems):
    copies = []
    for buf, dst in parts:
        rows = buf.shape[1]
        for p in range(NDEV):
            cp = pltpu.make_async_copy(buf.at[p], dst.at[pl.ds(p * rows, rows), :], sems.at[len(copies)])
            cp.start()
            copies.append(cp)
    return copies


def _sigmoid(a):
    return 1.0 / (1.0 + jnp.exp(-a))


def _remote(src, dst, send_sem, recv_sem, to):
    return pltpu.make_async_remote_copy(
        src_ref=src, dst_ref=dst, send_sem=send_sem, recv_sem=recv_sem, device_id=to, device_id_type=pl.DeviceIdType.MESH
    )


class _Gather:
    X, Y, FAR = 4, 2, 6
    peers = (1, 2, 4)
    COPIES = 8

    def __init__(self, shards):
        n = len(shards)
        self.operands = list(shards)
        self.out_shape = [jax.ShapeDtypeStruct((NDEV,) + a.shape, a.dtype) for a in shards]
        self.sems = [
            pltpu.SemaphoreType.DMA((self.COPIES * n,)), pltpu.SemaphoreType.DMA((self.COPIES * n,)),
            pltpu.SemaphoreType.DMA((n,)),
        ]
        self.stages = [self.begin, self.relay, self.relay_far, self.end]

    def _copy(self, t, k, block, to, ins, outs, sems, own=False, half=None):
        rows = outs[t].shape[1]
        part = pl.ds(0, rows) if half is None else pl.ds(half * (rows // 2), rows // 2)
        dst = outs[t].at[_flat(block), part, :]
        at = self.COPIES * t + k
        return _remote(ins[t] if own else dst, dst, sems[0].at[at], sems[1].at[at], to)

    def _local(self, t, ins, outs, sems):
        return pltpu.make_async_copy(ins[t], outs[t].at[_flat(_peer(0))], sems[2].at[t])

    def begin(self, ins, outs, sems):
        me = _peer(0)
        for t in range(len(ins)):
            self._local(t, ins, outs, sems).start()
            for k, code in enumerate((1, self.X, self.Y)):
                self._copy(t, k, me, _peer(code), ins, outs, sems, own=True).start()

    def relay(self, ins, outs, sems):
        me, sibling = _peer(0), _peer(1)
        for t in range(len(ins)):
            self._copy(t, 1, _peer(self.X), me, ins, outs, sems).wait_recv()
            self._copy(t, 3, _peer(self.X), _peer(self.Y), ins, outs, sems, half=0).start()
            self._copy(t, 5, _peer(self.X), sibling, ins, outs, sems).start()
            self._copy(t, 2, _peer(self.Y), me, ins, outs, sems).wait_recv()
            self._copy(t, 4, _peer(self.Y), _peer(self.X), ins, outs, sems, half=1).start()
            self._copy(t, 6, _peer(self.Y), sibling, ins, outs, sems).start()

    def relay_far(self, ins, outs, sems):
        me, sibling = _peer(0), _peer(1)
        for t in range(len(ins)):
            self._copy(t, 3, _peer(self.FAR), me, ins, outs, sems, half=0).wait_recv()
            self._copy(t, 4, _peer(self.FAR), me, ins, outs, sems, half=1).wait_recv()
            self._copy(t, 7, _peer(self.FAR), sibling, ins, outs, sems).start()

    def end(self, ins, outs, sems):
        me = _peer(0)
        for t in range(len(ins)):
            self._copy(t, 0, _peer(1), me, ins, outs, sems).wait_recv()
            for k, code in ((5, self.X), (6, self.Y), (7, self.FAR)):
                self._copy(t, k, _peer(code ^ 1), me, ins, outs, sems).wait_recv()
            for k in range(self.COPIES):
                self._copy(t, k, me, me, ins, outs, sems, half=0 if k == 3 else 1 if k == 4 else None).wait_send()
            self._local(t, ins, outs, sems).wait()


class _GatherDirect:
    peers = tuple(range(1, NDEV))

    def __init__(self, arrays):
        n = len(arrays)
        self.operands = list(arrays)
        self.out_shape = [jax.ShapeDtypeStruct((NDEV,) + a.shape, a.dtype) for a in arrays]
        self.sems = [pltpu.SemaphoreType.DMA((7 * n,)), pltpu.SemaphoreType.DMA((7 * n,)), pltpu.SemaphoreType.DMA((n,))]
        self.stages = [self.begin, self.end]

    def begin(self, ins, outs, sems):
        mine = _flat(_peer(0))
        for t in range(len(ins)):
            pltpu.make_async_copy(ins[t], outs[t].at[mine], sems[2].at[t]).start()
            for k in range(1, NDEV):
                _remote(ins[t], outs[t].at[mine], sems[0].at[7 * t + k - 1], sems[1].at[7 * t + k - 1], _peer(k)).start()

    def end(self, ins, outs, sems):
        mine = _flat(_peer(0))
        for t in range(len(ins)):
            for k in range(1, NDEV):
                cp = _remote(ins[t], outs[t].at[_flat(_peer(k))], sems[0].at[7 * t + k - 1], sems[1].at[7 * t + k - 1], _peer(k))
                cp.wait_recv()
                cp.wait_send()
            pltpu.make_async_copy(ins[t], outs[t].at[mine], sems[2].at[t]).wait()


class _ChipScatter:
    peers = (2, 4, 6)
    across = (4, 2, 6)

    def __init__(self, sums):
        n = len(sums) * len(self.across)
        self.operands = list(sums)
        self.out_shape = [jax.ShapeDtypeStruct((len(self.across),) + a.shape[1:], a.dtype) for a in sums]
        self.sems = [pltpu.SemaphoreType.DMA((n,)), pltpu.SemaphoreType.DMA((n,))]
        self.stages = [self.begin, self.end]

    def _copies(self, ins, outs, sems):
        copies = []
        for t in range(len(ins)):
            for slot, k in enumerate(self.across):
                at = len(copies)
                copies.append(
                    _remote(ins[t].at[_chip(_peer(k))], outs[t].at[slot], sems[0].at[at], sems[1].at[at], _peer(k))
                )
        return copies

    def begin(self, ins, outs, sems):
        for cp in self._copies(ins, outs, sems):
            cp.start()

    def end(self, ins, outs, sems):
        for cp in self._copies(ins, outs, sems):
            cp.wait_recv()
            cp.wait_send()


class _CastRows:
    peers = ()

    def __init__(self, arrays):
        n = len(arrays)
        self.operands = list(arrays)
        self.out_shape = [jax.ShapeDtypeStruct(a.shape, BF16) for a in arrays]
        self.sems = [pltpu.SemaphoreType.DMA((n,)), pltpu.SemaphoreType.DMA((n,))]
        self.sems += [pltpu.VMEM(a.shape, F32) for a in arrays] + [pltpu.VMEM(a.shape, BF16) for a in arrays]
        self.stages = [self.begin, self.convert, self.end]

    def _moves(self, t, ins, outs, scratch):
        n = len(ins)
        load = pltpu.make_async_copy(ins[t], scratch[2 + t], scratch[0].at[t])
        store = pltpu.make_async_copy(scratch[2 + n + t], outs[t], scratch[1].at[t])
        return load, store

    def begin(self, ins, outs, scratch):
        for t in range(len(ins)):
            self._moves(t, ins, outs, scratch)[0].start()

    def convert(self, ins, outs, scratch):
        n = len(ins)
        for t in range(n):
            load, store = self._moves(t, ins, outs, scratch)
            load.wait()
            scratch[2 + n + t][...] = scratch[2 + t][...].astype(BF16)
            store.start()

    def end(self, ins, outs, scratch):
        for t in range(len(ins)):
            self._moves(t, ins, outs, scratch)[1].wait()


def _split_refs(refs, counts):
    out, at = [], 0
    for n in counts:
        out.append(refs[at:at + n])
        at += n
    return out


BARRIER_IDS = {(2, 4, 6): 0, (1, 2, 4): 1, (1,): 2, (1, 2, 4, 6): 3, tuple(range(1, NDEV)): 4}


def _peers_of(carries, own=()):
    peers = tuple(sorted(set(own).union(*[c.peers for c in carries])))
    return (peers, BARRIER_IDS[peers]) if peers in BARRIER_IDS else (None, None)


def _announce(peers):
    barrier = pltpu.get_barrier_semaphore()
    for k in peers:
        pl.semaphore_signal(barrier, inc=1, device_id=_peer(k), device_id_type=pl.DeviceIdType.MESH)


def _await(peers):
    pl.semaphore_wait(pltpu.get_barrier_semaphore(), len(peers))


def _handshake(peers):
    _announce(peers)
    _await(peers)


def _comm_call(carries, name):
    nin = [len(c.operands) for c in carries]
    nout = [len(c.out_shape) for c in carries]
    nsem = [len(c.sems) for c in carries]
    peers, collective_id = _peers_of(carries)

    def body(*refs):
        if peers:
            _handshake(peers)
        ins, outs, sems = _split_refs(refs, (sum(nin), sum(nout), sum(nsem)))
        parts = list(zip(carries, _split_refs(ins, nin), _split_refs(outs, nout), _split_refs(sems, nsem)))
        for depth in range(max(len(c.stages) for c in carries)):
            for c, i, o, s in parts:
                if depth < len(c.stages) - 1:
                    c.stages[depth](i, o, s)
        for c, i, o, s in parts:
            c.stages[-1](i, o, s)

    res = _call(
        body,
        name=name,
        out_shape=[sh for c in carries for sh in c.out_shape],
        in_specs=[ANY] * sum(nin),
        out_specs=[ANY] * sum(nout),
        scratch_shapes=[sm for c in carries for sm in c.sems],
        compiler_params=_params(has_side_effects=True, collective_id=collective_id),
    )(*[a for c in carries for a in c.operands])
    return _split_refs(list(res), nout)


def _grid_call(body, carries, *, name, grid, in_specs, out_specs, out_shape, scratch_shapes, args, own_peers=(),
               work_first=False, prelude=None):
    ni, no, ns = len(in_specs), len(out_specs), len(scratch_shapes)
    nin = [len(c.operands) for c in carries]
    nout = [len(c.out_shape) for c in carries]
    nsem = [len(c.sems) for c in carries]
    steps = int(np.prod(grid))
    peers, collective_id = _peers_of(carries, own_peers)

    def when_of(stage, count):
        first, last = (5 * steps) // 8 - 1, steps - 2
        return max(0, last if count <= 3 else first + (last - first) * (stage - 1) // (count - 3))

    def wrapped(*refs):
        ins, cins, outs, couts, scr, csems = _split_refs(refs, (ni, sum(nin), no, sum(nout), ns, sum(nsem)))
        if not carries and not peers and not prelude:
            return body(*ins, *outs, *scr)
        parts = list(zip(carries, _split_refs(cins, nin), _split_refs(couts, nout), _split_refs(csems, nsem)))
        step = pl.program_id(0)
        for axis in range(1, len(grid)):
            step = step * grid[axis] + pl.program_id(axis)

        def first_stage():
            if peers:
                _await(peers)
            for c, i, o, s in parts:
                c.stages[0](i, o, s)

        if peers:
            pl.when(step == 0)(functools.partial(_announce, peers))
        if prelude:
            pl.when(step == 0)(functools.partial(prelude, *ins, *outs, *scr))
        if not work_first:
            pl.when(step == 0)(first_stage)

        body(*ins, *outs, *scr)

        if work_first:
            pl.when(step == 0)(first_stage)

        for c, i, o, s in parts:
            for stage in range(1, len(c.stages) - 1):
                pl.when(step == when_of(stage, len(c.stages)))(functools.partial(c.stages[stage], i, o, s))

        @pl.when(step == steps - 1)
        def _():
            for c, i, o, s in parts:
                c.stages[-1](i, o, s)

    res = _call(
        wrapped,
        name=name,
        grid=tuple(grid),
        in_specs=list(in_specs) + [ANY] * sum(nin),
        out_specs=list(out_specs) + [ANY] * sum(nout),
        out_shape=list(out_shape) + [sh for c in carries for sh in c.out_shape],
        scratch_shapes=list(scratch_shapes) + [sm for c in carries for sm in c.sems],
        compiler_params=_params(dimension_semantics=("arbitrary",) * len(grid), collective_id=collective_id),
    )(*args, *[a for c in carries for a in c.operands])
    res = list(res)
    return res[:no], _split_refs(res[no:], nout)


def _chunks(width):
    return [(at, min(FFN_CHUNK, width - at)) for at in range(0, width, FFN_CHUNK)]


def _ffn_fwd_loss(x, gain, weights, final_gain, target):
    s, d = x.shape
    ffn = weights[0].shape[0] * weights[0].shape[1]
    tm = min(512, s)

    def body(x_ref, g_ref, b1, b3, b2, gf_ref, t_ref, dh_ref, a_ref, b_ref, hm_ref, dgf_ref, loss_ref, w1s, w3s, w2s, sems):
        @pl.when(pl.program_id(0) == 0)
        def _():
            for cp in _load_weights(((b1, w1s), (b3, w3s), (b2, w2s)), sems):
                cp.wait()
            dgf_ref[...] = jnp.zeros_like(dgf_ref)
            loss_ref[...] = jnp.zeros_like(loss_ref)

        xv = x_ref[...]
        r = lax.rsqrt(jnp.mean(xv * xv, axis=-1, keepdims=True) + EPS)
        n = (xv * r * g_ref[...]).astype(BF16)
        acc = jnp.zeros((tm, d), F32)
        for at, width in _chunks(ffn):
            cols = slice(at, at + width)
            a = _dot(n, w1s[cols, :], NT)
            b = _dot(n, w3s[cols, :], NT)
            a_ref[:, cols] = a.astype(BF16)
            b_ref[:, cols] = b.astype(BF16)
            hm = (a * _sigmoid(a) * b).astype(BF16)
            hm_ref[:, cols] = hm
            acc = acc + _dot(hm, w2s[cols, :], NN)
        h = xv + 0.5 * acc
        rf = lax.rsqrt(jnp.mean(h * h, axis=-1, keepdims=True) + EPS)
        nh = h * rf
        gf = gf_ref[...]
        err = nh * gf - t_ref[...]
        loss_ref[...] += jnp.sum(err * err, axis=0, keepdims=True) * (0.5 / d)
        dy = err * (1.0 / d)
        dgf_ref[...] += jnp.sum(dy * nh, axis=0, keepdims=True)
        dn = dy * gf
        dh_ref[...] = rf * (dn - nh * jnp.mean(dn * nh, axis=-1, keepdims=True))

    tile = pl.BlockSpec((tm, d), lambda i: (i, 0))
    row = pl.BlockSpec((1, d), lambda i: (0, 0))
    wide = pl.BlockSpec((tm, ffn), lambda i: (i, 0))
    return _call(
        body,
        name="ffn_fwd_loss",
        grid=(s // tm,),
        in_specs=[tile, row, ANY, ANY, ANY, row, tile],
        out_specs=[tile, wide, wide, wide, row, row],
        out_shape=[jax.ShapeDtypeStruct((s, d), F32)] + [jax.ShapeDtypeStruct((s, ffn), BF16)] * 3
        + [jax.ShapeDtypeStruct((1, d), F32)] * 2,
        scratch_shapes=[pltpu.VMEM((ffn, d), BF16)] * 3 + [pltpu.SemaphoreType.DMA((3 * NDEV,))],
        compiler_params=_seq(1),
    )(x, gain, *weights, final_gain, target)


def _ffn_up(x, gain, w1, w3, carries=()):
    s, d = x.shape
    ffn = w1.shape[0] * w1.shape[1]
    tm = min(512, s)

    def body(x_ref, g_ref, b1, b3, a_ref, b_ref, hm_ref, w1s, w3s, sems):
        @pl.when(pl.program_id(0) == 0)
        def _():
            for cp in _load_weights(((b1, w1s), (b3, w3s)), sems):
                cp.wait()

        xv = x_ref[...]
        r = lax.rsqrt(jnp.mean(xv * xv, axis=-1, keepdims=True) + EPS)
        n = (xv * r * g_ref[...]).astype(BF16)
        for at, width in _chunks(ffn):
            cols = slice(at, at + width)
            a = _dot(n, w1s[cols, :], NT)
            b = _dot(n, w3s[cols, :], NT)
            a_ref[:, cols] = a.astype(BF16)
            b_ref[:, cols] = b.astype(BF16)
            hm_ref[:, cols] = (a * _sigmoid(a) * b).astype(BF16)

    wide = pl.BlockSpec((tm, ffn), lambda i: (i, 0))
    return _grid_call(
        body,
        carries,
        name="ffn_up",
        grid=(s // tm,),
        in_specs=[pl.BlockSpec((tm, d), lambda i: (i, 0)), pl.BlockSpec((1, d), lambda i: (0, 0)), ANY, ANY],
        out_specs=[wide] * 3,
        out_shape=[jax.ShapeDtypeStruct((s, ffn), BF16)] * 3,
        scratch_shapes=[pltpu.VMEM((ffn, d), BF16)] * 2 + [pltpu.SemaphoreType.DMA((2 * NDEV,))],
        args=[x, gain, w1, w3],
    )


def _ffn_down(x, hm, w2, carries=()):
    s, d = x.shape
    ffn = w2.shape[0] * w2.shape[1]
    tm = min(512, s)

    def body(x_ref, hm_ref, b2, h_ref, w2s, sems):
        @pl.when(pl.program_id(0) == 0)
        def _():
            for cp in _load_weights(((b2, w2s),), sems):
                cp.wait()

        acc = jnp.zeros((tm, d), F32)
        for at, width in _chunks(ffn):
            cols = slice(at, at + width)
            acc = acc + _dot(hm_ref[:, cols], w2s[cols, :], NN)
        h_ref[...] = x_ref[...] + 0.5 * acc

    tile = pl.BlockSpec((tm, d), lambda i: (i, 0))
    return _grid_call(
        body,
        carries,
        name="ffn_down",
        grid=(s // tm,),
        in_specs=[tile, pl.BlockSpec((tm, ffn), lambda i: (i, 0)), ANY],
        out_specs=[tile],
        out_shape=[jax.ShapeDtypeStruct((s, d), F32)],
        scratch_shapes=[pltpu.VMEM((ffn, d), BF16), pltpu.SemaphoreType.DMA((NDEV,))],
        args=[x, hm, w2],
    )


def _ffn_bwd(dh, x, a, b, gain, weights, ffn, name, carries=()):
    s, d = x.shape
    tm = min(512, s)
    halves = 2
    fh = ffn // halves

    def body(dh_ref, x_ref, a_ref, b_ref, g_ref, b1, b3, b2, dx_ref, da_ref, db_ref, n_ref, dg_ref, w1s, w3s, w2s, sems):
        i, j = pl.program_id(0), pl.program_id(1)

        @pl.when((i == 0) & (j == 0))
        def _():
            for cp in _load_weights(((b1, w1s), (b3, w3s), (b2, w2s)), sems):
                cp.wait()
            dg_ref[...] = jnp.zeros_like(dg_ref)

        @pl.when(j == 0)
        def _():
            dx_ref[...] = jnp.zeros_like(dx_ref)

        dob = (0.5 * dh_ref[...]).astype(BF16)
        chunks = _chunks(fh)

        def dhm_of(k):
            at, width = chunks[k]
            return _dot(dob, w2s[pl.ds(pl.multiple_of(j * fh + at, GROUP), width), :], NT)

        ahead = dhm_of(0)
        for k, (at, width) in enumerate(chunks):
            cols = slice(at, at + width)
            dhm = ahead
            if k + 1 < len(chunks):
                ahead = dhm_of(k + 1)
            for top in range(0, tm, ROW_BAND):
                band = slice(top, top + ROW_BAND)
                av = a_ref[band, cols].astype(F32)
                bv = b_ref[band, cols].astype(F32)
                sg = _sigmoid(av)
                dv = dhm[band]
                da_ref[band, cols] = (dv * bv * (sg * (1.0 + av * (1.0 - sg)))).astype(BF16)
                db_ref[band, cols] = (dv * (av * sg)).astype(BF16)
        half = pl.ds(pl.multiple_of(j * fh, GROUP), fh)
        dx_ref[...] += _dot(da_ref[...], w1s[half, :], NN) + _dot(db_ref[...], w3s[half, :], NN)

        @pl.when(j == halves - 1)
        def _():
            xv = x_ref[...]
            g = g_ref[...]
            r = lax.rsqrt(jnp.mean(xv * xv, axis=-1, keepdims=True) + EPS)
            nh = xv * r
            n_ref[...] = (nh * g).astype(BF16)
            total = dx_ref[...]
            dg_ref[...] += jnp.sum(total * nh, axis=0, keepdims=True)
            dnh = total * g
            dx_ref[...] = dh_ref[...] + r * (dnh - nh * jnp.mean(dnh * nh, axis=-1, keepdims=True))

    tile = pl.BlockSpec((tm, d), lambda i, j: (i, 0))
    row = pl.BlockSpec((1, d), lambda i, j: (0, 0))
    wide = pl.BlockSpec((tm, fh), lambda i, j: (i, j))
    return _grid_call(
        body,
        carries,
        name=name,
        grid=(s // tm, halves),
        in_specs=[tile, tile, wide, wide, row, ANY, ANY, ANY],
        out_specs=[tile, wide, wide, tile, row],
        out_shape=[
            jax.ShapeDtypeStruct((s, d), F32),
            jax.ShapeDtypeStruct((s, ffn), BF16),
            jax.ShapeDtypeStruct((s, ffn), BF16),
            jax.ShapeDtypeStruct((s, d), BF16),
            jax.ShapeDtypeStruct((1, d), F32),
        ],
        scratch_shapes=[pltpu.VMEM((ffn, d), BF16)] * 3 + [pltpu.SemaphoreType.DMA((3 * NDEV,))],
        args=[dh, x, a, b, gain] + list(weights),
    )


SWAP_PIECES = 1
RING = 3


def _wgrad(lhs, rhs, scale, name, carries=(), scatter=False):
    s, m = lhs.shape
    n = rhs.shape[1]
    rs = m // NDEV
    tk = min(1024, s)
    steps = s // tk
    pieces = [(j, at, size) for j in range(2) for at, size in _pieces(rs, SWAP_PIECES)]

    def ring_of(refs):
        return refs[:2] + refs[3 + scatter:6 + scatter]

    def loads_of(t, l_hbm, r_hbm, l_buf, r_buf, load_sems):
        if isinstance(t, int):
            slot, at, half = t % RING, (t % steps) * tk, (t // steps) * (m // 2)
        else:
            slot = lax.rem(t, RING)
            at = pl.multiple_of(lax.rem(t, steps) * tk, tk)
            half = pl.multiple_of(lax.div(t, steps) * (m // 2), GROUP)
        return (
            pltpu.make_async_copy(l_hbm.at[pl.ds(at, tk), pl.ds(half, m // 2)], l_buf.at[slot], load_sems.at[0, slot]),
            pltpu.make_async_copy(r_hbm.at[pl.ds(at, tk), :], r_buf.at[slot], load_sems.at[1, slot]),
        )

    def first_loads(*refs):
        for t in range(min(RING - 1, 2 * steps)):
            for cp in loads_of(t, *ring_of(refs)):
                cp.start()

    def body(*refs):
        o_ref = refs[2]
        far_ref = refs[3] if scatter else None
        l_buf, r_buf, _, acc, mine, theirs, send_sems, recv_sems, *far_sems = refs[3 + scatter:]
        h, k = pl.program_id(0), pl.program_id(1)
        step = h * steps + k
        loads = lambda t: loads_of(t, *ring_of(refs))

        @pl.when(step + RING - 1 < 2 * steps)
        def _():
            for cp in loads(step + RING - 1):
                cp.start()

        for cp in loads(step):
            cp.wait()
        slot = lax.rem(step, RING)

        @pl.when(k == 0)
        def _():
            acc[...] = _dot(l_buf[slot], r_buf[slot].astype(BF16), TN)

        @pl.when(k > 0)
        def _():
            acc[...] += _dot(l_buf[slot], r_buf[slot].astype(BF16), TN)

        def exchange(half):
            c = lax.axis_index("c")
            return [
                _remote(mine.at[half, 1 - c, j, pl.ds(at, size), :], theirs.at[half, j, pl.ds(at, size), :],
                        send_sems.at[half * len(pieces) + q], recv_sems.at[half * len(pieces) + q], _peer(1))
                for q, (j, at, size) in enumerate(pieces)
            ]

        def settle(half):
            for cp in exchange(half):
                cp.wait_recv()
            both = mine[half, lax.axis_index("c")].astype(F32) + theirs[half].astype(F32)
            o_ref[2 * half:2 * half + 2] = both.astype(BF16)
            for cp in exchange(half):
                cp.wait_send()

        for half in range(2):
            @pl.when((h == half) & (k == steps - 1))
            def _():
                for p in range(NCHIP):
                    mine[half, p % 2, p // 2] = (acc[p * rs:(p + 1) * rs, :] * scale).astype(BF16)
                for cp in exchange(half):
                    cp.start()
                if half == 1:
                    settle(0)
                    settle(1)
                    if scatter:
                        onward = [
                            _remote(o_ref.at[_chip(_peer(code))], far_ref.at[slot], far_sems[0].at[slot],
                                    far_sems[1].at[slot], _peer(code))
                            for slot, code in enumerate(_ChipScatter.across)
                        ]
                        for cp in onward:
                            cp.start()
                        for cp in onward:
                            cp.wait_recv()
                            cp.wait_send()

    far = len(_ChipScatter.across)
    outs, carried = _grid_call(
        body,
        carries,
        name=name,
        grid=(2, steps),
        in_specs=[ANY, ANY],
        out_specs=[pl.BlockSpec((NCHIP, rs, n), lambda h, k: (0, 0, 0))] + [ANY] * scatter,
        out_shape=[jax.ShapeDtypeStruct((NCHIP, rs, n), BF16)] + [jax.ShapeDtypeStruct((far, rs, n), BF16)] * scatter,
        scratch_shapes=[
            pltpu.VMEM((RING, tk, m // 2), lhs.dtype), pltpu.VMEM((RING, tk, n), rhs.dtype),
            pltpu.SemaphoreType.DMA((2, RING)),
            pltpu.VMEM((m // 2, n), F32), pltpu.VMEM((2, 2, 2, rs, n), BF16), pltpu.VMEM((2, 2, rs, n), BF16),
            pltpu.SemaphoreType.DMA((2 * len(pieces),)), pltpu.SemaphoreType.DMA((2 * len(pieces),)),
        ] + [pltpu.SemaphoreType.DMA((far,))] * (2 * scatter),
        args=[lhs, rhs],
        own_peers=(1,) + (_ChipScatter.peers if scatter else ()),
        prelude=first_loads,
    )
    return (tuple(outs) if scatter else outs[0]), carried


def _mix_constants(s):
    c = GROUP
    lg = np.log1p(-np.exp2(-5.0 - np.arange(RET_HEADS, dtype=np.float32))).astype(np.float32)
    pos = np.arange(c, dtype=np.float32)
    rel = pos[:, None] - pos[None, :]
    decay = np.where(rel[None] >= 0, np.exp(lg[:, None, None] * np.maximum(rel, 0.0)[None]), 0.0).astype(np.float32)
    ktail = np.exp(lg[:, None] * (c - 1 - pos)[None, :]).astype(np.float32)
    qhead = np.exp(lg[:, None] * (pos + 1.0)[None, :]).astype(np.float32)
    chunk_decay = [float(v) for v in np.exp(lg * np.float32(c)).astype(np.float32)]
    ones = np.ones((1, 1, c), np.float32)
    inv_freq = (1.0 / (np.float32(ROPE_BASE) ** (np.arange(0, c, 2, dtype=np.float32) / np.float32(c)))).astype(np.float32)
    ang = (np.arange(s, dtype=np.float32)[:, None] * inv_freq[None, :]).astype(np.float32)
    cos, sin = np.cos(ang).astype(np.float32), np.sin(ang).astype(np.float32)
    return dict(
        decay=jnp.asarray(decay),
        ktail=jnp.asarray(ktail[:, :, None] * ones),
        qhead=jnp.asarray(qhead[:, :, None] * ones),
        chunk_decay=chunk_decay,
        cos=jnp.asarray(np.concatenate([cos, cos], axis=-1)),
        sin=jnp.asarray(np.concatenate([-sin, sin], axis=-1)),
    )


def _rope(t, cos, sin):
    return t * cos + pltpu.roll(t, GROUP // 2, axis=1) * sin


def _rope_bwd(dt, cos, sin):
    return dt * cos + pltpu.roll(dt * sin, GROUP // 2, axis=1)


def _window_sums(ext, w, forward):
    rows = ext.shape[0]
    acc, k = ext, 1
    while k < w:
        acc = acc + pltpu.roll(acc, k if forward else rows - k, axis=0)
        k *= 2
    return acc


def _pool_counts(tile, tm, w):
    t = lax.broadcasted_iota(jnp.int32, (tm, 1), 0) + tile * tm
    return jnp.minimum(t + 1, w).astype(F32)


def _mix_fwd(h1, gain, weights, pool_w, pool_scale, ret_gain, consts, carries=()):
    s, d = h1.shape
    pwid = N_POOL_GROUPS * GROUP
    rwid = RET_HEADS * GROUP
    inw = pwid + 4 * rwid
    tm = min(256, s)
    nck = tm // GROUP
    cd = consts["chunk_decay"]

    def body(h_ref, g_ref, bin_, bout, pw_ref, ps_ref, rg_ref, cos_ref, sin_ref, dec_ref, kt_ref, qh_ref,
             h2_ref, proj_ref, o_ref, rs_ref, wins, wouts, state, carry, mbuf, sems):
        i = pl.program_id(0)

        @pl.when(i == 0)
        def _():
            for cp in _load_weights(((bin_, wins), (bout, wouts)), sems):
                cp.wait()
            state[...] = jnp.zeros_like(state)
            carry[...] = jnp.zeros_like(carry)

        hv = h_ref[...]
        r = lax.rsqrt(jnp.mean(hv * hv, axis=-1, keepdims=True) + EPS)
        u = (hv * r * g_ref[...]).astype(BF16)
        proj_ref[...] = _dot(u, wins[...], NT)

        ext = jnp.concatenate([carry[...], proj_ref[:, 0:pwid]], axis=0)
        carry[...] = proj_ref[tm - MAX_WINDOW:tm, 0:pwid]
        for gi, w in enumerate(POOL_WINDOWS):
            cols = slice(gi * GROUP, (gi + 1) * GROUP)
            xg = ext[:, cols]
            ws = _window_sums(xg, w, True)[MAX_WINDOW:, :]
            pooled = ws / _pool_counts(i, tm, w) - xg[MAX_WINDOW:, :]
            z = _dot(pooled.astype(BF16), pw_ref[gi].astype(BF16), NN)
            mbuf[:, cols] = (z * ps_ref[:, cols]).astype(BF16)

        cos, sin = cos_ref[...], sin_ref[...]
        for h in range(RET_HEADS):
            cq = slice(pwid + h * GROUP, pwid + (h + 1) * GROUP)
            ck = slice(pwid + rwid + h * GROUP, pwid + rwid + (h + 1) * GROUP)
            cv = slice(pwid + 2 * rwid + h * GROUP, pwid + 2 * rwid + (h + 1) * GROUP)
            cg = slice(pwid + 3 * rwid + h * GROUP, pwid + 3 * rwid + (h + 1) * GROUP)
            ch = slice(h * GROUP, (h + 1) * GROUP)
            qr = _rope(proj_ref[:, cq], cos, sin)
            kr = _rope(proj_ref[:, ck], cos, sin) * (GROUP ** -0.5)
            vb = proj_ref[:, cv].astype(BF16)
            for n in range(nck):
                rows = slice(n * GROUP, (n + 1) * GROUP)
                qc, kc, vc = qr[rows], kr[rows], vb[rows]
                rb = state[h]
                rs_ref[n, h] = rb
                p = (_dot(qc.astype(BF16), kc.astype(BF16), NT) * dec_ref[h]).astype(BF16)
                o = _dot(p, vc, NN) + _dot((qc * qh_ref[h]).astype(BF16), rb.astype(BF16), NN)
                state[h] = cd[h] * rb + _dot((kc * kt_ref[h]).astype(BF16), vc, TN)
                o_ref[rows, ch] = o
                on = o * lax.rsqrt(jnp.mean(o * o, axis=-1, keepdims=True) + EPS)
                gv = proj_ref[rows, cg]
                mbuf[rows, pwid + h * GROUP:pwid + (h + 1) * GROUP] = (
                    gv * _sigmoid(gv) * (on * rg_ref[:, ch])
                ).astype(BF16)
        h2_ref[...] = hv + _dot(mbuf[...], wouts[...], NN)

    tile = pl.BlockSpec((tm, d), lambda i: (i, 0))
    full = lambda shape: pl.BlockSpec(shape, lambda i: (0,) * len(shape))
    return _grid_call(
        body,
        carries,
        name="mix_fwd",
        work_first=True,
        grid=(s // tm,),
        in_specs=[
            tile, full((1, d)), ANY, ANY,
            full((N_POOL_GROUPS, GROUP, GROUP)), full((1, pwid)), full((1, rwid)),
            pl.BlockSpec((tm, GROUP), lambda i: (i, 0)), pl.BlockSpec((tm, GROUP), lambda i: (i, 0)),
            full((RET_HEADS, GROUP, GROUP)), full((RET_HEADS, GROUP, GROUP)), full((RET_HEADS, GROUP, GROUP)),
        ],
        out_specs=[
            tile,
            pl.BlockSpec((tm, inw), lambda i: (i, 0)),
            pl.BlockSpec((tm, rwid), lambda i: (i, 0)),
            pl.BlockSpec((nck, RET_HEADS, GROUP, GROUP), lambda i: (i, 0, 0, 0)),
        ],
        out_shape=[
            jax.ShapeDtypeStruct((s, d), F32),
            jax.ShapeDtypeStruct((s, inw), F32),
            jax.ShapeDtypeStruct((s, rwid), F32),
            jax.ShapeDtypeStruct((s // GROUP, RET_HEADS, GROUP, GROUP), F32),
        ],
        scratch_shapes=[
            pltpu.VMEM((inw, d), BF16), pltpu.VMEM((d, d), BF16),
            pltpu.VMEM((RET_HEADS, GROUP, GROUP), F32), pltpu.VMEM((MAX_WINDOW, pwid), F32),
            pltpu.VMEM((tm, d), BF16), pltpu.SemaphoreType.DMA((2 * NDEV,)),
        ],
        args=[h1, gain, weights[0], weights[1], pool_w, pool_scale, ret_gain,
              consts["cos"], consts["sin"], consts["decay"], consts["ktail"], consts["qhead"]],
    )


def _mix_bwd(dh2, h1, proj, o_saved, rsave, gain, weights, pool_w, pool_scale, ret_gain, consts, carries=()):
    s, d = h1.shape
    pwid = N_POOL_GROUPS * GROUP
    rwid = RET_HEADS * GROUP
    inw = pwid + 4 * rwid
    tm = min(256, s)
    nck = tm // GROUP
    nt = s // tm
    cd = consts["chunk_decay"]
    halo_per_tile = tm // MAX_WINDOW

    def body(dh2_ref, h_ref, proj_ref, halo_ref, o_ref, rs_ref, g_ref, bin_, bout, pw_ref, ps_ref, rg_ref,
             cos_ref, sin_ref, dec_ref, kt_ref, qh_ref,
             dh1_ref, dproj_ref, u_ref, m_ref, dpw_ref, dps_ref, drg_ref, dg_ref,
             wins, wouts, dstate, carry, dm, dpj, sems):
        i = pl.program_id(0)
        tile = nt - 1 - i

        @pl.when(i == 0)
        def _():
            for cp in _load_weights(((bin_, wins), (bout, wouts)), sems):
                cp.wait()
            dstate[...] = jnp.zeros_like(dstate)
            carry[...] = jnp.zeros_like(carry)
            for ref in (dpw_ref, dps_ref, drg_ref, dg_ref):
                ref[...] = jnp.zeros_like(ref)

        dh2v = dh2_ref[...]
        dm[...] = _dot(dh2v.astype(BF16), wouts[...], NT)
        hv = h_ref[...]
        g = g_ref[...]
        r = lax.rsqrt(jnp.mean(hv * hv, axis=-1, keepdims=True) + EPS)
        uh = hv * r
        u_ref[...] = (uh * g).astype(BF16)

        halo = jnp.where(tile == 0, 0.0, halo_ref[...])
        ext = jnp.concatenate([halo, proj_ref[:, 0:pwid]], axis=0)
        next_dpn = carry[...]
        for gi, w in enumerate(POOL_WINDOWS):
            cols = slice(gi * GROUP, (gi + 1) * GROUP)
            xg = ext[:, cols]
            cnt = _pool_counts(tile, tm, w)
            pooled = (_window_sums(xg, w, True)[MAX_WINDOW:, :] / cnt - xg[MAX_WINDOW:, :]).astype(BF16)
            pwb = pw_ref[gi].astype(BF16)
            z = _dot(pooled, pwb, NN)
            scale = ps_ref[:, cols]
            m_ref[:, cols] = (z * scale).astype(BF16)
            da = dm[:, cols]
            dps_ref[:, cols] += jnp.sum(da * z, axis=0, keepdims=True)
            dz = (da * scale).astype(BF16)
            dpw_ref[gi] += _dot(pooled, dz, TN)
            dpl = _dot(dz, pwb, NT)
            dpn = dpl / cnt
            ext2 = jnp.concatenate([dpn, next_dpn[:, cols]], axis=0)
            dpj[:, cols] = (_window_sums(ext2, w, False)[0:tm, :] - dpl).astype(BF16)
            carry[:, cols] = dpn[0:MAX_WINDOW, :]

        cos, sin = cos_ref[...], sin_ref[...]
        for h in range(RET_HEADS):
            cq = slice(pwid + h * GROUP, pwid + (h + 1) * GROUP)
            ck = slice(pwid + rwid + h * GROUP, pwid + rwid + (h + 1) * GROUP)
            cv = slice(pwid + 2 * rwid + h * GROUP, pwid + 2 * rwid + (h + 1) * GROUP)
            cg = slice(pwid + 3 * rwid + h * GROUP, pwid + 3 * rwid + (h + 1) * GROUP)
            ch = slice(h * GROUP, (h + 1) * GROUP)
            qr = _rope(proj_ref[:, cq], cos, sin)
            kr = _rope(proj_ref[:, ck], cos, sin) * (GROUP ** -0.5)
            vb = proj_ref[:, cv].astype(BF16)
            gv = proj_ref[:, cg]
            ov = o_ref[:, ch]
            ro = lax.rsqrt(jnp.mean(ov * ov, axis=-1, keepdims=True) + EPS)
            on = ov * ro
            rg = rg_ref[:, ch]
            db = dm[:, pwid + h * GROUP:pwid + (h + 1) * GROUP]
            sg = _sigmoid(gv)
            sl = gv * sg
            m_ref[:, pwid + h * GROUP:pwid + (h + 1) * GROUP] = (sl * (on * rg)).astype(BF16)
            dpj[:, cg] = (db * (on * rg) * (sg * (1.0 + gv * (1.0 - sg)))).astype(BF16)
            drg_ref[:, ch] += jnp.sum(db * sl * on, axis=0, keepdims=True)
            don = db * sl * rg
            do = (ro * (don - on * jnp.mean(don * on, axis=-1, keepdims=True))).astype(BF16)
            for n in reversed(range(nck)):
                rows = slice(n * GROUP, (n + 1) * GROUP)
                qc, kc, vc, dob = qr[rows], kr[rows], vb[rows], do[rows]
                qcb, kcb = qc.astype(BF16), kc.astype(BF16)
                qh = (qc * qh_ref[h]).astype(BF16)
                kt = (kc * kt_ref[h]).astype(BF16)
                rn = rs_ref[n, h].astype(BF16)
                dnext = dstate[h]
                dnb = dnext.astype(BF16)
                dec = dec_ref[h]
                p = (_dot(qcb, kcb, NT) * dec).astype(BF16)
                ds = (_dot(dob, vc, NT) * dec).astype(BF16)
                dv = _dot(p, dob, TN) + _dot(kt, dnb, NN)
                dq = _dot(ds, kcb, NN) + _dot(dob, rn, NT) * qh_ref[h]
                dk = _dot(ds, qcb, TN) + _dot(vc, dnb, NT) * kt_ref[h]
                dstate[h] = cd[h] * dnext + _dot(qh, dob, TN)
                dpj[rows, cq] = _rope_bwd(dq, cos[rows], sin[rows]).astype(BF16)
                dpj[rows, ck] = _rope_bwd(dk * (GROUP ** -0.5), cos[rows], sin[rows]).astype(BF16)
                dpj[rows, cv] = dv.astype(BF16)

        dproj_ref[...] = dpj[...]
        du = _dot(dpj[...], wins[...], NN)
        dg_ref[...] += jnp.sum(du * uh, axis=0, keepdims=True)
        dn = du * g
        dh1_ref[...] = dh2v + r * (dn - uh * jnp.mean(dn * uh, axis=-1, keepdims=True))

    rev = lambda i: (nt - 1 - i, 0)
    tile = pl.BlockSpec((tm, d), rev)
    full = lambda shape: pl.BlockSpec(shape, lambda i: (0,) * len(shape))
    return _grid_call(
        body,
        carries,
        name="mix_bwd",
        work_first=True,
        grid=(nt,),
        in_specs=[
            tile, tile,
            pl.BlockSpec((tm, inw), rev),
            pl.BlockSpec((MAX_WINDOW, pwid), lambda i: (jnp.maximum((nt - 1 - i) * halo_per_tile - 1, 0), 0)),
            pl.BlockSpec((tm, rwid), rev),
            pl.BlockSpec((nck, RET_HEADS, GROUP, GROUP), lambda i: (nt - 1 - i, 0, 0, 0)),
            full((1, d)), ANY, ANY,
            full((N_POOL_GROUPS, GROUP, GROUP)), full((1, pwid)), full((1, rwid)),
            pl.BlockSpec((tm, GROUP), rev), pl.BlockSpec((tm, GROUP), rev),
            full((RET_HEADS, GROUP, GROUP)), full((RET_HEADS, GROUP, GROUP)), full((RET_HEADS, GROUP, GROUP)),
        ],
        out_specs=[
            tile, pl.BlockSpec((tm, inw), rev), tile, tile,
            full((N_POOL_GROUPS, GROUP, GROUP)), full((1, pwid)), full((1, rwid)), full((1, d)),
        ],
        out_shape=[
            jax.ShapeDtypeStruct((s, d), F32),
            jax.ShapeDtypeStruct((s, inw), BF16),
            jax.ShapeDtypeStruct((s, d), BF16),
            jax.ShapeDtypeStruct((s, d), BF16),
            jax.ShapeDtypeStruct((N_POOL_GROUPS, GROUP, GROUP), F32),
            jax.ShapeDtypeStruct((1, pwid), F32),
            jax.ShapeDtypeStruct((1, rwid), F32),
            jax.ShapeDtypeStruct((1, d), F32),
        ],
        scratch_shapes=[
            pltpu.VMEM((inw, d), BF16), pltpu.VMEM((d, d), BF16),
            pltpu.VMEM((RET_HEADS, GROUP, GROUP), F32), pltpu.VMEM((MAX_WINDOW, pwid), F32),
            pltpu.VMEM((tm, d), F32), pltpu.VMEM((tm, inw), BF16), pltpu.SemaphoreType.DMA((2 * NDEV,)),
        ],
        args=[dh2, h1, proj, proj, o_saved, rsave, gain, weights[0], weights[1], pool_w, pool_scale, ret_gain,
              consts["cos"], consts["sin"], consts["decay"], consts["ktail"], consts["qhead"]],
    )


def _adam(w, g, m, v):
    m = ADAM_B1 * m + (1.0 - ADAM_B1) * g
    v = ADAM_B2 * v + (1.0 - ADAM_B2) * jnp.square(g)
    m_hat = m / (1.0 - ADAM_B1 ** ADAM_STEP)
    v_hat = v / (1.0 - ADAM_B2 ** ADAM_STEP)
    delta = -ADAM_LR * (m_hat / (jnp.sqrt(v_hat) + ADAM_EPS) + ADAM_WD * w)
    return delta, m, v


def _adamw_big(my_chip, w, sums, parts, m, v, name):
    rows, d = w.shape
    tr = _row_tile(rows, 176)

    def body(chip_ref, w_ref, own_ref, p_ref, m_ref, v_ref, g_ref, d_ref, nm_ref, nv_ref):
        g = own_ref[0].astype(F32)
        for q in range(NCHIP - 1):
            g = g + p_ref[q].astype(F32)
        g_ref[...] = g
        d_ref[...], nm_ref[...], nv_ref[...] = _adam(w_ref[...], g, m_ref[...], v_ref[...])

    spec = pl.BlockSpec((tr, d), lambda i, chip: (i, 0))
    return _call(
        body,
        name=name,
        grid_spec=pltpu.PrefetchScalarGridSpec(
            num_scalar_prefetch=1,
            grid=(rows // tr,),
            in_specs=[
                spec,
                pl.BlockSpec((1, tr, d), lambda i, chip: (chip[0], i, 0)),
                pl.BlockSpec((NCHIP - 1, tr, d), lambda i, chip: (0, i, 0)),
                spec,
                spec,
            ],
            out_specs=[spec] * 4,
        ),
        out_shape=[jax.ShapeDtypeStruct((rows, d), F32)] * 4,
        compiler_params=_seq(1),
    )(my_chip, w, sums, parts, m, v)


def _adamw_small(stats_all, pw_all, ws, ms, vs, pwid):
    nsmall = len(ws)

    def body(*refs):
        st_ref, pwa_ref = refs[0], refs[1]
        w_refs = refs[2:2 + nsmall]
        m_refs = refs[2 + nsmall:2 + 2 * nsmall]
        v_refs = refs[2 + 2 * nsmall:2 + 3 * nsmall]
        outs = refs[2 + 3 * nsmall:]
        st = st_ref[0]
        pwg = pwa_ref[0]
        for q in range(1, NDEV):
            st = st + st_ref[q]
            pwg = pwg + pwa_ref[q]
        grads = [st[0:1, :], st[1:2, :], st[2:3, :], st[3:4, :], st[4:5, 0:pwid], st[4:5, pwid:2 * pwid], pwg]
        outs[0][...] = jnp.zeros((1, GROUP), F32) + jnp.sum(st[5:6, :])
        for j in range(nsmall):
            delta, nm, nv = _adam(w_refs[j][...], grads[j], m_refs[j][...], v_refs[j][...])
            outs[1 + 4 * j][...] = grads[j]
            outs[2 + 4 * j][...] = delta
            outs[3 + 4 * j][...] = nm
            outs[4 + 4 * j][...] = nv

    out_shape = [jax.ShapeDtypeStruct((1, GROUP), F32)]
    for w in ws:
        out_shape += [jax.ShapeDtypeStruct(w.shape, F32)] * 4
    return _call(body, name="adamw_small", out_shape=out_shape, compiler_params=_params())(
        stats_all, pw_all, *ws, *ms, *vs
    )


def kernel(x, ffn1_norm, ffn1_w1, ffn1_w3, ffn1_w2, mix_norm, w_in, pool_w, pool_scale, ret_norm, w_out, ffn2_norm, ffn2_w1, ffn2_w3, ffn2_w2, final_norm, loss_target, m_ffn1_norm, m_ffn1_w1, m_ffn1_w3, m_ffn1_w2, m_mix_norm, m_w_in, m_pool_w, m_pool_scale, m_ret_norm, m_w_out, m_ffn2_norm, m_ffn2_w1, m_ffn2_w3, m_ffn2_w2, m_final_norm, v_ffn1_norm, v_ffn1_w1, v_ffn1_w3, v_ffn1_w2, v_mix_norm, v_w_in, v_pool_w, v_pool_scale, v_ret_norm, v_w_out, v_ffn2_norm, v_ffn2_w1, v_ffn2_w3, v_ffn2_w2, v_final_norm):
    s, d = x.shape[1], x.shape[2]
    ffn = ffn1_w1.shape[2] * NDEV
    pwid = pool_scale.shape[1]
    xs, tgt = x[0], loss_target[0]
    consts = _mix_constants(s)
    pw3 = pool_w[0]
    fnorm = final_norm.reshape(1, d)

    rows_of = lambda w, transposed: w[0].T if transposed else w[0]
    send_f1 = [rows_of(w, t).astype(BF16) for w, t in ((ffn1_w1, True), (ffn1_w3, True), (ffn1_w2, False))]
    later = [rows_of(w, t) for w, t in ((w_in, True), (w_out, False), (ffn2_w1, True), (ffn2_w3, True), (ffn2_w2, False))]

    sent_later, w13_f1 = _comm_call([_CastRows(later), _Gather(send_f1[:2])], "gather_ffn1")
    send_mix, send_f2 = sent_later[:2], sent_later[2:]
    (a1, b1, hm1), ((w2_f1, w_in_all),) = _ffn_up(xs, ffn1_norm, *w13_f1, carries=[_Gather(send_f1[2:] + send_mix[:1])])
    w_f1 = w13_f1 + [w2_f1]
    (h1,), ((w_out_all, w1_f2),) = _ffn_down(xs, hm1, w2_f1, carries=[_Gather(send_mix[1:] + send_f2[:1])])
    w_mix = [w_in_all, w_out_all]
    (h2, proj, o_saved, rsave), (rest,) = _mix_fwd(
        h1, mix_norm, w_mix, pw3, pool_scale, ret_norm, consts, carries=[_Gather(send_f2[1:])]
    )
    w_f2 = [w1_f2] + rest
    dh3, a2, b2, hm2, dgf, loss_cols = _ffn_fwd_loss(h2, ffn2_norm, w_f2, fnorm, tgt)

    (dh2, da2, db2, n2, dg2), _ = _ffn_bwd(dh3, h2, a2, b2, ffn2_norm, w_f2, ffn, "ffn2_bwd")
    sum_f2w1, _ = _wgrad(da2, n2, 1.0, "ffn2_w1_grad")
    sum_f2w3, _ = _wgrad(db2, n2, 1.0, "ffn2_w3_grad")
    sum_f2w2, ((parts_f2w1,),) = _wgrad(hm2, dh3, 0.5, "ffn2_w2_grad", carries=[_ChipScatter([sum_f2w1])])

    (dh1, dproj, u, mm, dpw, dps, drg, dgm), ((parts_f2w3, parts_f2w2),) = _mix_bwd(
        dh2, h1, proj, o_saved, rsave, mix_norm, w_mix, pw3, pool_scale, ret_norm, consts,
        carries=[_ChipScatter([sum_f2w3, sum_f2w2])],
    )
    (dx, da1, db1, n1, dg1), _ = _ffn_bwd(dh1, xs, a1, b1, ffn1_norm, w_f1, ffn, "ffn1_bwd")
    stats = jnp.concatenate(
        [dg1, dgm, dg2, dgf, jnp.concatenate([dps, drg], axis=1), loss_cols, jnp.zeros((2, d), F32)], axis=0
    )
    small = _GatherDirect([stats, dpw.reshape(N_POOL_GROUPS * GROUP, GROUP)])
    sum_f1w2, ((stats_all, pw_all),) = _wgrad(hm1, dh1, 0.5, "ffn1_w2_grad", carries=[small])
    sum_f1w1, ((parts_f1w2,),) = _wgrad(da1, n1, 1.0, "ffn1_w1_grad", carries=[_ChipScatter([sum_f1w2])])
    sum_f1w3, ((parts_f1w1,),) = _wgrad(db1, n1, 1.0, "ffn1_w3_grad", carries=[_ChipScatter([sum_f1w1])])
    sum_in, ((parts_f1w3,),) = _wgrad(dproj, u, 1.0, "w_in_grad", carries=[_ChipScatter([sum_f1w3])])
    (sum_out, parts_out), ((parts_in,),) = _wgrad(
        mm, dh2, 1.0, "w_out_grad", carries=[_ChipScatter([sum_in])], scatter=True
    )

    big = (
        (ffn1_w1, m_ffn1_w1, v_ffn1_w1, sum_f1w1, parts_f1w1, True),
        (ffn1_w3, m_ffn1_w3, v_ffn1_w3, sum_f1w3, parts_f1w3, True),
        (ffn1_w2, m_ffn1_w2, v_ffn1_w2, sum_f1w2, parts_f1w2, False),
        (w_in, m_w_in, v_w_in, sum_in, parts_in, True),
        (w_out, m_w_out, v_w_out, sum_out, parts_out, False),
        (ffn2_w1, m_ffn2_w1, v_ffn2_w1, sum_f2w1, parts_f2w1, True),
        (ffn2_w3, m_ffn2_w3, v_ffn2_w3, sum_f2w3, parts_f2w3, True),
        (ffn2_w2, m_ffn2_w2, v_ffn2_w2, sum_f2w2, parts_f2w2, False),
    )
    my_chip = jnp.reshape(_chip(_peer(0)), (1,)).astype(jnp.int32)
    big_out = []
    for j, (w, m, v, sums, parts, t) in enumerate(big):
        view = (lambda a: a[0].T) if t else (lambda a: a[0])
        back = (lambda a: a.T[None]) if t else (lambda a: a[None])
        big_out.append([back(a) for a in _adamw_big(my_chip, view(w), sums, parts, view(m), view(v), "adamw_%d" % j)])

    small_w = (ffn1_norm, mix_norm, ffn2_norm, fnorm, pool_scale, ret_norm, pw3.reshape(-1, GROUP))
    small_m = (m_ffn1_norm, m_mix_norm, m_ffn2_norm, m_final_norm.reshape(1, d), m_pool_scale, m_ret_norm, m_pool_w.reshape(-1, GROUP))
    small_v = (v_ffn1_norm, v_mix_norm, v_ffn2_norm, v_final_norm.reshape(1, d), v_pool_scale, v_ret_norm, v_pool_w.reshape(-1, GROUP))
    res = _adamw_small(stats_all, pw_all, small_w, small_m, small_v, pwid)
    loss = res[0][0, 0]
    small_out = [list(res[1 + 4 * j:5 + 4 * j]) for j in range(len(small_w))]
    small_out[3] = [a.reshape(d) for a in small_out[3]]
    small_out[6] = [a.reshape(pool_w.shape) for a in small_out[6]]

    order = [small_out[0], big_out[0], big_out[1], big_out[2], small_out[1], big_out[3], small_out[6], small_out[4],
             small_out[5], big_out[4], small_out[2], big_out[5], big_out[6], big_out[7], small_out[3]]
    result = [loss, dx[None]]
    for kind in range(4):
        result += [t[kind] for t in order]
    return tuple(result)
```

```python
import functools

import numpy as np
import jax
import jax.numpy as jnp
from jax import lax
from jax.experimental import pallas as pl
from jax.experimental.pallas import tpu as pltpu

F32 = jnp.float32
BF16 = jnp.bfloat16

NDEV = 8
NCHIP = 4
EPS = 1e-6
N_POOL_GROUPS = 4
POOL_WINDOWS = (2, 4, 8, 16)
MAX_WINDOW = 16
GROUP = 128
RET_HEADS = 4
ROPE_BASE = 10000.0
ADAM_LR = 0.001
ADAM_B1 = 0.9
ADAM_B2 = 0.999
ADAM_EPS = 1e-08
ADAM_WD = 0.01
ADAM_STEP = 10

VMEM_LIMIT = 56 * 1024 * 1024
FFN_CHUNK = 256
ROW_BAND = 32

NT = (((1,), (1,)), ((), ()))
NN = (((1,), (0,)), ((), ()))
TN = (((0,), (0,)), ((), ()))

ANY = pl.BlockSpec(memory_space=pl.ANY)


def _dot(a, b, dims):
    return lax.dot_general(a, b, dims, preferred_element_type=F32)


def _call(body, **kw):
    return pl.pallas_call(body, **kw)


def _params(**kw):
    return pltpu.CompilerParams(vmem_limit_bytes=VMEM_LIMIT, **kw)


def _seq(n):
    return _params(dimension_semantics=("arbitrary",) * n)


def _peer(k):
    x, y, c = lax.axis_index("x"), lax.axis_index("y"), lax.axis_index("c")
    return (1 - x if k & 4 else x, 1 - y if k & 2 else y, 1 - c if k & 1 else c)


def _flat(pos):
    return 4 * pos[0] + 2 * pos[1] + pos[2]


def _chip(pos):
    return 2 * pos[0] + pos[1]


def _row_tile(rows, cap):
    return max(t for t in range(16, min(rows, cap) + 1, 16) if rows % t == 0)


def _pieces(rows, n):
    tiles = rows // 16
    cuts = [16 * (tiles * q // n) for q in range(n + 1)]
    return [(a, b - a) for a, b in zip(cuts[:-1], cuts[1:])]


def _load_weights(parts, sems):
    copies = []
    for buf, dst in parts:
        rows = buf.shape[1]
        for p in range(NDEV):
            cp = pltpu.make_async_copy(buf.at[p], dst.at[pl.ds(p * rows, rows), :], sems.at[len(copies)])
            cp.start()
            copies.append(cp)
    return copies


def _sigmoid(a):
    return 1.0 / (1.0 + jnp.exp(-a))


def _remote(src, dst, send_sem, recv_sem, to):
    return pltpu.make_async_remote_copy(
        src_ref=src, dst_ref=dst, send_sem=send_sem, recv_sem=recv_sem, device_id=to, device_id_type=pl.DeviceIdType.MESH
    )


class _Gather:
    X, Y, FAR = 4, 2, 6
    peers = (1, 2, 4)
    COPIES = 8

    def __init__(self, shards):
        n = len(shards)
        self.operands = list(shards)
        self.out_shape = [jax.ShapeDtypeStruct((NDEV,) + a.shape, a.dtype) for a in shards]
        self.sems = [
            pltpu.SemaphoreType.DMA((self.COPIES * n,)), pltpu.SemaphoreType.DMA((self.COPIES * n,)),
            pltpu.SemaphoreType.DMA((n,)),
        ]
        self.stages = [self.begin, self.relay, self.relay_far, self.end]

    def _copy(self, t, k, block, to, ins, outs, sems, own=False, half=None):
        rows = outs[t].shape[1]
        part = pl.ds(0, rows) if half is None else pl.ds(half * (rows // 2), rows // 2)
        dst = outs[t].at[_flat(block), part, :]
        at = self.COPIES * t + k
        return _remote(ins[t] if own else dst, dst, sems[0].at[at], sems[1].at[at], to)

    def _local(self, t, ins, outs, sems):
        return pltpu.make_async_copy(ins[t], outs[t].at[_flat(_peer(0))], sems[2].at[t])

    def begin(self, ins, outs, sems):
        me = _peer(0)
        for t in range(len(ins)):
            self._local(t, ins, outs, sems).start()
            for k, code in enumerate((1, self.X, self.Y)):
                self._copy(t, k, me, _peer(code), ins, outs, sems, own=True).start()

    def relay(self, ins, outs, sems):
        me, sibling = _peer(0), _peer(1)
        for t in range(len(ins)):
            self._copy(t, 1, _peer(self.X), me, ins, outs, sems).wait_recv()
            self._copy(t, 3, _peer(self.X), _peer(self.Y), ins, outs, sems, half=0).start()
            self._copy(t, 5, _peer(self.X), sibling, ins, outs, sems).start()
            self._copy(t, 2, _peer(self.Y), me, ins, outs, sems).wait_recv()
            self._copy(t, 4, _peer(self.Y), _peer(self.X), ins, outs, sems, half=1).start()
            self._copy(t, 6, _peer(self.Y), sibling, ins, outs, sems).start()

    def relay_far(self, ins, outs, sems):
        me, sibling = _peer(0), _peer(1)
        for t in range(len(ins)):
            self._copy(t, 3, _peer(self.FAR), me, ins, outs, sems, half=0).wait_recv()
            self._copy(t, 4, _peer(self.FAR), me, ins, outs, sems, half=1).wait_recv()
            self._copy(t, 7, _peer(self.FAR), sibling, ins, outs, sems).start()

    def end(self, ins, outs, sems):
        me = _peer(0)
        for t in range(len(ins)):
            self._copy(t, 0, _peer(1), me, ins, outs, sems).wait_recv()
            for k, code in ((5, self.X), (6, self.Y), (7, self.FAR)):
                self._copy(t, k, _peer(code ^ 1), me, ins, outs, sems).wait_recv()
            for k in range(self.COPIES):
                self._copy(t, k, me, me, ins, outs, sems, half=0 if k == 3 else 1 if k == 4 else None).wait_send()
            self._local(t, ins, outs, sems).wait()


class _GatherDirect:
    peers = tuple(range(1, NDEV))

    def __init__(self, arrays):
        n = len(arrays)
        self.operands = list(arrays)
        self.out_shape = [jax.ShapeDtypeStruct((NDEV,) + a.shape, a.dtype) for a in arrays]
        self.sems = [pltpu.SemaphoreType.DMA((7 * n,)), pltpu.SemaphoreType.DMA((7 * n,)), pltpu.SemaphoreType.DMA((n,))]
        self.stages = [self.begin, self.end]

    def begin(self, ins, outs, sems):
        mine = _flat(_peer(0))
        for t in range(len(ins)):
            pltpu.make_async_copy(ins[t], outs[t].at[mine], sems[2].at[t]).start()
            for k in range(1, NDEV):
                _remote(ins[t], outs[t].at[mine], sems[0].at[7 * t + k - 1], sems[1].at[7 * t + k - 1], _peer(k)).start()

    def end(self, ins, outs, sems):
        mine = _flat(_peer(0))
        for t in range(len(ins)):
            for k in range(1, NDEV):
                cp = _remote(ins[t], outs[t].at[_flat(_peer(k))], sems[0].at[7 * t + k - 1], sems[1].at[7 * t + k - 1], _peer(k))
                cp.wait_recv()
                cp.wait_send()
            pltpu.make_async_copy(ins[t], outs[t].at[mine], sems[2].at[t]).wait()


class _ChipScatter:
    peers = (2, 4, 6)
    across = (4, 2, 6)

    def __init__(self, sums):
        n = len(sums) * len(self.across)
        self.operands = list(sums)
        self.out_shape = [jax.ShapeDtypeStruct((len(self.across),) + a.shape[1:], a.dtype) for a in sums]
        self.sems = [pltpu.SemaphoreType.DMA((n,)), pltpu.SemaphoreType.DMA((n,))]
        self.stages = [self.begin, self.end]

    def _copies(self, ins, outs, sems):
        copies = []
        for t in range(len(ins)):
            for slot, k in enumerate(self.across):
                at = len(copies)
                copies.append(
                    _remote(ins[t].at[_chip(_peer(k))], outs[t].at[slot], sems[0].at[at], sems[1].at[at], _peer(k))
                )
        return copies

    def begin(self, ins, outs, sems):
        for cp in self._copies(ins, outs, sems):
            cp.start()

    def end(self, ins, outs, sems):
        for cp in self._copies(ins, outs, sems):
            cp.wait_recv()
            cp.wait_send()


class _CastRows:
    peers = ()

    def __init__(self, arrays):
        n = len(arrays)
        self.operands = list(arrays)
        self.out_shape = [jax.ShapeDtypeStruct(a.shape, BF16) for a in arrays]
        self.sems = [pltpu.SemaphoreType.DMA((n,)), pltpu.SemaphoreType.DMA((n,))]
        self.sems += [pltpu.VMEM(a.shape, F32) for a in arrays] + [pltpu.VMEM(a.shape, BF16) for a in arrays]
        self.stages = [self.begin, self.convert, self.end]

    def _moves(self, t, ins, outs, scratch):
        n = len(ins)
        load = pltpu.make_async_copy(ins[t], scratch[2 + t], scratch[0].at[t])
        store = pltpu.make_async_copy(scratch[2 + n + t], outs[t], scratch[1].at[t])
        return load, store

    def begin(self, ins, outs, scratch):
        for t in range(len(ins)):
            self._moves(t, ins, outs, scratch)[0].start()

    def convert(self, ins, outs, scratch):
        n = len(ins)
        for t in range(n):
            load, store = self._moves(t, ins, outs, scratch)
            load.wait()
            scratch[2 + n + t][...] = scratch[2 + t][...].astype(BF16)
            store.start()

    def end(self, ins, outs, scratch):
        for t in range(len(ins)):
            self._moves(t, ins, outs, scratch)[1].wait()


def _split_refs(refs, counts):
    out, at = [], 0
    for n in counts:
        out.append(refs[at:at + n])
        at += n
    return out


BARRIER_IDS = {(2, 4, 6): 0, (1, 2, 4): 1, (1,): 2, (1, 2, 4, 6): 3, tuple(range(1, NDEV)): 4}


def _peers_of(carries, own=()):
    peers = tuple(sorted(set(own).union(*[c.peers for c in carries])))
    return (peers, BARRIER_IDS[peers]) if peers in BARRIER_IDS else (None, None)


def _announce(peers):
    barrier = pltpu.get_barrier_semaphore()
    for k in peers:
        pl.semaphore_signal(barrier, inc=1, device_id=_peer(k), device_id_type=pl.DeviceIdType.MESH)


def _await(peers):
    pl.semaphore_wait(pltpu.get_barrier_semaphore(), len(peers))


def _handshake(peers):
    _announce(peers)
    _await(peers)


def _comm_call(carries, name):
    nin = [len(c.operands) for c in carries]
    nout = [len(c.out_shape) for c in carries]
    nsem = [len(c.sems) for c in carries]
    peers, collective_id = _peers_of(carries)

    def body(*refs):
        if peers:
            _handshake(peers)
        ins, outs, sems = _split_refs(refs, (sum(nin), sum(nout), sum(nsem)))
        parts = list(zip(carries, _split_refs(ins, nin), _split_refs(outs, nout), _split_refs(sems, nsem)))
        for depth in range(max(len(c.stages) for c in carries)):
            for c, i, o, s in parts:
                if depth < len(c.stages) - 1:
                    c.stages[depth](i, o, s)
        for c, i, o, s in parts:
            c.stages[-1](i, o, s)

    res = _call(
        body,
        name=name,
        out_shape=[sh for c in carries for sh in c.out_shape],
        in_specs=[ANY] * sum(nin),
        out_specs=[ANY] * sum(nout),
        scratch_shapes=[sm for c in carries for sm in c.sems],
        compiler_params=_params(has_side_effects=True, collective_id=collective_id),
    )(*[a for c in carries for a in c.operands])
    return _split_refs(list(res), nout)


def _grid_call(body, carries, *, name, grid, in_specs, out_specs, out_shape, scratch_shapes, args, own_peers=(),
               work_first=False, prelude=None):
    ni, no, ns = len(in_specs), len(out_specs), len(scratch_shapes)
    nin = [len(c.operands) for c in carries]
    nout = [len(c.out_shape) for c in carries]
    nsem = [len(c.sems) for c in carries]
    steps = int(np.prod(grid))
    peers, collective_id = _peers_of(carries, own_peers)

    def when_of(stage, count):
        first, last = (5 * steps) // 8 - 1, steps - 2
        return max(0, last if count <= 3 else first + (last - first) * (stage - 1) // (count - 3))

    def wrapped(*refs):
        ins, cins, outs, couts, scr, csems = _split_refs(refs, (ni, sum(nin), no, sum(nout), ns, sum(nsem)))
        if not carries and not peers and not prelude:
            return body(*ins, *outs, *scr)
        parts = list(zip(carries, _split_refs(cins, nin), _split_refs(couts, nout), _split_refs(csems, nsem)))
        step = pl.program_id(0)
        for axis in range(1, len(grid)):
            step = step * grid[axis] + pl.program_id(axis)

        def first_stage():
            if peers:
                _await(peers)
            for c, i, o, s in parts:
                c.stages[0](i, o, s)

        if peers:
            pl.when(step == 0)(functools.partial(_announce, peers))
        if prelude:
            pl.when(step == 0)(functools.partial(prelude, *ins, *outs, *scr))
        if not work_first:
            pl.when(step == 0)(first_stage)

        body(*ins, *outs, *scr)

        if work_first:
            pl.when(step == 0)(first_stage)

        for c, i, o, s in parts:
            for stage in range(1, len(c.stages) - 1):
                pl.when(step == when_of(stage, len(c.stages)))(functools.partial(c.stages[stage], i, o, s))

        @pl.when(step == steps - 1)
        def _():
            for c, i, o, s in parts:
                c.stages[-1](i, o, s)

    res = _call(
        wrapped,
        name=name,
        grid=tuple(grid),
        in_specs=list(in_specs) + [ANY] * sum(nin),
        out_specs=list(out_specs) + [ANY] * sum(nout),
        out_shape=list(out_shape) + [sh for c in carries for sh in c.out_shape],
        scratch_shapes=list(scratch_shapes) + [sm for c in carries for sm in c.sems],
        compiler_params=_params(dimension_semantics=("arbitrary",) * len(grid), collective_id=collective_id),
    )(*args, *[a for c in carries for a in c.operands])
    res = list(res)
    return res[:no], _split_refs(res[no:], nout)


def _chunks(width):
    return [(at, min(FFN_CHUNK, width - at)) for at in range(0, width, FFN_CHUNK)]


def _ffn_fwd_loss(x, gain, weights, final_gain, target):
    s, d = x.shape
    ffn = weights[0].shape[0] * weights[0].shape[1]
    tm = min(512, s)

    def body(x_ref, g_ref, b1, b3, b2, gf_ref, t_ref, dh_ref, a_ref, b_ref, hm_ref, dgf_ref, loss_ref, w1s, w3s, w2s, sems):
        @pl.when(pl.program_id(0) == 0)
        def _():
            for cp in _load_weights(((b1, w1s), (b3, w3s), (b2, w2s)), sems):
                cp.wait()
            dgf_ref[...] = jnp.zeros_like(dgf_ref)
            loss_ref[...] = jnp.zeros_like(loss_ref)

        xv = x_ref[...]
        r = lax.rsqrt(jnp.mean(xv * xv, axis=-1, keepdims=True) + EPS)
        n = (xv * r * g_ref[...]).astype(BF16)
        acc = jnp.zeros((tm, d), F32)
        for at, width in _chunks(ffn):
            cols = slice(at, at + width)
            a = _dot(n, w1s[cols, :], NT)
            b = _dot(n, w3s[cols, :], NT)
            a_ref[:, cols] = a.astype(BF16)
            b_ref[:, cols] = b.astype(BF16)
            hm = (a * _sigmoid(a) * b).astype(BF16)
            hm_ref[:, cols] = hm
            acc = acc + _dot(hm, w2s[cols, :], NN)
        h = xv + 0.5 * acc
        rf = lax.rsqrt(jnp.mean(h * h, axis=-1, keepdims=True) + EPS)
        nh = h * rf
        gf = gf_ref[...]
        err = nh * gf - t_ref[...]
        loss_ref[...] += jnp.sum(err * err, axis=0, keepdims=True) * (0.5 / d)
        dy = err * (1.0 / d)
        dgf_ref[...] += jnp.sum(dy * nh, axis=0, keepdims=True)
        dn = dy * gf
        dh_ref[...] = rf * (dn - nh * jnp.mean(dn * nh, axis=-1, keepdims=True))

    tile = pl.BlockSpec((tm, d), lambda i: (i, 0))
    row = pl.BlockSpec((1, d), lambda i: (0, 0))
    wide = pl.BlockSpec((tm, ffn), lambda i: (i, 0))
    return _call(
        body,
        name="ffn_fwd_loss",
        grid=(s // tm,),
        in_specs=[tile, row, ANY, ANY, ANY, row, tile],
        out_specs=[tile, wide, wide, wide, row, row],
        out_shape=[jax.ShapeDtypeStruct((s, d), F32)] + [jax.ShapeDtypeStruct((s, ffn), BF16)] * 3
        + [jax.ShapeDtypeStruct((1, d), F32)] * 2,
        scratch_shapes=[pltpu.VMEM((ffn, d), BF16)] * 3 + [pltpu.SemaphoreType.DMA((3 * NDEV,))],
        compiler_params=_seq(1),
    )(x, gain, *weights, final_gain, target)


def _ffn_up(x, gain, w1, w3, carries=()):
    s, d = x.shape
    ffn = w1.shape[0] * w1.shape[1]
    tm = min(512, s)

    def body(x_ref, g_ref, b1, b3, a_ref, b_ref, hm_ref, w1s, w3s, sems):
        @pl.when(pl.program_id(0) == 0)
        def _():
            for cp in _load_weights(((b1, w1s), (b3, w3s)), sems):
                cp.wait()

        xv = x_ref[...]
        r = lax.rsqrt(jnp.mean(xv * xv, axis=-1, keepdims=True) + EPS)
        n = (xv * r * g_ref[...]).astype(BF16)
        for at, width in _chunks(ffn):
            cols = slice(at, at + width)
            a = _dot(n, w1s[cols, :], NT)
            b = _dot(n, w3s[cols, :], NT)
            a_ref[:, cols] = a.astype(BF16)
            b_ref[:, cols] = b.astype(BF16)
            hm_ref[:, cols] = (a * _sigmoid(a) * b).astype(BF16)

    wide = pl.BlockSpec((tm, ffn), lambda i: (i, 0))
    return _grid_call(
        body,
        carries,
        name="ffn_up",
        grid=(s // tm,),
        in_specs=[pl.BlockSpec((tm, d), lambda i: (i, 0)), pl.BlockSpec((1, d), lambda i: (0, 0)), ANY, ANY],
        out_specs=[wide] * 3,
        out_shape=[jax.ShapeDtypeStruct((s, ffn), BF16)] * 3,
        scratch_shapes=[pltpu.VMEM((ffn, d), BF16)] * 2 + [pltpu.SemaphoreType.DMA((2 * NDEV,))],
        args=[x, gain, w1, w3],
    )


def _ffn_down(x, hm, w2, carries=()):
    s, d = x.shape
    ffn = w2.shape[0] * w2.shape[1]
    tm = min(512, s)

    def body(x_ref, hm_ref, b2, h_ref, w2s, sems):
        @pl.when(pl.program_id(0) == 0)
        def _():
            for cp in _load_weights(((b2, w2s),), sems):
                cp.wait()

        acc = jnp.zeros((tm, d), F32)
        for at, width in _chunks(ffn):
            cols = slice(at, at + width)
            acc = acc + _dot(hm_ref[:, cols], w2s[cols, :], NN)
        h_ref[...] = x_ref[...] + 0.5 * acc

    tile = pl.BlockSpec((tm, d), lambda i: (i, 0))
    return _grid_call(
        body,
        carries,
        name="ffn_down",
        grid=(s // tm,),
        in_specs=[tile, pl.BlockSpec((tm, ffn), lambda i: (i, 0)), ANY],
        out_specs=[tile],
        out_shape=[jax.ShapeDtypeStruct((s, d), F32)],
        scratch_shapes=[pltpu.VMEM((ffn, d), BF16), pltpu.SemaphoreType.DMA((NDEV,))],
        args=[x, hm, w2],
    )


def _ffn_bwd(dh, x, a, b, gain, weights, ffn, name, carries=()):
    s, d = x.shape
    tm = min(512, s)
    halves = 2
    fh = ffn // halves

    def body(dh_ref, x_ref, a_ref, b_ref, g_ref, b1, b3, b2, dx_ref, da_ref, db_ref, n_ref, dg_ref, w1s, w3s, w2s, sems):
        i, j = pl.program_id(0), pl.program_id(1)

        @pl.when((i == 0) & (j == 0))
        def _():
            for cp in _load_weights(((b1, w1s), (b3, w3s), (b2, w2s)), sems):
                cp.wait()
            dg_ref[...] = jnp.zeros_like(dg_ref)

        @pl.when(j == 0)
        def _():
            dx_ref[...] = jnp.zeros_like(dx_ref)

        dob = (0.5 * dh_ref[...]).astype(BF16)
        chunks = _chunks(fh)

        def dhm_of(k):
            at, width = chunks[k]
            return _dot(dob, w2s[pl.ds(pl.multiple_of(j * fh + at, GROUP), width), :], NT)

        ahead = dhm_of(0)
        for k, (at, width) in enumerate(chunks):
            cols = slice(at, at + width)
            dhm = ahead
            if k + 1 < len(chunks):
                ahead = dhm_of(k + 1)
            for top in range(0, tm, ROW_BAND):
                band = slice(top, top + ROW_BAND)
                av = a_ref[band, cols].astype(F32)
                bv = b_ref[band, cols].astype(F32)
                sg = _sigmoid(av)
                dv = dhm[band]
                da_ref[band, cols] = (dv * bv * (sg * (1.0 + av * (1.0 - sg)))).astype(BF16)
                db_ref[band, cols] = (dv * (av * sg)).astype(BF16)
        half = pl.ds(pl.multiple_of(j * fh, GROUP), fh)
        dx_ref[...] += _dot(da_ref[...], w1s[half, :], NN) + _dot(db_ref[...], w3s[half, :], NN)

        @pl.when(j == halves - 1)
        def _():
            xv = x_ref[...]
            g = g_ref[...]
            r = lax.rsqrt(jnp.mean(xv * xv, axis=-1, keepdims=True) + EPS)
            nh = xv * r
            n_ref[...] = (nh * g).astype(BF16)
            total = dx_ref[...]
            dg_ref[...] += jnp.sum(total * nh, axis=0, keepdims=True)
            dnh = total * g
            dx_ref[...] = dh_ref[...] + r * (dnh - nh * jnp.mean(dnh * nh, axis=-1, keepdims=True))

    tile = pl.BlockSpec((tm, d), lambda i, j: (i, 0))
    row = pl.BlockSpec((1, d), lambda i, j: (0, 0))
    wide = pl.BlockSpec((tm, fh), lambda i, j: (i, j))
    return _grid_call(
        body,
        carries,
        name=name,
        grid=(s // tm, halves),
        in_specs=[tile, tile, wide, wide, row, ANY, ANY, ANY],
        out_specs=[tile, wide, wide, tile, row],
        out_shape=[
            jax.ShapeDtypeStruct((s, d), F32),
            jax.ShapeDtypeStruct((s, ffn), BF16),
            jax.ShapeDtypeStruct((s, ffn), BF16),
            jax.ShapeDtypeStruct((s, d), BF16),
            jax.ShapeDtypeStruct((1, d), F32),
        ],
        scratch_shapes=[pltpu.VMEM((ffn, d), BF16)] * 3 + [pltpu.SemaphoreType.DMA((3 * NDEV,))],
        args=[dh, x, a, b, gain] + list(weights),
    )


SWAP_PIECES = 1
RING = 3


def _wgrad(lhs, rhs, scale, name, carries=(), scatter=False):
    s, m = lhs.shape
    n = rhs.shape[1]
    rs = m // NDEV
    tk = min(1024, s)
    steps = s // tk
    pieces = [(j, at, size) for j in range(2) for at, size in _pieces(rs, SWAP_PIECES)]

    def ring_of(refs):
        return refs[:2] + refs[3 + scatter:6 + scatter]

    def loads_of(t, l_hbm, r_hbm, l_buf, r_buf, load_sems):
        if isinstance(t, int):
            slot, at, half = t % RING, (t % steps) * tk, (t // steps) * (m // 2)
        else:
            slot = lax.rem(t, RING)
            at = pl.multiple_of(lax.rem(t, steps) * tk, tk)
            half = pl.multiple_of(lax.div(t, steps) * (m // 2), GROUP)
        return (
            pltpu.make_async_copy(l_hbm.at[pl.ds(at, tk), pl.ds(half, m // 2)], l_buf.at[slot], load_sems.at[0, slot]),
            pltpu.make_async_copy(r_hbm.at[pl.ds(at, tk), :], r_buf.at[slot], load_sems.at[1, slot]),
        )

    def first_loads(*refs):
        for t in range(min(RING - 1, 2 * steps)):
            for cp in loads_of(t, *ring_of(refs)):
                cp.start()

    def body(*refs):
        o_ref = refs[2]
        far_ref = refs[3] if scatter else None
        l_buf, r_buf, _, acc, mine, theirs, send_sems, recv_sems, *far_sems = refs[3 + scatter:]
        h, k = pl.program_id(0), pl.program_id(1)
        step = h * steps + k
        loads = lambda t: loads_of(t, *ring_of(refs))

        @pl.when(step + RING - 1 < 2 * steps)
        def _():
            for cp in loads(step + RING - 1):
                cp.start()

        for cp in loads(step):
            cp.wait()
        slot = lax.rem(step, RING)

        @pl.when(k == 0)
        def _():
            acc[...] = _dot(l_buf[slot], r_buf[slot].astype(BF16), TN)

        @pl.when(k > 0)
        def _():
            acc[...] += _dot(l_buf[slot], r_buf[slot].astype(BF16), TN)

        def exchange(half):
            c = lax.axis_index("c")
            return [
                _remote(mine.at[half, 1 - c, j, pl.ds(at, size), :], theirs.at[half, j, pl.ds(at, size), :],
                        send_sems.at[half * len(pieces) + q], recv_sems.at[half * len(pieces) + q], _peer(1))
                for q, (j, at, size) in enumerate(pieces)
            ]

        def settle(half):
            for cp in exchange(half):
                cp.wait_recv()
            both = mine[half, lax.axis_index("c")].astype(F32) + theirs[half].astype(F32)
            o_ref[2 * half:2 * half + 2] = both.astype(BF16)
            for cp in exchange(half):
                cp.wait_send()

        for half in range(2):
            @pl.when((h == half) & (k == steps - 1))
            def _():
                def stage(core):
                    for j in range(2):
                        rows = pl.ds(pl.multiple_of((2 * j + core) * rs, 8), rs)
                        mine[half, core, j] = (acc[rows, :] * scale).astype(BF16)

                c = lax.axis_index("c")
                stage(1 - c)
                for cp in exchange(half):
                    cp.start()
                stage(c)
                if half == 1:
                    settle(0)
                    settle(1)
                    if scatter:
                        onward = [
                            _remote(o_ref.at[_chip(_peer(code))], far_ref.at[slot], far_sems[0].at[slot],
                                    far_sems[1].at[slot], _peer(code))
                            for slot, code in enumerate(_ChipScatter.across)
                        ]
                        for cp in onward:
                            cp.start()
                        for cp in onward:
                            cp.wait_recv()
                            cp.wait_send()

    far = len(_ChipScatter.across)
    outs, carried = _grid_call(
        body,
        carries,
        name=name,
        grid=(2, steps),
        in_specs=[ANY, ANY],
        out_specs=[pl.BlockSpec((NCHIP, rs, n), lambda h, k: (0, 0, 0))] + [ANY] * scatter,
        out_shape=[jax.ShapeDtypeStruct((NCHIP, rs, n), BF16)] + [jax.ShapeDtypeStruct((far, rs, n), BF16)] * scatter,
        scratch_shapes=[
            pltpu.VMEM((RING, tk, m // 2), lhs.dtype), pltpu.VMEM((RING, tk, n), rhs.dtype),
            pltpu.SemaphoreType.DMA((2, RING)),
            pltpu.VMEM((m // 2, n), F32), pltpu.VMEM((2, 2, 2, rs, n), BF16), pltpu.VMEM((2, 2, rs, n), BF16),
            pltpu.SemaphoreType.DMA((2 * len(pieces),)), pltpu.SemaphoreType.DMA((2 * len(pieces),)),
        ] + [pltpu.SemaphoreType.DMA((far,))] * (2 * scatter),
        args=[lhs, rhs],
        own_peers=(1,) + (_ChipScatter.peers if scatter else ()),
        prelude=first_loads,
    )
    return (tuple(outs) if scatter else outs[0]), carried


def _mix_constants(s):
    c = GROUP
    lg = np.log1p(-np.exp2(-5.0 - np.arange(RET_HEADS, dtype=np.float32))).astype(np.float32)
    pos = np.arange(c, dtype=np.float32)
    rel = pos[:, None] - pos[None, :]
    decay = np.where(rel[None] >= 0, np.exp(lg[:, None, None] * np.maximum(rel, 0.0)[None]), 0.0).astype(np.float32)
    ktail = np.exp(lg[:, None] * (c - 1 - pos)[None, :]).astype(np.float32)
    qhead = np.exp(lg[:, None] * (pos + 1.0)[None, :]).astype(np.float32)
    chunk_decay = [float(v) for v in np.exp(lg * np.float32(c)).astype(np.float32)]
    ones = np.ones((1, 1, c), np.float32)
    inv_freq = (1.0 / (np.float32(ROPE_BASE) ** (np.arange(0, c, 2, dtype=np.float32) / np.float32(c)))).astype(np.float32)
    ang = (np.arange(s, dtype=np.float32)[:, None] * inv_freq[None, :]).astype(np.float32)
    cos, sin = np.cos(ang).astype(np.float32), np.sin(ang).astype(np.float32)
    return dict(
        decay=jnp.asarray(decay),
        ktail=jnp.asarray(ktail[:, :, None] * ones),
        qhead=jnp.asarray(qhead[:, :, None] * ones),
        chunk_decay=chunk_decay,
        cos=jnp.asarray(np.concatenate([cos, cos], axis=-1)),
        sin=jnp.asarray(np.concatenate([-sin, sin], axis=-1)),
    )


def _rope(t, cos, sin):
    return t * cos + pltpu.roll(t, GROUP // 2, axis=1) * sin


def _rope_bwd(dt, cos, sin):
    return dt * cos + pltpu.roll(dt * sin, GROUP // 2, axis=1)


def _window_sums(ext, w, forward):
    rows = ext.shape[0]
    acc, k = ext, 1
    while k < w:
        acc = acc + pltpu.roll(acc, k if forward else rows - k, axis=0)
        k *= 2
    return acc


def _pool_counts(tile, tm, w):
    t = lax.broadcasted_iota(jnp.int32, (tm, 1), 0) + tile * tm
    return jnp.minimum(t + 1, w).astype(F32)


def _mix_fwd(h1, gain, weights, pool_w, pool_scale, ret_gain, consts, carries=()):
    s, d = h1.shape
    pwid = N_POOL_GROUPS * GROUP
    rwid = RET_HEADS * GROUP
    inw = pwid + 4 * rwid
    tm = min(256, s)
    nck = tm // GROUP
    cd = consts["chunk_decay"]

    def body(h_ref, g_ref, bin_, bout, pw_ref, ps_ref, rg_ref, cos_ref, sin_ref, dec_ref, kt_ref, qh_ref,
             h2_ref, proj_ref, o_ref, rs_ref, wins, wouts, state, carry, mbuf, sems):
        i = pl.program_id(0)

        @pl.when(i == 0)
        def _():
            for cp in _load_weights(((bin_, wins), (bout, wouts)), sems):
                cp.wait()
            state[...] = jnp.zeros_like(state)
            carry[...] = jnp.zeros_like(carry)

        hv = h_ref[...]
        r = lax.rsqrt(jnp.mean(hv * hv, axis=-1, keepdims=True) + EPS)
        u = (hv * r * g_ref[...]).astype(BF16)
        proj_ref[...] = _dot(u, wins[...], NT)

        ext = jnp.concatenate([carry[...], proj_ref[:, 0:pwid]], axis=0)
        carry[...] = proj_ref[tm - MAX_WINDOW:tm, 0:pwid]
        for gi, w in enumerate(POOL_WINDOWS):
            cols = slice(gi * GROUP, (gi + 1) * GROUP)
            xg = ext[:, cols]
            ws = _window_sums(xg, w, True)[MAX_WINDOW:, :]
            pooled = ws / _pool_counts(i, tm, w) - xg[MAX_WINDOW:, :]
            z = _dot(pooled.astype(BF16), pw_ref[gi].astype(BF16), NN)
            mbuf[:, cols] = (z * ps_ref[:, cols]).astype(BF16)

        cos, sin = cos_ref[...], sin_ref[...]
        for h in range(RET_HEADS):
            cq = slice(pwid + h * GROUP, pwid + (h + 1) * GROUP)
            ck = slice(pwid + rwid + h * GROUP, pwid + rwid + (h + 1) * GROUP)
            cv = slice(pwid + 2 * rwid + h * GROUP, pwid + 2 * rwid + (h + 1) * GROUP)
            cg = slice(pwid + 3 * rwid + h * GROUP, pwid + 3 * rwid + (h + 1) * GROUP)
            ch = slice(h * GROUP, (h + 1) * GROUP)
            qr = _rope(proj_ref[:, cq], cos, sin)
            kr = _rope(proj_ref[:, ck], cos, sin) * (GROUP ** -0.5)
            vb = proj_ref[:, cv].astype(BF16)
            for n in range(nck):
                rows = slice(n * GROUP, (n + 1) * GROUP)
                qc, kc, vc = qr[rows], kr[rows], vb[rows]
                rb = state[h]
                rs_ref[n, h] = rb
                p = (_dot(qc.astype(BF16), kc.astype(BF16), NT) * dec_ref[h]).astype(BF16)
                o = _dot(p, vc, NN) + _dot((qc * qh_ref[h]).astype(BF16), rb.astype(BF16), NN)
                state[h] = cd[h] * rb + _dot((kc * kt_ref[h]).astype(BF16), vc, TN)
                o_ref[rows, ch] = o
                on = o * lax.rsqrt(jnp.mean(o * o, axis=-1, keepdims=True) + EPS)
                gv = proj_ref[rows, cg]
                mbuf[rows, pwid + h * GROUP:pwid + (h + 1) * GROUP] = (
                    gv * _sigmoid(gv) * (on * rg_ref[:, ch])
                ).astype(BF16)
        h2_ref[...] = hv + _dot(mbuf[...], wouts[...], NN)

    tile = pl.BlockSpec((tm, d), lambda i: (i, 0))
    full = lambda shape: pl.BlockSpec(shape, lambda i: (0,) * len(shape))
    return _grid_call(
        body,
        carries,
        name="mix_fwd",
        work_first=True,
        grid=(s // tm,),
        in_specs=[
            tile, full((1, d)), ANY, ANY,
            full((N_POOL_GROUPS, GROUP, GROUP)), full((1, pwid)), full((1, rwid)),
            pl.BlockSpec((tm, GROUP), lambda i: (i, 0)), pl.BlockSpec((tm, GROUP), lambda i: (i, 0)),
            full((RET_HEADS, GROUP, GROUP)), full((RET_HEADS, GROUP, GROUP)), full((RET_HEADS, GROUP, GROUP)),
        ],
        out_specs=[
            tile,
            pl.BlockSpec((tm, inw), lambda i: (i, 0)),
            pl.BlockSpec((tm, rwid), lambda i: (i, 0)),
            pl.BlockSpec((nck, RET_HEADS, GROUP, GROUP), lambda i: (i, 0, 0, 0)),
        ],
        out_shape=[
            jax.ShapeDtypeStruct((s, d), F32),
            jax.ShapeDtypeStruct((s, inw), F32),
            jax.ShapeDtypeStruct((s, rwid), F32),
            jax.ShapeDtypeStruct((s // GROUP, RET_HEADS, GROUP, GROUP), F32),
        ],
        scratch_shapes=[
            pltpu.VMEM((inw, d), BF16), pltpu.VMEM((d, d), BF16),
            pltpu.VMEM((RET_HEADS, GROUP, GROUP), F32), pltpu.VMEM((MAX_WINDOW, pwid), F32),
            pltpu.VMEM((tm, d), BF16), pltpu.SemaphoreType.DMA((2 * NDEV,)),
        ],
        args=[h1, gain, weights[0], weights[1], pool_w, pool_scale, ret_gain,
              consts["cos"], consts["sin"], consts["decay"], consts["ktail"], consts["qhead"]],
    )


def _mix_bwd(dh2, h1, proj, o_saved, rsave, gain, weights, pool_w, pool_scale, ret_gain, consts, carries=()):
    s, d = h1.shape
    pwid = N_POOL_GROUPS * GROUP
    rwid = RET_HEADS * GROUP
    inw = pwid + 4 * rwid
    tm = min(256, s)
    nck = tm // GROUP
    nt = s // tm
    cd = consts["chunk_decay"]
    halo_per_tile = tm // MAX_WINDOW

    def body(dh2_ref, h_ref, proj_ref, halo_ref, o_ref, rs_ref, g_ref, bin_, bout, pw_ref, ps_ref, rg_ref,
             cos_ref, sin_ref, dec_ref, kt_ref, qh_ref,
             dh1_ref, dproj_ref, u_ref, m_ref, dpw_ref, dps_ref, drg_ref, dg_ref,
             wins, wouts, dstate, carry, dm, dpj, sems):
        i = pl.program_id(0)
        tile = nt - 1 - i

        @pl.when(i == 0)
        def _():
            for cp in _load_weights(((bin_, wins), (bout, wouts)), sems):
                cp.wait()
            dstate[...] = jnp.zeros_like(dstate)
            carry[...] = jnp.zeros_like(carry)
            for ref in (dpw_ref, dps_ref, drg_ref, dg_ref):
                ref[...] = jnp.zeros_like(ref)

        dh2v = dh2_ref[...]
        dm[...] = _dot(dh2v.astype(BF16), wouts[...], NT)
        hv = h_ref[...]
        g = g_ref[...]
        r = lax.rsqrt(jnp.mean(hv * hv, axis=-1, keepdims=True) + EPS)
        uh = hv * r
        u_ref[...] = (uh * g).astype(BF16)

        halo = jnp.where(tile == 0, 0.0, halo_ref[...])
        ext = jnp.concatenate([halo, proj_ref[:, 0:pwid]], axis=0)
        next_dpn = carry[...]
        for gi, w in enumerate(POOL_WINDOWS):
            cols = slice(gi * GROUP, (gi + 1) * GROUP)
            xg = ext[:, cols]
            cnt = _pool_counts(tile, tm, w)
            pooled = (_window_sums(xg, w, True)[MAX_WINDOW:, :] / cnt - xg[MAX_WINDOW:, :]).astype(BF16)
            pwb = pw_ref[gi].astype(BF16)
            z = _dot(pooled, pwb, NN)
            scale = ps_ref[:, cols]
            m_ref[:, cols] = (z * scale).astype(BF16)
            da = dm[:, cols]
            dps_ref[:, cols] += jnp.sum(da * z, axis=0, keepdims=True)
            dz = (da * scale).astype(BF16)
            dpw_ref[gi] += _dot(pooled, dz, TN)
            dpl = _dot(dz, pwb, NT)
            dpn = dpl / cnt
            ext2 = jnp.concatenate([dpn, next_dpn[:, cols]], axis=0)
            dpj[:, cols] = (_window_sums(ext2, w, False)[0:tm, :] - dpl).astype(BF16)
            carry[:, cols] = dpn[0:MAX_WINDOW, :]

        cos, sin = cos_ref[...], sin_ref[...]
        for h in range(RET_HEADS):
            cq = slice(pwid + h * GROUP, pwid + (h + 1) * GROUP)
            ck = slice(pwid + rwid + h * GROUP, pwid + rwid + (h + 1) * GROUP)
            cv = slice(pwid + 2 * rwid + h * GROUP, pwid + 2 * rwid + (h + 1) * GROUP)
            cg = slice(pwid + 3 * rwid + h * GROUP, pwid + 3 * rwid + (h + 1) * GROUP)
            ch = slice(h * GROUP, (h + 1) * GROUP)
            qr = _rope(proj_ref[:, cq], cos, sin)
            kr = _rope(proj_ref[:, ck], cos, sin) * (GROUP ** -0.5)
            vb = proj_ref[:, cv].astype(BF16)
            gv = proj_ref[:, cg]
            ov = o_ref[:, ch]
            ro = lax.rsqrt(jnp.mean(ov * ov, axis=-1, keepdims=True) + EPS)
            on = ov * ro
            rg = rg_ref[:, ch]
            db = dm[:, pwid + h * GROUP:pwid + (h + 1) * GROUP]
            sg = _sigmoid(gv)
            sl = gv * sg
            m_ref[:, pwid + h * GROUP:pwid + (h + 1) * GROUP] = (sl * (on * rg)).astype(BF16)
            dpj[:, cg] = (db * (on * rg) * (sg * (1.0 + gv * (1.0 - sg)))).astype(BF16)
            drg_ref[:, ch] += jnp.sum(db * sl * on, axis=0, keepdims=True)
            don = db * sl * rg
            do = (ro * (don - on * jnp.mean(don * on, axis=-1, keepdims=True))).astype(BF16)
            for n in reversed(range(nck)):
                rows = slice(n * GROUP, (n + 1) * GROUP)
                qc, kc, vc, dob = qr[rows], kr[rows], vb[rows], do[rows]
                qcb, kcb = qc.astype(BF16), kc.astype(BF16)
                qh = (qc * qh_ref[h]).astype(BF16)
                kt = (kc * kt_ref[h]).astype(BF16)
                rn = rs_ref[n, h].astype(BF16)
                dnext = dstate[h]
                dnb = dnext.astype(BF16)
                dec = dec_ref[h]
                p = (_dot(qcb, kcb, NT) * dec).astype(BF16)
                ds = (_dot(dob, vc, NT) * dec).astype(BF16)
                dv = _dot(p, dob, TN) + _dot(kt, dnb, NN)
                dq = _dot(ds, kcb, NN) + _dot(dob, rn, NT) * qh_ref[h]
                dk = _dot(ds, qcb, TN) + _dot(vc, dnb, NT) * kt_ref[h]
                dstate[h] = cd[h] * dnext + _dot(qh, dob, TN)
                dpj[rows, cq] = _rope_bwd(dq, cos[rows], sin[rows]).astype(BF16)
                dpj[rows, ck] = _rope_bwd(dk * (GROUP ** -0.5), cos[rows], sin[rows]).astype(BF16)
                dpj[rows, cv] = dv.astype(BF16)

        dproj_ref[...] = dpj[...]
        du = _dot(dpj[...], wins[...], NN)
        dg_ref[...] += jnp.sum(du * uh, axis=0, keepdims=True)
        dn = du * g
        dh1_ref[...] = dh2v + r * (dn - uh * jnp.mean(dn * uh, axis=-1, keepdims=True))

    rev = lambda i: (nt - 1 - i, 0)
    tile = pl.BlockSpec((tm, d), rev)
    full = lambda shape: pl.BlockSpec(shape, lambda i: (0,) * len(shape))
    return _grid_call(
        body,
        carries,
        name="mix_bwd",
        work_first=True,
        grid=(nt,),
        in_specs=[
            tile, tile,
            pl.BlockSpec((tm, inw), rev),
            pl.BlockSpec((MAX_WINDOW, pwid), lambda i: (jnp.maximum((nt - 1 - i) * halo_per_tile - 1, 0), 0)),
            pl.BlockSpec((tm, rwid), rev),
            pl.BlockSpec((nck, RET_HEADS, GROUP, GROUP), lambda i: (nt - 1 - i, 0, 0, 0)),
            full((1, d)), ANY, ANY,
            full((N_POOL_GROUPS, GROUP, GROUP)), full((1, pwid)), full((1, rwid)),
            pl.BlockSpec((tm, GROUP), rev), pl.BlockSpec((tm, GROUP), rev),
            full((RET_HEADS, GROUP, GROUP)), full((RET_HEADS, GROUP, GROUP)), full((RET_HEADS, GROUP, GROUP)),
        ],
        out_specs=[
            tile, pl.BlockSpec((tm, inw), rev), tile, tile,
            full((N_POOL_GROUPS, GROUP, GROUP)), full((1, pwid)), full((1, rwid)), full((1, d)),
        ],
        out_shape=[
            jax.ShapeDtypeStruct((s, d), F32),
            jax.ShapeDtypeStruct((s, inw), BF16),
            jax.ShapeDtypeStruct((s, d), BF16),
            jax.ShapeDtypeStruct((s, d), BF16),
            jax.ShapeDtypeStruct((N_POOL_GROUPS, GROUP, GROUP), F32),
            jax.ShapeDtypeStruct((1, pwid), F32),
            jax.ShapeDtypeStruct((1, rwid), F32),
            jax.ShapeDtypeStruct((1, d), F32),
        ],
        scratch_shapes=[
            pltpu.VMEM((inw, d), BF16), pltpu.VMEM((d, d), BF16),
            pltpu.VMEM((RET_HEADS, GROUP, GROUP), F32), pltpu.VMEM((MAX_WINDOW, pwid), F32),
            pltpu.VMEM((tm, d), F32), pltpu.VMEM((tm, inw), BF16), pltpu.SemaphoreType.DMA((2 * NDEV,)),
        ],
        args=[dh2, h1, proj, proj, o_saved, rsave, gain, weights[0], weights[1], pool_w, pool_scale, ret_gain,
              consts["cos"], consts["sin"], consts["decay"], consts["ktail"], consts["qhead"]],
    )


def _adam(w, g, m, v):
    m = ADAM_B1 * m + (1.0 - ADAM_B1) * g
    v = ADAM_B2 * v + (1.0 - ADAM_B2) * jnp.square(g)
    m_hat = m / (1.0 - ADAM_B1 ** ADAM_STEP)
    v_hat = v / (1.0 - ADAM_B2 ** ADAM_STEP)
    delta = -ADAM_LR * (m_hat / (jnp.sqrt(v_hat) + ADAM_EPS) + ADAM_WD * w)
    return delta, m, v


def _adamw_big(my_chip, w, sums, parts, m, v, name):
    rows, d = w.shape
    tr = _row_tile(rows, 176)

    def body(chip_ref, w_ref, own_ref, p_ref, m_ref, v_ref, g_ref, d_ref, nm_ref, nv_ref):
        g = own_ref[0].astype(F32)
        for q in range(NCHIP - 1):
            g = g + p_ref[q].astype(F32)
        g_ref[...] = g
        d_ref[...], nm_ref[...], nv_ref[...] = _adam(w_ref[...], g, m_ref[...], v_ref[...])

    spec = pl.BlockSpec((tr, d), lambda i, chip: (i, 0))
    return _call(
        body,
        name=name,
        grid_spec=pltpu.PrefetchScalarGridSpec(
            num_scalar_prefetch=1,
            grid=(rows // tr,),
            in_specs=[
                spec,
                pl.BlockSpec((1, tr, d), lambda i, chip: (chip[0], i, 0)),
                pl.BlockSpec((NCHIP - 1, tr, d), lambda i, chip: (0, i, 0)),
                spec,
                spec,
            ],
            out_specs=[spec] * 4,
        ),
        out_shape=[jax.ShapeDtypeStruct((rows, d), F32)] * 4,
        compiler_params=_seq(1),
    )(my_chip, w, sums, parts, m, v)


def _adamw_small(stats_all, pw_all, ws, ms, vs, pwid):
    nsmall = len(ws)

    def body(*refs):
        st_ref, pwa_ref = refs[0], refs[1]
        w_refs = refs[2:2 + nsmall]
        m_refs = refs[2 + nsmall:2 + 2 * nsmall]
        v_refs = refs[2 + 2 * nsmall:2 + 3 * nsmall]
        outs = refs[2 + 3 * nsmall:]
        st = st_ref[0]
        pwg = pwa_ref[0]
        for q in range(1, NDEV):
            st = st + st_ref[q]
            pwg = pwg + pwa_ref[q]
        grads = [st[0:1, :], st[1:2, :], st[2:3, :], st[3:4, :], st[4:5, 0:pwid], st[4:5, pwid:2 * pwid], pwg]
        outs[0][...] = jnp.zeros((1, GROUP), F32) + jnp.sum(st[5:6, :])
        for j in range(nsmall):
            delta, nm, nv = _adam(w_refs[j][...], grads[j], m_refs[j][...], v_refs[j][...])
            outs[1 + 4 * j][...] = grads[j]
            outs[2 + 4 * j][...] = delta
            outs[3 + 4 * j][...] = nm
            outs[4 + 4 * j][...] = nv

    out_shape = [jax.ShapeDtypeStruct((1, GROUP), F32)]
    for w in ws:
        out_shape += [jax.ShapeDtypeStruct(w.shape, F32)] * 4
    return _call(body, name="adamw_small", out_shape=out_shape, compiler_params=_params())(
        stats_all, pw_all, *ws, *ms, *vs
    )


def kernel(x, ffn1_norm, ffn1_w1, ffn1_w3, ffn1_w2, mix_norm, w_in, pool_w, pool_scale, ret_norm, w_out, ffn2_norm, ffn2_w1, ffn2_w3, ffn2_w2, final_norm, loss_target, m_ffn1_norm, m_ffn1_w1, m_ffn1_w3, m_ffn1_w2, m_mix_norm, m_w_in, m_pool_w, m_pool_scale, m_ret_norm, m_w_out, m_ffn2_norm, m_ffn2_w1, m_ffn2_w3, m_ffn2_w2, m_final_norm, v_ffn1_norm, v_ffn1_w1, v_ffn1_w3, v_ffn1_w2, v_mix_norm, v_w_in, v_pool_w, v_pool_scale, v_ret_norm, v_w_out, v_ffn2_norm, v_ffn2_w1, v_ffn2_w3, v_ffn2_w2, v_final_norm):
    s, d = x.shape[1], x.shape[2]
    ffn = ffn1_w1.shape[2] * NDEV
    pwid = pool_scale.shape[1]
    xs, tgt = x[0], loss_target[0]
    consts = _mix_constants(s)
    pw3 = pool_w[0]
    fnorm = final_norm.reshape(1, d)

    rows_of = lambda w, transposed: w[0].T if transposed else w[0]
    send_f1 = [rows_of(w, t).astype(BF16) for w, t in ((ffn1_w1, True), (ffn1_w3, True), (ffn1_w2, False))]
    later = [rows_of(w, t) for w, t in ((w_in, True), (w_out, False), (ffn2_w1, True), (ffn2_w3, True), (ffn2_w2, False))]

    sent_later, w13_f1 = _comm_call([_CastRows(later), _Gather(send_f1[:2])], "gather_ffn1")
    send_mix, send_f2 = sent_later[:2], sent_later[2:]
    (a1, b1, hm1), ((w2_f1, w_in_all),) = _ffn_up(xs, ffn1_norm, *w13_f1, carries=[_Gather(send_f1[2:] + send_mix[:1])])
    w_f1 = w13_f1 + [w2_f1]
    (h1,), ((w_out_all, w1_f2),) = _ffn_down(xs, hm1, w2_f1, carries=[_Gather(send_mix[1:] + send_f2[:1])])
    w_mix = [w_in_all, w_out_all]
    (h2, proj, o_saved, rsave), (rest,) = _mix_fwd(
        h1, mix_norm, w_mix, pw3, pool_scale, ret_norm, consts, carries=[_Gather(send_f2[1:])]
    )
    w_f2 = [w1_f2] + rest
    dh3, a2, b2, hm2, dgf, loss_cols = _ffn_fwd_loss(h2, ffn2_norm, w_f2, fnorm, tgt)

    (dh2, da2, db2, n2, dg2), _ = _ffn_bwd(dh3, h2, a2, b2, ffn2_norm, w_f2, ffn, "ffn2_bwd")
    sum_f2w1, _ = _wgrad(da2, n2, 1.0, "ffn2_w1_grad")
    sum_f2w3, _ = _wgrad(db2, n2, 1.0, "ffn2_w3_grad")
    sum_f2w2, ((parts_f2w1,),) = _wgrad(hm2, dh3, 0.5, "ffn2_w2_grad", carries=[_ChipScatter([sum_f2w1])])

    (dh1, dproj, u, mm, dpw, dps, drg, dgm), ((parts_f2w3, parts_f2w2),) = _mix_bwd(
        dh2, h1, proj, o_saved, rsave, mix_norm, w_mix, pw3, pool_scale, ret_norm, consts,
        carries=[_ChipScatter([sum_f2w3, sum_f2w2])],
    )
    (dx, da1, db1, n1, dg1), _ = _ffn_bwd(dh1, xs, a1, b1, ffn1_norm, w_f1, ffn, "ffn1_bwd")
    stats = jnp.concatenate(
        [dg1, dgm, dg2, dgf, jnp.concatenate([dps, drg], axis=1), loss_cols, jnp.zeros((2, d), F32)], axis=0
    )
    small = _GatherDirect([stats, dpw.reshape(N_POOL_GROUPS * GROUP, GROUP)])
    sum_f1w2, ((stats_all, pw_all),) = _wgrad(hm1, dh1, 0.5, "ffn1_w2_grad", carries=[small])
    sum_f1w1, ((parts_f1w2,),) = _wgrad(da1, n1, 1.0, "ffn1_w1_grad", carries=[_ChipScatter([sum_f1w2])])
    sum_f1w3, ((parts_f1w1,),) = _wgrad(db1, n1, 1.0, "ffn1_w3_grad", carries=[_ChipScatter([sum_f1w1])])
    sum_in, ((parts_f1w3,),) = _wgrad(dproj, u, 1.0, "w_in_grad", carries=[_ChipScatter([sum_f1w3])])
    (sum_out, parts_out), ((parts_in,),) = _wgrad(
        mm, dh2, 1.0, "w_out_grad", carries=[_ChipScatter([sum_in])], scatter=True
    )

    big = (
        (ffn1_w1, m_ffn1_w1, v_ffn1_w1, sum_f1w1, parts_f1w1, True),
        (ffn1_w3, m_ffn1_w3, v_ffn1_w3, sum_f1w3, parts_f1w3, True),
        (ffn1_w2, m_ffn1_w2, v_ffn1_w2, sum_f1w2, parts_f1w2, False),
        (w_in, m_w_in, v_w_in, sum_in, parts_in, True),
        (w_out, m_w_out, v_w_out, sum_out, parts_out, False),
        (ffn2_w1, m_ffn2_w1, v_ffn2_w1, sum_f2w1, parts_f2w1, True),
        (ffn2_w3, m_ffn2_w3, v_ffn2_w3, sum_f2w3, parts_f2w3, True),
        (ffn2_w2, m_ffn2_w2, v_ffn2_w2, sum_f2w2, parts_f2w2, False),
    )
    my_chip = jnp.reshape(_chip(_peer(0)), (1,)).astype(jnp.int32)
    big_out = []
    for j, (w, m, v, sums, parts, t) in enumerate(big):
        view = (lambda a: a[0].T) if t else (lambda a: a[0])
        back = (lambda a: a.T[None]) if t else (lambda a: a[None])
        big_out.append([back(a) for a in _adamw_big(my_chip, view(w), sums, parts, view(m), view(v), "adamw_%d" % j)])

    small_w = (ffn1_norm, mix_norm, ffn2_norm, fnorm, pool_scale, ret_norm, pw3.reshape(-1, GROUP))
    small_m = (m_ffn1_norm, m_mix_norm, m_ffn2_norm, m_final_norm.reshape(1, d), m_pool_scale, m_ret_norm, m_pool_w.reshape(-1, GROUP))
    small_v = (v_ffn1_norm, v_mix_norm, v_ffn2_norm, v_final_norm.reshape(1, d), v_pool_scale, v_ret_norm, v_pool_w.reshape(-1, GROUP))
    res = _adamw_small(stats_all, pw_all, small_w, small_m, small_v, pwid)
    loss = res[0][0, 0]
    small_out = [list(res[1 + 4 * j:5 + 4 * j]) for j in range(len(small_w))]
    small_out[3] = [a.reshape(d) for a in small_out[3]]
    small_out[6] = [a.reshape(pool_w.shape) for a in small_out[6]]

    order = [small_out[0], big_out[0], big_out[1], big_out[2], small_out[1], big_out[3], small_out[6], small_out[4],
             small_out[5], big_out[4], small_out[2], big_out[5], big_out[6], big_out[7], small_out[3]]
    result = [loss, dx[None]]
    for kind in range(4):
        result += [t[kind] for t in order]
    return tuple(result)
```
